```python
import jax, jax.numpy as jnp
from jax import lax
import numpy as np

D_MODEL = 1024
BATCH = 16
SEQ = 2048
DEPTH = 1

D_MIX = D_MODEL
D_RG = D_MIX // 2
D_SC = D_MIX - D_RG
RG_HEADS = 8
RG_HEAD_DIM = D_RG // RG_HEADS
RG_CONV = 4
RG_C = 8.0
SC_GROUPS = 8
SC_CONV = 3
D_IN_PROJ = 2 * D_RG + 3 * D_SC
N_GROUPS = 4
EXPERTS_PER_GROUP = 8
N_EXPERTS = N_GROUPS * EXPERTS_PER_GROUP
TOP_K = 2
D_EXPERT = D_MODEL // 4
LN_EPS = 1e-5
DEEPNORM_ALPHA = (2 * DEPTH) ** 0.25
DEEPNORM_BETA = (8 * DEPTH) ** -0.25

kernel_name = "hymba_rglru_shortconv_hmoe_deepnorm"


def layer_norm(x, g, b):
    xf = x.astype(jnp.float32)
    mu = jnp.mean(xf, axis=-1, keepdims=True)
    var = jnp.mean(jnp.square(xf - mu), axis=-1, keepdims=True)
    y = (xf - mu) * lax.rsqrt(var + LN_EPS) * g.astype(jnp.float32) + b.astype(jnp.float32)
    return y.astype(x.dtype)


def causal_depthwise_conv(x, w):
    k, c = w.shape
    rhs = w.astype(x.dtype)[:, None, :]
    return lax.conv_general_dilated(
        x, rhs, window_strides=(1,), padding=[(k - 1, 0)],
        dimension_numbers=("NWC", "WIO", "NWC"), feature_group_count=c)


def _linear_recurrence_combine(e1, e2):
    a1, b1 = e1
    a2, b2 = e2
    return (a1 * a2, a2 * b1 + b2)


def rg_lru(xc, gate_a_w, gate_a_b, gate_x_w, gate_x_b, lam):
    bsz, s, _ = xc.shape
    xh = xc.reshape(bsz, s, RG_HEADS, RG_HEAD_DIM)
    r = jax.nn.sigmoid(jnp.einsum("bshi,hij->bshj", xh, gate_a_w).reshape(bsz, s, D_RG) + gate_a_b)
    i = jax.nn.sigmoid(jnp.einsum("bshi,hij->bshj", xh, gate_x_w).reshape(bsz, s, D_RG) + gate_x_b)
    r = r.astype(jnp.float32)
    i = i.astype(jnp.float32)
    log_a = -RG_C * r * jax.nn.softplus(-lam.astype(jnp.float32))
    a = jnp.exp(log_a)
    mult = jnp.sqrt(-jnp.expm1(2.0 * log_a))
    u = mult * (i * xc.astype(jnp.float32))
    _, h = lax.associative_scan(_linear_recurrence_combine, (a, u), axis=1)
    return h.astype(xc.dtype)


def hybrid_token_mixer(x, w_in, rg_conv_w, rg_conv_b, rg_gate_a_w, rg_gate_a_b,
                       rg_gate_x_w, rg_gate_x_b, rg_lambda, sc_conv_w, w_out):
    proj = jnp.einsum("bsd,de->bse", x, w_in)
    splits = [D_RG, 2 * D_RG, 2 * D_RG + D_SC, 2 * D_RG + 2 * D_SC]
    rg_x, rg_gate, sc_b, sc_c, sc_v = jnp.split(proj, splits, axis=-1)
    xc = causal_depthwise_conv(rg_x, rg_conv_w) + rg_conv_b
    h = rg_lru(xc, rg_gate_a_w, rg_gate_a_b, rg_gate_x_w, rg_gate_x_b, rg_lambda)
    y_rg = h * jax.nn.gelu(rg_gate)
    y_sc = sc_b * causal_depthwise_conv(sc_c * sc_v, sc_conv_w)
    y = jnp.concatenate([y_rg, y_sc], axis=-1)
    return jnp.einsum("bse,ed->bsd", y, w_out)


def hierarchical_moe(x, router_group_w, router_group_b, router_expert_w, router_expert_b,
                     exp_w_gate, exp_w_up, exp_w_down):
    bsz, s, d = x.shape
    t = x.reshape(bsz * s, d)
    g_logits = (t @ router_group_w + router_group_b).astype(jnp.float32)
    g_prob = jax.nn.softmax(g_logits, axis=-1)
    g_top_p, g_idx = lax.top_k(g_prob, 1)
    e_logits = (t @ router_expert_w + router_expert_b).astype(jnp.float32)
    e_logits = e_logits.reshape(-1, N_GROUPS, EXPERTS_PER_GROUP)
    e_sel = jnp.take_along_axis(e_logits, g_idx[:, :, None], axis=1)[:, 0]
    e_top_v, e_top_i = lax.top_k(e_sel, TOP_K)
    weights = jax.nn.softmax(e_top_v, axis=-1) * g_top_p
    global_idx = g_idx * EXPERTS_PER_GROUP + e_top_i
    combine = jnp.sum(jax.nn.one_hot(global_idx, N_EXPERTS, dtype=jnp.float32) * weights[..., None], axis=1)
    combine = combine.astype(x.dtype)
    hg = jnp.einsum("td,edf->tef", t, exp_w_gate)
    hu = jnp.einsum("td,edf->tef", t, exp_w_up)
    hid = jax.nn.silu(hg) * hu * combine[:, :, None]
    out = jnp.einsum("tef,efd->td", hid, exp_w_down)
    return out.reshape(bsz, s, d)


def setup_inputs(seed: int = 0) -> dict:
    key = jax.random.key(seed)
    ks = jax.random.split(key, 24)
    f32 = jnp.float32
    nrm = lambda k, shape, scale: jax.random.normal(k, shape, f32) * scale
    x = jax.random.normal(ks[0], (BATCH, SEQ, D_MODEL), f32)
    w_in = nrm(ks[1], (D_MODEL, D_IN_PROJ), D_MODEL ** -0.5)
    rg_conv_w = nrm(ks[2], (RG_CONV, D_RG), RG_CONV ** -0.5)
    rg_conv_b = nrm(ks[3], (D_RG,), 0.02)
    rg_gate_a_w = nrm(ks[4], (RG_HEADS, RG_HEAD_DIM, RG_HEAD_DIM), RG_HEAD_DIM ** -0.5)
    rg_gate_a_b = nrm(ks[5], (D_RG,), 0.02)
    rg_gate_x_w = nrm(ks[6], (RG_HEADS, RG_HEAD_DIM, RG_HEAD_DIM), RG_HEAD_DIM ** -0.5)
    rg_gate_x_b = nrm(ks[7], (D_RG,), 0.02)
    u = jax.random.uniform(ks[8], (D_RG,), f32, 0.9, 0.999)
    sq = u ** (1.0 / RG_C)
    rg_lambda = jnp.log(sq) - jnp.log1p(-sq)
    sc_conv_w = nrm(ks[9], (SC_CONV, D_SC), SC_CONV ** -0.5)
    w_out = nrm(ks[10], (D_MIX, D_MODEL), D_MIX ** -0.5) * DEEPNORM_BETA
    ln1_g = 1.0 + nrm(ks[11], (D_MODEL,), 0.02)
    ln1_b = nrm(ks[12], (D_MODEL,), 0.02)
    router_group_w = nrm(ks[13], (D_MODEL, N_GROUPS), D_MODEL ** -0.5)
    router_group_b = nrm(ks[14], (N_GROUPS,), 0.01)
    router_expert_w = nrm(ks[15], (D_MODEL, N_EXPERTS), D_MODEL ** -0.5)
    router_expert_b = nrm(ks[16], (N_EXPERTS,), 0.01)
    exp_w_gate = nrm(ks[17], (N_EXPERTS, D_MODEL, D_EXPERT), D_MODEL ** -0.5)
    exp_w_up = nrm(ks[18], (N_EXPERTS, D_MODEL, D_EXPERT), D_MODEL ** -0.5)
    exp_w_down = nrm(ks[19], (N_EXPERTS, D_EXPERT, D_MODEL), D_EXPERT ** -0.5) * DEEPNORM_BETA
    ln2_g = 1.0 + nrm(ks[20], (D_MODEL,), 0.02)
    ln2_b = nrm(ks[21], (D_MODEL,), 0.02)
    return {"x": x, "w_in": w_in, "rg_conv_w": rg_conv_w, "rg_conv_b": rg_conv_b,
            "rg_gate_a_w": rg_gate_a_w, "rg_gate_a_b": rg_gate_a_b,
            "rg_gate_x_w": rg_gate_x_w, "rg_gate_x_b": rg_gate_x_b,
            "rg_lambda": rg_lambda, "sc_conv_w": sc_conv_w, "w_out": w_out,
            "ln1_g": ln1_g, "ln1_b": ln1_b,
            "router_group_w": router_group_w, "router_group_b": router_group_b,
            "router_expert_w": router_expert_w, "router_expert_b": router_expert_b,
            "exp_w_gate": exp_w_gate, "exp_w_up": exp_w_up, "exp_w_down": exp_w_down,
            "ln2_g": ln2_g, "ln2_b": ln2_b}


def reference(x, w_in, rg_conv_w, rg_conv_b, rg_gate_a_w, rg_gate_a_b, rg_gate_x_w,
              rg_gate_x_b, rg_lambda, sc_conv_w, w_out, ln1_g, ln1_b,
              router_group_w, router_group_b, router_expert_w, router_expert_b,
              exp_w_gate, exp_w_up, exp_w_down, ln2_g, ln2_b):
    h = x
    for _ in range(DEPTH):
        mix = hybrid_token_mixer(h, w_in, rg_conv_w, rg_conv_b, rg_gate_a_w, rg_gate_a_b,
                                 rg_gate_x_w, rg_gate_x_b, rg_lambda, sc_conv_w, w_out)
        h = layer_norm(DEEPNORM_ALPHA * h + mix, ln1_g, ln1_b)
        ffn = hierarchical_moe(h, router_group_w, router_group_b, router_expert_w,
                               router_expert_b, exp_w_gate, exp_w_up, exp_w_down)
        h = layer_norm(DEEPNORM_ALPHA * h + ffn, ln2_g, ln2_b)
    return h
```

```python
import functools

import jax
import jax.numpy as jnp
import numpy as np
from jax import lax
from jax.experimental import pallas as pl
from jax.experimental.pallas import tpu as pltpu

D_MODEL = 1024
BATCH = 16
SEQ = 2048
D_RG = 512
D_SC = 512
RG_HEADS = 8
RG_HEAD_DIM = D_RG // RG_HEADS
RG_CONV = 4
RG_C = 8.0
SC_CONV = 3
D_IN_PROJ = 2 * D_RG + 3 * D_SC
N_GROUPS = 4
EXPERTS_PER_GROUP = 8
N_EXPERTS = N_GROUPS * EXPERTS_PER_GROUP
D_EXPERT = D_MODEL // 4
LN_EPS = 1e-5
DEEPNORM_ALPHA = 2.0 ** 0.25

N_TOKENS = BATCH * SEQ
LANES = 128
ROUTER_LANES = LANES

TS = 64
M1 = TS * BATCH

TM = 128
SUB = 4
N_PAIRS = EXPERTS_PER_GROUP * (EXPERTS_PER_GROUP - 1) // 2
N_CLASSES = N_GROUPS * N_PAIRS
N_TILES = -(-(N_TOKENS // TM + N_CLASSES + N_GROUPS * (SUB - 1)) // SUB) * SUB
N_STEPS3 = N_TILES // SUB
P_ROWS = N_TILES * TM

VMEM_LIMIT = 56 * 1024 * 1024


def _layer_norm(z, g, b):
    mu = jnp.mean(z, axis=-1, keepdims=True)
    zc = z - mu
    var = jnp.mean(zc * zc, axis=-1, keepdims=True)
    return zc * lax.rsqrt(var + LN_EPS) * g + b


def _mixer_kernel(x_ref, w_in_ref, cw_ref, cb_ref, wa_ref, ba_ref, wx_ref, bx_ref, lam_ref,
                  scw_ref, w_out_ref, g1_ref, b1_ref, wr_ref, br_ref,
                  h1_ref, logit_ref,
                  xt_ref, cbuf_ref, sbuf_ref, a_ref, u_ref, hstate_ref):
    c = pl.program_id(0)
    rg_pad = (RG_CONV - 1) * BATCH
    sc_pad = (SC_CONV - 1) * BATCH

    @pl.when(c == 0)
    def _():
        cbuf_ref[0:rg_pad, :] = jnp.zeros((rg_pad, D_RG), jnp.float32)
        sbuf_ref[0:sc_pad, :] = jnp.zeros((sc_pad, D_SC), jnp.float32)
        hstate_ref[...] = jnp.zeros_like(hstate_ref)

    for t in range(TS):
        xt_ref[t * BATCH:(t + 1) * BATCH, :] = x_ref[:, t, :]

    xb = xt_ref[...].astype(jnp.bfloat16)

    def proj(lo, width):
        return jnp.dot(xb, w_in_ref[:, lo:lo + width], preferred_element_type=jnp.float32)

    cbuf_ref[rg_pad:rg_pad + M1, :] = proj(0, D_RG)
    xc = cb_ref[...] + cw_ref[0:1, :] * cbuf_ref[0:M1, :]
    for k in range(1, RG_CONV):
        xc = xc + cw_ref[k:k + 1, :] * cbuf_ref[k * BATCH:k * BATCH + M1, :]
    cbuf_ref[0:rg_pad, :] = cbuf_ref[M1:M1 + rg_pad, :]

    xcb = xc.astype(jnp.bfloat16)
    r = jax.nn.sigmoid(jnp.dot(xcb, wa_ref[...], preferred_element_type=jnp.float32) + ba_ref[...])
    i = jax.nn.sigmoid(jnp.dot(xcb, wx_ref[...], preferred_element_type=jnp.float32) + bx_ref[...])
    log_a = (-RG_C) * r * jax.nn.softplus(-lam_ref[...])
    a_ref[...] = jnp.exp(log_a)
    th = jnp.tanh(log_a)
    u_ref[...] = jnp.sqrt(-2.0 * th / (1.0 - th)) * (i * xc)

    def scan_body(t, h):
        rows = pl.ds(pl.multiple_of(t * BATCH, BATCH), BATCH)
        h = a_ref[rows, :] * h + u_ref[rows, :]
        u_ref[rows, :] = h
        return h

    hstate_ref[...] = lax.fori_loop(0, TS, scan_body, hstate_ref[...], unroll=8)

    y_rg = u_ref[...] * jax.nn.gelu(proj(D_RG, D_RG))

    sc_b = proj(2 * D_RG, D_SC)
    sbuf_ref[sc_pad:sc_pad + M1, :] = proj(2 * D_RG + D_SC, D_SC) * proj(2 * D_RG + 2 * D_SC, D_SC)
    conv = scw_ref[0:1, :] * sbuf_ref[0:M1, :]
    for k in range(1, SC_CONV):
        conv = conv + scw_ref[k:k + 1, :] * sbuf_ref[k * BATCH:k * BATCH + M1, :]
    sbuf_ref[0:sc_pad, :] = sbuf_ref[M1:M1 + sc_pad, :]
    y_sc = sc_b * conv

    mix = jnp.dot(y_rg.astype(jnp.bfloat16), w_out_ref[0:D_RG, :], preferred_element_type=jnp.float32)
    mix = mix + jnp.dot(y_sc.astype(jnp.bfloat16), w_out_ref[D_RG:, :], preferred_element_type=jnp.float32)

    h1 = _layer_norm(DEEPNORM_ALPHA * xt_ref[...] + mix, g1_ref[...], b1_ref[...])
    h1_ref[...] = h1
    logit_ref[...] = jnp.dot(h1, wr_ref[...], preferred_element_type=jnp.float32,
                             precision=lax.Precision.HIGHEST) + br_ref[...]


def _const_spec(shape):
    return pl.BlockSpec(shape, lambda c: (0,) * len(shape))


def _mixer_call(x, w_in, cw, cb, wa, ba, wx, bx, lam, scw, w_out, g1, b1, wr, br):
    n_chunks = SEQ // TS
    in_specs = [
        pl.BlockSpec((BATCH, TS, D_MODEL), lambda c: (0, c, 0)),
        _const_spec((D_MODEL, D_IN_PROJ)),
        _const_spec((RG_CONV, D_RG)), _const_spec((1, D_RG)),
        _const_spec((D_RG, D_RG)), _const_spec((1, D_RG)),
        _const_spec((D_RG, D_RG)), _const_spec((1, D_RG)),
        _const_spec((1, D_RG)),
        _const_spec((SC_CONV, D_SC)),
        _const_spec((D_MODEL, D_MODEL)),
        _const_spec((1, D_MODEL)), _const_spec((1, D_MODEL)),
        _const_spec((D_MODEL, ROUTER_LANES)), _const_spec((1, ROUTER_LANES)),
    ]
    out_specs = [
        pl.BlockSpec((M1, D_MODEL), lambda c: (c, 0)),
        pl.BlockSpec((M1, ROUTER_LANES), lambda c: (c, 0)),
    ]
    return pl.pallas_call(
        _mixer_kernel,
        grid=(n_chunks,),
        in_specs=in_specs,
        out_specs=out_specs,
        out_shape=[
            jax.ShapeDtypeStruct((N_TOKENS, D_MODEL), jnp.float32),
            jax.ShapeDtypeStruct((N_TOKENS, ROUTER_LANES), jnp.float32),
        ],
        scratch_shapes=[
            pltpu.VMEM((M1, D_MODEL), jnp.float32),
            pltpu.VMEM((M1 + (RG_CONV - 1) * BATCH, D_RG), jnp.float32),
            pltpu.VMEM((M1 + (SC_CONV - 1) * BATCH, D_SC), jnp.float32),
            pltpu.VMEM((M1, D_RG), jnp.float32),
            pltpu.VMEM((M1, D_RG), jnp.float32),
            pltpu.VMEM((BATCH, D_RG), jnp.float32),
        ],
        compiler_params=pltpu.CompilerParams(
            dimension_semantics=("arbitrary",), vmem_limit_bytes=VMEM_LIMIT),
        name="mixer_ln_router",
    )(x, w_in, cw, cb, wa, ba, wx, bx, lam, scw, w_out, g1, b1, wr, br)


def _expert_kernel(grp_ref, ea_ref, eb_ref, valid_ref,
                   xs_ref, wt_ref, wg_ref, wu_ref, wd_ref, g2_ref, b2_ref, out_ref):
    s = pl.program_id(0)

    def sub_tile(j, carry):
        tile = s * SUB + j
        rows = pl.ds(pl.multiple_of(j * TM, TM), TM)

        @pl.when(valid_ref[tile] == 0)
        def _():
            out_ref[rows, :] = jnp.zeros((TM, D_MODEL), jnp.float32)

        @pl.when(valid_ref[tile] != 0)
        def _():
            x = xs_ref[rows, :]
            xb = x.astype(jnp.bfloat16)
            wt = wt_ref[rows, :]
            y = jnp.zeros((TM, D_MODEL), jnp.float32)
            for e_ref, col in ((ea_ref, 0), (eb_ref, 1)):
                e = e_ref[tile]
                hg = jnp.dot(xb, wg_ref[e], preferred_element_type=jnp.float32)
                hu = jnp.dot(xb, wu_ref[e], preferred_element_type=jnp.float32)
                hid = jax.nn.silu(hg) * hu * wt[:, col:col + 1]
                y = y + jnp.dot(hid.astype(jnp.bfloat16), wd_ref[e],
                                preferred_element_type=jnp.float32)
            out_ref[rows, :] = _layer_norm(DEEPNORM_ALPHA * x + y, g2_ref[...], b2_ref[...])

        return carry

    lax.fori_loop(0, SUB, sub_tile, 0)


def _expert_call(step_group, tile_ea, tile_eb, tile_valid, xs, wts, wg, wu, wd, g2, b2):
    rows = SUB * TM
    grid_spec = pltpu.PrefetchScalarGridSpec(
        num_scalar_prefetch=4,
        grid=(N_STEPS3,),
        in_specs=[
            pl.BlockSpec((rows, D_MODEL), lambda s, grp, ea, eb, va: (s, 0)),
            pl.BlockSpec((rows, LANES), lambda s, grp, ea, eb, va: (s, 0)),
            pl.BlockSpec((None, EXPERTS_PER_GROUP, D_MODEL, D_EXPERT),
                         lambda s, grp, ea, eb, va: (grp[s], 0, 0, 0)),
            pl.BlockSpec((None, EXPERTS_PER_GROUP, D_MODEL, D_EXPERT),
                         lambda s, grp, ea, eb, va: (grp[s], 0, 0, 0)),
            pl.BlockSpec((None, EXPERTS_PER_GROUP, D_EXPERT, D_MODEL),
                         lambda s, grp, ea, eb, va: (grp[s], 0, 0, 0)),
            pl.BlockSpec((1, D_MODEL), lambda s, grp, ea, eb, va: (0, 0)),
            pl.BlockSpec((1, D_MODEL), lambda s, grp, ea, eb, va: (0, 0)),
        ],
        out_specs=pl.BlockSpec((rows, D_MODEL), lambda s, grp, ea, eb, va: (s, 0)),
    )
    return pl.pallas_call(
        _expert_kernel,
        grid_spec=grid_spec,
        out_shape=jax.ShapeDtypeStruct((P_ROWS, D_MODEL), jnp.float32),
        compiler_params=pltpu.CompilerParams(
            dimension_semantics=("arbitrary",), vmem_limit_bytes=VMEM_LIMIT),
        name="experts_ln",
    )(step_group, tile_ea, tile_eb, tile_valid, xs, wts, wg, wu, wd, g2, b2)


def _block_diag(w):
    h, d, _ = w.shape
    eye = jnp.eye(h, dtype=w.dtype)
    return (eye[:, None, :, None] * w[:, :, None, :]).reshape(h * d, h * d)


def _pair_tables():
    pair_of = np.zeros((EXPERTS_PER_GROUP, EXPERTS_PER_GROUP), np.int32)
    lo = np.zeros((N_PAIRS,), np.int32)
    hi = np.zeros((N_PAIRS,), np.int32)
    p = 0
    for a in range(EXPERTS_PER_GROUP):
        for b in range(a + 1, EXPERTS_PER_GROUP):
            pair_of[a, b] = pair_of[b, a] = p
            lo[p], hi[p] = a, b
            p += 1
    return pair_of, lo, hi


def _route(logits):
    g_logits = logits[:, :N_GROUPS]
    e_logits = logits[:, N_GROUPS:N_GROUPS + N_EXPERTS].reshape(-1, N_GROUPS, EXPERTS_PER_GROUP)
    g_prob = jax.nn.softmax(g_logits, axis=-1)
    g_top_p, g_idx = lax.top_k(g_prob, 1)
    e_sel = jnp.take_along_axis(e_logits, g_idx[:, :, None], axis=1)[:, 0]
    e_top_v, e_top_i = lax.top_k(e_sel, 2)
    w = jax.nn.softmax(e_top_v, axis=-1) * g_top_p
    first_is_lo = e_top_i[:, 0] < e_top_i[:, 1]
    w_lo = jnp.where(first_is_lo, w[:, 0], w[:, 1])
    w_hi = jnp.where(first_is_lo, w[:, 1], w[:, 0])
    pair_of, _, _ = _pair_tables()
    cls = g_idx[:, 0] * N_PAIRS + jnp.asarray(pair_of)[e_top_i[:, 0], e_top_i[:, 1]]
    return cls.astype(jnp.int32), w_lo, w_hi


def _dispatch_plan(cls):
    _, pair_lo, pair_hi = _pair_tables()
    onehot = (cls[:, None] == jnp.arange(N_CLASSES, dtype=jnp.int32)[None, :]).astype(jnp.int32)
    csum = jnp.cumsum(onehot, axis=0)
    counts = csum[-1]
    rank = jnp.take_along_axis(csum, cls[:, None], axis=1)[:, 0] - 1
    tiles_c = (counts + TM - 1) // TM
    tiles_g = tiles_c.reshape(N_GROUPS, N_PAIRS).sum(axis=1)
    tiles_g_pad = (tiles_g + SUB - 1) // SUB * SUB
    g_start = jnp.cumsum(tiles_g_pad) - tiles_g_pad
    tc = tiles_c.reshape(N_GROUPS, N_PAIRS)
    c_start = (g_start[:, None] + jnp.cumsum(tc, axis=1) - tc).reshape(N_CLASSES)
    pos = c_start[cls] * TM + rank

    tile_ids = jnp.arange(N_TILES, dtype=jnp.int32)
    c_end = c_start + tiles_c
    owner = jnp.sum((tile_ids[:, None] >= c_end[None, :]).astype(jnp.int32), axis=1)
    owner = jnp.minimum(owner, N_CLASSES - 1)
    valid = ((tile_ids >= c_start[owner]) & (tile_ids < c_end[owner])).astype(jnp.int32)
    pair = owner % N_PAIRS
    tile_ea = jnp.asarray(pair_lo)[pair]
    tile_eb = jnp.asarray(pair_hi)[pair]
    g_end = g_start + tiles_g_pad
    step_first = jnp.arange(N_STEPS3, dtype=jnp.int32) * SUB
    step_group = jnp.sum((step_first[:, None] >= g_end[None, :]).astype(jnp.int32), axis=1)
    step_group = jnp.minimum(step_group, N_GROUPS - 1)
    return pos.astype(jnp.int32), tile_ea, tile_eb, valid, step_group.astype(jnp.int32)


def kernel(x, w_in, rg_conv_w, rg_conv_b, rg_gate_a_w, rg_gate_a_b, rg_gate_x_w, rg_gate_x_b,
           rg_lambda, sc_conv_w, w_out, ln1_g, ln1_b, router_group_w, router_group_b,
           router_expert_w, router_expert_b, exp_w_gate, exp_w_up, exp_w_down, ln2_g, ln2_b):
    bf16 = jnp.bfloat16
    row = lambda v: v.reshape(1, -1)
    n_router = N_GROUPS + N_EXPERTS
    wr = jnp.zeros((D_MODEL, ROUTER_LANES), jnp.float32)
    wr = wr.at[:, :N_GROUPS].set(router_group_w).at[:, N_GROUPS:n_router].set(router_expert_w)
    br = jnp.zeros((1, ROUTER_LANES), jnp.float32)
    br = br.at[0, :N_GROUPS].set(router_group_b).at[0, N_GROUPS:n_router].set(router_expert_b)

    h1, logits = _mixer_call(
        x, w_in.astype(bf16), rg_conv_w, row(rg_conv_b),
        _block_diag(rg_gate_a_w).astype(bf16), row(rg_gate_a_b),
        _block_diag(rg_gate_x_w).astype(bf16), row(rg_gate_x_b),
        row(rg_lambda), sc_conv_w, w_out.astype(bf16), row(ln1_g), row(ln1_b), wr, br)

    cls, w_lo, w_hi = _route(logits)
    pos, tile_ea, tile_eb, tile_valid, step_group = _dispatch_plan(cls)

    src = jnp.zeros((P_ROWS,), jnp.int32).at[pos].set(jnp.arange(N_TOKENS, dtype=jnp.int32))
    xs = jnp.take(h1, src, axis=0)
    wts = jnp.zeros((N_TOKENS, LANES), jnp.float32).at[:, 0].set(w_lo).at[:, 1].set(w_hi)
    wts = jnp.take(wts, src, axis=0)

    grp_shape = (N_GROUPS, EXPERTS_PER_GROUP)
    ys = _expert_call(
        step_group, tile_ea, tile_eb, tile_valid, xs, wts,
        exp_w_gate.astype(bf16).reshape(grp_shape + (D_MODEL, D_EXPERT)),
        exp_w_up.astype(bf16).reshape(grp_shape + (D_MODEL, D_EXPERT)),
        exp_w_down.astype(bf16).reshape(grp_shape + (D_EXPERT, D_MODEL)),
        row(ln2_g), row(ln2_b))

    out_rows = pos.reshape(SEQ, BATCH).T.reshape(-1)
    return jnp.take(ys, out_rows, axis=0).reshape(BATCH, SEQ, D_MODEL)
```

```python
import functools

import jax
import jax.numpy as jnp
import numpy as np
from jax import lax
from jax.experimental import pallas as pl
from jax.experimental.pallas import tpu as pltpu

D_MODEL = 1024
BATCH = 16
SEQ = 2048
D_RG = 512
D_SC = 512
RG_HEADS = 8
RG_HEAD_DIM = D_RG // RG_HEADS
RG_CONV = 4
RG_C = 8.0
SC_CONV = 3
D_IN_PROJ = 2 * D_RG + 3 * D_SC
N_GROUPS = 4
EXPERTS_PER_GROUP = 8
N_EXPERTS = N_GROUPS * EXPERTS_PER_GROUP
D_EXPERT = D_MODEL // 4
LN_EPS = 1e-5
DEEPNORM_ALPHA = 2.0 ** 0.25

N_TOKENS = BATCH * SEQ
LANES = 128
ROUTER_LANES = LANES
GROUP_ROWS = 8
ROUTER_ROWS = GROUP_ROWS + N_EXPERTS

TS = 64
M1 = TS * BATCH

ROUTE_BLOCK = 4096

TM = 128
SUB = 4
N_PAIRS = EXPERTS_PER_GROUP * (EXPERTS_PER_GROUP - 1) // 2
N_CLASSES = N_GROUPS * N_PAIRS
N_TILES = -(-(N_TOKENS // TM + N_CLASSES + N_GROUPS * (SUB - 1)) // SUB) * SUB
N_STEPS3 = N_TILES // SUB
P_ROWS = N_TILES * TM

VMEM_LIMIT = 56 * 1024 * 1024


def _layer_norm(z, g, b):
    mu = jnp.mean(z, axis=-1, keepdims=True)
    zc = z - mu
    var = jnp.mean(zc * zc, axis=-1, keepdims=True)
    return zc * lax.rsqrt(var + LN_EPS) * g + b


def _mixer_kernel(x_ref, w_in_ref, cw_ref, cb_ref, wa_ref, ba_ref, wx_ref, bx_ref, lam_ref,
                  scw_ref, w_out_ref, g1_ref, b1_ref, wr_ref, br_ref,
                  h1_ref, logit_ref,
                  xt_ref, cbuf_ref, sbuf_ref, a_ref, u_ref, hstate_ref):
    c = pl.program_id(0)
    rg_pad = (RG_CONV - 1) * BATCH
    sc_pad = (SC_CONV - 1) * BATCH

    @pl.when(c == 0)
    def _():
        cbuf_ref[0:rg_pad, :] = jnp.zeros((rg_pad, D_RG), jnp.float32)
        sbuf_ref[0:sc_pad, :] = jnp.zeros((sc_pad, D_SC), jnp.float32)
        hstate_ref[...] = jnp.zeros_like(hstate_ref)

    for t in range(TS):
        xt_ref[t * BATCH:(t + 1) * BATCH, :] = x_ref[:, t, :]

    xb = xt_ref[...].astype(jnp.bfloat16)

    def proj(lo, width):
        return jnp.dot(xb, w_in_ref[:, lo:lo + width], preferred_element_type=jnp.float32)

    cbuf_ref[rg_pad:rg_pad + M1, :] = proj(0, D_RG)
    xc = cb_ref[...] + cw_ref[0:1, :] * cbuf_ref[0:M1, :]
    for k in range(1, RG_CONV):
        xc = xc + cw_ref[k:k + 1, :] * cbuf_ref[k * BATCH:k * BATCH + M1, :]
    cbuf_ref[0:rg_pad, :] = cbuf_ref[M1:M1 + rg_pad, :]

    xcb = xc.astype(jnp.bfloat16)
    r = jax.nn.sigmoid(jnp.dot(xcb, wa_ref[...], preferred_element_type=jnp.float32) + ba_ref[...])
    i = jax.nn.sigmoid(jnp.dot(xcb, wx_ref[...], preferred_element_type=jnp.float32) + bx_ref[...])
    log_a = (-RG_C) * r * jax.nn.softplus(-lam_ref[...])
    a_ref[...] = jnp.exp(log_a)
    th = jnp.tanh(log_a)
    u_ref[...] = jnp.sqrt(-2.0 * th / (1.0 - th)) * (i * xc)

    def scan_body(t, h):
        rows = pl.ds(pl.multiple_of(t * BATCH, BATCH), BATCH)
        h = a_ref[rows, :] * h + u_ref[rows, :]
        u_ref[rows, :] = h
        return h

    hstate_ref[...] = lax.fori_loop(0, TS, scan_body, hstate_ref[...], unroll=8)

    y_rg = u_ref[...] * jax.nn.gelu(proj(D_RG, D_RG))

    sc_b = proj(2 * D_RG, D_SC)
    sbuf_ref[sc_pad:sc_pad + M1, :] = proj(2 * D_RG + D_SC, D_SC) * proj(2 * D_RG + 2 * D_SC, D_SC)
    conv = scw_ref[0:1, :] * sbuf_ref[0:M1, :]
    for k in range(1, SC_CONV):
        conv = conv + scw_ref[k:k + 1, :] * sbuf_ref[k * BATCH:k * BATCH + M1, :]
    sbuf_ref[0:sc_pad, :] = sbuf_ref[M1:M1 + sc_pad, :]
    y_sc = sc_b * conv

    mix = jnp.dot(y_rg.astype(jnp.bfloat16), w_out_ref[0:D_RG, :], preferred_element_type=jnp.float32)
    mix = mix + jnp.dot(y_sc.astype(jnp.bfloat16), w_out_ref[D_RG:, :], preferred_element_type=jnp.float32)

    h1 = _layer_norm(DEEPNORM_ALPHA * xt_ref[...] + mix, g1_ref[...], b1_ref[...])
    h1_ref[...] = h1
    h_hi = h1.astype(jnp.bfloat16)
    h_lo = (h1 - h_hi.astype(jnp.float32)).astype(jnp.bfloat16)
    both = jnp.dot(h_hi, wr_ref[...], preferred_element_type=jnp.float32)
    logits = both[:, :ROUTER_LANES] + both[:, ROUTER_LANES:] + br_ref[...]
    logits = logits + jnp.dot(h_lo, wr_ref[:, :ROUTER_LANES], preferred_element_type=jnp.float32)
    logit_ref[...] = logits.T[:ROUTER_ROWS, :]


def _const_spec(shape):
    return pl.BlockSpec(shape, lambda c: (0,) * len(shape))


def _mixer_call(x, w_in, cw, cb, wa, ba, wx, bx, lam, scw, w_out, g1, b1, wr, br):
    n_chunks = SEQ // TS
    in_specs = [
        pl.BlockSpec((BATCH, TS, D_MODEL), lambda c: (0, c, 0)),
        _const_spec((D_MODEL, D_IN_PROJ)),
        _const_spec((RG_CONV, D_RG)), _const_spec((1, D_RG)),
        _const_spec((D_RG, D_RG)), _const_spec((1, D_RG)),
        _const_spec((D_RG, D_RG)), _const_spec((1, D_RG)),
        _const_spec((1, D_RG)),
        _const_spec((SC_CONV, D_SC)),
        _const_spec((D_MODEL, D_MODEL)),
        _const_spec((1, D_MODEL)), _const_spec((1, D_MODEL)),
        _const_spec((D_MODEL, 2 * ROUTER_LANES)), _const_spec((1, ROUTER_LANES)),
    ]
    out_specs = [
        pl.BlockSpec((M1, D_MODEL), lambda c: (c, 0)),
        pl.BlockSpec((ROUTER_ROWS, M1), lambda c: (0, c)),
    ]
    return pl.pallas_call(
        _mixer_kernel,
        grid=(n_chunks,),
        in_specs=in_specs,
        out_specs=out_specs,
        out_shape=[
            jax.ShapeDtypeStruct((N_TOKENS, D_MODEL), jnp.float32),
            jax.ShapeDtypeStruct((ROUTER_ROWS, N_TOKENS), jnp.float32),
        ],
        scratch_shapes=[
            pltpu.VMEM((M1, D_MODEL), jnp.float32),
            pltpu.VMEM((M1 + (RG_CONV - 1) * BATCH, D_RG), jnp.float32),
            pltpu.VMEM((M1 + (SC_CONV - 1) * BATCH, D_SC), jnp.float32),
            pltpu.VMEM((M1, D_RG), jnp.float32),
            pltpu.VMEM((M1, D_RG), jnp.float32),
            pltpu.VMEM((BATCH, D_RG), jnp.float32),
        ],
        compiler_params=pltpu.CompilerParams(
            dimension_semantics=("arbitrary",), vmem_limit_bytes=VMEM_LIMIT),
        name="mixer_ln_router",
    )(x, w_in, cw, cb, wa, ba, wx, bx, lam, scw, w_out, g1, b1, wr, br)


def _expert_kernel(grp_ref, ea_ref, eb_ref, valid_ref,
                   xs_ref, wt_ref, wg_ref, wu_ref, wd_ref, g2_ref, b2_ref, out_ref):
    s = pl.program_id(0)

    def sub_tile(j, carry):
        tile = s * SUB + j
        rows = pl.ds(pl.multiple_of(j * TM, TM), TM)

        @pl.when(valid_ref[tile] == 0)
        def _():
            out_ref[rows, :] = jnp.zeros((TM, D_MODEL), jnp.float32)

        @pl.when(valid_ref[tile] != 0)
        def _():
            x = xs_ref[rows, :]
            xb = x.astype(jnp.bfloat16)
            wt = wt_ref[rows, :]
            y = jnp.zeros((TM, D_MODEL), jnp.float32)
            for e_ref, col in ((ea_ref, 0), (eb_ref, 1)):
                e = e_ref[tile]
                hg = jnp.dot(xb, wg_ref[e], preferred_element_type=jnp.float32)
                hu = jnp.dot(xb, wu_ref[e], preferred_element_type=jnp.float32)
                hid = jax.nn.silu(hg) * hu * wt[:, col:col + 1]
                y = y + jnp.dot(hid.astype(jnp.bfloat16), wd_ref[e],
                                preferred_element_type=jnp.float32)
            out_ref[rows, :] = _layer_norm(DEEPNORM_ALPHA * x + y, g2_ref[...], b2_ref[...])

        return carry

    lax.fori_loop(0, SUB, sub_tile, 0)


def _expert_call(step_group, tile_ea, tile_eb, tile_valid, xs, wts, wg, wu, wd, g2, b2):
    rows = SUB * TM
    grid_spec = pltpu.PrefetchScalarGridSpec(
        num_scalar_prefetch=4,
        grid=(N_STEPS3,),
        in_specs=[
            pl.BlockSpec((rows, D_MODEL), lambda s, grp, ea, eb, va: (s, 0)),
            pl.BlockSpec((rows, LANES), lambda s, grp, ea, eb, va: (s, 0)),
            pl.BlockSpec((None, EXPERTS_PER_GROUP, D_MODEL, D_EXPERT),
                         lambda s, grp, ea, eb, va: (grp[s], 0, 0, 0)),
            pl.BlockSpec((None, EXPERTS_PER_GROUP, D_MODEL, D_EXPERT),
                         lambda s, grp, ea, eb, va: (grp[s], 0, 0, 0)),
            pl.BlockSpec((None, EXPERTS_PER_GROUP, D_EXPERT, D_MODEL),
                         lambda s, grp, ea, eb, va: (grp[s], 0, 0, 0)),
            pl.BlockSpec((1, D_MODEL), lambda s, grp, ea, eb, va: (0, 0)),
            pl.BlockSpec((1, D_MODEL), lambda s, grp, ea, eb, va: (0, 0)),
        ],
        out_specs=pl.BlockSpec((rows, D_MODEL), lambda s, grp, ea, eb, va: (s, 0)),
    )
    return pl.pallas_call(
        _expert_kernel,
        grid_spec=grid_spec,
        out_shape=jax.ShapeDtypeStruct((P_ROWS, D_MODEL), jnp.float32),
        compiler_params=pltpu.CompilerParams(
            dimension_semantics=("arbitrary",), vmem_limit_bytes=VMEM_LIMIT),
        name="experts_ln",
    )(step_group, tile_ea, tile_eb, tile_valid, xs, wts, wg, wu, wd, g2, b2)


def _block_diag(w):
    h, d, _ = w.shape
    eye = jnp.eye(h, dtype=w.dtype)
    return (eye[:, None, :, None] * w[:, :, None, :]).reshape(h * d, h * d)


def _pair_tables():
    pair_of = np.zeros((EXPERTS_PER_GROUP, EXPERTS_PER_GROUP), np.int32)
    lo = np.zeros((N_PAIRS,), np.int32)
    hi = np.zeros((N_PAIRS,), np.int32)
    p = 0
    for a in range(EXPERTS_PER_GROUP):
        for b in range(a + 1, EXPERTS_PER_GROUP):
            pair_of[a, b] = pair_of[b, a] = p
            lo[p], hi[p] = a, b
            p += 1
    return pair_of, lo, hi


def _route_kernel(lt_ref, info_ref, meta_ref, counts_ref, run_ref):
    step = pl.program_id(0)

    @pl.when(step == 0)
    def _():
        run_ref[...] = jnp.zeros_like(run_ref)

    f32 = jnp.float32
    sub8 = lax.broadcasted_iota(jnp.int32, (8, LANES), 0).astype(f32)
    row_id = lax.broadcasted_iota(jnp.int32, (LANES, LANES), 0)
    col_id = lax.broadcasted_iota(jnp.int32, (LANES, LANES), 1)
    prefix_mat = (row_id <= col_id).astype(jnp.bfloat16)
    ones_mat = jnp.ones((LANES, LANES), jnp.bfloat16)
    neg_inf = f32(-jnp.inf)

    def first_index_of_max(v):
        m = jnp.max(v, axis=0, keepdims=True)
        idx = jnp.min(jnp.where(v == m, sub8, f32(8)), axis=0, keepdims=True)
        return m, idx

    def lane_tile(k, run):
        lanes = pl.ds(pl.multiple_of(k * LANES, LANES), LANES)
        g = jnp.where(sub8 < N_GROUPS, lt_ref[0:GROUP_ROWS, lanes], neg_inf)
        g_max, g_idx = first_index_of_max(g)
        g_top_p = 1.0 / jnp.sum(jnp.exp(g - g_max), axis=0, keepdims=True)

        e_sel = lt_ref[GROUP_ROWS:GROUP_ROWS + EXPERTS_PER_GROUP, lanes]
        for grp in range(1, N_GROUPS):
            lo = GROUP_ROWS + grp * EXPERTS_PER_GROUP
            e_sel = jnp.where(g_idx == grp, lt_ref[lo:lo + EXPERTS_PER_GROUP, lanes], e_sel)
        m1, i1 = first_index_of_max(e_sel)
        rest = jnp.where(sub8 == i1, neg_inf, e_sel)
        m2 = jnp.max(rest, axis=0, keepdims=True)
        i2 = jnp.min(jnp.where((rest == m2) & (sub8 != i1), sub8, f32(8)), axis=0, keepdims=True)

        e = jnp.exp(m2 - m1)
        w1 = g_top_p / (1.0 + e)
        w2 = g_top_p * e / (1.0 + e)
        first_is_lo = i1 < i2
        w_lo = jnp.where(first_is_lo, w1, w2)
        w_hi = jnp.where(first_is_lo, w2, w1)
        lo_e = jnp.minimum(i1, i2)
        hi_e = jnp.maximum(i1, i2)
        pair = lo_e * (2 * EXPERTS_PER_GROUP - 1 - lo_e) * 0.5 + hi_e - lo_e - 1.0
        cls = (g_idx * N_PAIRS + pair).astype(jnp.int32)

        onehot = (row_id == cls).astype(jnp.bfloat16)
        csum = jnp.dot(onehot, prefix_mat, preferred_element_type=f32)
        rank = jnp.sum(onehot.astype(f32) * (csum + run), axis=0, keepdims=True) - 1.0
        run = run + jnp.dot(onehot, ones_mat, preferred_element_type=f32)

        info_ref[:, lanes] = jnp.where(sub8 == 0, cls, jnp.where(sub8 == 1, rank.astype(jnp.int32), 0))
        meta8 = jnp.where(sub8 == 0, w_lo, jnp.where(sub8 == 1, w_hi, 0.0))
        meta_t = jnp.concatenate([meta8, jnp.zeros((LANES - 8, LANES), f32)], axis=0)
        meta_ref[lanes, :] = meta_t.T
        return run

    run = lax.fori_loop(0, ROUTE_BLOCK // LANES, lane_tile, run_ref[...])
    run_ref[...] = run
    counts_ref[...] = run


def _route_call(lt):
    return pl.pallas_call(
        _route_kernel,
        grid=(N_TOKENS // ROUTE_BLOCK,),
        in_specs=[pl.BlockSpec((ROUTER_ROWS, ROUTE_BLOCK), lambda i: (0, i))],
        out_specs=[
            pl.BlockSpec((8, ROUTE_BLOCK), lambda i: (0, i)),
            pl.BlockSpec((ROUTE_BLOCK, LANES), lambda i: (i, 0)),
            pl.BlockSpec((LANES, LANES), lambda i: (0, 0)),
        ],
        out_shape=[
            jax.ShapeDtypeStruct((8, N_TOKENS), jnp.int32),
            jax.ShapeDtypeStruct((N_TOKENS, LANES), jnp.float32),
            jax.ShapeDtypeStruct((LANES, LANES), jnp.float32),
        ],
        scratch_shapes=[pltpu.VMEM((LANES, LANES), jnp.float32)],
        compiler_params=pltpu.CompilerParams(dimension_semantics=("arbitrary",)),
        name="route_rank",
    )(lt)


def _dispatch_plan(cls, rank, counts):
    _, pair_lo, pair_hi = _pair_tables()
    tiles_c = (counts + TM - 1) // TM
    tiles_g = tiles_c.reshape(N_GROUPS, N_PAIRS).sum(axis=1)
    tiles_g_pad = (tiles_g + SUB - 1) // SUB * SUB
    g_start = jnp.cumsum(tiles_g_pad) - tiles_g_pad
    tc = tiles_c.reshape(N_GROUPS, N_PAIRS)
    c_start = (g_start[:, None] + jnp.cumsum(tc, axis=1) - tc).reshape(N_CLASSES)
    pos = c_start[cls] * TM + rank

    tile_ids = jnp.arange(N_TILES, dtype=jnp.int32)
    c_end = c_start + tiles_c
    owner = jnp.sum((tile_ids[:, None] >= c_end[None, :]).astype(jnp.int32), axis=1)
    owner = jnp.minimum(owner, N_CLASSES - 1)
    valid = ((tile_ids >= c_start[owner]) & (tile_ids < c_end[owner])).astype(jnp.int32)
    pair = owner % N_PAIRS
    tile_ea = jnp.asarray(pair_lo)[pair]
    tile_eb = jnp.asarray(pair_hi)[pair]
    g_end = g_start + tiles_g_pad
    step_first = jnp.arange(N_STEPS3, dtype=jnp.int32) * SUB
    step_group = jnp.sum((step_first[:, None] >= g_end[None, :]).astype(jnp.int32), axis=1)
    step_group = jnp.minimum(step_group, N_GROUPS - 1)
    return pos.astype(jnp.int32), tile_ea, tile_eb, valid, step_group.astype(jnp.int32)


def kernel(x, w_in, rg_conv_w, rg_conv_b, rg_gate_a_w, rg_gate_a_b, rg_gate_x_w, rg_gate_x_b,
           rg_lambda, sc_conv_w, w_out, ln1_g, ln1_b, router_group_w, router_group_b,
           router_expert_w, router_expert_b, exp_w_gate, exp_w_up, exp_w_down, ln2_g, ln2_b):
    bf16 = jnp.bfloat16
    row = lambda v: v.reshape(1, -1)
    pad_g = GROUP_ROWS - N_GROUPS
    pad_e = ROUTER_LANES - ROUTER_ROWS
    wr = jnp.concatenate([router_group_w, jnp.zeros((D_MODEL, pad_g), jnp.float32),
                          router_expert_w, jnp.zeros((D_MODEL, pad_e), jnp.float32)], axis=1)
    br = jnp.concatenate([router_group_b, jnp.zeros((pad_g,), jnp.float32),
                          router_expert_b, jnp.zeros((pad_e,), jnp.float32)]).reshape(1, -1)
    wr_hi = wr.astype(bf16)
    wr_lo = (wr - wr_hi.astype(jnp.float32)).astype(bf16)

    h1, lt = _mixer_call(
        x, w_in.astype(bf16), rg_conv_w, row(rg_conv_b),
        _block_diag(rg_gate_a_w).astype(bf16), row(rg_gate_a_b),
        _block_diag(rg_gate_x_w).astype(bf16), row(rg_gate_x_b),
        row(rg_lambda), sc_conv_w, w_out.astype(bf16), row(ln1_g), row(ln1_b),
        jnp.concatenate([wr_hi, wr_lo], axis=1), br)

    info, meta, counts = _route_call(lt)
    cls, rank = info[0], info[1]
    pos, tile_ea, tile_eb, tile_valid, step_group = _dispatch_plan(
        cls, rank, counts[:N_CLASSES, 0].astype(jnp.int32))

    src = jnp.zeros((P_ROWS,), jnp.int32).at[pos].set(jnp.arange(N_TOKENS, dtype=jnp.int32))
    xs = jnp.take(h1, src, axis=0)
    wts = jnp.take(meta, src, axis=0)

    grp_shape = (N_GROUPS, EXPERTS_PER_GROUP)
    ys = _expert_call(
        step_group, tile_ea, tile_eb, tile_valid, xs, wts,
        exp_w_gate.astype(bf16).reshape(grp_shape + (D_MODEL, D_EXPERT)),
        exp_w_up.astype(bf16).reshape(grp_shape + (D_MODEL, D_EXPERT)),
        exp_w_down.astype(bf16).reshape(grp_shape + (D_EXPERT, D_MODEL)),
        row(ln2_g), row(ln2_b))

    out_rows = pos.reshape(SEQ, BATCH).T.reshape(-1)
    return jnp.take(ys, out_rows, axis=0).reshape(BATCH, SEQ, D_MODEL)
```

```python
import functools

import jax
import jax.numpy as jnp
import numpy as np
from jax import lax
from jax.experimental import pallas as pl
from jax.experimental.pallas import tpu as pltpu
from jax.experimental.pallas import tpu_sc as plsc

D_MODEL = 1024
BATCH = 16
SEQ = 2048
D_RG = 512
D_SC = 512
RG_HEADS = 8
RG_HEAD_DIM = D_RG // RG_HEADS
RG_CONV = 4
RG_C = 8.0
SC_CONV = 3
D_IN_PROJ = 2 * D_RG + 3 * D_SC
N_GROUPS = 4
EXPERTS_PER_GROUP = 8
N_EXPERTS = N_GROUPS * EXPERTS_PER_GROUP
D_EXPERT = D_MODEL // 4
LN_EPS = 1e-5
DEEPNORM_ALPHA = 2.0 ** 0.25

N_TOKENS = BATCH * SEQ
LANES = 128
ROUTER_LANES = LANES
GROUP_ROWS = 8
ROUTER_ROWS = GROUP_ROWS + N_EXPERTS

TS = 64
M1 = TS * BATCH

ROUTE_BLOCK = 4096

TM = 128
SUB = 4
N_PAIRS = EXPERTS_PER_GROUP * (EXPERTS_PER_GROUP - 1) // 2
N_CLASSES = N_GROUPS * N_PAIRS
N_TILES = -(-(N_TOKENS // TM + N_CLASSES + N_GROUPS * (SUB - 1)) // SUB) * SUB
N_STEPS3 = N_TILES // SUB
P_ROWS = N_TILES * TM

VMEM_LIMIT = 56 * 1024 * 1024


def _layer_norm(z, g, b):
    mu = jnp.mean(z, axis=-1, keepdims=True)
    zc = z - mu
    var = jnp.mean(zc * zc, axis=-1, keepdims=True)
    return zc * lax.rsqrt(var + LN_EPS) * g + b


def _mixer_kernel(x_ref, w_in_ref, cw_ref, cb_ref, wa_ref, ba_ref, wx_ref, bx_ref, lam_ref,
                  scw_ref, w_out_ref, g1_ref, b1_ref, wr_ref, br_ref,
                  h1_ref, logit_ref,
                  xt_ref, cbuf_ref, sbuf_ref, a_ref, u_ref, hstate_ref):
    c = pl.program_id(0)
    rg_pad = (RG_CONV - 1) * BATCH
    sc_pad = (SC_CONV - 1) * BATCH

    @pl.when(c == 0)
    def _():
        cbuf_ref[0:rg_pad, :] = jnp.zeros((rg_pad, D_RG), jnp.float32)
        sbuf_ref[0:sc_pad, :] = jnp.zeros((sc_pad, D_SC), jnp.float32)
        hstate_ref[...] = jnp.zeros_like(hstate_ref)

    for t in range(TS):
        xt_ref[t * BATCH:(t + 1) * BATCH, :] = x_ref[:, t, :]

    xb = xt_ref[...].astype(jnp.bfloat16)

    def proj(lo, width):
        return jnp.dot(xb, w_in_ref[:, lo:lo + width], preferred_element_type=jnp.float32)

    cbuf_ref[rg_pad:rg_pad + M1, :] = proj(0, D_RG)
    xc = cb_ref[...] + cw_ref[0:1, :] * cbuf_ref[0:M1, :]
    for k in range(1, RG_CONV):
        xc = xc + cw_ref[k:k + 1, :] * cbuf_ref[k * BATCH:k * BATCH + M1, :]
    cbuf_ref[0:rg_pad, :] = cbuf_ref[M1:M1 + rg_pad, :]

    xcb = xc.astype(jnp.bfloat16)
    r = jax.nn.sigmoid(jnp.dot(xcb, wa_ref[...], preferred_element_type=jnp.float32) + ba_ref[...])
    i = jax.nn.sigmoid(jnp.dot(xcb, wx_ref[...], preferred_element_type=jnp.float32) + bx_ref[...])
    log_a = (-RG_C) * r * jax.nn.softplus(-lam_ref[...])
    a_ref[...] = jnp.exp(log_a)
    th = jnp.tanh(log_a)
    u_ref[...] = jnp.sqrt(-2.0 * th / (1.0 - th)) * (i * xc)

    def scan_body(t, h):
        rows = pl.ds(pl.multiple_of(t * BATCH, BATCH), BATCH)
        h = a_ref[rows, :] * h + u_ref[rows, :]
        u_ref[rows, :] = h
        return h

    hstate_ref[...] = lax.fori_loop(0, TS, scan_body, hstate_ref[...], unroll=8)

    y_rg = u_ref[...] * jax.nn.gelu(proj(D_RG, D_RG))

    sc_b = proj(2 * D_RG, D_SC)
    sbuf_ref[sc_pad:sc_pad + M1, :] = proj(2 * D_RG + D_SC, D_SC) * proj(2 * D_RG + 2 * D_SC, D_SC)
    conv = scw_ref[0:1, :] * sbuf_ref[0:M1, :]
    for k in range(1, SC_CONV):
        conv = conv + scw_ref[k:k + 1, :] * sbuf_ref[k * BATCH:k * BATCH + M1, :]
    sbuf_ref[0:sc_pad, :] = sbuf_ref[M1:M1 + sc_pad, :]
    y_sc = sc_b * conv

    mix = jnp.dot(y_rg.astype(jnp.bfloat16), w_out_ref[0:D_RG, :], preferred_element_type=jnp.float32)
    mix = mix + jnp.dot(y_sc.astype(jnp.bfloat16), w_out_ref[D_RG:, :], preferred_element_type=jnp.float32)

    h1 = _layer_norm(DEEPNORM_ALPHA * xt_ref[...] + mix, g1_ref[...], b1_ref[...])
    h1_ref[...] = h1
    h_hi = h1.astype(jnp.bfloat16)
    h_lo = (h1 - h_hi.astype(jnp.float32)).astype(jnp.bfloat16)
    both = jnp.dot(h_hi, wr_ref[...], preferred_element_type=jnp.float32)
    logits = both[:, :ROUTER_LANES] + both[:, ROUTER_LANES:] + br_ref[...]
    logits = logits + jnp.dot(h_lo, wr_ref[:, :ROUTER_LANES], preferred_element_type=jnp.float32)
    logit_ref[...] = logits.T[:ROUTER_ROWS, :]


def _const_spec(shape):
    return pl.BlockSpec(shape, lambda c: (0,) * len(shape))


def _mixer_call(x, w_in, cw, cb, wa, ba, wx, bx, lam, scw, w_out, g1, b1, wr, br):
    n_chunks = SEQ // TS
    in_specs = [
        pl.BlockSpec((BATCH, TS, D_MODEL), lambda c: (0, c, 0)),
        _const_spec((D_MODEL, D_IN_PROJ)),
        _const_spec((RG_CONV, D_RG)), _const_spec((1, D_RG)),
        _const_spec((D_RG, D_RG)), _const_spec((1, D_RG)),
        _const_spec((D_RG, D_RG)), _const_spec((1, D_RG)),
        _const_spec((1, D_RG)),
        _const_spec((SC_CONV, D_SC)),
        _const_spec((D_MODEL, D_MODEL)),
        _const_spec((1, D_MODEL)), _const_spec((1, D_MODEL)),
        _const_spec((D_MODEL, 2 * ROUTER_LANES)), _const_spec((1, ROUTER_LANES)),
    ]
    out_specs = [
        pl.BlockSpec((M1, D_MODEL), lambda c: (c, 0)),
        pl.BlockSpec((ROUTER_ROWS, M1), lambda c: (0, c)),
    ]
    return pl.pallas_call(
        _mixer_kernel,
        grid=(n_chunks,),
        in_specs=in_specs,
        out_specs=out_specs,
        out_shape=[
            jax.ShapeDtypeStruct((N_TOKENS, D_MODEL), jnp.float32),
            jax.ShapeDtypeStruct((ROUTER_ROWS, N_TOKENS), jnp.float32),
        ],
        scratch_shapes=[
            pltpu.VMEM((M1, D_MODEL), jnp.float32),
            pltpu.VMEM((M1 + (RG_CONV - 1) * BATCH, D_RG), jnp.float32),
            pltpu.VMEM((M1 + (SC_CONV - 1) * BATCH, D_SC), jnp.float32),
            pltpu.VMEM((M1, D_RG), jnp.float32),
            pltpu.VMEM((M1, D_RG), jnp.float32),
            pltpu.VMEM((BATCH, D_RG), jnp.float32),
        ],
        compiler_params=pltpu.CompilerParams(
            dimension_semantics=("arbitrary",), vmem_limit_bytes=VMEM_LIMIT),
        name="mixer_ln_router",
    )(x, w_in, cw, cb, wa, ba, wx, bx, lam, scw, w_out, g1, b1, wr, br)


def _expert_kernel(grp_ref, ea_ref, eb_ref, valid_ref,
                   xs_ref, wt_ref, wg_ref, wu_ref, wd_ref, g2_ref, b2_ref, out_ref):
    s = pl.program_id(0)

    def sub_tile(j, carry):
        tile = s * SUB + j
        rows = pl.ds(pl.multiple_of(j * TM, TM), TM)

        @pl.when(valid_ref[tile] == 0)
        def _():
            out_ref[rows, :] = jnp.zeros((TM, D_MODEL), jnp.float32)

        @pl.when(valid_ref[tile] != 0)
        def _():
            x = xs_ref[rows, :]
            xb = x.astype(jnp.bfloat16)
            wt = wt_ref[rows, :]
            y = jnp.zeros((TM, D_MODEL), jnp.float32)
            for e_ref, col in ((ea_ref, 0), (eb_ref, 1)):
                e = e_ref[tile]
                hg = jnp.dot(xb, wg_ref[e], preferred_element_type=jnp.float32)
                hu = jnp.dot(xb, wu_ref[e], preferred_element_type=jnp.float32)
                hid = jax.nn.silu(hg) * hu * wt[:, col:col + 1]
                y = y + jnp.dot(hid.astype(jnp.bfloat16), wd_ref[e],
                                preferred_element_type=jnp.float32)
            out_ref[rows, :] = _layer_norm(DEEPNORM_ALPHA * x + y, g2_ref[...], b2_ref[...])

        return carry

    lax.fori_loop(0, SUB, sub_tile, 0)


def _expert_call(step_group, tile_ea, tile_eb, tile_valid, xs, wts, wg, wu, wd, g2, b2):
    rows = SUB * TM
    grid_spec = pltpu.PrefetchScalarGridSpec(
        num_scalar_prefetch=4,
        grid=(N_STEPS3,),
        in_specs=[
            pl.BlockSpec((rows, D_MODEL), lambda s, grp, ea, eb, va: (s, 0)),
            pl.BlockSpec((rows, LANES), lambda s, grp, ea, eb, va: (s, 0)),
            pl.BlockSpec((None, EXPERTS_PER_GROUP, D_MODEL, D_EXPERT),
                         lambda s, grp, ea, eb, va: (grp[s], 0, 0, 0)),
            pl.BlockSpec((None, EXPERTS_PER_GROUP, D_MODEL, D_EXPERT),
                         lambda s, grp, ea, eb, va: (grp[s], 0, 0, 0)),
            pl.BlockSpec((None, EXPERTS_PER_GROUP, D_EXPERT, D_MODEL),
                         lambda s, grp, ea, eb, va: (grp[s], 0, 0, 0)),
            pl.BlockSpec((1, D_MODEL), lambda s, grp, ea, eb, va: (0, 0)),
            pl.BlockSpec((1, D_MODEL), lambda s, grp, ea, eb, va: (0, 0)),
        ],
        out_specs=pl.BlockSpec((rows, D_MODEL), lambda s, grp, ea, eb, va: (s, 0)),
    )
    return pl.pallas_call(
        _expert_kernel,
        grid_spec=grid_spec,
        out_shape=jax.ShapeDtypeStruct((P_ROWS, D_MODEL), jnp.float32),
        compiler_params=pltpu.CompilerParams(
            dimension_semantics=("arbitrary",), vmem_limit_bytes=VMEM_LIMIT),
        name="experts_ln",
    )(step_group, tile_ea, tile_eb, tile_valid, xs, wts, wg, wu, wd, g2, b2)


SC_CORES = 2
SC_SUBCORES = 16
SC_WORKERS = SC_CORES * SC_SUBCORES
DISPATCH_CHUNK = 40
COMBINE_CHUNK = 32


def _sc_gather_rows(table, idx, chunk):
    n_out, = idx.shape
    d = table.shape[1]
    per_w = n_out // SC_WORKERS
    n_chunks = per_w // chunk
    assert per_w * SC_WORKERS == n_out and n_chunks * chunk == per_w
    assert n_chunks % 2 == 0 and chunk % 8 == 0
    mesh = plsc.VectorSubcoreMesh(core_axis_name="c", subcore_axis_name="s")

    @functools.partial(
        pl.kernel, mesh=mesh,
        out_type=jax.ShapeDtypeStruct((n_out, d), table.dtype),
        scratch_types=[
            pltpu.VMEM((per_w,), jnp.int32),
            pltpu.VMEM((2, chunk, d), table.dtype),
            pltpu.SemaphoreType.DMA((2,)),
            pltpu.SemaphoreType.DMA((2,)),
        ],
    )
    def gather_kernel(table_hbm, idx_hbm, out_hbm, idx_v, buf, gsem, wsem):
        wid = lax.axis_index("s") * SC_CORES + lax.axis_index("c")
        base = wid * per_w
        pltpu.sync_copy(idx_hbm.at[pl.ds(base, per_w)], idx_v)

        def gather(j, slot):
            rows = idx_v.at[pl.ds(j * chunk, chunk)]
            return pltpu.make_async_copy(table_hbm.at[rows], buf.at[slot], gsem.at[slot])

        def write(j, slot):
            dst = out_hbm.at[pl.ds(base + j * chunk, chunk)]
            return pltpu.make_async_copy(buf.at[slot], dst, wsem.at[slot])

        gather(0, 0).start()

        @pl.loop(0, n_chunks, step=2)
        def _(j0):
            for slot in range(2):
                j = j0 + slot
                gather(j, slot).wait()

                @pl.when(j >= 1)
                def _():
                    write(j - 1, 1 - slot).wait()

                @pl.when(j + 1 < n_chunks)
                def _():
                    gather(j + 1, 1 - slot).start()

                write(j, slot).start()

        write(n_chunks - 1, (n_chunks - 1) % 2).wait()

    return gather_kernel(table, idx)


def _block_diag(w):
    h, d, _ = w.shape
    eye = jnp.eye(h, dtype=w.dtype)
    return (eye[:, None, :, None] * w[:, :, None, :]).reshape(h * d, h * d)


def _pair_tables():
    pair_of = np.zeros((EXPERTS_PER_GROUP, EXPERTS_PER_GROUP), np.int32)
    lo = np.zeros((N_PAIRS,), np.int32)
    hi = np.zeros((N_PAIRS,), np.int32)
    p = 0
    for a in range(EXPERTS_PER_GROUP):
        for b in range(a + 1, EXPERTS_PER_GROUP):
            pair_of[a, b] = pair_of[b, a] = p
            lo[p], hi[p] = a, b
            p += 1
    return pair_of, lo, hi


def _route_kernel(lt_ref, info_ref, meta_ref, counts_ref, run_ref):
    step = pl.program_id(0)

    @pl.when(step == 0)
    def _():
        run_ref[...] = jnp.zeros_like(run_ref)

    f32 = jnp.float32
    sub8 = lax.broadcasted_iota(jnp.int32, (8, LANES), 0).astype(f32)
    row_id = lax.broadcasted_iota(jnp.int32, (LANES, LANES), 0)
    col_id = lax.broadcasted_iota(jnp.int32, (LANES, LANES), 1)
    prefix_mat = (row_id <= col_id).astype(jnp.bfloat16)
    ones_mat = jnp.ones((LANES, LANES), jnp.bfloat16)
    neg_inf = f32(-jnp.inf)

    def first_index_of_max(v):
        m = jnp.max(v, axis=0, keepdims=True)
        idx = jnp.min(jnp.where(v == m, sub8, f32(8)), axis=0, keepdims=True)
        return m, idx

    def lane_tile(k, run):
        lanes = pl.ds(pl.multiple_of(k * LANES, LANES), LANES)
        g = jnp.where(sub8 < N_GROUPS, lt_ref[0:GROUP_ROWS, lanes], neg_inf)
        g_max, g_idx = first_index_of_max(g)
        g_top_p = 1.0 / jnp.sum(jnp.exp(g - g_max), axis=0, keepdims=True)

        e_sel = lt_ref[GROUP_ROWS:GROUP_ROWS + EXPERTS_PER_GROUP, lanes]
        for grp in range(1, N_GROUPS):
            lo = GROUP_ROWS + grp * EXPERTS_PER_GROUP
            e_sel = jnp.where(g_idx == grp, lt_ref[lo:lo + EXPERTS_PER_GROUP, lanes], e_sel)
        m1, i1 = first_index_of_max(e_sel)
        rest = jnp.where(sub8 == i1, neg_inf, e_sel)
        m2 = jnp.max(rest, axis=0, keepdims=True)
        i2 = jnp.min(jnp.where((rest == m2) & (sub8 != i1), sub8, f32(8)), axis=0, keepdims=True)

        e = jnp.exp(m2 - m1)
        w1 = g_top_p / (1.0 + e)
        w2 = g_top_p * e / (1.0 + e)
        first_is_lo = i1 < i2
        w_lo = jnp.where(first_is_lo, w1, w2)
        w_hi = jnp.where(first_is_lo, w2, w1)
        lo_e = jnp.minimum(i1, i2)
        hi_e = jnp.maximum(i1, i2)
        pair = lo_e * (2 * EXPERTS_PER_GROUP - 1 - lo_e) * 0.5 + hi_e - lo_e - 1.0
        cls = (g_idx * N_PAIRS + pair).astype(jnp.int32)

        onehot = (row_id == cls).astype(jnp.bfloat16)
        csum = jnp.dot(onehot, prefix_mat, preferred_element_type=f32)
        rank = jnp.sum(onehot.astype(f32) * (csum + run), axis=0, keepdims=True) - 1.0
        run = run + jnp.dot(onehot, ones_mat, preferred_element_type=f32)

        info_ref[:, lanes] = jnp.where(sub8 == 0, cls, jnp.where(sub8 == 1, rank.astype(jnp.int32), 0))
        meta8 = jnp.where(sub8 == 0, w_lo, jnp.where(sub8 == 1, w_hi, 0.0))
        meta_t = jnp.concatenate([meta8, jnp.zeros((LANES - 8, LANES), f32)], axis=0)
        meta_ref[lanes, :] = meta_t.T
        return run

    run = lax.fori_loop(0, ROUTE_BLOCK // LANES, lane_tile, run_ref[...])
    run_ref[...] = run
    counts_ref[...] = run


def _route_call(lt):
    return pl.pallas_call(
        _route_kernel,
        grid=(N_TOKENS // ROUTE_BLOCK,),
        in_specs=[pl.BlockSpec((ROUTER_ROWS, ROUTE_BLOCK), lambda i: (0, i))],
        out_specs=[
            pl.BlockSpec((8, ROUTE_BLOCK), lambda i: (0, i)),
            pl.BlockSpec((ROUTE_BLOCK, LANES), lambda i: (i, 0)),
            pl.BlockSpec((LANES, LANES), lambda i: (0, 0)),
        ],
        out_shape=[
            jax.ShapeDtypeStruct((8, N_TOKENS), jnp.int32),
            jax.ShapeDtypeStruct((N_TOKENS, LANES), jnp.float32),
            jax.ShapeDtypeStruct((LANES, LANES), jnp.float32),
        ],
        scratch_shapes=[pltpu.VMEM((LANES, LANES), jnp.float32)],
        compiler_params=pltpu.CompilerParams(dimension_semantics=("arbitrary",)),
        name="route_rank",
    )(lt)


def _dispatch_plan(cls, rank, counts):
    _, pair_lo, pair_hi = _pair_tables()
    tiles_c = (counts + TM - 1) // TM
    tiles_g = tiles_c.reshape(N_GROUPS, N_PAIRS).sum(axis=1)
    tiles_g_pad = (tiles_g + SUB - 1) // SUB * SUB
    g_start = jnp.cumsum(tiles_g_pad) - tiles_g_pad
    tc = tiles_c.reshape(N_GROUPS, N_PAIRS)
    c_start = (g_start[:, None] + jnp.cumsum(tc, axis=1) - tc).reshape(N_CLASSES)
    pos = c_start[cls] * TM + rank

    tile_ids = jnp.arange(N_TILES, dtype=jnp.int32)
    c_end = c_start + tiles_c
    owner = jnp.sum((tile_ids[:, None] >= c_end[None, :]).astype(jnp.int32), axis=1)
    owner = jnp.minimum(owner, N_CLASSES - 1)
    valid = ((tile_ids >= c_start[owner]) & (tile_ids < c_end[owner])).astype(jnp.int32)
    pair = owner % N_PAIRS
    tile_ea = jnp.asarray(pair_lo)[pair]
    tile_eb = jnp.asarray(pair_hi)[pair]
    g_end = g_start + tiles_g_pad
    step_first = jnp.arange(N_STEPS3, dtype=jnp.int32) * SUB
    step_group = jnp.sum((step_first[:, None] >= g_end[None, :]).astype(jnp.int32), axis=1)
    step_group = jnp.minimum(step_group, N_GROUPS - 1)
    return pos.astype(jnp.int32), tile_ea, tile_eb, valid, step_group.astype(jnp.int32)


def kernel(x, w_in, rg_conv_w, rg_conv_b, rg_gate_a_w, rg_gate_a_b, rg_gate_x_w, rg_gate_x_b,
           rg_lambda, sc_conv_w, w_out, ln1_g, ln1_b, router_group_w, router_group_b,
           router_expert_w, router_expert_b, exp_w_gate, exp_w_up, exp_w_down, ln2_g, ln2_b):
    bf16 = jnp.bfloat16
    row = lambda v: v.reshape(1, -1)
    pad_g = GROUP_ROWS - N_GROUPS
    pad_e = ROUTER_LANES - ROUTER_ROWS
    wr = jnp.concatenate([router_group_w, jnp.zeros((D_MODEL, pad_g), jnp.float32),
                          router_expert_w, jnp.zeros((D_MODEL, pad_e), jnp.float32)], axis=1)
    br = jnp.concatenate([router_group_b, jnp.zeros((pad_g,), jnp.float32),
                          router_expert_b, jnp.zeros((pad_e,), jnp.float32)]).reshape(1, -1)
    wr_hi = wr.astype(bf16)
    wr_lo = (wr - wr_hi.astype(jnp.float32)).astype(bf16)

    h1, lt = _mixer_call(
        x, w_in.astype(bf16), rg_conv_w, row(rg_conv_b),
        _block_diag(rg_gate_a_w).astype(bf16), row(rg_gate_a_b),
        _block_diag(rg_gate_x_w).astype(bf16), row(rg_gate_x_b),
        row(rg_lambda), sc_conv_w, w_out.astype(bf16), row(ln1_g), row(ln1_b),
        jnp.concatenate([wr_hi, wr_lo], axis=1), br)

    info, meta, counts = _route_call(lt)
    cls, rank = info[0], info[1]
    pos, tile_ea, tile_eb, tile_valid, step_group = _dispatch_plan(
        cls, rank, counts[:N_CLASSES, 0].astype(jnp.int32))

    src = jnp.zeros((P_ROWS,), jnp.int32).at[pos].set(jnp.arange(N_TOKENS, dtype=jnp.int32))
    xs = _sc_gather_rows(h1, src, DISPATCH_CHUNK)
    wts = _sc_gather_rows(meta, src, DISPATCH_CHUNK)

    grp_shape = (N_GROUPS, EXPERTS_PER_GROUP)
    ys = _expert_call(
        step_group, tile_ea, tile_eb, tile_valid, xs, wts,
        exp_w_gate.astype(bf16).reshape(grp_shape + (D_MODEL, D_EXPERT)),
        exp_w_up.astype(bf16).reshape(grp_shape + (D_MODEL, D_EXPERT)),
        exp_w_down.astype(bf16).reshape(grp_shape + (D_EXPERT, D_MODEL)),
        row(ln2_g), row(ln2_b))

    out_rows = pos.reshape(SEQ, BATCH).T.reshape(-1)
    return _sc_gather_rows(ys, out_rows, COMBINE_CHUNK).reshape(BATCH, SEQ, D_MODEL)
```

```python
import functools

import jax
import jax.numpy as jnp
import numpy as np
from jax import lax
from jax.experimental import pallas as pl
from jax.experimental.pallas import tpu as pltpu
from jax.experimental.pallas import tpu_sc as plsc

D_MODEL = 1024
BATCH = 16
SEQ = 2048
D_RG = 512
D_SC = 512
RG_HEADS = 8
RG_HEAD_DIM = D_RG // RG_HEADS
RG_CONV = 4
RG_C = 8.0
SC_CONV = 3
D_IN_PROJ = 2 * D_RG + 3 * D_SC
N_GROUPS = 4
EXPERTS_PER_GROUP = 8
N_EXPERTS = N_GROUPS * EXPERTS_PER_GROUP
D_EXPERT = D_MODEL // 4
LN_EPS = 1e-5
DEEPNORM_ALPHA = 2.0 ** 0.25

N_TOKENS = BATCH * SEQ
LANES = 128
LANE_TILES = D_MODEL // LANES
ROUTER_LANES = LANES
GROUP_ROWS = 8
ROUTER_ROWS = GROUP_ROWS + N_EXPERTS

TS = 64
M1 = TS * BATCH

ROUTE_BLOCK = 4096

TM = 128
SUB = 4
N_PAIRS = EXPERTS_PER_GROUP * (EXPERTS_PER_GROUP - 1) // 2
N_CLASSES = N_GROUPS * N_PAIRS
N_TILES = -(-(N_TOKENS // TM + N_CLASSES + N_GROUPS * (SUB - 1)) // SUB) * SUB
N_STEPS3 = N_TILES // SUB
P_ROWS = N_TILES * TM

VMEM_LIMIT = 56 * 1024 * 1024


def _layer_norm(z, g, b):
    mu = jnp.mean(z, axis=-1, keepdims=True)
    zc = z - mu
    var = jnp.mean(zc * zc, axis=-1, keepdims=True)
    return zc * lax.rsqrt(var + LN_EPS) * g + b


def _store_row_tiles(ref, first_row, value):
    n = value.shape[0]
    for j in range(LANE_TILES):
        ref[pl.ds(first_row * LANE_TILES + j, n, stride=LANE_TILES), :] = (
            value[:, j * LANES:(j + 1) * LANES])


def _load_row_tiles(ref, first_row, n):
    return jnp.concatenate(
        [ref[pl.ds(first_row * LANE_TILES + j, n, stride=LANE_TILES), :]
         for j in range(LANE_TILES)], axis=1)


def _mixer_kernel(x_ref, w_in_ref, cw_ref, cb_ref, wa_ref, ba_ref, wx_ref, bx_ref, lam_ref,
                  scw_ref, w_out_ref, g1_ref, b1_ref, wr_ref, br_ref,
                  h1_ref, logit_ref,
                  xt_ref, cbuf_ref, sbuf_ref, a_ref, u_ref, hstate_ref):
    c = pl.program_id(0)
    rg_pad = (RG_CONV - 1) * BATCH
    sc_pad = (SC_CONV - 1) * BATCH

    @pl.when(c == 0)
    def _():
        cbuf_ref[0:rg_pad, :] = jnp.zeros((rg_pad, D_RG), jnp.float32)
        sbuf_ref[0:sc_pad, :] = jnp.zeros((sc_pad, D_SC), jnp.float32)
        hstate_ref[...] = jnp.zeros_like(hstate_ref)

    for t in range(TS):
        xt_ref[t * BATCH:(t + 1) * BATCH, :] = x_ref[:, t, :]

    xb = xt_ref[...].astype(jnp.bfloat16)

    def proj(lo, width):
        return jnp.dot(xb, w_in_ref[:, lo:lo + width], preferred_element_type=jnp.float32)

    cbuf_ref[rg_pad:rg_pad + M1, :] = proj(0, D_RG)
    xc = cb_ref[...] + cw_ref[0:1, :] * cbuf_ref[0:M1, :]
    for k in range(1, RG_CONV):
        xc = xc + cw_ref[k:k + 1, :] * cbuf_ref[k * BATCH:k * BATCH + M1, :]
    cbuf_ref[0:rg_pad, :] = cbuf_ref[M1:M1 + rg_pad, :]

    xcb = xc.astype(jnp.bfloat16)
    r = jax.nn.sigmoid(jnp.dot(xcb, wa_ref[...], preferred_element_type=jnp.float32) + ba_ref[...])
    i = jax.nn.sigmoid(jnp.dot(xcb, wx_ref[...], preferred_element_type=jnp.float32) + bx_ref[...])
    log_a = (-RG_C) * r * jax.nn.softplus(-lam_ref[...])
    a_ref[...] = jnp.exp(log_a)
    th = jnp.tanh(log_a)
    u_ref[...] = jnp.sqrt(-2.0 * th / (1.0 - th)) * (i * xc)

    def scan_body(t, h):
        rows = pl.ds(pl.multiple_of(t * BATCH, BATCH), BATCH)
        h = a_ref[rows, :] * h + u_ref[rows, :]
        u_ref[rows, :] = h
        return h

    hstate_ref[...] = lax.fori_loop(0, TS, scan_body, hstate_ref[...], unroll=8)

    y_rg = u_ref[...] * jax.nn.gelu(proj(D_RG, D_RG))

    sc_b = proj(2 * D_RG, D_SC)
    sbuf_ref[sc_pad:sc_pad + M1, :] = proj(2 * D_RG + D_SC, D_SC) * proj(2 * D_RG + 2 * D_SC, D_SC)
    conv = scw_ref[0:1, :] * sbuf_ref[0:M1, :]
    for k in range(1, SC_CONV):
        conv = conv + scw_ref[k:k + 1, :] * sbuf_ref[k * BATCH:k * BATCH + M1, :]
    sbuf_ref[0:sc_pad, :] = sbuf_ref[M1:M1 + sc_pad, :]
    y_sc = sc_b * conv

    mix = jnp.dot(y_rg.astype(jnp.bfloat16), w_out_ref[0:D_RG, :], preferred_element_type=jnp.float32)
    mix = mix + jnp.dot(y_sc.astype(jnp.bfloat16), w_out_ref[D_RG:, :], preferred_element_type=jnp.float32)

    h1 = _layer_norm(DEEPNORM_ALPHA * xt_ref[...] + mix, g1_ref[...], b1_ref[...])
    _store_row_tiles(h1_ref, 0, h1)
    h_hi = h1.astype(jnp.bfloat16)
    h_lo = (h1 - h_hi.astype(jnp.float32)).astype(jnp.bfloat16)
    both = jnp.dot(h_hi, wr_ref[...], preferred_element_type=jnp.float32)
    logits = both[:, :ROUTER_LANES] + both[:, ROUTER_LANES:] + br_ref[...]
    logits = logits + jnp.dot(h_lo, wr_ref[:, :ROUTER_LANES], preferred_element_type=jnp.float32)
    logit_ref[...] = logits.T[:ROUTER_ROWS, :]


def _const_spec(shape):
    return pl.BlockSpec(shape, lambda c: (0,) * len(shape))


def _mixer_call(x, w_in, cw, cb, wa, ba, wx, bx, lam, scw, w_out, g1, b1, wr, br):
    n_chunks = SEQ // TS
    in_specs = [
        pl.BlockSpec((BATCH, TS, D_MODEL), lambda c: (0, c, 0)),
        _const_spec((D_MODEL, D_IN_PROJ)),
        _const_spec((RG_CONV, D_RG)), _const_spec((1, D_RG)),
        _const_spec((D_RG, D_RG)), _const_spec((1, D_RG)),
        _const_spec((D_RG, D_RG)), _const_spec((1, D_RG)),
        _const_spec((1, D_RG)),
        _const_spec((SC_CONV, D_SC)),
        _const_spec((D_MODEL, D_MODEL)),
        _const_spec((1, D_MODEL)), _const_spec((1, D_MODEL)),
        _const_spec((D_MODEL, 2 * ROUTER_LANES)), _const_spec((1, ROUTER_LANES)),
    ]
    out_specs = [
        pl.BlockSpec((M1 * LANE_TILES, LANES), lambda c: (c, 0)),
        pl.BlockSpec((ROUTER_ROWS, M1), lambda c: (0, c)),
    ]
    return pl.pallas_call(
        _mixer_kernel,
        grid=(n_chunks,),
        in_specs=in_specs,
        out_specs=out_specs,
        out_shape=[
            jax.ShapeDtypeStruct((N_TOKENS * LANE_TILES, LANES), jnp.float32),
            jax.ShapeDtypeStruct((ROUTER_ROWS, N_TOKENS), jnp.float32),
        ],
        scratch_shapes=[
            pltpu.VMEM((M1, D_MODEL), jnp.float32),
            pltpu.VMEM((M1 + (RG_CONV - 1) * BATCH, D_RG), jnp.float32),
            pltpu.VMEM((M1 + (SC_CONV - 1) * BATCH, D_SC), jnp.float32),
            pltpu.VMEM((M1, D_RG), jnp.float32),
            pltpu.VMEM((M1, D_RG), jnp.float32),
            pltpu.VMEM((BATCH, D_RG), jnp.float32),
        ],
        compiler_params=pltpu.CompilerParams(
            dimension_semantics=("arbitrary",), vmem_limit_bytes=VMEM_LIMIT),
        name="mixer_ln_router",
    )(x, w_in, cw, cb, wa, ba, wx, bx, lam, scw, w_out, g1, b1, wr, br)


def _expert_kernel(grp_ref, ea_ref, eb_ref, valid_ref,
                   xs_ref, wt_ref, wg_ref, wu_ref, wd_ref, g2_ref, b2_ref, out_ref):
    s = pl.program_id(0)

    def sub_tile(j, carry):
        tile = s * SUB + j
        first_row = pl.multiple_of(j * TM, TM)

        @pl.when(valid_ref[tile] == 0)
        def _():
            out_ref[pl.ds(first_row * LANE_TILES, TM * LANE_TILES), :] = (
                jnp.zeros((TM * LANE_TILES, LANES), jnp.float32))

        @pl.when(valid_ref[tile] != 0)
        def _():
            x = _load_row_tiles(xs_ref, first_row, TM)
            xb = x.astype(jnp.bfloat16)
            wt = wt_ref[pl.ds(first_row, TM), :]
            y = jnp.zeros((TM, D_MODEL), jnp.float32)
            for e_ref, col in ((ea_ref, 0), (eb_ref, 1)):
                e = e_ref[tile]
                hg = jnp.dot(xb, wg_ref[e], preferred_element_type=jnp.float32)
                hu = jnp.dot(xb, wu_ref[e], preferred_element_type=jnp.float32)
                hid = jax.nn.silu(hg) * hu * wt[:, col:col + 1]
                y = y + jnp.dot(hid.astype(jnp.bfloat16), wd_ref[e],
                                preferred_element_type=jnp.float32)
            _store_row_tiles(out_ref, first_row,
                             _layer_norm(DEEPNORM_ALPHA * x + y, g2_ref[...], b2_ref[...]))

        return carry

    lax.fori_loop(0, SUB, sub_tile, 0)


def _expert_call(step_group, tile_ea, tile_eb, tile_valid, xs, wts, wg, wu, wd, g2, b2):
    rows = SUB * TM
    grid_spec = pltpu.PrefetchScalarGridSpec(
        num_scalar_prefetch=4,
        grid=(N_STEPS3,),
        in_specs=[
            pl.BlockSpec((rows * LANE_TILES, LANES), lambda s, grp, ea, eb, va: (s, 0)),
            pl.BlockSpec((rows, LANES), lambda s, grp, ea, eb, va: (s, 0)),
            pl.BlockSpec((None, EXPERTS_PER_GROUP, D_MODEL, D_EXPERT),
                         lambda s, grp, ea, eb, va: (grp[s], 0, 0, 0)),
            pl.BlockSpec((None, EXPERTS_PER_GROUP, D_MODEL, D_EXPERT),
                         lambda s, grp, ea, eb, va: (grp[s], 0, 0, 0)),
            pl.BlockSpec((None, EXPERTS_PER_GROUP, D_EXPERT, D_MODEL),
                         lambda s, grp, ea, eb, va: (grp[s], 0, 0, 0)),
            pl.BlockSpec((1, D_MODEL), lambda s, grp, ea, eb, va: (0, 0)),
            pl.BlockSpec((1, D_MODEL), lambda s, grp, ea, eb, va: (0, 0)),
        ],
        out_specs=pl.BlockSpec((rows * LANE_TILES, LANES), lambda s, grp, ea, eb, va: (s, 0)),
    )
    return pl.pallas_call(
        _expert_kernel,
        grid_spec=grid_spec,
        out_shape=jax.ShapeDtypeStruct((P_ROWS * LANE_TILES, LANES), jnp.float32),
        compiler_params=pltpu.CompilerParams(
            dimension_semantics=("arbitrary",), vmem_limit_bytes=VMEM_LIMIT),
        name="experts_ln",
    )(step_group, tile_ea, tile_eb, tile_valid, xs, wts, wg, wu, wd, g2, b2)


SC_CORES = 2
SC_SUBCORES = 16
SC_WORKERS = SC_CORES * SC_SUBCORES
DISPATCH_CHUNK = 40
COMBINE_CHUNK = 32


def _sc_gather_rows(table, idx, chunk):
    n_out, = idx.shape
    row_shape = table.shape[1:]
    per_w = n_out // SC_WORKERS
    n_chunks = per_w // chunk
    assert per_w * SC_WORKERS == n_out and n_chunks * chunk == per_w
    assert n_chunks % 2 == 0 and chunk % 8 == 0
    mesh = plsc.VectorSubcoreMesh(core_axis_name="c", subcore_axis_name="s")

    @functools.partial(
        pl.kernel, mesh=mesh,
        out_type=jax.ShapeDtypeStruct((n_out,) + row_shape, table.dtype),
        scratch_types=[
            pltpu.VMEM((per_w,), jnp.int32),
            pltpu.VMEM((2, chunk) + row_shape, table.dtype),
            pltpu.SemaphoreType.DMA((2,)),
            pltpu.SemaphoreType.DMA((2,)),
        ],
    )
    def gather_kernel(table_hbm, idx_hbm, out_hbm, idx_v, buf, gsem, wsem):
        wid = lax.axis_index("s") * SC_CORES + lax.axis_index("c")
        base = wid * per_w
        pltpu.sync_copy(idx_hbm.at[pl.ds(base, per_w)], idx_v)

        def gather(j, slot):
            rows = idx_v.at[pl.ds(j * chunk, chunk)]
            return pltpu.make_async_copy(table_hbm.at[rows], buf.at[slot], gsem.at[slot])

        def write(j, slot):
            dst = out_hbm.at[pl.ds(base + j * chunk, chunk)]
            return pltpu.make_async_copy(buf.at[slot], dst, wsem.at[slot])

        gather(0, 0).start()

        @pl.loop(0, n_chunks, step=2)
        def _(j0):
            for slot in range(2):
                j = j0 + slot
                gather(j, slot).wait()

                @pl.when(j >= 1)
                def _():
                    write(j - 1, 1 - slot).wait()

                @pl.when(j + 1 < n_chunks)
                def _():
                    gather(j + 1, 1 - slot).start()

                write(j, slot).start()

        write(n_chunks - 1, (n_chunks - 1) % 2).wait()

    return gather_kernel(table, idx)


def _block_diag(w):
    h, d, _ = w.shape
    eye = jnp.eye(h, dtype=w.dtype)
    return (eye[:, None, :, None] * w[:, :, None, :]).reshape(h * d, h * d)


def _pair_tables():
    pair_of = np.zeros((EXPERTS_PER_GROUP, EXPERTS_PER_GROUP), np.int32)
    lo = np.zeros((N_PAIRS,), np.int32)
    hi = np.zeros((N_PAIRS,), np.int32)
    p = 0
    for a in range(EXPERTS_PER_GROUP):
        for b in range(a + 1, EXPERTS_PER_GROUP):
            pair_of[a, b] = pair_of[b, a] = p
            lo[p], hi[p] = a, b
            p += 1
    return pair_of, lo, hi


def _route_kernel(lt_ref, info_ref, meta_ref, counts_ref, run_ref):
    step = pl.program_id(0)

    @pl.when(step == 0)
    def _():
        run_ref[...] = jnp.zeros_like(run_ref)

    f32 = jnp.float32
    sub8 = lax.broadcasted_iota(jnp.int32, (8, LANES), 0).astype(f32)
    row_id = lax.broadcasted_iota(jnp.int32, (LANES, LANES), 0)
    col_id = lax.broadcasted_iota(jnp.int32, (LANES, LANES), 1)
    prefix_mat = (row_id <= col_id).astype(jnp.bfloat16)
    ones_mat = jnp.ones((LANES, LANES), jnp.bfloat16)
    neg_inf = f32(-jnp.inf)

    def first_index_of_max(v):
        m = jnp.max(v, axis=0, keepdims=True)
        idx = jnp.min(jnp.where(v == m, sub8, f32(8)), axis=0, keepdims=True)
        return m, idx

    def lane_tile(k, run):
        lanes = pl.ds(pl.multiple_of(k * LANES, LANES), LANES)
        g = jnp.where(sub8 < N_GROUPS, lt_ref[0:GROUP_ROWS, lanes], neg_inf)
        g_max, g_idx = first_index_of_max(g)
        g_top_p = 1.0 / jnp.sum(jnp.exp(g - g_max), axis=0, keepdims=True)

        e_sel = lt_ref[GROUP_ROWS:GROUP_ROWS + EXPERTS_PER_GROUP, lanes]
        for grp in range(1, N_GROUPS):
            lo = GROUP_ROWS + grp * EXPERTS_PER_GROUP
            e_sel = jnp.where(g_idx == grp, lt_ref[lo:lo + EXPERTS_PER_GROUP, lanes], e_sel)
        m1, i1 = first_index_of_max(e_sel)
        rest = jnp.where(sub8 == i1, neg_inf, e_sel)
        m2 = jnp.max(rest, axis=0, keepdims=True)
        i2 = jnp.min(jnp.where((rest == m2) & (sub8 != i1), sub8, f32(8)), axis=0, keepdims=True)

        e = jnp.exp(m2 - m1)
        w1 = g_top_p / (1.0 + e)
        w2 = g_top_p * e / (1.0 + e)
        first_is_lo = i1 < i2
        w_lo = jnp.where(first_is_lo, w1, w2)
        w_hi = jnp.where(first_is_lo, w2, w1)
        lo_e = jnp.minimum(i1, i2)
        hi_e = jnp.maximum(i1, i2)
        pair = lo_e * (2 * EXPERTS_PER_GROUP - 1 - lo_e) * 0.5 + hi_e - lo_e - 1.0
        cls = (g_idx * N_PAIRS + pair).astype(jnp.int32)

        onehot = (row_id == cls).astype(jnp.bfloat16)
        csum = jnp.dot(onehot, prefix_mat, preferred_element_type=f32)
        rank = jnp.sum(onehot.astype(f32) * (csum + run), axis=0, keepdims=True) - 1.0
        run = run + jnp.dot(onehot, ones_mat, preferred_element_type=f32)

        info_ref[:, lanes] = jnp.where(sub8 == 0, cls, jnp.where(sub8 == 1, rank.astype(jnp.int32), 0))
        meta8 = jnp.where(sub8 == 0, w_lo, jnp.where(sub8 == 1, w_hi, 0.0))
        meta_t = jnp.concatenate([meta8, jnp.zeros((LANES - 8, LANES), f32)], axis=0)
        meta_ref[lanes, :] = meta_t.T
        return run

    run = lax.fori_loop(0, ROUTE_BLOCK // LANES, lane_tile, run_ref[...])
    run_ref[...] = run
    counts_ref[...] = run


def _route_call(lt):
    return pl.pallas_call(
        _route_kernel,
        grid=(N_TOKENS // ROUTE_BLOCK,),
        in_specs=[pl.BlockSpec((ROUTER_ROWS, ROUTE_BLOCK), lambda i: (0, i))],
        out_specs=[
            pl.BlockSpec((8, ROUTE_BLOCK), lambda i: (0, i)),
            pl.BlockSpec((ROUTE_BLOCK, LANES), lambda i: (i, 0)),
            pl.BlockSpec((LANES, LANES), lambda i: (0, 0)),
        ],
        out_shape=[
            jax.ShapeDtypeStruct((8, N_TOKENS), jnp.int32),
            jax.ShapeDtypeStruct((N_TOKENS, LANES), jnp.float32),
            jax.ShapeDtypeStruct((LANES, LANES), jnp.float32),
        ],
        scratch_shapes=[pltpu.VMEM((LANES, LANES), jnp.float32)],
        compiler_params=pltpu.CompilerParams(dimension_semantics=("arbitrary",)),
        name="route_rank",
    )(lt)


def _dispatch_plan(cls, rank, counts):
    _, pair_lo, pair_hi = _pair_tables()
    tiles_c = (counts + TM - 1) // TM
    tiles_g = tiles_c.reshape(N_GROUPS, N_PAIRS).sum(axis=1)
    tiles_g_pad = (tiles_g + SUB - 1) // SUB * SUB
    g_start = jnp.cumsum(tiles_g_pad) - tiles_g_pad
    tc = tiles_c.reshape(N_GROUPS, N_PAIRS)
    c_start = (g_start[:, None] + jnp.cumsum(tc, axis=1) - tc).reshape(N_CLASSES)
    pos = c_start[cls] * TM + rank

    tile_ids = jnp.arange(N_TILES, dtype=jnp.int32)
    c_end = c_start + tiles_c
    owner = jnp.sum((tile_ids[:, None] >= c_end[None, :]).astype(jnp.int32), axis=1)
    owner = jnp.minimum(owner, N_CLASSES - 1)
    valid = ((tile_ids >= c_start[owner]) & (tile_ids < c_end[owner])).astype(jnp.int32)
    pair = owner % N_PAIRS
    tile_ea = jnp.asarray(pair_lo)[pair]
    tile_eb = jnp.asarray(pair_hi)[pair]
    g_end = g_start + tiles_g_pad
    step_first = jnp.arange(N_STEPS3, dtype=jnp.int32) * SUB
    step_group = jnp.sum((step_first[:, None] >= g_end[None, :]).astype(jnp.int32), axis=1)
    step_group = jnp.minimum(step_group, N_GROUPS - 1)
    return pos.astype(jnp.int32), tile_ea, tile_eb, valid, step_group.astype(jnp.int32)


def kernel(x, w_in, rg_conv_w, rg_conv_b, rg_gate_a_w, rg_gate_a_b, rg_gate_x_w, rg_gate_x_b,
           rg_lambda, sc_conv_w, w_out, ln1_g, ln1_b, router_group_w, router_group_b,
           router_expert_w, router_expert_b, exp_w_gate, exp_w_up, exp_w_down, ln2_g, ln2_b):
    bf16 = jnp.bfloat16
    row = lambda v: v.reshape(1, -1)
    pad_g = GROUP_ROWS - N_GROUPS
    pad_e = ROUTER_LANES - ROUTER_ROWS
    wr = jnp.concatenate([router_group_w, jnp.zeros((D_MODEL, pad_g), jnp.float32),
                          router_expert_w, jnp.zeros((D_MODEL, pad_e), jnp.float32)], axis=1)
    br = jnp.concatenate([router_group_b, jnp.zeros((pad_g,), jnp.float32),
                          router_expert_b, jnp.zeros((pad_e,), jnp.float32)]).reshape(1, -1)
    wr_hi = wr.astype(bf16)
    wr_lo = (wr - wr_hi.astype(jnp.float32)).astype(bf16)

    h1, lt = _mixer_call(
        x, w_in.astype(bf16), rg_conv_w, row(rg_conv_b),
        _block_diag(rg_gate_a_w).astype(bf16), row(rg_gate_a_b),
        _block_diag(rg_gate_x_w).astype(bf16), row(rg_gate_x_b),
        row(rg_lambda), sc_conv_w, w_out.astype(bf16), row(ln1_g), row(ln1_b),
        jnp.concatenate([wr_hi, wr_lo], axis=1), br)

    info, meta, counts = _route_call(lt)
    cls, rank = info[0], info[1]
    pos, tile_ea, tile_eb, tile_valid, step_group = _dispatch_plan(
        cls, rank, counts[:N_CLASSES, 0].astype(jnp.int32))

    src = (jnp.arange(P_ROWS, dtype=jnp.int32) % N_TOKENS).at[pos].set(
        jnp.arange(N_TOKENS, dtype=jnp.int32))
    row_tiles = (-1, LANE_TILES, LANES)
    xs = _sc_gather_rows(h1.reshape(row_tiles), src, DISPATCH_CHUNK).reshape(-1, LANES)
    wts = _sc_gather_rows(meta, src, DISPATCH_CHUNK)

    grp_shape = (N_GROUPS, EXPERTS_PER_GROUP)
    ys = _expert_call(
        step_group, tile_ea, tile_eb, tile_valid, xs, wts,
        exp_w_gate.astype(bf16).reshape(grp_shape + (D_MODEL, D_EXPERT)),
        exp_w_up.astype(bf16).reshape(grp_shape + (D_MODEL, D_EXPERT)),
        exp_w_down.astype(bf16).reshape(grp_shape + (D_EXPERT, D_MODEL)),
        row(ln2_g), row(ln2_b))

    out_rows = pos.reshape(SEQ, BATCH).T.reshape(-1)
    out = _sc_gather_rows(ys.reshape(row_tiles), out_rows, COMBINE_CHUNK)
    return out.reshape(BATCH, SEQ, D_MODEL)
```

```python
import functools

import jax
import jax.numpy as jnp
import numpy as np
from jax import lax
from jax.experimental import pallas as pl
from jax.experimental.pallas import tpu as pltpu
from jax.experimental.pallas import tpu_sc as plsc

D_MODEL = 1024
BATCH = 16
SEQ = 2048
D_RG = 512
D_SC = 512
RG_HEADS = 8
RG_HEAD_DIM = D_RG // RG_HEADS
RG_CONV = 4
RG_C = 8.0
SC_CONV = 3
D_IN_PROJ = 2 * D_RG + 3 * D_SC
N_GROUPS = 4
EXPERTS_PER_GROUP = 8
N_EXPERTS = N_GROUPS * EXPERTS_PER_GROUP
D_EXPERT = D_MODEL // 4
LN_EPS = 1e-5
DEEPNORM_ALPHA = 2.0 ** 0.25

N_TOKENS = BATCH * SEQ
LANES = 128
LANE_TILES = D_MODEL // LANES
ROUTER_LANES = LANES
GROUP_ROWS = 8
ROUTER_ROWS = GROUP_ROWS + N_EXPERTS

TS = 64
M1 = TS * BATCH

ROUTE_BLOCK = 4096

TM = 128
SUB = 4
N_PAIRS = EXPERTS_PER_GROUP * (EXPERTS_PER_GROUP - 1) // 2
N_CLASSES = N_GROUPS * N_PAIRS
N_TILES = -(-(N_TOKENS // TM + N_CLASSES + N_GROUPS * (SUB - 1)) // SUB) * SUB
N_STEPS3 = N_TILES // SUB
P_ROWS = N_TILES * TM

VMEM_LIMIT = 56 * 1024 * 1024


def _layer_norm(z, g, b):
    mu = jnp.mean(z, axis=-1, keepdims=True)
    zc = z - mu
    var = jnp.mean(zc * zc, axis=-1, keepdims=True)
    return zc * lax.rsqrt(var + LN_EPS) * g + b


def _store_row_tiles(ref, first_row, value):
    n = value.shape[0]
    for j in range(LANE_TILES):
        ref[pl.ds(first_row * LANE_TILES + j, n, stride=LANE_TILES), :] = (
            value[:, j * LANES:(j + 1) * LANES])


def _load_row_tiles(ref, first_row, n):
    return jnp.concatenate(
        [ref[pl.ds(first_row * LANE_TILES + j, n, stride=LANE_TILES), :]
         for j in range(LANE_TILES)], axis=1)


def _mixer_kernel(x_ref, w_in_ref, cw_ref, cb_ref, wa_ref, ba_ref, wx_ref, bx_ref, lam_ref,
                  scw_ref, w_out_ref, g1_ref, b1_ref, wr_ref, br_ref,
                  h1_ref, logit_ref,
                  xt_ref, cbuf_ref, sbuf_ref, a_ref, u_ref, hstate_ref):
    c = pl.program_id(0)
    rg_pad = (RG_CONV - 1) * BATCH
    sc_pad = (SC_CONV - 1) * BATCH

    @pl.when(c == 0)
    def _():
        cbuf_ref[0:rg_pad, :] = jnp.zeros((rg_pad, D_RG), jnp.float32)
        sbuf_ref[0:sc_pad, :] = jnp.zeros((sc_pad, D_SC), jnp.float32)
        hstate_ref[...] = jnp.zeros_like(hstate_ref)

    for t in range(TS):
        xt_ref[t * BATCH:(t + 1) * BATCH, :] = x_ref[:, t, :]

    xb = xt_ref[...].astype(jnp.bfloat16)

    def proj(lo, width):
        return jnp.dot(xb, w_in_ref[:, lo:lo + width], preferred_element_type=jnp.float32)

    cbuf_ref[rg_pad:rg_pad + M1, :] = proj(0, D_RG)
    xc = cb_ref[...] + cw_ref[0:1, :] * cbuf_ref[0:M1, :]
    for k in range(1, RG_CONV):
        xc = xc + cw_ref[k:k + 1, :] * cbuf_ref[k * BATCH:k * BATCH + M1, :]
    cbuf_ref[0:rg_pad, :] = cbuf_ref[M1:M1 + rg_pad, :]

    xcb = xc.astype(jnp.bfloat16)
    r = jax.nn.sigmoid(jnp.dot(xcb, wa_ref[...], preferred_element_type=jnp.float32) + ba_ref[...])
    i = jax.nn.sigmoid(jnp.dot(xcb, wx_ref[...], preferred_element_type=jnp.float32) + bx_ref[...])
    log_a = (-RG_C) * r * jax.nn.softplus(-lam_ref[...])
    a_ref[...] = jnp.exp(log_a)
    th = jnp.tanh(log_a)
    u_ref[...] = jnp.sqrt(-2.0 * th / (1.0 - th)) * (i * xc)

    def scan_body(t, h):
        rows = pl.ds(pl.multiple_of(t * BATCH, BATCH), BATCH)
        h = a_ref[rows, :] * h + u_ref[rows, :]
        u_ref[rows, :] = h
        return h

    hstate_ref[...] = lax.fori_loop(0, TS, scan_body, hstate_ref[...], unroll=8)

    y_rg = u_ref[...] * jax.nn.gelu(proj(D_RG, D_RG))

    sc_b = proj(2 * D_RG, D_SC)
    sbuf_ref[sc_pad:sc_pad + M1, :] = proj(2 * D_RG + D_SC, D_SC) * proj(2 * D_RG + 2 * D_SC, D_SC)
    conv = scw_ref[0:1, :] * sbuf_ref[0:M1, :]
    for k in range(1, SC_CONV):
        conv = conv + scw_ref[k:k + 1, :] * sbuf_ref[k * BATCH:k * BATCH + M1, :]
    sbuf_ref[0:sc_pad, :] = sbuf_ref[M1:M1 + sc_pad, :]
    y_sc = sc_b * conv

    mix = jnp.dot(y_rg.astype(jnp.bfloat16), w_out_ref[0:D_RG, :], preferred_element_type=jnp.float32)
    mix = mix + jnp.dot(y_sc.astype(jnp.bfloat16), w_out_ref[D_RG:, :], preferred_element_type=jnp.float32)

    h1 = _layer_norm(DEEPNORM_ALPHA * xt_ref[...] + mix, g1_ref[...], b1_ref[...])
    _store_row_tiles(h1_ref, 0, h1)
    h_hi = h1.astype(jnp.bfloat16)
    h_lo = (h1 - h_hi.astype(jnp.float32)).astype(jnp.bfloat16)
    both = jnp.dot(h_hi, wr_ref[...], preferred_element_type=jnp.float32)
    logits = both[:, :ROUTER_LANES] + both[:, ROUTER_LANES:] + br_ref[...]
    logits = logits + jnp.dot(h_lo, wr_ref[:, :ROUTER_LANES], preferred_element_type=jnp.float32)
    logit_ref[...] = logits.T[:ROUTER_ROWS, :]


def _const_spec(shape):
    return pl.BlockSpec(shape, lambda c: (0,) * len(shape))


def _mixer_call(x, w_in, cw, cb, wa, ba, wx, bx, lam, scw, w_out, g1, b1, wr, br):
    n_chunks = SEQ // TS
    in_specs = [
        pl.BlockSpec((BATCH, TS, D_MODEL), lambda c: (0, c, 0)),
        _const_spec((D_MODEL, D_IN_PROJ)),
        _const_spec((RG_CONV, D_RG)), _const_spec((1, D_RG)),
        _const_spec((D_RG, D_RG)), _const_spec((1, D_RG)),
        _const_spec((D_RG, D_RG)), _const_spec((1, D_RG)),
        _const_spec((1, D_RG)),
        _const_spec((SC_CONV, D_SC)),
        _const_spec((D_MODEL, D_MODEL)),
        _const_spec((1, D_MODEL)), _const_spec((1, D_MODEL)),
        _const_spec((D_MODEL, 2 * ROUTER_LANES)), _const_spec((1, ROUTER_LANES)),
    ]
    out_specs = [
        pl.BlockSpec((M1 * LANE_TILES, LANES), lambda c: (c, 0)),
        pl.BlockSpec((ROUTER_ROWS, M1), lambda c: (0, c)),
    ]
    return pl.pallas_call(
        _mixer_kernel,
        grid=(n_chunks,),
        in_specs=in_specs,
        out_specs=out_specs,
        out_shape=[
            jax.ShapeDtypeStruct((N_TOKENS * LANE_TILES, LANES), jnp.float32),
            jax.ShapeDtypeStruct((ROUTER_ROWS, N_TOKENS), jnp.float32),
        ],
        scratch_shapes=[
            pltpu.VMEM((M1, D_MODEL), jnp.float32),
            pltpu.VMEM((M1 + (RG_CONV - 1) * BATCH, D_RG), jnp.float32),
            pltpu.VMEM((M1 + (SC_CONV - 1) * BATCH, D_SC), jnp.float32),
            pltpu.VMEM((M1, D_RG), jnp.float32),
            pltpu.VMEM((M1, D_RG), jnp.float32),
            pltpu.VMEM((BATCH, D_RG), jnp.float32),
        ],
        compiler_params=pltpu.CompilerParams(
            dimension_semantics=("arbitrary",), vmem_limit_bytes=VMEM_LIMIT),
        name="mixer_ln_router",
    )(x, w_in, cw, cb, wa, ba, wx, bx, lam, scw, w_out, g1, b1, wr, br)


def _expert_kernel(grp_ref, ea_ref, eb_ref, rows_ref,
                   xs_ref, wt_ref, wg_ref, wu_ref, wd_ref, g2_ref, b2_ref, out_ref):
    s = pl.program_id(0)

    def sub_tile(j, carry):
        tile = s * SUB + j
        first_row = pl.multiple_of(j * TM, TM)

        n_rows = rows_ref[tile]

        @pl.when(n_rows == 0)
        def _():
            out_ref[pl.ds(first_row * LANE_TILES, TM * LANE_TILES), :] = (
                jnp.zeros((TM * LANE_TILES, LANES), jnp.float32))

        @pl.when(n_rows != 0)
        def _():
            in_use = lax.broadcasted_iota(jnp.int32, (TM, 1), 0) < n_rows
            x = jnp.where(in_use, _load_row_tiles(xs_ref, first_row, TM), 0.0)
            xb = x.astype(jnp.bfloat16)
            wt = jnp.where(in_use, wt_ref[pl.ds(first_row, TM), :], 0.0)
            y = jnp.zeros((TM, D_MODEL), jnp.float32)
            for e_ref, col in ((ea_ref, 0), (eb_ref, 1)):
                e = e_ref[tile]
                hg = jnp.dot(xb, wg_ref[e], preferred_element_type=jnp.float32)
                hu = jnp.dot(xb, wu_ref[e], preferred_element_type=jnp.float32)
                hid = jax.nn.silu(hg) * hu * wt[:, col:col + 1]
                y = y + jnp.dot(hid.astype(jnp.bfloat16), wd_ref[e],
                                preferred_element_type=jnp.float32)
            _store_row_tiles(out_ref, first_row,
                             _layer_norm(DEEPNORM_ALPHA * x + y, g2_ref[...], b2_ref[...]))

        return carry

    lax.fori_loop(0, SUB, sub_tile, 0)


def _expert_call(step_group, tile_ea, tile_eb, tile_rows, xs, wts, wg, wu, wd, g2, b2):
    rows = SUB * TM
    grid_spec = pltpu.PrefetchScalarGridSpec(
        num_scalar_prefetch=4,
        grid=(N_STEPS3,),
        in_specs=[
            pl.BlockSpec((rows * LANE_TILES, LANES), lambda s, grp, ea, eb, va: (s, 0)),
            pl.BlockSpec((rows, LANES), lambda s, grp, ea, eb, va: (s, 0)),
            pl.BlockSpec((None, EXPERTS_PER_GROUP, D_MODEL, D_EXPERT),
                         lambda s, grp, ea, eb, va: (grp[s], 0, 0, 0)),
            pl.BlockSpec((None, EXPERTS_PER_GROUP, D_MODEL, D_EXPERT),
                         lambda s, grp, ea, eb, va: (grp[s], 0, 0, 0)),
            pl.BlockSpec((None, EXPERTS_PER_GROUP, D_EXPERT, D_MODEL),
                         lambda s, grp, ea, eb, va: (grp[s], 0, 0, 0)),
            pl.BlockSpec((1, D_MODEL), lambda s, grp, ea, eb, va: (0, 0)),
            pl.BlockSpec((1, D_MODEL), lambda s, grp, ea, eb, va: (0, 0)),
        ],
        out_specs=pl.BlockSpec((rows * LANE_TILES, LANES), lambda s, grp, ea, eb, va: (s, 0)),
    )
    return pl.pallas_call(
        _expert_kernel,
        grid_spec=grid_spec,
        out_shape=jax.ShapeDtypeStruct((P_ROWS * LANE_TILES, LANES), jnp.float32),
        compiler_params=pltpu.CompilerParams(
            dimension_semantics=("arbitrary",), vmem_limit_bytes=VMEM_LIMIT),
        name="experts_ln",
    )(step_group, tile_ea, tile_eb, tile_rows, xs, wts, wg, wu, wd, g2, b2)


SC_CORES = 2
SC_SUBCORES = 16
SC_WORKERS = SC_CORES * SC_SUBCORES
COMBINE_CHUNK = 32


def _sc_gather_rows(table, idx, chunk):
    n_out, = idx.shape
    row_shape = table.shape[1:]
    per_w = n_out // SC_WORKERS
    n_chunks = per_w // chunk
    assert per_w * SC_WORKERS == n_out and n_chunks * chunk == per_w
    assert n_chunks % 2 == 0 and chunk % 8 == 0
    mesh = plsc.VectorSubcoreMesh(core_axis_name="c", subcore_axis_name="s")

    @functools.partial(
        pl.kernel, mesh=mesh,
        out_type=jax.ShapeDtypeStruct((n_out,) + row_shape, table.dtype),
        scratch_types=[
            pltpu.VMEM((per_w,), jnp.int32),
            pltpu.VMEM((2, chunk) + row_shape, table.dtype),
            pltpu.SemaphoreType.DMA((2,)),
            pltpu.SemaphoreType.DMA((2,)),
        ],
    )
    def gather_kernel(table_hbm, idx_hbm, out_hbm, idx_v, buf, gsem, wsem):
        wid = lax.axis_index("s") * SC_CORES + lax.axis_index("c")
        base = wid * per_w
        pltpu.sync_copy(idx_hbm.at[pl.ds(base, per_w)], idx_v)

        def gather(j, slot):
            rows = idx_v.at[pl.ds(j * chunk, chunk)]
            return pltpu.make_async_copy(table_hbm.at[rows], buf.at[slot], gsem.at[slot])

        def write(j, slot):
            dst = out_hbm.at[pl.ds(base + j * chunk, chunk)]
            return pltpu.make_async_copy(buf.at[slot], dst, wsem.at[slot])

        gather(0, 0).start()

        @pl.loop(0, n_chunks, step=2)
        def _(j0):
            for slot in range(2):
                j = j0 + slot
                gather(j, slot).wait()

                @pl.when(j >= 1)
                def _():
                    write(j - 1, 1 - slot).wait()

                @pl.when(j + 1 < n_chunks)
                def _():
                    gather(j + 1, 1 - slot).start()

                write(j, slot).start()

        write(n_chunks - 1, (n_chunks - 1) % 2).wait()

    return gather_kernel(table, idx)


SC_LANES = 16


def _sc_dispatch(rows, meta, cls, rank, class_start):
    n_rows = rows.shape[0]
    chunk = COMBINE_CHUNK
    per_w = n_rows // SC_WORKERS
    n_chunks = per_w // chunk
    assert per_w * SC_WORKERS == n_rows and n_chunks * chunk == per_w
    assert n_chunks % 2 == 0 and chunk % SC_LANES == 0
    mesh = plsc.VectorSubcoreMesh(core_axis_name="c", subcore_axis_name="s")

    @functools.partial(
        pl.kernel, mesh=mesh,
        out_type=[
            jax.ShapeDtypeStruct((P_ROWS,) + rows.shape[1:], rows.dtype),
            jax.ShapeDtypeStruct((P_ROWS,) + meta.shape[1:], meta.dtype),
            jax.ShapeDtypeStruct((n_rows,), jnp.int32),
        ],
        scratch_types=[
            pltpu.VMEM((per_w,), jnp.int32),
            pltpu.VMEM((per_w,), jnp.int32),
            pltpu.VMEM((LANES,), jnp.int32),
            pltpu.VMEM((per_w,), jnp.int32),
            pltpu.VMEM((n_chunks, chunk), jnp.int32),
            pltpu.VMEM((2, chunk) + rows.shape[1:], rows.dtype),
            pltpu.VMEM((2, chunk) + meta.shape[1:], meta.dtype),
            pltpu.SemaphoreType.DMA((2,)),
            pltpu.SemaphoreType.DMA((2,)),
            pltpu.SemaphoreType.DMA((2,)),
            pltpu.SemaphoreType.DMA((2,)),
        ],
        compiler_params=pltpu.CompilerParams(needs_layout_passes=False),
    )
    def dispatch_kernel(rows_hbm, meta_hbm, cls_hbm, rank_hbm, start_hbm,
                        xs_hbm, ws_hbm, pos_hbm,
                        cls_v, rank_v, start_v, pos_flat, pos_v, rbuf, mbuf,
                        rsem, msem, xsem, wsem):
        wid = lax.axis_index("s") * SC_CORES + lax.axis_index("c")
        base = wid * per_w
        pltpu.sync_copy(cls_hbm.at[pl.ds(base, per_w)], cls_v)
        pltpu.sync_copy(rank_hbm.at[pl.ds(base, per_w)], rank_v)
        pltpu.sync_copy(start_hbm, start_v)

        @pl.loop(0, n_chunks)
        def _(j):
            for k in range(chunk // SC_LANES):
                off = j * chunk + k * SC_LANES
                c = cls_v[pl.ds(off, SC_LANES)]
                pos = plsc.load_gather(start_v, [c]) + rank_v[pl.ds(off, SC_LANES)]
                pos_flat[pl.ds(off, SC_LANES)] = pos
                pos_v[j, pl.ds(k * SC_LANES, SC_LANES)] = pos

        pltpu.sync_copy(pos_flat, pos_hbm.at[pl.ds(base, per_w)])

        def read_rows(j, slot):
            src = rows_hbm.at[pl.ds(base + j * chunk, chunk)]
            return pltpu.make_async_copy(src, rbuf.at[slot], rsem.at[slot])

        def read_meta(j, slot):
            src = meta_hbm.at[pl.ds(base + j * chunk, chunk)]
            return pltpu.make_async_copy(src, mbuf.at[slot], msem.at[slot])

        def put_rows(j, slot):
            return pltpu.make_async_copy(rbuf.at[slot], xs_hbm.at[pos_v.at[j]], xsem.at[slot])

        def put_meta(j, slot):
            return pltpu.make_async_copy(mbuf.at[slot], ws_hbm.at[pos_v.at[j]], wsem.at[slot])

        read_rows(0, 0).start()
        read_meta(0, 0).start()

        @pl.loop(0, n_chunks, step=2)
        def _(j0):
            for slot in range(2):
                j = j0 + slot
                read_rows(j, slot).wait()
                read_meta(j, slot).wait()

                @pl.when(j >= 1)
                def _():
                    put_rows(j - 1, 1 - slot).wait()
                    put_meta(j - 1, 1 - slot).wait()

                @pl.when(j + 1 < n_chunks)
                def _():
                    read_rows(j + 1, 1 - slot).start()
                    read_meta(j + 1, 1 - slot).start()

                put_rows(j, slot).start()
                put_meta(j, slot).start()

        put_rows(n_chunks - 1, (n_chunks - 1) % 2).wait()
        put_meta(n_chunks - 1, (n_chunks - 1) % 2).wait()

    return dispatch_kernel(rows, meta, cls, rank, class_start)


def _block_diag(w):
    h, d, _ = w.shape
    eye = jnp.eye(h, dtype=w.dtype)
    return (eye[:, None, :, None] * w[:, :, None, :]).reshape(h * d, h * d)


def _pair_tables():
    pair_of = np.zeros((EXPERTS_PER_GROUP, EXPERTS_PER_GROUP), np.int32)
    lo = np.zeros((N_PAIRS,), np.int32)
    hi = np.zeros((N_PAIRS,), np.int32)
    p = 0
    for a in range(EXPERTS_PER_GROUP):
        for b in range(a + 1, EXPERTS_PER_GROUP):
            pair_of[a, b] = pair_of[b, a] = p
            lo[p], hi[p] = a, b
            p += 1
    return pair_of, lo, hi


def _route_kernel(lt_ref, info_ref, meta_ref, counts_ref, run_ref):
    step = pl.program_id(0)

    @pl.when(step == 0)
    def _():
        run_ref[...] = jnp.zeros_like(run_ref)

    f32 = jnp.float32
    sub8 = lax.broadcasted_iota(jnp.int32, (8, LANES), 0).astype(f32)
    row_id = lax.broadcasted_iota(jnp.int32, (LANES, LANES), 0)
    col_id = lax.broadcasted_iota(jnp.int32, (LANES, LANES), 1)
    prefix_mat = (row_id <= col_id).astype(jnp.bfloat16)
    ones_mat = jnp.ones((LANES, LANES), jnp.bfloat16)
    neg_inf = f32(-jnp.inf)

    def first_index_of_max(v):
        m = jnp.max(v, axis=0, keepdims=True)
        idx = jnp.min(jnp.where(v == m, sub8, f32(8)), axis=0, keepdims=True)
        return m, idx

    def lane_tile(k, run):
        lanes = pl.ds(pl.multiple_of(k * LANES, LANES), LANES)
        g = jnp.where(sub8 < N_GROUPS, lt_ref[0:GROUP_ROWS, lanes], neg_inf)
        g_max, g_idx = first_index_of_max(g)
        g_top_p = 1.0 / jnp.sum(jnp.exp(g - g_max), axis=0, keepdims=True)

        e_sel = lt_ref[GROUP_ROWS:GROUP_ROWS + EXPERTS_PER_GROUP, lanes]
        for grp in range(1, N_GROUPS):
            lo = GROUP_ROWS + grp * EXPERTS_PER_GROUP
            e_sel = jnp.where(g_idx == grp, lt_ref[lo:lo + EXPERTS_PER_GROUP, lanes], e_sel)
        m1, i1 = first_index_of_max(e_sel)
        rest = jnp.where(sub8 == i1, neg_inf, e_sel)
        m2 = jnp.max(rest, axis=0, keepdims=True)
        i2 = jnp.min(jnp.where((rest == m2) & (sub8 != i1), sub8, f32(8)), axis=0, keepdims=True)

        e = jnp.exp(m2 - m1)
        w1 = g_top_p / (1.0 + e)
        w2 = g_top_p * e / (1.0 + e)
        first_is_lo = i1 < i2
        w_lo = jnp.where(first_is_lo, w1, w2)
        w_hi = jnp.where(first_is_lo, w2, w1)
        lo_e = jnp.minimum(i1, i2)
        hi_e = jnp.maximum(i1, i2)
        pair = lo_e * (2 * EXPERTS_PER_GROUP - 1 - lo_e) * 0.5 + hi_e - lo_e - 1.0
        cls = (g_idx * N_PAIRS + pair).astype(jnp.int32)

        onehot = (row_id == cls).astype(jnp.bfloat16)
        csum = jnp.dot(onehot, prefix_mat, preferred_element_type=f32)
        rank = jnp.sum(onehot.astype(f32) * (csum + run), axis=0, keepdims=True) - 1.0
        run = run + jnp.dot(onehot, ones_mat, preferred_element_type=f32)

        info_ref[:, lanes] = jnp.where(sub8 == 0, cls, jnp.where(sub8 == 1, rank.astype(jnp.int32), 0))
        meta8 = jnp.where(sub8 == 0, w_lo, jnp.where(sub8 == 1, w_hi, 0.0))
        meta_t = jnp.concatenate([meta8, jnp.zeros((LANES - 8, LANES), f32)], axis=0)
        meta_ref[lanes, :] = meta_t.T
        return run

    run = lax.fori_loop(0, ROUTE_BLOCK // LANES, lane_tile, run_ref[...])
    run_ref[...] = run
    counts_ref[...] = run


def _route_call(lt):
    return pl.pallas_call(
        _route_kernel,
        grid=(N_TOKENS // ROUTE_BLOCK,),
        in_specs=[pl.BlockSpec((ROUTER_ROWS, ROUTE_BLOCK), lambda i: (0, i))],
        out_specs=[
            pl.BlockSpec((8, ROUTE_BLOCK), lambda i: (0, i)),
            pl.BlockSpec((ROUTE_BLOCK, LANES), lambda i: (i, 0)),
            pl.BlockSpec((LANES, LANES), lambda i: (0, 0)),
        ],
        out_shape=[
            jax.ShapeDtypeStruct((8, N_TOKENS), jnp.int32),
            jax.ShapeDtypeStruct((N_TOKENS, LANES), jnp.float32),
            jax.ShapeDtypeStruct((LANES, LANES), jnp.float32),
        ],
        scratch_shapes=[pltpu.VMEM((LANES, LANES), jnp.float32)],
        compiler_params=pltpu.CompilerParams(dimension_semantics=("arbitrary",)),
        name="route_rank",
    )(lt)


def _dispatch_plan(counts):
    _, pair_lo, pair_hi = _pair_tables()
    tiles_c = (counts + TM - 1) // TM
    tiles_g = tiles_c.reshape(N_GROUPS, N_PAIRS).sum(axis=1)
    tiles_g_pad = (tiles_g + SUB - 1) // SUB * SUB
    g_start = jnp.cumsum(tiles_g_pad) - tiles_g_pad
    tc = tiles_c.reshape(N_GROUPS, N_PAIRS)
    c_start = (g_start[:, None] + jnp.cumsum(tc, axis=1) - tc).reshape(N_CLASSES)
    class_start = jnp.zeros((LANES,), jnp.int32).at[:N_CLASSES].set(c_start * TM)

    tile_ids = jnp.arange(N_TILES, dtype=jnp.int32)
    c_end = c_start + tiles_c
    owner = jnp.sum((tile_ids[:, None] >= c_end[None, :]).astype(jnp.int32), axis=1)
    owner = jnp.minimum(owner, N_CLASSES - 1)
    in_class = (tile_ids >= c_start[owner]) & (tile_ids < c_end[owner])
    rows_left = counts[owner] - (tile_ids - c_start[owner]) * TM
    tile_rows = jnp.where(in_class, jnp.minimum(rows_left, TM), 0).astype(jnp.int32)
    pair = owner % N_PAIRS
    tile_ea = jnp.asarray(pair_lo)[pair]
    tile_eb = jnp.asarray(pair_hi)[pair]
    g_end = g_start + tiles_g_pad
    step_first = jnp.arange(N_STEPS3, dtype=jnp.int32) * SUB
    step_group = jnp.sum((step_first[:, None] >= g_end[None, :]).astype(jnp.int32), axis=1)
    step_group = jnp.minimum(step_group, N_GROUPS - 1)
    return class_start, tile_ea, tile_eb, tile_rows, step_group.astype(jnp.int32)


def kernel(x, w_in, rg_conv_w, rg_conv_b, rg_gate_a_w, rg_gate_a_b, rg_gate_x_w, rg_gate_x_b,
           rg_lambda, sc_conv_w, w_out, ln1_g, ln1_b, router_group_w, router_group_b,
           router_expert_w, router_expert_b, exp_w_gate, exp_w_up, exp_w_down, ln2_g, ln2_b):
    bf16 = jnp.bfloat16
    row = lambda v: v.reshape(1, -1)
    pad_g = GROUP_ROWS - N_GROUPS
    pad_e = ROUTER_LANES - ROUTER_ROWS
    wr = jnp.concatenate([router_group_w, jnp.zeros((D_MODEL, pad_g), jnp.float32),
                          router_expert_w, jnp.zeros((D_MODEL, pad_e), jnp.float32)], axis=1)
    br = jnp.concatenate([router_group_b, jnp.zeros((pad_g,), jnp.float32),
                          router_expert_b, jnp.zeros((pad_e,), jnp.float32)]).reshape(1, -1)
    wr_hi = wr.astype(bf16)
    wr_lo = (wr - wr_hi.astype(jnp.float32)).astype(bf16)

    h1, lt = _mixer_call(
        x, w_in.astype(bf16), rg_conv_w, row(rg_conv_b),
        _block_diag(rg_gate_a_w).astype(bf16), row(rg_gate_a_b),
        _block_diag(rg_gate_x_w).astype(bf16), row(rg_gate_x_b),
        row(rg_lambda), sc_conv_w, w_out.astype(bf16), row(ln1_g), row(ln1_b),
        jnp.concatenate([wr_hi, wr_lo], axis=1), br)

    info, meta, counts = _route_call(lt)
    class_start, tile_ea, tile_eb, tile_rows, step_group = _dispatch_plan(
        counts[:N_CLASSES, 0].astype(jnp.int32))

    row_tiles = (-1, LANE_TILES, LANES)
    xs, wts, pos = _sc_dispatch(h1.reshape(row_tiles), meta, info[0], info[1], class_start)

    grp_shape = (N_GROUPS, EXPERTS_PER_GROUP)
    ys = _expert_call(
        step_group, tile_ea, tile_eb, tile_rows, xs.reshape(-1, LANES), wts,
        exp_w_gate.astype(bf16).reshape(grp_shape + (D_MODEL, D_EXPERT)),
        exp_w_up.astype(bf16).reshape(grp_shape + (D_MODEL, D_EXPERT)),
        exp_w_down.astype(bf16).reshape(grp_shape + (D_EXPERT, D_MODEL)),
        row(ln2_g), row(ln2_b))

    out_rows = pos.reshape(SEQ, BATCH).T.reshape(-1)
    out = _sc_gather_rows(ys.reshape(row_tiles), out_rows, COMBINE_CHUNK)
    return out.reshape(BATCH, SEQ, D_MODEL)
```

```python
import functools

import jax
import jax.numpy as jnp
import numpy as np
from jax import lax
from jax.experimental import pallas as pl
from jax.experimental.pallas import tpu as pltpu
from jax.experimental.pallas import tpu_sc as plsc

D_MODEL = 1024
BATCH = 16
SEQ = 2048
D_RG = 512
D_SC = 512
RG_HEADS = 8
RG_HEAD_DIM = D_RG // RG_HEADS
RG_CONV = 4
RG_C = 8.0
SC_CONV = 3
D_IN_PROJ = 2 * D_RG + 3 * D_SC
N_GROUPS = 4
EXPERTS_PER_GROUP = 8
N_EXPERTS = N_GROUPS * EXPERTS_PER_GROUP
D_EXPERT = D_MODEL // 4
LN_EPS = 1e-5
DEEPNORM_ALPHA = 2.0 ** 0.25

N_TOKENS = BATCH * SEQ
LANES = 128
LANE_TILES = D_MODEL // LANES
ROUTER_LANES = LANES
GROUP_ROWS = 8
ROUTER_ROWS = GROUP_ROWS + N_EXPERTS

TS = 64
M1 = TS * BATCH

ROUTE_BLOCK = 4096

TM = 128
SUB = 4
N_PAIRS = EXPERTS_PER_GROUP * (EXPERTS_PER_GROUP - 1) // 2
N_CLASSES = N_GROUPS * N_PAIRS
N_TILES = -(-(N_TOKENS // TM + N_CLASSES + N_GROUPS * (SUB - 1)) // SUB) * SUB
N_STEPS3 = N_TILES // SUB
P_ROWS = N_TILES * TM

VMEM_LIMIT = 56 * 1024 * 1024


def _layer_norm(z, g, b):
    mu = jnp.mean(z, axis=-1, keepdims=True)
    zc = z - mu
    var = jnp.mean(zc * zc, axis=-1, keepdims=True)
    return zc * lax.rsqrt(var + LN_EPS) * g + b


def _store_row_tiles(ref, first_row, value):
    n = value.shape[0]
    for j in range(LANE_TILES):
        ref[pl.ds(first_row * LANE_TILES + j, n, stride=LANE_TILES), :] = (
            value[:, j * LANES:(j + 1) * LANES])


def _load_row_tiles(ref, first_row, n):
    return jnp.concatenate(
        [ref[pl.ds(first_row * LANE_TILES + j, n, stride=LANE_TILES), :]
         for j in range(LANE_TILES)], axis=1)


def _mixer_kernel(x_ref, w_in_ref, cw_ref, cb_ref, wa_ref, ba_ref, wx_ref, bx_ref, lam_ref,
                  scw_ref, w_out_ref, g1_ref, b1_ref, wr_ref, br_ref,
                  h1_ref, logit_ref,
                  xt_ref, cbuf_ref, sbuf_ref, a_ref, u_ref, hstate_ref):
    c = pl.program_id(0)
    rg_pad = (RG_CONV - 1) * BATCH
    sc_pad = (SC_CONV - 1) * BATCH

    @pl.when(c == 0)
    def _():
        cbuf_ref[0:rg_pad, :] = jnp.zeros((rg_pad, D_RG), jnp.float32)
        sbuf_ref[0:sc_pad, :] = jnp.zeros((sc_pad, D_SC), jnp.float32)
        hstate_ref[...] = jnp.zeros_like(hstate_ref)

    for t in range(TS):
        xt_ref[t * BATCH:(t + 1) * BATCH, :] = x_ref[:, t, :]

    xb = xt_ref[...].astype(jnp.bfloat16)

    def proj(lo, width):
        return jnp.dot(xb, w_in_ref[:, lo:lo + width], preferred_element_type=jnp.float32)

    cbuf_ref[rg_pad:rg_pad + M1, :] = proj(0, D_RG)
    xc = cb_ref[...] + cw_ref[0:1, :] * cbuf_ref[0:M1, :]
    for k in range(1, RG_CONV):
        xc = xc + cw_ref[k:k + 1, :] * cbuf_ref[k * BATCH:k * BATCH + M1, :]
    cbuf_ref[0:rg_pad, :] = cbuf_ref[M1:M1 + rg_pad, :]

    xcb = xc.astype(jnp.bfloat16)
    r = jax.nn.sigmoid(jnp.dot(xcb, wa_ref[...], preferred_element_type=jnp.float32) + ba_ref[...])
    i = jax.nn.sigmoid(jnp.dot(xcb, wx_ref[...], preferred_element_type=jnp.float32) + bx_ref[...])
    log_a = (-RG_C) * r * jax.nn.softplus(-lam_ref[...])
    a_ref[...] = jnp.exp(log_a)
    th = jnp.tanh(log_a)
    u_ref[...] = jnp.sqrt(-2.0 * th / (1.0 - th)) * (i * xc)

    def scan_body(t, h):
        rows = pl.ds(pl.multiple_of(t * BATCH, BATCH), BATCH)
        h = a_ref[rows, :] * h + u_ref[rows, :]
        u_ref[rows, :] = h
        return h

    hstate_ref[...] = lax.fori_loop(0, TS, scan_body, hstate_ref[...], unroll=8)

    y_rg = u_ref[...] * jax.nn.gelu(proj(D_RG, D_RG))

    sc_b = proj(2 * D_RG, D_SC)
    sbuf_ref[sc_pad:sc_pad + M1, :] = proj(2 * D_RG + D_SC, D_SC) * proj(2 * D_RG + 2 * D_SC, D_SC)
    conv = scw_ref[0:1, :] * sbuf_ref[0:M1, :]
    for k in range(1, SC_CONV):
        conv = conv + scw_ref[k:k + 1, :] * sbuf_ref[k * BATCH:k * BATCH + M1, :]
    sbuf_ref[0:sc_pad, :] = sbuf_ref[M1:M1 + sc_pad, :]
    y_sc = sc_b * conv

    mix = jnp.dot(y_rg.astype(jnp.bfloat16), w_out_ref[0:D_RG, :], preferred_element_type=jnp.float32)
    mix = mix + jnp.dot(y_sc.astype(jnp.bfloat16), w_out_ref[D_RG:, :], preferred_element_type=jnp.float32)

    h1 = _layer_norm(DEEPNORM_ALPHA * xt_ref[...] + mix, g1_ref[...], b1_ref[...])
    _store_row_tiles(h1_ref, 0, h1)
    h_hi = h1.astype(jnp.bfloat16)
    h_lo = (h1 - h_hi.astype(jnp.float32)).astype(jnp.bfloat16)
    both = jnp.dot(h_hi, wr_ref[...], preferred_element_type=jnp.float32)
    logits = both[:, :ROUTER_LANES] + both[:, ROUTER_LANES:] + br_ref[...]
    logits = logits + jnp.dot(h_lo, wr_ref[:, :ROUTER_LANES], preferred_element_type=jnp.float32)
    logit_ref[...] = logits.T[:ROUTER_ROWS, :]


def _const_spec(shape):
    return pl.BlockSpec(shape, lambda c: (0,) * len(shape))


def _mixer_call(x, w_in, cw, cb, wa, ba, wx, bx, lam, scw, w_out, g1, b1, wr, br):
    n_chunks = SEQ // TS
    in_specs = [
        pl.BlockSpec((BATCH, TS, D_MODEL), lambda c: (0, c, 0)),
        _const_spec((D_MODEL, D_IN_PROJ)),
        _const_spec((RG_CONV, D_RG)), _const_spec((1, D_RG)),
        _const_spec((D_RG, D_RG)), _const_spec((1, D_RG)),
        _const_spec((D_RG, D_RG)), _const_spec((1, D_RG)),
        _const_spec((1, D_RG)),
        _const_spec((SC_CONV, D_SC)),
        _const_spec((D_MODEL, D_MODEL)),
        _const_spec((1, D_MODEL)), _const_spec((1, D_MODEL)),
        _const_spec((D_MODEL, 2 * ROUTER_LANES)), _const_spec((1, ROUTER_LANES)),
    ]
    out_specs = [
        pl.BlockSpec((M1 * LANE_TILES, LANES), lambda c: (c, 0)),
        pl.BlockSpec((ROUTER_ROWS, M1), lambda c: (0, c)),
    ]
    return pl.pallas_call(
        _mixer_kernel,
        grid=(n_chunks,),
        in_specs=in_specs,
        out_specs=out_specs,
        out_shape=[
            jax.ShapeDtypeStruct((N_TOKENS * LANE_TILES, LANES), jnp.float32),
            jax.ShapeDtypeStruct((ROUTER_ROWS, N_TOKENS), jnp.float32),
        ],
        scratch_shapes=[
            pltpu.VMEM((M1, D_MODEL), jnp.float32),
            pltpu.VMEM((M1 + (RG_CONV - 1) * BATCH, D_RG), jnp.float32),
            pltpu.VMEM((M1 + (SC_CONV - 1) * BATCH, D_SC), jnp.float32),
            pltpu.VMEM((M1, D_RG), jnp.float32),
            pltpu.VMEM((M1, D_RG), jnp.float32),
            pltpu.VMEM((BATCH, D_RG), jnp.float32),
        ],
        compiler_params=pltpu.CompilerParams(
            dimension_semantics=("arbitrary",), vmem_limit_bytes=VMEM_LIMIT),
        name="mixer_ln_router",
    )(x, w_in, cw, cb, wa, ba, wx, bx, lam, scw, w_out, g1, b1, wr, br)


def _expert_kernel(grp_ref, ea_ref, eb_ref, rows_ref,
                   xs_ref, wt_ref, wg_ref, wu_ref, wd_ref, g2_ref, b2_ref, out_ref):
    s = pl.program_id(0)

    def sub_tile(j, carry):
        tile = s * SUB + j
        first_row = pl.multiple_of(j * TM, TM)

        n_rows = rows_ref[tile]

        @pl.when(n_rows == 0)
        def _():
            out_ref[pl.ds(first_row * LANE_TILES, TM * LANE_TILES), :] = (
                jnp.zeros((TM * LANE_TILES, LANES), jnp.float32))

        @pl.when(n_rows != 0)
        def _():
            in_use = lax.broadcasted_iota(jnp.int32, (TM, 1), 0) < n_rows
            x = jnp.where(in_use, _load_row_tiles(xs_ref, first_row, TM), 0.0)
            xb = x.astype(jnp.bfloat16)
            wt = jnp.where(in_use, wt_ref[pl.ds(first_row, TM), :], 0.0)
            y = jnp.zeros((TM, D_MODEL), jnp.float32)
            for e_ref, col in ((ea_ref, 0), (eb_ref, 1)):
                e = e_ref[tile]
                hg = jnp.dot(xb, wg_ref[e], preferred_element_type=jnp.float32)
                hu = jnp.dot(xb, wu_ref[e], preferred_element_type=jnp.float32)
                hid = jax.nn.silu(hg) * hu * wt[:, col:col + 1]
                y = y + jnp.dot(hid.astype(jnp.bfloat16), wd_ref[e],
                                preferred_element_type=jnp.float32)
            _store_row_tiles(out_ref, first_row,
                             _layer_norm(DEEPNORM_ALPHA * x + y, g2_ref[...], b2_ref[...]))

        return carry

    lax.fori_loop(0, SUB, sub_tile, 0)


def _expert_call(step_group, tile_ea, tile_eb, tile_rows, xs, wts, wg, wu, wd, g2, b2):
    rows = SUB * TM
    grid_spec = pltpu.PrefetchScalarGridSpec(
        num_scalar_prefetch=4,
        grid=(N_STEPS3,),
        in_specs=[
            pl.BlockSpec((rows * LANE_TILES, LANES), lambda s, grp, ea, eb, va: (s, 0)),
            pl.BlockSpec((rows, LANES), lambda s, grp, ea, eb, va: (s, 0)),
            pl.BlockSpec((None, EXPERTS_PER_GROUP, D_MODEL, D_EXPERT),
                         lambda s, grp, ea, eb, va: (grp[s], 0, 0, 0)),
            pl.BlockSpec((None, EXPERTS_PER_GROUP, D_MODEL, D_EXPERT),
                         lambda s, grp, ea, eb, va: (grp[s], 0, 0, 0)),
            pl.BlockSpec((None, EXPERTS_PER_GROUP, D_EXPERT, D_MODEL),
                         lambda s, grp, ea, eb, va: (grp[s], 0, 0, 0)),
            pl.BlockSpec((1, D_MODEL), lambda s, grp, ea, eb, va: (0, 0)),
            pl.BlockSpec((1, D_MODEL), lambda s, grp, ea, eb, va: (0, 0)),
        ],
        out_specs=pl.BlockSpec((rows * LANE_TILES, LANES), lambda s, grp, ea, eb, va: (s, 0)),
    )
    return pl.pallas_call(
        _expert_kernel,
        grid_spec=grid_spec,
        out_shape=jax.ShapeDtypeStruct((P_ROWS * LANE_TILES, LANES), jnp.float32),
        compiler_params=pltpu.CompilerParams(
            dimension_semantics=("arbitrary",), vmem_limit_bytes=VMEM_LIMIT),
        name="experts_ln",
    )(step_group, tile_ea, tile_eb, tile_rows, xs, wts, wg, wu, wd, g2, b2)


SC_CORES = 2
SC_SUBCORES = 16
SC_WORKERS = SC_CORES * SC_SUBCORES
COMBINE_CHUNK = 32


def _sc_gather_rows(table, idx, chunk, out_row_shape=None):
    n_out, = idx.shape
    row_shape = table.shape[1:]
    out_row_shape = row_shape if out_row_shape is None else out_row_shape
    per_w = n_out // SC_WORKERS
    n_chunks = per_w // chunk
    assert per_w * SC_WORKERS == n_out and n_chunks * chunk == per_w
    assert n_chunks % 2 == 0 and chunk % 8 == 0
    mesh = plsc.VectorSubcoreMesh(core_axis_name="c", subcore_axis_name="s")

    @functools.partial(
        pl.kernel, mesh=mesh,
        out_type=jax.ShapeDtypeStruct((n_out,) + out_row_shape, table.dtype),
        scratch_types=[
            pltpu.VMEM((per_w,), jnp.int32),
            pltpu.VMEM((2, chunk) + row_shape, table.dtype),
            pltpu.SemaphoreType.DMA((2,)),
            pltpu.SemaphoreType.DMA((2,)),
        ],
    )
    def gather_kernel(table_hbm, idx_hbm, out_hbm, idx_v, buf, gsem, wsem):
        wid = lax.axis_index("s") * SC_CORES + lax.axis_index("c")
        base = wid * per_w
        pltpu.sync_copy(idx_hbm.at[pl.ds(base, per_w)], idx_v)

        def gather(j, slot):
            rows = idx_v.at[pl.ds(j * chunk, chunk)]
            return pltpu.make_async_copy(table_hbm.at[rows], buf.at[slot], gsem.at[slot])

        def write(j, slot):
            dst = out_hbm.at[pl.ds(base + j * chunk, chunk)]
            src = buf.at[slot].reshape((chunk,) + out_row_shape)
            return pltpu.make_async_copy(src, dst, wsem.at[slot])

        gather(0, 0).start()

        @pl.loop(0, n_chunks, step=2)
        def _(j0):
            for slot in range(2):
                j = j0 + slot
                gather(j, slot).wait()

                @pl.when(j >= 1)
                def _():
                    write(j - 1, 1 - slot).wait()

                @pl.when(j + 1 < n_chunks)
                def _():
                    gather(j + 1, 1 - slot).start()

                write(j, slot).start()

        write(n_chunks - 1, (n_chunks - 1) % 2).wait()

    return gather_kernel(table, idx)


SC_LANES = 16


def _sc_dispatch(rows, meta, cls, rank, class_start):
    n_rows = rows.shape[0]
    chunk = COMBINE_CHUNK
    per_w = n_rows // SC_WORKERS
    n_chunks = per_w // chunk
    assert per_w * SC_WORKERS == n_rows and n_chunks * chunk == per_w
    assert n_chunks % 2 == 0 and chunk % SC_LANES == 0
    mesh = plsc.VectorSubcoreMesh(core_axis_name="c", subcore_axis_name="s")

    @functools.partial(
        pl.kernel, mesh=mesh,
        out_type=[
            jax.ShapeDtypeStruct((P_ROWS,) + rows.shape[1:], rows.dtype),
            jax.ShapeDtypeStruct((P_ROWS,) + meta.shape[1:], meta.dtype),
            jax.ShapeDtypeStruct((n_rows,), jnp.int32),
        ],
        scratch_types=[
            pltpu.VMEM((per_w,), jnp.int32),
            pltpu.VMEM((per_w,), jnp.int32),
            pltpu.VMEM((LANES,), jnp.int32),
            pltpu.VMEM((per_w,), jnp.int32),
            pltpu.VMEM((n_chunks, chunk), jnp.int32),
            pltpu.VMEM((2, chunk) + rows.shape[1:], rows.dtype),
            pltpu.VMEM((2, chunk) + meta.shape[1:], meta.dtype),
            pltpu.SemaphoreType.DMA((2,)),
            pltpu.SemaphoreType.DMA((2,)),
            pltpu.SemaphoreType.DMA((2,)),
            pltpu.SemaphoreType.DMA((2,)),
        ],
        compiler_params=pltpu.CompilerParams(needs_layout_passes=False),
    )
    def dispatch_kernel(rows_hbm, meta_hbm, cls_hbm, rank_hbm, start_hbm,
                        xs_hbm, ws_hbm, pos_hbm,
                        cls_v, rank_v, start_v, pos_flat, pos_v, rbuf, mbuf,
                        rsem, msem, xsem, wsem):
        wid = lax.axis_index("s") * SC_CORES + lax.axis_index("c")
        base = wid * per_w
        pltpu.sync_copy(cls_hbm.at[pl.ds(base, per_w)], cls_v)
        pltpu.sync_copy(rank_hbm.at[pl.ds(base, per_w)], rank_v)
        pltpu.sync_copy(start_hbm, start_v)

        @pl.loop(0, n_chunks)
        def _(j):
            for k in range(chunk // SC_LANES):
                off = j * chunk + k * SC_LANES
                c = cls_v[pl.ds(off, SC_LANES)]
                pos = plsc.load_gather(start_v, [c]) + rank_v[pl.ds(off, SC_LANES)]
                pos_flat[pl.ds(off, SC_LANES)] = pos
                pos_v[j, pl.ds(k * SC_LANES, SC_LANES)] = pos

        pltpu.sync_copy(pos_flat, pos_hbm.at[pl.ds(base, per_w)])

        def read_rows(j, slot):
            src = rows_hbm.at[pl.ds(base + j * chunk, chunk)]
            return pltpu.make_async_copy(src, rbuf.at[slot], rsem.at[slot])

        def read_meta(j, slot):
            src = meta_hbm.at[pl.ds(base + j * chunk, chunk)]
            return pltpu.make_async_copy(src, mbuf.at[slot], msem.at[slot])

        def put_rows(j, slot):
            return pltpu.make_async_copy(rbuf.at[slot], xs_hbm.at[pos_v.at[j]], xsem.at[slot])

        def put_meta(j, slot):
            return pltpu.make_async_copy(mbuf.at[slot], ws_hbm.at[pos_v.at[j]], wsem.at[slot])

        read_rows(0, 0).start()
        read_meta(0, 0).start()

        @pl.loop(0, n_chunks, step=2)
        def _(j0):
            for slot in range(2):
                j = j0 + slot
                read_rows(j, slot).wait()
                read_meta(j, slot).wait()

                @pl.when(j >= 1)
                def _():
                    put_rows(j - 1, 1 - slot).wait()
                    put_meta(j - 1, 1 - slot).wait()

                @pl.when(j + 1 < n_chunks)
                def _():
                    read_rows(j + 1, 1 - slot).start()
                    read_meta(j + 1, 1 - slot).start()

                put_rows(j, slot).start()
                put_meta(j, slot).start()

        put_rows(n_chunks - 1, (n_chunks - 1) % 2).wait()
        put_meta(n_chunks - 1, (n_chunks - 1) % 2).wait()

    return dispatch_kernel(rows, meta, cls, rank, class_start)


def _block_diag(w):
    h, d, _ = w.shape
    eye = jnp.eye(h, dtype=w.dtype)
    return (eye[:, None, :, None] * w[:, :, None, :]).reshape(h * d, h * d)


def _pair_tables():
    pair_of = np.zeros((EXPERTS_PER_GROUP, EXPERTS_PER_GROUP), np.int32)
    lo = np.zeros((N_PAIRS,), np.int32)
    hi = np.zeros((N_PAIRS,), np.int32)
    p = 0
    for a in range(EXPERTS_PER_GROUP):
        for b in range(a + 1, EXPERTS_PER_GROUP):
            pair_of[a, b] = pair_of[b, a] = p
            lo[p], hi[p] = a, b
            p += 1
    return pair_of, lo, hi


def _route_kernel(lt_ref, info_ref, meta_ref, counts_ref, run_ref):
    step = pl.program_id(0)

    @pl.when(step == 0)
    def _():
        run_ref[...] = jnp.zeros_like(run_ref)

    f32 = jnp.float32
    sub8 = lax.broadcasted_iota(jnp.int32, (8, LANES), 0).astype(f32)
    row_id = lax.broadcasted_iota(jnp.int32, (LANES, LANES), 0)
    col_id = lax.broadcasted_iota(jnp.int32, (LANES, LANES), 1)
    prefix_mat = (row_id <= col_id).astype(jnp.bfloat16)
    ones_mat = jnp.ones((LANES, LANES), jnp.bfloat16)
    neg_inf = f32(-jnp.inf)

    def first_index_of_max(v):
        m = jnp.max(v, axis=0, keepdims=True)
        idx = jnp.min(jnp.where(v == m, sub8, f32(8)), axis=0, keepdims=True)
        return m, idx

    def lane_tile(k, run):
        lanes = pl.ds(pl.multiple_of(k * LANES, LANES), LANES)
        g = jnp.where(sub8 < N_GROUPS, lt_ref[0:GROUP_ROWS, lanes], neg_inf)
        g_max, g_idx = first_index_of_max(g)
        g_top_p = 1.0 / jnp.sum(jnp.exp(g - g_max), axis=0, keepdims=True)

        e_sel = lt_ref[GROUP_ROWS:GROUP_ROWS + EXPERTS_PER_GROUP, lanes]
        for grp in range(1, N_GROUPS):
            lo = GROUP_ROWS + grp * EXPERTS_PER_GROUP
            e_sel = jnp.where(g_idx == grp, lt_ref[lo:lo + EXPERTS_PER_GROUP, lanes], e_sel)
        m1, i1 = first_index_of_max(e_sel)
        rest = jnp.where(sub8 == i1, neg_inf, e_sel)
        m2 = jnp.max(rest, axis=0, keepdims=True)
        i2 = jnp.min(jnp.where((rest == m2) & (sub8 != i1), sub8, f32(8)), axis=0, keepdims=True)

        e = jnp.exp(m2 - m1)
        w1 = g_top_p / (1.0 + e)
        w2 = g_top_p * e / (1.0 + e)
        first_is_lo = i1 < i2
        w_lo = jnp.where(first_is_lo, w1, w2)
        w_hi = jnp.where(first_is_lo, w2, w1)
        lo_e = jnp.minimum(i1, i2)
        hi_e = jnp.maximum(i1, i2)
        pair = lo_e * (2 * EXPERTS_PER_GROUP - 1 - lo_e) * 0.5 + hi_e - lo_e - 1.0
        cls = (g_idx * N_PAIRS + pair).astype(jnp.int32)

        onehot = (row_id == cls).astype(jnp.bfloat16)
        csum = jnp.dot(onehot, prefix_mat, preferred_element_type=f32)
        rank = jnp.sum(onehot.astype(f32) * (csum + run), axis=0, keepdims=True) - 1.0
        run = run + jnp.dot(onehot, ones_mat, preferred_element_type=f32)

        info_ref[:, lanes] = jnp.where(sub8 == 0, cls, jnp.where(sub8 == 1, rank.astype(jnp.int32), 0))
        meta8 = jnp.where(sub8 == 0, w_lo, jnp.where(sub8 == 1, w_hi, 0.0))
        meta_t = jnp.concatenate([meta8, jnp.zeros((LANES - 8, LANES), f32)], axis=0)
        meta_ref[lanes, :] = meta_t.T
        return run

    run = lax.fori_loop(0, ROUTE_BLOCK // LANES, lane_tile, run_ref[...])
    run_ref[...] = run
    counts_ref[...] = run


def _route_call(lt):
    return pl.pallas_call(
        _route_kernel,
        grid=(N_TOKENS // ROUTE_BLOCK,),
        in_specs=[pl.BlockSpec((ROUTER_ROWS, ROUTE_BLOCK), lambda i: (0, i))],
        out_specs=[
            pl.BlockSpec((8, ROUTE_BLOCK), lambda i: (0, i)),
            pl.BlockSpec((ROUTE_BLOCK, LANES), lambda i: (i, 0)),
            pl.BlockSpec((LANES, LANES), lambda i: (0, 0)),
        ],
        out_shape=[
            jax.ShapeDtypeStruct((8, N_TOKENS), jnp.int32),
            jax.ShapeDtypeStruct((N_TOKENS, LANES), jnp.float32),
            jax.ShapeDtypeStruct((LANES, LANES), jnp.float32),
        ],
        scratch_shapes=[pltpu.VMEM((LANES, LANES), jnp.float32)],
        compiler_params=pltpu.CompilerParams(dimension_semantics=("arbitrary",)),
        name="route_rank",
    )(lt)


def _dispatch_plan(counts):
    _, pair_lo, pair_hi = _pair_tables()
    tiles_c = (counts + TM - 1) // TM
    tiles_g = tiles_c.reshape(N_GROUPS, N_PAIRS).sum(axis=1)
    tiles_g_pad = (tiles_g + SUB - 1) // SUB * SUB
    g_start = jnp.cumsum(tiles_g_pad) - tiles_g_pad
    tc = tiles_c.reshape(N_GROUPS, N_PAIRS)
    c_start = (g_start[:, None] + jnp.cumsum(tc, axis=1) - tc).reshape(N_CLASSES)
    class_start = jnp.zeros((LANES,), jnp.int32).at[:N_CLASSES].set(c_start * TM)

    tile_ids = jnp.arange(N_TILES, dtype=jnp.int32)
    c_end = c_start + tiles_c
    owner = jnp.sum((tile_ids[:, None] >= c_end[None, :]).astype(jnp.int32), axis=1)
    owner = jnp.minimum(owner, N_CLASSES - 1)
    in_class = (tile_ids >= c_start[owner]) & (tile_ids < c_end[owner])
    rows_left = counts[owner] - (tile_ids - c_start[owner]) * TM
    tile_rows = jnp.where(in_class, jnp.minimum(rows_left, TM), 0).astype(jnp.int32)
    pair = owner % N_PAIRS
    tile_ea = jnp.asarray(pair_lo)[pair]
    tile_eb = jnp.asarray(pair_hi)[pair]
    g_end = g_start + tiles_g_pad
    step_first = jnp.arange(N_STEPS3, dtype=jnp.int32) * SUB
    step_group = jnp.sum((step_first[:, None] >= g_end[None, :]).astype(jnp.int32), axis=1)
    step_group = jnp.minimum(step_group, N_GROUPS - 1)
    return class_start, tile_ea, tile_eb, tile_rows, step_group.astype(jnp.int32)


def kernel(x, w_in, rg_conv_w, rg_conv_b, rg_gate_a_w, rg_gate_a_b, rg_gate_x_w, rg_gate_x_b,
           rg_lambda, sc_conv_w, w_out, ln1_g, ln1_b, router_group_w, router_group_b,
           router_expert_w, router_expert_b, exp_w_gate, exp_w_up, exp_w_down, ln2_g, ln2_b):
    bf16 = jnp.bfloat16
    row = lambda v: v.reshape(1, -1)
    pad_g = GROUP_ROWS - N_GROUPS
    pad_e = ROUTER_LANES - ROUTER_ROWS
    wr = jnp.concatenate([router_group_w, jnp.zeros((D_MODEL, pad_g), jnp.float32),
                          router_expert_w, jnp.zeros((D_MODEL, pad_e), jnp.float32)], axis=1)
    br = jnp.concatenate([router_group_b, jnp.zeros((pad_g,), jnp.float32),
                          router_expert_b, jnp.zeros((pad_e,), jnp.float32)]).reshape(1, -1)
    wr_hi = wr.astype(bf16)
    wr_lo = (wr - wr_hi.astype(jnp.float32)).astype(bf16)

    h1, lt = _mixer_call(
        x, w_in.astype(bf16), rg_conv_w, row(rg_conv_b),
        _block_diag(rg_gate_a_w).astype(bf16), row(rg_gate_a_b),
        _block_diag(rg_gate_x_w).astype(bf16), row(rg_gate_x_b),
        row(rg_lambda), sc_conv_w, w_out.astype(bf16), row(ln1_g), row(ln1_b),
        jnp.concatenate([wr_hi, wr_lo], axis=1), br)

    info, meta, counts = _route_call(lt)
    class_start, tile_ea, tile_eb, tile_rows, step_group = _dispatch_plan(
        counts[:N_CLASSES, 0].astype(jnp.int32))

    row_tiles = (-1, LANE_TILES, LANES)
    xs, wts, pos = _sc_dispatch(h1.reshape(row_tiles), meta, info[0], info[1], class_start)

    grp_shape = (N_GROUPS, EXPERTS_PER_GROUP)
    ys = _expert_call(
        step_group, tile_ea, tile_eb, tile_rows, xs.reshape(-1, LANES), wts,
        exp_w_gate.astype(bf16).reshape(grp_shape + (D_MODEL, D_EXPERT)),
        exp_w_up.astype(bf16).reshape(grp_shape + (D_MODEL, D_EXPERT)),
        exp_w_down.astype(bf16).reshape(grp_shape + (D_EXPERT, D_MODEL)),
        row(ln2_g), row(ln2_b))

    out_rows = pos.reshape(SEQ, BATCH).T.reshape(-1)
    out = _sc_gather_rows(ys.reshape(row_tiles), out_rows, COMBINE_CHUNK, (D_MODEL,))
    return out.reshape(BATCH, SEQ, D_MODEL)
```

```python
import functools

import jax
import jax.numpy as jnp
import numpy as np
from jax import lax
from jax.experimental import pallas as pl
from jax.experimental.pallas import tpu as pltpu
from jax.experimental.pallas import tpu_sc as plsc

D_MODEL = 1024
BATCH = 16
SEQ = 2048
D_RG = 512
D_SC = 512
RG_HEADS = 8
RG_HEAD_DIM = D_RG // RG_HEADS
RG_CONV = 4
RG_C = 8.0
SC_CONV = 3
D_IN_PROJ = 2 * D_RG + 3 * D_SC
N_GROUPS = 4
EXPERTS_PER_GROUP = 8
N_EXPERTS = N_GROUPS * EXPERTS_PER_GROUP
D_EXPERT = D_MODEL // 4
LN_EPS = 1e-5
DEEPNORM_ALPHA = 2.0 ** 0.25

N_TOKENS = BATCH * SEQ
LANES = 128
LANE_TILES = D_MODEL // LANES
MXU_WIDTH = 256
ROUTER_LANES = LANES
GROUP_ROWS = 8
ROUTER_ROWS = GROUP_ROWS + N_EXPERTS

TS = 64
M1 = TS * BATCH

ROUTE_BLOCK = 4096

TM = 128
SUB = 4
N_PAIRS = EXPERTS_PER_GROUP * (EXPERTS_PER_GROUP - 1) // 2
N_CLASSES = N_GROUPS * N_PAIRS
N_TILES = -(-(N_TOKENS // TM + N_CLASSES + N_GROUPS * (SUB - 1)) // SUB) * SUB
N_STEPS3 = N_TILES // SUB
P_ROWS = N_TILES * TM

VMEM_LIMIT = 56 * 1024 * 1024


def _layer_norm(z, g, b):
    mu = jnp.mean(z, axis=-1, keepdims=True)
    zc = z - mu
    var = jnp.mean(zc * zc, axis=-1, keepdims=True)
    return zc * lax.rsqrt(var + LN_EPS) * g + b


def _store_row_tiles(ref, first_row, value):
    n = value.shape[0]
    for j in range(LANE_TILES):
        ref[pl.ds(first_row * LANE_TILES + j, n, stride=LANE_TILES), :] = (
            value[:, j * LANES:(j + 1) * LANES])


def _load_row_tiles(ref, first_row, n):
    return jnp.concatenate(
        [ref[pl.ds(first_row * LANE_TILES + j, n, stride=LANE_TILES), :]
         for j in range(LANE_TILES)], axis=1)


def _mixer_kernel(x_ref, w_in_ref, cw_ref, cb_ref, wa_ref, ba_ref, wx_ref, bx_ref, lam_ref,
                  scw_ref, w_out_ref, g1_ref, b1_ref, wr_ref, br_ref,
                  h1_ref, logit_ref,
                  xt_ref, cbuf_ref, sbuf_ref, a_ref, u_ref, hstate_ref):
    c = pl.program_id(0)
    rg_pad = (RG_CONV - 1) * BATCH
    sc_pad = (SC_CONV - 1) * BATCH

    @pl.when(c == 0)
    def _():
        cbuf_ref[0:rg_pad, :] = jnp.zeros((rg_pad, D_RG), jnp.float32)
        sbuf_ref[0:sc_pad, :] = jnp.zeros((sc_pad, D_SC), jnp.float32)
        hstate_ref[...] = jnp.zeros_like(hstate_ref)

    for t in range(TS):
        xt_ref[t * BATCH:(t + 1) * BATCH, :] = x_ref[:, t, :]

    xb = xt_ref[...].astype(jnp.bfloat16)

    def proj(lo, width):
        return jnp.dot(xb, w_in_ref[:, lo:lo + width], preferred_element_type=jnp.float32)

    cbuf_ref[rg_pad:rg_pad + M1, :] = proj(0, D_RG)
    xc = cb_ref[...] + cw_ref[0:1, :] * cbuf_ref[0:M1, :]
    for k in range(1, RG_CONV):
        xc = xc + cw_ref[k:k + 1, :] * cbuf_ref[k * BATCH:k * BATCH + M1, :]
    cbuf_ref[0:rg_pad, :] = cbuf_ref[M1:M1 + rg_pad, :]

    xcb = xc.astype(jnp.bfloat16)

    def gate(w_ref, b_ref):
        parts = [jnp.dot(xcb[:, lo:lo + MXU_WIDTH], w_ref[lo:lo + MXU_WIDTH, lo:lo + MXU_WIDTH],
                         preferred_element_type=jnp.float32)
                 for lo in range(0, D_RG, MXU_WIDTH)]
        return jax.nn.sigmoid(jnp.concatenate(parts, axis=1) + b_ref[...])

    r = gate(wa_ref, ba_ref)
    i = gate(wx_ref, bx_ref)
    log_a = (-RG_C) * r * jax.nn.softplus(-lam_ref[...])
    a_ref[...] = jnp.exp(log_a)
    th = jnp.tanh(log_a)
    u_ref[...] = jnp.sqrt(-2.0 * th / (1.0 - th)) * (i * xc)

    def scan_body(t, h):
        rows = pl.ds(pl.multiple_of(t * BATCH, BATCH), BATCH)
        h = a_ref[rows, :] * h + u_ref[rows, :]
        u_ref[rows, :] = h
        return h

    hstate_ref[...] = lax.fori_loop(0, TS, scan_body, hstate_ref[...], unroll=8)

    y_rg = u_ref[...] * jax.nn.gelu(proj(D_RG, D_RG))

    sc_b = proj(2 * D_RG, D_SC)
    sbuf_ref[sc_pad:sc_pad + M1, :] = proj(2 * D_RG + D_SC, D_SC) * proj(2 * D_RG + 2 * D_SC, D_SC)
    conv = scw_ref[0:1, :] * sbuf_ref[0:M1, :]
    for k in range(1, SC_CONV):
        conv = conv + scw_ref[k:k + 1, :] * sbuf_ref[k * BATCH:k * BATCH + M1, :]
    sbuf_ref[0:sc_pad, :] = sbuf_ref[M1:M1 + sc_pad, :]
    y_sc = sc_b * conv

    mix = jnp.dot(y_rg.astype(jnp.bfloat16), w_out_ref[0:D_RG, :], preferred_element_type=jnp.float32)
    mix = mix + jnp.dot(y_sc.astype(jnp.bfloat16), w_out_ref[D_RG:, :], preferred_element_type=jnp.float32)

    h1 = _layer_norm(DEEPNORM_ALPHA * xt_ref[...] + mix, g1_ref[...], b1_ref[...])
    _store_row_tiles(h1_ref, 0, h1)
    h_hi = h1.astype(jnp.bfloat16)
    h_lo = (h1 - h_hi.astype(jnp.float32)).astype(jnp.bfloat16)
    both = jnp.dot(h_hi, wr_ref[...], preferred_element_type=jnp.float32)
    logits = both[:, :ROUTER_LANES] + both[:, ROUTER_LANES:] + br_ref[...]
    logits = logits + jnp.dot(h_lo, wr_ref[:, :ROUTER_LANES], preferred_element_type=jnp.float32)
    logit_ref[...] = logits.T[:ROUTER_ROWS, :]


def _const_spec(shape):
    return pl.BlockSpec(shape, lambda c: (0,) * len(shape))


def _mixer_call(x, w_in, cw, cb, wa, ba, wx, bx, lam, scw, w_out, g1, b1, wr, br):
    n_chunks = SEQ // TS
    in_specs = [
        pl.BlockSpec((BATCH, TS, D_MODEL), lambda c: (0, c, 0)),
        _const_spec((D_MODEL, D_IN_PROJ)),
        _const_spec((RG_CONV, D_RG)), _const_spec((1, D_RG)),
        _const_spec((D_RG, D_RG)), _const_spec((1, D_RG)),
        _const_spec((D_RG, D_RG)), _const_spec((1, D_RG)),
        _const_spec((1, D_RG)),
        _const_spec((SC_CONV, D_SC)),
        _const_spec((D_MODEL, D_MODEL)),
        _const_spec((1, D_MODEL)), _const_spec((1, D_MODEL)),
        _const_spec((D_MODEL, 2 * ROUTER_LANES)), _const_spec((1, ROUTER_LANES)),
    ]
    out_specs = [
        pl.BlockSpec((M1 * LANE_TILES, LANES), lambda c: (c, 0)),
        pl.BlockSpec((ROUTER_ROWS, M1), lambda c: (0, c)),
    ]
    return pl.pallas_call(
        _mixer_kernel,
        grid=(n_chunks,),
        in_specs=in_specs,
        out_specs=out_specs,
        out_shape=[
            jax.ShapeDtypeStruct((N_TOKENS * LANE_TILES, LANES), jnp.float32),
            jax.ShapeDtypeStruct((ROUTER_ROWS, N_TOKENS), jnp.float32),
        ],
        scratch_shapes=[
            pltpu.VMEM((M1, D_MODEL), jnp.float32),
            pltpu.VMEM((M1 + (RG_CONV - 1) * BATCH, D_RG), jnp.float32),
            pltpu.VMEM((M1 + (SC_CONV - 1) * BATCH, D_SC), jnp.float32),
            pltpu.VMEM((M1, D_RG), jnp.float32),
            pltpu.VMEM((M1, D_RG), jnp.float32),
            pltpu.VMEM((BATCH, D_RG), jnp.float32),
        ],
        compiler_params=pltpu.CompilerParams(
            dimension_semantics=("arbitrary",), vmem_limit_bytes=VMEM_LIMIT),
        name="mixer_ln_router",
    )(x, w_in, cw, cb, wa, ba, wx, bx, lam, scw, w_out, g1, b1, wr, br)


def _expert_kernel(grp_ref, ea_ref, eb_ref, rows_ref,
                   xs_ref, wt_ref, wgu_ref, wd_ref, g2_ref, b2_ref, out_ref):
    s = pl.program_id(0)
    tile_rows = [rows_ref[s * SUB + j] for j in range(SUB)]

    @pl.when(sum(tile_rows) == 0)
    def _():
        out_ref[...] = jnp.zeros_like(out_ref)

    @pl.when(sum(tile_rows) != 0)
    def _():
        for j in range(SUB):
            tile = s * SUB + j
            first_row = j * TM
            in_use = lax.broadcasted_iota(jnp.int32, (TM, 1), 0) < tile_rows[j]
            x = jnp.where(in_use, _load_row_tiles(xs_ref, first_row, TM), 0.0)
            xb = x.astype(jnp.bfloat16)
            wt = jnp.where(in_use, wt_ref[first_row:first_row + TM, :], 0.0)
            y = jnp.zeros((TM, D_MODEL), jnp.float32)
            for e_ref, col in ((ea_ref, 0), (eb_ref, 1)):
                e = e_ref[tile]
                hgu = jnp.dot(xb, wgu_ref[e], preferred_element_type=jnp.float32)
                hid = jax.nn.silu(hgu[:, :D_EXPERT]) * hgu[:, D_EXPERT:] * wt[:, col:col + 1]
                y = y + jnp.dot(hid.astype(jnp.bfloat16), wd_ref[e],
                                preferred_element_type=jnp.float32)
            _store_row_tiles(out_ref, first_row,
                             _layer_norm(DEEPNORM_ALPHA * x + y, g2_ref[...], b2_ref[...]))


def _expert_call(step_group, tile_ea, tile_eb, tile_rows, xs, wts, wgu, wd, g2, b2):
    rows = SUB * TM
    grid_spec = pltpu.PrefetchScalarGridSpec(
        num_scalar_prefetch=4,
        grid=(N_STEPS3,),
        in_specs=[
            pl.BlockSpec((rows * LANE_TILES, LANES), lambda s, grp, ea, eb, va: (s, 0)),
            pl.BlockSpec((rows, LANES), lambda s, grp, ea, eb, va: (s, 0)),
            pl.BlockSpec((None, EXPERTS_PER_GROUP, D_MODEL, 2 * D_EXPERT),
                         lambda s, grp, ea, eb, va: (grp[s], 0, 0, 0)),
            pl.BlockSpec((None, EXPERTS_PER_GROUP, D_EXPERT, D_MODEL),
                         lambda s, grp, ea, eb, va: (grp[s], 0, 0, 0)),
            pl.BlockSpec((1, D_MODEL), lambda s, grp, ea, eb, va: (0, 0)),
            pl.BlockSpec((1, D_MODEL), lambda s, grp, ea, eb, va: (0, 0)),
        ],
        out_specs=pl.BlockSpec((rows * LANE_TILES, LANES), lambda s, grp, ea, eb, va: (s, 0)),
    )
    return pl.pallas_call(
        _expert_kernel,
        grid_spec=grid_spec,
        out_shape=jax.ShapeDtypeStruct((P_ROWS * LANE_TILES, LANES), jnp.float32),
        compiler_params=pltpu.CompilerParams(
            dimension_semantics=("arbitrary",), vmem_limit_bytes=VMEM_LIMIT),
        name="experts_ln",
    )(step_group, tile_ea, tile_eb, tile_rows, xs, wts, wgu, wd, g2, b2)


SC_CORES = 2
SC_SUBCORES = 16
SC_WORKERS = SC_CORES * SC_SUBCORES
COMBINE_CHUNK = 32


def _sc_gather_rows(table, idx, chunk, out_row_shape=None):
    n_out, = idx.shape
    row_shape = table.shape[1:]
    out_row_shape = row_shape if out_row_shape is None else out_row_shape
    per_w = n_out // SC_WORKERS
    n_chunks = per_w // chunk
    assert per_w * SC_WORKERS == n_out and n_chunks * chunk == per_w
    assert n_chunks % 2 == 0 and chunk % 8 == 0
    mesh = plsc.VectorSubcoreMesh(core_axis_name="c", subcore_axis_name="s")

    @functools.partial(
        pl.kernel, mesh=mesh,
        out_type=jax.ShapeDtypeStruct((n_out,) + out_row_shape, table.dtype),
        scratch_types=[
            pltpu.VMEM((per_w,), jnp.int32),
            pltpu.VMEM((2, chunk) + row_shape, table.dtype),
            pltpu.SemaphoreType.DMA((2,)),
            pltpu.SemaphoreType.DMA((2,)),
        ],
    )
    def gather_kernel(table_hbm, idx_hbm, out_hbm, idx_v, buf, gsem, wsem):
        wid = lax.axis_index("s") * SC_CORES + lax.axis_index("c")
        base = wid * per_w
        pltpu.sync_copy(idx_hbm.at[pl.ds(base, per_w)], idx_v)

        def gather(j, slot):
            rows = idx_v.at[pl.ds(j * chunk, chunk)]
            return pltpu.make_async_copy(table_hbm.at[rows], buf.at[slot], gsem.at[slot])

        def write(j, slot):
            dst = out_hbm.at[pl.ds(base + j * chunk, chunk)]
            src = buf.at[slot].reshape((chunk,) + out_row_shape)
            return pltpu.make_async_copy(src, dst, wsem.at[slot])

        gather(0, 0).start()

        @pl.loop(0, n_chunks, step=2)
        def _(j0):
            for slot in range(2):
                j = j0 + slot
                gather(j, slot).wait()

                @pl.when(j >= 1)
                def _():
                    write(j - 1, 1 - slot).wait()

                @pl.when(j + 1 < n_chunks)
                def _():
                    gather(j + 1, 1 - slot).start()

                write(j, slot).start()

        write(n_chunks - 1, (n_chunks - 1) % 2).wait()

    return gather_kernel(table, idx)


SC_LANES = 16


def _sc_dispatch(rows, meta, cls, rank, class_start):
    n_rows = rows.shape[0]
    chunk = COMBINE_CHUNK
    per_w = n_rows // SC_WORKERS
    n_chunks = per_w // chunk
    assert per_w * SC_WORKERS == n_rows and n_chunks * chunk == per_w
    assert n_chunks % 2 == 0 and chunk % SC_LANES == 0
    mesh = plsc.VectorSubcoreMesh(core_axis_name="c", subcore_axis_name="s")

    @functools.partial(
        pl.kernel, mesh=mesh,
        out_type=[
            jax.ShapeDtypeStruct((P_ROWS,) + rows.shape[1:], rows.dtype),
            jax.ShapeDtypeStruct((P_ROWS,) + meta.shape[1:], meta.dtype),
            jax.ShapeDtypeStruct((n_rows,), jnp.int32),
        ],
        scratch_types=[
            pltpu.VMEM((per_w,), jnp.int32),
            pltpu.VMEM((per_w,), jnp.int32),
            pltpu.VMEM((LANES,), jnp.int32),
            pltpu.VMEM((per_w,), jnp.int32),
            pltpu.VMEM((n_chunks, chunk), jnp.int32),
            pltpu.VMEM((2, chunk) + rows.shape[1:], rows.dtype),
            pltpu.VMEM((2, chunk) + meta.shape[1:], meta.dtype),
            pltpu.SemaphoreType.DMA((2,)),
            pltpu.SemaphoreType.DMA((2,)),
            pltpu.SemaphoreType.DMA((2,)),
            pltpu.SemaphoreType.DMA((2,)),
        ],
        compiler_params=pltpu.CompilerParams(needs_layout_passes=False),
    )
    def dispatch_kernel(rows_hbm, meta_hbm, cls_hbm, rank_hbm, start_hbm,
                        xs_hbm, ws_hbm, pos_hbm,
                        cls_v, rank_v, start_v, pos_flat, pos_v, rbuf, mbuf,
                        rsem, msem, xsem, wsem):
        wid = lax.axis_index("s") * SC_CORES + lax.axis_index("c")
        base = wid * per_w
        pltpu.sync_copy(cls_hbm.at[pl.ds(base, per_w)], cls_v)
        pltpu.sync_copy(rank_hbm.at[pl.ds(base, per_w)], rank_v)
        pltpu.sync_copy(start_hbm, start_v)

        @pl.loop(0, n_chunks)
        def _(j):
            for k in range(chunk // SC_LANES):
                off = j * chunk + k * SC_LANES
                c = cls_v[pl.ds(off, SC_LANES)]
                pos = plsc.load_gather(start_v, [c]) + rank_v[pl.ds(off, SC_LANES)]
                pos_flat[pl.ds(off, SC_LANES)] = pos
                pos_v[j, pl.ds(k * SC_LANES, SC_LANES)] = pos

        pltpu.sync_copy(pos_flat, pos_hbm.at[pl.ds(base, per_w)])

        def read_rows(j, slot):
            src = rows_hbm.at[pl.ds(base + j * chunk, chunk)]
            return pltpu.make_async_copy(src, rbuf.at[slot], rsem.at[slot])

        def read_meta(j, slot):
            src = meta_hbm.at[pl.ds(base + j * chunk, chunk)]
            return pltpu.make_async_copy(src, mbuf.at[slot], msem.at[slot])

        def put_rows(j, slot):
            return pltpu.make_async_copy(rbuf.at[slot], xs_hbm.at[pos_v.at[j]], xsem.at[slot])

        def put_meta(j, slot):
            return pltpu.make_async_copy(mbuf.at[slot], ws_hbm.at[pos_v.at[j]], wsem.at[slot])

        read_rows(0, 0).start()
        read_meta(0, 0).start()

        @pl.loop(0, n_chunks, step=2)
        def _(j0):
            for slot in range(2):
                j = j0 + slot
                read_rows(j, slot).wait()
                read_meta(j, slot).wait()

                @pl.when(j >= 1)
                def _():
                    put_rows(j - 1, 1 - slot).wait()
                    put_meta(j - 1, 1 - slot).wait()

                @pl.when(j + 1 < n_chunks)
                def _():
                    read_rows(j + 1, 1 - slot).start()
                    read_meta(j + 1, 1 - slot).start()

                put_rows(j, slot).start()
                put_meta(j, slot).start()

        put_rows(n_chunks - 1, (n_chunks - 1) % 2).wait()
        put_meta(n_chunks - 1, (n_chunks - 1) % 2).wait()

    return dispatch_kernel(rows, meta, cls, rank, class_start)


def _block_diag(w):
    h, d, _ = w.shape
    eye = jnp.eye(h, dtype=w.dtype)
    return (eye[:, None, :, None] * w[:, :, None, :]).reshape(h * d, h * d)


def _pair_tables():
    pair_of = np.zeros((EXPERTS_PER_GROUP, EXPERTS_PER_GROUP), np.int32)
    lo = np.zeros((N_PAIRS,), np.int32)
    hi = np.zeros((N_PAIRS,), np.int32)
    p = 0
    for a in range(EXPERTS_PER_GROUP):
        for b in range(a + 1, EXPERTS_PER_GROUP):
            pair_of[a, b] = pair_of[b, a] = p
            lo[p], hi[p] = a, b
            p += 1
    return pair_of, lo, hi


def _route_kernel(lt_ref, info_ref, meta_ref, counts_ref, run_ref):
    step = pl.program_id(0)

    @pl.when(step == 0)
    def _():
        run_ref[...] = jnp.zeros_like(run_ref)

    f32 = jnp.float32
    sub8 = lax.broadcasted_iota(jnp.int32, (8, LANES), 0).astype(f32)
    row_id = lax.broadcasted_iota(jnp.int32, (LANES, LANES), 0)
    col_id = lax.broadcasted_iota(jnp.int32, (LANES, LANES), 1)
    prefix_mat = (row_id <= col_id).astype(jnp.bfloat16)
    ones_mat = jnp.ones((LANES, LANES), jnp.bfloat16)
    neg_inf = f32(-jnp.inf)

    def first_index_of_max(v):
        m = jnp.max(v, axis=0, keepdims=True)
        idx = jnp.min(jnp.where(v == m, sub8, f32(8)), axis=0, keepdims=True)
        return m, idx

    def lane_tile(k, run):
        lanes = pl.ds(pl.multiple_of(k * LANES, LANES), LANES)
        g = jnp.where(sub8 < N_GROUPS, lt_ref[0:GROUP_ROWS, lanes], neg_inf)
        g_max, g_idx = first_index_of_max(g)
        g_top_p = 1.0 / jnp.sum(jnp.exp(g - g_max), axis=0, keepdims=True)

        e_sel = lt_ref[GROUP_ROWS:GROUP_ROWS + EXPERTS_PER_GROUP, lanes]
        for grp in range(1, N_GROUPS):
            lo = GROUP_ROWS + grp * EXPERTS_PER_GROUP
            e_sel = jnp.where(g_idx == grp, lt_ref[lo:lo + EXPERTS_PER_GROUP, lanes], e_sel)
        m1, i1 = first_index_of_max(e_sel)
        rest = jnp.where(sub8 == i1, neg_inf, e_sel)
        m2 = jnp.max(rest, axis=0, keepdims=True)
        i2 = jnp.min(jnp.where((rest == m2) & (sub8 != i1), sub8, f32(8)), axis=0, keepdims=True)

        e = jnp.exp(m2 - m1)
        w1 = g_top_p / (1.0 + e)
        w2 = g_top_p * e / (1.0 + e)
        first_is_lo = i1 < i2
        w_lo = jnp.where(first_is_lo, w1, w2)
        w_hi = jnp.where(first_is_lo, w2, w1)
        lo_e = jnp.minimum(i1, i2)
        hi_e = jnp.maximum(i1, i2)
        pair = lo_e * (2 * EXPERTS_PER_GROUP - 1 - lo_e) * 0.5 + hi_e - lo_e - 1.0
        cls = (g_idx * N_PAIRS + pair).astype(jnp.int32)

        onehot = (row_id == cls).astype(jnp.bfloat16)
        csum = jnp.dot(onehot, prefix_mat, preferred_element_type=f32)
        rank = jnp.sum(onehot.astype(f32) * (csum + run), axis=0, keepdims=True) - 1.0
        run = run + jnp.dot(onehot, ones_mat, preferred_element_type=f32)

        info_ref[:, lanes] = jnp.where(sub8 == 0, cls, jnp.where(sub8 == 1, rank.astype(jnp.int32), 0))
        meta8 = jnp.where(sub8 == 0, w_lo, jnp.where(sub8 == 1, w_hi, 0.0))
        meta_t = jnp.concatenate([meta8, jnp.zeros((LANES - 8, LANES), f32)], axis=0)
        meta_ref[lanes, :] = meta_t.T
        return run

    run = lax.fori_loop(0, ROUTE_BLOCK // LANES, lane_tile, run_ref[...])
    run_ref[...] = run
    counts_ref[...] = run


def _route_call(lt):
    return pl.pallas_call(
        _route_kernel,
        grid=(N_TOKENS // ROUTE_BLOCK,),
        in_specs=[pl.BlockSpec((ROUTER_ROWS, ROUTE_BLOCK), lambda i: (0, i))],
        out_specs=[
            pl.BlockSpec((8, ROUTE_BLOCK), lambda i: (0, i)),
            pl.BlockSpec((ROUTE_BLOCK, LANES), lambda i: (i, 0)),
            pl.BlockSpec((LANES, LANES), lambda i: (0, 0)),
        ],
        out_shape=[
            jax.ShapeDtypeStruct((8, N_TOKENS), jnp.int32),
            jax.ShapeDtypeStruct((N_TOKENS, LANES), jnp.float32),
            jax.ShapeDtypeStruct((LANES, LANES), jnp.float32),
        ],
        scratch_shapes=[pltpu.VMEM((LANES, LANES), jnp.float32)],
        compiler_params=pltpu.CompilerParams(dimension_semantics=("arbitrary",)),
        name="route_rank",
    )(lt)


def _dispatch_plan(counts):
    _, pair_lo, pair_hi = _pair_tables()
    tiles_c = (counts + TM - 1) // TM
    tiles_g = tiles_c.reshape(N_GROUPS, N_PAIRS).sum(axis=1)
    tiles_g_pad = (tiles_g + SUB - 1) // SUB * SUB
    g_start = jnp.cumsum(tiles_g_pad) - tiles_g_pad
    tc = tiles_c.reshape(N_GROUPS, N_PAIRS)
    c_start = (g_start[:, None] + jnp.cumsum(tc, axis=1) - tc).reshape(N_CLASSES)
    class_start = jnp.zeros((LANES,), jnp.int32).at[:N_CLASSES].set(c_start * TM)

    tile_ids = jnp.arange(N_TILES, dtype=jnp.int32)
    c_end = c_start + tiles_c
    owner = jnp.sum((tile_ids[:, None] >= c_end[None, :]).astype(jnp.int32), axis=1)
    owner = jnp.minimum(owner, N_CLASSES - 1)
    in_class = (tile_ids >= c_start[owner]) & (tile_ids < c_end[owner])
    rows_left = counts[owner] - (tile_ids - c_start[owner]) * TM
    tile_rows = jnp.where(in_class, jnp.minimum(rows_left, TM), 0).astype(jnp.int32)
    pair = owner % N_PAIRS
    tile_ea = jnp.asarray(pair_lo)[pair]
    tile_eb = jnp.asarray(pair_hi)[pair]
    g_end = g_start + tiles_g_pad
    step_first = jnp.arange(N_STEPS3, dtype=jnp.int32) * SUB
    step_group = jnp.sum((step_first[:, None] >= g_end[None, :]).astype(jnp.int32), axis=1)
    step_group = jnp.minimum(step_group, N_GROUPS - 1)
    return class_start, tile_ea, tile_eb, tile_rows, step_group.astype(jnp.int32)


def kernel(x, w_in, rg_conv_w, rg_conv_b, rg_gate_a_w, rg_gate_a_b, rg_gate_x_w, rg_gate_x_b,
           rg_lambda, sc_conv_w, w_out, ln1_g, ln1_b, router_group_w, router_group_b,
           router_expert_w, router_expert_b, exp_w_gate, exp_w_up, exp_w_down, ln2_g, ln2_b):
    bf16 = jnp.bfloat16
    row = lambda v: v.reshape(1, -1)
    pad_g = GROUP_ROWS - N_GROUPS
    pad_e = ROUTER_LANES - ROUTER_ROWS
    wr = jnp.concatenate([router_group_w, jnp.zeros((D_MODEL, pad_g), jnp.float32),
                          router_expert_w, jnp.zeros((D_MODEL, pad_e), jnp.float32)], axis=1)
    br = jnp.concatenate([router_group_b, jnp.zeros((pad_g,), jnp.float32),
                          router_expert_b, jnp.zeros((pad_e,), jnp.float32)]).reshape(1, -1)
    wr_hi = wr.astype(bf16)
    wr_lo = (wr - wr_hi.astype(jnp.float32)).astype(bf16)

    h1, lt = _mixer_call(
        x, w_in.astype(bf16), rg_conv_w, row(rg_conv_b),
        _block_diag(rg_gate_a_w).astype(bf16), row(rg_gate_a_b),
        _block_diag(rg_gate_x_w).astype(bf16), row(rg_gate_x_b),
        row(rg_lambda), sc_conv_w, w_out.astype(bf16), row(ln1_g), row(ln1_b),
        jnp.concatenate([wr_hi, wr_lo], axis=1), br)

    info, meta, counts = _route_call(lt)
    class_start, tile_ea, tile_eb, tile_rows, step_group = _dispatch_plan(
        counts[:N_CLASSES, 0].astype(jnp.int32))

    row_tiles = (-1, LANE_TILES, LANES)
    xs, wts, pos = _sc_dispatch(h1.reshape(row_tiles), meta, info[0], info[1], class_start)

    grp_shape = (N_GROUPS, EXPERTS_PER_GROUP)
    ys = _expert_call(
        step_group, tile_ea, tile_eb, tile_rows, xs.reshape(-1, LANES), wts,
        jnp.concatenate([exp_w_gate.astype(bf16), exp_w_up.astype(bf16)], axis=2).reshape(
            grp_shape + (D_MODEL, 2 * D_EXPERT)),
        exp_w_down.astype(bf16).reshape(grp_shape + (D_EXPERT, D_MODEL)),
        row(ln2_g), row(ln2_b))

    out_rows = pos.reshape(SEQ, BATCH).T.reshape(-1)
    out = _sc_gather_rows(ys.reshape(row_tiles), out_rows, COMBINE_CHUNK, (D_MODEL,))
    return out.reshape(BATCH, SEQ, D_MODEL)
```

```python
import functools

import jax
import jax.numpy as jnp
import numpy as np
from jax import lax
from jax.experimental import pallas as pl
from jax.experimental.pallas import tpu as pltpu
from jax.experimental.pallas import tpu_sc as plsc

D_MODEL = 1024
BATCH = 16
SEQ = 2048
D_RG = 512
D_SC = 512
RG_HEADS = 8
RG_HEAD_DIM = D_RG // RG_HEADS
RG_CONV = 4
RG_C = 8.0
SC_CONV = 3
D_IN_PROJ = 2 * D_RG + 3 * D_SC
N_GROUPS = 4
EXPERTS_PER_GROUP = 8
N_EXPERTS = N_GROUPS * EXPERTS_PER_GROUP
D_EXPERT = D_MODEL // 4
LN_EPS = 1e-5
DEEPNORM_ALPHA = 2.0 ** 0.25

N_TOKENS = BATCH * SEQ
LANES = 128
LANE_TILES = D_MODEL // LANES
MXU_WIDTH = 256
ROUTER_LANES = LANES
GROUP_ROWS = 8
ROUTER_ROWS = GROUP_ROWS + N_EXPERTS

TS = 64
M1 = TS * BATCH

ROUTE_BLOCK = 4096

TM = 128
SUB = 4
N_PAIRS = EXPERTS_PER_GROUP * (EXPERTS_PER_GROUP - 1) // 2
N_CLASSES = N_GROUPS * N_PAIRS
N_TILES = -(-(N_TOKENS // TM + N_CLASSES + N_GROUPS * (SUB - 1)) // SUB) * SUB
N_STEPS3 = N_TILES // SUB
P_ROWS = N_TILES * TM

VMEM_LIMIT = 56 * 1024 * 1024


def _layer_norm(z, g, b):
    mu = jnp.mean(z, axis=-1, keepdims=True)
    zc = z - mu
    var = jnp.mean(zc * zc, axis=-1, keepdims=True)
    return zc * lax.rsqrt(var + LN_EPS) * g + b


def _store_row_tiles(ref, first_row, value):
    n = value.shape[0]
    for j in range(LANE_TILES):
        ref[pl.ds(first_row * LANE_TILES + j, n, stride=LANE_TILES), :] = (
            value[:, j * LANES:(j + 1) * LANES])


def _load_row_tiles(ref, first_row, n):
    return jnp.concatenate(
        [ref[pl.ds(first_row * LANE_TILES + j, n, stride=LANE_TILES), :]
         for j in range(LANE_TILES)], axis=1)


def _mixer_kernel(x_ref, w_in_ref, cw_ref, cb_ref, wa_ref, ba_ref, wx_ref, bx_ref, lam_ref,
                  scw_ref, w_out_ref, g1_ref, b1_ref, wr_ref, br_ref,
                  h1_ref, logit_ref,
                  xbuf_ref, xsem, cbuf_ref, sbuf_ref, a_ref, u_ref, hstate_ref):
    c = pl.program_id(0)
    slot = c % 2
    rg_pad = (RG_CONV - 1) * BATCH
    sc_pad = (SC_CONV - 1) * BATCH

    def x_copies(chunk, slot_):
        return [pltpu.make_async_copy(x_ref.at[b, pl.ds(chunk * TS, TS), :],
                                      xbuf_ref.at[slot_, :, b, :], xsem.at[slot_])
                for b in range(BATCH)]

    @pl.when(c == 0)
    def _():
        for cp in x_copies(0, 0):
            cp.start()
        cbuf_ref[0:rg_pad, :] = jnp.zeros((rg_pad, D_RG), jnp.float32)
        sbuf_ref[0:sc_pad, :] = jnp.zeros((sc_pad, D_SC), jnp.float32)
        hstate_ref[...] = jnp.zeros_like(hstate_ref)

    @pl.when(c + 1 < SEQ // TS)
    def _():
        for cp in x_copies(c + 1, 1 - slot):
            cp.start()

    for cp in x_copies(c, slot):
        cp.wait()

    xt_ref = xbuf_ref.at[slot]
    xb = xt_ref[...].reshape(M1, D_MODEL).astype(jnp.bfloat16)

    def proj(lo, width):
        return jnp.dot(xb, w_in_ref[:, lo:lo + width], preferred_element_type=jnp.float32)

    cbuf_ref[rg_pad:rg_pad + M1, :] = proj(0, D_RG)
    xc = cb_ref[...] + cw_ref[0:1, :] * cbuf_ref[0:M1, :]
    for k in range(1, RG_CONV):
        xc = xc + cw_ref[k:k + 1, :] * cbuf_ref[k * BATCH:k * BATCH + M1, :]
    cbuf_ref[0:rg_pad, :] = cbuf_ref[M1:M1 + rg_pad, :]

    xcb = xc.astype(jnp.bfloat16)

    def gate(w_ref, b_ref):
        parts = [jnp.dot(xcb[:, lo:lo + MXU_WIDTH], w_ref[lo:lo + MXU_WIDTH, lo:lo + MXU_WIDTH],
                         preferred_element_type=jnp.float32)
                 for lo in range(0, D_RG, MXU_WIDTH)]
        return jax.nn.sigmoid(jnp.concatenate(parts, axis=1) + b_ref[...])

    r = gate(wa_ref, ba_ref)
    i = gate(wx_ref, bx_ref)
    log_a = (-RG_C) * r * jax.nn.softplus(-lam_ref[...])
    a_ref[...] = jnp.exp(log_a)
    th = jnp.tanh(log_a)
    u_ref[...] = jnp.sqrt(-2.0 * th / (1.0 - th)) * (i * xc)

    def scan_body(t, h):
        rows = pl.ds(pl.multiple_of(t * BATCH, BATCH), BATCH)
        h = a_ref[rows, :] * h + u_ref[rows, :]
        u_ref[rows, :] = h
        return h

    hstate_ref[...] = lax.fori_loop(0, TS, scan_body, hstate_ref[...], unroll=8)

    y_rg = u_ref[...] * jax.nn.gelu(proj(D_RG, D_RG))

    sc_b = proj(2 * D_RG, D_SC)
    sbuf_ref[sc_pad:sc_pad + M1, :] = proj(2 * D_RG + D_SC, D_SC) * proj(2 * D_RG + 2 * D_SC, D_SC)
    conv = scw_ref[0:1, :] * sbuf_ref[0:M1, :]
    for k in range(1, SC_CONV):
        conv = conv + scw_ref[k:k + 1, :] * sbuf_ref[k * BATCH:k * BATCH + M1, :]
    sbuf_ref[0:sc_pad, :] = sbuf_ref[M1:M1 + sc_pad, :]
    y_sc = sc_b * conv

    mix = jnp.dot(y_rg.astype(jnp.bfloat16), w_out_ref[0:D_RG, :], preferred_element_type=jnp.float32)
    mix = mix + jnp.dot(y_sc.astype(jnp.bfloat16), w_out_ref[D_RG:, :], preferred_element_type=jnp.float32)

    x_rows = xt_ref[...].reshape(M1, D_MODEL)
    h1 = _layer_norm(DEEPNORM_ALPHA * x_rows + mix, g1_ref[...], b1_ref[...])
    _store_row_tiles(h1_ref, 0, h1)
    h_hi = h1.astype(jnp.bfloat16)
    h_lo = (h1 - h_hi.astype(jnp.float32)).astype(jnp.bfloat16)
    both = jnp.dot(h_hi, wr_ref[...], preferred_element_type=jnp.float32)
    logits = both[:, :ROUTER_LANES] + both[:, ROUTER_LANES:] + br_ref[...]
    logits = logits + jnp.dot(h_lo, wr_ref[:, :ROUTER_LANES], preferred_element_type=jnp.float32)
    logit_ref[...] = logits.T[:ROUTER_ROWS, :]


def _const_spec(shape):
    return pl.BlockSpec(shape, lambda c: (0,) * len(shape))


def _mixer_call(x, w_in, cw, cb, wa, ba, wx, bx, lam, scw, w_out, g1, b1, wr, br):
    n_chunks = SEQ // TS
    in_specs = [
        pl.BlockSpec(memory_space=pl.ANY),
        _const_spec((D_MODEL, D_IN_PROJ)),
        _const_spec((RG_CONV, D_RG)), _const_spec((1, D_RG)),
        _const_spec((D_RG, D_RG)), _const_spec((1, D_RG)),
        _const_spec((D_RG, D_RG)), _const_spec((1, D_RG)),
        _const_spec((1, D_RG)),
        _const_spec((SC_CONV, D_SC)),
        _const_spec((D_MODEL, D_MODEL)),
        _const_spec((1, D_MODEL)), _const_spec((1, D_MODEL)),
        _const_spec((D_MODEL, 2 * ROUTER_LANES)), _const_spec((1, ROUTER_LANES)),
    ]
    out_specs = [
        pl.BlockSpec((M1 * LANE_TILES, LANES), lambda c: (c, 0)),
        pl.BlockSpec((ROUTER_ROWS, M1), lambda c: (0, c)),
    ]
    return pl.pallas_call(
        _mixer_kernel,
        grid=(n_chunks,),
        in_specs=in_specs,
        out_specs=out_specs,
        out_shape=[
            jax.ShapeDtypeStruct((N_TOKENS * LANE_TILES, LANES), jnp.float32),
            jax.ShapeDtypeStruct((ROUTER_ROWS, N_TOKENS), jnp.float32),
        ],
        scratch_shapes=[
            pltpu.VMEM((2, TS, BATCH, D_MODEL), jnp.float32),
            pltpu.SemaphoreType.DMA((2,)),
            pltpu.VMEM((M1 + (RG_CONV - 1) * BATCH, D_RG), jnp.float32),
            pltpu.VMEM((M1 + (SC_CONV - 1) * BATCH, D_SC), jnp.float32),
            pltpu.VMEM((M1, D_RG), jnp.float32),
            pltpu.VMEM((M1, D_RG), jnp.float32),
            pltpu.VMEM((BATCH, D_RG), jnp.float32),
        ],
        compiler_params=pltpu.CompilerParams(
            dimension_semantics=("arbitrary",), vmem_limit_bytes=VMEM_LIMIT),
        name="mixer_ln_router",
    )(x, w_in, cw, cb, wa, ba, wx, bx, lam, scw, w_out, g1, b1, wr, br)


def _expert_kernel(grp_ref, ea_ref, eb_ref, rows_ref,
                   xs_ref, wt_ref, wgu_ref, wd_ref, g2_ref, b2_ref, out_ref):
    s = pl.program_id(0)
    tile_rows = [rows_ref[s * SUB + j] for j in range(SUB)]

    @pl.when(sum(tile_rows) == 0)
    def _():
        out_ref[...] = jnp.zeros_like(out_ref)

    @pl.when(sum(tile_rows) != 0)
    def _():
        for j in range(SUB):
            tile = s * SUB + j
            first_row = j * TM
            in_use = lax.broadcasted_iota(jnp.int32, (TM, 1), 0) < tile_rows[j]
            x = jnp.where(in_use, _load_row_tiles(xs_ref, first_row, TM), 0.0)
            xb = x.astype(jnp.bfloat16)
            wt = jnp.where(in_use, wt_ref[first_row:first_row + TM, :], 0.0)
            y = jnp.zeros((TM, D_MODEL), jnp.float32)
            for e_ref, col in ((ea_ref, 0), (eb_ref, 1)):
                e = e_ref[tile]
                hgu = jnp.dot(xb, wgu_ref[e], preferred_element_type=jnp.float32)
                hid = jax.nn.silu(hgu[:, :D_EXPERT]) * hgu[:, D_EXPERT:] * wt[:, col:col + 1]
                y = y + jnp.dot(hid.astype(jnp.bfloat16), wd_ref[e],
                                preferred_element_type=jnp.float32)
            _store_row_tiles(out_ref, first_row,
                             _layer_norm(DEEPNORM_ALPHA * x + y, g2_ref[...], b2_ref[...]))


def _expert_call(step_group, tile_ea, tile_eb, tile_rows, xs, wts, wgu, wd, g2, b2):
    rows = SUB * TM
    grid_spec = pltpu.PrefetchScalarGridSpec(
        num_scalar_prefetch=4,
        grid=(N_STEPS3,),
        in_specs=[
            pl.BlockSpec((rows * LANE_TILES, LANES), lambda s, grp, ea, eb, va: (s, 0)),
            pl.BlockSpec((rows, LANES), lambda s, grp, ea, eb, va: (s, 0)),
            pl.BlockSpec((None, EXPERTS_PER_GROUP, D_MODEL, 2 * D_EXPERT),
                         lambda s, grp, ea, eb, va: (grp[s], 0, 0, 0)),
            pl.BlockSpec((None, EXPERTS_PER_GROUP, D_EXPERT, D_MODEL),
                         lambda s, grp, ea, eb, va: (grp[s], 0, 0, 0)),
            pl.BlockSpec((1, D_MODEL), lambda s, grp, ea, eb, va: (0, 0)),
            pl.BlockSpec((1, D_MODEL), lambda s, grp, ea, eb, va: (0, 0)),
        ],
        out_specs=pl.BlockSpec((rows * LANE_TILES, LANES), lambda s, grp, ea, eb, va: (s, 0)),
    )
    return pl.pallas_call(
        _expert_kernel,
        grid_spec=grid_spec,
        out_shape=jax.ShapeDtypeStruct((P_ROWS * LANE_TILES, LANES), jnp.float32),
        compiler_params=pltpu.CompilerParams(
            dimension_semantics=("arbitrary",), vmem_limit_bytes=VMEM_LIMIT),
        name="experts_ln",
    )(step_group, tile_ea, tile_eb, tile_rows, xs, wts, wgu, wd, g2, b2)


SC_CORES = 2
SC_SUBCORES = 16
SC_WORKERS = SC_CORES * SC_SUBCORES
COMBINE_CHUNK = 32


def _sc_gather_rows(table, idx, chunk, out_row_shape=None):
    n_out, = idx.shape
    row_shape = table.shape[1:]
    out_row_shape = row_shape if out_row_shape is None else out_row_shape
    per_w = n_out // SC_WORKERS
    n_chunks = per_w // chunk
    assert per_w * SC_WORKERS == n_out and n_chunks * chunk == per_w
    assert n_chunks % 2 == 0 and chunk % 8 == 0
    mesh = plsc.VectorSubcoreMesh(core_axis_name="c", subcore_axis_name="s")

    @functools.partial(
        pl.kernel, mesh=mesh,
        out_type=jax.ShapeDtypeStruct((n_out,) + out_row_shape, table.dtype),
        scratch_types=[
            pltpu.VMEM((per_w,), jnp.int32),
            pltpu.VMEM((2, chunk) + row_shape, table.dtype),
            pltpu.SemaphoreType.DMA((2,)),
            pltpu.SemaphoreType.DMA((2,)),
        ],
    )
    def gather_kernel(table_hbm, idx_hbm, out_hbm, idx_v, buf, gsem, wsem):
        wid = lax.axis_index("s") * SC_CORES + lax.axis_index("c")
        base = wid * per_w
        pltpu.sync_copy(idx_hbm.at[pl.ds(base, per_w)], idx_v)

        def gather(j, slot):
            rows = idx_v.at[pl.ds(j * chunk, chunk)]
            return pltpu.make_async_copy(table_hbm.at[rows], buf.at[slot], gsem.at[slot])

        def write(j, slot):
            dst = out_hbm.at[pl.ds(base + j * chunk, chunk)]
            src = buf.at[slot].reshape((chunk,) + out_row_shape)
            return pltpu.make_async_copy(src, dst, wsem.at[slot])

        gather(0, 0).start()

        @pl.loop(0, n_chunks, step=2)
        def _(j0):
            for slot in range(2):
                j = j0 + slot
                gather(j, slot).wait()

                @pl.when(j >= 1)
                def _():
                    write(j - 1, 1 - slot).wait()

                @pl.when(j + 1 < n_chunks)
                def _():
                    gather(j + 1, 1 - slot).start()

                write(j, slot).start()

        write(n_chunks - 1, (n_chunks - 1) % 2).wait()

    return gather_kernel(table, idx)


SC_LANES = 16


def _sc_dispatch(rows, meta, cls, rank, class_start):
    n_rows = rows.shape[0]
    chunk = COMBINE_CHUNK
    per_w = n_rows // SC_WORKERS
    n_chunks = per_w // chunk
    assert per_w * SC_WORKERS == n_rows and n_chunks * chunk == per_w
    assert n_chunks % 2 == 0 and chunk % SC_LANES == 0
    mesh = plsc.VectorSubcoreMesh(core_axis_name="c", subcore_axis_name="s")

    @functools.partial(
        pl.kernel, mesh=mesh,
        out_type=[
            jax.ShapeDtypeStruct((P_ROWS,) + rows.shape[1:], rows.dtype),
            jax.ShapeDtypeStruct((P_ROWS,) + meta.shape[1:], meta.dtype),
            jax.ShapeDtypeStruct((n_rows,), jnp.int32),
        ],
        scratch_types=[
            pltpu.VMEM((per_w,), jnp.int32),
            pltpu.VMEM((per_w,), jnp.int32),
            pltpu.VMEM((LANES,), jnp.int32),
            pltpu.VMEM((per_w,), jnp.int32),
            pltpu.VMEM((n_chunks, chunk), jnp.int32),
            pltpu.VMEM((2, chunk) + rows.shape[1:], rows.dtype),
            pltpu.VMEM((2, chunk) + meta.shape[1:], meta.dtype),
            pltpu.SemaphoreType.DMA((2,)),
            pltpu.SemaphoreType.DMA((2,)),
            pltpu.SemaphoreType.DMA((2,)),
            pltpu.SemaphoreType.DMA((2,)),
        ],
        compiler_params=pltpu.CompilerParams(needs_layout_passes=False),
    )
    def dispatch_kernel(rows_hbm, meta_hbm, cls_hbm, rank_hbm, start_hbm,
                        xs_hbm, ws_hbm, pos_hbm,
                        cls_v, rank_v, start_v, pos_flat, pos_v, rbuf, mbuf,
                        rsem, msem, xsem, wsem):
        wid = lax.axis_index("s") * SC_CORES + lax.axis_index("c")
        base = wid * per_w
        pltpu.sync_copy(cls_hbm.at[pl.ds(base, per_w)], cls_v)
        pltpu.sync_copy(rank_hbm.at[pl.ds(base, per_w)], rank_v)
        pltpu.sync_copy(start_hbm, start_v)

        @pl.loop(0, n_chunks)
        def _(j):
            for k in range(chunk // SC_LANES):
                off = j * chunk + k * SC_LANES
                c = cls_v[pl.ds(off, SC_LANES)]
                pos = plsc.load_gather(start_v, [c]) + rank_v[pl.ds(off, SC_LANES)]
                pos_flat[pl.ds(off, SC_LANES)] = pos
                pos_v[j, pl.ds(k * SC_LANES, SC_LANES)] = pos

        pltpu.sync_copy(pos_flat, pos_hbm.at[pl.ds(base, per_w)])

        def read_rows(j, slot):
            src = rows_hbm.at[pl.ds(base + j * chunk, chunk)]
            return pltpu.make_async_copy(src, rbuf.at[slot], rsem.at[slot])

        def read_meta(j, slot):
            src = meta_hbm.at[pl.ds(base + j * chunk, chunk)]
            return pltpu.make_async_copy(src, mbuf.at[slot], msem.at[slot])

        def put_rows(j, slot):
            return pltpu.make_async_copy(rbuf.at[slot], xs_hbm.at[pos_v.at[j]], xsem.at[slot])

        def put_meta(j, slot):
            return pltpu.make_async_copy(mbuf.at[slot], ws_hbm.at[pos_v.at[j]], wsem.at[slot])

        read_rows(0, 0).start()
        read_meta(0, 0).start()

        @pl.loop(0, n_chunks, step=2)
        def _(j0):
            for slot in range(2):
                j = j0 + slot
                read_rows(j, slot).wait()
                read_meta(j, slot).wait()

                @pl.when(j >= 1)
                def _():
                    put_rows(j - 1, 1 - slot).wait()
                    put_meta(j - 1, 1 - slot).wait()

                @pl.when(j + 1 < n_chunks)
                def _():
                    read_rows(j + 1, 1 - slot).start()
                    read_meta(j + 1, 1 - slot).start()

                put_rows(j, slot).start()
                put_meta(j, slot).start()

        put_rows(n_chunks - 1, (n_chunks - 1) % 2).wait()
        put_meta(n_chunks - 1, (n_chunks - 1) % 2).wait()

    return dispatch_kernel(rows, meta, cls, rank, class_start)


def _block_diag(w):
    h, d, _ = w.shape
    eye = jnp.eye(h, dtype=w.dtype)
    return (eye[:, None, :, None] * w[:, :, None, :]).reshape(h * d, h * d)


def _pair_tables():
    pair_of = np.zeros((EXPERTS_PER_GROUP, EXPERTS_PER_GROUP), np.int32)
    lo = np.zeros((N_PAIRS,), np.int32)
    hi = np.zeros((N_PAIRS,), np.int32)
    p = 0
    for a in range(EXPERTS_PER_GROUP):
        for b in range(a + 1, EXPERTS_PER_GROUP):
            pair_of[a, b] = pair_of[b, a] = p
            lo[p], hi[p] = a, b
            p += 1
    return pair_of, lo, hi


def _route_kernel(lt_ref, info_ref, meta_ref, counts_ref, run_ref):
    step = pl.program_id(0)

    @pl.when(step == 0)
    def _():
        run_ref[...] = jnp.zeros_like(run_ref)

    f32 = jnp.float32
    sub8 = lax.broadcasted_iota(jnp.int32, (8, LANES), 0).astype(f32)
    row_id = lax.broadcasted_iota(jnp.int32, (LANES, LANES), 0)
    col_id = lax.broadcasted_iota(jnp.int32, (LANES, LANES), 1)
    prefix_mat = (row_id <= col_id).astype(jnp.bfloat16)
    ones_mat = jnp.ones((LANES, LANES), jnp.bfloat16)
    neg_inf = f32(-jnp.inf)

    def first_index_of_max(v):
        m = jnp.max(v, axis=0, keepdims=True)
        idx = jnp.min(jnp.where(v == m, sub8, f32(8)), axis=0, keepdims=True)
        return m, idx

    def lane_tile(k, run):
        lanes = pl.ds(pl.multiple_of(k * LANES, LANES), LANES)
        g = jnp.where(sub8 < N_GROUPS, lt_ref[0:GROUP_ROWS, lanes], neg_inf)
        g_max, g_idx = first_index_of_max(g)
        g_top_p = 1.0 / jnp.sum(jnp.exp(g - g_max), axis=0, keepdims=True)

        e_sel = lt_ref[GROUP_ROWS:GROUP_ROWS + EXPERTS_PER_GROUP, lanes]
        for grp in range(1, N_GROUPS):
            lo = GROUP_ROWS + grp * EXPERTS_PER_GROUP
            e_sel = jnp.where(g_idx == grp, lt_ref[lo:lo + EXPERTS_PER_GROUP, lanes], e_sel)
        m1, i1 = first_index_of_max(e_sel)
        rest = jnp.where(sub8 == i1, neg_inf, e_sel)
        m2 = jnp.max(rest, axis=0, keepdims=True)
        i2 = jnp.min(jnp.where((rest == m2) & (sub8 != i1), sub8, f32(8)), axis=0, keepdims=True)

        e = jnp.exp(m2 - m1)
        w1 = g_top_p / (1.0 + e)
        w2 = g_top_p * e / (1.0 + e)
        first_is_lo = i1 < i2
        w_lo = jnp.where(first_is_lo, w1, w2)
        w_hi = jnp.where(first_is_lo, w2, w1)
        lo_e = jnp.minimum(i1, i2)
        hi_e = jnp.maximum(i1, i2)
        pair = lo_e * (2 * EXPERTS_PER_GROUP - 1 - lo_e) * 0.5 + hi_e - lo_e - 1.0
        cls = (g_idx * N_PAIRS + pair).astype(jnp.int32)

        onehot = (row_id == cls).astype(jnp.bfloat16)
        csum = jnp.dot(onehot, prefix_mat, preferred_element_type=f32)
        rank = jnp.sum(onehot.astype(f32) * (csum + run), axis=0, keepdims=True) - 1.0
        run = run + jnp.dot(onehot, ones_mat, preferred_element_type=f32)

        info_ref[:, lanes] = jnp.where(sub8 == 0, cls, jnp.where(sub8 == 1, rank.astype(jnp.int32), 0))
        meta8 = jnp.where(sub8 == 0, w_lo, jnp.where(sub8 == 1, w_hi, 0.0))
        meta_t = jnp.concatenate([meta8, jnp.zeros((LANES - 8, LANES), f32)], axis=0)
        meta_ref[lanes, :] = meta_t.T
        return run

    run = lax.fori_loop(0, ROUTE_BLOCK // LANES, lane_tile, run_ref[...])
    run_ref[...] = run
    counts_ref[...] = run


def _route_call(lt):
    return pl.pallas_call(
        _route_kernel,
        grid=(N_TOKENS // ROUTE_BLOCK,),
        in_specs=[pl.BlockSpec((ROUTER_ROWS, ROUTE_BLOCK), lambda i: (0, i))],
        out_specs=[
            pl.BlockSpec((8, ROUTE_BLOCK), lambda i: (0, i)),
            pl.BlockSpec((ROUTE_BLOCK, LANES), lambda i: (i, 0)),
            pl.BlockSpec((LANES, LANES), lambda i: (0, 0)),
        ],
        out_shape=[
            jax.ShapeDtypeStruct((8, N_TOKENS), jnp.int32),
            jax.ShapeDtypeStruct((N_TOKENS, LANES), jnp.float32),
            jax.ShapeDtypeStruct((LANES, LANES), jnp.float32),
        ],
        scratch_shapes=[pltpu.VMEM((LANES, LANES), jnp.float32)],
        compiler_params=pltpu.CompilerParams(dimension_semantics=("arbitrary",)),
        name="route_rank",
    )(lt)


def _dispatch_plan(counts):
    _, pair_lo, pair_hi = _pair_tables()
    tiles_c = (counts + TM - 1) // TM
    tiles_g = tiles_c.reshape(N_GROUPS, N_PAIRS).sum(axis=1)
    tiles_g_pad = (tiles_g + SUB - 1) // SUB * SUB
    g_start = jnp.cumsum(tiles_g_pad) - tiles_g_pad
    tc = tiles_c.reshape(N_GROUPS, N_PAIRS)
    c_start = (g_start[:, None] + jnp.cumsum(tc, axis=1) - tc).reshape(N_CLASSES)
    class_start = jnp.zeros((LANES,), jnp.int32).at[:N_CLASSES].set(c_start * TM)

    tile_ids = jnp.arange(N_TILES, dtype=jnp.int32)
    c_end = c_start + tiles_c
    owner = jnp.sum((tile_ids[:, None] >= c_end[None, :]).astype(jnp.int32), axis=1)
    owner = jnp.minimum(owner, N_CLASSES - 1)
    in_class = (tile_ids >= c_start[owner]) & (tile_ids < c_end[owner])
    rows_left = counts[owner] - (tile_ids - c_start[owner]) * TM
    tile_rows = jnp.where(in_class, jnp.minimum(rows_left, TM), 0).astype(jnp.int32)
    pair = owner % N_PAIRS
    tile_ea = jnp.asarray(pair_lo)[pair]
    tile_eb = jnp.asarray(pair_hi)[pair]
    g_end = g_start + tiles_g_pad
    step_first = jnp.arange(N_STEPS3, dtype=jnp.int32) * SUB
    step_group = jnp.sum((step_first[:, None] >= g_end[None, :]).astype(jnp.int32), axis=1)
    step_group = jnp.minimum(step_group, N_GROUPS - 1)
    return class_start, tile_ea, tile_eb, tile_rows, step_group.astype(jnp.int32)


def kernel(x, w_in, rg_conv_w, rg_conv_b, rg_gate_a_w, rg_gate_a_b, rg_gate_x_w, rg_gate_x_b,
           rg_lambda, sc_conv_w, w_out, ln1_g, ln1_b, router_group_w, router_group_b,
           router_expert_w, router_expert_b, exp_w_gate, exp_w_up, exp_w_down, ln2_g, ln2_b):
    bf16 = jnp.bfloat16
    row = lambda v: v.reshape(1, -1)
    pad_g = GROUP_ROWS - N_GROUPS
    pad_e = ROUTER_LANES - ROUTER_ROWS
    wr = jnp.concatenate([router_group_w, jnp.zeros((D_MODEL, pad_g), jnp.float32),
                          router_expert_w, jnp.zeros((D_MODEL, pad_e), jnp.float32)], axis=1)
    br = jnp.concatenate([router_group_b, jnp.zeros((pad_g,), jnp.float32),
                          router_expert_b, jnp.zeros((pad_e,), jnp.float32)]).reshape(1, -1)
    wr_hi = wr.astype(bf16)
    wr_lo = (wr - wr_hi.astype(jnp.float32)).astype(bf16)

    h1, lt = _mixer_call(
        x, w_in.astype(bf16), rg_conv_w, row(rg_conv_b),
        _block_diag(rg_gate_a_w).astype(bf16), row(rg_gate_a_b),
        _block_diag(rg_gate_x_w).astype(bf16), row(rg_gate_x_b),
        row(rg_lambda), sc_conv_w, w_out.astype(bf16), row(ln1_g), row(ln1_b),
        jnp.concatenate([wr_hi, wr_lo], axis=1), br)

    info, meta, counts = _route_call(lt)
    class_start, tile_ea, tile_eb, tile_rows, step_group = _dispatch_plan(
        counts[:N_CLASSES, 0].astype(jnp.int32))

    row_tiles = (-1, LANE_TILES, LANES)
    xs, wts, pos = _sc_dispatch(h1.reshape(row_tiles), meta, info[0], info[1], class_start)

    grp_shape = (N_GROUPS, EXPERTS_PER_GROUP)
    ys = _expert_call(
        step_group, tile_ea, tile_eb, tile_rows, xs.reshape(-1, LANES), wts,
        jnp.concatenate([exp_w_gate.astype(bf16), exp_w_up.astype(bf16)], axis=2).reshape(
            grp_shape + (D_MODEL, 2 * D_EXPERT)),
        exp_w_down.astype(bf16).reshape(grp_shape + (D_EXPERT, D_MODEL)),
        row(ln2_g), row(ln2_b))

    out_rows = pos.reshape(SEQ, BATCH).T.reshape(-1)
    out = _sc_gather_rows(ys.reshape(row_tiles), out_rows, COMBINE_CHUNK, (D_MODEL,))
    return out.reshape(BATCH, SEQ, D_MODEL)
```

```python
import functools

import jax
import jax.numpy as jnp
import numpy as np
from jax import lax
from jax.experimental import pallas as pl
from jax.experimental.pallas import tpu as pltpu
from jax.experimental.pallas import tpu_sc as plsc

D_MODEL = 1024
BATCH = 16
SEQ = 2048
D_RG = 512
D_SC = 512
RG_HEADS = 8
RG_HEAD_DIM = D_RG // RG_HEADS
RG_CONV = 4
RG_C = 8.0
SC_CONV = 3
D_IN_PROJ = 2 * D_RG + 3 * D_SC
N_GROUPS = 4
EXPERTS_PER_GROUP = 8
N_EXPERTS = N_GROUPS * EXPERTS_PER_GROUP
D_EXPERT = D_MODEL // 4
LN_EPS = 1e-5
DEEPNORM_ALPHA = 2.0 ** 0.25

N_TOKENS = BATCH * SEQ
LANES = 128
LANE_TILES = D_MODEL // LANES
MXU_WIDTH = 256
ROUTER_LANES = LANES
GROUP_ROWS = 8
ROUTER_ROWS = GROUP_ROWS + N_EXPERTS

TS = 64
M1 = TS * BATCH

ROUTE_BLOCK = 4096
ROUTE_UNROLL = 4

TM = 128
SUB = 4
N_PAIRS = EXPERTS_PER_GROUP * (EXPERTS_PER_GROUP - 1) // 2
N_CLASSES = N_GROUPS * N_PAIRS
N_TILES = -(-(N_TOKENS // TM + N_CLASSES + N_GROUPS * (SUB - 1)) // SUB) * SUB
N_STEPS3 = N_TILES // SUB
P_ROWS = N_TILES * TM

VMEM_LIMIT = 56 * 1024 * 1024


def _layer_norm(z, g, b):
    mu = jnp.mean(z, axis=-1, keepdims=True)
    zc = z - mu
    var = jnp.mean(zc * zc, axis=-1, keepdims=True)
    return zc * lax.rsqrt(var + LN_EPS) * g + b


def _store_row_tiles(ref, first_row, value):
    n = value.shape[0]
    for j in range(LANE_TILES):
        ref[pl.ds(first_row * LANE_TILES + j, n, stride=LANE_TILES), :] = (
            value[:, j * LANES:(j + 1) * LANES])


def _load_row_tiles(ref, first_row, n):
    return jnp.concatenate(
        [ref[pl.ds(first_row * LANE_TILES + j, n, stride=LANE_TILES), :]
         for j in range(LANE_TILES)], axis=1)


def _mixer_kernel(x_ref, w_in_ref, cw_ref, cb_ref, wa_ref, ba_ref, wx_ref, bx_ref, lam_ref,
                  scw_ref, w_out_ref, g1_ref, b1_ref, wr_ref, br_ref,
                  h1_ref, logit_ref,
                  xbuf_ref, xsem, cbuf_ref, sbuf_ref, a_ref, u_ref, hstate_ref):
    c = pl.program_id(0)
    slot = c % 2
    rg_pad = (RG_CONV - 1) * BATCH
    sc_pad = (SC_CONV - 1) * BATCH

    def x_copies(chunk, slot_):
        return [pltpu.make_async_copy(x_ref.at[b, pl.ds(chunk * TS, TS), :],
                                      xbuf_ref.at[slot_, :, b, :], xsem.at[slot_])
                for b in range(BATCH)]

    @pl.when(c == 0)
    def _():
        for cp in x_copies(0, 0):
            cp.start()
        cbuf_ref[0:rg_pad, :] = jnp.zeros((rg_pad, D_RG), jnp.float32)
        sbuf_ref[0:sc_pad, :] = jnp.zeros((sc_pad, D_SC), jnp.float32)
        hstate_ref[...] = jnp.zeros_like(hstate_ref)

    @pl.when(c + 1 < SEQ // TS)
    def _():
        for cp in x_copies(c + 1, 1 - slot):
            cp.start()

    for cp in x_copies(c, slot):
        cp.wait()

    xt_ref = xbuf_ref.at[slot]
    xb = xt_ref[...].reshape(M1, D_MODEL).astype(jnp.bfloat16)

    def proj(lo, width):
        return jnp.dot(xb, w_in_ref[:, lo:lo + width], preferred_element_type=jnp.float32)

    cbuf_ref[rg_pad:rg_pad + M1, :] = proj(0, D_RG)
    xc = cb_ref[...] + cw_ref[0:1, :] * cbuf_ref[0:M1, :]
    for k in range(1, RG_CONV):
        xc = xc + cw_ref[k:k + 1, :] * cbuf_ref[k * BATCH:k * BATCH + M1, :]
    cbuf_ref[0:rg_pad, :] = cbuf_ref[M1:M1 + rg_pad, :]

    xcb = xc.astype(jnp.bfloat16)

    def gate(w_ref, b_ref):
        parts = [jnp.dot(xcb[:, lo:lo + MXU_WIDTH], w_ref[lo:lo + MXU_WIDTH, lo:lo + MXU_WIDTH],
                         preferred_element_type=jnp.float32)
                 for lo in range(0, D_RG, MXU_WIDTH)]
        return jax.nn.sigmoid(jnp.concatenate(parts, axis=1) + b_ref[...])

    r = gate(wa_ref, ba_ref)
    i = gate(wx_ref, bx_ref)
    log_a = (-RG_C) * r * jax.nn.softplus(-lam_ref[...])
    a_ref[...] = jnp.exp(log_a)
    th = jnp.tanh(log_a)
    u_ref[...] = jnp.sqrt(-2.0 * th / (1.0 - th)) * (i * xc)

    sc_b = proj(2 * D_RG, D_SC)
    sbuf_ref[sc_pad:sc_pad + M1, :] = proj(2 * D_RG + D_SC, D_SC) * proj(2 * D_RG + 2 * D_SC, D_SC)
    conv = scw_ref[0:1, :] * sbuf_ref[0:M1, :]
    for k in range(1, SC_CONV):
        conv = conv + scw_ref[k:k + 1, :] * sbuf_ref[k * BATCH:k * BATCH + M1, :]
    sbuf_ref[0:sc_pad, :] = sbuf_ref[M1:M1 + sc_pad, :]
    y_sc = sc_b * conv

    h = hstate_ref[...]
    for t in range(TS):
        h = a_ref[t * BATCH:(t + 1) * BATCH, :] * h + u_ref[t * BATCH:(t + 1) * BATCH, :]
        u_ref[t * BATCH:(t + 1) * BATCH, :] = h
    hstate_ref[...] = h

    y_rg = u_ref[...] * jax.nn.gelu(proj(D_RG, D_RG))

    mix = jnp.dot(y_rg.astype(jnp.bfloat16), w_out_ref[0:D_RG, :], preferred_element_type=jnp.float32)
    mix = mix + jnp.dot(y_sc.astype(jnp.bfloat16), w_out_ref[D_RG:, :], preferred_element_type=jnp.float32)

    x_rows = xt_ref[...].reshape(M1, D_MODEL)
    h1 = _layer_norm(DEEPNORM_ALPHA * x_rows + mix, g1_ref[...], b1_ref[...])
    _store_row_tiles(h1_ref, 0, h1)
    h_hi = h1.astype(jnp.bfloat16)
    h_lo = (h1 - h_hi.astype(jnp.float32)).astype(jnp.bfloat16)
    both = jnp.dot(h_hi, wr_ref[...], preferred_element_type=jnp.float32)
    logits = both[:, :ROUTER_LANES] + both[:, ROUTER_LANES:] + br_ref[...]
    logits = logits + jnp.dot(h_lo, wr_ref[:, :ROUTER_LANES], preferred_element_type=jnp.float32)
    logit_ref[...] = logits.T[:ROUTER_ROWS, :]


def _const_spec(shape):
    return pl.BlockSpec(shape, lambda c: (0,) * len(shape))


def _mixer_call(x, w_in, cw, cb, wa, ba, wx, bx, lam, scw, w_out, g1, b1, wr, br):
    n_chunks = SEQ // TS
    in_specs = [
        pl.BlockSpec(memory_space=pl.ANY),
        _const_spec((D_MODEL, D_IN_PROJ)),
        _const_spec((RG_CONV, D_RG)), _const_spec((1, D_RG)),
        _const_spec((D_RG, D_RG)), _const_spec((1, D_RG)),
        _const_spec((D_RG, D_RG)), _const_spec((1, D_RG)),
        _const_spec((1, D_RG)),
        _const_spec((SC_CONV, D_SC)),
        _const_spec((D_MODEL, D_MODEL)),
        _const_spec((1, D_MODEL)), _const_spec((1, D_MODEL)),
        _const_spec((D_MODEL, 2 * ROUTER_LANES)), _const_spec((1, ROUTER_LANES)),
    ]
    out_specs = [
        pl.BlockSpec((M1 * LANE_TILES, LANES), lambda c: (c, 0)),
        pl.BlockSpec((ROUTER_ROWS, M1), lambda c: (0, c)),
    ]
    return pl.pallas_call(
        _mixer_kernel,
        grid=(n_chunks,),
        in_specs=in_specs,
        out_specs=out_specs,
        out_shape=[
            jax.ShapeDtypeStruct((N_TOKENS * LANE_TILES, LANES), jnp.float32),
            jax.ShapeDtypeStruct((ROUTER_ROWS, N_TOKENS), jnp.float32),
        ],
        scratch_shapes=[
            pltpu.VMEM((2, TS, BATCH, D_MODEL), jnp.float32),
            pltpu.SemaphoreType.DMA((2,)),
            pltpu.VMEM((M1 + (RG_CONV - 1) * BATCH, D_RG), jnp.float32),
            pltpu.VMEM((M1 + (SC_CONV - 1) * BATCH, D_SC), jnp.float32),
            pltpu.VMEM((M1, D_RG), jnp.float32),
            pltpu.VMEM((M1, D_RG), jnp.float32),
            pltpu.VMEM((BATCH, D_RG), jnp.float32),
        ],
        compiler_params=pltpu.CompilerParams(
            dimension_semantics=("arbitrary",), vmem_limit_bytes=VMEM_LIMIT),
        name="mixer_ln_router",
    )(x, w_in, cw, cb, wa, ba, wx, bx, lam, scw, w_out, g1, b1, wr, br)


def _expert_kernel(grp_ref, ea_ref, eb_ref, rows_ref,
                   xs_ref, wt_ref, wgu_ref, wd_ref, g2_ref, b2_ref, out_ref):
    s = pl.program_id(0)
    tile_rows = [rows_ref[s * SUB + j] for j in range(SUB)]

    @pl.when(sum(tile_rows) == 0)
    def _():
        out_ref[...] = jnp.zeros_like(out_ref)

    @pl.when(sum(tile_rows) != 0)
    def _():
        for j in range(SUB):
            tile = s * SUB + j
            first_row = j * TM
            in_use = lax.broadcasted_iota(jnp.int32, (TM, 1), 0) < tile_rows[j]
            x = jnp.where(in_use, _load_row_tiles(xs_ref, first_row, TM), 0.0)
            xb = x.astype(jnp.bfloat16)
            wt = jnp.where(in_use, wt_ref[first_row:first_row + TM, :], 0.0)
            y = jnp.zeros((TM, D_MODEL), jnp.float32)
            for e_ref, col in ((ea_ref, 0), (eb_ref, 1)):
                e = e_ref[tile]
                hgu = jnp.dot(xb, wgu_ref[e], preferred_element_type=jnp.float32)
                hid = jax.nn.silu(hgu[:, :D_EXPERT]) * hgu[:, D_EXPERT:] * wt[:, col:col + 1]
                y = y + jnp.dot(hid.astype(jnp.bfloat16), wd_ref[e],
                                preferred_element_type=jnp.float32)
            _store_row_tiles(out_ref, first_row,
                             _layer_norm(DEEPNORM_ALPHA * x + y, g2_ref[...], b2_ref[...]))


def _expert_call(step_group, tile_ea, tile_eb, tile_rows, xs, wts, wgu, wd, g2, b2):
    rows = SUB * TM
    grid_spec = pltpu.PrefetchScalarGridSpec(
        num_scalar_prefetch=4,
        grid=(N_STEPS3,),
        in_specs=[
            pl.BlockSpec((rows * LANE_TILES, LANES), lambda s, grp, ea, eb, va: (s, 0)),
            pl.BlockSpec((rows, LANES), lambda s, grp, ea, eb, va: (s, 0)),
            pl.BlockSpec((None, EXPERTS_PER_GROUP, D_MODEL, 2 * D_EXPERT),
                         lambda s, grp, ea, eb, va: (grp[s], 0, 0, 0)),
            pl.BlockSpec((None, EXPERTS_PER_GROUP, D_EXPERT, D_MODEL),
                         lambda s, grp, ea, eb, va: (grp[s], 0, 0, 0)),
            pl.BlockSpec((1, D_MODEL), lambda s, grp, ea, eb, va: (0, 0)),
            pl.BlockSpec((1, D_MODEL), lambda s, grp, ea, eb, va: (0, 0)),
        ],
        out_specs=pl.BlockSpec((rows * LANE_TILES, LANES), lambda s, grp, ea, eb, va: (s, 0)),
    )
    return pl.pallas_call(
        _expert_kernel,
        grid_spec=grid_spec,
        out_shape=jax.ShapeDtypeStruct((P_ROWS * LANE_TILES, LANES), jnp.float32),
        compiler_params=pltpu.CompilerParams(
            dimension_semantics=("arbitrary",), vmem_limit_bytes=VMEM_LIMIT),
        name="experts_ln",
    )(step_group, tile_ea, tile_eb, tile_rows, xs, wts, wgu, wd, g2, b2)


SC_CORES = 2
SC_SUBCORES = 16
SC_WORKERS = SC_CORES * SC_SUBCORES
COMBINE_CHUNK = 32


def _sc_gather_rows(table, idx, chunk, out_row_shape=None):
    n_out, = idx.shape
    row_shape = table.shape[1:]
    out_row_shape = row_shape if out_row_shape is None else out_row_shape
    per_w = n_out // SC_WORKERS
    n_chunks = per_w // chunk
    assert per_w * SC_WORKERS == n_out and n_chunks * chunk == per_w
    assert n_chunks % 2 == 0 and chunk % 8 == 0
    mesh = plsc.VectorSubcoreMesh(core_axis_name="c", subcore_axis_name="s")

    @functools.partial(
        pl.kernel, mesh=mesh,
        out_type=jax.ShapeDtypeStruct((n_out,) + out_row_shape, table.dtype),
        scratch_types=[
            pltpu.VMEM((per_w,), jnp.int32),
            pltpu.VMEM((2, chunk) + row_shape, table.dtype),
            pltpu.SemaphoreType.DMA((2,)),
            pltpu.SemaphoreType.DMA((2,)),
        ],
    )
    def gather_kernel(table_hbm, idx_hbm, out_hbm, idx_v, buf, gsem, wsem):
        wid = lax.axis_index("s") * SC_CORES + lax.axis_index("c")
        base = wid * per_w
        pltpu.sync_copy(idx_hbm.at[pl.ds(base, per_w)], idx_v)

        def gather(j, slot):
            rows = idx_v.at[pl.ds(j * chunk, chunk)]
            return pltpu.make_async_copy(table_hbm.at[rows], buf.at[slot], gsem.at[slot])

        def write(j, slot):
            dst = out_hbm.at[pl.ds(base + j * chunk, chunk)]
            src = buf.at[slot].reshape((chunk,) + out_row_shape)
            return pltpu.make_async_copy(src, dst, wsem.at[slot])

        gather(0, 0).start()

        @pl.loop(0, n_chunks, step=2)
        def _(j0):
            for slot in range(2):
                j = j0 + slot
                gather(j, slot).wait()

                @pl.when(j >= 1)
                def _():
                    write(j - 1, 1 - slot).wait()

                @pl.when(j + 1 < n_chunks)
                def _():
                    gather(j + 1, 1 - slot).start()

                write(j, slot).start()

        write(n_chunks - 1, (n_chunks - 1) % 2).wait()

    return gather_kernel(table, idx)


SC_LANES = 16


def _sc_dispatch(rows, meta, cls, rank, class_start):
    n_rows = rows.shape[0]
    chunk = COMBINE_CHUNK
    per_w = n_rows // SC_WORKERS
    n_chunks = per_w // chunk
    assert per_w * SC_WORKERS == n_rows and n_chunks * chunk == per_w
    assert n_chunks % 2 == 0 and chunk % SC_LANES == 0
    mesh = plsc.VectorSubcoreMesh(core_axis_name="c", subcore_axis_name="s")

    @functools.partial(
        pl.kernel, mesh=mesh,
        out_type=[
            jax.ShapeDtypeStruct((P_ROWS,) + rows.shape[1:], rows.dtype),
            jax.ShapeDtypeStruct((P_ROWS,) + meta.shape[1:], meta.dtype),
            jax.ShapeDtypeStruct((n_rows,), jnp.int32),
        ],
        scratch_types=[
            pltpu.VMEM((per_w,), jnp.int32),
            pltpu.VMEM((per_w,), jnp.int32),
            pltpu.VMEM((LANES,), jnp.int32),
            pltpu.VMEM((per_w,), jnp.int32),
            pltpu.VMEM((n_chunks, chunk), jnp.int32),
            pltpu.VMEM((2, chunk) + rows.shape[1:], rows.dtype),
            pltpu.VMEM((2, chunk) + meta.shape[1:], meta.dtype),
            pltpu.SemaphoreType.DMA((2,)),
            pltpu.SemaphoreType.DMA((2,)),
            pltpu.SemaphoreType.DMA((2,)),
            pltpu.SemaphoreType.DMA((2,)),
        ],
        compiler_params=pltpu.CompilerParams(needs_layout_passes=False),
    )
    def dispatch_kernel(rows_hbm, meta_hbm, cls_hbm, rank_hbm, start_hbm,
                        xs_hbm, ws_hbm, pos_hbm,
                        cls_v, rank_v, start_v, pos_flat, pos_v, rbuf, mbuf,
                        rsem, msem, xsem, wsem):
        wid = lax.axis_index("s") * SC_CORES + lax.axis_index("c")
        base = wid * per_w
        pltpu.sync_copy(cls_hbm.at[pl.ds(base, per_w)], cls_v)
        pltpu.sync_copy(rank_hbm.at[pl.ds(base, per_w)], rank_v)
        pltpu.sync_copy(start_hbm, start_v)

        @pl.loop(0, n_chunks)
        def _(j):
            for k in range(chunk // SC_LANES):
                off = j * chunk + k * SC_LANES
                c = cls_v[pl.ds(off, SC_LANES)]
                pos = plsc.load_gather(start_v, [c]) + rank_v[pl.ds(off, SC_LANES)]
                pos_flat[pl.ds(off, SC_LANES)] = pos
                pos_v[j, pl.ds(k * SC_LANES, SC_LANES)] = pos

        pltpu.sync_copy(pos_flat, pos_hbm.at[pl.ds(base, per_w)])

        def read_rows(j, slot):
            src = rows_hbm.at[pl.ds(base + j * chunk, chunk)]
            return pltpu.make_async_copy(src, rbuf.at[slot], rsem.at[slot])

        def read_meta(j, slot):
            src = meta_hbm.at[pl.ds(base + j * chunk, chunk)]
            return pltpu.make_async_copy(src, mbuf.at[slot], msem.at[slot])

        def put_rows(j, slot):
            return pltpu.make_async_copy(rbuf.at[slot], xs_hbm.at[pos_v.at[j]], xsem.at[slot])

        def put_meta(j, slot):
            return pltpu.make_async_copy(mbuf.at[slot], ws_hbm.at[pos_v.at[j]], wsem.at[slot])

        read_rows(0, 0).start()
        read_meta(0, 0).start()

        @pl.loop(0, n_chunks, step=2)
        def _(j0):
            for slot in range(2):
                j = j0 + slot
                read_rows(j, slot).wait()
                read_meta(j, slot).wait()

                @pl.when(j >= 1)
                def _():
                    put_rows(j - 1, 1 - slot).wait()
                    put_meta(j - 1, 1 - slot).wait()

                @pl.when(j + 1 < n_chunks)
                def _():
                    read_rows(j + 1, 1 - slot).start()
                    read_meta(j + 1, 1 - slot).start()

                put_rows(j, slot).start()
                put_meta(j, slot).start()

        put_rows(n_chunks - 1, (n_chunks - 1) % 2).wait()
        put_meta(n_chunks - 1, (n_chunks - 1) % 2).wait()

    return dispatch_kernel(rows, meta, cls, rank, class_start)


def _block_diag(w):
    h, d, _ = w.shape
    eye = jnp.eye(h, dtype=w.dtype)
    return (eye[:, None, :, None] * w[:, :, None, :]).reshape(h * d, h * d)


def _pair_tables():
    pair_of = np.zeros((EXPERTS_PER_GROUP, EXPERTS_PER_GROUP), np.int32)
    lo = np.zeros((N_PAIRS,), np.int32)
    hi = np.zeros((N_PAIRS,), np.int32)
    p = 0
    for a in range(EXPERTS_PER_GROUP):
        for b in range(a + 1, EXPERTS_PER_GROUP):
            pair_of[a, b] = pair_of[b, a] = p
            lo[p], hi[p] = a, b
            p += 1
    return pair_of, lo, hi


def _route_kernel(lt_ref, info_ref, meta_ref, counts_ref, run_ref):
    step = pl.program_id(0)

    @pl.when(step == 0)
    def _():
        run_ref[...] = jnp.zeros_like(run_ref)

    f32 = jnp.float32
    sub8 = lax.broadcasted_iota(jnp.int32, (8, LANES), 0).astype(f32)
    row_id = lax.broadcasted_iota(jnp.int32, (LANES, LANES), 0)
    col_id = lax.broadcasted_iota(jnp.int32, (LANES, LANES), 1)
    prefix_mat = (row_id <= col_id).astype(jnp.bfloat16)
    ones_mat = jnp.ones((LANES, LANES), jnp.bfloat16)
    neg_inf = f32(-jnp.inf)

    def first_index_of_max(v):
        m = jnp.max(v, axis=0, keepdims=True)
        idx = jnp.min(jnp.where(v == m, sub8, f32(8)), axis=0, keepdims=True)
        return m, idx

    def lane_tile(k, run):
        lanes = pl.ds(pl.multiple_of(k * LANES, LANES), LANES)
        g = jnp.where(sub8 < N_GROUPS, lt_ref[0:GROUP_ROWS, lanes], neg_inf)
        g_max, g_idx = first_index_of_max(g)
        g_top_p = 1.0 / jnp.sum(jnp.exp(g - g_max), axis=0, keepdims=True)

        e_sel = lt_ref[GROUP_ROWS:GROUP_ROWS + EXPERTS_PER_GROUP, lanes]
        for grp in range(1, N_GROUPS):
            lo = GROUP_ROWS + grp * EXPERTS_PER_GROUP
            e_sel = jnp.where(g_idx == grp, lt_ref[lo:lo + EXPERTS_PER_GROUP, lanes], e_sel)
        m1, i1 = first_index_of_max(e_sel)
        rest = jnp.where(sub8 == i1, neg_inf, e_sel)
        m2 = jnp.max(rest, axis=0, keepdims=True)
        i2 = jnp.min(jnp.where((rest == m2) & (sub8 != i1), sub8, f32(8)), axis=0, keepdims=True)

        e = jnp.exp(m2 - m1)
        w1 = g_top_p / (1.0 + e)
        w2 = g_top_p * e / (1.0 + e)
        first_is_lo = i1 < i2
        w_lo = jnp.where(first_is_lo, w1, w2)
        w_hi = jnp.where(first_is_lo, w2, w1)
        lo_e = jnp.minimum(i1, i2)
        hi_e = jnp.maximum(i1, i2)
        pair = lo_e * (2 * EXPERTS_PER_GROUP - 1 - lo_e) * 0.5 + hi_e - lo_e - 1.0
        cls = (g_idx * N_PAIRS + pair).astype(jnp.int32)

        onehot = (row_id == cls).astype(jnp.bfloat16)
        csum = jnp.dot(onehot, prefix_mat, preferred_element_type=f32)
        rank = jnp.sum(onehot.astype(f32) * (csum + run), axis=0, keepdims=True) - 1.0
        run = run + jnp.dot(onehot, ones_mat, preferred_element_type=f32)

        info_ref[:, lanes] = jnp.where(sub8 == 0, cls, jnp.where(sub8 == 1, rank.astype(jnp.int32), 0))
        meta8 = jnp.where(sub8 == 0, w_lo, jnp.where(sub8 == 1, w_hi, 0.0))
        meta_t = jnp.concatenate([meta8, jnp.zeros((LANES - 8, LANES), f32)], axis=0)
        meta_ref[lanes, :] = meta_t.T
        return run

    run = lax.fori_loop(0, ROUTE_BLOCK // LANES, lane_tile, run_ref[...], unroll=ROUTE_UNROLL)
    run_ref[...] = run
    counts_ref[...] = run


def _route_call(lt):
    return pl.pallas_call(
        _route_kernel,
        grid=(N_TOKENS // ROUTE_BLOCK,),
        in_specs=[pl.BlockSpec((ROUTER_ROWS, ROUTE_BLOCK), lambda i: (0, i))],
        out_specs=[
            pl.BlockSpec((8, ROUTE_BLOCK), lambda i: (0, i)),
            pl.BlockSpec((ROUTE_BLOCK, LANES), lambda i: (i, 0)),
            pl.BlockSpec((LANES, LANES), lambda i: (0, 0)),
        ],
        out_shape=[
            jax.ShapeDtypeStruct((8, N_TOKENS), jnp.int32),
            jax.ShapeDtypeStruct((N_TOKENS, LANES), jnp.float32),
            jax.ShapeDtypeStruct((LANES, LANES), jnp.float32),
        ],
        scratch_shapes=[pltpu.VMEM((LANES, LANES), jnp.float32)],
        compiler_params=pltpu.CompilerParams(dimension_semantics=("arbitrary",)),
        name="route_rank",
    )(lt)


def _dispatch_plan(counts):
    _, pair_lo, pair_hi = _pair_tables()
    tiles_c = (counts + TM - 1) // TM
    tiles_g = tiles_c.reshape(N_GROUPS, N_PAIRS).sum(axis=1)
    tiles_g_pad = (tiles_g + SUB - 1) // SUB * SUB
    g_start = jnp.cumsum(tiles_g_pad) - tiles_g_pad
    tc = tiles_c.reshape(N_GROUPS, N_PAIRS)
    c_start = (g_start[:, None] + jnp.cumsum(tc, axis=1) - tc).reshape(N_CLASSES)
    class_start = jnp.zeros((LANES,), jnp.int32).at[:N_CLASSES].set(c_start * TM)

    tile_ids = jnp.arange(N_TILES, dtype=jnp.int32)
    c_end = c_start + tiles_c
    owner = jnp.sum((tile_ids[:, None] >= c_end[None, :]).astype(jnp.int32), axis=1)
    owner = jnp.minimum(owner, N_CLASSES - 1)
    in_class = (tile_ids >= c_start[owner]) & (tile_ids < c_end[owner])
    rows_left = counts[owner] - (tile_ids - c_start[owner]) * TM
    tile_rows = jnp.where(in_class, jnp.minimum(rows_left, TM), 0).astype(jnp.int32)
    pair = owner % N_PAIRS
    tile_ea = jnp.asarray(pair_lo)[pair]
    tile_eb = jnp.asarray(pair_hi)[pair]
    g_end = g_start + tiles_g_pad
    step_first = jnp.arange(N_STEPS3, dtype=jnp.int32) * SUB
    step_group = jnp.sum((step_first[:, None] >= g_end[None, :]).astype(jnp.int32), axis=1)
    step_group = jnp.minimum(step_group, N_GROUPS - 1)
    return class_start, tile_ea, tile_eb, tile_rows, step_group.astype(jnp.int32)


def kernel(x, w_in, rg_conv_w, rg_conv_b, rg_gate_a_w, rg_gate_a_b, rg_gate_x_w, rg_gate_x_b,
           rg_lambda, sc_conv_w, w_out, ln1_g, ln1_b, router_group_w, router_group_b,
           router_expert_w, router_expert_b, exp_w_gate, exp_w_up, exp_w_down, ln2_g, ln2_b):
    bf16 = jnp.bfloat16
    row = lambda v: v.reshape(1, -1)
    pad_g = GROUP_ROWS - N_GROUPS
    pad_e = ROUTER_LANES - ROUTER_ROWS
    wr = jnp.concatenate([router_group_w, jnp.zeros((D_MODEL, pad_g), jnp.float32),
                          router_expert_w, jnp.zeros((D_MODEL, pad_e), jnp.float32)], axis=1)
    br = jnp.concatenate([router_group_b, jnp.zeros((pad_g,), jnp.float32),
                          router_expert_b, jnp.zeros((pad_e,), jnp.float32)]).reshape(1, -1)
    wr_hi = wr.astype(bf16)
    wr_lo = (wr - wr_hi.astype(jnp.float32)).astype(bf16)

    h1, lt = _mixer_call(
        x, w_in.astype(bf16), rg_conv_w, row(rg_conv_b),
        _block_diag(rg_gate_a_w).astype(bf16), row(rg_gate_a_b),
        _block_diag(rg_gate_x_w).astype(bf16), row(rg_gate_x_b),
        row(rg_lambda), sc_conv_w, w_out.astype(bf16), row(ln1_g), row(ln1_b),
        jnp.concatenate([wr_hi, wr_lo], axis=1), br)

    info, meta, counts = _route_call(lt)
    class_start, tile_ea, tile_eb, tile_rows, step_group = _dispatch_plan(
        counts[:N_CLASSES, 0].astype(jnp.int32))

    row_tiles = (-1, LANE_TILES, LANES)
    xs, wts, pos = _sc_dispatch(h1.reshape(row_tiles), meta, info[0], info[1], class_start)

    grp_shape = (N_GROUPS, EXPERTS_PER_GROUP)
    ys = _expert_call(
        step_group, tile_ea, tile_eb, tile_rows, xs.reshape(-1, LANES), wts,
        jnp.concatenate([exp_w_gate.astype(bf16), exp_w_up.astype(bf16)], axis=2).reshape(
            grp_shape + (D_MODEL, 2 * D_EXPERT)),
        exp_w_down.astype(bf16).reshape(grp_shape + (D_EXPERT, D_MODEL)),
        row(ln2_g), row(ln2_b))

    out_rows = pos.reshape(SEQ, BATCH).T.reshape(-1)
    out = _sc_gather_rows(ys.reshape(row_tiles), out_rows, COMBINE_CHUNK, (D_MODEL,))
    return out.reshape(BATCH, SEQ, D_MODEL)
```

```python
import functools

import jax
import jax.numpy as jnp
import numpy as np
from jax import lax
from jax.experimental import pallas as pl
from jax.experimental.pallas import tpu as pltpu
from jax.experimental.pallas import tpu_sc as plsc

D_MODEL = 1024
BATCH = 16
SEQ = 2048
D_RG = 512
D_SC = 512
RG_HEADS = 8
RG_HEAD_DIM = D_RG // RG_HEADS
RG_CONV = 4
RG_C = 8.0
SC_CONV = 3
D_IN_PROJ = 2 * D_RG + 3 * D_SC
N_GROUPS = 4
EXPERTS_PER_GROUP = 8
N_EXPERTS = N_GROUPS * EXPERTS_PER_GROUP
D_EXPERT = D_MODEL // 4
LN_EPS = 1e-5
DEEPNORM_ALPHA = 2.0 ** 0.25

N_TOKENS = BATCH * SEQ
LANES = 128
LANE_TILES = D_MODEL // LANES
MXU_WIDTH = 256
ROUTER_LANES = LANES
GROUP_ROWS = 8
ROUTER_ROWS = GROUP_ROWS + N_EXPERTS

TS = 64
M1 = TS * BATCH

ROUTE_BLOCK = 4096
ROUTE_UNROLL = 4

TM = 128
SUB = 8
N_PAIRS = EXPERTS_PER_GROUP * (EXPERTS_PER_GROUP - 1) // 2
N_CLASSES = N_GROUPS * N_PAIRS
N_TILES = -(-(N_TOKENS // TM + N_CLASSES + N_GROUPS * (SUB - 1)) // SUB) * SUB
N_STEPS3 = N_TILES // SUB
P_ROWS = N_TILES * TM

VMEM_LIMIT = 56 * 1024 * 1024


def _layer_norm(z, g, b):
    mu = jnp.mean(z, axis=-1, keepdims=True)
    zc = z - mu
    var = jnp.mean(zc * zc, axis=-1, keepdims=True)
    return zc * lax.rsqrt(var + LN_EPS) * g + b


def _store_row_tiles(ref, first_row, value):
    n = value.shape[0]
    for j in range(LANE_TILES):
        ref[pl.ds(first_row * LANE_TILES + j, n, stride=LANE_TILES), :] = (
            value[:, j * LANES:(j + 1) * LANES])


def _load_row_tiles(ref, first_row, n):
    return jnp.concatenate(
        [ref[pl.ds(first_row * LANE_TILES + j, n, stride=LANE_TILES), :]
         for j in range(LANE_TILES)], axis=1)


def _mixer_kernel(x_ref, w_in_ref, cw_ref, cb_ref, wa_ref, ba_ref, wx_ref, bx_ref, lam_ref,
                  scw_ref, w_out_ref, g1_ref, b1_ref, wr_ref, br_ref,
                  h1_ref, logit_ref,
                  xbuf_ref, xsem, cbuf_ref, sbuf_ref, a_ref, u_ref, hstate_ref):
    c = pl.program_id(0)
    slot = c % 2
    rg_pad = (RG_CONV - 1) * BATCH
    sc_pad = (SC_CONV - 1) * BATCH

    def x_copies(chunk, slot_):
        return [pltpu.make_async_copy(x_ref.at[b, pl.ds(chunk * TS, TS), :],
                                      xbuf_ref.at[slot_, :, b, :], xsem.at[slot_])
                for b in range(BATCH)]

    @pl.when(c == 0)
    def _():
        for cp in x_copies(0, 0):
            cp.start()
        cbuf_ref[0:rg_pad, :] = jnp.zeros((rg_pad, D_RG), jnp.float32)
        sbuf_ref[0:sc_pad, :] = jnp.zeros((sc_pad, D_SC), jnp.float32)
        hstate_ref[...] = jnp.zeros_like(hstate_ref)

    @pl.when(c + 1 < SEQ // TS)
    def _():
        for cp in x_copies(c + 1, 1 - slot):
            cp.start()

    for cp in x_copies(c, slot):
        cp.wait()

    xt_ref = xbuf_ref.at[slot]
    xb = xt_ref[...].reshape(M1, D_MODEL).astype(jnp.bfloat16)

    def proj(lo, width):
        return jnp.dot(xb, w_in_ref[:, lo:lo + width], preferred_element_type=jnp.float32)

    cbuf_ref[rg_pad:rg_pad + M1, :] = proj(0, D_RG)
    xc = cb_ref[...] + cw_ref[0:1, :] * cbuf_ref[0:M1, :]
    for k in range(1, RG_CONV):
        xc = xc + cw_ref[k:k + 1, :] * cbuf_ref[k * BATCH:k * BATCH + M1, :]
    cbuf_ref[0:rg_pad, :] = cbuf_ref[M1:M1 + rg_pad, :]

    xcb = xc.astype(jnp.bfloat16)

    def gate(w_ref, b_ref):
        parts = [jnp.dot(xcb[:, lo:lo + MXU_WIDTH], w_ref[lo:lo + MXU_WIDTH, lo:lo + MXU_WIDTH],
                         preferred_element_type=jnp.float32)
                 for lo in range(0, D_RG, MXU_WIDTH)]
        return jax.nn.sigmoid(jnp.concatenate(parts, axis=1) + b_ref[...])

    r = gate(wa_ref, ba_ref)
    i = gate(wx_ref, bx_ref)
    log_a = (-RG_C) * r * jax.nn.softplus(-lam_ref[...])
    a_ref[...] = jnp.exp(log_a)
    th = jnp.tanh(log_a)
    u_ref[...] = jnp.sqrt(-2.0 * th / (1.0 - th)) * (i * xc)

    sc_b = proj(2 * D_RG, D_SC)
    sbuf_ref[sc_pad:sc_pad + M1, :] = proj(2 * D_RG + D_SC, D_SC) * proj(2 * D_RG + 2 * D_SC, D_SC)
    conv = scw_ref[0:1, :] * sbuf_ref[0:M1, :]
    for k in range(1, SC_CONV):
        conv = conv + scw_ref[k:k + 1, :] * sbuf_ref[k * BATCH:k * BATCH + M1, :]
    sbuf_ref[0:sc_pad, :] = sbuf_ref[M1:M1 + sc_pad, :]
    y_sc = sc_b * conv

    h = hstate_ref[...]
    for t in range(TS):
        h = a_ref[t * BATCH:(t + 1) * BATCH, :] * h + u_ref[t * BATCH:(t + 1) * BATCH, :]
        u_ref[t * BATCH:(t + 1) * BATCH, :] = h
    hstate_ref[...] = h

    y_rg = u_ref[...] * jax.nn.gelu(proj(D_RG, D_RG))

    mix = jnp.dot(y_rg.astype(jnp.bfloat16), w_out_ref[0:D_RG, :], preferred_element_type=jnp.float32)
    mix = mix + jnp.dot(y_sc.astype(jnp.bfloat16), w_out_ref[D_RG:, :], preferred_element_type=jnp.float32)

    x_rows = xt_ref[...].reshape(M1, D_MODEL)
    h1 = _layer_norm(DEEPNORM_ALPHA * x_rows + mix, g1_ref[...], b1_ref[...])
    _store_row_tiles(h1_ref, 0, h1)
    h_hi = h1.astype(jnp.bfloat16)
    h_lo = (h1 - h_hi.astype(jnp.float32)).astype(jnp.bfloat16)
    both = jnp.dot(h_hi, wr_ref[...], preferred_element_type=jnp.float32)
    logits = both[:, :ROUTER_LANES] + both[:, ROUTER_LANES:] + br_ref[...]
    logits = logits + jnp.dot(h_lo, wr_ref[:, :ROUTER_LANES], preferred_element_type=jnp.float32)
    logit_ref[...] = logits.T[:ROUTER_ROWS, :]


def _const_spec(shape):
    return pl.BlockSpec(shape, lambda c: (0,) * len(shape))


def _mixer_call(x, w_in, cw, cb, wa, ba, wx, bx, lam, scw, w_out, g1, b1, wr, br):
    n_chunks = SEQ // TS
    in_specs = [
        pl.BlockSpec(memory_space=pl.ANY),
        _const_spec((D_MODEL, D_IN_PROJ)),
        _const_spec((RG_CONV, D_RG)), _const_spec((1, D_RG)),
        _const_spec((D_RG, D_RG)), _const_spec((1, D_RG)),
        _const_spec((D_RG, D_RG)), _const_spec((1, D_RG)),
        _const_spec((1, D_RG)),
        _const_spec((SC_CONV, D_SC)),
        _const_spec((D_MODEL, D_MODEL)),
        _const_spec((1, D_MODEL)), _const_spec((1, D_MODEL)),
        _const_spec((D_MODEL, 2 * ROUTER_LANES)), _const_spec((1, ROUTER_LANES)),
    ]
    out_specs = [
        pl.BlockSpec((M1 * LANE_TILES, LANES), lambda c: (c, 0)),
        pl.BlockSpec((ROUTER_ROWS, M1), lambda c: (0, c)),
    ]
    return pl.pallas_call(
        _mixer_kernel,
        grid=(n_chunks,),
        in_specs=in_specs,
        out_specs=out_specs,
        out_shape=[
            jax.ShapeDtypeStruct((N_TOKENS * LANE_TILES, LANES), jnp.float32),
            jax.ShapeDtypeStruct((ROUTER_ROWS, N_TOKENS), jnp.float32),
        ],
        scratch_shapes=[
            pltpu.VMEM((2, TS, BATCH, D_MODEL), jnp.float32),
            pltpu.SemaphoreType.DMA((2,)),
            pltpu.VMEM((M1 + (RG_CONV - 1) * BATCH, D_RG), jnp.float32),
            pltpu.VMEM((M1 + (SC_CONV - 1) * BATCH, D_SC), jnp.float32),
            pltpu.VMEM((M1, D_RG), jnp.float32),
            pltpu.VMEM((M1, D_RG), jnp.float32),
            pltpu.VMEM((BATCH, D_RG), jnp.float32),
        ],
        compiler_params=pltpu.CompilerParams(
            dimension_semantics=("arbitrary",), vmem_limit_bytes=VMEM_LIMIT),
        name="mixer_ln_router",
    )(x, w_in, cw, cb, wa, ba, wx, bx, lam, scw, w_out, g1, b1, wr, br)


def _expert_kernel(grp_ref, ea_ref, eb_ref, rows_ref,
                   xs_ref, wt_ref, wgu_ref, wd_ref, g2_ref, b2_ref, out_ref):
    s = pl.program_id(0)
    tile_rows = [rows_ref[s * SUB + j] for j in range(SUB)]

    @pl.when(sum(tile_rows) == 0)
    def _():
        out_ref[...] = jnp.zeros_like(out_ref)

    @pl.when(sum(tile_rows) != 0)
    def _():
        for j in range(SUB):
            tile = s * SUB + j
            first_row = j * TM
            in_use = lax.broadcasted_iota(jnp.int32, (TM, 1), 0) < tile_rows[j]
            x = jnp.where(in_use, _load_row_tiles(xs_ref, first_row, TM), 0.0)
            xb = x.astype(jnp.bfloat16)
            wt = jnp.where(in_use, wt_ref[first_row:first_row + TM, :], 0.0)
            y = jnp.zeros((TM, D_MODEL), jnp.float32)
            for e_ref, col in ((ea_ref, 0), (eb_ref, 1)):
                e = e_ref[tile]
                hgu = jnp.dot(xb, wgu_ref[e], preferred_element_type=jnp.float32)
                hid = jax.nn.silu(hgu[:, :D_EXPERT]) * hgu[:, D_EXPERT:] * wt[:, col:col + 1]
                y = y + jnp.dot(hid.astype(jnp.bfloat16), wd_ref[e],
                                preferred_element_type=jnp.float32)
            _store_row_tiles(out_ref, first_row,
                             _layer_norm(DEEPNORM_ALPHA * x + y, g2_ref[...], b2_ref[...]))


def _expert_call(step_group, tile_ea, tile_eb, tile_rows, xs, wts, wgu, wd, g2, b2):
    rows = SUB * TM
    grid_spec = pltpu.PrefetchScalarGridSpec(
        num_scalar_prefetch=4,
        grid=(N_STEPS3,),
        in_specs=[
            pl.BlockSpec((rows * LANE_TILES, LANES), lambda s, grp, ea, eb, va: (s, 0)),
            pl.BlockSpec((rows, LANES), lambda s, grp, ea, eb, va: (s, 0)),
            pl.BlockSpec((None, EXPERTS_PER_GROUP, D_MODEL, 2 * D_EXPERT),
                         lambda s, grp, ea, eb, va: (grp[s], 0, 0, 0)),
            pl.BlockSpec((None, EXPERTS_PER_GROUP, D_EXPERT, D_MODEL),
                         lambda s, grp, ea, eb, va: (grp[s], 0, 0, 0)),
            pl.BlockSpec((1, D_MODEL), lambda s, grp, ea, eb, va: (0, 0)),
            pl.BlockSpec((1, D_MODEL), lambda s, grp, ea, eb, va: (0, 0)),
        ],
        out_specs=pl.BlockSpec((rows * LANE_TILES, LANES), lambda s, grp, ea, eb, va: (s, 0)),
    )
    return pl.pallas_call(
        _expert_kernel,
        grid_spec=grid_spec,
        out_shape=jax.ShapeDtypeStruct((P_ROWS * LANE_TILES, LANES), jnp.float32),
        compiler_params=pltpu.CompilerParams(
            dimension_semantics=("arbitrary",), vmem_limit_bytes=VMEM_LIMIT),
        name="experts_ln",
    )(step_group, tile_ea, tile_eb, tile_rows, xs, wts, wgu, wd, g2, b2)


CAST_EXPERTS = 2


def _cast_kernel(wg_ref, wu_ref, wd_ref, wgu_ref, wdb_ref):
    wgu_ref[:, :, :D_EXPERT] = wg_ref[...].astype(jnp.bfloat16)
    wgu_ref[:, :, D_EXPERT:] = wu_ref[...].astype(jnp.bfloat16)
    wdb_ref[...] = wd_ref[...].astype(jnp.bfloat16)


def _cast_expert_weights(wg, wu, wd):
    eb = CAST_EXPERTS
    return pl.pallas_call(
        _cast_kernel,
        grid=(N_EXPERTS // eb,),
        in_specs=[
            pl.BlockSpec((eb, D_MODEL, D_EXPERT), lambda i: (i, 0, 0)),
            pl.BlockSpec((eb, D_MODEL, D_EXPERT), lambda i: (i, 0, 0)),
            pl.BlockSpec((eb, D_EXPERT, D_MODEL), lambda i: (i, 0, 0)),
        ],
        out_specs=[
            pl.BlockSpec((eb, D_MODEL, 2 * D_EXPERT), lambda i: (i, 0, 0)),
            pl.BlockSpec((eb, D_EXPERT, D_MODEL), lambda i: (i, 0, 0)),
        ],
        out_shape=[
            jax.ShapeDtypeStruct((N_EXPERTS, D_MODEL, 2 * D_EXPERT), jnp.bfloat16),
            jax.ShapeDtypeStruct((N_EXPERTS, D_EXPERT, D_MODEL), jnp.bfloat16),
        ],
        compiler_params=pltpu.CompilerParams(dimension_semantics=("arbitrary",)),
        name="cast_expert_weights",
    )(wg, wu, wd)


SC_CORES = 2
SC_SUBCORES = 16
SC_WORKERS = SC_CORES * SC_SUBCORES
COMBINE_CHUNK = 32


def _sc_gather_rows(table, idx, chunk, out_row_shape=None):
    n_out, = idx.shape
    row_shape = table.shape[1:]
    out_row_shape = row_shape if out_row_shape is None else out_row_shape
    per_w = n_out // SC_WORKERS
    n_chunks = per_w // chunk
    assert per_w * SC_WORKERS == n_out and n_chunks * chunk == per_w
    assert n_chunks % 2 == 0 and chunk % 8 == 0
    mesh = plsc.VectorSubcoreMesh(core_axis_name="c", subcore_axis_name="s")

    @functools.partial(
        pl.kernel, mesh=mesh,
        out_type=jax.ShapeDtypeStruct((n_out,) + out_row_shape, table.dtype),
        scratch_types=[
            pltpu.VMEM((per_w,), jnp.int32),
            pltpu.VMEM((2, chunk) + row_shape, table.dtype),
            pltpu.SemaphoreType.DMA((2,)),
            pltpu.SemaphoreType.DMA((2,)),
        ],
    )
    def gather_kernel(table_hbm, idx_hbm, out_hbm, idx_v, buf, gsem, wsem):
        wid = lax.axis_index("s") * SC_CORES + lax.axis_index("c")
        base = wid * per_w
        pltpu.sync_copy(idx_hbm.at[pl.ds(base, per_w)], idx_v)

        def gather(j, slot):
            rows = idx_v.at[pl.ds(j * chunk, chunk)]
            return pltpu.make_async_copy(table_hbm.at[rows], buf.at[slot], gsem.at[slot])

        def write(j, slot):
            dst = out_hbm.at[pl.ds(base + j * chunk, chunk)]
            src = buf.at[slot].reshape((chunk,) + out_row_shape)
            return pltpu.make_async_copy(src, dst, wsem.at[slot])

        gather(0, 0).start()

        @pl.loop(0, n_chunks, step=2)
        def _(j0):
            for slot in range(2):
                j = j0 + slot
                gather(j, slot).wait()

                @pl.when(j >= 1)
                def _():
                    write(j - 1, 1 - slot).wait()

                @pl.when(j + 1 < n_chunks)
                def _():
                    gather(j + 1, 1 - slot).start()

                write(j, slot).start()

        write(n_chunks - 1, (n_chunks - 1) % 2).wait()

    return gather_kernel(table, idx)


SC_LANES = 16


def _sc_dispatch(rows, meta, cls, rank, class_start):
    n_rows = rows.shape[0]
    chunk = COMBINE_CHUNK
    per_w = n_rows // SC_WORKERS
    n_chunks = per_w // chunk
    assert per_w * SC_WORKERS == n_rows and n_chunks * chunk == per_w
    assert n_chunks % 2 == 0 and chunk % SC_LANES == 0
    mesh = plsc.VectorSubcoreMesh(core_axis_name="c", subcore_axis_name="s")

    @functools.partial(
        pl.kernel, mesh=mesh,
        out_type=[
            jax.ShapeDtypeStruct((P_ROWS,) + rows.shape[1:], rows.dtype),
            jax.ShapeDtypeStruct((P_ROWS,) + meta.shape[1:], meta.dtype),
            jax.ShapeDtypeStruct((n_rows,), jnp.int32),
        ],
        scratch_types=[
            pltpu.VMEM((per_w,), jnp.int32),
            pltpu.VMEM((per_w,), jnp.int32),
            pltpu.VMEM((LANES,), jnp.int32),
            pltpu.VMEM((per_w,), jnp.int32),
            pltpu.VMEM((n_chunks, chunk), jnp.int32),
            pltpu.VMEM((2, chunk) + rows.shape[1:], rows.dtype),
            pltpu.VMEM((2, chunk) + meta.shape[1:], meta.dtype),
            pltpu.SemaphoreType.DMA((2,)),
            pltpu.SemaphoreType.DMA((2,)),
            pltpu.SemaphoreType.DMA((2,)),
            pltpu.SemaphoreType.DMA((2,)),
        ],
        compiler_params=pltpu.CompilerParams(needs_layout_passes=False),
    )
    def dispatch_kernel(rows_hbm, meta_hbm, cls_hbm, rank_hbm, start_hbm,
                        xs_hbm, ws_hbm, pos_hbm,
                        cls_v, rank_v, start_v, pos_flat, pos_v, rbuf, mbuf,
                        rsem, msem, xsem, wsem):
        wid = lax.axis_index("s") * SC_CORES + lax.axis_index("c")
        base = wid * per_w
        pltpu.sync_copy(cls_hbm.at[pl.ds(base, per_w)], cls_v)
        pltpu.sync_copy(rank_hbm.at[pl.ds(base, per_w)], rank_v)
        pltpu.sync_copy(start_hbm, start_v)

        @pl.loop(0, n_chunks)
        def _(j):
            for k in range(chunk // SC_LANES):
                off = j * chunk + k * SC_LANES
                c = cls_v[pl.ds(off, SC_LANES)]
                pos = plsc.load_gather(start_v, [c]) + rank_v[pl.ds(off, SC_LANES)]
                pos_flat[pl.ds(off, SC_LANES)] = pos
                pos_v[j, pl.ds(k * SC_LANES, SC_LANES)] = pos

        pltpu.sync_copy(pos_flat, pos_hbm.at[pl.ds(base, per_w)])

        def read_rows(j, slot):
            src = rows_hbm.at[pl.ds(base + j * chunk, chunk)]
            return pltpu.make_async_copy(src, rbuf.at[slot], rsem.at[slot])

        def read_meta(j, slot):
            src = meta_hbm.at[pl.ds(base + j * chunk, chunk)]
            return pltpu.make_async_copy(src, mbuf.at[slot], msem.at[slot])

        def put_rows(j, slot):
            return pltpu.make_async_copy(rbuf.at[slot], xs_hbm.at[pos_v.at[j]], xsem.at[slot])

        def put_meta(j, slot):
            return pltpu.make_async_copy(mbuf.at[slot], ws_hbm.at[pos_v.at[j]], wsem.at[slot])

        read_rows(0, 0).start()
        read_meta(0, 0).start()

        @pl.loop(0, n_chunks, step=2)
        def _(j0):
            for slot in range(2):
                j = j0 + slot
                read_rows(j, slot).wait()
                read_meta(j, slot).wait()

                @pl.when(j >= 1)
                def _():
                    put_rows(j - 1, 1 - slot).wait()
                    put_meta(j - 1, 1 - slot).wait()

                @pl.when(j + 1 < n_chunks)
                def _():
                    read_rows(j + 1, 1 - slot).start()
                    read_meta(j + 1, 1 - slot).start()

                put_rows(j, slot).start()
                put_meta(j, slot).start()

        put_rows(n_chunks - 1, (n_chunks - 1) % 2).wait()
        put_meta(n_chunks - 1, (n_chunks - 1) % 2).wait()

    return dispatch_kernel(rows, meta, cls, rank, class_start)


def _block_diag(w):
    h, d, _ = w.shape
    eye = jnp.eye(h, dtype=w.dtype)
    return (eye[:, None, :, None] * w[:, :, None, :]).reshape(h * d, h * d)


def _pair_tables():
    pair_of = np.zeros((EXPERTS_PER_GROUP, EXPERTS_PER_GROUP), np.int32)
    lo = np.zeros((N_PAIRS,), np.int32)
    hi = np.zeros((N_PAIRS,), np.int32)
    p = 0
    for a in range(EXPERTS_PER_GROUP):
        for b in range(a + 1, EXPERTS_PER_GROUP):
            pair_of[a, b] = pair_of[b, a] = p
            lo[p], hi[p] = a, b
            p += 1
    return pair_of, lo, hi


def _route_kernel(lt_ref, info_ref, meta_ref, counts_ref, run_ref):
    step = pl.program_id(0)

    @pl.when(step == 0)
    def _():
        run_ref[...] = jnp.zeros_like(run_ref)

    f32 = jnp.float32
    sub8 = lax.broadcasted_iota(jnp.int32, (8, LANES), 0).astype(f32)
    row_id = lax.broadcasted_iota(jnp.int32, (LANES, LANES), 0)
    col_id = lax.broadcasted_iota(jnp.int32, (LANES, LANES), 1)
    prefix_mat = (row_id <= col_id).astype(jnp.bfloat16)
    ones_mat = jnp.ones((LANES, LANES), jnp.bfloat16)
    neg_inf = f32(-jnp.inf)

    def first_index_of_max(v):
        m = jnp.max(v, axis=0, keepdims=True)
        idx = jnp.min(jnp.where(v == m, sub8, f32(8)), axis=0, keepdims=True)
        return m, idx

    def lane_tile(k, run):
        lanes = pl.ds(pl.multiple_of(k * LANES, LANES), LANES)
        g = jnp.where(sub8 < N_GROUPS, lt_ref[0:GROUP_ROWS, lanes], neg_inf)
        g_max, g_idx = first_index_of_max(g)
        g_top_p = 1.0 / jnp.sum(jnp.exp(g - g_max), axis=0, keepdims=True)

        e_sel = lt_ref[GROUP_ROWS:GROUP_ROWS + EXPERTS_PER_GROUP, lanes]
        for grp in range(1, N_GROUPS):
            lo = GROUP_ROWS + grp * EXPERTS_PER_GROUP
            e_sel = jnp.where(g_idx == grp, lt_ref[lo:lo + EXPERTS_PER_GROUP, lanes], e_sel)
        m1, i1 = first_index_of_max(e_sel)
        rest = jnp.where(sub8 == i1, neg_inf, e_sel)
        m2 = jnp.max(rest, axis=0, keepdims=True)
        i2 = jnp.min(jnp.where((rest == m2) & (sub8 != i1), sub8, f32(8)), axis=0, keepdims=True)

        e = jnp.exp(m2 - m1)
        w1 = g_top_p / (1.0 + e)
        w2 = g_top_p * e / (1.0 + e)
        first_is_lo = i1 < i2
        w_lo = jnp.where(first_is_lo, w1, w2)
        w_hi = jnp.where(first_is_lo, w2, w1)
        lo_e = jnp.minimum(i1, i2)
        hi_e = jnp.maximum(i1, i2)
        pair = lo_e * (2 * EXPERTS_PER_GROUP - 1 - lo_e) * 0.5 + hi_e - lo_e - 1.0
        cls = (g_idx * N_PAIRS + pair).astype(jnp.int32)

        onehot = (row_id == cls).astype(jnp.bfloat16)
        csum = jnp.dot(onehot, prefix_mat, preferred_element_type=f32)
        rank = jnp.sum(onehot.astype(f32) * (csum + run), axis=0, keepdims=True) - 1.0
        run = run + jnp.dot(onehot, ones_mat, preferred_element_type=f32)

        info_ref[:, lanes] = jnp.where(sub8 == 0, cls, jnp.where(sub8 == 1, rank.astype(jnp.int32), 0))
        meta8 = jnp.where(sub8 == 0, w_lo, jnp.where(sub8 == 1, w_hi, 0.0))
        meta_t = jnp.concatenate([meta8, jnp.zeros((LANES - 8, LANES), f32)], axis=0)
        meta_ref[lanes, :] = meta_t.T
        return run

    run = lax.fori_loop(0, ROUTE_BLOCK // LANES, lane_tile, run_ref[...], unroll=ROUTE_UNROLL)
    run_ref[...] = run
    counts_ref[...] = run


def _route_call(lt):
    return pl.pallas_call(
        _route_kernel,
        grid=(N_TOKENS // ROUTE_BLOCK,),
        in_specs=[pl.BlockSpec((ROUTER_ROWS, ROUTE_BLOCK), lambda i: (0, i))],
        out_specs=[
            pl.BlockSpec((8, ROUTE_BLOCK), lambda i: (0, i)),
            pl.BlockSpec((ROUTE_BLOCK, LANES), lambda i: (i, 0)),
            pl.BlockSpec((LANES, LANES), lambda i: (0, 0)),
        ],
        out_shape=[
            jax.ShapeDtypeStruct((8, N_TOKENS), jnp.int32),
            jax.ShapeDtypeStruct((N_TOKENS, LANES), jnp.float32),
            jax.ShapeDtypeStruct((LANES, LANES), jnp.float32),
        ],
        scratch_shapes=[pltpu.VMEM((LANES, LANES), jnp.float32)],
        compiler_params=pltpu.CompilerParams(dimension_semantics=("arbitrary",)),
        name="route_rank",
    )(lt)


def _dispatch_plan(counts):
    _, pair_lo, pair_hi = _pair_tables()
    tiles_c = (counts + TM - 1) // TM
    tiles_g = tiles_c.reshape(N_GROUPS, N_PAIRS).sum(axis=1)
    tiles_g_pad = (tiles_g + SUB - 1) // SUB * SUB
    g_start = jnp.cumsum(tiles_g_pad) - tiles_g_pad
    tc = tiles_c.reshape(N_GROUPS, N_PAIRS)
    c_start = (g_start[:, None] + jnp.cumsum(tc, axis=1) - tc).reshape(N_CLASSES)
    class_start = jnp.zeros((LANES,), jnp.int32).at[:N_CLASSES].set(c_start * TM)

    tile_ids = jnp.arange(N_TILES, dtype=jnp.int32)
    c_end = c_start + tiles_c
    owner = jnp.sum((tile_ids[:, None] >= c_end[None, :]).astype(jnp.int32), axis=1)
    owner = jnp.minimum(owner, N_CLASSES - 1)
    in_class = (tile_ids >= c_start[owner]) & (tile_ids < c_end[owner])
    rows_left = counts[owner] - (tile_ids - c_start[owner]) * TM
    tile_rows = jnp.where(in_class, jnp.minimum(rows_left, TM), 0).astype(jnp.int32)
    pair = owner % N_PAIRS
    tile_ea = jnp.asarray(pair_lo)[pair]
    tile_eb = jnp.asarray(pair_hi)[pair]
    g_end = g_start + tiles_g_pad
    step_first = jnp.arange(N_STEPS3, dtype=jnp.int32) * SUB
    step_group = jnp.sum((step_first[:, None] >= g_end[None, :]).astype(jnp.int32), axis=1)
    step_group = jnp.minimum(step_group, N_GROUPS - 1)
    return class_start, tile_ea, tile_eb, tile_rows, step_group.astype(jnp.int32)


def kernel(x, w_in, rg_conv_w, rg_conv_b, rg_gate_a_w, rg_gate_a_b, rg_gate_x_w, rg_gate_x_b,
           rg_lambda, sc_conv_w, w_out, ln1_g, ln1_b, router_group_w, router_group_b,
           router_expert_w, router_expert_b, exp_w_gate, exp_w_up, exp_w_down, ln2_g, ln2_b):
    bf16 = jnp.bfloat16
    row = lambda v: v.reshape(1, -1)
    pad_g = GROUP_ROWS - N_GROUPS
    pad_e = ROUTER_LANES - ROUTER_ROWS
    wr = jnp.concatenate([router_group_w, jnp.zeros((D_MODEL, pad_g), jnp.float32),
                          router_expert_w, jnp.zeros((D_MODEL, pad_e), jnp.float32)], axis=1)
    br = jnp.concatenate([router_group_b, jnp.zeros((pad_g,), jnp.float32),
                          router_expert_b, jnp.zeros((pad_e,), jnp.float32)]).reshape(1, -1)
    wr_hi = wr.astype(bf16)
    wr_lo = (wr - wr_hi.astype(jnp.float32)).astype(bf16)

    h1, lt = _mixer_call(
        x, w_in.astype(bf16), rg_conv_w, row(rg_conv_b),
        _block_diag(rg_gate_a_w).astype(bf16), row(rg_gate_a_b),
        _block_diag(rg_gate_x_w).astype(bf16), row(rg_gate_x_b),
        row(rg_lambda), sc_conv_w, w_out.astype(bf16), row(ln1_g), row(ln1_b),
        jnp.concatenate([wr_hi, wr_lo], axis=1), br)

    info, meta, counts = _route_call(lt)
    class_start, tile_ea, tile_eb, tile_rows, step_group = _dispatch_plan(
        counts[:N_CLASSES, 0].astype(jnp.int32))

    row_tiles = (-1, LANE_TILES, LANES)
    xs, wts, pos = _sc_dispatch(h1.reshape(row_tiles), meta, info[0], info[1], class_start)

    grp_shape = (N_GROUPS, EXPERTS_PER_GROUP)
    wgu, wdb = _cast_expert_weights(exp_w_gate, exp_w_up, exp_w_down)
    ys = _expert_call(
        step_group, tile_ea, tile_eb, tile_rows, xs.reshape(-1, LANES), wts,
        wgu.reshape(grp_shape + (D_MODEL, 2 * D_EXPERT)),
        wdb.reshape(grp_shape + (D_EXPERT, D_MODEL)),
        row(ln2_g), row(ln2_b))

    out_rows = pos.reshape(SEQ, BATCH).T.reshape(-1)
    out = _sc_gather_rows(ys.reshape(row_tiles), out_rows, COMBINE_CHUNK, (D_MODEL,))
    return out.reshape(BATCH, SEQ, D_MODEL)
```

```python
import functools

import jax
import jax.numpy as jnp
import numpy as np
from jax import lax
from jax.experimental import pallas as pl
from jax.experimental.pallas import tpu as pltpu
from jax.experimental.pallas import tpu_sc as plsc

D_MODEL = 1024
BATCH = 16
SEQ = 2048
D_RG = 512
D_SC = 512
RG_HEADS = 8
RG_HEAD_DIM = D_RG // RG_HEADS
RG_CONV = 4
RG_C = 8.0
SC_CONV = 3
D_IN_PROJ = 2 * D_RG + 3 * D_SC
N_GROUPS = 4
EXPERTS_PER_GROUP = 8
N_EXPERTS = N_GROUPS * EXPERTS_PER_GROUP
D_EXPERT = D_MODEL // 4
LN_EPS = 1e-5
DEEPNORM_ALPHA = 2.0 ** 0.25

N_TOKENS = BATCH * SEQ
LANES = 128
LANE_TILES = D_MODEL // LANES
MXU_WIDTH = 256
ROUTER_LANES = LANES
GROUP_ROWS = 8
ROUTER_ROWS = GROUP_ROWS + N_EXPERTS

TS = 64
M1 = TS * BATCH

ROUTE_BLOCK = 4096
ROUTE_UNROLL = 4

N_PAIRS = EXPERTS_PER_GROUP * (EXPERTS_PER_GROUP - 1) // 2
N_CLASSES = N_GROUPS * N_PAIRS
BF16_ROWS = 16
TM = -(-int(N_TOKENS / N_CLASSES * 1.09 / 2) // BF16_ROWS) * BF16_ROWS
SUB = 8
N_TILES = -(-(N_TOKENS // TM + N_CLASSES + N_GROUPS * (SUB - 1)) // SUB) * SUB
N_STEPS3 = N_TILES // SUB
P_ROWS = N_TILES * TM

VMEM_LIMIT = 56 * 1024 * 1024


def _layer_norm(z, g, b):
    mu = jnp.mean(z, axis=-1, keepdims=True)
    zc = z - mu
    var = jnp.mean(zc * zc, axis=-1, keepdims=True)
    return zc * lax.rsqrt(var + LN_EPS) * g + b


def _store_row_tiles(ref, first_row, value):
    n = value.shape[0]
    for j in range(LANE_TILES):
        ref[pl.ds(first_row * LANE_TILES + j, n, stride=LANE_TILES), :] = (
            value[:, j * LANES:(j + 1) * LANES])


def _load_row_tiles(ref, first_row, n):
    return jnp.concatenate(
        [ref[pl.ds(first_row * LANE_TILES + j, n, stride=LANE_TILES), :]
         for j in range(LANE_TILES)], axis=1)


def _mixer_kernel(x_ref, w_in_ref, cw_ref, cb_ref, wa_ref, ba_ref, wx_ref, bx_ref, lam_ref,
                  scw_ref, w_out_ref, g1_ref, b1_ref, wr_ref, br_ref,
                  h1_ref, logit_ref,
                  xbuf_ref, xsem, cbuf_ref, sbuf_ref, a_ref, u_ref, hstate_ref):
    c = pl.program_id(0)
    slot = c % 2
    rg_pad = (RG_CONV - 1) * BATCH
    sc_pad = (SC_CONV - 1) * BATCH

    def x_copies(chunk, slot_):
        return [pltpu.make_async_copy(x_ref.at[b, pl.ds(chunk * TS, TS), :],
                                      xbuf_ref.at[slot_, :, b, :], xsem.at[slot_])
                for b in range(BATCH)]

    @pl.when(c == 0)
    def _():
        for cp in x_copies(0, 0):
            cp.start()
        cbuf_ref[0:rg_pad, :] = jnp.zeros((rg_pad, D_RG), jnp.float32)
        sbuf_ref[0:sc_pad, :] = jnp.zeros((sc_pad, D_SC), jnp.float32)
        hstate_ref[...] = jnp.zeros_like(hstate_ref)

    @pl.when(c + 1 < SEQ // TS)
    def _():
        for cp in x_copies(c + 1, 1 - slot):
            cp.start()

    for cp in x_copies(c, slot):
        cp.wait()

    xt_ref = xbuf_ref.at[slot]
    xb = xt_ref[...].reshape(M1, D_MODEL).astype(jnp.bfloat16)

    def proj(lo, width):
        return jnp.dot(xb, w_in_ref[:, lo:lo + width], preferred_element_type=jnp.float32)

    cbuf_ref[rg_pad:rg_pad + M1, :] = proj(0, D_RG)
    xc = cb_ref[...] + cw_ref[0:1, :] * cbuf_ref[0:M1, :]
    for k in range(1, RG_CONV):
        xc = xc + cw_ref[k:k + 1, :] * cbuf_ref[k * BATCH:k * BATCH + M1, :]
    cbuf_ref[0:rg_pad, :] = cbuf_ref[M1:M1 + rg_pad, :]

    xcb = xc.astype(jnp.bfloat16)

    def gate(w_ref, b_ref):
        parts = [jnp.dot(xcb[:, lo:lo + MXU_WIDTH], w_ref[lo:lo + MXU_WIDTH, lo:lo + MXU_WIDTH],
                         preferred_element_type=jnp.float32)
                 for lo in range(0, D_RG, MXU_WIDTH)]
        return 0.5 + 0.5 * jnp.tanh(0.5 * (jnp.concatenate(parts, axis=1) + b_ref[...]))

    r = gate(wa_ref, ba_ref)
    i = gate(wx_ref, bx_ref)
    log_a = (-RG_C) * r * jax.nn.softplus(-lam_ref[...])
    a_ref[...] = jnp.exp(log_a)
    th = jnp.tanh(log_a)
    u_ref[...] = jnp.sqrt(-2.0 * th / (1.0 - th)) * (i * xc)

    sc_b = proj(2 * D_RG, D_SC)
    sbuf_ref[sc_pad:sc_pad + M1, :] = proj(2 * D_RG + D_SC, D_SC) * proj(2 * D_RG + 2 * D_SC, D_SC)
    conv = scw_ref[0:1, :] * sbuf_ref[0:M1, :]
    for k in range(1, SC_CONV):
        conv = conv + scw_ref[k:k + 1, :] * sbuf_ref[k * BATCH:k * BATCH + M1, :]
    sbuf_ref[0:sc_pad, :] = sbuf_ref[M1:M1 + sc_pad, :]
    y_sc = sc_b * conv

    h = hstate_ref[...]
    for t in range(TS):
        h = a_ref[t * BATCH:(t + 1) * BATCH, :] * h + u_ref[t * BATCH:(t + 1) * BATCH, :]
        u_ref[t * BATCH:(t + 1) * BATCH, :] = h
    hstate_ref[...] = h

    y_rg = u_ref[...] * jax.nn.gelu(proj(D_RG, D_RG))

    mix = jnp.dot(y_rg.astype(jnp.bfloat16), w_out_ref[0:D_RG, :], preferred_element_type=jnp.float32)
    mix = mix + jnp.dot(y_sc.astype(jnp.bfloat16), w_out_ref[D_RG:, :], preferred_element_type=jnp.float32)

    x_rows = xt_ref[...].reshape(M1, D_MODEL)
    h1 = _layer_norm(DEEPNORM_ALPHA * x_rows + mix, g1_ref[...], b1_ref[...])
    _store_row_tiles(h1_ref, 0, h1)
    h_hi = h1.astype(jnp.bfloat16)
    h_lo = (h1 - h_hi.astype(jnp.float32)).astype(jnp.bfloat16)
    both = jnp.dot(h_hi, wr_ref[...], preferred_element_type=jnp.float32)
    logits = both[:, :ROUTER_LANES] + both[:, ROUTER_LANES:] + br_ref[...]
    logits = logits + jnp.dot(h_lo, wr_ref[:, :ROUTER_LANES], preferred_element_type=jnp.float32)
    logit_ref[...] = logits.T[:ROUTER_ROWS, :]


def _const_spec(shape):
    return pl.BlockSpec(shape, lambda c: (0,) * len(shape))


def _mixer_call(x, w_in, cw, cb, wa, ba, wx, bx, lam, scw, w_out, g1, b1, wr, br):
    n_chunks = SEQ // TS
    in_specs = [
        pl.BlockSpec(memory_space=pl.ANY),
        _const_spec((D_MODEL, D_IN_PROJ)),
        _const_spec((RG_CONV, D_RG)), _const_spec((1, D_RG)),
        _const_spec((D_RG, D_RG)), _const_spec((1, D_RG)),
        _const_spec((D_RG, D_RG)), _const_spec((1, D_RG)),
        _const_spec((1, D_RG)),
        _const_spec((SC_CONV, D_SC)),
        _const_spec((D_MODEL, D_MODEL)),
        _const_spec((1, D_MODEL)), _const_spec((1, D_MODEL)),
        _const_spec((D_MODEL, 2 * ROUTER_LANES)), _const_spec((1, ROUTER_LANES)),
    ]
    out_specs = [
        pl.BlockSpec((M1 * LANE_TILES, LANES), lambda c: (c, 0)),
        pl.BlockSpec((ROUTER_ROWS, M1), lambda c: (0, c)),
    ]
    return pl.pallas_call(
        _mixer_kernel,
        grid=(n_chunks,),
        in_specs=in_specs,
        out_specs=out_specs,
        out_shape=[
            jax.ShapeDtypeStruct((N_TOKENS * LANE_TILES, LANES), jnp.float32),
            jax.ShapeDtypeStruct((ROUTER_ROWS, N_TOKENS), jnp.float32),
        ],
        scratch_shapes=[
            pltpu.VMEM((2, TS, BATCH, D_MODEL), jnp.float32),
            pltpu.SemaphoreType.DMA((2,)),
            pltpu.VMEM((M1 + (RG_CONV - 1) * BATCH, D_RG), jnp.float32),
            pltpu.VMEM((M1 + (SC_CONV - 1) * BATCH, D_SC), jnp.float32),
            pltpu.VMEM((M1, D_RG), jnp.float32),
            pltpu.VMEM((M1, D_RG), jnp.float32),
            pltpu.VMEM((BATCH, D_RG), jnp.float32),
        ],
        compiler_params=pltpu.CompilerParams(
            dimension_semantics=("arbitrary",), vmem_limit_bytes=VMEM_LIMIT),
        name="mixer_ln_router",
    )(x, w_in, cw, cb, wa, ba, wx, bx, lam, scw, w_out, g1, b1, wr, br)


def _expert_kernel(grp_ref, ea_ref, eb_ref, rows_ref,
                   xs_ref, wt_ref, wgu_ref, wd_ref, g2_ref, b2_ref, out_ref):
    s = pl.program_id(0)
    tile_rows = [rows_ref[s * SUB + j] for j in range(SUB)]

    @pl.when(sum(tile_rows) == 0)
    def _():
        out_ref[...] = jnp.zeros_like(out_ref)

    @pl.when(sum(tile_rows) != 0)
    def _():
        for j in range(SUB):
            tile = s * SUB + j
            first_row = j * TM
            in_use = lax.broadcasted_iota(jnp.int32, (TM, 1), 0) < tile_rows[j]
            x = jnp.where(in_use, _load_row_tiles(xs_ref, first_row, TM), 0.0)
            xb = x.astype(jnp.bfloat16)
            wt = jnp.where(in_use, wt_ref[first_row:first_row + TM, :], 0.0)
            y = jnp.zeros((TM, D_MODEL), jnp.float32)
            for e_ref, col in ((ea_ref, 0), (eb_ref, 1)):
                e = e_ref[tile]
                hgu = jnp.dot(xb, wgu_ref[e], preferred_element_type=jnp.float32)
                hid = jax.nn.silu(hgu[:, :D_EXPERT]) * hgu[:, D_EXPERT:] * wt[:, col:col + 1]
                y = y + jnp.dot(hid.astype(jnp.bfloat16), wd_ref[e],
                                preferred_element_type=jnp.float32)
            _store_row_tiles(out_ref, first_row,
                             _layer_norm(DEEPNORM_ALPHA * x + y, g2_ref[...], b2_ref[...]))


def _expert_call(step_group, tile_ea, tile_eb, tile_rows, xs, wts, wgu, wd, g2, b2):
    rows = SUB * TM
    grid_spec = pltpu.PrefetchScalarGridSpec(
        num_scalar_prefetch=4,
        grid=(N_STEPS3,),
        in_specs=[
            pl.BlockSpec((rows * LANE_TILES, LANES), lambda s, grp, ea, eb, va: (s, 0)),
            pl.BlockSpec((rows, LANES), lambda s, grp, ea, eb, va: (s, 0)),
            pl.BlockSpec((None, EXPERTS_PER_GROUP, D_MODEL, 2 * D_EXPERT),
                         lambda s, grp, ea, eb, va: (grp[s], 0, 0, 0)),
            pl.BlockSpec((None, EXPERTS_PER_GROUP, D_EXPERT, D_MODEL),
                         lambda s, grp, ea, eb, va: (grp[s], 0, 0, 0)),
            pl.BlockSpec((1, D_MODEL), lambda s, grp, ea, eb, va: (0, 0)),
            pl.BlockSpec((1, D_MODEL), lambda s, grp, ea, eb, va: (0, 0)),
        ],
        out_specs=pl.BlockSpec((rows * LANE_TILES, LANES), lambda s, grp, ea, eb, va: (s, 0)),
    )
    return pl.pallas_call(
        _expert_kernel,
        grid_spec=grid_spec,
        out_shape=jax.ShapeDtypeStruct((P_ROWS * LANE_TILES, LANES), jnp.float32),
        compiler_params=pltpu.CompilerParams(
            dimension_semantics=("arbitrary",), vmem_limit_bytes=VMEM_LIMIT),
        name="experts_ln",
    )(step_group, tile_ea, tile_eb, tile_rows, xs, wts, wgu, wd, g2, b2)


CAST_EXPERTS = 2


def _cast_kernel(wg_ref, wu_ref, wd_ref, wgu_ref, wdb_ref):
    wgu_ref[:, :, :D_EXPERT] = wg_ref[...].astype(jnp.bfloat16)
    wgu_ref[:, :, D_EXPERT:] = wu_ref[...].astype(jnp.bfloat16)
    wdb_ref[...] = wd_ref[...].astype(jnp.bfloat16)


def _cast_expert_weights(wg, wu, wd):
    eb = CAST_EXPERTS
    return pl.pallas_call(
        _cast_kernel,
        grid=(N_EXPERTS // eb,),
        in_specs=[
            pl.BlockSpec((eb, D_MODEL, D_EXPERT), lambda i: (i, 0, 0)),
            pl.BlockSpec((eb, D_MODEL, D_EXPERT), lambda i: (i, 0, 0)),
            pl.BlockSpec((eb, D_EXPERT, D_MODEL), lambda i: (i, 0, 0)),
        ],
        out_specs=[
            pl.BlockSpec((eb, D_MODEL, 2 * D_EXPERT), lambda i: (i, 0, 0)),
            pl.BlockSpec((eb, D_EXPERT, D_MODEL), lambda i: (i, 0, 0)),
        ],
        out_shape=[
            jax.ShapeDtypeStruct((N_EXPERTS, D_MODEL, 2 * D_EXPERT), jnp.bfloat16),
            jax.ShapeDtypeStruct((N_EXPERTS, D_EXPERT, D_MODEL), jnp.bfloat16),
        ],
        compiler_params=pltpu.CompilerParams(dimension_semantics=("arbitrary",)),
        name="cast_expert_weights",
    )(wg, wu, wd)


SC_CORES = 2
SC_SUBCORES = 16
SC_WORKERS = SC_CORES * SC_SUBCORES
COMBINE_CHUNK = 32


def _sc_gather_rows(table, idx, chunk, out_row_shape=None):
    n_out, = idx.shape
    row_shape = table.shape[1:]
    out_row_shape = row_shape if out_row_shape is None else out_row_shape
    per_w = n_out // SC_WORKERS
    n_chunks = per_w // chunk
    assert per_w * SC_WORKERS == n_out and n_chunks * chunk == per_w
    assert n_chunks % 2 == 0 and chunk % 8 == 0
    mesh = plsc.VectorSubcoreMesh(core_axis_name="c", subcore_axis_name="s")

    @functools.partial(
        pl.kernel, mesh=mesh,
        out_type=jax.ShapeDtypeStruct((n_out,) + out_row_shape, table.dtype),
        scratch_types=[
            pltpu.VMEM((per_w,), jnp.int32),
            pltpu.VMEM((2, chunk) + row_shape, table.dtype),
            pltpu.SemaphoreType.DMA((2,)),
            pltpu.SemaphoreType.DMA((2,)),
        ],
    )
    def gather_kernel(table_hbm, idx_hbm, out_hbm, idx_v, buf, gsem, wsem):
        wid = lax.axis_index("s") * SC_CORES + lax.axis_index("c")
        base = wid * per_w
        pltpu.sync_copy(idx_hbm.at[pl.ds(base, per_w)], idx_v)

        def gather(j, slot):
            rows = idx_v.at[pl.ds(j * chunk, chunk)]
            return pltpu.make_async_copy(table_hbm.at[rows], buf.at[slot], gsem.at[slot])

        def write(j, slot):
            dst = out_hbm.at[pl.ds(base + j * chunk, chunk)]
            src = buf.at[slot].reshape((chunk,) + out_row_shape)
            return pltpu.make_async_copy(src, dst, wsem.at[slot])

        gather(0, 0).start()

        @pl.loop(0, n_chunks, step=2)
        def _(j0):
            for slot in range(2):
                j = j0 + slot
                gather(j, slot).wait()

                @pl.when(j >= 1)
                def _():
                    write(j - 1, 1 - slot).wait()

                @pl.when(j + 1 < n_chunks)
                def _():
                    gather(j + 1, 1 - slot).start()

                write(j, slot).start()

        write(n_chunks - 1, (n_chunks - 1) % 2).wait()

    return gather_kernel(table, idx)


SC_LANES = 16


def _sc_dispatch(rows, meta, cls, rank, class_start):
    n_rows = rows.shape[0]
    chunk = COMBINE_CHUNK
    per_w = n_rows // SC_WORKERS
    n_chunks = per_w // chunk
    assert per_w * SC_WORKERS == n_rows and n_chunks * chunk == per_w
    assert n_chunks % 2 == 0 and chunk % SC_LANES == 0
    mesh = plsc.VectorSubcoreMesh(core_axis_name="c", subcore_axis_name="s")

    @functools.partial(
        pl.kernel, mesh=mesh,
        out_type=[
            jax.ShapeDtypeStruct((P_ROWS,) + rows.shape[1:], rows.dtype),
            jax.ShapeDtypeStruct((P_ROWS,) + meta.shape[1:], meta.dtype),
            jax.ShapeDtypeStruct((n_rows,), jnp.int32),
        ],
        scratch_types=[
            pltpu.VMEM((per_w,), jnp.int32),
            pltpu.VMEM((per_w,), jnp.int32),
            pltpu.VMEM((LANES,), jnp.int32),
            pltpu.VMEM((per_w,), jnp.int32),
            pltpu.VMEM((n_chunks, chunk), jnp.int32),
            pltpu.VMEM((2, chunk) + rows.shape[1:], rows.dtype),
            pltpu.VMEM((2, chunk) + meta.shape[1:], meta.dtype),
            pltpu.SemaphoreType.DMA((2,)),
            pltpu.SemaphoreType.DMA((2,)),
            pltpu.SemaphoreType.DMA((2,)),
            pltpu.SemaphoreType.DMA((2,)),
        ],
        compiler_params=pltpu.CompilerParams(needs_layout_passes=False),
    )
    def dispatch_kernel(rows_hbm, meta_hbm, cls_hbm, rank_hbm, start_hbm,
                        xs_hbm, ws_hbm, pos_hbm,
                        cls_v, rank_v, start_v, pos_flat, pos_v, rbuf, mbuf,
                        rsem, msem, xsem, wsem):
        wid = lax.axis_index("s") * SC_CORES + lax.axis_index("c")
        base = wid * per_w
        pltpu.sync_copy(cls_hbm.at[pl.ds(base, per_w)], cls_v)
        pltpu.sync_copy(rank_hbm.at[pl.ds(base, per_w)], rank_v)
        pltpu.sync_copy(start_hbm, start_v)

        @pl.loop(0, n_chunks)
        def _(j):
            for k in range(chunk // SC_LANES):
                off = j * chunk + k * SC_LANES
                c = cls_v[pl.ds(off, SC_LANES)]
                pos = plsc.load_gather(start_v, [c]) + rank_v[pl.ds(off, SC_LANES)]
                pos_flat[pl.ds(off, SC_LANES)] = pos
                pos_v[j, pl.ds(k * SC_LANES, SC_LANES)] = pos

        pltpu.sync_copy(pos_flat, pos_hbm.at[pl.ds(base, per_w)])

        def read_rows(j, slot):
            src = rows_hbm.at[pl.ds(base + j * chunk, chunk)]
            return pltpu.make_async_copy(src, rbuf.at[slot], rsem.at[slot])

        def read_meta(j, slot):
            src = meta_hbm.at[pl.ds(base + j * chunk, chunk)]
            return pltpu.make_async_copy(src, mbuf.at[slot], msem.at[slot])

        def put_rows(j, slot):
            return pltpu.make_async_copy(rbuf.at[slot], xs_hbm.at[pos_v.at[j]], xsem.at[slot])

        def put_meta(j, slot):
            return pltpu.make_async_copy(mbuf.at[slot], ws_hbm.at[pos_v.at[j]], wsem.at[slot])

        read_rows(0, 0).start()
        read_meta(0, 0).start()

        @pl.loop(0, n_chunks, step=2)
        def _(j0):
            for slot in range(2):
                j = j0 + slot
                read_rows(j, slot).wait()
                read_meta(j, slot).wait()

                @pl.when(j >= 1)
                def _():
                    put_rows(j - 1, 1 - slot).wait()
                    put_meta(j - 1, 1 - slot).wait()

                @pl.when(j + 1 < n_chunks)
                def _():
                    read_rows(j + 1, 1 - slot).start()
                    read_meta(j + 1, 1 - slot).start()

                put_rows(j, slot).start()
                put_meta(j, slot).start()

        put_rows(n_chunks - 1, (n_chunks - 1) % 2).wait()
        put_meta(n_chunks - 1, (n_chunks - 1) % 2).wait()

    return dispatch_kernel(rows, meta, cls, rank, class_start)


def _block_diag(w):
    h, d, _ = w.shape
    eye = jnp.eye(h, dtype=w.dtype)
    return (eye[:, None, :, None] * w[:, :, None, :]).reshape(h * d, h * d)


def _pair_tables():
    pair_of = np.zeros((EXPERTS_PER_GROUP, EXPERTS_PER_GROUP), np.int32)
    lo = np.zeros((N_PAIRS,), np.int32)
    hi = np.zeros((N_PAIRS,), np.int32)
    p = 0
    for a in range(EXPERTS_PER_GROUP):
        for b in range(a + 1, EXPERTS_PER_GROUP):
            pair_of[a, b] = pair_of[b, a] = p
            lo[p], hi[p] = a, b
            p += 1
    return pair_of, lo, hi


def _route_kernel(lt_ref, info_ref, meta_ref, counts_ref, run_ref):
    step = pl.program_id(0)

    @pl.when(step == 0)
    def _():
        run_ref[...] = jnp.zeros_like(run_ref)

    f32 = jnp.float32
    sub8 = lax.broadcasted_iota(jnp.int32, (8, LANES), 0).astype(f32)
    row_id = lax.broadcasted_iota(jnp.int32, (LANES, LANES), 0)
    col_id = lax.broadcasted_iota(jnp.int32, (LANES, LANES), 1)
    prefix_mat = (row_id <= col_id).astype(jnp.bfloat16)
    ones_mat = jnp.ones((LANES, LANES), jnp.bfloat16)
    neg_inf = f32(-jnp.inf)

    def first_index_of_max(v):
        m = jnp.max(v, axis=0, keepdims=True)
        idx = jnp.min(jnp.where(v == m, sub8, f32(8)), axis=0, keepdims=True)
        return m, idx

    def lane_tile(k, run):
        lanes = pl.ds(pl.multiple_of(k * LANES, LANES), LANES)
        g = jnp.where(sub8 < N_GROUPS, lt_ref[0:GROUP_ROWS, lanes], neg_inf)
        g_max, g_idx = first_index_of_max(g)
        g_top_p = 1.0 / jnp.sum(jnp.exp(g - g_max), axis=0, keepdims=True)

        e_sel = lt_ref[GROUP_ROWS:GROUP_ROWS + EXPERTS_PER_GROUP, lanes]
        for grp in range(1, N_GROUPS):
            lo = GROUP_ROWS + grp * EXPERTS_PER_GROUP
            e_sel = jnp.where(g_idx == grp, lt_ref[lo:lo + EXPERTS_PER_GROUP, lanes], e_sel)
        m1, i1 = first_index_of_max(e_sel)
        rest = jnp.where(sub8 == i1, neg_inf, e_sel)
        m2 = jnp.max(rest, axis=0, keepdims=True)
        i2 = jnp.min(jnp.where((rest == m2) & (sub8 != i1), sub8, f32(8)), axis=0, keepdims=True)

        e = jnp.exp(m2 - m1)
        w1 = g_top_p / (1.0 + e)
        w2 = g_top_p * e / (1.0 + e)
        first_is_lo = i1 < i2
        w_lo = jnp.where(first_is_lo, w1, w2)
        w_hi = jnp.where(first_is_lo, w2, w1)
        lo_e = jnp.minimum(i1, i2)
        hi_e = jnp.maximum(i1, i2)
        pair = lo_e * (2 * EXPERTS_PER_GROUP - 1 - lo_e) * 0.5 + hi_e - lo_e - 1.0
        cls = (g_idx * N_PAIRS + pair).astype(jnp.int32)

        onehot = (row_id == cls).astype(jnp.bfloat16)
        csum = jnp.dot(onehot, prefix_mat, preferred_element_type=f32)
        rank = jnp.sum(onehot.astype(f32) * (csum + run), axis=0, keepdims=True) - 1.0
        run = run + jnp.dot(onehot, ones_mat, preferred_element_type=f32)

        info_ref[:, lanes] = jnp.where(sub8 == 0, cls, jnp.where(sub8 == 1, rank.astype(jnp.int32), 0))
        meta8 = jnp.where(sub8 == 0, w_lo, jnp.where(sub8 == 1, w_hi, 0.0))
        meta_t = jnp.concatenate([meta8, jnp.zeros((LANES - 8, LANES), f32)], axis=0)
        meta_ref[lanes, :] = meta_t.T
        return run

    run = lax.fori_loop(0, ROUTE_BLOCK // LANES, lane_tile, run_ref[...], unroll=ROUTE_UNROLL)
    run_ref[...] = run
    counts_ref[...] = run


def _route_call(lt):
    return pl.pallas_call(
        _route_kernel,
        grid=(N_TOKENS // ROUTE_BLOCK,),
        in_specs=[pl.BlockSpec((ROUTER_ROWS, ROUTE_BLOCK), lambda i: (0, i))],
        out_specs=[
            pl.BlockSpec((8, ROUTE_BLOCK), lambda i: (0, i)),
            pl.BlockSpec((ROUTE_BLOCK, LANES), lambda i: (i, 0)),
            pl.BlockSpec((LANES, LANES), lambda i: (0, 0)),
        ],
        out_shape=[
            jax.ShapeDtypeStruct((8, N_TOKENS), jnp.int32),
            jax.ShapeDtypeStruct((N_TOKENS, LANES), jnp.float32),
            jax.ShapeDtypeStruct((LANES, LANES), jnp.float32),
        ],
        scratch_shapes=[pltpu.VMEM((LANES, LANES), jnp.float32)],
        compiler_params=pltpu.CompilerParams(dimension_semantics=("arbitrary",)),
        name="route_rank",
    )(lt)


def _dispatch_plan(counts):
    _, pair_lo, pair_hi = _pair_tables()
    tiles_c = (counts + TM - 1) // TM
    tiles_g = tiles_c.reshape(N_GROUPS, N_PAIRS).sum(axis=1)
    tiles_g_pad = (tiles_g + SUB - 1) // SUB * SUB
    g_start = jnp.cumsum(tiles_g_pad) - tiles_g_pad
    tc = tiles_c.reshape(N_GROUPS, N_PAIRS)
    c_start = (g_start[:, None] + jnp.cumsum(tc, axis=1) - tc).reshape(N_CLASSES)
    class_start = jnp.zeros((LANES,), jnp.int32).at[:N_CLASSES].set(c_start * TM)

    tile_ids = jnp.arange(N_TILES, dtype=jnp.int32)
    c_end = c_start + tiles_c
    owner = jnp.sum((tile_ids[:, None] >= c_end[None, :]).astype(jnp.int32), axis=1)
    owner = jnp.minimum(owner, N_CLASSES - 1)
    in_class = (tile_ids >= c_start[owner]) & (tile_ids < c_end[owner])
    rows_left = counts[owner] - (tile_ids - c_start[owner]) * TM
    tile_rows = jnp.where(in_class, jnp.minimum(rows_left, TM), 0).astype(jnp.int32)
    pair = owner % N_PAIRS
    tile_ea = jnp.asarray(pair_lo)[pair]
    tile_eb = jnp.asarray(pair_hi)[pair]
    g_end = g_start + tiles_g_pad
    step_first = jnp.arange(N_STEPS3, dtype=jnp.int32) * SUB
    step_group = jnp.sum((step_first[:, None] >= g_end[None, :]).astype(jnp.int32), axis=1)
    step_group = jnp.minimum(step_group, N_GROUPS - 1)
    return class_start, tile_ea, tile_eb, tile_rows, step_group.astype(jnp.int32)


def kernel(x, w_in, rg_conv_w, rg_conv_b, rg_gate_a_w, rg_gate_a_b, rg_gate_x_w, rg_gate_x_b,
           rg_lambda, sc_conv_w, w_out, ln1_g, ln1_b, router_group_w, router_group_b,
           router_expert_w, router_expert_b, exp_w_gate, exp_w_up, exp_w_down, ln2_g, ln2_b):
    bf16 = jnp.bfloat16
    row = lambda v: v.reshape(1, -1)
    pad_g = GROUP_ROWS - N_GROUPS
    pad_e = ROUTER_LANES - ROUTER_ROWS
    wr = jnp.concatenate([router_group_w, jnp.zeros((D_MODEL, pad_g), jnp.float32),
                          router_expert_w, jnp.zeros((D_MODEL, pad_e), jnp.float32)], axis=1)
    br = jnp.concatenate([router_group_b, jnp.zeros((pad_g,), jnp.float32),
                          router_expert_b, jnp.zeros((pad_e,), jnp.float32)]).reshape(1, -1)
    wr_hi = wr.astype(bf16)
    wr_lo = (wr - wr_hi.astype(jnp.float32)).astype(bf16)

    h1, lt = _mixer_call(
        x, w_in.astype(bf16), rg_conv_w, row(rg_conv_b),
        _block_diag(rg_gate_a_w).astype(bf16), row(rg_gate_a_b),
        _block_diag(rg_gate_x_w).astype(bf16), row(rg_gate_x_b),
        row(rg_lambda), sc_conv_w, w_out.astype(bf16), row(ln1_g), row(ln1_b),
        jnp.concatenate([wr_hi, wr_lo], axis=1), br)

    info, meta, counts = _route_call(lt)
    class_start, tile_ea, tile_eb, tile_rows, step_group = _dispatch_plan(
        counts[:N_CLASSES, 0].astype(jnp.int32))

    row_tiles = (-1, LANE_TILES, LANES)
    xs, wts, pos = _sc_dispatch(h1.reshape(row_tiles), meta, info[0], info[1], class_start)

    grp_shape = (N_GROUPS, EXPERTS_PER_GROUP)
    wgu, wdb = _cast_expert_weights(exp_w_gate, exp_w_up, exp_w_down)
    ys = _expert_call(
        step_group, tile_ea, tile_eb, tile_rows, xs.reshape(-1, LANES), wts,
        wgu.reshape(grp_shape + (D_MODEL, 2 * D_EXPERT)),
        wdb.reshape(grp_shape + (D_EXPERT, D_MODEL)),
        row(ln2_g), row(ln2_b))

    out_rows = pos.reshape(SEQ, BATCH).T.reshape(-1)
    out = _sc_gather_rows(ys.reshape(row_tiles), out_rows, COMBINE_CHUNK, (D_MODEL,))
    return out.reshape(BATCH, SEQ, D_MODEL)
```

```python
import functools

import jax
import jax.numpy as jnp
import numpy as np
from jax import lax
from jax.experimental import pallas as pl
from jax.experimental.pallas import tpu as pltpu
from jax.experimental.pallas import tpu_sc as plsc

D_MODEL = 1024
BATCH = 16
SEQ = 2048
D_RG = 512
D_SC = 512
RG_HEADS = 8
RG_HEAD_DIM = D_RG // RG_HEADS
RG_CONV = 4
RG_C = 8.0
SC_CONV = 3
D_IN_PROJ = 2 * D_RG + 3 * D_SC
N_GROUPS = 4
EXPERTS_PER_GROUP = 8
N_EXPERTS = N_GROUPS * EXPERTS_PER_GROUP
D_EXPERT = D_MODEL // 4
LN_EPS = 1e-5
DEEPNORM_ALPHA = 2.0 ** 0.25

N_TOKENS = BATCH * SEQ
LANES = 128
LANE_TILES = D_MODEL // LANES
MXU_WIDTH = 256
ROUTER_LANES = LANES
GROUP_ROWS = 8
ROUTER_ROWS = GROUP_ROWS + N_EXPERTS

TS = 64
M1 = TS * BATCH

ROUTE_BLOCK = 4096
ROUTE_UNROLL = 4

N_PAIRS = EXPERTS_PER_GROUP * (EXPERTS_PER_GROUP - 1) // 2
N_CLASSES = N_GROUPS * N_PAIRS
BF16_ROWS = 16
TM = -(-int(N_TOKENS / N_CLASSES * 1.09 / 2) // BF16_ROWS) * BF16_ROWS
SUB = 8
N_TILES = -(-(N_TOKENS // TM + N_CLASSES + N_GROUPS * (SUB - 1)) // SUB) * SUB
N_STEPS3 = N_TILES // SUB
P_ROWS = N_TILES * TM

VMEM_LIMIT = 56 * 1024 * 1024


def _layer_norm(z, g, b):
    mu = jnp.mean(z, axis=-1, keepdims=True)
    zc = z - mu
    var = jnp.mean(zc * zc, axis=-1, keepdims=True)
    return zc * lax.rsqrt(var + LN_EPS) * g + b


def _store_row_tiles(ref, first_row, value):
    n = value.shape[0]
    for j in range(LANE_TILES):
        ref[pl.ds(first_row * LANE_TILES + j, n, stride=LANE_TILES), :] = (
            value[:, j * LANES:(j + 1) * LANES])


def _load_row_tiles(ref, first_row, n):
    return jnp.concatenate(
        [ref[pl.ds(first_row * LANE_TILES + j, n, stride=LANE_TILES), :]
         for j in range(LANE_TILES)], axis=1)


def _mixer_kernel(x_ref, w_in_ref, cw_ref, cb_ref, wa_ref, ba_ref, wx_ref, bx_ref, lam_ref,
                  scw_ref, w_out_ref, g1_ref, b1_ref, wr_ref, br_ref,
                  h1_ref, logit_ref,
                  xbuf_ref, xsem, cbuf_ref, sbuf_ref, a_ref, u_ref, hstate_ref):
    c = pl.program_id(0)
    slot = c % 2
    rg_pad = (RG_CONV - 1) * BATCH
    sc_pad = (SC_CONV - 1) * BATCH

    def x_copies(chunk, slot_):
        return [pltpu.make_async_copy(x_ref.at[b, pl.ds(chunk * TS, TS), :],
                                      xbuf_ref.at[slot_, :, b, :], xsem.at[slot_])
                for b in range(BATCH)]

    @pl.when(c == 0)
    def _():
        for cp in x_copies(0, 0):
            cp.start()
        cbuf_ref[0:rg_pad, :] = jnp.zeros((rg_pad, D_RG), jnp.float32)
        sbuf_ref[0:sc_pad, :] = jnp.zeros((sc_pad, D_SC), jnp.float32)
        hstate_ref[...] = jnp.zeros_like(hstate_ref)

    @pl.when(c + 1 < SEQ // TS)
    def _():
        for cp in x_copies(c + 1, 1 - slot):
            cp.start()

    for cp in x_copies(c, slot):
        cp.wait()

    xt_ref = xbuf_ref.at[slot]
    xb = xt_ref[...].reshape(M1, D_MODEL).astype(jnp.bfloat16)

    def proj(lo, width):
        return jnp.dot(xb, w_in_ref[:, lo:lo + width], preferred_element_type=jnp.float32)

    cbuf_ref[rg_pad:rg_pad + M1, :] = proj(0, D_RG)
    xc = cb_ref[...] + cw_ref[0:1, :] * cbuf_ref[0:M1, :]
    for k in range(1, RG_CONV):
        xc = xc + cw_ref[k:k + 1, :] * cbuf_ref[k * BATCH:k * BATCH + M1, :]
    cbuf_ref[0:rg_pad, :] = cbuf_ref[M1:M1 + rg_pad, :]

    xcb = xc.astype(jnp.bfloat16)

    def gate(w_ref, b_ref):
        parts = [jnp.dot(xcb[:, lo:lo + MXU_WIDTH], w_ref[lo:lo + MXU_WIDTH, lo:lo + MXU_WIDTH],
                         preferred_element_type=jnp.float32)
                 for lo in range(0, D_RG, MXU_WIDTH)]
        return 0.5 + 0.5 * jnp.tanh(0.5 * (jnp.concatenate(parts, axis=1) + b_ref[...]))

    r = gate(wa_ref, ba_ref)
    i = gate(wx_ref, bx_ref)
    log_a = (-RG_C) * r * jax.nn.softplus(-lam_ref[...])
    a_ref[...] = jnp.exp(log_a)
    th = jnp.tanh(log_a)
    u_ref[...] = jnp.sqrt(-2.0 * th / (1.0 - th)) * (i * xc)

    sc_b = proj(2 * D_RG, D_SC)
    sbuf_ref[sc_pad:sc_pad + M1, :] = proj(2 * D_RG + D_SC, D_SC) * proj(2 * D_RG + 2 * D_SC, D_SC)
    conv = scw_ref[0:1, :] * sbuf_ref[0:M1, :]
    for k in range(1, SC_CONV):
        conv = conv + scw_ref[k:k + 1, :] * sbuf_ref[k * BATCH:k * BATCH + M1, :]
    sbuf_ref[0:sc_pad, :] = sbuf_ref[M1:M1 + sc_pad, :]
    y_sc = sc_b * conv

    h = hstate_ref[...]
    for t in range(TS):
        h = a_ref[t * BATCH:(t + 1) * BATCH, :] * h + u_ref[t * BATCH:(t + 1) * BATCH, :]
        u_ref[t * BATCH:(t + 1) * BATCH, :] = h
    hstate_ref[...] = h

    y_rg = u_ref[...] * jax.nn.gelu(proj(D_RG, D_RG))

    mix = jnp.dot(y_rg.astype(jnp.bfloat16), w_out_ref[0:D_RG, :], preferred_element_type=jnp.float32)
    mix = mix + jnp.dot(y_sc.astype(jnp.bfloat16), w_out_ref[D_RG:, :], preferred_element_type=jnp.float32)

    x_rows = xt_ref[...].reshape(M1, D_MODEL)
    h1 = _layer_norm(DEEPNORM_ALPHA * x_rows + mix, g1_ref[...], b1_ref[...])
    _store_row_tiles(h1_ref, 0, h1)
    h_hi = h1.astype(jnp.bfloat16)
    h_lo = (h1 - h_hi.astype(jnp.float32)).astype(jnp.bfloat16)
    both = jnp.dot(h_hi, wr_ref[...], preferred_element_type=jnp.float32)
    logits = both[:, :ROUTER_LANES] + both[:, ROUTER_LANES:] + br_ref[...]
    logits = logits + jnp.dot(h_lo, wr_ref[:, :ROUTER_LANES], preferred_element_type=jnp.float32)
    logit_ref[...] = logits.T[:ROUTER_ROWS, :]


def _const_spec(shape):
    return pl.BlockSpec(shape, lambda c: (0,) * len(shape))


def _mixer_call(x, w_in, cw, cb, wa, ba, wx, bx, lam, scw, w_out, g1, b1, wr, br):
    n_chunks = SEQ // TS
    in_specs = [
        pl.BlockSpec(memory_space=pl.ANY),
        _const_spec((D_MODEL, D_IN_PROJ)),
        _const_spec((RG_CONV, D_RG)), _const_spec((1, D_RG)),
        _const_spec((D_RG, D_RG)), _const_spec((1, D_RG)),
        _const_spec((D_RG, D_RG)), _const_spec((1, D_RG)),
        _const_spec((1, D_RG)),
        _const_spec((SC_CONV, D_SC)),
        _const_spec((D_MODEL, D_MODEL)),
        _const_spec((1, D_MODEL)), _const_spec((1, D_MODEL)),
        _const_spec((D_MODEL, 2 * ROUTER_LANES)), _const_spec((1, ROUTER_LANES)),
    ]
    out_specs = [
        pl.BlockSpec((M1 * LANE_TILES, LANES), lambda c: (c, 0)),
        pl.BlockSpec((ROUTER_ROWS, M1), lambda c: (0, c)),
    ]
    return pl.pallas_call(
        _mixer_kernel,
        grid=(n_chunks,),
        in_specs=in_specs,
        out_specs=out_specs,
        out_shape=[
            jax.ShapeDtypeStruct((N_TOKENS * LANE_TILES, LANES), jnp.float32),
            jax.ShapeDtypeStruct((ROUTER_ROWS, N_TOKENS), jnp.float32),
        ],
        scratch_shapes=[
            pltpu.VMEM((2, TS, BATCH, D_MODEL), jnp.float32),
            pltpu.SemaphoreType.DMA((2,)),
            pltpu.VMEM((M1 + (RG_CONV - 1) * BATCH, D_RG), jnp.float32),
            pltpu.VMEM((M1 + (SC_CONV - 1) * BATCH, D_SC), jnp.float32),
            pltpu.VMEM((M1, D_RG), jnp.float32),
            pltpu.VMEM((M1, D_RG), jnp.float32),
            pltpu.VMEM((BATCH, D_RG), jnp.float32),
        ],
        compiler_params=pltpu.CompilerParams(
            dimension_semantics=("arbitrary",), vmem_limit_bytes=VMEM_LIMIT),
        name="mixer_ln_router",
    )(x, w_in, cw, cb, wa, ba, wx, bx, lam, scw, w_out, g1, b1, wr, br)


def _expert_kernel(blk_ref, grp_ref, ea_ref, eb_ref, rows_ref,
                   xs_ref, wt_ref, wgu_ref, wd_ref, g2_ref, b2_ref, out_ref):
    s = pl.program_id(0)
    tile_rows = [rows_ref[s * SUB + j] for j in range(SUB)]

    @pl.when(blk_ref[s] == s)
    def _():
        for j in range(SUB):
            tile = s * SUB + j
            first_row = j * TM
            in_use = lax.broadcasted_iota(jnp.int32, (TM, 1), 0) < tile_rows[j]
            x = jnp.where(in_use, _load_row_tiles(xs_ref, first_row, TM), 0.0)
            xb = x.astype(jnp.bfloat16)
            wt = jnp.where(in_use, wt_ref[first_row:first_row + TM, :], 0.0)
            y = jnp.zeros((TM, D_MODEL), jnp.float32)
            for e_ref, col in ((ea_ref, 0), (eb_ref, 1)):
                e = e_ref[tile]
                hgu = jnp.dot(xb, wgu_ref[e], preferred_element_type=jnp.float32)
                hid = jax.nn.silu(hgu[:, :D_EXPERT]) * hgu[:, D_EXPERT:] * wt[:, col:col + 1]
                y = y + jnp.dot(hid.astype(jnp.bfloat16), wd_ref[e],
                                preferred_element_type=jnp.float32)
            _store_row_tiles(out_ref, first_row,
                             _layer_norm(DEEPNORM_ALPHA * x + y, g2_ref[...], b2_ref[...]))


def _expert_call(step_block, step_group, tile_ea, tile_eb, tile_rows, xs, wts, wgu, wd, g2, b2):
    rows = SUB * TM
    grid_spec = pltpu.PrefetchScalarGridSpec(
        num_scalar_prefetch=5,
        grid=(N_STEPS3,),
        in_specs=[
            pl.BlockSpec((rows * LANE_TILES, LANES), lambda s, blk, grp, ea, eb, va: (blk[s], 0)),
            pl.BlockSpec((rows, LANES), lambda s, blk, grp, ea, eb, va: (blk[s], 0)),
            pl.BlockSpec((None, EXPERTS_PER_GROUP, D_MODEL, 2 * D_EXPERT),
                         lambda s, blk, grp, ea, eb, va: (grp[s], 0, 0, 0)),
            pl.BlockSpec((None, EXPERTS_PER_GROUP, D_EXPERT, D_MODEL),
                         lambda s, blk, grp, ea, eb, va: (grp[s], 0, 0, 0)),
            pl.BlockSpec((1, D_MODEL), lambda s, blk, grp, ea, eb, va: (0, 0)),
            pl.BlockSpec((1, D_MODEL), lambda s, blk, grp, ea, eb, va: (0, 0)),
        ],
        out_specs=pl.BlockSpec((rows * LANE_TILES, LANES),
                               lambda s, blk, grp, ea, eb, va: (blk[s], 0)),
    )
    return pl.pallas_call(
        _expert_kernel,
        grid_spec=grid_spec,
        out_shape=jax.ShapeDtypeStruct((P_ROWS * LANE_TILES, LANES), jnp.float32),
        compiler_params=pltpu.CompilerParams(
            dimension_semantics=("arbitrary",), vmem_limit_bytes=VMEM_LIMIT),
        name="experts_ln",
    )(step_block, step_group, tile_ea, tile_eb, tile_rows, xs, wts, wgu, wd, g2, b2)


CAST_EXPERTS = 2


def _cast_kernel(wg_ref, wu_ref, wd_ref, wgu_ref, wdb_ref):
    wgu_ref[:, :, :D_EXPERT] = wg_ref[...].astype(jnp.bfloat16)
    wgu_ref[:, :, D_EXPERT:] = wu_ref[...].astype(jnp.bfloat16)
    wdb_ref[...] = wd_ref[...].astype(jnp.bfloat16)


def _cast_expert_weights(wg, wu, wd):
    eb = CAST_EXPERTS
    return pl.pallas_call(
        _cast_kernel,
        grid=(N_EXPERTS // eb,),
        in_specs=[
            pl.BlockSpec((eb, D_MODEL, D_EXPERT), lambda i: (i, 0, 0)),
            pl.BlockSpec((eb, D_MODEL, D_EXPERT), lambda i: (i, 0, 0)),
            pl.BlockSpec((eb, D_EXPERT, D_MODEL), lambda i: (i, 0, 0)),
        ],
        out_specs=[
            pl.BlockSpec((eb, D_MODEL, 2 * D_EXPERT), lambda i: (i, 0, 0)),
            pl.BlockSpec((eb, D_EXPERT, D_MODEL), lambda i: (i, 0, 0)),
        ],
        out_shape=[
            jax.ShapeDtypeStruct((N_EXPERTS, D_MODEL, 2 * D_EXPERT), jnp.bfloat16),
            jax.ShapeDtypeStruct((N_EXPERTS, D_EXPERT, D_MODEL), jnp.bfloat16),
        ],
        compiler_params=pltpu.CompilerParams(dimension_semantics=("arbitrary",)),
        name="cast_expert_weights",
    )(wg, wu, wd)


SC_CORES = 2
SC_SUBCORES = 16
SC_WORKERS = SC_CORES * SC_SUBCORES
COMBINE_CHUNK = 32


def _sc_gather_rows(table, idx, chunk, out_row_shape=None):
    n_out, = idx.shape
    row_shape = table.shape[1:]
    out_row_shape = row_shape if out_row_shape is None else out_row_shape
    per_w = n_out // SC_WORKERS
    n_chunks = per_w // chunk
    assert per_w * SC_WORKERS == n_out and n_chunks * chunk == per_w
    assert n_chunks % 2 == 0 and chunk % 8 == 0
    mesh = plsc.VectorSubcoreMesh(core_axis_name="c", subcore_axis_name="s")

    @functools.partial(
        pl.kernel, mesh=mesh,
        out_type=jax.ShapeDtypeStruct((n_out,) + out_row_shape, table.dtype),
        scratch_types=[
            pltpu.VMEM((per_w,), jnp.int32),
            pltpu.VMEM((2, chunk) + row_shape, table.dtype),
            pltpu.SemaphoreType.DMA((2,)),
            pltpu.SemaphoreType.DMA((2,)),
        ],
    )
    def gather_kernel(table_hbm, idx_hbm, out_hbm, idx_v, buf, gsem, wsem):
        wid = lax.axis_index("s") * SC_CORES + lax.axis_index("c")
        base = wid * per_w
        pltpu.sync_copy(idx_hbm.at[pl.ds(base, per_w)], idx_v)

        def gather(j, slot):
            rows = idx_v.at[pl.ds(j * chunk, chunk)]
            return pltpu.make_async_copy(table_hbm.at[rows], buf.at[slot], gsem.at[slot])

        def write(j, slot):
            dst = out_hbm.at[pl.ds(base + j * chunk, chunk)]
            src = buf.at[slot].reshape((chunk,) + out_row_shape)
            return pltpu.make_async_copy(src, dst, wsem.at[slot])

        gather(0, 0).start()

        @pl.loop(0, n_chunks, step=2)
        def _(j0):
            for slot in range(2):
                j = j0 + slot
                gather(j, slot).wait()

                @pl.when(j >= 1)
                def _():
                    write(j - 1, 1 - slot).wait()

                @pl.when(j + 1 < n_chunks)
                def _():
                    gather(j + 1, 1 - slot).start()

                write(j, slot).start()

        write(n_chunks - 1, (n_chunks - 1) % 2).wait()

    return gather_kernel(table, idx)


SC_LANES = 16


def _sc_dispatch(rows, meta, cls, rank, class_start):
    n_rows = rows.shape[0]
    chunk = COMBINE_CHUNK
    per_w = n_rows // SC_WORKERS
    n_chunks = per_w // chunk
    assert per_w * SC_WORKERS == n_rows and n_chunks * chunk == per_w
    assert n_chunks % 2 == 0 and chunk % SC_LANES == 0
    mesh = plsc.VectorSubcoreMesh(core_axis_name="c", subcore_axis_name="s")

    @functools.partial(
        pl.kernel, mesh=mesh,
        out_type=[
            jax.ShapeDtypeStruct((P_ROWS,) + rows.shape[1:], rows.dtype),
            jax.ShapeDtypeStruct((P_ROWS,) + meta.shape[1:], meta.dtype),
            jax.ShapeDtypeStruct((n_rows,), jnp.int32),
        ],
        scratch_types=[
            pltpu.VMEM((per_w,), jnp.int32),
            pltpu.VMEM((per_w,), jnp.int32),
            pltpu.VMEM((LANES,), jnp.int32),
            pltpu.VMEM((per_w,), jnp.int32),
            pltpu.VMEM((n_chunks, chunk), jnp.int32),
            pltpu.VMEM((2, chunk) + rows.shape[1:], rows.dtype),
            pltpu.VMEM((2, chunk) + meta.shape[1:], meta.dtype),
            pltpu.SemaphoreType.DMA((2,)),
            pltpu.SemaphoreType.DMA((2,)),
            pltpu.SemaphoreType.DMA((2,)),
            pltpu.SemaphoreType.DMA((2,)),
        ],
        compiler_params=pltpu.CompilerParams(needs_layout_passes=False),
    )
    def dispatch_kernel(rows_hbm, meta_hbm, cls_hbm, rank_hbm, start_hbm,
                        xs_hbm, ws_hbm, pos_hbm,
                        cls_v, rank_v, start_v, pos_flat, pos_v, rbuf, mbuf,
                        rsem, msem, xsem, wsem):
        wid = lax.axis_index("s") * SC_CORES + lax.axis_index("c")
        base = wid * per_w
        pltpu.sync_copy(cls_hbm.at[pl.ds(base, per_w)], cls_v)
        pltpu.sync_copy(rank_hbm.at[pl.ds(base, per_w)], rank_v)
        pltpu.sync_copy(start_hbm, start_v)

        @pl.loop(0, n_chunks)
        def _(j):
            for k in range(chunk // SC_LANES):
                off = j * chunk + k * SC_LANES
                c = cls_v[pl.ds(off, SC_LANES)]
                pos = plsc.load_gather(start_v, [c]) + rank_v[pl.ds(off, SC_LANES)]
                pos_flat[pl.ds(off, SC_LANES)] = pos
                pos_v[j, pl.ds(k * SC_LANES, SC_LANES)] = pos

        pltpu.sync_copy(pos_flat, pos_hbm.at[pl.ds(base, per_w)])

        def read_rows(j, slot):
            src = rows_hbm.at[pl.ds(base + j * chunk, chunk)]
            return pltpu.make_async_copy(src, rbuf.at[slot], rsem.at[slot])

        def read_meta(j, slot):
            src = meta_hbm.at[pl.ds(base + j * chunk, chunk)]
            return pltpu.make_async_copy(src, mbuf.at[slot], msem.at[slot])

        def put_rows(j, slot):
            return pltpu.make_async_copy(rbuf.at[slot], xs_hbm.at[pos_v.at[j]], xsem.at[slot])

        def put_meta(j, slot):
            return pltpu.make_async_copy(mbuf.at[slot], ws_hbm.at[pos_v.at[j]], wsem.at[slot])

        read_rows(0, 0).start()
        read_meta(0, 0).start()

        @pl.loop(0, n_chunks, step=2)
        def _(j0):
            for slot in range(2):
                j = j0 + slot
                read_rows(j, slot).wait()
                read_meta(j, slot).wait()

                @pl.when(j >= 1)
                def _():
                    put_rows(j - 1, 1 - slot).wait()
                    put_meta(j - 1, 1 - slot).wait()

                @pl.when(j + 1 < n_chunks)
                def _():
                    read_rows(j + 1, 1 - slot).start()
                    read_meta(j + 1, 1 - slot).start()

                put_rows(j, slot).start()
                put_meta(j, slot).start()

        put_rows(n_chunks - 1, (n_chunks - 1) % 2).wait()
        put_meta(n_chunks - 1, (n_chunks - 1) % 2).wait()

    return dispatch_kernel(rows, meta, cls, rank, class_start)


def _block_diag(w):
    h, d, _ = w.shape
    eye = jnp.eye(h, dtype=w.dtype)
    return (eye[:, None, :, None] * w[:, :, None, :]).reshape(h * d, h * d)


def _pair_tables():
    pair_of = np.zeros((EXPERTS_PER_GROUP, EXPERTS_PER_GROUP), np.int32)
    lo = np.zeros((N_PAIRS,), np.int32)
    hi = np.zeros((N_PAIRS,), np.int32)
    p = 0
    for a in range(EXPERTS_PER_GROUP):
        for b in range(a + 1, EXPERTS_PER_GROUP):
            pair_of[a, b] = pair_of[b, a] = p
            lo[p], hi[p] = a, b
            p += 1
    return pair_of, lo, hi


def _route_kernel(lt_ref, info_ref, meta_ref, counts_ref, run_ref):
    step = pl.program_id(0)

    @pl.when(step == 0)
    def _():
        run_ref[...] = jnp.zeros_like(run_ref)

    f32 = jnp.float32
    sub8 = lax.broadcasted_iota(jnp.int32, (8, LANES), 0).astype(f32)
    row_id = lax.broadcasted_iota(jnp.int32, (LANES, LANES), 0)
    col_id = lax.broadcasted_iota(jnp.int32, (LANES, LANES), 1)
    prefix_mat = (row_id <= col_id).astype(jnp.bfloat16)
    ones_mat = jnp.ones((LANES, LANES), jnp.bfloat16)
    neg_inf = f32(-jnp.inf)

    def first_index_of_max(v):
        m = jnp.max(v, axis=0, keepdims=True)
        idx = jnp.min(jnp.where(v == m, sub8, f32(8)), axis=0, keepdims=True)
        return m, idx

    def lane_tile(k, run):
        lanes = pl.ds(pl.multiple_of(k * LANES, LANES), LANES)
        g = jnp.where(sub8 < N_GROUPS, lt_ref[0:GROUP_ROWS, lanes], neg_inf)
        g_max, g_idx = first_index_of_max(g)
        g_top_p = 1.0 / jnp.sum(jnp.exp(g - g_max), axis=0, keepdims=True)

        e_sel = lt_ref[GROUP_ROWS:GROUP_ROWS + EXPERTS_PER_GROUP, lanes]
        for grp in range(1, N_GROUPS):
            lo = GROUP_ROWS + grp * EXPERTS_PER_GROUP
            e_sel = jnp.where(g_idx == grp, lt_ref[lo:lo + EXPERTS_PER_GROUP, lanes], e_sel)
        m1, i1 = first_index_of_max(e_sel)
        rest = jnp.where(sub8 == i1, neg_inf, e_sel)
        m2 = jnp.max(rest, axis=0, keepdims=True)
        i2 = jnp.min(jnp.where((rest == m2) & (sub8 != i1), sub8, f32(8)), axis=0, keepdims=True)

        e = jnp.exp(m2 - m1)
        w1 = g_top_p / (1.0 + e)
        w2 = g_top_p * e / (1.0 + e)
        first_is_lo = i1 < i2
        w_lo = jnp.where(first_is_lo, w1, w2)
        w_hi = jnp.where(first_is_lo, w2, w1)
        lo_e = jnp.minimum(i1, i2)
        hi_e = jnp.maximum(i1, i2)
        pair = lo_e * (2 * EXPERTS_PER_GROUP - 1 - lo_e) * 0.5 + hi_e - lo_e - 1.0
        cls = (g_idx * N_PAIRS + pair).astype(jnp.int32)

        onehot = (row_id == cls).astype(jnp.bfloat16)
        csum = jnp.dot(onehot, prefix_mat, preferred_element_type=f32)
        rank = jnp.sum(onehot.astype(f32) * (csum + run), axis=0, keepdims=True) - 1.0
        run = run + jnp.dot(onehot, ones_mat, preferred_element_type=f32)

        info_ref[:, lanes] = jnp.where(sub8 == 0, cls, jnp.where(sub8 == 1, rank.astype(jnp.int32), 0))
        meta8 = jnp.where(sub8 == 0, w_lo, jnp.where(sub8 == 1, w_hi, 0.0))
        meta_t = jnp.concatenate([meta8, jnp.zeros((LANES - 8, LANES), f32)], axis=0)
        meta_ref[lanes, :] = meta_t.T
        return run

    run = lax.fori_loop(0, ROUTE_BLOCK // LANES, lane_tile, run_ref[...], unroll=ROUTE_UNROLL)
    run_ref[...] = run
    counts_ref[...] = run


def _route_call(lt):
    return pl.pallas_call(
        _route_kernel,
        grid=(N_TOKENS // ROUTE_BLOCK,),
        in_specs=[pl.BlockSpec((ROUTER_ROWS, ROUTE_BLOCK), lambda i: (0, i))],
        out_specs=[
            pl.BlockSpec((8, ROUTE_BLOCK), lambda i: (0, i)),
            pl.BlockSpec((ROUTE_BLOCK, LANES), lambda i: (i, 0)),
            pl.BlockSpec((LANES, LANES), lambda i: (0, 0)),
        ],
        out_shape=[
            jax.ShapeDtypeStruct((8, N_TOKENS), jnp.int32),
            jax.ShapeDtypeStruct((N_TOKENS, LANES), jnp.float32),
            jax.ShapeDtypeStruct((LANES, LANES), jnp.float32),
        ],
        scratch_shapes=[pltpu.VMEM((LANES, LANES), jnp.float32)],
        compiler_params=pltpu.CompilerParams(dimension_semantics=("arbitrary",)),
        name="route_rank",
    )(lt)


def _dispatch_plan(counts):
    _, pair_lo, pair_hi = _pair_tables()
    tiles_c = (counts + TM - 1) // TM
    tiles_g = tiles_c.reshape(N_GROUPS, N_PAIRS).sum(axis=1)
    tiles_g_pad = (tiles_g + SUB - 1) // SUB * SUB
    g_start = jnp.cumsum(tiles_g_pad) - tiles_g_pad
    tc = tiles_c.reshape(N_GROUPS, N_PAIRS)
    c_start = (g_start[:, None] + jnp.cumsum(tc, axis=1) - tc).reshape(N_CLASSES)
    class_start = jnp.zeros((LANES,), jnp.int32).at[:N_CLASSES].set(c_start * TM)

    tile_ids = jnp.arange(N_TILES, dtype=jnp.int32)
    c_end = c_start + tiles_c
    owner = jnp.sum((tile_ids[:, None] >= c_end[None, :]).astype(jnp.int32), axis=1)
    owner = jnp.minimum(owner, N_CLASSES - 1)
    in_class = (tile_ids >= c_start[owner]) & (tile_ids < c_end[owner])
    rows_left = counts[owner] - (tile_ids - c_start[owner]) * TM
    tile_rows = jnp.where(in_class, jnp.minimum(rows_left, TM), 0).astype(jnp.int32)
    pair = owner % N_PAIRS
    tile_ea = jnp.asarray(pair_lo)[pair]
    tile_eb = jnp.asarray(pair_hi)[pair]
    g_end = g_start + tiles_g_pad
    step_first = jnp.arange(N_STEPS3, dtype=jnp.int32) * SUB
    step_group = jnp.sum((step_first[:, None] >= g_end[None, :]).astype(jnp.int32), axis=1)
    step_group = jnp.minimum(step_group, N_GROUPS - 1)
    last_step = jnp.maximum(g_end[-1] // SUB - 1, 0)
    step_block = jnp.minimum(jnp.arange(N_STEPS3, dtype=jnp.int32), last_step)
    return (class_start, tile_ea, tile_eb, tile_rows, step_group.astype(jnp.int32),
            step_block.astype(jnp.int32))


def kernel(x, w_in, rg_conv_w, rg_conv_b, rg_gate_a_w, rg_gate_a_b, rg_gate_x_w, rg_gate_x_b,
           rg_lambda, sc_conv_w, w_out, ln1_g, ln1_b, router_group_w, router_group_b,
           router_expert_w, router_expert_b, exp_w_gate, exp_w_up, exp_w_down, ln2_g, ln2_b):
    bf16 = jnp.bfloat16
    row = lambda v: v.reshape(1, -1)
    pad_g = GROUP_ROWS - N_GROUPS
    pad_e = ROUTER_LANES - ROUTER_ROWS
    wr = jnp.concatenate([router_group_w, jnp.zeros((D_MODEL, pad_g), jnp.float32),
                          router_expert_w, jnp.zeros((D_MODEL, pad_e), jnp.float32)], axis=1)
    br = jnp.concatenate([router_group_b, jnp.zeros((pad_g,), jnp.float32),
                          router_expert_b, jnp.zeros((pad_e,), jnp.float32)]).reshape(1, -1)
    wr_hi = wr.astype(bf16)
    wr_lo = (wr - wr_hi.astype(jnp.float32)).astype(bf16)

    h1, lt = _mixer_call(
        x, w_in.astype(bf16), rg_conv_w, row(rg_conv_b),
        _block_diag(rg_gate_a_w).astype(bf16), row(rg_gate_a_b),
        _block_diag(rg_gate_x_w).astype(bf16), row(rg_gate_x_b),
        row(rg_lambda), sc_conv_w, w_out.astype(bf16), row(ln1_g), row(ln1_b),
        jnp.concatenate([wr_hi, wr_lo], axis=1), br)

    info, meta, counts = _route_call(lt)
    class_start, tile_ea, tile_eb, tile_rows, step_group, step_block = _dispatch_plan(
        counts[:N_CLASSES, 0].astype(jnp.int32))

    row_tiles = (-1, LANE_TILES, LANES)
    xs, wts, pos = _sc_dispatch(h1.reshape(row_tiles), meta, info[0], info[1], class_start)

    grp_shape = (N_GROUPS, EXPERTS_PER_GROUP)
    wgu, wdb = _cast_expert_weights(exp_w_gate, exp_w_up, exp_w_down)
    ys = _expert_call(
        step_block, step_group, tile_ea, tile_eb, tile_rows, xs.reshape(-1, LANES), wts,
        wgu.reshape(grp_shape + (D_MODEL, 2 * D_EXPERT)),
        wdb.reshape(grp_shape + (D_EXPERT, D_MODEL)),
        row(ln2_g), row(ln2_b))

    out_rows = pos.reshape(SEQ, BATCH).T.reshape(-1)
    out = _sc_gather_rows(ys.reshape(row_tiles), out_rows, COMBINE_CHUNK, (D_MODEL,))
    return out.reshape(BATCH, SEQ, D_MODEL)
```

```python
import functools

import jax
import jax.numpy as jnp
import numpy as np
from jax import lax
from jax.experimental import pallas as pl
from jax.experimental.pallas import tpu as pltpu
from jax.experimental.pallas import tpu_sc as plsc

D_MODEL = 1024
BATCH = 16
SEQ = 2048
D_RG = 512
D_SC = 512
RG_HEADS = 8
RG_HEAD_DIM = D_RG // RG_HEADS
RG_CONV = 4
RG_C = 8.0
SC_CONV = 3
D_IN_PROJ = 2 * D_RG + 3 * D_SC
N_GROUPS = 4
EXPERTS_PER_GROUP = 8
N_EXPERTS = N_GROUPS * EXPERTS_PER_GROUP
D_EXPERT = D_MODEL // 4
LN_EPS = 1e-5
DEEPNORM_ALPHA = 2.0 ** 0.25

N_TOKENS = BATCH * SEQ
LANES = 128
LANE_TILES = D_MODEL // LANES
MXU_WIDTH = 256
ROUTER_LANES = LANES
GROUP_ROWS = 8
ROUTER_ROWS = GROUP_ROWS + N_EXPERTS

TS = 64
M1 = TS * BATCH
DOT_ROWS = 256

ROUTE_BLOCK = 4096
ROUTE_UNROLL = 4

N_PAIRS = EXPERTS_PER_GROUP * (EXPERTS_PER_GROUP - 1) // 2
N_CLASSES = N_GROUPS * N_PAIRS
BF16_ROWS = 16
TM = -(-int(N_TOKENS / N_CLASSES * 1.09 / 2) // BF16_ROWS) * BF16_ROWS
SUB = 8
N_TILES = -(-(N_TOKENS // TM + N_CLASSES + N_GROUPS * (SUB - 1)) // SUB) * SUB
N_STEPS3 = N_TILES // SUB
P_ROWS = N_TILES * TM

VMEM_LIMIT = 56 * 1024 * 1024


def _layer_norm(z, g, b):
    mu = jnp.mean(z, axis=-1, keepdims=True)
    zc = z - mu
    var = jnp.mean(zc * zc, axis=-1, keepdims=True)
    return zc * lax.rsqrt(var + LN_EPS) * g + b


def _store_row_tiles(ref, first_row, value):
    n = value.shape[0]
    for j in range(LANE_TILES):
        ref[pl.ds(first_row * LANE_TILES + j, n, stride=LANE_TILES), :] = (
            value[:, j * LANES:(j + 1) * LANES])


def _load_row_tiles(ref, first_row, n):
    return jnp.concatenate(
        [ref[pl.ds(first_row * LANE_TILES + j, n, stride=LANE_TILES), :]
         for j in range(LANE_TILES)], axis=1)


def _mixer_kernel(x_ref, w_in_ref, cw_ref, cb_ref, wa_ref, ba_ref, wx_ref, bx_ref, lam_ref,
                  scw_ref, w_out_ref, g1_ref, b1_ref, wr_ref, br_ref,
                  h1_ref, logit_ref,
                  xbuf_ref, xsem, cbuf_ref, sbuf_ref, a_ref, u_ref, hstate_ref):
    c = pl.program_id(0)
    slot = c % 2
    rg_pad = (RG_CONV - 1) * BATCH
    sc_pad = (SC_CONV - 1) * BATCH

    def x_copies(chunk, slot_):
        return [pltpu.make_async_copy(x_ref.at[b, pl.ds(chunk * TS, TS), :],
                                      xbuf_ref.at[slot_, :, b, :], xsem.at[slot_])
                for b in range(BATCH)]

    @pl.when(c == 0)
    def _():
        for cp in x_copies(0, 0):
            cp.start()
        cbuf_ref[0:rg_pad, :] = jnp.zeros((rg_pad, D_RG), jnp.float32)
        sbuf_ref[0:sc_pad, :] = jnp.zeros((sc_pad, D_SC), jnp.float32)
        hstate_ref[...] = jnp.zeros_like(hstate_ref)

    @pl.when(c + 1 < SEQ // TS)
    def _():
        for cp in x_copies(c + 1, 1 - slot):
            cp.start()

    for cp in x_copies(c, slot):
        cp.wait()

    xt_ref = xbuf_ref.at[slot]
    def gate(xcb, w_ref, b_ref):
        parts = [jnp.dot(xcb[:, lo:lo + MXU_WIDTH], w_ref[lo:lo + MXU_WIDTH, lo:lo + MXU_WIDTH],
                         preferred_element_type=jnp.float32)
                 for lo in range(0, D_RG, MXU_WIDTH)]
        return 0.5 + 0.5 * jnp.tanh(0.5 * (jnp.concatenate(parts, axis=1) + b_ref[...]))

    soft_lam = jax.nn.softplus(-lam_ref[...])
    gelu_chunks, y_sc_chunks = [], []
    for r0 in range(0, M1, DOT_ROWS):
        t0 = r0 // BATCH
        xk = xt_ref[t0:t0 + DOT_ROWS // BATCH].reshape(DOT_ROWS, D_MODEL).astype(jnp.bfloat16)

        def proj(lo, width):
            return jnp.dot(xk, w_in_ref[:, lo:lo + width], preferred_element_type=jnp.float32)

        cbuf_ref[rg_pad + r0:rg_pad + r0 + DOT_ROWS, :] = proj(0, D_RG)
        rg_gate = proj(D_RG, D_RG)
        xc = cb_ref[...] + cw_ref[0:1, :] * cbuf_ref[r0:r0 + DOT_ROWS, :]
        for k in range(1, RG_CONV):
            xc = xc + cw_ref[k:k + 1, :] * cbuf_ref[r0 + k * BATCH:r0 + k * BATCH + DOT_ROWS, :]
        sc_b = proj(2 * D_RG, D_SC)
        xcb = xc.astype(jnp.bfloat16)
        r = gate(xcb, wa_ref, ba_ref)
        i = gate(xcb, wx_ref, bx_ref)
        sc_c = proj(2 * D_RG + D_SC, D_SC)
        log_a = (-RG_C) * r * soft_lam
        a_ref[r0:r0 + DOT_ROWS, :] = jnp.exp(log_a)
        th = jnp.tanh(log_a)
        u_ref[r0:r0 + DOT_ROWS, :] = jnp.sqrt(-2.0 * th / (1.0 - th)) * (i * xc)
        gelu_chunks.append(jax.nn.gelu(rg_gate))

        sbuf_ref[sc_pad + r0:sc_pad + r0 + DOT_ROWS, :] = sc_c * proj(2 * D_RG + 2 * D_SC, D_SC)
        conv = scw_ref[0:1, :] * sbuf_ref[r0:r0 + DOT_ROWS, :]
        for k in range(1, SC_CONV):
            conv = conv + scw_ref[k:k + 1, :] * sbuf_ref[r0 + k * BATCH:r0 + k * BATCH + DOT_ROWS, :]
        y_sc_chunks.append((sc_b * conv).astype(jnp.bfloat16))

    cbuf_ref[0:rg_pad, :] = cbuf_ref[M1:M1 + rg_pad, :]
    sbuf_ref[0:sc_pad, :] = sbuf_ref[M1:M1 + sc_pad, :]
    y_sc = jnp.concatenate(y_sc_chunks, axis=0)

    h = hstate_ref[...]
    for t in range(TS):
        h = a_ref[t * BATCH:(t + 1) * BATCH, :] * h + u_ref[t * BATCH:(t + 1) * BATCH, :]
        u_ref[t * BATCH:(t + 1) * BATCH, :] = h
    hstate_ref[...] = h

    y_rg = (u_ref[...] * jnp.concatenate(gelu_chunks, axis=0)).astype(jnp.bfloat16)

    half = M1 // 2
    mixes = []
    for lo in (0, half):
        mix = jnp.dot(y_rg[lo:lo + half], w_out_ref[0:D_RG, :], preferred_element_type=jnp.float32)
        mixes.append(mix + jnp.dot(y_sc[lo:lo + half], w_out_ref[D_RG:, :],
                                   preferred_element_type=jnp.float32))
    for lo, mix in zip((0, half), mixes):
        x_rows = xt_ref[lo // BATCH:(lo + half) // BATCH].reshape(half, D_MODEL)
        h1 = _layer_norm(DEEPNORM_ALPHA * x_rows + mix, g1_ref[...], b1_ref[...])
        _store_row_tiles(h1_ref, lo, h1)
        h_hi = h1.astype(jnp.bfloat16)
        h_lo = (h1 - h_hi.astype(jnp.float32)).astype(jnp.bfloat16)
        both = jnp.dot(h_hi, wr_ref[...], preferred_element_type=jnp.float32)
        logits = both[:, :ROUTER_LANES] + both[:, ROUTER_LANES:] + br_ref[...]
        logits = logits + jnp.dot(h_lo, wr_ref[:, :ROUTER_LANES],
                                  preferred_element_type=jnp.float32)
        logit_ref[:, lo:lo + half] = logits.T[:ROUTER_ROWS, :]


def _const_spec(shape):
    return pl.BlockSpec(shape, lambda c: (0,) * len(shape))


def _mixer_call(x, w_in, cw, cb, wa, ba, wx, bx, lam, scw, w_out, g1, b1, wr, br):
    n_chunks = SEQ // TS
    in_specs = [
        pl.BlockSpec(memory_space=pl.ANY),
        _const_spec((D_MODEL, D_IN_PROJ)),
        _const_spec((RG_CONV, D_RG)), _const_spec((1, D_RG)),
        _const_spec((D_RG, D_RG)), _const_spec((1, D_RG)),
        _const_spec((D_RG, D_RG)), _const_spec((1, D_RG)),
        _const_spec((1, D_RG)),
        _const_spec((SC_CONV, D_SC)),
        _const_spec((D_MODEL, D_MODEL)),
        _const_spec((1, D_MODEL)), _const_spec((1, D_MODEL)),
        _const_spec((D_MODEL, 2 * ROUTER_LANES)), _const_spec((1, ROUTER_LANES)),
    ]
    out_specs = [
        pl.BlockSpec((M1 * LANE_TILES, LANES), lambda c: (c, 0)),
        pl.BlockSpec((ROUTER_ROWS, M1), lambda c: (0, c)),
    ]
    return pl.pallas_call(
        _mixer_kernel,
        grid=(n_chunks,),
        in_specs=in_specs,
        out_specs=out_specs,
        out_shape=[
            jax.ShapeDtypeStruct((N_TOKENS * LANE_TILES, LANES), jnp.float32),
            jax.ShapeDtypeStruct((ROUTER_ROWS, N_TOKENS), jnp.float32),
        ],
        scratch_shapes=[
            pltpu.VMEM((2, TS, BATCH, D_MODEL), jnp.float32),
            pltpu.SemaphoreType.DMA((2,)),
            pltpu.VMEM((M1 + (RG_CONV - 1) * BATCH, D_RG), jnp.float32),
            pltpu.VMEM((M1 + (SC_CONV - 1) * BATCH, D_SC), jnp.float32),
            pltpu.VMEM((M1, D_RG), jnp.float32),
            pltpu.VMEM((M1, D_RG), jnp.float32),
            pltpu.VMEM((BATCH, D_RG), jnp.float32),
        ],
        compiler_params=pltpu.CompilerParams(
            dimension_semantics=("arbitrary",), vmem_limit_bytes=VMEM_LIMIT),
        name="mixer_ln_router",
    )(x, w_in, cw, cb, wa, ba, wx, bx, lam, scw, w_out, g1, b1, wr, br)


def _expert_kernel(blk_ref, grp_ref, ea_ref, eb_ref, rows_ref,
                   xs_ref, wt_ref, wgu_ref, wd_ref, g2_ref, b2_ref, out_ref):
    s = pl.program_id(0)
    tile_rows = [rows_ref[s * SUB + j] for j in range(SUB)]

    @pl.when(blk_ref[s] == s)
    def _():
        for j in range(SUB):
            tile = s * SUB + j
            first_row = j * TM
            in_use = lax.broadcasted_iota(jnp.int32, (TM, 1), 0) < tile_rows[j]
            x = jnp.where(in_use, _load_row_tiles(xs_ref, first_row, TM), 0.0)
            xb = x.astype(jnp.bfloat16)
            wt = jnp.where(in_use, wt_ref[first_row:first_row + TM, :], 0.0)
            y = jnp.zeros((TM, D_MODEL), jnp.float32)
            for e_ref, col in ((ea_ref, 0), (eb_ref, 1)):
                e = e_ref[tile]
                hgu = jnp.dot(xb, wgu_ref[e], preferred_element_type=jnp.float32)
                hid = jax.nn.silu(hgu[:, :D_EXPERT]) * hgu[:, D_EXPERT:] * wt[:, col:col + 1]
                y = y + jnp.dot(hid.astype(jnp.bfloat16), wd_ref[e],
                                preferred_element_type=jnp.float32)
            _store_row_tiles(out_ref, first_row,
                             _layer_norm(DEEPNORM_ALPHA * x + y, g2_ref[...], b2_ref[...]))


def _expert_call(step_block, step_group, tile_ea, tile_eb, tile_rows, xs, wts, wgu, wd, g2, b2):
    rows = SUB * TM
    grid_spec = pltpu.PrefetchScalarGridSpec(
        num_scalar_prefetch=5,
        grid=(N_STEPS3,),
        in_specs=[
            pl.BlockSpec((rows * LANE_TILES, LANES), lambda s, blk, grp, ea, eb, va: (blk[s], 0)),
            pl.BlockSpec((rows, LANES), lambda s, blk, grp, ea, eb, va: (blk[s], 0)),
            pl.BlockSpec((None, EXPERTS_PER_GROUP, D_MODEL, 2 * D_EXPERT),
                         lambda s, blk, grp, ea, eb, va: (grp[s], 0, 0, 0)),
            pl.BlockSpec((None, EXPERTS_PER_GROUP, D_EXPERT, D_MODEL),
                         lambda s, blk, grp, ea, eb, va: (grp[s], 0, 0, 0)),
            pl.BlockSpec((1, D_MODEL), lambda s, blk, grp, ea, eb, va: (0, 0)),
            pl.BlockSpec((1, D_MODEL), lambda s, blk, grp, ea, eb, va: (0, 0)),
        ],
        out_specs=pl.BlockSpec((rows * LANE_TILES, LANES),
                               lambda s, blk, grp, ea, eb, va: (blk[s], 0)),
    )
    return pl.pallas_call(
        _expert_kernel,
        grid_spec=grid_spec,
        out_shape=jax.ShapeDtypeStruct((P_ROWS * LANE_TILES, LANES), jnp.float32),
        compiler_params=pltpu.CompilerParams(
            dimension_semantics=("arbitrary",), vmem_limit_bytes=VMEM_LIMIT),
        name="experts_ln",
    )(step_block, step_group, tile_ea, tile_eb, tile_rows, xs, wts, wgu, wd, g2, b2)


CAST_EXPERTS = 2


def _cast_kernel(wg_ref, wu_ref, wd_ref, wgu_ref, wdb_ref):
    wgu_ref[:, :, :D_EXPERT] = wg_ref[...].astype(jnp.bfloat16)
    wgu_ref[:, :, D_EXPERT:] = wu_ref[...].astype(jnp.bfloat16)
    wdb_ref[...] = wd_ref[...].astype(jnp.bfloat16)


def _cast_expert_weights(wg, wu, wd):
    eb = CAST_EXPERTS
    return pl.pallas_call(
        _cast_kernel,
        grid=(N_EXPERTS // eb,),
        in_specs=[
            pl.BlockSpec((eb, D_MODEL, D_EXPERT), lambda i: (i, 0, 0)),
            pl.BlockSpec((eb, D_MODEL, D_EXPERT), lambda i: (i, 0, 0)),
            pl.BlockSpec((eb, D_EXPERT, D_MODEL), lambda i: (i, 0, 0)),
        ],
        out_specs=[
            pl.BlockSpec((eb, D_MODEL, 2 * D_EXPERT), lambda i: (i, 0, 0)),
            pl.BlockSpec((eb, D_EXPERT, D_MODEL), lambda i: (i, 0, 0)),
        ],
        out_shape=[
            jax.ShapeDtypeStruct((N_EXPERTS, D_MODEL, 2 * D_EXPERT), jnp.bfloat16),
            jax.ShapeDtypeStruct((N_EXPERTS, D_EXPERT, D_MODEL), jnp.bfloat16),
        ],
        compiler_params=pltpu.CompilerParams(dimension_semantics=("arbitrary",)),
        name="cast_expert_weights",
    )(wg, wu, wd)


SC_CORES = 2
SC_SUBCORES = 16
SC_WORKERS = SC_CORES * SC_SUBCORES
COMBINE_CHUNK = 32


def _sc_gather_rows(table, idx, chunk, out_row_shape=None):
    n_out, = idx.shape
    row_shape = table.shape[1:]
    out_row_shape = row_shape if out_row_shape is None else out_row_shape
    per_w = n_out // SC_WORKERS
    n_chunks = per_w // chunk
    assert per_w * SC_WORKERS == n_out and n_chunks * chunk == per_w
    assert n_chunks % 2 == 0 and chunk % 8 == 0
    mesh = plsc.VectorSubcoreMesh(core_axis_name="c", subcore_axis_name="s")

    @functools.partial(
        pl.kernel, mesh=mesh,
        out_type=jax.ShapeDtypeStruct((n_out,) + out_row_shape, table.dtype),
        scratch_types=[
            pltpu.VMEM((per_w,), jnp.int32),
            pltpu.VMEM((2, chunk) + row_shape, table.dtype),
            pltpu.SemaphoreType.DMA((2,)),
            pltpu.SemaphoreType.DMA((2,)),
        ],
    )
    def gather_kernel(table_hbm, idx_hbm, out_hbm, idx_v, buf, gsem, wsem):
        wid = lax.axis_index("s") * SC_CORES + lax.axis_index("c")
        base = wid * per_w
        pltpu.sync_copy(idx_hbm.at[pl.ds(base, per_w)], idx_v)

        def gather(j, slot):
            rows = idx_v.at[pl.ds(j * chunk, chunk)]
            return pltpu.make_async_copy(table_hbm.at[rows], buf.at[slot], gsem.at[slot])

        def write(j, slot):
            dst = out_hbm.at[pl.ds(base + j * chunk, chunk)]
            src = buf.at[slot].reshape((chunk,) + out_row_shape)
            return pltpu.make_async_copy(src, dst, wsem.at[slot])

        gather(0, 0).start()

        @pl.loop(0, n_chunks, step=2)
        def _(j0):
            for slot in range(2):
                j = j0 + slot
                gather(j, slot).wait()

                @pl.when(j >= 1)
                def _():
                    write(j - 1, 1 - slot).wait()

                @pl.when(j + 1 < n_chunks)
                def _():
                    gather(j + 1, 1 - slot).start()

                write(j, slot).start()

        write(n_chunks - 1, (n_chunks - 1) % 2).wait()

    return gather_kernel(table, idx)


SC_LANES = 16


def _sc_dispatch(rows, meta, cls, rank, class_start):
    n_rows = rows.shape[0]
    chunk = COMBINE_CHUNK
    per_w = n_rows // SC_WORKERS
    n_chunks = per_w // chunk
    assert per_w * SC_WORKERS == n_rows and n_chunks * chunk == per_w
    assert n_chunks % 2 == 0 and chunk % SC_LANES == 0
    mesh = plsc.VectorSubcoreMesh(core_axis_name="c", subcore_axis_name="s")

    @functools.partial(
        pl.kernel, mesh=mesh,
        out_type=[
            jax.ShapeDtypeStruct((P_ROWS,) + rows.shape[1:], rows.dtype),
            jax.ShapeDtypeStruct((P_ROWS,) + meta.shape[1:], meta.dtype),
            jax.ShapeDtypeStruct((n_rows,), jnp.int32),
        ],
        scratch_types=[
            pltpu.VMEM((per_w,), jnp.int32),
            pltpu.VMEM((per_w,), jnp.int32),
            pltpu.VMEM((LANES,), jnp.int32),
            pltpu.VMEM((per_w,), jnp.int32),
            pltpu.VMEM((n_chunks, chunk), jnp.int32),
            pltpu.VMEM((2, chunk) + rows.shape[1:], rows.dtype),
            pltpu.VMEM((2, chunk) + meta.shape[1:], meta.dtype),
            pltpu.SemaphoreType.DMA((2,)),
            pltpu.SemaphoreType.DMA((2,)),
            pltpu.SemaphoreType.DMA((2,)),
            pltpu.SemaphoreType.DMA((2,)),
        ],
        compiler_params=pltpu.CompilerParams(needs_layout_passes=False),
    )
    def dispatch_kernel(rows_hbm, meta_hbm, cls_hbm, rank_hbm, start_hbm,
                        xs_hbm, ws_hbm, pos_hbm,
                        cls_v, rank_v, start_v, pos_flat, pos_v, rbuf, mbuf,
                        rsem, msem, xsem, wsem):
        wid = lax.axis_index("s") * SC_CORES + lax.axis_index("c")
        base = wid * per_w
        pltpu.sync_copy(cls_hbm.at[pl.ds(base, per_w)], cls_v)
        pltpu.sync_copy(rank_hbm.at[pl.ds(base, per_w)], rank_v)
        pltpu.sync_copy(start_hbm, start_v)

        @pl.loop(0, n_chunks)
        def _(j):
            for k in range(chunk // SC_LANES):
                off = j * chunk + k * SC_LANES
                c = cls_v[pl.ds(off, SC_LANES)]
                pos = plsc.load_gather(start_v, [c]) + rank_v[pl.ds(off, SC_LANES)]
                pos_flat[pl.ds(off, SC_LANES)] = pos
                pos_v[j, pl.ds(k * SC_LANES, SC_LANES)] = pos

        pltpu.sync_copy(pos_flat, pos_hbm.at[pl.ds(base, per_w)])

        def read_rows(j, slot):
            src = rows_hbm.at[pl.ds(base + j * chunk, chunk)]
            return pltpu.make_async_copy(src, rbuf.at[slot], rsem.at[slot])

        def read_meta(j, slot):
            src = meta_hbm.at[pl.ds(base + j * chunk, chunk)]
            return pltpu.make_async_copy(src, mbuf.at[slot], msem.at[slot])

        def put_rows(j, slot):
            return pltpu.make_async_copy(rbuf.at[slot], xs_hbm.at[pos_v.at[j]], xsem.at[slot])

        def put_meta(j, slot):
            return pltpu.make_async_copy(mbuf.at[slot], ws_hbm.at[pos_v.at[j]], wsem.at[slot])

        read_rows(0, 0).start()
        read_meta(0, 0).start()

        @pl.loop(0, n_chunks, step=2)
        def _(j0):
            for slot in range(2):
                j = j0 + slot
                read_rows(j, slot).wait()
                read_meta(j, slot).wait()

                @pl.when(j >= 1)
                def _():
                    put_rows(j - 1, 1 - slot).wait()
                    put_meta(j - 1, 1 - slot).wait()

                @pl.when(j + 1 < n_chunks)
                def _():
                    read_rows(j + 1, 1 - slot).start()
                    read_meta(j + 1, 1 - slot).start()

                put_rows(j, slot).start()
                put_meta(j, slot).start()

        put_rows(n_chunks - 1, (n_chunks - 1) % 2).wait()
        put_meta(n_chunks - 1, (n_chunks - 1) % 2).wait()

    return dispatch_kernel(rows, meta, cls, rank, class_start)


def _block_diag(w):
    h, d, _ = w.shape
    eye = jnp.eye(h, dtype=w.dtype)
    return (eye[:, None, :, None] * w[:, :, None, :]).reshape(h * d, h * d)


def _pair_tables():
    pair_of = np.zeros((EXPERTS_PER_GROUP, EXPERTS_PER_GROUP), np.int32)
    lo = np.zeros((N_PAIRS,), np.int32)
    hi = np.zeros((N_PAIRS,), np.int32)
    p = 0
    for a in range(EXPERTS_PER_GROUP):
        for b in range(a + 1, EXPERTS_PER_GROUP):
            pair_of[a, b] = pair_of[b, a] = p
            lo[p], hi[p] = a, b
            p += 1
    return pair_of, lo, hi


def _route_kernel(lt_ref, info_ref, meta_ref, counts_ref, run_ref):
    step = pl.program_id(0)

    @pl.when(step == 0)
    def _():
        run_ref[...] = jnp.zeros_like(run_ref)

    f32 = jnp.float32
    sub8 = lax.broadcasted_iota(jnp.int32, (8, LANES), 0).astype(f32)
    row_id = lax.broadcasted_iota(jnp.int32, (LANES, LANES), 0)
    col_id = lax.broadcasted_iota(jnp.int32, (LANES, LANES), 1)
    prefix_mat = (row_id <= col_id).astype(jnp.bfloat16)
    ones_mat = jnp.ones((LANES, LANES), jnp.bfloat16)
    neg_inf = f32(-jnp.inf)

    def first_index_of_max(v):
        m = jnp.max(v, axis=0, keepdims=True)
        idx = jnp.min(jnp.where(v == m, sub8, f32(8)), axis=0, keepdims=True)
        return m, idx

    def lane_tile(k, run):
        lanes = pl.ds(pl.multiple_of(k * LANES, LANES), LANES)
        g = jnp.where(sub8 < N_GROUPS, lt_ref[0:GROUP_ROWS, lanes], neg_inf)
        g_max, g_idx = first_index_of_max(g)
        g_top_p = 1.0 / jnp.sum(jnp.exp(g - g_max), axis=0, keepdims=True)

        e_sel = lt_ref[GROUP_ROWS:GROUP_ROWS + EXPERTS_PER_GROUP, lanes]
        for grp in range(1, N_GROUPS):
            lo = GROUP_ROWS + grp * EXPERTS_PER_GROUP
            e_sel = jnp.where(g_idx == grp, lt_ref[lo:lo + EXPERTS_PER_GROUP, lanes], e_sel)
        m1, i1 = first_index_of_max(e_sel)
        rest = jnp.where(sub8 == i1, neg_inf, e_sel)
        m2 = jnp.max(rest, axis=0, keepdims=True)
        i2 = jnp.min(jnp.where((rest == m2) & (sub8 != i1), sub8, f32(8)), axis=0, keepdims=True)

        e = jnp.exp(m2 - m1)
        w1 = g_top_p / (1.0 + e)
        w2 = g_top_p * e / (1.0 + e)
        first_is_lo = i1 < i2
        w_lo = jnp.where(first_is_lo, w1, w2)
        w_hi = jnp.where(first_is_lo, w2, w1)
        lo_e = jnp.minimum(i1, i2)
        hi_e = jnp.maximum(i1, i2)
        pair = lo_e * (2 * EXPERTS_PER_GROUP - 1 - lo_e) * 0.5 + hi_e - lo_e - 1.0
        cls = (g_idx * N_PAIRS + pair).astype(jnp.int32)

        onehot = (row_id == cls).astype(jnp.bfloat16)
        csum = jnp.dot(onehot, prefix_mat, preferred_element_type=f32)
        rank = jnp.sum(onehot.astype(f32) * (csum + run), axis=0, keepdims=True) - 1.0
        run = run + jnp.dot(onehot, ones_mat, preferred_element_type=f32)

        info_ref[:, lanes] = jnp.where(sub8 == 0, cls, jnp.where(sub8 == 1, rank.astype(jnp.int32), 0))
        meta8 = jnp.where(sub8 == 0, w_lo, jnp.where(sub8 == 1, w_hi, 0.0))
        meta_t = jnp.concatenate([meta8, jnp.zeros((LANES - 8, LANES), f32)], axis=0)
        meta_ref[lanes, :] = meta_t.T
        return run

    run = lax.fori_loop(0, ROUTE_BLOCK // LANES, lane_tile, run_ref[...], unroll=ROUTE_UNROLL)
    run_ref[...] = run
    counts_ref[...] = run


def _route_call(lt):
    return pl.pallas_call(
        _route_kernel,
        grid=(N_TOKENS // ROUTE_BLOCK,),
        in_specs=[pl.BlockSpec((ROUTER_ROWS, ROUTE_BLOCK), lambda i: (0, i))],
        out_specs=[
            pl.BlockSpec((8, ROUTE_BLOCK), lambda i: (0, i)),
            pl.BlockSpec((ROUTE_BLOCK, LANES), lambda i: (i, 0)),
            pl.BlockSpec((LANES, LANES), lambda i: (0, 0)),
        ],
        out_shape=[
            jax.ShapeDtypeStruct((8, N_TOKENS), jnp.int32),
            jax.ShapeDtypeStruct((N_TOKENS, LANES), jnp.float32),
            jax.ShapeDtypeStruct((LANES, LANES), jnp.float32),
        ],
        scratch_shapes=[pltpu.VMEM((LANES, LANES), jnp.float32)],
        compiler_params=pltpu.CompilerParams(dimension_semantics=("arbitrary",)),
        name="route_rank",
    )(lt)


def _dispatch_plan(counts):
    _, pair_lo, pair_hi = _pair_tables()
    tiles_c = (counts + TM - 1) // TM
    tiles_g = tiles_c.reshape(N_GROUPS, N_PAIRS).sum(axis=1)
    tiles_g_pad = (tiles_g + SUB - 1) // SUB * SUB
    g_start = jnp.cumsum(tiles_g_pad) - tiles_g_pad
    tc = tiles_c.reshape(N_GROUPS, N_PAIRS)
    c_start = (g_start[:, None] + jnp.cumsum(tc, axis=1) - tc).reshape(N_CLASSES)
    class_start = jnp.zeros((LANES,), jnp.int32).at[:N_CLASSES].set(c_start * TM)

    tile_ids = jnp.arange(N_TILES, dtype=jnp.int32)
    c_end = c_start + tiles_c
    owner = jnp.sum((tile_ids[:, None] >= c_end[None, :]).astype(jnp.int32), axis=1)
    owner = jnp.minimum(owner, N_CLASSES - 1)
    in_class = (tile_ids >= c_start[owner]) & (tile_ids < c_end[owner])
    rows_left = counts[owner] - (tile_ids - c_start[owner]) * TM
    tile_rows = jnp.where(in_class, jnp.minimum(rows_left, TM), 0).astype(jnp.int32)
    pair = owner % N_PAIRS
    tile_ea = jnp.asarray(pair_lo)[pair]
    tile_eb = jnp.asarray(pair_hi)[pair]
    g_end = g_start + tiles_g_pad
    step_first = jnp.arange(N_STEPS3, dtype=jnp.int32) * SUB
    step_group = jnp.sum((step_first[:, None] >= g_end[None, :]).astype(jnp.int32), axis=1)
    step_group = jnp.minimum(step_group, N_GROUPS - 1)
    last_step = jnp.maximum(g_end[-1] // SUB - 1, 0)
    step_block = jnp.minimum(jnp.arange(N_STEPS3, dtype=jnp.int32), last_step)
    return (class_start, tile_ea, tile_eb, tile_rows, step_group.astype(jnp.int32),
            step_block.astype(jnp.int32))


def kernel(x, w_in, rg_conv_w, rg_conv_b, rg_gate_a_w, rg_gate_a_b, rg_gate_x_w, rg_gate_x_b,
           rg_lambda, sc_conv_w, w_out, ln1_g, ln1_b, router_group_w, router_group_b,
           router_expert_w, router_expert_b, exp_w_gate, exp_w_up, exp_w_down, ln2_g, ln2_b):
    bf16 = jnp.bfloat16
    row = lambda v: v.reshape(1, -1)
    pad_g = GROUP_ROWS - N_GROUPS
    pad_e = ROUTER_LANES - ROUTER_ROWS
    wr = jnp.concatenate([router_group_w, jnp.zeros((D_MODEL, pad_g), jnp.float32),
                          router_expert_w, jnp.zeros((D_MODEL, pad_e), jnp.float32)], axis=1)
    br = jnp.concatenate([router_group_b, jnp.zeros((pad_g,), jnp.float32),
                          router_expert_b, jnp.zeros((pad_e,), jnp.float32)]).reshape(1, -1)
    wr_hi = wr.astype(bf16)
    wr_lo = (wr - wr_hi.astype(jnp.float32)).astype(bf16)

    h1, lt = _mixer_call(
        x, w_in.astype(bf16), rg_conv_w, row(rg_conv_b),
        _block_diag(rg_gate_a_w).astype(bf16), row(rg_gate_a_b),
        _block_diag(rg_gate_x_w).astype(bf16), row(rg_gate_x_b),
        row(rg_lambda), sc_conv_w, w_out.astype(bf16), row(ln1_g), row(ln1_b),
        jnp.concatenate([wr_hi, wr_lo], axis=1), br)

    info, meta, counts = _route_call(lt)
    class_start, tile_ea, tile_eb, tile_rows, step_group, step_block = _dispatch_plan(
        counts[:N_CLASSES, 0].astype(jnp.int32))

    row_tiles = (-1, LANE_TILES, LANES)
    xs, wts, pos = _sc_dispatch(h1.reshape(row_tiles), meta, info[0], info[1], class_start)

    grp_shape = (N_GROUPS, EXPERTS_PER_GROUP)
    wgu, wdb = _cast_expert_weights(exp_w_gate, exp_w_up, exp_w_down)
    ys = _expert_call(
        step_block, step_group, tile_ea, tile_eb, tile_rows, xs.reshape(-1, LANES), wts,
        wgu.reshape(grp_shape + (D_MODEL, 2 * D_EXPERT)),
        wdb.reshape(grp_shape + (D_EXPERT, D_MODEL)),
        row(ln2_g), row(ln2_b))

    out_rows = pos.reshape(SEQ, BATCH).T.reshape(-1)
    out = _sc_gather_rows(ys.reshape(row_tiles), out_rows, COMBINE_CHUNK, (D_MODEL,))
    return out.reshape(BATCH, SEQ, D_MODEL)
```

```python
import functools

import jax
import jax.numpy as jnp
import numpy as np
from jax import lax
from jax.experimental import pallas as pl
from jax.experimental.pallas import tpu as pltpu
from jax.experimental.pallas import tpu_sc as plsc

D_MODEL = 1024
BATCH = 16
SEQ = 2048
D_RG = 512
D_SC = 512
RG_HEADS = 8
RG_HEAD_DIM = D_RG // RG_HEADS
RG_CONV = 4
RG_C = 8.0
SC_CONV = 3
D_IN_PROJ = 2 * D_RG + 3 * D_SC
N_GROUPS = 4
EXPERTS_PER_GROUP = 8
N_EXPERTS = N_GROUPS * EXPERTS_PER_GROUP
D_EXPERT = D_MODEL // 4
LN_EPS = 1e-5
DEEPNORM_ALPHA = 2.0 ** 0.25

N_TOKENS = BATCH * SEQ
LANES = 128
LANE_TILES = D_MODEL // LANES
MXU_WIDTH = 256
ROUTER_LANES = LANES
GROUP_ROWS = 8
ROUTER_ROWS = GROUP_ROWS + N_EXPERTS

TS = 64
M1 = TS * BATCH
DOT_ROWS = 256

ROUTE_BLOCK = 4096
ROUTE_UNROLL = 4

N_PAIRS = EXPERTS_PER_GROUP * (EXPERTS_PER_GROUP - 1) // 2
N_CLASSES = N_GROUPS * N_PAIRS
BF16_ROWS = 16
TM = -(-int(N_TOKENS / N_CLASSES * 1.09 / 2) // BF16_ROWS) * BF16_ROWS
SUB = 8
N_TILES = -(-(N_TOKENS // TM + N_CLASSES + N_GROUPS * (SUB - 1)) // SUB) * SUB
N_STEPS3 = N_TILES // SUB
P_ROWS = N_TILES * TM

VMEM_LIMIT = 56 * 1024 * 1024


def _layer_norm(z, g, b):
    mu = jnp.mean(z, axis=-1, keepdims=True)
    zc = z - mu
    var = jnp.mean(zc * zc, axis=-1, keepdims=True)
    return zc * lax.rsqrt(var + LN_EPS) * g + b


def _store_row_tiles(ref, first_row, value):
    n = value.shape[0]
    for j in range(LANE_TILES):
        ref[pl.ds(first_row * LANE_TILES + j, n, stride=LANE_TILES), :] = (
            value[:, j * LANES:(j + 1) * LANES])


def _load_row_tiles(ref, first_row, n):
    return jnp.concatenate(
        [ref[pl.ds(first_row * LANE_TILES + j, n, stride=LANE_TILES), :]
         for j in range(LANE_TILES)], axis=1)


def _mixer_kernel(x_ref, w_in_ref, cw_ref, cb_ref, wa_ref, ba_ref, wx_ref, bx_ref, lam_ref,
                  scw_ref, w_out_ref, g1_ref, b1_ref, wr_ref, br_ref, wg_ref, wu_ref, wd_ref,
                  h1_ref, logit_ref, wgu_ref, wdb_ref,
                  xbuf_ref, xsem, cbuf_ref, sbuf_ref, a_ref, u_ref, hstate_ref):
    c = pl.program_id(0)

    wgu_ref[:, :, :D_EXPERT] = wg_ref[...].astype(jnp.bfloat16)
    wgu_ref[:, :, D_EXPERT:] = wu_ref[...].astype(jnp.bfloat16)
    wdb_ref[...] = wd_ref[...].astype(jnp.bfloat16)
    slot = c % 2
    rg_pad = (RG_CONV - 1) * BATCH
    sc_pad = (SC_CONV - 1) * BATCH

    def x_copies(chunk, slot_):
        return [pltpu.make_async_copy(x_ref.at[b, pl.ds(chunk * TS, TS), :],
                                      xbuf_ref.at[slot_, :, b, :], xsem.at[slot_])
                for b in range(BATCH)]

    @pl.when(c == 0)
    def _():
        for cp in x_copies(0, 0):
            cp.start()
        cbuf_ref[0:rg_pad, :] = jnp.zeros((rg_pad, D_RG), jnp.float32)
        sbuf_ref[0:sc_pad, :] = jnp.zeros((sc_pad, D_SC), jnp.float32)
        hstate_ref[...] = jnp.zeros_like(hstate_ref)

    @pl.when(c + 1 < SEQ // TS)
    def _():
        for cp in x_copies(c + 1, 1 - slot):
            cp.start()

    for cp in x_copies(c, slot):
        cp.wait()

    xt_ref = xbuf_ref.at[slot]
    def gate(xcb, w_ref, b_ref):
        parts = [jnp.dot(xcb[:, lo:lo + MXU_WIDTH], w_ref[lo:lo + MXU_WIDTH, lo:lo + MXU_WIDTH],
                         preferred_element_type=jnp.float32)
                 for lo in range(0, D_RG, MXU_WIDTH)]
        return 0.5 + 0.5 * jnp.tanh(0.5 * (jnp.concatenate(parts, axis=1) + b_ref[...]))

    soft_lam = jax.nn.softplus(-lam_ref[...])
    gelu_chunks, y_sc_chunks = [], []
    for r0 in range(0, M1, DOT_ROWS):
        t0 = r0 // BATCH
        xk = xt_ref[t0:t0 + DOT_ROWS // BATCH].reshape(DOT_ROWS, D_MODEL).astype(jnp.bfloat16)

        def proj(lo, width):
            return jnp.dot(xk, w_in_ref[:, lo:lo + width], preferred_element_type=jnp.float32)

        cbuf_ref[rg_pad + r0:rg_pad + r0 + DOT_ROWS, :] = proj(0, D_RG)
        rg_gate = proj(D_RG, D_RG)
        xc = cb_ref[...] + cw_ref[0:1, :] * cbuf_ref[r0:r0 + DOT_ROWS, :]
        for k in range(1, RG_CONV):
            xc = xc + cw_ref[k:k + 1, :] * cbuf_ref[r0 + k * BATCH:r0 + k * BATCH + DOT_ROWS, :]
        sc_b = proj(2 * D_RG, D_SC)
        xcb = xc.astype(jnp.bfloat16)
        r = gate(xcb, wa_ref, ba_ref)
        i = gate(xcb, wx_ref, bx_ref)
        sc_c = proj(2 * D_RG + D_SC, D_SC)
        log_a = (-RG_C) * r * soft_lam
        a_ref[r0:r0 + DOT_ROWS, :] = jnp.exp(log_a)
        th = jnp.tanh(log_a)
        u_ref[r0:r0 + DOT_ROWS, :] = jnp.sqrt(-2.0 * th / (1.0 - th)) * (i * xc)
        gelu_chunks.append(jax.nn.gelu(rg_gate))

        sbuf_ref[sc_pad + r0:sc_pad + r0 + DOT_ROWS, :] = sc_c * proj(2 * D_RG + 2 * D_SC, D_SC)
        conv = scw_ref[0:1, :] * sbuf_ref[r0:r0 + DOT_ROWS, :]
        for k in range(1, SC_CONV):
            conv = conv + scw_ref[k:k + 1, :] * sbuf_ref[r0 + k * BATCH:r0 + k * BATCH + DOT_ROWS, :]
        y_sc_chunks.append((sc_b * conv).astype(jnp.bfloat16))

    cbuf_ref[0:rg_pad, :] = cbuf_ref[M1:M1 + rg_pad, :]
    sbuf_ref[0:sc_pad, :] = sbuf_ref[M1:M1 + sc_pad, :]

    h = hstate_ref[...]
    for t in range(TS):
        h = a_ref[t * BATCH:(t + 1) * BATCH, :] * h + u_ref[t * BATCH:(t + 1) * BATCH, :]
        u_ref[t * BATCH:(t + 1) * BATCH, :] = h
    hstate_ref[...] = h

    tail_rows = M1 // 2
    per_tail = tail_rows // DOT_ROWS

    def out_proj(k):
        r0 = k * tail_rows
        gelu_k = jnp.concatenate(gelu_chunks[k * per_tail:(k + 1) * per_tail], axis=0)
        y_sc_k = jnp.concatenate(y_sc_chunks[k * per_tail:(k + 1) * per_tail], axis=0)
        y_rg = (u_ref[r0:r0 + tail_rows, :] * gelu_k).astype(jnp.bfloat16)
        mix = jnp.dot(y_rg, w_out_ref[0:D_RG, :], preferred_element_type=jnp.float32)
        return mix + jnp.dot(y_sc_k, w_out_ref[D_RG:, :], preferred_element_type=jnp.float32)

    def finish(k, mix):
        r0 = k * tail_rows
        t0 = r0 // BATCH
        x_rows = xt_ref[t0:t0 + tail_rows // BATCH].reshape(tail_rows, D_MODEL)
        h1 = _layer_norm(DEEPNORM_ALPHA * x_rows + mix, g1_ref[...], b1_ref[...])
        _store_row_tiles(h1_ref, r0, h1)
        h_hi = h1.astype(jnp.bfloat16)
        h_lo = (h1 - h_hi.astype(jnp.float32)).astype(jnp.bfloat16)
        both = jnp.dot(h_hi, wr_ref[...], preferred_element_type=jnp.float32)
        logits = both[:, :ROUTER_LANES] + both[:, ROUTER_LANES:] + br_ref[...]
        logits = logits + jnp.dot(h_lo, wr_ref[:, :ROUTER_LANES],
                                  preferred_element_type=jnp.float32)
        logit_ref[:, r0:r0 + tail_rows] = logits.T[:ROUTER_ROWS, :]

    n_tails = M1 // tail_rows
    mix = out_proj(0)
    for k in range(n_tails):
        next_mix = out_proj(k + 1) if k + 1 < n_tails else None
        finish(k, mix)
        mix = next_mix


def _const_spec(shape):
    return pl.BlockSpec(shape, lambda c: (0,) * len(shape))


def _mixer_call(x, w_in, cw, cb, wa, ba, wx, bx, lam, scw, w_out, g1, b1, wr, br, wg, wu, wd):
    n_chunks = SEQ // TS
    epc = N_EXPERTS // n_chunks
    assert epc * n_chunks == N_EXPERTS
    in_specs = [
        pl.BlockSpec(memory_space=pl.ANY),
        _const_spec((D_MODEL, D_IN_PROJ)),
        _const_spec((RG_CONV, D_RG)), _const_spec((1, D_RG)),
        _const_spec((D_RG, D_RG)), _const_spec((1, D_RG)),
        _const_spec((D_RG, D_RG)), _const_spec((1, D_RG)),
        _const_spec((1, D_RG)),
        _const_spec((SC_CONV, D_SC)),
        _const_spec((D_MODEL, D_MODEL)),
        _const_spec((1, D_MODEL)), _const_spec((1, D_MODEL)),
        _const_spec((D_MODEL, 2 * ROUTER_LANES)), _const_spec((1, ROUTER_LANES)),
        pl.BlockSpec((epc, D_MODEL, D_EXPERT), lambda c: (c, 0, 0)),
        pl.BlockSpec((epc, D_MODEL, D_EXPERT), lambda c: (c, 0, 0)),
        pl.BlockSpec((epc, D_EXPERT, D_MODEL), lambda c: (c, 0, 0)),
    ]
    out_specs = [
        pl.BlockSpec((M1 * LANE_TILES, LANES), lambda c: (c, 0)),
        pl.BlockSpec((ROUTER_ROWS, M1), lambda c: (0, c)),
        pl.BlockSpec((epc, D_MODEL, 2 * D_EXPERT), lambda c: (c, 0, 0)),
        pl.BlockSpec((epc, D_EXPERT, D_MODEL), lambda c: (c, 0, 0)),
    ]
    return pl.pallas_call(
        _mixer_kernel,
        grid=(n_chunks,),
        in_specs=in_specs,
        out_specs=out_specs,
        out_shape=[
            jax.ShapeDtypeStruct((N_TOKENS * LANE_TILES, LANES), jnp.float32),
            jax.ShapeDtypeStruct((ROUTER_ROWS, N_TOKENS), jnp.float32),
            jax.ShapeDtypeStruct((N_EXPERTS, D_MODEL, 2 * D_EXPERT), jnp.bfloat16),
            jax.ShapeDtypeStruct((N_EXPERTS, D_EXPERT, D_MODEL), jnp.bfloat16),
        ],
        scratch_shapes=[
            pltpu.VMEM((2, TS, BATCH, D_MODEL), jnp.float32),
            pltpu.SemaphoreType.DMA((2,)),
            pltpu.VMEM((M1 + (RG_CONV - 1) * BATCH, D_RG), jnp.float32),
            pltpu.VMEM((M1 + (SC_CONV - 1) * BATCH, D_SC), jnp.float32),
            pltpu.VMEM((M1, D_RG), jnp.float32),
            pltpu.VMEM((M1, D_RG), jnp.float32),
            pltpu.VMEM((BATCH, D_RG), jnp.float32),
        ],
        compiler_params=pltpu.CompilerParams(
            dimension_semantics=("arbitrary",), vmem_limit_bytes=VMEM_LIMIT),
        name="mixer_ln_router",
    )(x, w_in, cw, cb, wa, ba, wx, bx, lam, scw, w_out, g1, b1, wr, br, wg, wu, wd)


def _expert_kernel(blk_ref, grp_ref, ea_ref, eb_ref, rows_ref,
                   xs_ref, wt_ref, wgu_ref, wd_ref, g2_ref, b2_ref, out_ref):
    s = pl.program_id(0)
    tile_rows = [rows_ref[s * SUB + j] for j in range(SUB)]

    @pl.when(blk_ref[s] == s)
    def _():
        for j in range(SUB):
            tile = s * SUB + j
            first_row = j * TM
            in_use = lax.broadcasted_iota(jnp.int32, (TM, 1), 0) < tile_rows[j]
            x = jnp.where(in_use, _load_row_tiles(xs_ref, first_row, TM), 0.0)
            xb = x.astype(jnp.bfloat16)
            wt = jnp.where(in_use, wt_ref[first_row:first_row + TM, :], 0.0)
            y = jnp.zeros((TM, D_MODEL), jnp.float32)
            for e_ref, col in ((ea_ref, 0), (eb_ref, 1)):
                e = e_ref[tile]
                hgu = jnp.dot(xb, wgu_ref[e], preferred_element_type=jnp.float32)
                hid = jax.nn.silu(hgu[:, :D_EXPERT]) * hgu[:, D_EXPERT:] * wt[:, col:col + 1]
                y = y + jnp.dot(hid.astype(jnp.bfloat16), wd_ref[e],
                                preferred_element_type=jnp.float32)
            _store_row_tiles(out_ref, first_row,
                             _layer_norm(DEEPNORM_ALPHA * x + y, g2_ref[...], b2_ref[...]))


def _expert_call(step_block, step_group, tile_ea, tile_eb, tile_rows, xs, wts, wgu, wd, g2, b2):
    rows = SUB * TM
    grid_spec = pltpu.PrefetchScalarGridSpec(
        num_scalar_prefetch=5,
        grid=(N_STEPS3,),
        in_specs=[
            pl.BlockSpec((rows * LANE_TILES, LANES), lambda s, blk, grp, ea, eb, va: (blk[s], 0)),
            pl.BlockSpec((rows, LANES), lambda s, blk, grp, ea, eb, va: (blk[s], 0)),
            pl.BlockSpec((None, EXPERTS_PER_GROUP, D_MODEL, 2 * D_EXPERT),
                         lambda s, blk, grp, ea, eb, va: (grp[s], 0, 0, 0)),
            pl.BlockSpec((None, EXPERTS_PER_GROUP, D_EXPERT, D_MODEL),
                         lambda s, blk, grp, ea, eb, va: (grp[s], 0, 0, 0)),
            pl.BlockSpec((1, D_MODEL), lambda s, blk, grp, ea, eb, va: (0, 0)),
            pl.BlockSpec((1, D_MODEL), lambda s, blk, grp, ea, eb, va: (0, 0)),
        ],
        out_specs=pl.BlockSpec((rows * LANE_TILES, LANES),
                               lambda s, blk, grp, ea, eb, va: (blk[s], 0)),
    )
    return pl.pallas_call(
        _expert_kernel,
        grid_spec=grid_spec,
        out_shape=jax.ShapeDtypeStruct((P_ROWS * LANE_TILES, LANES), jnp.float32),
        compiler_params=pltpu.CompilerParams(
            dimension_semantics=("arbitrary",), vmem_limit_bytes=VMEM_LIMIT),
        name="experts_ln",
    )(step_block, step_group, tile_ea, tile_eb, tile_rows, xs, wts, wgu, wd, g2, b2)


SC_CORES = 2
SC_SUBCORES = 16
SC_WORKERS = SC_CORES * SC_SUBCORES
COMBINE_CHUNK = 32


def _sc_gather_rows(table, idx, chunk, out_row_shape=None):
    n_out, = idx.shape
    row_shape = table.shape[1:]
    out_row_shape = row_shape if out_row_shape is None else out_row_shape
    per_w = n_out // SC_WORKERS
    n_chunks = per_w // chunk
    assert per_w * SC_WORKERS == n_out and n_chunks * chunk == per_w
    assert n_chunks % 2 == 0 and chunk % 8 == 0
    mesh = plsc.VectorSubcoreMesh(core_axis_name="c", subcore_axis_name="s")

    @functools.partial(
        pl.kernel, mesh=mesh,
        out_type=jax.ShapeDtypeStruct((n_out,) + out_row_shape, table.dtype),
        scratch_types=[
            pltpu.VMEM((per_w,), jnp.int32),
            pltpu.VMEM((2, chunk) + row_shape, table.dtype),
            pltpu.SemaphoreType.DMA((2,)),
            pltpu.SemaphoreType.DMA((2,)),
        ],
    )
    def gather_kernel(table_hbm, idx_hbm, out_hbm, idx_v, buf, gsem, wsem):
        wid = lax.axis_index("s") * SC_CORES + lax.axis_index("c")
        base = wid * per_w
        pltpu.sync_copy(idx_hbm.at[pl.ds(base, per_w)], idx_v)

        def gather(j, slot):
            rows = idx_v.at[pl.ds(j * chunk, chunk)]
            return pltpu.make_async_copy(table_hbm.at[rows], buf.at[slot], gsem.at[slot])

        def write(j, slot):
            dst = out_hbm.at[pl.ds(base + j * chunk, chunk)]
            src = buf.at[slot].reshape((chunk,) + out_row_shape)
            return pltpu.make_async_copy(src, dst, wsem.at[slot])

        gather(0, 0).start()

        @pl.loop(0, n_chunks, step=2)
        def _(j0):
            for slot in range(2):
                j = j0 + slot
                gather(j, slot).wait()

                @pl.when(j >= 1)
                def _():
                    write(j - 1, 1 - slot).wait()

                @pl.when(j + 1 < n_chunks)
                def _():
                    gather(j + 1, 1 - slot).start()

                write(j, slot).start()

        write(n_chunks - 1, (n_chunks - 1) % 2).wait()

    return gather_kernel(table, idx)


SC_LANES = 16


def _sc_dispatch(rows, meta, cls, rank, class_start):
    n_rows = rows.shape[0]
    chunk = COMBINE_CHUNK
    per_w = n_rows // SC_WORKERS
    n_chunks = per_w // chunk
    assert per_w * SC_WORKERS == n_rows and n_chunks * chunk == per_w
    assert n_chunks % 2 == 0 and chunk % SC_LANES == 0
    mesh = plsc.VectorSubcoreMesh(core_axis_name="c", subcore_axis_name="s")

    @functools.partial(
        pl.kernel, mesh=mesh,
        out_type=[
            jax.ShapeDtypeStruct((P_ROWS,) + rows.shape[1:], rows.dtype),
            jax.ShapeDtypeStruct((P_ROWS,) + meta.shape[1:], meta.dtype),
            jax.ShapeDtypeStruct((n_rows,), jnp.int32),
        ],
        scratch_types=[
            pltpu.VMEM((per_w,), jnp.int32),
            pltpu.VMEM((per_w,), jnp.int32),
            pltpu.VMEM((LANES,), jnp.int32),
            pltpu.VMEM((per_w,), jnp.int32),
            pltpu.VMEM((n_chunks, chunk), jnp.int32),
            pltpu.VMEM((2, chunk) + rows.shape[1:], rows.dtype),
            pltpu.VMEM((2, chunk) + meta.shape[1:], meta.dtype),
            pltpu.SemaphoreType.DMA((2,)),
            pltpu.SemaphoreType.DMA((2,)),
            pltpu.SemaphoreType.DMA((2,)),
            pltpu.SemaphoreType.DMA((2,)),
        ],
        compiler_params=pltpu.CompilerParams(needs_layout_passes=False),
    )
    def dispatch_kernel(rows_hbm, meta_hbm, cls_hbm, rank_hbm, start_hbm,
                        xs_hbm, ws_hbm, pos_hbm,
                        cls_v, rank_v, start_v, pos_flat, pos_v, rbuf, mbuf,
                        rsem, msem, xsem, wsem):
        wid = lax.axis_index("s") * SC_CORES + lax.axis_index("c")
        base = wid * per_w
        pltpu.sync_copy(cls_hbm.at[pl.ds(base, per_w)], cls_v)
        pltpu.sync_copy(rank_hbm.at[pl.ds(base, per_w)], rank_v)
        pltpu.sync_copy(start_hbm, start_v)

        @pl.loop(0, n_chunks)
        def _(j):
            for k in range(chunk // SC_LANES):
                off = j * chunk + k * SC_LANES
                c = cls_v[pl.ds(off, SC_LANES)]
                pos = plsc.load_gather(start_v, [c]) + rank_v[pl.ds(off, SC_LANES)]
                pos_flat[pl.ds(off, SC_LANES)] = pos
                pos_v[j, pl.ds(k * SC_LANES, SC_LANES)] = pos

        pltpu.sync_copy(pos_flat, pos_hbm.at[pl.ds(base, per_w)])

        def read_rows(j, slot):
            src = rows_hbm.at[pl.ds(base + j * chunk, chunk)]
            return pltpu.make_async_copy(src, rbuf.at[slot], rsem.at[slot])

        def read_meta(j, slot):
            src = meta_hbm.at[pl.ds(base + j * chunk, chunk)]
            return pltpu.make_async_copy(src, mbuf.at[slot], msem.at[slot])

        def put_rows(j, slot):
            return pltpu.make_async_copy(rbuf.at[slot], xs_hbm.at[pos_v.at[j]], xsem.at[slot])

        def put_meta(j, slot):
            return pltpu.make_async_copy(mbuf.at[slot], ws_hbm.at[pos_v.at[j]], wsem.at[slot])

        read_rows(0, 0).start()
        read_meta(0, 0).start()

        @pl.loop(0, n_chunks, step=2)
        def _(j0):
            for slot in range(2):
                j = j0 + slot
                read_rows(j, slot).wait()
                read_meta(j, slot).wait()

                @pl.when(j >= 1)
                def _():
                    put_rows(j - 1, 1 - slot).wait()
                    put_meta(j - 1, 1 - slot).wait()

                @pl.when(j + 1 < n_chunks)
                def _():
                    read_rows(j + 1, 1 - slot).start()
                    read_meta(j + 1, 1 - slot).start()

                put_rows(j, slot).start()
                put_meta(j, slot).start()

        put_rows(n_chunks - 1, (n_chunks - 1) % 2).wait()
        put_meta(n_chunks - 1, (n_chunks - 1) % 2).wait()

    return dispatch_kernel(rows, meta, cls, rank, class_start)


def _block_diag(w):
    h, d, _ = w.shape
    eye = jnp.eye(h, dtype=w.dtype)
    return (eye[:, None, :, None] * w[:, :, None, :]).reshape(h * d, h * d)


def _pair_tables():
    pair_of = np.zeros((EXPERTS_PER_GROUP, EXPERTS_PER_GROUP), np.int32)
    lo = np.zeros((N_PAIRS,), np.int32)
    hi = np.zeros((N_PAIRS,), np.int32)
    p = 0
    for a in range(EXPERTS_PER_GROUP):
        for b in range(a + 1, EXPERTS_PER_GROUP):
            pair_of[a, b] = pair_of[b, a] = p
            lo[p], hi[p] = a, b
            p += 1
    return pair_of, lo, hi


def _route_kernel(lt_ref, info_ref, meta_ref, counts_ref, run_ref):
    step = pl.program_id(0)

    @pl.when(step == 0)
    def _():
        run_ref[...] = jnp.zeros_like(run_ref)

    f32 = jnp.float32
    sub8 = lax.broadcasted_iota(jnp.int32, (8, LANES), 0).astype(f32)
    row_id = lax.broadcasted_iota(jnp.int32, (LANES, LANES), 0)
    col_id = lax.broadcasted_iota(jnp.int32, (LANES, LANES), 1)
    prefix_mat = (row_id <= col_id).astype(jnp.bfloat16)
    ones_mat = jnp.ones((LANES, LANES), jnp.bfloat16)
    neg_inf = f32(-jnp.inf)

    def first_index_of_max(v):
        m = jnp.max(v, axis=0, keepdims=True)
        idx = jnp.min(jnp.where(v == m, sub8, f32(8)), axis=0, keepdims=True)
        return m, idx

    def lane_tile(k, run):
        lanes = pl.ds(pl.multiple_of(k * LANES, LANES), LANES)
        g = jnp.where(sub8 < N_GROUPS, lt_ref[0:GROUP_ROWS, lanes], neg_inf)
        g_max, g_idx = first_index_of_max(g)
        g_top_p = 1.0 / jnp.sum(jnp.exp(g - g_max), axis=0, keepdims=True)

        e_sel = lt_ref[GROUP_ROWS:GROUP_ROWS + EXPERTS_PER_GROUP, lanes]
        for grp in range(1, N_GROUPS):
            lo = GROUP_ROWS + grp * EXPERTS_PER_GROUP
            e_sel = jnp.where(g_idx == grp, lt_ref[lo:lo + EXPERTS_PER_GROUP, lanes], e_sel)
        m1, i1 = first_index_of_max(e_sel)
        rest = jnp.where(sub8 == i1, neg_inf, e_sel)
        m2 = jnp.max(rest, axis=0, keepdims=True)
        i2 = jnp.min(jnp.where((rest == m2) & (sub8 != i1), sub8, f32(8)), axis=0, keepdims=True)

        e = jnp.exp(m2 - m1)
        w1 = g_top_p / (1.0 + e)
        w2 = g_top_p * e / (1.0 + e)
        first_is_lo = i1 < i2
        w_lo = jnp.where(first_is_lo, w1, w2)
        w_hi = jnp.where(first_is_lo, w2, w1)
        lo_e = jnp.minimum(i1, i2)
        hi_e = jnp.maximum(i1, i2)
        pair = lo_e * (2 * EXPERTS_PER_GROUP - 1 - lo_e) * 0.5 + hi_e - lo_e - 1.0
        cls = (g_idx * N_PAIRS + pair).astype(jnp.int32)

        onehot = (row_id == cls).astype(jnp.bfloat16)
        csum = jnp.dot(onehot, prefix_mat, preferred_element_type=f32)
        rank = jnp.sum(onehot.astype(f32) * (csum + run), axis=0, keepdims=True) - 1.0
        run = run + jnp.dot(onehot, ones_mat, preferred_element_type=f32)

        info_ref[:, lanes] = jnp.where(sub8 == 0, cls, jnp.where(sub8 == 1, rank.astype(jnp.int32), 0))
        meta8 = jnp.where(sub8 == 0, w_lo, jnp.where(sub8 == 1, w_hi, 0.0))
        meta_t = jnp.concatenate([meta8, jnp.zeros((LANES - 8, LANES), f32)], axis=0)
        meta_ref[lanes, :] = meta_t.T
        return run

    run = lax.fori_loop(0, ROUTE_BLOCK // LANES, lane_tile, run_ref[...], unroll=ROUTE_UNROLL)
    run_ref[...] = run
    counts_ref[...] = run


def _route_call(lt):
    return pl.pallas_call(
        _route_kernel,
        grid=(N_TOKENS // ROUTE_BLOCK,),
        in_specs=[pl.BlockSpec((ROUTER_ROWS, ROUTE_BLOCK), lambda i: (0, i))],
        out_specs=[
            pl.BlockSpec((8, ROUTE_BLOCK), lambda i: (0, i)),
            pl.BlockSpec((ROUTE_BLOCK, LANES), lambda i: (i, 0)),
            pl.BlockSpec((LANES, LANES), lambda i: (0, 0)),
        ],
        out_shape=[
            jax.ShapeDtypeStruct((8, N_TOKENS), jnp.int32),
            jax.ShapeDtypeStruct((N_TOKENS, LANES), jnp.float32),
            jax.ShapeDtypeStruct((LANES, LANES), jnp.float32),
        ],
        scratch_shapes=[pltpu.VMEM((LANES, LANES), jnp.float32)],
        compiler_params=pltpu.CompilerParams(dimension_semantics=("arbitrary",)),
        name="route_rank",
    )(lt)


def _dispatch_plan(counts):
    _, pair_lo, pair_hi = _pair_tables()
    tiles_c = (counts + TM - 1) // TM
    tiles_g = tiles_c.reshape(N_GROUPS, N_PAIRS).sum(axis=1)
    tiles_g_pad = (tiles_g + SUB - 1) // SUB * SUB
    g_start = jnp.cumsum(tiles_g_pad) - tiles_g_pad
    tc = tiles_c.reshape(N_GROUPS, N_PAIRS)
    c_start = (g_start[:, None] + jnp.cumsum(tc, axis=1) - tc).reshape(N_CLASSES)
    class_start = jnp.zeros((LANES,), jnp.int32).at[:N_CLASSES].set(c_start * TM)

    tile_ids = jnp.arange(N_TILES, dtype=jnp.int32)
    c_end = c_start + tiles_c
    owner = jnp.sum((tile_ids[:, None] >= c_end[None, :]).astype(jnp.int32), axis=1)
    owner = jnp.minimum(owner, N_CLASSES - 1)
    in_class = (tile_ids >= c_start[owner]) & (tile_ids < c_end[owner])
    rows_left = counts[owner] - (tile_ids - c_start[owner]) * TM
    tile_rows = jnp.where(in_class, jnp.minimum(rows_left, TM), 0).astype(jnp.int32)
    pair = owner % N_PAIRS
    tile_ea = jnp.asarray(pair_lo)[pair]
    tile_eb = jnp.asarray(pair_hi)[pair]
    g_end = g_start + tiles_g_pad
    step_first = jnp.arange(N_STEPS3, dtype=jnp.int32) * SUB
    step_group = jnp.sum((step_first[:, None] >= g_end[None, :]).astype(jnp.int32), axis=1)
    step_group = jnp.minimum(step_group, N_GROUPS - 1)
    last_step = jnp.maximum(g_end[-1] // SUB - 1, 0)
    step_block = jnp.minimum(jnp.arange(N_STEPS3, dtype=jnp.int32), last_step)
    return (class_start, tile_ea, tile_eb, tile_rows, step_group.astype(jnp.int32),
            step_block.astype(jnp.int32))


def kernel(x, w_in, rg_conv_w, rg_conv_b, rg_gate_a_w, rg_gate_a_b, rg_gate_x_w, rg_gate_x_b,
           rg_lambda, sc_conv_w, w_out, ln1_g, ln1_b, router_group_w, router_group_b,
           router_expert_w, router_expert_b, exp_w_gate, exp_w_up, exp_w_down, ln2_g, ln2_b):
    bf16 = jnp.bfloat16
    row = lambda v: v.reshape(1, -1)
    pad_g = GROUP_ROWS - N_GROUPS
    pad_e = ROUTER_LANES - ROUTER_ROWS
    wr = jnp.concatenate([router_group_w, jnp.zeros((D_MODEL, pad_g), jnp.float32),
                          router_expert_w, jnp.zeros((D_MODEL, pad_e), jnp.float32)], axis=1)
    br = jnp.concatenate([router_group_b, jnp.zeros((pad_g,), jnp.float32),
                          router_expert_b, jnp.zeros((pad_e,), jnp.float32)]).reshape(1, -1)
    wr_hi = wr.astype(bf16)
    wr_lo = (wr - wr_hi.astype(jnp.float32)).astype(bf16)

    h1, lt, wgu, wdb = _mixer_call(
        x, w_in.astype(bf16), rg_conv_w, row(rg_conv_b),
        _block_diag(rg_gate_a_w).astype(bf16), row(rg_gate_a_b),
        _block_diag(rg_gate_x_w).astype(bf16), row(rg_gate_x_b),
        row(rg_lambda), sc_conv_w, w_out.astype(bf16), row(ln1_g), row(ln1_b),
        jnp.concatenate([wr_hi, wr_lo], axis=1), br, exp_w_gate, exp_w_up, exp_w_down)

    info, meta, counts = _route_call(lt)
    class_start, tile_ea, tile_eb, tile_rows, step_group, step_block = _dispatch_plan(
        counts[:N_CLASSES, 0].astype(jnp.int32))

    row_tiles = (-1, LANE_TILES, LANES)
    xs, wts, pos = _sc_dispatch(h1.reshape(row_tiles), meta, info[0], info[1], class_start)

    grp_shape = (N_GROUPS, EXPERTS_PER_GROUP)
    ys = _expert_call(
        step_block, step_group, tile_ea, tile_eb, tile_rows, xs.reshape(-1, LANES), wts,
        wgu.reshape(grp_shape + (D_MODEL, 2 * D_EXPERT)),
        wdb.reshape(grp_shape + (D_EXPERT, D_MODEL)),
        row(ln2_g), row(ln2_b))

    out_rows = pos.reshape(SEQ, BATCH).T.reshape(-1)
    out = _sc_gather_rows(ys.reshape(row_tiles), out_rows, COMBINE_CHUNK, (D_MODEL,))
    return out.reshape(BATCH, SEQ, D_MODEL)
```

```python
import functools

import jax
import jax.numpy as jnp
import numpy as np
from jax import lax
from jax.experimental import pallas as pl
from jax.experimental.pallas import tpu as pltpu
from jax.experimental.pallas import tpu_sc as plsc

D_MODEL = 1024
BATCH = 16
SEQ = 2048
D_RG = 512
D_SC = 512
RG_HEADS = 8
RG_HEAD_DIM = D_RG // RG_HEADS
RG_CONV = 4
RG_C = 8.0
SC_CONV = 3
D_IN_PROJ = 2 * D_RG + 3 * D_SC
N_GROUPS = 4
EXPERTS_PER_GROUP = 8
N_EXPERTS = N_GROUPS * EXPERTS_PER_GROUP
D_EXPERT = D_MODEL // 4
LN_EPS = 1e-5
DEEPNORM_ALPHA = 2.0 ** 0.25

N_TOKENS = BATCH * SEQ
LANES = 128
LANE_TILES = D_MODEL // LANES
MXU_WIDTH = 256
ROUTER_LANES = LANES
GROUP_ROWS = 8
ROUTER_ROWS = GROUP_ROWS + N_EXPERTS

TS = 64
M1 = TS * BATCH
DOT_ROWS = 256

ROUTE_BLOCK = 4096
ROUTE_UNROLL = 4

N_PAIRS = EXPERTS_PER_GROUP * (EXPERTS_PER_GROUP - 1) // 2
N_CLASSES = N_GROUPS * N_PAIRS
BF16_ROWS = 16
TM = -(-int(N_TOKENS / N_CLASSES * 1.09) // BF16_ROWS) * BF16_ROWS
SUB = 4
N_TILES = -(-(N_TOKENS // TM + N_CLASSES + N_GROUPS * (SUB - 1)) // SUB) * SUB
N_STEPS3 = N_TILES // SUB
P_ROWS = N_TILES * TM

VMEM_LIMIT = 56 * 1024 * 1024


def _layer_norm(z, g, b):
    mu = jnp.mean(z, axis=-1, keepdims=True)
    zc = z - mu
    var = jnp.mean(zc * zc, axis=-1, keepdims=True)
    return zc * lax.rsqrt(var + LN_EPS) * g + b


def _store_row_tiles(ref, first_row, value):
    n = value.shape[0]
    for j in range(LANE_TILES):
        ref[pl.ds(first_row * LANE_TILES + j, n, stride=LANE_TILES), :] = (
            value[:, j * LANES:(j + 1) * LANES])


def _load_row_tiles(ref, first_row, n):
    return jnp.concatenate(
        [ref[pl.ds(first_row * LANE_TILES + j, n, stride=LANE_TILES), :]
         for j in range(LANE_TILES)], axis=1)


def _mixer_kernel(x_ref, w_in_ref, cw_ref, cb_ref, wa_ref, ba_ref, wx_ref, bx_ref, lam_ref,
                  scw_ref, w_out_ref, g1_ref, b1_ref, wr_ref, br_ref, wg_ref, wu_ref, wd_ref,
                  h1_ref, logit_ref, wgu_ref, wdb_ref,
                  xbuf_ref, xsem, cbuf_ref, sbuf_ref, a_ref, u_ref, hstate_ref):
    c = pl.program_id(0)

    wgu_ref[:, :, :D_EXPERT] = wg_ref[...].astype(jnp.bfloat16)
    wgu_ref[:, :, D_EXPERT:] = wu_ref[...].astype(jnp.bfloat16)
    wdb_ref[...] = wd_ref[...].astype(jnp.bfloat16)
    slot = c % 2
    rg_pad = (RG_CONV - 1) * BATCH
    sc_pad = (SC_CONV - 1) * BATCH

    def x_copies(chunk, slot_):
        return [pltpu.make_async_copy(x_ref.at[b, pl.ds(chunk * TS, TS), :],
                                      xbuf_ref.at[slot_, :, b, :], xsem.at[slot_])
                for b in range(BATCH)]

    @pl.when(c == 0)
    def _():
        for cp in x_copies(0, 0):
            cp.start()
        cbuf_ref[0:rg_pad, :] = jnp.zeros((rg_pad, D_RG), jnp.float32)
        sbuf_ref[0:sc_pad, :] = jnp.zeros((sc_pad, D_SC), jnp.float32)
        hstate_ref[...] = jnp.zeros_like(hstate_ref)

    @pl.when(c + 1 < SEQ // TS)
    def _():
        for cp in x_copies(c + 1, 1 - slot):
            cp.start()

    for cp in x_copies(c, slot):
        cp.wait()

    xt_ref = xbuf_ref.at[slot]
    def gate(xcb, w_ref, b_ref):
        parts = [jnp.dot(xcb[:, lo:lo + MXU_WIDTH], w_ref[lo:lo + MXU_WIDTH, lo:lo + MXU_WIDTH],
                         preferred_element_type=jnp.float32)
                 for lo in range(0, D_RG, MXU_WIDTH)]
        return 0.5 + 0.5 * jnp.tanh(0.5 * (jnp.concatenate(parts, axis=1) + b_ref[...]))

    soft_lam = jax.nn.softplus(-lam_ref[...])
    gelu_chunks, y_sc_chunks = [], []
    for r0 in range(0, M1, DOT_ROWS):
        t0 = r0 // BATCH
        xk = xt_ref[t0:t0 + DOT_ROWS // BATCH].reshape(DOT_ROWS, D_MODEL).astype(jnp.bfloat16)

        def proj(lo, width):
            return jnp.dot(xk, w_in_ref[:, lo:lo + width], preferred_element_type=jnp.float32)

        cbuf_ref[rg_pad + r0:rg_pad + r0 + DOT_ROWS, :] = proj(0, D_RG)
        rg_gate = proj(D_RG, D_RG)
        xc = cb_ref[...] + cw_ref[0:1, :] * cbuf_ref[r0:r0 + DOT_ROWS, :]
        for k in range(1, RG_CONV):
            xc = xc + cw_ref[k:k + 1, :] * cbuf_ref[r0 + k * BATCH:r0 + k * BATCH + DOT_ROWS, :]
        sc_b = proj(2 * D_RG, D_SC)
        xcb = xc.astype(jnp.bfloat16)
        r = gate(xcb, wa_ref, ba_ref)
        i = gate(xcb, wx_ref, bx_ref)
        sc_c = proj(2 * D_RG + D_SC, D_SC)
        log_a = (-RG_C) * r * soft_lam
        a_ref[r0:r0 + DOT_ROWS, :] = jnp.exp(log_a)
        th = jnp.tanh(log_a)
        u_ref[r0:r0 + DOT_ROWS, :] = jnp.sqrt(-2.0 * th / (1.0 - th)) * (i * xc)
        gelu_chunks.append(jax.nn.gelu(rg_gate))

        sbuf_ref[sc_pad + r0:sc_pad + r0 + DOT_ROWS, :] = sc_c * proj(2 * D_RG + 2 * D_SC, D_SC)
        conv = scw_ref[0:1, :] * sbuf_ref[r0:r0 + DOT_ROWS, :]
        for k in range(1, SC_CONV):
            conv = conv + scw_ref[k:k + 1, :] * sbuf_ref[r0 + k * BATCH:r0 + k * BATCH + DOT_ROWS, :]
        y_sc_chunks.append((sc_b * conv).astype(jnp.bfloat16))

    cbuf_ref[0:rg_pad, :] = cbuf_ref[M1:M1 + rg_pad, :]
    sbuf_ref[0:sc_pad, :] = sbuf_ref[M1:M1 + sc_pad, :]

    h = hstate_ref[...]
    for t in range(TS):
        h = a_ref[t * BATCH:(t + 1) * BATCH, :] * h + u_ref[t * BATCH:(t + 1) * BATCH, :]
        u_ref[t * BATCH:(t + 1) * BATCH, :] = h
    hstate_ref[...] = h

    tail_rows = M1 // 2
    per_tail = tail_rows // DOT_ROWS

    def out_proj(k):
        r0 = k * tail_rows
        gelu_k = jnp.concatenate(gelu_chunks[k * per_tail:(k + 1) * per_tail], axis=0)
        y_sc_k = jnp.concatenate(y_sc_chunks[k * per_tail:(k + 1) * per_tail], axis=0)
        y_rg = (u_ref[r0:r0 + tail_rows, :] * gelu_k).astype(jnp.bfloat16)
        mix = jnp.dot(y_rg, w_out_ref[0:D_RG, :], preferred_element_type=jnp.float32)
        return mix + jnp.dot(y_sc_k, w_out_ref[D_RG:, :], preferred_element_type=jnp.float32)

    def finish(k, mix):
        r0 = k * tail_rows
        t0 = r0 // BATCH
        x_rows = xt_ref[t0:t0 + tail_rows // BATCH].reshape(tail_rows, D_MODEL)
        h1 = _layer_norm(DEEPNORM_ALPHA * x_rows + mix, g1_ref[...], b1_ref[...])
        _store_row_tiles(h1_ref, r0, h1)
        h_hi = h1.astype(jnp.bfloat16)
        h_lo = (h1 - h_hi.astype(jnp.float32)).astype(jnp.bfloat16)
        both = jnp.dot(h_hi, wr_ref[...], preferred_element_type=jnp.float32)
        logits = both[:, :ROUTER_LANES] + both[:, ROUTER_LANES:] + br_ref[...]
        logits = logits + jnp.dot(h_lo, wr_ref[:, :ROUTER_LANES],
                                  preferred_element_type=jnp.float32)
        logit_ref[:, r0:r0 + tail_rows] = logits.T[:ROUTER_ROWS, :]

    n_tails = M1 // tail_rows
    mix = out_proj(0)
    for k in range(n_tails):
        next_mix = out_proj(k + 1) if k + 1 < n_tails else None
        finish(k, mix)
        mix = next_mix


def _const_spec(shape):
    return pl.BlockSpec(shape, lambda c: (0,) * len(shape))


def _mixer_call(x, w_in, cw, cb, wa, ba, wx, bx, lam, scw, w_out, g1, b1, wr, br, wg, wu, wd):
    n_chunks = SEQ // TS
    epc = N_EXPERTS // n_chunks
    assert epc * n_chunks == N_EXPERTS
    in_specs = [
        pl.BlockSpec(memory_space=pl.ANY),
        _const_spec((D_MODEL, D_IN_PROJ)),
        _const_spec((RG_CONV, D_RG)), _const_spec((1, D_RG)),
        _const_spec((D_RG, D_RG)), _const_spec((1, D_RG)),
        _const_spec((D_RG, D_RG)), _const_spec((1, D_RG)),
        _const_spec((1, D_RG)),
        _const_spec((SC_CONV, D_SC)),
        _const_spec((D_MODEL, D_MODEL)),
        _const_spec((1, D_MODEL)), _const_spec((1, D_MODEL)),
        _const_spec((D_MODEL, 2 * ROUTER_LANES)), _const_spec((1, ROUTER_LANES)),
        pl.BlockSpec((epc, D_MODEL, D_EXPERT), lambda c: (c, 0, 0)),
        pl.BlockSpec((epc, D_MODEL, D_EXPERT), lambda c: (c, 0, 0)),
        pl.BlockSpec((epc, D_EXPERT, D_MODEL), lambda c: (c, 0, 0)),
    ]
    out_specs = [
        pl.BlockSpec((M1 * LANE_TILES, LANES), lambda c: (c, 0)),
        pl.BlockSpec((ROUTER_ROWS, M1), lambda c: (0, c)),
        pl.BlockSpec((epc, D_MODEL, 2 * D_EXPERT), lambda c: (c, 0, 0)),
        pl.BlockSpec((epc, D_EXPERT, D_MODEL), lambda c: (c, 0, 0)),
    ]
    return pl.pallas_call(
        _mixer_kernel,
        grid=(n_chunks,),
        in_specs=in_specs,
        out_specs=out_specs,
        out_shape=[
            jax.ShapeDtypeStruct((N_TOKENS * LANE_TILES, LANES), jnp.float32),
            jax.ShapeDtypeStruct((ROUTER_ROWS, N_TOKENS), jnp.float32),
            jax.ShapeDtypeStruct((N_EXPERTS, D_MODEL, 2 * D_EXPERT), jnp.bfloat16),
            jax.ShapeDtypeStruct((N_EXPERTS, D_EXPERT, D_MODEL), jnp.bfloat16),
        ],
        scratch_shapes=[
            pltpu.VMEM((2, TS, BATCH, D_MODEL), jnp.float32),
            pltpu.SemaphoreType.DMA((2,)),
            pltpu.VMEM((M1 + (RG_CONV - 1) * BATCH, D_RG), jnp.float32),
            pltpu.VMEM((M1 + (SC_CONV - 1) * BATCH, D_SC), jnp.float32),
            pltpu.VMEM((M1, D_RG), jnp.float32),
            pltpu.VMEM((M1, D_RG), jnp.float32),
            pltpu.VMEM((BATCH, D_RG), jnp.float32),
        ],
        compiler_params=pltpu.CompilerParams(
            dimension_semantics=("arbitrary",), vmem_limit_bytes=VMEM_LIMIT),
        name="mixer_ln_router",
    )(x, w_in, cw, cb, wa, ba, wx, bx, lam, scw, w_out, g1, b1, wr, br, wg, wu, wd)


def _expert_kernel(blk_ref, grp_ref, ea_ref, eb_ref, rows_ref,
                   xs_ref, wt_ref, wgu_ref, wd_ref, g2_ref, b2_ref, out_ref):
    s = pl.program_id(0)
    tile_rows = [rows_ref[s * SUB + j] for j in range(SUB)]

    @pl.when(blk_ref[s] == s)
    def _():
        for j in range(SUB):
            tile = s * SUB + j
            first_row = j * TM
            in_use = lax.broadcasted_iota(jnp.int32, (TM, 1), 0) < tile_rows[j]
            x = jnp.where(in_use, _load_row_tiles(xs_ref, first_row, TM), 0.0)
            xb = x.astype(jnp.bfloat16)
            wt = jnp.where(in_use, wt_ref[first_row:first_row + TM, :], 0.0)
            y = jnp.zeros((TM, D_MODEL), jnp.float32)
            for e_ref, col in ((ea_ref, 0), (eb_ref, 1)):
                e = e_ref[tile]
                hgu = jnp.dot(xb, wgu_ref[e], preferred_element_type=jnp.float32)
                hid = jax.nn.silu(hgu[:, :D_EXPERT]) * hgu[:, D_EXPERT:] * wt[:, col:col + 1]
                y = y + jnp.dot(hid.astype(jnp.bfloat16), wd_ref[e],
                                preferred_element_type=jnp.float32)
            _store_row_tiles(out_ref, first_row,
                             _layer_norm(DEEPNORM_ALPHA * x + y, g2_ref[...], b2_ref[...]))


def _expert_call(step_block, step_group, tile_ea, tile_eb, tile_rows, xs, wts, wgu, wd, g2, b2):
    rows = SUB * TM
    grid_spec = pltpu.PrefetchScalarGridSpec(
        num_scalar_prefetch=5,
        grid=(N_STEPS3,),
        in_specs=[
            pl.BlockSpec((rows * LANE_TILES, LANES), lambda s, blk, grp, ea, eb, va: (blk[s], 0)),
            pl.BlockSpec((rows, LANES), lambda s, blk, grp, ea, eb, va: (blk[s], 0)),
            pl.BlockSpec((None, EXPERTS_PER_GROUP, D_MODEL, 2 * D_EXPERT),
                         lambda s, blk, grp, ea, eb, va: (grp[s], 0, 0, 0)),
            pl.BlockSpec((None, EXPERTS_PER_GROUP, D_EXPERT, D_MODEL),
                         lambda s, blk, grp, ea, eb, va: (grp[s], 0, 0, 0)),
            pl.BlockSpec((1, D_MODEL), lambda s, blk, grp, ea, eb, va: (0, 0)),
            pl.BlockSpec((1, D_MODEL), lambda s, blk, grp, ea, eb, va: (0, 0)),
        ],
        out_specs=pl.BlockSpec((rows * LANE_TILES, LANES),
                               lambda s, blk, grp, ea, eb, va: (blk[s], 0)),
    )
    return pl.pallas_call(
        _expert_kernel,
        grid_spec=grid_spec,
        out_shape=jax.ShapeDtypeStruct((P_ROWS * LANE_TILES, LANES), jnp.float32),
        compiler_params=pltpu.CompilerParams(
            dimension_semantics=("arbitrary",), vmem_limit_bytes=VMEM_LIMIT),
        name="experts_ln",
    )(step_block, step_group, tile_ea, tile_eb, tile_rows, xs, wts, wgu, wd, g2, b2)


SC_CORES = 2
SC_SUBCORES = 16
SC_WORKERS = SC_CORES * SC_SUBCORES
COMBINE_CHUNK = 32


def _sc_gather_rows(table, idx, chunk, out_row_shape=None):
    n_out, = idx.shape
    row_shape = table.shape[1:]
    out_row_shape = row_shape if out_row_shape is None else out_row_shape
    per_w = n_out // SC_WORKERS
    n_chunks = per_w // chunk
    assert per_w * SC_WORKERS == n_out and n_chunks * chunk == per_w
    assert n_chunks % 2 == 0 and chunk % 8 == 0
    mesh = plsc.VectorSubcoreMesh(core_axis_name="c", subcore_axis_name="s")

    @functools.partial(
        pl.kernel, mesh=mesh,
        out_type=jax.ShapeDtypeStruct((n_out,) + out_row_shape, table.dtype),
        scratch_types=[
            pltpu.VMEM((per_w,), jnp.int32),
            pltpu.VMEM((2, chunk) + row_shape, table.dtype),
            pltpu.SemaphoreType.DMA((2,)),
            pltpu.SemaphoreType.DMA((2,)),
        ],
    )
    def gather_kernel(table_hbm, idx_hbm, out_hbm, idx_v, buf, gsem, wsem):
        wid = lax.axis_index("s") * SC_CORES + lax.axis_index("c")
        base = wid * per_w
        pltpu.sync_copy(idx_hbm.at[pl.ds(base, per_w)], idx_v)

        def gather(j, slot):
            rows = idx_v.at[pl.ds(j * chunk, chunk)]
            return pltpu.make_async_copy(table_hbm.at[rows], buf.at[slot], gsem.at[slot])

        def write(j, slot):
            dst = out_hbm.at[pl.ds(base + j * chunk, chunk)]
            src = buf.at[slot].reshape((chunk,) + out_row_shape)
            return pltpu.make_async_copy(src, dst, wsem.at[slot])

        gather(0, 0).start()

        @pl.loop(0, n_chunks, step=2)
        def _(j0):
            for slot in range(2):
                j = j0 + slot
                gather(j, slot).wait()

                @pl.when(j >= 1)
                def _():
                    write(j - 1, 1 - slot).wait()

                @pl.when(j + 1 < n_chunks)
                def _():
                    gather(j + 1, 1 - slot).start()

                write(j, slot).start()

        write(n_chunks - 1, (n_chunks - 1) % 2).wait()

    return gather_kernel(table, idx)


SC_LANES = 16


def _sc_dispatch(rows, meta, cls, rank, class_start):
    n_rows = rows.shape[0]
    chunk = COMBINE_CHUNK
    per_w = n_rows // SC_WORKERS
    n_chunks = per_w // chunk
    assert per_w * SC_WORKERS == n_rows and n_chunks * chunk == per_w
    assert n_chunks % 2 == 0 and chunk % SC_LANES == 0
    mesh = plsc.VectorSubcoreMesh(core_axis_name="c", subcore_axis_name="s")

    @functools.partial(
        pl.kernel, mesh=mesh,
        out_type=[
            jax.ShapeDtypeStruct((P_ROWS,) + rows.shape[1:], rows.dtype),
            jax.ShapeDtypeStruct((P_ROWS,) + meta.shape[1:], meta.dtype),
            jax.ShapeDtypeStruct((n_rows,), jnp.int32),
        ],
        scratch_types=[
            pltpu.VMEM((per_w,), jnp.int32),
            pltpu.VMEM((per_w,), jnp.int32),
            pltpu.VMEM((LANES,), jnp.int32),
            pltpu.VMEM((per_w,), jnp.int32),
            pltpu.VMEM((n_chunks, chunk), jnp.int32),
            pltpu.VMEM((2, chunk) + rows.shape[1:], rows.dtype),
            pltpu.VMEM((2, chunk) + meta.shape[1:], meta.dtype),
            pltpu.SemaphoreType.DMA((2,)),
            pltpu.SemaphoreType.DMA((2,)),
            pltpu.SemaphoreType.DMA((2,)),
            pltpu.SemaphoreType.DMA((2,)),
        ],
        compiler_params=pltpu.CompilerParams(needs_layout_passes=False),
    )
    def dispatch_kernel(rows_hbm, meta_hbm, cls_hbm, rank_hbm, start_hbm,
                        xs_hbm, ws_hbm, pos_hbm,
                        cls_v, rank_v, start_v, pos_flat, pos_v, rbuf, mbuf,
                        rsem, msem, xsem, wsem):
        wid = lax.axis_index("s") * SC_CORES + lax.axis_index("c")
        base = wid * per_w
        pltpu.sync_copy(cls_hbm.at[pl.ds(base, per_w)], cls_v)
        pltpu.sync_copy(rank_hbm.at[pl.ds(base, per_w)], rank_v)
        pltpu.sync_copy(start_hbm, start_v)

        @pl.loop(0, n_chunks)
        def _(j):
            for k in range(chunk // SC_LANES):
                off = j * chunk + k * SC_LANES
                c = cls_v[pl.ds(off, SC_LANES)]
                pos = plsc.load_gather(start_v, [c]) + rank_v[pl.ds(off, SC_LANES)]
                pos_flat[pl.ds(off, SC_LANES)] = pos
                pos_v[j, pl.ds(k * SC_LANES, SC_LANES)] = pos

        pltpu.sync_copy(pos_flat, pos_hbm.at[pl.ds(base, per_w)])

        def read_rows(j, slot):
            src = rows_hbm.at[pl.ds(base + j * chunk, chunk)]
            return pltpu.make_async_copy(src, rbuf.at[slot], rsem.at[slot])

        def read_meta(j, slot):
            src = meta_hbm.at[pl.ds(base + j * chunk, chunk)]
            return pltpu.make_async_copy(src, mbuf.at[slot], msem.at[slot])

        def put_rows(j, slot):
            return pltpu.make_async_copy(rbuf.at[slot], xs_hbm.at[pos_v.at[j]], xsem.at[slot])

        def put_meta(j, slot):
            return pltpu.make_async_copy(mbuf.at[slot], ws_hbm.at[pos_v.at[j]], wsem.at[slot])

        read_rows(0, 0).start()
        read_meta(0, 0).start()

        @pl.loop(0, n_chunks, step=2)
        def _(j0):
            for slot in range(2):
                j = j0 + slot
                read_rows(j, slot).wait()
                read_meta(j, slot).wait()

                @pl.when(j >= 1)
                def _():
                    put_rows(j - 1, 1 - slot).wait()
                    put_meta(j - 1, 1 - slot).wait()

                @pl.when(j + 1 < n_chunks)
                def _():
                    read_rows(j + 1, 1 - slot).start()
                    read_meta(j + 1, 1 - slot).start()

                put_rows(j, slot).start()
                put_meta(j, slot).start()

        put_rows(n_chunks - 1, (n_chunks - 1) % 2).wait()
        put_meta(n_chunks - 1, (n_chunks - 1) % 2).wait()

    return dispatch_kernel(rows, meta, cls, rank, class_start)


def _block_diag(w):
    h, d, _ = w.shape
    eye = jnp.eye(h, dtype=w.dtype)
    return (eye[:, None, :, None] * w[:, :, None, :]).reshape(h * d, h * d)


def _pair_tables():
    pair_of = np.zeros((EXPERTS_PER_GROUP, EXPERTS_PER_GROUP), np.int32)
    lo = np.zeros((N_PAIRS,), np.int32)
    hi = np.zeros((N_PAIRS,), np.int32)
    p = 0
    for a in range(EXPERTS_PER_GROUP):
        for b in range(a + 1, EXPERTS_PER_GROUP):
            pair_of[a, b] = pair_of[b, a] = p
            lo[p], hi[p] = a, b
            p += 1
    return pair_of, lo, hi


def _route_kernel(lt_ref, info_ref, meta_ref, counts_ref, run_ref):
    step = pl.program_id(0)

    @pl.when(step == 0)
    def _():
        run_ref[...] = jnp.zeros_like(run_ref)

    f32 = jnp.float32
    sub8 = lax.broadcasted_iota(jnp.int32, (8, LANES), 0).astype(f32)
    row_id = lax.broadcasted_iota(jnp.int32, (LANES, LANES), 0)
    col_id = lax.broadcasted_iota(jnp.int32, (LANES, LANES), 1)
    prefix_mat = (row_id <= col_id).astype(jnp.bfloat16)
    ones_mat = jnp.ones((LANES, LANES), jnp.bfloat16)
    neg_inf = f32(-jnp.inf)

    def first_index_of_max(v):
        m = jnp.max(v, axis=0, keepdims=True)
        idx = jnp.min(jnp.where(v == m, sub8, f32(8)), axis=0, keepdims=True)
        return m, idx

    def lane_tile(k, run):
        lanes = pl.ds(pl.multiple_of(k * LANES, LANES), LANES)
        g = jnp.where(sub8 < N_GROUPS, lt_ref[0:GROUP_ROWS, lanes], neg_inf)
        g_max, g_idx = first_index_of_max(g)
        g_top_p = 1.0 / jnp.sum(jnp.exp(g - g_max), axis=0, keepdims=True)

        e_sel = lt_ref[GROUP_ROWS:GROUP_ROWS + EXPERTS_PER_GROUP, lanes]
        for grp in range(1, N_GROUPS):
            lo = GROUP_ROWS + grp * EXPERTS_PER_GROUP
            e_sel = jnp.where(g_idx == grp, lt_ref[lo:lo + EXPERTS_PER_GROUP, lanes], e_sel)
        m1, i1 = first_index_of_max(e_sel)
        rest = jnp.where(sub8 == i1, neg_inf, e_sel)
        m2 = jnp.max(rest, axis=0, keepdims=True)
        i2 = jnp.min(jnp.where((rest == m2) & (sub8 != i1), sub8, f32(8)), axis=0, keepdims=True)

        e = jnp.exp(m2 - m1)
        w1 = g_top_p / (1.0 + e)
        w2 = g_top_p * e / (1.0 + e)
        first_is_lo = i1 < i2
        w_lo = jnp.where(first_is_lo, w1, w2)
        w_hi = jnp.where(first_is_lo, w2, w1)
        lo_e = jnp.minimum(i1, i2)
        hi_e = jnp.maximum(i1, i2)
        pair = lo_e * (2 * EXPERTS_PER_GROUP - 1 - lo_e) * 0.5 + hi_e - lo_e - 1.0
        cls = (g_idx * N_PAIRS + pair).astype(jnp.int32)

        onehot = (row_id == cls).astype(jnp.bfloat16)
        csum = jnp.dot(onehot, prefix_mat, preferred_element_type=f32)
        rank = jnp.sum(onehot.astype(f32) * (csum + run), axis=0, keepdims=True) - 1.0
        run = run + jnp.dot(onehot, ones_mat, preferred_element_type=f32)

        info_ref[:, lanes] = jnp.where(sub8 == 0, cls, jnp.where(sub8 == 1, rank.astype(jnp.int32), 0))
        meta8 = jnp.where(sub8 == 0, w_lo, jnp.where(sub8 == 1, w_hi, 0.0))
        meta_t = jnp.concatenate([meta8, jnp.zeros((LANES - 8, LANES), f32)], axis=0)
        meta_ref[lanes, :] = meta_t.T
        return run

    run = lax.fori_loop(0, ROUTE_BLOCK // LANES, lane_tile, run_ref[...], unroll=ROUTE_UNROLL)
    run_ref[...] = run
    counts_ref[...] = run


def _route_call(lt):
    return pl.pallas_call(
        _route_kernel,
        grid=(N_TOKENS // ROUTE_BLOCK,),
        in_specs=[pl.BlockSpec((ROUTER_ROWS, ROUTE_BLOCK), lambda i: (0, i))],
        out_specs=[
            pl.BlockSpec((8, ROUTE_BLOCK), lambda i: (0, i)),
            pl.BlockSpec((ROUTE_BLOCK, LANES), lambda i: (i, 0)),
            pl.BlockSpec((LANES, LANES), lambda i: (0, 0)),
        ],
        out_shape=[
            jax.ShapeDtypeStruct((8, N_TOKENS), jnp.int32),
            jax.ShapeDtypeStruct((N_TOKENS, LANES), jnp.float32),
            jax.ShapeDtypeStruct((LANES, LANES), jnp.float32),
        ],
        scratch_shapes=[pltpu.VMEM((LANES, LANES), jnp.float32)],
        compiler_params=pltpu.CompilerParams(dimension_semantics=("arbitrary",)),
        name="route_rank",
    )(lt)


def _dispatch_plan(counts):
    _, pair_lo, pair_hi = _pair_tables()
    tiles_c = (counts + TM - 1) // TM
    tiles_g = tiles_c.reshape(N_GROUPS, N_PAIRS).sum(axis=1)
    tiles_g_pad = (tiles_g + SUB - 1) // SUB * SUB
    g_start = jnp.cumsum(tiles_g_pad) - tiles_g_pad
    tc = tiles_c.reshape(N_GROUPS, N_PAIRS)
    c_start = (g_start[:, None] + jnp.cumsum(tc, axis=1) - tc).reshape(N_CLASSES)
    class_start = jnp.zeros((LANES,), jnp.int32).at[:N_CLASSES].set(c_start * TM)

    tile_ids = jnp.arange(N_TILES, dtype=jnp.int32)
    c_end = c_start + tiles_c
    owner = jnp.sum((tile_ids[:, None] >= c_end[None, :]).astype(jnp.int32), axis=1)
    owner = jnp.minimum(owner, N_CLASSES - 1)
    in_class = (tile_ids >= c_start[owner]) & (tile_ids < c_end[owner])
    rows_left = counts[owner] - (tile_ids - c_start[owner]) * TM
    tile_rows = jnp.where(in_class, jnp.minimum(rows_left, TM), 0).astype(jnp.int32)
    pair = owner % N_PAIRS
    tile_ea = jnp.asarray(pair_lo)[pair]
    tile_eb = jnp.asarray(pair_hi)[pair]
    g_end = g_start + tiles_g_pad
    step_first = jnp.arange(N_STEPS3, dtype=jnp.int32) * SUB
    step_group = jnp.sum((step_first[:, None] >= g_end[None, :]).astype(jnp.int32), axis=1)
    step_group = jnp.minimum(step_group, N_GROUPS - 1)
    last_step = jnp.maximum(g_end[-1] // SUB - 1, 0)
    step_block = jnp.minimum(jnp.arange(N_STEPS3, dtype=jnp.int32), last_step)
    return (class_start, tile_ea, tile_eb, tile_rows, step_group.astype(jnp.int32),
            step_block.astype(jnp.int32))


def kernel(x, w_in, rg_conv_w, rg_conv_b, rg_gate_a_w, rg_gate_a_b, rg_gate_x_w, rg_gate_x_b,
           rg_lambda, sc_conv_w, w_out, ln1_g, ln1_b, router_group_w, router_group_b,
           router_expert_w, router_expert_b, exp_w_gate, exp_w_up, exp_w_down, ln2_g, ln2_b):
    bf16 = jnp.bfloat16
    row = lambda v: v.reshape(1, -1)
    pad_g = GROUP_ROWS - N_GROUPS
    pad_e = ROUTER_LANES - ROUTER_ROWS
    wr = jnp.concatenate([router_group_w, jnp.zeros((D_MODEL, pad_g), jnp.float32),
                          router_expert_w, jnp.zeros((D_MODEL, pad_e), jnp.float32)], axis=1)
    br = jnp.concatenate([router_group_b, jnp.zeros((pad_g,), jnp.float32),
                          router_expert_b, jnp.zeros((pad_e,), jnp.float32)]).reshape(1, -1)
    wr_hi = wr.astype(bf16)
    wr_lo = (wr - wr_hi.astype(jnp.float32)).astype(bf16)

    h1, lt, wgu, wdb = _mixer_call(
        x, w_in.astype(bf16), rg_conv_w, row(rg_conv_b),
        _block_diag(rg_gate_a_w).astype(bf16), row(rg_gate_a_b),
        _block_diag(rg_gate_x_w).astype(bf16), row(rg_gate_x_b),
        row(rg_lambda), sc_conv_w, w_out.astype(bf16), row(ln1_g), row(ln1_b),
        jnp.concatenate([wr_hi, wr_lo], axis=1), br, exp_w_gate, exp_w_up, exp_w_down)

    info, meta, counts = _route_call(lt)
    class_start, tile_ea, tile_eb, tile_rows, step_group, step_block = _dispatch_plan(
        counts[:N_CLASSES, 0].astype(jnp.int32))

    row_tiles = (-1, LANE_TILES, LANES)
    xs, wts, pos = _sc_dispatch(h1.reshape(row_tiles), meta, info[0], info[1], class_start)

    grp_shape = (N_GROUPS, EXPERTS_PER_GROUP)
    ys = _expert_call(
        step_block, step_group, tile_ea, tile_eb, tile_rows, xs.reshape(-1, LANES), wts,
        wgu.reshape(grp_shape + (D_MODEL, 2 * D_EXPERT)),
        wdb.reshape(grp_shape + (D_EXPERT, D_MODEL)),
        row(ln2_g), row(ln2_b))

    out_rows = pos.reshape(SEQ, BATCH).T.reshape(-1)
    out = _sc_gather_rows(ys.reshape(row_tiles), out_rows, COMBINE_CHUNK, (D_MODEL,))
    return out.reshape(BATCH, SEQ, D_MODEL)
```

```python
import functools

import jax
import jax.numpy as jnp
import numpy as np
from jax import lax
from jax.experimental import pallas as pl
from jax.experimental.pallas import tpu as pltpu
from jax.experimental.pallas import tpu_sc as plsc

D_MODEL = 1024
BATCH = 16
SEQ = 2048
D_RG = 512
D_SC = 512
RG_HEADS = 8
RG_HEAD_DIM = D_RG // RG_HEADS
RG_CONV = 4
RG_C = 8.0
SC_CONV = 3
D_IN_PROJ = 2 * D_RG + 3 * D_SC
N_GROUPS = 4
EXPERTS_PER_GROUP = 8
N_EXPERTS = N_GROUPS * EXPERTS_PER_GROUP
D_EXPERT = D_MODEL // 4
LN_EPS = 1e-5
DEEPNORM_ALPHA = 2.0 ** 0.25

N_TOKENS = BATCH * SEQ
LANES = 128
LANE_TILES = D_MODEL // LANES
MXU_WIDTH = 256
ROUTER_LANES = LANES
GROUP_ROWS = 8
ROUTER_ROWS = GROUP_ROWS + N_EXPERTS

TS = 64
M1 = TS * BATCH
DOT_ROWS = 256

ROUTE_BLOCK = 4096
ROUTE_UNROLL = 4

N_PAIRS = EXPERTS_PER_GROUP * (EXPERTS_PER_GROUP - 1) // 2
N_CLASSES = N_GROUPS * N_PAIRS
BF16_ROWS = 16
N_PARTS = 2
PART_TOKENS = N_TOKENS // N_PARTS
TM = -(-int(PART_TOKENS / N_CLASSES * 1.09) // BF16_ROWS) * BF16_ROWS
SUB = 8
N_TILES = -(-(PART_TOKENS // TM + N_CLASSES + N_GROUPS * (SUB - 1)) // SUB) * SUB
N_STEPS3 = N_TILES // SUB
P_ROWS = N_TILES * TM
ROUTE_STEPS_PER_PART = PART_TOKENS // ROUTE_BLOCK

VMEM_LIMIT = 56 * 1024 * 1024


def _layer_norm(z, g, b):
    mu = jnp.mean(z, axis=-1, keepdims=True)
    zc = z - mu
    var = jnp.mean(zc * zc, axis=-1, keepdims=True)
    return zc * lax.rsqrt(var + LN_EPS) * g + b


def _store_row_tiles(ref, first_row, value):
    n = value.shape[0]
    for j in range(LANE_TILES):
        ref[pl.ds(first_row * LANE_TILES + j, n, stride=LANE_TILES), :] = (
            value[:, j * LANES:(j + 1) * LANES])


def _load_row_tiles(ref, first_row, n):
    return jnp.concatenate(
        [ref[pl.ds(first_row * LANE_TILES + j, n, stride=LANE_TILES), :]
         for j in range(LANE_TILES)], axis=1)


def _mixer_kernel(x_ref, w_in_ref, cw_ref, cb_ref, wa_ref, ba_ref, wx_ref, bx_ref, lam_ref,
                  scw_ref, w_out_ref, g1_ref, b1_ref, wr_ref, br_ref, wg_ref, wu_ref, wd_ref,
                  h1_ref, logit_ref, wgu_ref, wdb_ref,
                  xbuf_ref, xsem, cbuf_ref, sbuf_ref, a_ref, u_ref, hstate_ref):
    c = pl.program_id(0)

    wgu_ref[:, :, :D_EXPERT] = wg_ref[...].astype(jnp.bfloat16)
    wgu_ref[:, :, D_EXPERT:] = wu_ref[...].astype(jnp.bfloat16)
    wdb_ref[...] = wd_ref[...].astype(jnp.bfloat16)
    slot = c % 2
    rg_pad = (RG_CONV - 1) * BATCH
    sc_pad = (SC_CONV - 1) * BATCH

    def x_copies(chunk, slot_):
        return [pltpu.make_async_copy(x_ref.at[b, pl.ds(chunk * TS, TS), :],
                                      xbuf_ref.at[slot_, :, b, :], xsem.at[slot_])
                for b in range(BATCH)]

    @pl.when(c == 0)
    def _():
        for cp in x_copies(0, 0):
            cp.start()
        cbuf_ref[0:rg_pad, :] = jnp.zeros((rg_pad, D_RG), jnp.float32)
        sbuf_ref[0:sc_pad, :] = jnp.zeros((sc_pad, D_SC), jnp.float32)
        hstate_ref[...] = jnp.zeros_like(hstate_ref)

    @pl.when(c + 1 < SEQ // TS)
    def _():
        for cp in x_copies(c + 1, 1 - slot):
            cp.start()

    for cp in x_copies(c, slot):
        cp.wait()

    xt_ref = xbuf_ref.at[slot]
    def gate(xcb, w_ref, b_ref):
        parts = [jnp.dot(xcb[:, lo:lo + MXU_WIDTH], w_ref[lo:lo + MXU_WIDTH, lo:lo + MXU_WIDTH],
                         preferred_element_type=jnp.float32)
                 for lo in range(0, D_RG, MXU_WIDTH)]
        return 0.5 + 0.5 * jnp.tanh(0.5 * (jnp.concatenate(parts, axis=1) + b_ref[...]))

    soft_lam = jax.nn.softplus(-lam_ref[...])
    gelu_chunks, y_sc_chunks = [], []
    for r0 in range(0, M1, DOT_ROWS):
        t0 = r0 // BATCH
        xk = xt_ref[t0:t0 + DOT_ROWS // BATCH].reshape(DOT_ROWS, D_MODEL).astype(jnp.bfloat16)

        def proj(lo, width):
            return jnp.dot(xk, w_in_ref[:, lo:lo + width], preferred_element_type=jnp.float32)

        cbuf_ref[rg_pad + r0:rg_pad + r0 + DOT_ROWS, :] = proj(0, D_RG)
        rg_gate = proj(D_RG, D_RG)
        xc = cb_ref[...] + cw_ref[0:1, :] * cbuf_ref[r0:r0 + DOT_ROWS, :]
        for k in range(1, RG_CONV):
            xc = xc + cw_ref[k:k + 1, :] * cbuf_ref[r0 + k * BATCH:r0 + k * BATCH + DOT_ROWS, :]
        sc_b = proj(2 * D_RG, D_SC)
        xcb = xc.astype(jnp.bfloat16)
        r = gate(xcb, wa_ref, ba_ref)
        i = gate(xcb, wx_ref, bx_ref)
        sc_c = proj(2 * D_RG + D_SC, D_SC)
        log_a = (-RG_C) * r * soft_lam
        a_ref[r0:r0 + DOT_ROWS, :] = jnp.exp(log_a)
        th = jnp.tanh(log_a)
        u_ref[r0:r0 + DOT_ROWS, :] = jnp.sqrt(-2.0 * th / (1.0 - th)) * (i * xc)
        gelu_chunks.append(jax.nn.gelu(rg_gate))

        sbuf_ref[sc_pad + r0:sc_pad + r0 + DOT_ROWS, :] = sc_c * proj(2 * D_RG + 2 * D_SC, D_SC)
        conv = scw_ref[0:1, :] * sbuf_ref[r0:r0 + DOT_ROWS, :]
        for k in range(1, SC_CONV):
            conv = conv + scw_ref[k:k + 1, :] * sbuf_ref[r0 + k * BATCH:r0 + k * BATCH + DOT_ROWS, :]
        y_sc_chunks.append((sc_b * conv).astype(jnp.bfloat16))

    cbuf_ref[0:rg_pad, :] = cbuf_ref[M1:M1 + rg_pad, :]
    sbuf_ref[0:sc_pad, :] = sbuf_ref[M1:M1 + sc_pad, :]

    h = hstate_ref[...]
    for t in range(TS):
        h = a_ref[t * BATCH:(t + 1) * BATCH, :] * h + u_ref[t * BATCH:(t + 1) * BATCH, :]
        u_ref[t * BATCH:(t + 1) * BATCH, :] = h
    hstate_ref[...] = h

    tail_rows = M1 // 2
    per_tail = tail_rows // DOT_ROWS

    def out_proj(k):
        r0 = k * tail_rows
        gelu_k = jnp.concatenate(gelu_chunks[k * per_tail:(k + 1) * per_tail], axis=0)
        y_sc_k = jnp.concatenate(y_sc_chunks[k * per_tail:(k + 1) * per_tail], axis=0)
        y_rg = (u_ref[r0:r0 + tail_rows, :] * gelu_k).astype(jnp.bfloat16)
        mix = jnp.dot(y_rg, w_out_ref[0:D_RG, :], preferred_element_type=jnp.float32)
        return mix + jnp.dot(y_sc_k, w_out_ref[D_RG:, :], preferred_element_type=jnp.float32)

    def finish(k, mix):
        r0 = k * tail_rows
        t0 = r0 // BATCH
        x_rows = xt_ref[t0:t0 + tail_rows // BATCH].reshape(tail_rows, D_MODEL)
        h1 = _layer_norm(DEEPNORM_ALPHA * x_rows + mix, g1_ref[...], b1_ref[...])
        _store_row_tiles(h1_ref, r0, h1)
        h_hi = h1.astype(jnp.bfloat16)
        h_lo = (h1 - h_hi.astype(jnp.float32)).astype(jnp.bfloat16)
        both = jnp.dot(h_hi, wr_ref[...], preferred_element_type=jnp.float32)
        logits = both[:, :ROUTER_LANES] + both[:, ROUTER_LANES:] + br_ref[...]
        logits = logits + jnp.dot(h_lo, wr_ref[:, :ROUTER_LANES],
                                  preferred_element_type=jnp.float32)
        logit_ref[:, r0:r0 + tail_rows] = logits.T[:ROUTER_ROWS, :]

    n_tails = M1 // tail_rows
    mix = out_proj(0)
    for k in range(n_tails):
        next_mix = out_proj(k + 1) if k + 1 < n_tails else None
        finish(k, mix)
        mix = next_mix


def _const_spec(shape):
    return pl.BlockSpec(shape, lambda c: (0,) * len(shape))


def _mixer_call(x, w_in, cw, cb, wa, ba, wx, bx, lam, scw, w_out, g1, b1, wr, br, wg, wu, wd):
    n_chunks = SEQ // TS
    epc = N_EXPERTS // n_chunks
    assert epc * n_chunks == N_EXPERTS
    in_specs = [
        pl.BlockSpec(memory_space=pl.ANY),
        _const_spec((D_MODEL, D_IN_PROJ)),
        _const_spec((RG_CONV, D_RG)), _const_spec((1, D_RG)),
        _const_spec((D_RG, D_RG)), _const_spec((1, D_RG)),
        _const_spec((D_RG, D_RG)), _const_spec((1, D_RG)),
        _const_spec((1, D_RG)),
        _const_spec((SC_CONV, D_SC)),
        _const_spec((D_MODEL, D_MODEL)),
        _const_spec((1, D_MODEL)), _const_spec((1, D_MODEL)),
        _const_spec((D_MODEL, 2 * ROUTER_LANES)), _const_spec((1, ROUTER_LANES)),
        pl.BlockSpec((epc, D_MODEL, D_EXPERT), lambda c: (c, 0, 0)),
        pl.BlockSpec((epc, D_MODEL, D_EXPERT), lambda c: (c, 0, 0)),
        pl.BlockSpec((epc, D_EXPERT, D_MODEL), lambda c: (c, 0, 0)),
    ]
    out_specs = [
        pl.BlockSpec((M1 * LANE_TILES, LANES), lambda c: (c, 0)),
        pl.BlockSpec((ROUTER_ROWS, M1), lambda c: (0, c)),
        pl.BlockSpec((epc, D_MODEL, 2 * D_EXPERT), lambda c: (c, 0, 0)),
        pl.BlockSpec((epc, D_EXPERT, D_MODEL), lambda c: (c, 0, 0)),
    ]
    return pl.pallas_call(
        _mixer_kernel,
        grid=(n_chunks,),
        in_specs=in_specs,
        out_specs=out_specs,
        out_shape=[
            jax.ShapeDtypeStruct((N_TOKENS * LANE_TILES, LANES), jnp.float32),
            jax.ShapeDtypeStruct((ROUTER_ROWS, N_TOKENS), jnp.float32),
            jax.ShapeDtypeStruct((N_EXPERTS, D_MODEL, 2 * D_EXPERT), jnp.bfloat16),
            jax.ShapeDtypeStruct((N_EXPERTS, D_EXPERT, D_MODEL), jnp.bfloat16),
        ],
        scratch_shapes=[
            pltpu.VMEM((2, TS, BATCH, D_MODEL), jnp.float32),
            pltpu.SemaphoreType.DMA((2,)),
            pltpu.VMEM((M1 + (RG_CONV - 1) * BATCH, D_RG), jnp.float32),
            pltpu.VMEM((M1 + (SC_CONV - 1) * BATCH, D_SC), jnp.float32),
            pltpu.VMEM((M1, D_RG), jnp.float32),
            pltpu.VMEM((M1, D_RG), jnp.float32),
            pltpu.VMEM((BATCH, D_RG), jnp.float32),
        ],
        compiler_params=pltpu.CompilerParams(
            dimension_semantics=("arbitrary",), vmem_limit_bytes=VMEM_LIMIT),
        name="mixer_ln_router",
    )(x, w_in, cw, cb, wa, ba, wx, bx, lam, scw, w_out, g1, b1, wr, br, wg, wu, wd)


def _expert_kernel(blk_ref, grp_ref, ea_ref, eb_ref, rows_ref,
                   xs_ref, wt_ref, wgu_ref, wd_ref, g2_ref, b2_ref, out_ref):
    s = pl.program_id(0)
    tile_rows = [rows_ref[s * SUB + j] for j in range(SUB)]

    @pl.when(blk_ref[s] == s)
    def _():
        for j in range(SUB):
            tile = s * SUB + j
            first_row = j * TM
            in_use = lax.broadcasted_iota(jnp.int32, (TM, 1), 0) < tile_rows[j]
            x = jnp.where(in_use, _load_row_tiles(xs_ref, first_row, TM), 0.0)
            xb = x.astype(jnp.bfloat16)
            wt = jnp.where(in_use, wt_ref[first_row:first_row + TM, :], 0.0)
            y = jnp.zeros((TM, D_MODEL), jnp.float32)
            for e_ref, col in ((ea_ref, 0), (eb_ref, 1)):
                e = e_ref[tile]
                hgu = jnp.dot(xb, wgu_ref[e], preferred_element_type=jnp.float32)
                hid = jax.nn.silu(hgu[:, :D_EXPERT]) * hgu[:, D_EXPERT:] * wt[:, col:col + 1]
                y = y + jnp.dot(hid.astype(jnp.bfloat16), wd_ref[e],
                                preferred_element_type=jnp.float32)
            _store_row_tiles(out_ref, first_row,
                             _layer_norm(DEEPNORM_ALPHA * x + y, g2_ref[...], b2_ref[...]))


def _expert_call(step_block, step_group, tile_ea, tile_eb, tile_rows, xs, wts, wgu, wd, g2, b2):
    rows = SUB * TM
    grid_spec = pltpu.PrefetchScalarGridSpec(
        num_scalar_prefetch=5,
        grid=(N_STEPS3,),
        in_specs=[
            pl.BlockSpec((rows * LANE_TILES, LANES), lambda s, blk, grp, ea, eb, va: (blk[s], 0)),
            pl.BlockSpec((rows, LANES), lambda s, blk, grp, ea, eb, va: (blk[s], 0)),
            pl.BlockSpec((None, EXPERTS_PER_GROUP, D_MODEL, 2 * D_EXPERT),
                         lambda s, blk, grp, ea, eb, va: (grp[s], 0, 0, 0)),
            pl.BlockSpec((None, EXPERTS_PER_GROUP, D_EXPERT, D_MODEL),
                         lambda s, blk, grp, ea, eb, va: (grp[s], 0, 0, 0)),
            pl.BlockSpec((1, D_MODEL), lambda s, blk, grp, ea, eb, va: (0, 0)),
            pl.BlockSpec((1, D_MODEL), lambda s, blk, grp, ea, eb, va: (0, 0)),
        ],
        out_specs=pl.BlockSpec((rows * LANE_TILES, LANES),
                               lambda s, blk, grp, ea, eb, va: (blk[s], 0)),
    )
    return pl.pallas_call(
        _expert_kernel,
        grid_spec=grid_spec,
        out_shape=jax.ShapeDtypeStruct((P_ROWS * LANE_TILES, LANES), jnp.float32),
        compiler_params=pltpu.CompilerParams(
            dimension_semantics=("arbitrary",), vmem_limit_bytes=VMEM_LIMIT),
        name="experts_ln",
    )(step_block, step_group, tile_ea, tile_eb, tile_rows, xs, wts, wgu, wd, g2, b2)


SC_CORES = 2
SC_SUBCORES = 16
SC_WORKERS = SC_CORES * SC_SUBCORES
COMBINE_CHUNK = 32


def _sc_unpermute(tables, idx, chunk, out_row_shape):
    n_out, = idx.shape
    row_shape = tables[0].shape[1:]
    dtype = tables[0].dtype
    per_w = n_out // SC_WORKERS
    n_chunks = per_w // chunk
    part_len = SEQ // N_PARTS
    assert per_w * SC_WORKERS == n_out and n_chunks * chunk == per_w
    assert n_chunks % 2 == 0 and chunk % 8 == 0 and part_len % per_w == 0
    mesh = plsc.VectorSubcoreMesh(core_axis_name="c", subcore_axis_name="s")

    @functools.partial(
        pl.kernel, mesh=mesh,
        out_type=jax.ShapeDtypeStruct((n_out,) + out_row_shape, dtype),
        scratch_types=[
            pltpu.VMEM((per_w,), jnp.int32),
            pltpu.VMEM((2, chunk) + row_shape, dtype),
            pltpu.SemaphoreType.DMA((2,)),
            pltpu.SemaphoreType.DMA((2,)),
        ],
    )
    def gather_kernel(*refs):
        tables_hbm = refs[:N_PARTS]
        idx_hbm, out_hbm, idx_v, buf, gsem, wsem = refs[N_PARTS:]
        wid = lax.axis_index("s") * SC_CORES + lax.axis_index("c")
        base = wid * per_w
        part = (base % SEQ) // part_len
        pltpu.sync_copy(idx_hbm.at[pl.ds(base, per_w)], idx_v)

        def gather(j, slot, action):
            rows = idx_v.at[pl.ds(j * chunk, chunk)]
            for p, table_hbm in enumerate(tables_hbm):
                @pl.when(part == p)
                def _():
                    action(pltpu.make_async_copy(table_hbm.at[rows], buf.at[slot], gsem.at[slot]))

        def write(j, slot):
            dst = out_hbm.at[pl.ds(base + j * chunk, chunk)]
            src = buf.at[slot].reshape((chunk,) + out_row_shape)
            return pltpu.make_async_copy(src, dst, wsem.at[slot])

        start = lambda cp: cp.start()
        wait = lambda cp: cp.wait()
        gather(0, 0, start)

        @pl.loop(0, n_chunks, step=2)
        def _(j0):
            for slot in range(2):
                j = j0 + slot
                gather(j, slot, wait)

                @pl.when(j >= 1)
                def _():
                    write(j - 1, 1 - slot).wait()

                @pl.when(j + 1 < n_chunks)
                def _():
                    gather(j + 1, 1 - slot, start)

                write(j, slot).start()

        write(n_chunks - 1, (n_chunks - 1) % 2).wait()

    return gather_kernel(*tables, idx)


SC_LANES = 16


def _sc_dispatch(rows, meta, cls, rank, class_start, part):
    n_rows = PART_TOKENS
    row_off = part * PART_TOKENS
    chunk = COMBINE_CHUNK
    per_w = n_rows // SC_WORKERS
    n_chunks = per_w // chunk
    assert per_w * SC_WORKERS == n_rows and n_chunks * chunk == per_w
    assert n_chunks % 2 == 0 and chunk % SC_LANES == 0
    mesh = plsc.VectorSubcoreMesh(core_axis_name="c", subcore_axis_name="s")

    @functools.partial(
        pl.kernel, mesh=mesh,
        out_type=[
            jax.ShapeDtypeStruct((P_ROWS,) + rows.shape[1:], rows.dtype),
            jax.ShapeDtypeStruct((P_ROWS,) + meta.shape[1:], meta.dtype),
            jax.ShapeDtypeStruct((n_rows,), jnp.int32),
        ],
        scratch_types=[
            pltpu.VMEM((per_w,), jnp.int32),
            pltpu.VMEM((per_w,), jnp.int32),
            pltpu.VMEM((LANES,), jnp.int32),
            pltpu.VMEM((per_w,), jnp.int32),
            pltpu.VMEM((n_chunks, chunk), jnp.int32),
            pltpu.VMEM((2, chunk) + rows.shape[1:], rows.dtype),
            pltpu.VMEM((2, chunk) + meta.shape[1:], meta.dtype),
            pltpu.SemaphoreType.DMA((2,)),
            pltpu.SemaphoreType.DMA((2,)),
            pltpu.SemaphoreType.DMA((2,)),
            pltpu.SemaphoreType.DMA((2,)),
        ],
        compiler_params=pltpu.CompilerParams(needs_layout_passes=False),
    )
    def dispatch_kernel(rows_hbm, meta_hbm, cls_hbm, rank_hbm, start_hbm,
                        xs_hbm, ws_hbm, pos_hbm,
                        cls_v, rank_v, start_v, pos_flat, pos_v, rbuf, mbuf,
                        rsem, msem, xsem, wsem):
        wid = lax.axis_index("s") * SC_CORES + lax.axis_index("c")
        base = wid * per_w
        src_base = row_off + base
        pltpu.sync_copy(cls_hbm.at[pl.ds(src_base, per_w)], cls_v)
        pltpu.sync_copy(rank_hbm.at[pl.ds(src_base, per_w)], rank_v)
        pltpu.sync_copy(start_hbm, start_v)

        @pl.loop(0, n_chunks)
        def _(j):
            for k in range(chunk // SC_LANES):
                off = j * chunk + k * SC_LANES
                c = cls_v[pl.ds(off, SC_LANES)]
                pos = plsc.load_gather(start_v, [c]) + rank_v[pl.ds(off, SC_LANES)]
                pos_flat[pl.ds(off, SC_LANES)] = pos
                pos_v[j, pl.ds(k * SC_LANES, SC_LANES)] = pos

        pltpu.sync_copy(pos_flat, pos_hbm.at[pl.ds(base, per_w)])

        def read_rows(j, slot):
            src = rows_hbm.at[pl.ds(src_base + j * chunk, chunk)]
            return pltpu.make_async_copy(src, rbuf.at[slot], rsem.at[slot])

        def read_meta(j, slot):
            src = meta_hbm.at[pl.ds(src_base + j * chunk, chunk)]
            return pltpu.make_async_copy(src, mbuf.at[slot], msem.at[slot])

        def put_rows(j, slot):
            return pltpu.make_async_copy(rbuf.at[slot], xs_hbm.at[pos_v.at[j]], xsem.at[slot])

        def put_meta(j, slot):
            return pltpu.make_async_copy(mbuf.at[slot], ws_hbm.at[pos_v.at[j]], wsem.at[slot])

        read_rows(0, 0).start()
        read_meta(0, 0).start()

        @pl.loop(0, n_chunks, step=2)
        def _(j0):
            for slot in range(2):
                j = j0 + slot
                read_rows(j, slot).wait()
                read_meta(j, slot).wait()

                @pl.when(j >= 1)
                def _():
                    put_rows(j - 1, 1 - slot).wait()
                    put_meta(j - 1, 1 - slot).wait()

                @pl.when(j + 1 < n_chunks)
                def _():
                    read_rows(j + 1, 1 - slot).start()
                    read_meta(j + 1, 1 - slot).start()

                put_rows(j, slot).start()
                put_meta(j, slot).start()

        put_rows(n_chunks - 1, (n_chunks - 1) % 2).wait()
        put_meta(n_chunks - 1, (n_chunks - 1) % 2).wait()

    return dispatch_kernel(rows, meta, cls, rank, class_start)


def _block_diag(w):
    h, d, _ = w.shape
    eye = jnp.eye(h, dtype=w.dtype)
    return (eye[:, None, :, None] * w[:, :, None, :]).reshape(h * d, h * d)


def _pair_tables():
    pair_of = np.zeros((EXPERTS_PER_GROUP, EXPERTS_PER_GROUP), np.int32)
    lo = np.zeros((N_PAIRS,), np.int32)
    hi = np.zeros((N_PAIRS,), np.int32)
    p = 0
    for a in range(EXPERTS_PER_GROUP):
        for b in range(a + 1, EXPERTS_PER_GROUP):
            pair_of[a, b] = pair_of[b, a] = p
            lo[p], hi[p] = a, b
            p += 1
    return pair_of, lo, hi


def _route_kernel(lt_ref, info_ref, meta_ref, counts_ref, run_ref):
    step = pl.program_id(0)

    @pl.when(step % ROUTE_STEPS_PER_PART == 0)
    def _():
        run_ref[...] = jnp.zeros_like(run_ref)

    f32 = jnp.float32
    sub8 = lax.broadcasted_iota(jnp.int32, (8, LANES), 0).astype(f32)
    row_id = lax.broadcasted_iota(jnp.int32, (LANES, LANES), 0)
    col_id = lax.broadcasted_iota(jnp.int32, (LANES, LANES), 1)
    prefix_mat = (row_id <= col_id).astype(jnp.bfloat16)
    ones_mat = jnp.ones((LANES, LANES), jnp.bfloat16)
    neg_inf = f32(-jnp.inf)

    def first_index_of_max(v):
        m = jnp.max(v, axis=0, keepdims=True)
        idx = jnp.min(jnp.where(v == m, sub8, f32(8)), axis=0, keepdims=True)
        return m, idx

    def lane_tile(k, run):
        lanes = pl.ds(pl.multiple_of(k * LANES, LANES), LANES)
        g = jnp.where(sub8 < N_GROUPS, lt_ref[0:GROUP_ROWS, lanes], neg_inf)
        g_max, g_idx = first_index_of_max(g)
        g_top_p = 1.0 / jnp.sum(jnp.exp(g - g_max), axis=0, keepdims=True)

        e_sel = lt_ref[GROUP_ROWS:GROUP_ROWS + EXPERTS_PER_GROUP, lanes]
        for grp in range(1, N_GROUPS):
            lo = GROUP_ROWS + grp * EXPERTS_PER_GROUP
            e_sel = jnp.where(g_idx == grp, lt_ref[lo:lo + EXPERTS_PER_GROUP, lanes], e_sel)
        m1, i1 = first_index_of_max(e_sel)
        rest = jnp.where(sub8 == i1, neg_inf, e_sel)
        m2 = jnp.max(rest, axis=0, keepdims=True)
        i2 = jnp.min(jnp.where((rest == m2) & (sub8 != i1), sub8, f32(8)), axis=0, keepdims=True)

        e = jnp.exp(m2 - m1)
        w1 = g_top_p / (1.0 + e)
        w2 = g_top_p * e / (1.0 + e)
        first_is_lo = i1 < i2
        w_lo = jnp.where(first_is_lo, w1, w2)
        w_hi = jnp.where(first_is_lo, w2, w1)
        lo_e = jnp.minimum(i1, i2)
        hi_e = jnp.maximum(i1, i2)
        pair = lo_e * (2 * EXPERTS_PER_GROUP - 1 - lo_e) * 0.5 + hi_e - lo_e - 1.0
        cls = (g_idx * N_PAIRS + pair).astype(jnp.int32)

        onehot = (row_id == cls).astype(jnp.bfloat16)
        csum = jnp.dot(onehot, prefix_mat, preferred_element_type=f32)
        rank = jnp.sum(onehot.astype(f32) * (csum + run), axis=0, keepdims=True) - 1.0
        run = run + jnp.dot(onehot, ones_mat, preferred_element_type=f32)

        info_ref[:, lanes] = jnp.where(sub8 == 0, cls, jnp.where(sub8 == 1, rank.astype(jnp.int32), 0))
        meta8 = jnp.where(sub8 == 0, w_lo, jnp.where(sub8 == 1, w_hi, 0.0))
        meta_t = jnp.concatenate([meta8, jnp.zeros((LANES - 8, LANES), f32)], axis=0)
        meta_ref[lanes, :] = meta_t.T
        return run

    run = lax.fori_loop(0, ROUTE_BLOCK // LANES, lane_tile, run_ref[...], unroll=ROUTE_UNROLL)
    run_ref[...] = run
    counts_ref[...] = run


def _route_call(lt):
    return pl.pallas_call(
        _route_kernel,
        grid=(N_TOKENS // ROUTE_BLOCK,),
        in_specs=[pl.BlockSpec((ROUTER_ROWS, ROUTE_BLOCK), lambda i: (0, i))],
        out_specs=[
            pl.BlockSpec((8, ROUTE_BLOCK), lambda i: (0, i)),
            pl.BlockSpec((ROUTE_BLOCK, LANES), lambda i: (i, 0)),
            pl.BlockSpec((LANES, LANES), lambda i: (i // ROUTE_STEPS_PER_PART, 0)),
        ],
        out_shape=[
            jax.ShapeDtypeStruct((8, N_TOKENS), jnp.int32),
            jax.ShapeDtypeStruct((N_TOKENS, LANES), jnp.float32),
            jax.ShapeDtypeStruct((N_PARTS * LANES, LANES), jnp.float32),
        ],
        scratch_shapes=[pltpu.VMEM((LANES, LANES), jnp.float32)],
        compiler_params=pltpu.CompilerParams(dimension_semantics=("arbitrary",)),
        name="route_rank",
    )(lt)


def _dispatch_plan(counts):
    _, pair_lo, pair_hi = _pair_tables()
    tiles_c = (counts + TM - 1) // TM
    tiles_g = tiles_c.reshape(N_GROUPS, N_PAIRS).sum(axis=1)
    tiles_g_pad = (tiles_g + SUB - 1) // SUB * SUB
    g_start = jnp.cumsum(tiles_g_pad) - tiles_g_pad
    tc = tiles_c.reshape(N_GROUPS, N_PAIRS)
    c_start = (g_start[:, None] + jnp.cumsum(tc, axis=1) - tc).reshape(N_CLASSES)
    class_start = jnp.zeros((LANES,), jnp.int32).at[:N_CLASSES].set(c_start * TM)

    tile_ids = jnp.arange(N_TILES, dtype=jnp.int32)
    c_end = c_start + tiles_c
    owner = jnp.sum((tile_ids[:, None] >= c_end[None, :]).astype(jnp.int32), axis=1)
    owner = jnp.minimum(owner, N_CLASSES - 1)
    in_class = (tile_ids >= c_start[owner]) & (tile_ids < c_end[owner])
    rows_left = counts[owner] - (tile_ids - c_start[owner]) * TM
    tile_rows = jnp.where(in_class, jnp.minimum(rows_left, TM), 0).astype(jnp.int32)
    pair = owner % N_PAIRS
    tile_ea = jnp.asarray(pair_lo)[pair]
    tile_eb = jnp.asarray(pair_hi)[pair]
    g_end = g_start + tiles_g_pad
    step_first = jnp.arange(N_STEPS3, dtype=jnp.int32) * SUB
    step_group = jnp.sum((step_first[:, None] >= g_end[None, :]).astype(jnp.int32), axis=1)
    step_group = jnp.minimum(step_group, N_GROUPS - 1)
    last_step = jnp.maximum(g_end[-1] // SUB - 1, 0)
    step_block = jnp.minimum(jnp.arange(N_STEPS3, dtype=jnp.int32), last_step)
    return (class_start, tile_ea, tile_eb, tile_rows, step_group.astype(jnp.int32),
            step_block.astype(jnp.int32))


def kernel(x, w_in, rg_conv_w, rg_conv_b, rg_gate_a_w, rg_gate_a_b, rg_gate_x_w, rg_gate_x_b,
           rg_lambda, sc_conv_w, w_out, ln1_g, ln1_b, router_group_w, router_group_b,
           router_expert_w, router_expert_b, exp_w_gate, exp_w_up, exp_w_down, ln2_g, ln2_b):
    bf16 = jnp.bfloat16
    row = lambda v: v.reshape(1, -1)
    pad_g = GROUP_ROWS - N_GROUPS
    pad_e = ROUTER_LANES - ROUTER_ROWS
    wr = jnp.concatenate([router_group_w, jnp.zeros((D_MODEL, pad_g), jnp.float32),
                          router_expert_w, jnp.zeros((D_MODEL, pad_e), jnp.float32)], axis=1)
    br = jnp.concatenate([router_group_b, jnp.zeros((pad_g,), jnp.float32),
                          router_expert_b, jnp.zeros((pad_e,), jnp.float32)]).reshape(1, -1)
    wr_hi = wr.astype(bf16)
    wr_lo = (wr - wr_hi.astype(jnp.float32)).astype(bf16)

    h1, lt, wgu, wdb = _mixer_call(
        x, w_in.astype(bf16), rg_conv_w, row(rg_conv_b),
        _block_diag(rg_gate_a_w).astype(bf16), row(rg_gate_a_b),
        _block_diag(rg_gate_x_w).astype(bf16), row(rg_gate_x_b),
        row(rg_lambda), sc_conv_w, w_out.astype(bf16), row(ln1_g), row(ln1_b),
        jnp.concatenate([wr_hi, wr_lo], axis=1), br, exp_w_gate, exp_w_up, exp_w_down)

    info, meta, counts = _route_call(lt)

    row_tiles = (-1, LANE_TILES, LANES)
    grp_shape = (N_GROUPS, EXPERTS_PER_GROUP)
    h1_tiles = h1.reshape(row_tiles)
    ys_parts, pos_parts = [], []
    for part in range(N_PARTS):
        part_counts = counts[part * LANES:part * LANES + N_CLASSES, 0].astype(jnp.int32)
        class_start, tile_ea, tile_eb, tile_rows, step_group, step_block = _dispatch_plan(
            part_counts)
        xs, wts, pos = _sc_dispatch(h1_tiles, meta, info[0], info[1], class_start, part)
        ys = _expert_call(
            step_block, step_group, tile_ea, tile_eb, tile_rows, xs.reshape(-1, LANES), wts,
            wgu.reshape(grp_shape + (D_MODEL, 2 * D_EXPERT)),
            wdb.reshape(grp_shape + (D_EXPERT, D_MODEL)),
            row(ln2_g), row(ln2_b))
        ys_parts.append(ys.reshape(row_tiles))
        pos_parts.append(pos.reshape(SEQ // N_PARTS, BATCH).T)

    out_rows = jnp.stack(pos_parts, axis=1).reshape(-1)
    out = _sc_unpermute(ys_parts, out_rows, COMBINE_CHUNK, (D_MODEL,))
    return out.reshape(BATCH, SEQ, D_MODEL)
```

```python
import functools

import jax
import jax.numpy as jnp
import numpy as np
from jax import lax
from jax.experimental import pallas as pl
from jax.experimental.pallas import tpu as pltpu
from jax.experimental.pallas import tpu_sc as plsc

D_MODEL = 1024
BATCH = 16
SEQ = 2048
D_RG = 512
D_SC = 512
RG_HEADS = 8
RG_HEAD_DIM = D_RG // RG_HEADS
RG_CONV = 4
RG_C = 8.0
SC_CONV = 3
D_IN_PROJ = 2 * D_RG + 3 * D_SC
N_GROUPS = 4
EXPERTS_PER_GROUP = 8
N_EXPERTS = N_GROUPS * EXPERTS_PER_GROUP
D_EXPERT = D_MODEL // 4
LN_EPS = 1e-5
DEEPNORM_ALPHA = 2.0 ** 0.25

N_TOKENS = BATCH * SEQ
LANES = 128
LANE_TILES = D_MODEL // LANES
MXU_WIDTH = 256
ROUTER_LANES = LANES
GROUP_ROWS = 8
ROUTER_ROWS = GROUP_ROWS + N_EXPERTS

TS = 64
M1 = TS * BATCH
DOT_ROWS = 256

ROUTE_BLOCK = 4096
ROUTE_UNROLL = 4

N_PAIRS = EXPERTS_PER_GROUP * (EXPERTS_PER_GROUP - 1) // 2
N_CLASSES = N_GROUPS * N_PAIRS
BF16_ROWS = 16
N_PARTS = 2
PART_TOKENS = N_TOKENS // N_PARTS
TM = -(-int(PART_TOKENS / N_CLASSES * 1.09) // BF16_ROWS) * BF16_ROWS
SUB = 8
N_TILES = -(-(PART_TOKENS // TM + N_CLASSES + N_GROUPS * (SUB - 1)) // SUB) * SUB
N_STEPS3 = N_TILES // SUB
P_ROWS = N_TILES * TM
ROUTE_STEPS_PER_PART = PART_TOKENS // ROUTE_BLOCK

VMEM_LIMIT = 56 * 1024 * 1024


def _layer_norm(z, g, b):
    mu = jnp.mean(z, axis=-1, keepdims=True)
    zc = z - mu
    var = jnp.mean(zc * zc, axis=-1, keepdims=True)
    return zc * lax.rsqrt(var + LN_EPS) * g + b


def _store_row_tiles(ref, first_row, value):
    n = value.shape[0]
    for j in range(LANE_TILES):
        ref[pl.ds(first_row * LANE_TILES + j, n, stride=LANE_TILES), :] = (
            value[:, j * LANES:(j + 1) * LANES])


def _load_row_tiles(ref, first_row, n):
    return jnp.concatenate(
        [ref[pl.ds(first_row * LANE_TILES + j, n, stride=LANE_TILES), :]
         for j in range(LANE_TILES)], axis=1)


def _mixer_kernel(x_ref, w_in_ref, cw_ref, cb_ref, wa_ref, ba_ref, wx_ref, bx_ref, lam_ref,
                  scw_ref, w_out_ref, g1_ref, b1_ref, wr_ref, br_ref, wg_ref, wu_ref, wd_ref,
                  h1_ref, logit_ref, wgu_ref, wdb_ref,
                  xbuf_ref, xsem, cbuf_ref, sbuf_ref, a_ref, u_ref, hstate_ref):
    c = pl.program_id(0)

    wgu_ref[:, :, :D_EXPERT] = wg_ref[...].astype(jnp.bfloat16)
    wgu_ref[:, :, D_EXPERT:] = wu_ref[...].astype(jnp.bfloat16)
    wdb_ref[...] = wd_ref[...].astype(jnp.bfloat16)
    slot = c % 2
    rg_pad = (RG_CONV - 1) * BATCH
    sc_pad = (SC_CONV - 1) * BATCH

    def x_copies(chunk, slot_):
        return [pltpu.make_async_copy(x_ref.at[b, pl.ds(chunk * TS, TS), :],
                                      xbuf_ref.at[slot_, :, b, :], xsem.at[slot_])
                for b in range(BATCH)]

    @pl.when(c == 0)
    def _():
        for cp in x_copies(0, 0):
            cp.start()
        cbuf_ref[0:rg_pad, :] = jnp.zeros((rg_pad, D_RG), jnp.float32)
        sbuf_ref[0:sc_pad, :] = jnp.zeros((sc_pad, D_SC), jnp.float32)
        hstate_ref[...] = jnp.zeros_like(hstate_ref)

    @pl.when(c + 1 < SEQ // TS)
    def _():
        for cp in x_copies(c + 1, 1 - slot):
            cp.start()

    for cp in x_copies(c, slot):
        cp.wait()

    xt_ref = xbuf_ref.at[slot]
    def gate(xcb, w_ref, b_ref):
        parts = [jnp.dot(xcb[:, lo:lo + MXU_WIDTH], w_ref[lo:lo + MXU_WIDTH, lo:lo + MXU_WIDTH],
                         preferred_element_type=jnp.float32)
                 for lo in range(0, D_RG, MXU_WIDTH)]
        return 0.5 + 0.5 * jnp.tanh(0.5 * (jnp.concatenate(parts, axis=1) + b_ref[...]))

    soft_lam = jax.nn.softplus(-lam_ref[...])
    gelu_chunks, y_sc_chunks = [], []
    for r0 in range(0, M1, DOT_ROWS):
        t0 = r0 // BATCH
        xk = xt_ref[t0:t0 + DOT_ROWS // BATCH].reshape(DOT_ROWS, D_MODEL).astype(jnp.bfloat16)

        def proj(lo, width):
            return jnp.dot(xk, w_in_ref[:, lo:lo + width], preferred_element_type=jnp.float32)

        cbuf_ref[rg_pad + r0:rg_pad + r0 + DOT_ROWS, :] = proj(0, D_RG)
        rg_gate = proj(D_RG, D_RG)
        xc = cb_ref[...] + cw_ref[0:1, :] * cbuf_ref[r0:r0 + DOT_ROWS, :]
        for k in range(1, RG_CONV):
            xc = xc + cw_ref[k:k + 1, :] * cbuf_ref[r0 + k * BATCH:r0 + k * BATCH + DOT_ROWS, :]
        sc_b = proj(2 * D_RG, D_SC)
        xcb = xc.astype(jnp.bfloat16)
        r = gate(xcb, wa_ref, ba_ref)
        i = gate(xcb, wx_ref, bx_ref)
        sc_c = proj(2 * D_RG + D_SC, D_SC)
        log_a = (-RG_C) * r * soft_lam
        a_ref[r0:r0 + DOT_ROWS, :] = jnp.exp(log_a)
        th = jnp.tanh(log_a)
        u_ref[r0:r0 + DOT_ROWS, :] = jnp.sqrt(-2.0 * th / (1.0 - th)) * (i * xc)
        gelu_chunks.append(jax.nn.gelu(rg_gate))

        sbuf_ref[sc_pad + r0:sc_pad + r0 + DOT_ROWS, :] = sc_c * proj(2 * D_RG + 2 * D_SC, D_SC)
        conv = scw_ref[0:1, :] * sbuf_ref[r0:r0 + DOT_ROWS, :]
        for k in range(1, SC_CONV):
            conv = conv + scw_ref[k:k + 1, :] * sbuf_ref[r0 + k * BATCH:r0 + k * BATCH + DOT_ROWS, :]
        y_sc_chunks.append((sc_b * conv).astype(jnp.bfloat16))

    cbuf_ref[0:rg_pad, :] = cbuf_ref[M1:M1 + rg_pad, :]
    sbuf_ref[0:sc_pad, :] = sbuf_ref[M1:M1 + sc_pad, :]

    h = hstate_ref[...]
    for t in range(TS):
        h = a_ref[t * BATCH:(t + 1) * BATCH, :] * h + u_ref[t * BATCH:(t + 1) * BATCH, :]
        u_ref[t * BATCH:(t + 1) * BATCH, :] = h
    hstate_ref[...] = h

    tail_rows = M1 // 2
    per_tail = tail_rows // DOT_ROWS

    def out_proj(k):
        r0 = k * tail_rows
        gelu_k = jnp.concatenate(gelu_chunks[k * per_tail:(k + 1) * per_tail], axis=0)
        y_sc_k = jnp.concatenate(y_sc_chunks[k * per_tail:(k + 1) * per_tail], axis=0)
        y_rg = (u_ref[r0:r0 + tail_rows, :] * gelu_k).astype(jnp.bfloat16)
        mix = jnp.dot(y_rg, w_out_ref[0:D_RG, :], preferred_element_type=jnp.float32)
        return mix + jnp.dot(y_sc_k, w_out_ref[D_RG:, :], preferred_element_type=jnp.float32)

    def finish(k, mix):
        r0 = k * tail_rows
        t0 = r0 // BATCH
        x_rows = xt_ref[t0:t0 + tail_rows // BATCH].reshape(tail_rows, D_MODEL)
        h1 = _layer_norm(DEEPNORM_ALPHA * x_rows + mix, g1_ref[...], b1_ref[...])
        _store_row_tiles(h1_ref, r0, h1)
        h_hi = h1.astype(jnp.bfloat16)
        h_lo = (h1 - h_hi.astype(jnp.float32)).astype(jnp.bfloat16)
        both = jnp.dot(h_hi, wr_ref[...], preferred_element_type=jnp.float32)
        logits = both[:, :ROUTER_LANES] + both[:, ROUTER_LANES:] + br_ref[...]
        logits = logits + jnp.dot(h_lo, wr_ref[:, :ROUTER_LANES],
                                  preferred_element_type=jnp.float32)
        logit_ref[:, r0:r0 + tail_rows] = logits.T[:ROUTER_ROWS, :]

    n_tails = M1 // tail_rows
    mix = out_proj(0)
    for k in range(n_tails):
        next_mix = out_proj(k + 1) if k + 1 < n_tails else None
        finish(k, mix)
        mix = next_mix


def _const_spec(shape):
    return pl.BlockSpec(shape, lambda c: (0,) * len(shape))


def _mixer_call(x, w_in, cw, cb, wa, ba, wx, bx, lam, scw, w_out, g1, b1, wr, br, wg, wu, wd):
    n_chunks = SEQ // TS
    epc = N_EXPERTS // n_chunks
    assert epc * n_chunks == N_EXPERTS
    in_specs = [
        pl.BlockSpec(memory_space=pl.ANY),
        _const_spec((D_MODEL, D_IN_PROJ)),
        _const_spec((RG_CONV, D_RG)), _const_spec((1, D_RG)),
        _const_spec((D_RG, D_RG)), _const_spec((1, D_RG)),
        _const_spec((D_RG, D_RG)), _const_spec((1, D_RG)),
        _const_spec((1, D_RG)),
        _const_spec((SC_CONV, D_SC)),
        _const_spec((D_MODEL, D_MODEL)),
        _const_spec((1, D_MODEL)), _const_spec((1, D_MODEL)),
        _const_spec((D_MODEL, 2 * ROUTER_LANES)), _const_spec((1, ROUTER_LANES)),
        pl.BlockSpec((epc, D_MODEL, D_EXPERT), lambda c: (c, 0, 0)),
        pl.BlockSpec((epc, D_MODEL, D_EXPERT), lambda c: (c, 0, 0)),
        pl.BlockSpec((epc, D_EXPERT, D_MODEL), lambda c: (c, 0, 0)),
    ]
    out_specs = [
        pl.BlockSpec((M1 * LANE_TILES, LANES), lambda c: (c, 0)),
        pl.BlockSpec((ROUTER_ROWS, M1), lambda c: (0, c)),
        pl.BlockSpec((epc, D_MODEL, 2 * D_EXPERT), lambda c: (c, 0, 0)),
        pl.BlockSpec((epc, D_EXPERT, D_MODEL), lambda c: (c, 0, 0)),
    ]
    return pl.pallas_call(
        _mixer_kernel,
        grid=(n_chunks,),
        in_specs=in_specs,
        out_specs=out_specs,
        out_shape=[
            jax.ShapeDtypeStruct((N_TOKENS * LANE_TILES, LANES), jnp.float32),
            jax.ShapeDtypeStruct((ROUTER_ROWS, N_TOKENS), jnp.float32),
            jax.ShapeDtypeStruct((N_EXPERTS, D_MODEL, 2 * D_EXPERT), jnp.bfloat16),
            jax.ShapeDtypeStruct((N_EXPERTS, D_EXPERT, D_MODEL), jnp.bfloat16),
        ],
        scratch_shapes=[
            pltpu.VMEM((2, TS, BATCH, D_MODEL), jnp.float32),
            pltpu.SemaphoreType.DMA((2,)),
            pltpu.VMEM((M1 + (RG_CONV - 1) * BATCH, D_RG), jnp.float32),
            pltpu.VMEM((M1 + (SC_CONV - 1) * BATCH, D_SC), jnp.float32),
            pltpu.VMEM((M1, D_RG), jnp.float32),
            pltpu.VMEM((M1, D_RG), jnp.float32),
            pltpu.VMEM((BATCH, D_RG), jnp.float32),
        ],
        compiler_params=pltpu.CompilerParams(
            dimension_semantics=("arbitrary",), vmem_limit_bytes=VMEM_LIMIT),
        name="mixer_ln_router",
    )(x, w_in, cw, cb, wa, ba, wx, bx, lam, scw, w_out, g1, b1, wr, br, wg, wu, wd)


def _expert_kernel(blk_ref, grp_ref, ea_ref, eb_ref, rows_ref,
                   xs_ref, wt_ref, wgu_ref, wd_ref, g2_ref, b2_ref, out_ref):
    s = pl.program_id(0)
    tile_rows = [rows_ref[s * SUB + j] for j in range(SUB)]

    @pl.when(blk_ref[s] == s)
    def _():
        for j in range(SUB):
            tile = s * SUB + j
            first_row = j * TM
            in_use = lax.broadcasted_iota(jnp.int32, (TM, 1), 0) < tile_rows[j]
            x = jnp.where(in_use, _load_row_tiles(xs_ref, first_row, TM), 0.0)
            xb = x.astype(jnp.bfloat16)
            wt = jnp.where(in_use, wt_ref[first_row:first_row + TM, :], 0.0)
            y = jnp.zeros((TM, D_MODEL), jnp.float32)
            for e_ref, col in ((ea_ref, 0), (eb_ref, 1)):
                e = e_ref[tile]
                hgu = jnp.dot(xb, wgu_ref[e], preferred_element_type=jnp.float32)
                hid = jax.nn.silu(hgu[:, :D_EXPERT]) * hgu[:, D_EXPERT:] * wt[:, col:col + 1]
                y = y + jnp.dot(hid.astype(jnp.bfloat16), wd_ref[e],
                                preferred_element_type=jnp.float32)
            _store_row_tiles(out_ref, first_row,
                             _layer_norm(DEEPNORM_ALPHA * x + y, g2_ref[...], b2_ref[...]))


def _expert_call(step_block, step_group, tile_ea, tile_eb, tile_rows, xs, wts, wgu, wd, g2, b2):
    rows = SUB * TM
    grid_spec = pltpu.PrefetchScalarGridSpec(
        num_scalar_prefetch=5,
        grid=(N_STEPS3,),
        in_specs=[
            pl.BlockSpec((rows * LANE_TILES, LANES), lambda s, blk, grp, ea, eb, va: (blk[s], 0)),
            pl.BlockSpec((rows, LANES), lambda s, blk, grp, ea, eb, va: (blk[s], 0)),
            pl.BlockSpec((None, EXPERTS_PER_GROUP, D_MODEL, 2 * D_EXPERT),
                         lambda s, blk, grp, ea, eb, va: (grp[s], 0, 0, 0)),
            pl.BlockSpec((None, EXPERTS_PER_GROUP, D_EXPERT, D_MODEL),
                         lambda s, blk, grp, ea, eb, va: (grp[s], 0, 0, 0)),
            pl.BlockSpec((1, D_MODEL), lambda s, blk, grp, ea, eb, va: (0, 0)),
            pl.BlockSpec((1, D_MODEL), lambda s, blk, grp, ea, eb, va: (0, 0)),
        ],
        out_specs=pl.BlockSpec((rows * LANE_TILES, LANES),
                               lambda s, blk, grp, ea, eb, va: (blk[s], 0)),
    )
    return pl.pallas_call(
        _expert_kernel,
        grid_spec=grid_spec,
        out_shape=jax.ShapeDtypeStruct((P_ROWS * LANE_TILES, LANES), jnp.float32),
        compiler_params=pltpu.CompilerParams(
            dimension_semantics=("arbitrary",), vmem_limit_bytes=VMEM_LIMIT),
        cost_estimate=pl.CostEstimate(
            flops=P_ROWS * (2 * 3 * 2 * D_MODEL * D_EXPERT + 10 * D_MODEL),
            transcendentals=P_ROWS * (2 * D_EXPERT + 1),
            bytes_accessed=sum(a.size * a.dtype.itemsize for a in (xs, wts, wgu, wd, g2, b2))
            + P_ROWS * D_MODEL * 4),
        name="experts_ln",
    )(step_block, step_group, tile_ea, tile_eb, tile_rows, xs, wts, wgu, wd, g2, b2)


SC_CORES = 2
SC_SUBCORES = 16
SC_WORKERS = SC_CORES * SC_SUBCORES
COMBINE_CHUNK = 32


def _sc_unpermute(tables, idx, chunk, out_row_shape):
    n_out, = idx.shape
    row_shape = tables[0].shape[1:]
    dtype = tables[0].dtype
    per_w = n_out // SC_WORKERS
    n_chunks = per_w // chunk
    part_len = SEQ // N_PARTS
    assert per_w * SC_WORKERS == n_out and n_chunks * chunk == per_w
    assert n_chunks % 2 == 0 and chunk % 8 == 0 and part_len % per_w == 0
    mesh = plsc.VectorSubcoreMesh(core_axis_name="c", subcore_axis_name="s")

    @functools.partial(
        pl.kernel, mesh=mesh,
        out_type=jax.ShapeDtypeStruct((n_out,) + out_row_shape, dtype),
        scratch_types=[
            pltpu.VMEM((per_w,), jnp.int32),
            pltpu.VMEM((2, chunk) + row_shape, dtype),
            pltpu.SemaphoreType.DMA((2,)),
            pltpu.SemaphoreType.DMA((2,)),
        ],
    )
    def gather_kernel(*refs):
        tables_hbm = refs[:N_PARTS]
        idx_hbm, out_hbm, idx_v, buf, gsem, wsem = refs[N_PARTS:]
        wid = lax.axis_index("s") * SC_CORES + lax.axis_index("c")
        base = wid * per_w
        part = (base % SEQ) // part_len
        pltpu.sync_copy(idx_hbm.at[pl.ds(base, per_w)], idx_v)

        def gather(j, slot, action):
            rows = idx_v.at[pl.ds(j * chunk, chunk)]
            for p, table_hbm in enumerate(tables_hbm):
                @pl.when(part == p)
                def _():
                    action(pltpu.make_async_copy(table_hbm.at[rows], buf.at[slot], gsem.at[slot]))

        def write(j, slot):
            dst = out_hbm.at[pl.ds(base + j * chunk, chunk)]
            src = buf.at[slot].reshape((chunk,) + out_row_shape)
            return pltpu.make_async_copy(src, dst, wsem.at[slot])

        start = lambda cp: cp.start()
        wait = lambda cp: cp.wait()
        gather(0, 0, start)

        @pl.loop(0, n_chunks, step=2)
        def _(j0):
            for slot in range(2):
                j = j0 + slot
                gather(j, slot, wait)

                @pl.when(j >= 1)
                def _():
                    write(j - 1, 1 - slot).wait()

                @pl.when(j + 1 < n_chunks)
                def _():
                    gather(j + 1, 1 - slot, start)

                write(j, slot).start()

        write(n_chunks - 1, (n_chunks - 1) % 2).wait()

    return gather_kernel(*tables, idx)


SC_LANES = 16


def _sc_dispatch(rows, meta, cls, rank, class_start, part):
    n_rows = PART_TOKENS
    row_off = part * PART_TOKENS
    chunk = COMBINE_CHUNK
    per_w = n_rows // SC_WORKERS
    n_chunks = per_w // chunk
    assert per_w * SC_WORKERS == n_rows and n_chunks * chunk == per_w
    assert n_chunks % 2 == 0 and chunk % SC_LANES == 0
    mesh = plsc.VectorSubcoreMesh(core_axis_name="c", subcore_axis_name="s")

    @functools.partial(
        pl.kernel, mesh=mesh,
        out_type=[
            jax.ShapeDtypeStruct((P_ROWS,) + rows.shape[1:], rows.dtype),
            jax.ShapeDtypeStruct((P_ROWS,) + meta.shape[1:], meta.dtype),
            jax.ShapeDtypeStruct((n_rows,), jnp.int32),
        ],
        scratch_types=[
            pltpu.VMEM((per_w,), jnp.int32),
            pltpu.VMEM((per_w,), jnp.int32),
            pltpu.VMEM((LANES,), jnp.int32),
            pltpu.VMEM((per_w,), jnp.int32),
            pltpu.VMEM((n_chunks, chunk), jnp.int32),
            pltpu.VMEM((2, chunk) + rows.shape[1:], rows.dtype),
            pltpu.VMEM((2, chunk) + meta.shape[1:], meta.dtype),
            pltpu.SemaphoreType.DMA((2,)),
            pltpu.SemaphoreType.DMA((2,)),
            pltpu.SemaphoreType.DMA((2,)),
            pltpu.SemaphoreType.DMA((2,)),
        ],
        compiler_params=pltpu.CompilerParams(needs_layout_passes=False),
        cost_estimate=pl.CostEstimate(
            flops=n_rows, transcendentals=0,
            bytes_accessed=2 * n_rows * (D_MODEL + LANES + 3) * 4),
    )
    def dispatch_kernel(rows_hbm, meta_hbm, cls_hbm, rank_hbm, start_hbm,
                        xs_hbm, ws_hbm, pos_hbm,
                        cls_v, rank_v, start_v, pos_flat, pos_v, rbuf, mbuf,
                        rsem, msem, xsem, wsem):
        wid = lax.axis_index("s") * SC_CORES + lax.axis_index("c")
        base = wid * per_w
        src_base = row_off + base
        pltpu.sync_copy(cls_hbm.at[pl.ds(src_base, per_w)], cls_v)
        pltpu.sync_copy(rank_hbm.at[pl.ds(src_base, per_w)], rank_v)
        pltpu.sync_copy(start_hbm, start_v)

        @pl.loop(0, n_chunks)
        def _(j):
            for k in range(chunk // SC_LANES):
                off = j * chunk + k * SC_LANES
                c = cls_v[pl.ds(off, SC_LANES)]
                pos = plsc.load_gather(start_v, [c]) + rank_v[pl.ds(off, SC_LANES)]
                pos_flat[pl.ds(off, SC_LANES)] = pos
                pos_v[j, pl.ds(k * SC_LANES, SC_LANES)] = pos

        pltpu.sync_copy(pos_flat, pos_hbm.at[pl.ds(base, per_w)])

        def read_rows(j, slot):
            src = rows_hbm.at[pl.ds(src_base + j * chunk, chunk)]
            return pltpu.make_async_copy(src, rbuf.at[slot], rsem.at[slot])

        def read_meta(j, slot):
            src = meta_hbm.at[pl.ds(src_base + j * chunk, chunk)]
            return pltpu.make_async_copy(src, mbuf.at[slot], msem.at[slot])

        def put_rows(j, slot):
            return pltpu.make_async_copy(rbuf.at[slot], xs_hbm.at[pos_v.at[j]], xsem.at[slot])

        def put_meta(j, slot):
            return pltpu.make_async_copy(mbuf.at[slot], ws_hbm.at[pos_v.at[j]], wsem.at[slot])

        read_rows(0, 0).start()
        read_meta(0, 0).start()

        @pl.loop(0, n_chunks, step=2)
        def _(j0):
            for slot in range(2):
                j = j0 + slot
                read_rows(j, slot).wait()
                read_meta(j, slot).wait()

                @pl.when(j >= 1)
                def _():
                    put_rows(j - 1, 1 - slot).wait()
                    put_meta(j - 1, 1 - slot).wait()

                @pl.when(j + 1 < n_chunks)
                def _():
                    read_rows(j + 1, 1 - slot).start()
                    read_meta(j + 1, 1 - slot).start()

                put_rows(j, slot).start()
                put_meta(j, slot).start()

        put_rows(n_chunks - 1, (n_chunks - 1) % 2).wait()
        put_meta(n_chunks - 1, (n_chunks - 1) % 2).wait()

    return dispatch_kernel(rows, meta, cls, rank, class_start)


def _block_diag(w):
    h, d, _ = w.shape
    eye = jnp.eye(h, dtype=w.dtype)
    return (eye[:, None, :, None] * w[:, :, None, :]).reshape(h * d, h * d)


def _pair_tables():
    pair_of = np.zeros((EXPERTS_PER_GROUP, EXPERTS_PER_GROUP), np.int32)
    lo = np.zeros((N_PAIRS,), np.int32)
    hi = np.zeros((N_PAIRS,), np.int32)
    p = 0
    for a in range(EXPERTS_PER_GROUP):
        for b in range(a + 1, EXPERTS_PER_GROUP):
            pair_of[a, b] = pair_of[b, a] = p
            lo[p], hi[p] = a, b
            p += 1
    return pair_of, lo, hi


def _route_kernel(lt_ref, info_ref, meta_ref, counts_ref, run_ref):
    step = pl.program_id(0)

    @pl.when(step % ROUTE_STEPS_PER_PART == 0)
    def _():
        run_ref[...] = jnp.zeros_like(run_ref)

    f32 = jnp.float32
    sub8 = lax.broadcasted_iota(jnp.int32, (8, LANES), 0).astype(f32)
    row_id = lax.broadcasted_iota(jnp.int32, (LANES, LANES), 0)
    col_id = lax.broadcasted_iota(jnp.int32, (LANES, LANES), 1)
    prefix_mat = (row_id <= col_id).astype(jnp.bfloat16)
    ones_mat = jnp.ones((LANES, LANES), jnp.bfloat16)
    neg_inf = f32(-jnp.inf)

    def first_index_of_max(v):
        m = jnp.max(v, axis=0, keepdims=True)
        idx = jnp.min(jnp.where(v == m, sub8, f32(8)), axis=0, keepdims=True)
        return m, idx

    def lane_tile(k, run):
        lanes = pl.ds(pl.multiple_of(k * LANES, LANES), LANES)
        g = jnp.where(sub8 < N_GROUPS, lt_ref[0:GROUP_ROWS, lanes], neg_inf)
        g_max, g_idx = first_index_of_max(g)
        g_top_p = 1.0 / jnp.sum(jnp.exp(g - g_max), axis=0, keepdims=True)

        e_sel = lt_ref[GROUP_ROWS:GROUP_ROWS + EXPERTS_PER_GROUP, lanes]
        for grp in range(1, N_GROUPS):
            lo = GROUP_ROWS + grp * EXPERTS_PER_GROUP
            e_sel = jnp.where(g_idx == grp, lt_ref[lo:lo + EXPERTS_PER_GROUP, lanes], e_sel)
        m1, i1 = first_index_of_max(e_sel)
        rest = jnp.where(sub8 == i1, neg_inf, e_sel)
        m2 = jnp.max(rest, axis=0, keepdims=True)
        i2 = jnp.min(jnp.where((rest == m2) & (sub8 != i1), sub8, f32(8)), axis=0, keepdims=True)

        e = jnp.exp(m2 - m1)
        w1 = g_top_p / (1.0 + e)
        w2 = g_top_p * e / (1.0 + e)
        first_is_lo = i1 < i2
        w_lo = jnp.where(first_is_lo, w1, w2)
        w_hi = jnp.where(first_is_lo, w2, w1)
        lo_e = jnp.minimum(i1, i2)
        hi_e = jnp.maximum(i1, i2)
        pair = lo_e * (2 * EXPERTS_PER_GROUP - 1 - lo_e) * 0.5 + hi_e - lo_e - 1.0
        cls = (g_idx * N_PAIRS + pair).astype(jnp.int32)

        onehot = (row_id == cls).astype(jnp.bfloat16)
        csum = jnp.dot(onehot, prefix_mat, preferred_element_type=f32)
        rank = jnp.sum(onehot.astype(f32) * (csum + run), axis=0, keepdims=True) - 1.0
        run = run + jnp.dot(onehot, ones_mat, preferred_element_type=f32)

        info_ref[:, lanes] = jnp.where(sub8 == 0, cls, jnp.where(sub8 == 1, rank.astype(jnp.int32), 0))
        meta8 = jnp.where(sub8 == 0, w_lo, jnp.where(sub8 == 1, w_hi, 0.0))
        meta_t = jnp.concatenate([meta8, jnp.zeros((LANES - 8, LANES), f32)], axis=0)
        meta_ref[lanes, :] = meta_t.T
        return run

    run = lax.fori_loop(0, ROUTE_BLOCK // LANES, lane_tile, run_ref[...], unroll=ROUTE_UNROLL)
    run_ref[...] = run
    counts_ref[...] = run


def _route_call(lt):
    return pl.pallas_call(
        _route_kernel,
        grid=(N_TOKENS // ROUTE_BLOCK,),
        in_specs=[pl.BlockSpec((ROUTER_ROWS, ROUTE_BLOCK), lambda i: (0, i))],
        out_specs=[
            pl.BlockSpec((8, ROUTE_BLOCK), lambda i: (0, i)),
            pl.BlockSpec((ROUTE_BLOCK, LANES), lambda i: (i, 0)),
            pl.BlockSpec((LANES, LANES), lambda i: (i // ROUTE_STEPS_PER_PART, 0)),
        ],
        out_shape=[
            jax.ShapeDtypeStruct((8, N_TOKENS), jnp.int32),
            jax.ShapeDtypeStruct((N_TOKENS, LANES), jnp.float32),
            jax.ShapeDtypeStruct((N_PARTS * LANES, LANES), jnp.float32),
        ],
        scratch_shapes=[pltpu.VMEM((LANES, LANES), jnp.float32)],
        compiler_params=pltpu.CompilerParams(dimension_semantics=("arbitrary",)),
        name="route_rank",
    )(lt)


def _dispatch_plan(counts):
    _, pair_lo, pair_hi = _pair_tables()
    tiles_c = (counts + TM - 1) // TM
    tiles_g = tiles_c.reshape(N_GROUPS, N_PAIRS).sum(axis=1)
    tiles_g_pad = (tiles_g + SUB - 1) // SUB * SUB
    g_start = jnp.cumsum(tiles_g_pad) - tiles_g_pad
    tc = tiles_c.reshape(N_GROUPS, N_PAIRS)
    c_start = (g_start[:, None] + jnp.cumsum(tc, axis=1) - tc).reshape(N_CLASSES)
    class_start = jnp.zeros((LANES,), jnp.int32).at[:N_CLASSES].set(c_start * TM)

    tile_ids = jnp.arange(N_TILES, dtype=jnp.int32)
    c_end = c_start + tiles_c
    owner = jnp.sum((tile_ids[:, None] >= c_end[None, :]).astype(jnp.int32), axis=1)
    owner = jnp.minimum(owner, N_CLASSES - 1)
    in_class = (tile_ids >= c_start[owner]) & (tile_ids < c_end[owner])
    rows_left = counts[owner] - (tile_ids - c_start[owner]) * TM
    tile_rows = jnp.where(in_class, jnp.minimum(rows_left, TM), 0).astype(jnp.int32)
    pair = owner % N_PAIRS
    tile_ea = jnp.asarray(pair_lo)[pair]
    tile_eb = jnp.asarray(pair_hi)[pair]
    g_end = g_start + tiles_g_pad
    step_first = jnp.arange(N_STEPS3, dtype=jnp.int32) * SUB
    step_group = jnp.sum((step_first[:, None] >= g_end[None, :]).astype(jnp.int32), axis=1)
    step_group = jnp.minimum(step_group, N_GROUPS - 1)
    last_step = jnp.maximum(g_end[-1] // SUB - 1, 0)
    step_block = jnp.minimum(jnp.arange(N_STEPS3, dtype=jnp.int32), last_step)
    return (class_start, tile_ea, tile_eb, tile_rows, step_group.astype(jnp.int32),
            step_block.astype(jnp.int32))


def kernel(x, w_in, rg_conv_w, rg_conv_b, rg_gate_a_w, rg_gate_a_b, rg_gate_x_w, rg_gate_x_b,
           rg_lambda, sc_conv_w, w_out, ln1_g, ln1_b, router_group_w, router_group_b,
           router_expert_w, router_expert_b, exp_w_gate, exp_w_up, exp_w_down, ln2_g, ln2_b):
    bf16 = jnp.bfloat16
    row = lambda v: v.reshape(1, -1)
    pad_g = GROUP_ROWS - N_GROUPS
    pad_e = ROUTER_LANES - ROUTER_ROWS
    wr = jnp.concatenate([router_group_w, jnp.zeros((D_MODEL, pad_g), jnp.float32),
                          router_expert_w, jnp.zeros((D_MODEL, pad_e), jnp.float32)], axis=1)
    br = jnp.concatenate([router_group_b, jnp.zeros((pad_g,), jnp.float32),
                          router_expert_b, jnp.zeros((pad_e,), jnp.float32)]).reshape(1, -1)
    wr_hi = wr.astype(bf16)
    wr_lo = (wr - wr_hi.astype(jnp.float32)).astype(bf16)

    h1, lt, wgu, wdb = _mixer_call(
        x, w_in.astype(bf16), rg_conv_w, row(rg_conv_b),
        _block_diag(rg_gate_a_w).astype(bf16), row(rg_gate_a_b),
        _block_diag(rg_gate_x_w).astype(bf16), row(rg_gate_x_b),
        row(rg_lambda), sc_conv_w, w_out.astype(bf16), row(ln1_g), row(ln1_b),
        jnp.concatenate([wr_hi, wr_lo], axis=1), br, exp_w_gate, exp_w_up, exp_w_down)

    info, meta, counts = _route_call(lt)

    row_tiles = (-1, LANE_TILES, LANES)
    grp_shape = (N_GROUPS, EXPERTS_PER_GROUP)
    h1_tiles = h1.reshape(row_tiles)
    ys_parts, pos_parts = [], []
    for part in range(N_PARTS):
        part_counts = counts[part * LANES:part * LANES + N_CLASSES, 0].astype(jnp.int32)
        class_start, tile_ea, tile_eb, tile_rows, step_group, step_block = _dispatch_plan(
            part_counts)
        xs, wts, pos = _sc_dispatch(h1_tiles, meta, info[0], info[1], class_start, part)
        ys = _expert_call(
            step_block, step_group, tile_ea, tile_eb, tile_rows, xs.reshape(-1, LANES), wts,
            wgu.reshape(grp_shape + (D_MODEL, 2 * D_EXPERT)),
            wdb.reshape(grp_shape + (D_EXPERT, D_MODEL)),
            row(ln2_g), row(ln2_b))
        ys_parts.append(ys.reshape(row_tiles))
        pos_parts.append(pos.reshape(SEQ // N_PARTS, BATCH).T)

    out_rows = jnp.stack(pos_parts, axis=1).reshape(-1)
    out = _sc_unpermute(ys_parts, out_rows, COMBINE_CHUNK, (D_MODEL,))
    return out.reshape(BATCH, SEQ, D_MODEL)
```

```python
import functools

import jax
import jax.numpy as jnp
import numpy as np
from jax import lax
from jax.experimental import pallas as pl
from jax.experimental.pallas import tpu as pltpu
from jax.experimental.pallas import tpu_sc as plsc

D_MODEL = 1024
BATCH = 16
SEQ = 2048
D_RG = 512
D_SC = 512
RG_HEADS = 8
RG_HEAD_DIM = D_RG // RG_HEADS
RG_CONV = 4
RG_C = 8.0
SC_CONV = 3
D_IN_PROJ = 2 * D_RG + 3 * D_SC
N_GROUPS = 4
EXPERTS_PER_GROUP = 8
N_EXPERTS = N_GROUPS * EXPERTS_PER_GROUP
D_EXPERT = D_MODEL // 4
LN_EPS = 1e-5
DEEPNORM_ALPHA = 2.0 ** 0.25

N_TOKENS = BATCH * SEQ
LANES = 128
LANE_TILES = D_MODEL // LANES
MXU_WIDTH = 256
ROUTER_LANES = LANES
GROUP_ROWS = 8
ROUTER_ROWS = GROUP_ROWS + N_EXPERTS

TS = 64
M1 = TS * BATCH
DOT_ROWS = 256

ROUTE_BLOCK = 4096
ROUTE_UNROLL = 8

N_PAIRS = EXPERTS_PER_GROUP * (EXPERTS_PER_GROUP - 1) // 2
N_CLASSES = N_GROUPS * N_PAIRS
BF16_ROWS = 16
TM = -(-int(N_TOKENS / N_CLASSES * 1.09 / 2) // BF16_ROWS) * BF16_ROWS
SUB = 8
N_TILES = -(-(N_TOKENS // TM + N_CLASSES + N_GROUPS * (SUB - 1)) // SUB) * SUB
N_STEPS3 = N_TILES // SUB
P_ROWS = N_TILES * TM

VMEM_LIMIT = 56 * 1024 * 1024


def _layer_norm(z, g, b):
    mu = jnp.mean(z, axis=-1, keepdims=True)
    zc = z - mu
    var = jnp.mean(zc * zc, axis=-1, keepdims=True)
    return zc * lax.rsqrt(var + LN_EPS) * g + b


def _store_row_tiles(ref, first_row, value):
    n = value.shape[0]
    for j in range(LANE_TILES):
        ref[pl.ds(first_row * LANE_TILES + j, n, stride=LANE_TILES), :] = (
            value[:, j * LANES:(j + 1) * LANES])


def _load_row_tiles(ref, first_row, n):
    return jnp.concatenate(
        [ref[pl.ds(first_row * LANE_TILES + j, n, stride=LANE_TILES), :]
         for j in range(LANE_TILES)], axis=1)


def _mixer_kernel(x_ref, w_in_ref, cw_ref, cb_ref, wa_ref, ba_ref, wx_ref, bx_ref, lam_ref,
                  scw_ref, w_out_ref, g1_ref, b1_ref, wr_ref, br_ref, wg_ref, wu_ref, wd_ref,
                  h1_ref, logit_ref, wgu_ref, wdb_ref,
                  xbuf_ref, xsem, cbuf_ref, sbuf_ref, a_ref, u_ref, hstate_ref):
    c = pl.program_id(0)

    wgu_ref[:, :, :D_EXPERT] = wg_ref[...].astype(jnp.bfloat16)
    wgu_ref[:, :, D_EXPERT:] = wu_ref[...].astype(jnp.bfloat16)
    wdb_ref[...] = wd_ref[...].astype(jnp.bfloat16)
    slot = c % 2
    rg_pad = (RG_CONV - 1) * BATCH
    sc_pad = (SC_CONV - 1) * BATCH

    def x_copies(chunk, slot_):
        return [pltpu.make_async_copy(x_ref.at[b, pl.ds(chunk * TS, TS), :],
                                      xbuf_ref.at[slot_, :, b, :], xsem.at[slot_])
                for b in range(BATCH)]

    @pl.when(c == 0)
    def _():
        for cp in x_copies(0, 0):
            cp.start()
        cbuf_ref[0:rg_pad, :] = jnp.zeros((rg_pad, D_RG), jnp.float32)
        sbuf_ref[0:sc_pad, :] = jnp.zeros((sc_pad, D_SC), jnp.float32)
        hstate_ref[...] = jnp.zeros_like(hstate_ref)

    @pl.when(c + 1 < SEQ // TS)
    def _():
        for cp in x_copies(c + 1, 1 - slot):
            cp.start()

    for cp in x_copies(c, slot):
        cp.wait()

    xt_ref = xbuf_ref.at[slot]
    def gate(xcb, w_ref, b_ref):
        parts = [jnp.dot(xcb[:, lo:lo + MXU_WIDTH], w_ref[lo:lo + MXU_WIDTH, lo:lo + MXU_WIDTH],
                         preferred_element_type=jnp.float32)
                 for lo in range(0, D_RG, MXU_WIDTH)]
        return 0.5 + 0.5 * jnp.tanh(0.5 * (jnp.concatenate(parts, axis=1) + b_ref[...]))

    soft_lam = jax.nn.softplus(-lam_ref[...])
    gelu_chunks, y_sc_chunks = [], []
    for r0 in range(0, M1, DOT_ROWS):
        t0 = r0 // BATCH
        xk = xt_ref[t0:t0 + DOT_ROWS // BATCH].reshape(DOT_ROWS, D_MODEL).astype(jnp.bfloat16)

        def proj(lo, width):
            return jnp.dot(xk, w_in_ref[:, lo:lo + width], preferred_element_type=jnp.float32)

        cbuf_ref[rg_pad + r0:rg_pad + r0 + DOT_ROWS, :] = proj(0, D_RG)
        rg_gate = proj(D_RG, D_RG)
        xc = cb_ref[...] + cw_ref[0:1, :] * cbuf_ref[r0:r0 + DOT_ROWS, :]
        for k in range(1, RG_CONV):
            xc = xc + cw_ref[k:k + 1, :] * cbuf_ref[r0 + k * BATCH:r0 + k * BATCH + DOT_ROWS, :]
        sc_b = proj(2 * D_RG, D_SC)
        xcb = xc.astype(jnp.bfloat16)
        r = gate(xcb, wa_ref, ba_ref)
        i = gate(xcb, wx_ref, bx_ref)
        sc_c = proj(2 * D_RG + D_SC, D_SC)
        log_a = (-RG_C) * r * soft_lam
        a_ref[r0:r0 + DOT_ROWS, :] = jnp.exp(log_a)
        th = jnp.tanh(log_a)
        u_ref[r0:r0 + DOT_ROWS, :] = jnp.sqrt(-2.0 * th / (1.0 - th)) * (i * xc)
        gelu_chunks.append(jax.nn.gelu(rg_gate))

        sbuf_ref[sc_pad + r0:sc_pad + r0 + DOT_ROWS, :] = sc_c * proj(2 * D_RG + 2 * D_SC, D_SC)
        conv = scw_ref[0:1, :] * sbuf_ref[r0:r0 + DOT_ROWS, :]
        for k in range(1, SC_CONV):
            conv = conv + scw_ref[k:k + 1, :] * sbuf_ref[r0 + k * BATCH:r0 + k * BATCH + DOT_ROWS, :]
        y_sc_chunks.append((sc_b * conv).astype(jnp.bfloat16))

    cbuf_ref[0:rg_pad, :] = cbuf_ref[M1:M1 + rg_pad, :]
    sbuf_ref[0:sc_pad, :] = sbuf_ref[M1:M1 + sc_pad, :]

    h = hstate_ref[...]
    for t in range(TS):
        h = a_ref[t * BATCH:(t + 1) * BATCH, :] * h + u_ref[t * BATCH:(t + 1) * BATCH, :]
        u_ref[t * BATCH:(t + 1) * BATCH, :] = h
    hstate_ref[...] = h

    tail_rows = M1 // 2
    per_tail = tail_rows // DOT_ROWS

    def out_proj(k):
        r0 = k * tail_rows
        gelu_k = jnp.concatenate(gelu_chunks[k * per_tail:(k + 1) * per_tail], axis=0)
        y_sc_k = jnp.concatenate(y_sc_chunks[k * per_tail:(k + 1) * per_tail], axis=0)
        y_rg = (u_ref[r0:r0 + tail_rows, :] * gelu_k).astype(jnp.bfloat16)
        mix = jnp.dot(y_rg, w_out_ref[0:D_RG, :], preferred_element_type=jnp.float32)
        return mix + jnp.dot(y_sc_k, w_out_ref[D_RG:, :], preferred_element_type=jnp.float32)

    def finish(k, mix):
        r0 = k * tail_rows
        t0 = r0 // BATCH
        x_rows = xt_ref[t0:t0 + tail_rows // BATCH].reshape(tail_rows, D_MODEL)
        h1 = _layer_norm(DEEPNORM_ALPHA * x_rows + mix, g1_ref[...], b1_ref[...])
        _store_row_tiles(h1_ref, r0, h1)
        h_hi = h1.astype(jnp.bfloat16)
        h_lo = (h1 - h_hi.astype(jnp.float32)).astype(jnp.bfloat16)
        both = jnp.dot(h_hi, wr_ref[...], preferred_element_type=jnp.float32)
        logits = both[:, :ROUTER_LANES] + both[:, ROUTER_LANES:] + br_ref[...]
        logits = logits + jnp.dot(h_lo, wr_ref[:, :ROUTER_LANES],
                                  preferred_element_type=jnp.float32)
        logit_ref[:, r0:r0 + tail_rows] = logits.T[:ROUTER_ROWS, :]

    n_tails = M1 // tail_rows
    mix = out_proj(0)
    for k in range(n_tails):
        next_mix = out_proj(k + 1) if k + 1 < n_tails else None
        finish(k, mix)
        mix = next_mix


def _const_spec(shape):
    return pl.BlockSpec(shape, lambda c: (0,) * len(shape))


def _mixer_call(x, w_in, cw, cb, wa, ba, wx, bx, lam, scw, w_out, g1, b1, wr, br, wg, wu, wd):
    n_chunks = SEQ // TS
    epc = N_EXPERTS // n_chunks
    assert epc * n_chunks == N_EXPERTS
    in_specs = [
        pl.BlockSpec(memory_space=pl.ANY),
        _const_spec((D_MODEL, D_IN_PROJ)),
        _const_spec((RG_CONV, D_RG)), _const_spec((1, D_RG)),
        _const_spec((D_RG, D_RG)), _const_spec((1, D_RG)),
        _const_spec((D_RG, D_RG)), _const_spec((1, D_RG)),
        _const_spec((1, D_RG)),
        _const_spec((SC_CONV, D_SC)),
        _const_spec((D_MODEL, D_MODEL)),
        _const_spec((1, D_MODEL)), _const_spec((1, D_MODEL)),
        _const_spec((D_MODEL, 2 * ROUTER_LANES)), _const_spec((1, ROUTER_LANES)),
        pl.BlockSpec((epc, D_MODEL, D_EXPERT), lambda c: (c, 0, 0)),
        pl.BlockSpec((epc, D_MODEL, D_EXPERT), lambda c: (c, 0, 0)),
        pl.BlockSpec((epc, D_EXPERT, D_MODEL), lambda c: (c, 0, 0)),
    ]
    out_specs = [
        pl.BlockSpec((M1 * LANE_TILES, LANES), lambda c: (c, 0)),
        pl.BlockSpec((ROUTER_ROWS, M1), lambda c: (0, c)),
        pl.BlockSpec((epc, D_MODEL, 2 * D_EXPERT), lambda c: (c, 0, 0)),
        pl.BlockSpec((epc, D_EXPERT, D_MODEL), lambda c: (c, 0, 0)),
    ]
    return pl.pallas_call(
        _mixer_kernel,
        grid=(n_chunks,),
        in_specs=in_specs,
        out_specs=out_specs,
        out_shape=[
            jax.ShapeDtypeStruct((N_TOKENS * LANE_TILES, LANES), jnp.float32),
            jax.ShapeDtypeStruct((ROUTER_ROWS, N_TOKENS), jnp.float32),
            jax.ShapeDtypeStruct((N_EXPERTS, D_MODEL, 2 * D_EXPERT), jnp.bfloat16),
            jax.ShapeDtypeStruct((N_EXPERTS, D_EXPERT, D_MODEL), jnp.bfloat16),
        ],
        scratch_shapes=[
            pltpu.VMEM((2, TS, BATCH, D_MODEL), jnp.float32),
            pltpu.SemaphoreType.DMA((2,)),
            pltpu.VMEM((M1 + (RG_CONV - 1) * BATCH, D_RG), jnp.float32),
            pltpu.VMEM((M1 + (SC_CONV - 1) * BATCH, D_SC), jnp.float32),
            pltpu.VMEM((M1, D_RG), jnp.float32),
            pltpu.VMEM((M1, D_RG), jnp.float32),
            pltpu.VMEM((BATCH, D_RG), jnp.float32),
        ],
        compiler_params=pltpu.CompilerParams(
            dimension_semantics=("arbitrary",), vmem_limit_bytes=VMEM_LIMIT),
        name="mixer_ln_router",
    )(x, w_in, cw, cb, wa, ba, wx, bx, lam, scw, w_out, g1, b1, wr, br, wg, wu, wd)


def _expert_kernel(blk_ref, grp_ref, ea_ref, eb_ref, rows_ref,
                   xs_ref, wt_ref, wgu_ref, wd_ref, g2_ref, b2_ref, out_ref):
    s = pl.program_id(0)
    tile_rows = [rows_ref[s * SUB + j] for j in range(SUB)]

    @pl.when(blk_ref[s] == s)
    def _():
        for j in range(SUB):
            tile = s * SUB + j
            first_row = j * TM
            in_use = lax.broadcasted_iota(jnp.int32, (TM, 1), 0) < tile_rows[j]
            x = jnp.where(in_use, _load_row_tiles(xs_ref, first_row, TM), 0.0)
            xb = x.astype(jnp.bfloat16)
            wt = jnp.where(in_use, wt_ref[first_row:first_row + TM, :], 0.0)
            y = jnp.zeros((TM, D_MODEL), jnp.float32)
            for e_ref, col in ((ea_ref, 0), (eb_ref, 1)):
                e = e_ref[tile]
                hgu = jnp.dot(xb, wgu_ref[e], preferred_element_type=jnp.float32)
                hid = jax.nn.silu(hgu[:, :D_EXPERT]) * hgu[:, D_EXPERT:] * wt[:, col:col + 1]
                y = y + jnp.dot(hid.astype(jnp.bfloat16), wd_ref[e],
                                preferred_element_type=jnp.float32)
            _store_row_tiles(out_ref, first_row,
                             _layer_norm(DEEPNORM_ALPHA * x + y, g2_ref[...], b2_ref[...]))


def _expert_call(step_block, step_group, tile_ea, tile_eb, tile_rows, xs, wts, wgu, wd, g2, b2):
    rows = SUB * TM
    grid_spec = pltpu.PrefetchScalarGridSpec(
        num_scalar_prefetch=5,
        grid=(N_STEPS3,),
        in_specs=[
            pl.BlockSpec((rows * LANE_TILES, LANES), lambda s, blk, grp, ea, eb, va: (blk[s], 0)),
            pl.BlockSpec((rows, LANES), lambda s, blk, grp, ea, eb, va: (blk[s], 0)),
            pl.BlockSpec((None, EXPERTS_PER_GROUP, D_MODEL, 2 * D_EXPERT),
                         lambda s, blk, grp, ea, eb, va: (grp[s], 0, 0, 0)),
            pl.BlockSpec((None, EXPERTS_PER_GROUP, D_EXPERT, D_MODEL),
                         lambda s, blk, grp, ea, eb, va: (grp[s], 0, 0, 0)),
            pl.BlockSpec((1, D_MODEL), lambda s, blk, grp, ea, eb, va: (0, 0)),
            pl.BlockSpec((1, D_MODEL), lambda s, blk, grp, ea, eb, va: (0, 0)),
        ],
        out_specs=pl.BlockSpec((rows * LANE_TILES, LANES),
                               lambda s, blk, grp, ea, eb, va: (blk[s], 0)),
    )
    return pl.pallas_call(
        _expert_kernel,
        grid_spec=grid_spec,
        out_shape=jax.ShapeDtypeStruct((P_ROWS * LANE_TILES, LANES), jnp.float32),
        compiler_params=pltpu.CompilerParams(
            dimension_semantics=("arbitrary",), vmem_limit_bytes=VMEM_LIMIT),
        name="experts_ln",
    )(step_block, step_group, tile_ea, tile_eb, tile_rows, xs, wts, wgu, wd, g2, b2)


SC_CORES = 2
SC_SUBCORES = 16
SC_WORKERS = SC_CORES * SC_SUBCORES
COMBINE_CHUNK = 32


def _sc_gather_rows(table, idx, chunk, out_row_shape=None):
    n_out, = idx.shape
    row_shape = table.shape[1:]
    out_row_shape = row_shape if out_row_shape is None else out_row_shape
    per_w = n_out // SC_WORKERS
    n_chunks = per_w // chunk
    assert per_w * SC_WORKERS == n_out and n_chunks * chunk == per_w
    assert n_chunks % 2 == 0 and chunk % 8 == 0
    mesh = plsc.VectorSubcoreMesh(core_axis_name="c", subcore_axis_name="s")

    @functools.partial(
        pl.kernel, mesh=mesh,
        out_type=jax.ShapeDtypeStruct((n_out,) + out_row_shape, table.dtype),
        scratch_types=[
            pltpu.VMEM((per_w,), jnp.int32),
            pltpu.VMEM((2, chunk) + row_shape, table.dtype),
            pltpu.SemaphoreType.DMA((2,)),
            pltpu.SemaphoreType.DMA((2,)),
        ],
    )
    def gather_kernel(table_hbm, idx_hbm, out_hbm, idx_v, buf, gsem, wsem):
        wid = lax.axis_index("s") * SC_CORES + lax.axis_index("c")
        base = wid * per_w
        pltpu.sync_copy(idx_hbm.at[pl.ds(base, per_w)], idx_v)

        def gather(j, slot):
            rows = idx_v.at[pl.ds(j * chunk, chunk)]
            return pltpu.make_async_copy(table_hbm.at[rows], buf.at[slot], gsem.at[slot])

        def write(j, slot):
            dst = out_hbm.at[pl.ds(base + j * chunk, chunk)]
            src = buf.at[slot].reshape((chunk,) + out_row_shape)
            return pltpu.make_async_copy(src, dst, wsem.at[slot])

        gather(0, 0).start()

        @pl.loop(0, n_chunks, step=2)
        def _(j0):
            for slot in range(2):
                j = j0 + slot
                gather(j, slot).wait()

                @pl.when(j >= 1)
                def _():
                    write(j - 1, 1 - slot).wait()

                @pl.when(j + 1 < n_chunks)
                def _():
                    gather(j + 1, 1 - slot).start()

                write(j, slot).start()

        write(n_chunks - 1, (n_chunks - 1) % 2).wait()

    return gather_kernel(table, idx)


SC_LANES = 16


def _sc_dispatch(rows, meta, cls, rank, class_start):
    n_rows = rows.shape[0]
    chunk = COMBINE_CHUNK
    per_w = n_rows // SC_WORKERS
    n_chunks = per_w // chunk
    assert per_w * SC_WORKERS == n_rows and n_chunks * chunk == per_w
    assert n_chunks % 2 == 0 and chunk % SC_LANES == 0
    mesh = plsc.VectorSubcoreMesh(core_axis_name="c", subcore_axis_name="s")

    @functools.partial(
        pl.kernel, mesh=mesh,
        out_type=[
            jax.ShapeDtypeStruct((P_ROWS,) + rows.shape[1:], rows.dtype),
            jax.ShapeDtypeStruct((P_ROWS,) + meta.shape[1:], meta.dtype),
            jax.ShapeDtypeStruct((n_rows,), jnp.int32),
        ],
        scratch_types=[
            pltpu.VMEM((per_w,), jnp.int32),
            pltpu.VMEM((per_w,), jnp.int32),
            pltpu.VMEM((LANES,), jnp.int32),
            pltpu.VMEM((per_w,), jnp.int32),
            pltpu.VMEM((n_chunks, chunk), jnp.int32),
            pltpu.VMEM((2, chunk) + rows.shape[1:], rows.dtype),
            pltpu.VMEM((2, chunk) + meta.shape[1:], meta.dtype),
            pltpu.SemaphoreType.DMA((2,)),
            pltpu.SemaphoreType.DMA((2,)),
            pltpu.SemaphoreType.DMA((2,)),
            pltpu.SemaphoreType.DMA((2,)),
        ],
        compiler_params=pltpu.CompilerParams(needs_layout_passes=False),
    )
    def dispatch_kernel(rows_hbm, meta_hbm, cls_hbm, rank_hbm, start_hbm,
                        xs_hbm, ws_hbm, pos_hbm,
                        cls_v, rank_v, start_v, pos_flat, pos_v, rbuf, mbuf,
                        rsem, msem, xsem, wsem):
        wid = lax.axis_index("s") * SC_CORES + lax.axis_index("c")
        base = wid * per_w
        pltpu.sync_copy(cls_hbm.at[pl.ds(base, per_w)], cls_v)
        pltpu.sync_copy(rank_hbm.at[pl.ds(base, per_w)], rank_v)
        pltpu.sync_copy(start_hbm, start_v)

        @pl.loop(0, n_chunks)
        def _(j):
            for k in range(chunk // SC_LANES):
                off = j * chunk + k * SC_LANES
                c = cls_v[pl.ds(off, SC_LANES)]
                pos = plsc.load_gather(start_v, [c]) + rank_v[pl.ds(off, SC_LANES)]
                pos_flat[pl.ds(off, SC_LANES)] = pos
                pos_v[j, pl.ds(k * SC_LANES, SC_LANES)] = pos

        pltpu.sync_copy(pos_flat, pos_hbm.at[pl.ds(base, per_w)])

        def read_rows(j, slot):
            src = rows_hbm.at[pl.ds(base + j * chunk, chunk)]
            return pltpu.make_async_copy(src, rbuf.at[slot], rsem.at[slot])

        def read_meta(j, slot):
            src = meta_hbm.at[pl.ds(base + j * chunk, chunk)]
            return pltpu.make_async_copy(src, mbuf.at[slot], msem.at[slot])

        def put_rows(j, slot):
            return pltpu.make_async_copy(rbuf.at[slot], xs_hbm.at[pos_v.at[j]], xsem.at[slot])

        def put_meta(j, slot):
            return pltpu.make_async_copy(mbuf.at[slot], ws_hbm.at[pos_v.at[j]], wsem.at[slot])

        read_rows(0, 0).start()
        read_meta(0, 0).start()

        @pl.loop(0, n_chunks, step=2)
        def _(j0):
            for slot in range(2):
                j = j0 + slot
                read_rows(j, slot).wait()
                read_meta(j, slot).wait()

                @pl.when(j >= 1)
                def _():
                    put_rows(j - 1, 1 - slot).wait()
                    put_meta(j - 1, 1 - slot).wait()

                @pl.when(j + 1 < n_chunks)
                def _():
                    read_rows(j + 1, 1 - slot).start()
                    read_meta(j + 1, 1 - slot).start()

                put_rows(j, slot).start()
                put_meta(j, slot).start()

        put_rows(n_chunks - 1, (n_chunks - 1) % 2).wait()
        put_meta(n_chunks - 1, (n_chunks - 1) % 2).wait()

    return dispatch_kernel(rows, meta, cls, rank, class_start)


def _block_diag(w):
    h, d, _ = w.shape
    eye = jnp.eye(h, dtype=w.dtype)
    return (eye[:, None, :, None] * w[:, :, None, :]).reshape(h * d, h * d)


def _pair_tables():
    pair_of = np.zeros((EXPERTS_PER_GROUP, EXPERTS_PER_GROUP), np.int32)
    lo = np.zeros((N_PAIRS,), np.int32)
    hi = np.zeros((N_PAIRS,), np.int32)
    p = 0
    for a in range(EXPERTS_PER_GROUP):
        for b in range(a + 1, EXPERTS_PER_GROUP):
            pair_of[a, b] = pair_of[b, a] = p
            lo[p], hi[p] = a, b
            p += 1
    return pair_of, lo, hi


def _route_kernel(lt_ref, info_ref, meta_ref, counts_ref, run_ref):
    step = pl.program_id(0)

    @pl.when(step == 0)
    def _():
        run_ref[...] = jnp.zeros_like(run_ref)

    f32 = jnp.float32
    sub8 = lax.broadcasted_iota(jnp.int32, (8, LANES), 0).astype(f32)
    row_id = lax.broadcasted_iota(jnp.int32, (LANES, LANES), 0)
    col_id = lax.broadcasted_iota(jnp.int32, (LANES, LANES), 1)
    prefix_mat = (row_id <= col_id).astype(jnp.bfloat16)
    ones_mat = jnp.ones((LANES, LANES), jnp.bfloat16)
    neg_inf = f32(-jnp.inf)

    def first_index_of_max(v):
        m = jnp.max(v, axis=0, keepdims=True)
        idx = jnp.min(jnp.where(v == m, sub8, f32(8)), axis=0, keepdims=True)
        return m, idx

    def lane_tile(k, run):
        lanes = pl.ds(pl.multiple_of(k * LANES, LANES), LANES)
        g = jnp.where(sub8 < N_GROUPS, lt_ref[0:GROUP_ROWS, lanes], neg_inf)
        g_max, g_idx = first_index_of_max(g)
        g_top_p = 1.0 / jnp.sum(jnp.exp(g - g_max), axis=0, keepdims=True)

        e_sel = lt_ref[GROUP_ROWS:GROUP_ROWS + EXPERTS_PER_GROUP, lanes]
        for grp in range(1, N_GROUPS):
            lo = GROUP_ROWS + grp * EXPERTS_PER_GROUP
            e_sel = jnp.where(g_idx == grp, lt_ref[lo:lo + EXPERTS_PER_GROUP, lanes], e_sel)
        m1, i1 = first_index_of_max(e_sel)
        rest = jnp.where(sub8 == i1, neg_inf, e_sel)
        m2 = jnp.max(rest, axis=0, keepdims=True)
        i2 = jnp.min(jnp.where((rest == m2) & (sub8 != i1), sub8, f32(8)), axis=0, keepdims=True)

        e = jnp.exp(m2 - m1)
        w1 = g_top_p / (1.0 + e)
        w2 = g_top_p * e / (1.0 + e)
        first_is_lo = i1 < i2
        w_lo = jnp.where(first_is_lo, w1, w2)
        w_hi = jnp.where(first_is_lo, w2, w1)
        lo_e = jnp.minimum(i1, i2)
        hi_e = jnp.maximum(i1, i2)
        pair = lo_e * (2 * EXPERTS_PER_GROUP - 1 - lo_e) * 0.5 + hi_e - lo_e - 1.0
        cls = (g_idx * N_PAIRS + pair).astype(jnp.int32)

        onehot = (row_id == cls).astype(jnp.bfloat16)
        csum = jnp.dot(onehot, prefix_mat, preferred_element_type=f32)
        rank = jnp.sum(onehot.astype(f32) * (csum + run), axis=0, keepdims=True) - 1.0
        run = run + jnp.dot(onehot, ones_mat, preferred_element_type=f32)

        info_ref[:, lanes] = jnp.where(sub8 == 0, cls, jnp.where(sub8 == 1, rank.astype(jnp.int32), 0))
        meta8 = jnp.where(sub8 == 0, w_lo, jnp.where(sub8 == 1, w_hi, 0.0))
        meta_t = jnp.concatenate([meta8, jnp.zeros((LANES - 8, LANES), f32)], axis=0)
        meta_ref[lanes, :] = meta_t.T
        return run

    run = lax.fori_loop(0, ROUTE_BLOCK // LANES, lane_tile, run_ref[...], unroll=ROUTE_UNROLL)
    run_ref[...] = run
    counts_ref[...] = run


def _route_call(lt):
    return pl.pallas_call(
        _route_kernel,
        grid=(N_TOKENS // ROUTE_BLOCK,),
        in_specs=[pl.BlockSpec((ROUTER_ROWS, ROUTE_BLOCK), lambda i: (0, i))],
        out_specs=[
            pl.BlockSpec((8, ROUTE_BLOCK), lambda i: (0, i)),
            pl.BlockSpec((ROUTE_BLOCK, LANES), lambda i: (i, 0)),
            pl.BlockSpec((LANES, LANES), lambda i: (0, 0)),
        ],
        out_shape=[
            jax.ShapeDtypeStruct((8, N_TOKENS), jnp.int32),
            jax.ShapeDtypeStruct((N_TOKENS, LANES), jnp.float32),
            jax.ShapeDtypeStruct((LANES, LANES), jnp.float32),
        ],
        scratch_shapes=[pltpu.VMEM((LANES, LANES), jnp.float32)],
        compiler_params=pltpu.CompilerParams(dimension_semantics=("arbitrary",)),
        name="route_rank",
    )(lt)


def _dispatch_plan(counts):
    _, pair_lo, pair_hi = _pair_tables()
    tiles_c = (counts + TM - 1) // TM
    tiles_g = tiles_c.reshape(N_GROUPS, N_PAIRS).sum(axis=1)
    tiles_g_pad = (tiles_g + SUB - 1) // SUB * SUB
    g_start = jnp.cumsum(tiles_g_pad) - tiles_g_pad
    tc = tiles_c.reshape(N_GROUPS, N_PAIRS)
    c_start = (g_start[:, None] + jnp.cumsum(tc, axis=1) - tc).reshape(N_CLASSES)
    class_start = jnp.zeros((LANES,), jnp.int32).at[:N_CLASSES].set(c_start * TM)

    tile_ids = jnp.arange(N_TILES, dtype=jnp.int32)
    c_end = c_start + tiles_c
    owner = jnp.sum((tile_ids[:, None] >= c_end[None, :]).astype(jnp.int32), axis=1)
    owner = jnp.minimum(owner, N_CLASSES - 1)
    in_class = (tile_ids >= c_start[owner]) & (tile_ids < c_end[owner])
    rows_left = counts[owner] - (tile_ids - c_start[owner]) * TM
    tile_rows = jnp.where(in_class, jnp.minimum(rows_left, TM), 0).astype(jnp.int32)
    pair = owner % N_PAIRS
    tile_ea = jnp.asarray(pair_lo)[pair]
    tile_eb = jnp.asarray(pair_hi)[pair]
    g_end = g_start + tiles_g_pad
    step_first = jnp.arange(N_STEPS3, dtype=jnp.int32) * SUB
    step_group = jnp.sum((step_first[:, None] >= g_end[None, :]).astype(jnp.int32), axis=1)
    step_group = jnp.minimum(step_group, N_GROUPS - 1)
    last_step = jnp.maximum(g_end[-1] // SUB - 1, 0)
    step_block = jnp.minimum(jnp.arange(N_STEPS3, dtype=jnp.int32), last_step)
    return (class_start, tile_ea, tile_eb, tile_rows, step_group.astype(jnp.int32),
            step_block.astype(jnp.int32))


def kernel(x, w_in, rg_conv_w, rg_conv_b, rg_gate_a_w, rg_gate_a_b, rg_gate_x_w, rg_gate_x_b,
           rg_lambda, sc_conv_w, w_out, ln1_g, ln1_b, router_group_w, router_group_b,
           router_expert_w, router_expert_b, exp_w_gate, exp_w_up, exp_w_down, ln2_g, ln2_b):
    bf16 = jnp.bfloat16
    row = lambda v: v.reshape(1, -1)
    pad_g = GROUP_ROWS - N_GROUPS
    pad_e = ROUTER_LANES - ROUTER_ROWS
    wr = jnp.concatenate([router_group_w, jnp.zeros((D_MODEL, pad_g), jnp.float32),
                          router_expert_w, jnp.zeros((D_MODEL, pad_e), jnp.float32)], axis=1)
    br = jnp.concatenate([router_group_b, jnp.zeros((pad_g,), jnp.float32),
                          router_expert_b, jnp.zeros((pad_e,), jnp.float32)]).reshape(1, -1)
    wr_hi = wr.astype(bf16)
    wr_lo = (wr - wr_hi.astype(jnp.float32)).astype(bf16)

    h1, lt, wgu, wdb = _mixer_call(
        x, w_in.astype(bf16), rg_conv_w, row(rg_conv_b),
        _block_diag(rg_gate_a_w).astype(bf16), row(rg_gate_a_b),
        _block_diag(rg_gate_x_w).astype(bf16), row(rg_gate_x_b),
        row(rg_lambda), sc_conv_w, w_out.astype(bf16), row(ln1_g), row(ln1_b),
        jnp.concatenate([wr_hi, wr_lo], axis=1), br, exp_w_gate, exp_w_up, exp_w_down)

    info, meta, counts = _route_call(lt)
    class_start, tile_ea, tile_eb, tile_rows, step_group, step_block = _dispatch_plan(
        counts[:N_CLASSES, 0].astype(jnp.int32))

    row_tiles = (-1, LANE_TILES, LANES)
    xs, wts, pos = _sc_dispatch(h1.reshape(row_tiles), meta, info[0], info[1], class_start)

    grp_shape = (N_GROUPS, EXPERTS_PER_GROUP)
    ys = _expert_call(
        step_block, step_group, tile_ea, tile_eb, tile_rows, xs.reshape(-1, LANES), wts,
        wgu.reshape(grp_shape + (D_MODEL, 2 * D_EXPERT)),
        wdb.reshape(grp_shape + (D_EXPERT, D_MODEL)),
        row(ln2_g), row(ln2_b))

    out_rows = pos.reshape(SEQ, BATCH).T.reshape(-1)
    out = _sc_gather_rows(ys.reshape(row_tiles), out_rows, COMBINE_CHUNK, (D_MODEL,))
    return out.reshape(BATCH, SEQ, D_MODEL)
```

```python
import functools

import jax
import jax.numpy as jnp
import numpy as np
from jax import lax
from jax.experimental import pallas as pl
from jax.experimental.pallas import tpu as pltpu
from jax.experimental.pallas import tpu_sc as plsc

D_MODEL = 1024
BATCH = 16
SEQ = 2048
D_RG = 512
D_SC = 512
RG_HEADS = 8
RG_HEAD_DIM = D_RG // RG_HEADS
RG_CONV = 4
RG_C = 8.0
SC_CONV = 3
D_IN_PROJ = 2 * D_RG + 3 * D_SC
N_GROUPS = 4
EXPERTS_PER_GROUP = 8
N_EXPERTS = N_GROUPS * EXPERTS_PER_GROUP
D_EXPERT = D_MODEL // 4
LN_EPS = 1e-5
DEEPNORM_ALPHA = 2.0 ** 0.25

N_TOKENS = BATCH * SEQ
LANES = 128
LANE_TILES = D_MODEL // LANES
MXU_WIDTH = 256
ROUTER_LANES = LANES
GROUP_ROWS = 8
ROUTER_ROWS = GROUP_ROWS + N_EXPERTS

TS = 64
M1 = TS * BATCH
DOT_ROWS = 256

ROUTE_BLOCK = 4096
ROUTE_UNROLL = 8

N_PAIRS = EXPERTS_PER_GROUP * (EXPERTS_PER_GROUP - 1) // 2
N_CLASSES = N_GROUPS * N_PAIRS
BF16_ROWS = 16
TM = -(-int(N_TOKENS / N_CLASSES * 1.09 / 2) // BF16_ROWS) * BF16_ROWS
SUB = 8
N_TILES = -(-(N_TOKENS // TM + N_CLASSES + N_GROUPS * (SUB - 1)) // SUB) * SUB
N_STEPS3 = N_TILES // SUB
P_ROWS = N_TILES * TM

VMEM_LIMIT = 56 * 1024 * 1024


def _layer_norm(z, g, b):
    mu = jnp.mean(z, axis=-1, keepdims=True)
    zc = z - mu
    var = jnp.mean(zc * zc, axis=-1, keepdims=True)
    return zc * lax.rsqrt(var + LN_EPS) * g + b


def _store_row_tiles(ref, first_row, value):
    n = value.shape[0]
    for j in range(LANE_TILES):
        ref[pl.ds(first_row * LANE_TILES + j, n, stride=LANE_TILES), :] = (
            value[:, j * LANES:(j + 1) * LANES])


def _load_row_tiles(ref, first_row, n):
    return jnp.concatenate(
        [ref[pl.ds(first_row * LANE_TILES + j, n, stride=LANE_TILES), :]
         for j in range(LANE_TILES)], axis=1)


def _mixer_kernel(x_ref, w_in_ref, cw_ref, cb_ref, wa_ref, ba_ref, wx_ref, bx_ref, lam_ref,
                  scw_ref, w_out_ref, g1_ref, b1_ref, wr_ref, br_ref, wg_ref, wu_ref, wd_ref,
                  h1_ref, logit_ref, wgu_ref, wdb_ref,
                  xbuf_ref, xsem, cbuf_ref, sbuf_ref, a_ref, u_ref, hstate_ref):
    c = pl.program_id(0)

    wgu_ref[:, :, :D_EXPERT] = wg_ref[...].astype(jnp.bfloat16)
    wgu_ref[:, :, D_EXPERT:] = wu_ref[...].astype(jnp.bfloat16)
    wdb_ref[...] = wd_ref[...].astype(jnp.bfloat16)
    slot = c % 2
    rg_pad = (RG_CONV - 1) * BATCH
    sc_pad = (SC_CONV - 1) * BATCH

    def x_copies(chunk, slot_):
        return [pltpu.make_async_copy(x_ref.at[b, pl.ds(chunk * TS, TS), :],
                                      xbuf_ref.at[slot_, :, b, :], xsem.at[slot_])
                for b in range(BATCH)]

    @pl.when(c == 0)
    def _():
        for cp in x_copies(0, 0):
            cp.start()
        cbuf_ref[0:rg_pad, :] = jnp.zeros((rg_pad, D_RG), jnp.float32)
        sbuf_ref[0:sc_pad, :] = jnp.zeros((sc_pad, D_SC), jnp.float32)
        hstate_ref[...] = jnp.zeros_like(hstate_ref)

    @pl.when(c + 1 < SEQ // TS)
    def _():
        for cp in x_copies(c + 1, 1 - slot):
            cp.start()

    for cp in x_copies(c, slot):
        cp.wait()

    xt_ref = xbuf_ref.at[slot]
    def gate(xcb, w_ref, b_ref):
        parts = [jnp.dot(xcb[:, lo:lo + MXU_WIDTH], w_ref[lo:lo + MXU_WIDTH, lo:lo + MXU_WIDTH],
                         preferred_element_type=jnp.float32)
                 for lo in range(0, D_RG, MXU_WIDTH)]
        return 0.5 + 0.5 * jnp.tanh(0.5 * (jnp.concatenate(parts, axis=1) + b_ref[...]))

    soft_lam = jax.nn.softplus(-lam_ref[...])
    gelu_chunks, y_sc_chunks = [], []
    for r0 in range(0, M1, DOT_ROWS):
        t0 = r0 // BATCH
        xk = xt_ref[t0:t0 + DOT_ROWS // BATCH].reshape(DOT_ROWS, D_MODEL).astype(jnp.bfloat16)

        def proj(lo, width):
            return jnp.dot(xk, w_in_ref[:, lo:lo + width], preferred_element_type=jnp.float32)

        cbuf_ref[rg_pad + r0:rg_pad + r0 + DOT_ROWS, :] = proj(0, D_RG)
        rg_gate = proj(D_RG, D_RG)
        xc = cb_ref[...] + cw_ref[0:1, :] * cbuf_ref[r0:r0 + DOT_ROWS, :]
        for k in range(1, RG_CONV):
            xc = xc + cw_ref[k:k + 1, :] * cbuf_ref[r0 + k * BATCH:r0 + k * BATCH + DOT_ROWS, :]
        sc_b = proj(2 * D_RG, D_SC)
        xcb = xc.astype(jnp.bfloat16)
        r = gate(xcb, wa_ref, ba_ref)
        i = gate(xcb, wx_ref, bx_ref)
        sc_c = proj(2 * D_RG + D_SC, D_SC)
        log_a = (-RG_C) * r * soft_lam
        a_ref[r0:r0 + DOT_ROWS, :] = jnp.exp(log_a)
        th = jnp.tanh(log_a)
        u_ref[r0:r0 + DOT_ROWS, :] = jnp.sqrt(-2.0 * th / (1.0 - th)) * (i * xc)
        gelu_chunks.append(jax.nn.gelu(rg_gate))

        sbuf_ref[sc_pad + r0:sc_pad + r0 + DOT_ROWS, :] = sc_c * proj(2 * D_RG + 2 * D_SC, D_SC)
        conv = scw_ref[0:1, :] * sbuf_ref[r0:r0 + DOT_ROWS, :]
        for k in range(1, SC_CONV):
            conv = conv + scw_ref[k:k + 1, :] * sbuf_ref[r0 + k * BATCH:r0 + k * BATCH + DOT_ROWS, :]
        y_sc_chunks.append((sc_b * conv).astype(jnp.bfloat16))

    cbuf_ref[0:rg_pad, :] = cbuf_ref[M1:M1 + rg_pad, :]
    sbuf_ref[0:sc_pad, :] = sbuf_ref[M1:M1 + sc_pad, :]

    h = hstate_ref[...]
    for t in range(TS):
        h = a_ref[t * BATCH:(t + 1) * BATCH, :] * h + u_ref[t * BATCH:(t + 1) * BATCH, :]
        u_ref[t * BATCH:(t + 1) * BATCH, :] = h
    hstate_ref[...] = h

    tail_rows = M1 // 2
    per_tail = tail_rows // DOT_ROWS

    def out_proj(k):
        r0 = k * tail_rows
        gelu_k = jnp.concatenate(gelu_chunks[k * per_tail:(k + 1) * per_tail], axis=0)
        y_sc_k = jnp.concatenate(y_sc_chunks[k * per_tail:(k + 1) * per_tail], axis=0)
        y_rg = (u_ref[r0:r0 + tail_rows, :] * gelu_k).astype(jnp.bfloat16)
        mix = jnp.dot(y_rg, w_out_ref[0:D_RG, :], preferred_element_type=jnp.float32)
        return mix + jnp.dot(y_sc_k, w_out_ref[D_RG:, :], preferred_element_type=jnp.float32)

    def finish(k, mix):
        r0 = k * tail_rows
        t0 = r0 // BATCH
        x_rows = xt_ref[t0:t0 + tail_rows // BATCH].reshape(tail_rows, D_MODEL)
        h1 = _layer_norm(DEEPNORM_ALPHA * x_rows + mix, g1_ref[...], b1_ref[...])
        _store_row_tiles(h1_ref, r0, h1)
        h_hi = h1.astype(jnp.bfloat16)
        h_lo = (h1 - h_hi.astype(jnp.float32)).astype(jnp.bfloat16)
        both = jnp.dot(h_hi, wr_ref[...], preferred_element_type=jnp.float32)
        logits = both[:, :ROUTER_LANES] + both[:, ROUTER_LANES:] + br_ref[...]
        logits = logits + jnp.dot(h_lo, wr_ref[:, :ROUTER_LANES],
                                  preferred_element_type=jnp.float32)
        logit_ref[:, r0:r0 + tail_rows] = logits.T[:ROUTER_ROWS, :]

    n_tails = M1 // tail_rows
    mix = out_proj(0)
    for k in range(n_tails):
        next_mix = out_proj(k + 1) if k + 1 < n_tails else None
        finish(k, mix)
        mix = next_mix


def _const_spec(shape):
    return pl.BlockSpec(shape, lambda c: (0,) * len(shape))


def _mixer_call(x, w_in, cw, cb, wa, ba, wx, bx, lam, scw, w_out, g1, b1, wr, br, wg, wu, wd):
    n_chunks = SEQ // TS
    epc = N_EXPERTS // n_chunks
    assert epc * n_chunks == N_EXPERTS
    in_specs = [
        pl.BlockSpec(memory_space=pl.ANY),
        _const_spec((D_MODEL, D_IN_PROJ)),
        _const_spec((RG_CONV, D_RG)), _const_spec((1, D_RG)),
        _const_spec((D_RG, D_RG)), _const_spec((1, D_RG)),
        _const_spec((D_RG, D_RG)), _const_spec((1, D_RG)),
        _const_spec((1, D_RG)),
        _const_spec((SC_CONV, D_SC)),
        _const_spec((D_MODEL, D_MODEL)),
        _const_spec((1, D_MODEL)), _const_spec((1, D_MODEL)),
        _const_spec((D_MODEL, 2 * ROUTER_LANES)), _const_spec((1, ROUTER_LANES)),
        pl.BlockSpec((epc, D_MODEL, D_EXPERT), lambda c: (c, 0, 0)),
        pl.BlockSpec((epc, D_MODEL, D_EXPERT), lambda c: (c, 0, 0)),
        pl.BlockSpec((epc, D_EXPERT, D_MODEL), lambda c: (c, 0, 0)),
    ]
    out_specs = [
        pl.BlockSpec((M1 * LANE_TILES, LANES), lambda c: (c, 0)),
        pl.BlockSpec((ROUTER_ROWS, M1), lambda c: (0, c)),
        pl.BlockSpec((epc, D_MODEL, 2 * D_EXPERT), lambda c: (c, 0, 0)),
        pl.BlockSpec((epc, D_EXPERT, D_MODEL), lambda c: (c, 0, 0)),
    ]
    return pl.pallas_call(
        _mixer_kernel,
        grid=(n_chunks,),
        in_specs=in_specs,
        out_specs=out_specs,
        out_shape=[
            jax.ShapeDtypeStruct((N_TOKENS * LANE_TILES, LANES), jnp.float32),
            jax.ShapeDtypeStruct((ROUTER_ROWS, N_TOKENS), jnp.float32),
            jax.ShapeDtypeStruct((N_EXPERTS, D_MODEL, 2 * D_EXPERT), jnp.bfloat16),
            jax.ShapeDtypeStruct((N_EXPERTS, D_EXPERT, D_MODEL), jnp.bfloat16),
        ],
        scratch_shapes=[
            pltpu.VMEM((2, TS, BATCH, D_MODEL), jnp.float32),
            pltpu.SemaphoreType.DMA((2,)),
            pltpu.VMEM((M1 + (RG_CONV - 1) * BATCH, D_RG), jnp.float32),
            pltpu.VMEM((M1 + (SC_CONV - 1) * BATCH, D_SC), jnp.float32),
            pltpu.VMEM((M1, D_RG), jnp.float32),
            pltpu.VMEM((M1, D_RG), jnp.float32),
            pltpu.VMEM((BATCH, D_RG), jnp.float32),
        ],
        compiler_params=pltpu.CompilerParams(
            dimension_semantics=("arbitrary",), vmem_limit_bytes=VMEM_LIMIT),
        name="mixer_ln_router",
    )(x, w_in, cw, cb, wa, ba, wx, bx, lam, scw, w_out, g1, b1, wr, br, wg, wu, wd)


def _expert_kernel(blk_ref, grp_ref, ea_ref, eb_ref, rows_ref,
                   xs_ref, wt_ref, wgu_ref, wd_ref, g2_ref, b2_ref, ys_hbm, obuf_ref, osem):
    s = pl.program_id(0)
    slot = s % 2
    tile_rows = [rows_ref[s * SUB + j] for j in range(SUB)]
    block_rows = SUB * TM * LANE_TILES

    def write_back(step, slot_):
        dst = ys_hbm.at[pl.ds(step * block_rows, block_rows), :]
        return pltpu.make_async_copy(obuf_ref.at[slot_], dst, osem.at[slot_])

    def did_work(step):
        return blk_ref[jnp.maximum(step, 0)] == step

    @pl.when(jnp.logical_and(s >= 2, did_work(s - 2)))
    def _():
        write_back(s - 2, slot).wait()

    out_ref = obuf_ref.at[slot]

    @pl.when(did_work(s))
    def _():
        for j in range(SUB):
            tile = s * SUB + j
            first_row = j * TM
            in_use = lax.broadcasted_iota(jnp.int32, (TM, 1), 0) < tile_rows[j]
            x = jnp.where(in_use, _load_row_tiles(xs_ref, first_row, TM), 0.0)
            xb = x.astype(jnp.bfloat16)
            wt = jnp.where(in_use, wt_ref[first_row:first_row + TM, :], 0.0)
            y = jnp.zeros((TM, D_MODEL), jnp.float32)
            for e_ref, col in ((ea_ref, 0), (eb_ref, 1)):
                e = e_ref[tile]
                hgu = jnp.dot(xb, wgu_ref[e], preferred_element_type=jnp.float32)
                hid = jax.nn.silu(hgu[:, :D_EXPERT]) * hgu[:, D_EXPERT:] * wt[:, col:col + 1]
                y = y + jnp.dot(hid.astype(jnp.bfloat16), wd_ref[e],
                                preferred_element_type=jnp.float32)
            _store_row_tiles(out_ref, first_row,
                             _layer_norm(DEEPNORM_ALPHA * x + y, g2_ref[...], b2_ref[...]))
        write_back(s, slot).start()

    @pl.when(s == N_STEPS3 - 1)
    def _():
        @pl.when(did_work(s - 1))
        def _():
            write_back(s - 1, 1 - slot).wait()

        @pl.when(did_work(s))
        def _():
            write_back(s, slot).wait()


def _expert_call(step_block, step_group, tile_ea, tile_eb, tile_rows, xs, wts, wgu, wd, g2, b2):
    rows = SUB * TM
    grid_spec = pltpu.PrefetchScalarGridSpec(
        num_scalar_prefetch=5,
        grid=(N_STEPS3,),
        in_specs=[
            pl.BlockSpec((rows * LANE_TILES, LANES), lambda s, blk, grp, ea, eb, va: (blk[s], 0)),
            pl.BlockSpec((rows, LANES), lambda s, blk, grp, ea, eb, va: (blk[s], 0)),
            pl.BlockSpec((None, EXPERTS_PER_GROUP, D_MODEL, 2 * D_EXPERT),
                         lambda s, blk, grp, ea, eb, va: (grp[s], 0, 0, 0)),
            pl.BlockSpec((None, EXPERTS_PER_GROUP, D_EXPERT, D_MODEL),
                         lambda s, blk, grp, ea, eb, va: (grp[s], 0, 0, 0)),
            pl.BlockSpec((1, D_MODEL), lambda s, blk, grp, ea, eb, va: (0, 0)),
            pl.BlockSpec((1, D_MODEL), lambda s, blk, grp, ea, eb, va: (0, 0)),
        ],
        out_specs=pl.BlockSpec(memory_space=pl.ANY),
        scratch_shapes=[
            pltpu.VMEM((2, rows * LANE_TILES, LANES), jnp.float32),
            pltpu.SemaphoreType.DMA((2,)),
        ],
    )
    return pl.pallas_call(
        _expert_kernel,
        grid_spec=grid_spec,
        out_shape=jax.ShapeDtypeStruct((P_ROWS * LANE_TILES, LANES), jnp.float32),
        compiler_params=pltpu.CompilerParams(
            dimension_semantics=("arbitrary",), vmem_limit_bytes=VMEM_LIMIT),
        name="experts_ln",
    )(step_block, step_group, tile_ea, tile_eb, tile_rows, xs, wts, wgu, wd, g2, b2)


SC_CORES = 2
SC_SUBCORES = 16
SC_WORKERS = SC_CORES * SC_SUBCORES
COMBINE_CHUNK = 32


def _sc_gather_rows(table, idx, chunk, out_row_shape=None):
    n_out, = idx.shape
    row_shape = table.shape[1:]
    out_row_shape = row_shape if out_row_shape is None else out_row_shape
    per_w = n_out // SC_WORKERS
    n_chunks = per_w // chunk
    assert per_w * SC_WORKERS == n_out and n_chunks * chunk == per_w
    assert n_chunks % 2 == 0 and chunk % 8 == 0
    mesh = plsc.VectorSubcoreMesh(core_axis_name="c", subcore_axis_name="s")

    @functools.partial(
        pl.kernel, mesh=mesh,
        out_type=jax.ShapeDtypeStruct((n_out,) + out_row_shape, table.dtype),
        scratch_types=[
            pltpu.VMEM((per_w,), jnp.int32),
            pltpu.VMEM((2, chunk) + row_shape, table.dtype),
            pltpu.SemaphoreType.DMA((2,)),
            pltpu.SemaphoreType.DMA((2,)),
        ],
    )
    def gather_kernel(table_hbm, idx_hbm, out_hbm, idx_v, buf, gsem, wsem):
        wid = lax.axis_index("s") * SC_CORES + lax.axis_index("c")
        base = wid * per_w
        pltpu.sync_copy(idx_hbm.at[pl.ds(base, per_w)], idx_v)

        def gather(j, slot):
            rows = idx_v.at[pl.ds(j * chunk, chunk)]
            return pltpu.make_async_copy(table_hbm.at[rows], buf.at[slot], gsem.at[slot])

        def write(j, slot):
            dst = out_hbm.at[pl.ds(base + j * chunk, chunk)]
            src = buf.at[slot].reshape((chunk,) + out_row_shape)
            return pltpu.make_async_copy(src, dst, wsem.at[slot])

        gather(0, 0).start()

        @pl.loop(0, n_chunks, step=2)
        def _(j0):
            for slot in range(2):
                j = j0 + slot
                gather(j, slot).wait()

                @pl.when(j >= 1)
                def _():
                    write(j - 1, 1 - slot).wait()

                @pl.when(j + 1 < n_chunks)
                def _():
                    gather(j + 1, 1 - slot).start()

                write(j, slot).start()

        write(n_chunks - 1, (n_chunks - 1) % 2).wait()

    return gather_kernel(table, idx)


SC_LANES = 16


def _sc_dispatch(rows, meta, cls, rank, class_start):
    n_rows = rows.shape[0]
    chunk = COMBINE_CHUNK
    per_w = n_rows // SC_WORKERS
    n_chunks = per_w // chunk
    assert per_w * SC_WORKERS == n_rows and n_chunks * chunk == per_w
    assert n_chunks % 2 == 0 and chunk % SC_LANES == 0
    mesh = plsc.VectorSubcoreMesh(core_axis_name="c", subcore_axis_name="s")

    @functools.partial(
        pl.kernel, mesh=mesh,
        out_type=[
            jax.ShapeDtypeStruct((P_ROWS,) + rows.shape[1:], rows.dtype),
            jax.ShapeDtypeStruct((P_ROWS,) + meta.shape[1:], meta.dtype),
            jax.ShapeDtypeStruct((n_rows,), jnp.int32),
        ],
        scratch_types=[
            pltpu.VMEM((per_w,), jnp.int32),
            pltpu.VMEM((per_w,), jnp.int32),
            pltpu.VMEM((LANES,), jnp.int32),
            pltpu.VMEM((per_w,), jnp.int32),
            pltpu.VMEM((n_chunks, chunk), jnp.int32),
            pltpu.VMEM((2, chunk) + rows.shape[1:], rows.dtype),
            pltpu.VMEM((2, chunk) + meta.shape[1:], meta.dtype),
            pltpu.SemaphoreType.DMA((2,)),
            pltpu.SemaphoreType.DMA((2,)),
            pltpu.SemaphoreType.DMA((2,)),
            pltpu.SemaphoreType.DMA((2,)),
        ],
        compiler_params=pltpu.CompilerParams(needs_layout_passes=False),
    )
    def dispatch_kernel(rows_hbm, meta_hbm, cls_hbm, rank_hbm, start_hbm,
                        xs_hbm, ws_hbm, pos_hbm,
                        cls_v, rank_v, start_v, pos_flat, pos_v, rbuf, mbuf,
                        rsem, msem, xsem, wsem):
        wid = lax.axis_index("s") * SC_CORES + lax.axis_index("c")
        base = wid * per_w
        pltpu.sync_copy(cls_hbm.at[pl.ds(base, per_w)], cls_v)
        pltpu.sync_copy(rank_hbm.at[pl.ds(base, per_w)], rank_v)
        pltpu.sync_copy(start_hbm, start_v)

        @pl.loop(0, n_chunks)
        def _(j):
            for k in range(chunk // SC_LANES):
                off = j * chunk + k * SC_LANES
                c = cls_v[pl.ds(off, SC_LANES)]
                pos = plsc.load_gather(start_v, [c]) + rank_v[pl.ds(off, SC_LANES)]
                pos_flat[pl.ds(off, SC_LANES)] = pos
                pos_v[j, pl.ds(k * SC_LANES, SC_LANES)] = pos

        pltpu.sync_copy(pos_flat, pos_hbm.at[pl.ds(base, per_w)])

        def read_rows(j, slot):
            src = rows_hbm.at[pl.ds(base + j * chunk, chunk)]
            return pltpu.make_async_copy(src, rbuf.at[slot], rsem.at[slot])

        def read_meta(j, slot):
            src = meta_hbm.at[pl.ds(base + j * chunk, chunk)]
            return pltpu.make_async_copy(src, mbuf.at[slot], msem.at[slot])

        def put_rows(j, slot):
            return pltpu.make_async_copy(rbuf.at[slot], xs_hbm.at[pos_v.at[j]], xsem.at[slot])

        def put_meta(j, slot):
            return pltpu.make_async_copy(mbuf.at[slot], ws_hbm.at[pos_v.at[j]], wsem.at[slot])

        read_rows(0, 0).start()
        read_meta(0, 0).start()

        @pl.loop(0, n_chunks, step=2)
        def _(j0):
            for slot in range(2):
                j = j0 + slot
                read_rows(j, slot).wait()
                read_meta(j, slot).wait()

                @pl.when(j >= 1)
                def _():
                    put_rows(j - 1, 1 - slot).wait()
                    put_meta(j - 1, 1 - slot).wait()

                @pl.when(j + 1 < n_chunks)
                def _():
                    read_rows(j + 1, 1 - slot).start()
                    read_meta(j + 1, 1 - slot).start()

                put_rows(j, slot).start()
                put_meta(j, slot).start()

        put_rows(n_chunks - 1, (n_chunks - 1) % 2).wait()
        put_meta(n_chunks - 1, (n_chunks - 1) % 2).wait()

    return dispatch_kernel(rows, meta, cls, rank, class_start)


def _block_diag(w):
    h, d, _ = w.shape
    eye = jnp.eye(h, dtype=w.dtype)
    return (eye[:, None, :, None] * w[:, :, None, :]).reshape(h * d, h * d)


def _pair_tables():
    pair_of = np.zeros((EXPERTS_PER_GROUP, EXPERTS_PER_GROUP), np.int32)
    lo = np.zeros((N_PAIRS,), np.int32)
    hi = np.zeros((N_PAIRS,), np.int32)
    p = 0
    for a in range(EXPERTS_PER_GROUP):
        for b in range(a + 1, EXPERTS_PER_GROUP):
            pair_of[a, b] = pair_of[b, a] = p
            lo[p], hi[p] = a, b
            p += 1
    return pair_of, lo, hi


def _route_kernel(lt_ref, info_ref, meta_ref, counts_ref, run_ref):
    step = pl.program_id(0)

    @pl.when(step == 0)
    def _():
        run_ref[...] = jnp.zeros_like(run_ref)

    f32 = jnp.float32
    sub8 = lax.broadcasted_iota(jnp.int32, (8, LANES), 0).astype(f32)
    row_id = lax.broadcasted_iota(jnp.int32, (LANES, LANES), 0)
    col_id = lax.broadcasted_iota(jnp.int32, (LANES, LANES), 1)
    prefix_mat = (row_id <= col_id).astype(jnp.bfloat16)
    ones_mat = jnp.ones((LANES, LANES), jnp.bfloat16)
    neg_inf = f32(-jnp.inf)

    def first_index_of_max(v):
        m = jnp.max(v, axis=0, keepdims=True)
        idx = jnp.min(jnp.where(v == m, sub8, f32(8)), axis=0, keepdims=True)
        return m, idx

    def lane_tile(k, run):
        lanes = pl.ds(pl.multiple_of(k * LANES, LANES), LANES)
        g = jnp.where(sub8 < N_GROUPS, lt_ref[0:GROUP_ROWS, lanes], neg_inf)
        g_max, g_idx = first_index_of_max(g)
        g_top_p = 1.0 / jnp.sum(jnp.exp(g - g_max), axis=0, keepdims=True)

        e_sel = lt_ref[GROUP_ROWS:GROUP_ROWS + EXPERTS_PER_GROUP, lanes]
        for grp in range(1, N_GROUPS):
            lo = GROUP_ROWS + grp * EXPERTS_PER_GROUP
            e_sel = jnp.where(g_idx == grp, lt_ref[lo:lo + EXPERTS_PER_GROUP, lanes], e_sel)
        m1, i1 = first_index_of_max(e_sel)
        rest = jnp.where(sub8 == i1, neg_inf, e_sel)
        m2 = jnp.max(rest, axis=0, keepdims=True)
        i2 = jnp.min(jnp.where((rest == m2) & (sub8 != i1), sub8, f32(8)), axis=0, keepdims=True)

        e = jnp.exp(m2 - m1)
        w1 = g_top_p / (1.0 + e)
        w2 = g_top_p * e / (1.0 + e)
        first_is_lo = i1 < i2
        w_lo = jnp.where(first_is_lo, w1, w2)
        w_hi = jnp.where(first_is_lo, w2, w1)
        lo_e = jnp.minimum(i1, i2)
        hi_e = jnp.maximum(i1, i2)
        pair = lo_e * (2 * EXPERTS_PER_GROUP - 1 - lo_e) * 0.5 + hi_e - lo_e - 1.0
        cls = (g_idx * N_PAIRS + pair).astype(jnp.int32)

        onehot = (row_id == cls).astype(jnp.bfloat16)
        csum = jnp.dot(onehot, prefix_mat, preferred_element_type=f32)
        rank = jnp.sum(onehot.astype(f32) * (csum + run), axis=0, keepdims=True) - 1.0
        run = run + jnp.dot(onehot, ones_mat, preferred_element_type=f32)

        info_ref[:, lanes] = jnp.where(sub8 == 0, cls, jnp.where(sub8 == 1, rank.astype(jnp.int32), 0))
        meta8 = jnp.where(sub8 == 0, w_lo, jnp.where(sub8 == 1, w_hi, 0.0))
        meta_t = jnp.concatenate([meta8, jnp.zeros((LANES - 8, LANES), f32)], axis=0)
        meta_ref[lanes, :] = meta_t.T
        return run

    run = lax.fori_loop(0, ROUTE_BLOCK // LANES, lane_tile, run_ref[...], unroll=ROUTE_UNROLL)
    run_ref[...] = run
    counts_ref[...] = run


def _route_call(lt):
    return pl.pallas_call(
        _route_kernel,
        grid=(N_TOKENS // ROUTE_BLOCK,),
        in_specs=[pl.BlockSpec((ROUTER_ROWS, ROUTE_BLOCK), lambda i: (0, i))],
        out_specs=[
            pl.BlockSpec((8, ROUTE_BLOCK), lambda i: (0, i)),
            pl.BlockSpec((ROUTE_BLOCK, LANES), lambda i: (i, 0)),
            pl.BlockSpec((LANES, LANES), lambda i: (0, 0)),
        ],
        out_shape=[
            jax.ShapeDtypeStruct((8, N_TOKENS), jnp.int32),
            jax.ShapeDtypeStruct((N_TOKENS, LANES), jnp.float32),
            jax.ShapeDtypeStruct((LANES, LANES), jnp.float32),
        ],
        scratch_shapes=[pltpu.VMEM((LANES, LANES), jnp.float32)],
        compiler_params=pltpu.CompilerParams(dimension_semantics=("arbitrary",)),
        name="route_rank",
    )(lt)


def _dispatch_plan(counts):
    _, pair_lo, pair_hi = _pair_tables()
    tiles_c = (counts + TM - 1) // TM
    tiles_g = tiles_c.reshape(N_GROUPS, N_PAIRS).sum(axis=1)
    tiles_g_pad = (tiles_g + SUB - 1) // SUB * SUB
    g_start = jnp.cumsum(tiles_g_pad) - tiles_g_pad
    tc = tiles_c.reshape(N_GROUPS, N_PAIRS)
    c_start = (g_start[:, None] + jnp.cumsum(tc, axis=1) - tc).reshape(N_CLASSES)
    class_start = jnp.zeros((LANES,), jnp.int32).at[:N_CLASSES].set(c_start * TM)

    tile_ids = jnp.arange(N_TILES, dtype=jnp.int32)
    c_end = c_start + tiles_c
    owner = jnp.sum((tile_ids[:, None] >= c_end[None, :]).astype(jnp.int32), axis=1)
    owner = jnp.minimum(owner, N_CLASSES - 1)
    in_class = (tile_ids >= c_start[owner]) & (tile_ids < c_end[owner])
    rows_left = counts[owner] - (tile_ids - c_start[owner]) * TM
    tile_rows = jnp.where(in_class, jnp.minimum(rows_left, TM), 0).astype(jnp.int32)
    pair = owner % N_PAIRS
    tile_ea = jnp.asarray(pair_lo)[pair]
    tile_eb = jnp.asarray(pair_hi)[pair]
    g_end = g_start + tiles_g_pad
    step_first = jnp.arange(N_STEPS3, dtype=jnp.int32) * SUB
    step_group = jnp.sum((step_first[:, None] >= g_end[None, :]).astype(jnp.int32), axis=1)
    step_group = jnp.minimum(step_group, N_GROUPS - 1)
    last_step = jnp.maximum(g_end[-1] // SUB - 1, 0)
    step_block = jnp.minimum(jnp.arange(N_STEPS3, dtype=jnp.int32), last_step)
    return (class_start, tile_ea, tile_eb, tile_rows, step_group.astype(jnp.int32),
            step_block.astype(jnp.int32))


def kernel(x, w_in, rg_conv_w, rg_conv_b, rg_gate_a_w, rg_gate_a_b, rg_gate_x_w, rg_gate_x_b,
           rg_lambda, sc_conv_w, w_out, ln1_g, ln1_b, router_group_w, router_group_b,
           router_expert_w, router_expert_b, exp_w_gate, exp_w_up, exp_w_down, ln2_g, ln2_b):
    bf16 = jnp.bfloat16
    row = lambda v: v.reshape(1, -1)
    pad_g = GROUP_ROWS - N_GROUPS
    pad_e = ROUTER_LANES - ROUTER_ROWS
    wr = jnp.concatenate([router_group_w, jnp.zeros((D_MODEL, pad_g), jnp.float32),
                          router_expert_w, jnp.zeros((D_MODEL, pad_e), jnp.float32)], axis=1)
    br = jnp.concatenate([router_group_b, jnp.zeros((pad_g,), jnp.float32),
                          router_expert_b, jnp.zeros((pad_e,), jnp.float32)]).reshape(1, -1)
    wr_hi = wr.astype(bf16)
    wr_lo = (wr - wr_hi.astype(jnp.float32)).astype(bf16)

    h1, lt, wgu, wdb = _mixer_call(
        x, w_in.astype(bf16), rg_conv_w, row(rg_conv_b),
        _block_diag(rg_gate_a_w).astype(bf16), row(rg_gate_a_b),
        _block_diag(rg_gate_x_w).astype(bf16), row(rg_gate_x_b),
        row(rg_lambda), sc_conv_w, w_out.astype(bf16), row(ln1_g), row(ln1_b),
        jnp.concatenate([wr_hi, wr_lo], axis=1), br, exp_w_gate, exp_w_up, exp_w_down)

    info, meta, counts = _route_call(lt)
    class_start, tile_ea, tile_eb, tile_rows, step_group, step_block = _dispatch_plan(
        counts[:N_CLASSES, 0].astype(jnp.int32))

    row_tiles = (-1, LANE_TILES, LANES)
    xs, wts, pos = _sc_dispatch(h1.reshape(row_tiles), meta, info[0], info[1], class_start)

    grp_shape = (N_GROUPS, EXPERTS_PER_GROUP)
    ys = _expert_call(
        step_block, step_group, tile_ea, tile_eb, tile_rows, xs.reshape(-1, LANES), wts,
        wgu.reshape(grp_shape + (D_MODEL, 2 * D_EXPERT)),
        wdb.reshape(grp_shape + (D_EXPERT, D_MODEL)),
        row(ln2_g), row(ln2_b))

    out_rows = pos.reshape(SEQ, BATCH).T.reshape(-1)
    out = _sc_gather_rows(ys.reshape(row_tiles), out_rows, COMBINE_CHUNK, (D_MODEL,))
    return out.reshape(BATCH, SEQ, D_MODEL)
```

```python
import functools

import jax
import jax.numpy as jnp
import numpy as np
from jax import lax
from jax.experimental import pallas as pl
from jax.experimental.pallas import tpu as pltpu
from jax.experimental.pallas import tpu_sc as plsc

D_MODEL = 1024
BATCH = 16
SEQ = 2048
D_RG = 512
D_SC = 512
RG_HEADS = 8
RG_HEAD_DIM = D_RG // RG_HEADS
RG_CONV = 4
RG_C = 8.0
SC_CONV = 3
D_IN_PROJ = 2 * D_RG + 3 * D_SC
N_GROUPS = 4
EXPERTS_PER_GROUP = 8
N_EXPERTS = N_GROUPS * EXPERTS_PER_GROUP
D_EXPERT = D_MODEL // 4
LN_EPS = 1e-5
DEEPNORM_ALPHA = 2.0 ** 0.25

N_TOKENS = BATCH * SEQ
LANES = 128
LANE_TILES = D_MODEL // LANES
MXU_WIDTH = 256
ROUTER_LANES = LANES
GROUP_ROWS = 8
ROUTER_ROWS = GROUP_ROWS + N_EXPERTS

TS = 64
M1 = TS * BATCH
DOT_ROWS = 256

ROUTE_BLOCK = 4096
ROUTE_UNROLL = 8

N_PAIRS = EXPERTS_PER_GROUP * (EXPERTS_PER_GROUP - 1) // 2
N_CLASSES = N_GROUPS * N_PAIRS
BF16_ROWS = 16
TM = -(-int(N_TOKENS / N_CLASSES * 1.09 / 2) // BF16_ROWS) * BF16_ROWS
SUB = 8
N_TILES = -(-(N_TOKENS // TM + N_CLASSES + N_GROUPS * (SUB - 1)) // SUB) * SUB
N_STEPS3 = N_TILES // SUB
P_ROWS = N_TILES * TM

VMEM_LIMIT = 56 * 1024 * 1024


def _layer_norm(z, g, b):
    mu = jnp.mean(z, axis=-1, keepdims=True)
    zc = z - mu
    var = jnp.mean(zc * zc, axis=-1, keepdims=True)
    return zc * lax.rsqrt(var + LN_EPS) * g + b


def _store_row_tiles(ref, first_row, value):
    n = value.shape[0]
    for j in range(LANE_TILES):
        ref[pl.ds(first_row * LANE_TILES + j, n, stride=LANE_TILES), :] = (
            value[:, j * LANES:(j + 1) * LANES])


def _load_row_tiles(ref, first_row, n):
    return jnp.concatenate(
        [ref[pl.ds(first_row * LANE_TILES + j, n, stride=LANE_TILES), :]
         for j in range(LANE_TILES)], axis=1)


def _mixer_kernel(x_ref, w_in_ref, cw_ref, cb_ref, wa_ref, ba_ref, wx_ref, bx_ref, lam_ref,
                  scw_ref, w_out_ref, g1_ref, b1_ref, wr_ref, br_ref, wg_ref, wu_ref, wd_ref,
                  h1_ref, logit_ref, wgu_ref, wdb_ref,
                  xbuf_ref, xsem, cbuf_ref, sbuf_ref, a_ref, u_ref, hstate_ref):
    c = pl.program_id(0)

    wgu_ref[:, :, :D_EXPERT] = wg_ref[...].astype(jnp.bfloat16)
    wgu_ref[:, :, D_EXPERT:] = wu_ref[...].astype(jnp.bfloat16)
    wdb_ref[...] = wd_ref[...].astype(jnp.bfloat16)
    slot = c % 2
    rg_pad = (RG_CONV - 1) * BATCH
    sc_pad = (SC_CONV - 1) * BATCH

    def x_copies(chunk, slot_):
        return [pltpu.make_async_copy(x_ref.at[b, pl.ds(chunk * TS, TS), :],
                                      xbuf_ref.at[slot_, :, b, :], xsem.at[slot_])
                for b in range(BATCH)]

    @pl.when(c == 0)
    def _():
        for cp in x_copies(0, 0):
            cp.start()
        cbuf_ref[0:rg_pad, :] = jnp.zeros((rg_pad, D_RG), jnp.float32)
        sbuf_ref[0:sc_pad, :] = jnp.zeros((sc_pad, D_SC), jnp.float32)
        hstate_ref[...] = jnp.zeros_like(hstate_ref)

    @pl.when(c + 1 < SEQ // TS)
    def _():
        for cp in x_copies(c + 1, 1 - slot):
            cp.start()

    for cp in x_copies(c, slot):
        cp.wait()

    xt_ref = xbuf_ref.at[slot]
    def gate(xcb, w_ref, b_ref):
        parts = [jnp.dot(xcb[:, lo:lo + MXU_WIDTH], w_ref[lo:lo + MXU_WIDTH, lo:lo + MXU_WIDTH],
                         preferred_element_type=jnp.float32)
                 for lo in range(0, D_RG, MXU_WIDTH)]
        return 0.5 + 0.5 * jnp.tanh(0.5 * (jnp.concatenate(parts, axis=1) + b_ref[...]))

    soft_lam = jax.nn.softplus(-lam_ref[...])
    gelu_chunks, y_sc_chunks = [], []
    for r0 in range(0, M1, DOT_ROWS):
        t0 = r0 // BATCH
        xk = xt_ref[t0:t0 + DOT_ROWS // BATCH].reshape(DOT_ROWS, D_MODEL).astype(jnp.bfloat16)

        def proj(lo, width):
            return jnp.dot(xk, w_in_ref[:, lo:lo + width], preferred_element_type=jnp.float32)

        cbuf_ref[rg_pad + r0:rg_pad + r0 + DOT_ROWS, :] = proj(0, D_RG)
        rg_gate = proj(D_RG, D_RG)
        xc = cb_ref[...] + cw_ref[0:1, :] * cbuf_ref[r0:r0 + DOT_ROWS, :]
        for k in range(1, RG_CONV):
            xc = xc + cw_ref[k:k + 1, :] * cbuf_ref[r0 + k * BATCH:r0 + k * BATCH + DOT_ROWS, :]
        sc_b = proj(2 * D_RG, D_SC)
        xcb = xc.astype(jnp.bfloat16)
        r = gate(xcb, wa_ref, ba_ref)
        i = gate(xcb, wx_ref, bx_ref)
        sc_c = proj(2 * D_RG + D_SC, D_SC)
        log_a = (-RG_C) * r * soft_lam
        a_ref[r0:r0 + DOT_ROWS, :] = jnp.exp(log_a)
        th = jnp.tanh(log_a)
        u_ref[r0:r0 + DOT_ROWS, :] = jnp.sqrt(-2.0 * th / (1.0 - th)) * (i * xc)
        gelu_chunks.append(jax.nn.gelu(rg_gate))

        sbuf_ref[sc_pad + r0:sc_pad + r0 + DOT_ROWS, :] = sc_c * proj(2 * D_RG + 2 * D_SC, D_SC)
        conv = scw_ref[0:1, :] * sbuf_ref[r0:r0 + DOT_ROWS, :]
        for k in range(1, SC_CONV):
            conv = conv + scw_ref[k:k + 1, :] * sbuf_ref[r0 + k * BATCH:r0 + k * BATCH + DOT_ROWS, :]
        y_sc_chunks.append((sc_b * conv).astype(jnp.bfloat16))

    cbuf_ref[0:rg_pad, :] = cbuf_ref[M1:M1 + rg_pad, :]
    sbuf_ref[0:sc_pad, :] = sbuf_ref[M1:M1 + sc_pad, :]

    h = hstate_ref[...]
    for t in range(TS):
        h = a_ref[t * BATCH:(t + 1) * BATCH, :] * h + u_ref[t * BATCH:(t + 1) * BATCH, :]
        u_ref[t * BATCH:(t + 1) * BATCH, :] = h
    hstate_ref[...] = h

    tail_rows = M1 // 2
    per_tail = tail_rows // DOT_ROWS

    def out_proj(k):
        r0 = k * tail_rows
        gelu_k = jnp.concatenate(gelu_chunks[k * per_tail:(k + 1) * per_tail], axis=0)
        y_sc_k = jnp.concatenate(y_sc_chunks[k * per_tail:(k + 1) * per_tail], axis=0)
        y_rg = (u_ref[r0:r0 + tail_rows, :] * gelu_k).astype(jnp.bfloat16)
        mix = jnp.dot(y_rg, w_out_ref[0:D_RG, :], preferred_element_type=jnp.float32)
        return mix + jnp.dot(y_sc_k, w_out_ref[D_RG:, :], preferred_element_type=jnp.float32)

    def finish(k, mix):
        r0 = k * tail_rows
        t0 = r0 // BATCH
        x_rows = xt_ref[t0:t0 + tail_rows // BATCH].reshape(tail_rows, D_MODEL)
        h1 = _layer_norm(DEEPNORM_ALPHA * x_rows + mix, g1_ref[...], b1_ref[...])
        _store_row_tiles(h1_ref, r0, h1)
        h_hi = h1.astype(jnp.bfloat16)
        h_lo = (h1 - h_hi.astype(jnp.float32)).astype(jnp.bfloat16)
        both = jnp.dot(h_hi, wr_ref[...], preferred_element_type=jnp.float32)
        logits = both[:, :ROUTER_LANES] + both[:, ROUTER_LANES:] + br_ref[...]
        logits = logits + jnp.dot(h_lo, wr_ref[:, :ROUTER_LANES],
                                  preferred_element_type=jnp.float32)
        logit_ref[:, r0:r0 + tail_rows] = logits.T[:ROUTER_ROWS, :]

    n_tails = M1 // tail_rows
    mix = out_proj(0)
    for k in range(n_tails):
        next_mix = out_proj(k + 1) if k + 1 < n_tails else None
        finish(k, mix)
        mix = next_mix


def _const_spec(shape):
    return pl.BlockSpec(shape, lambda c: (0,) * len(shape))


def _mixer_call(x, w_in, cw, cb, wa, ba, wx, bx, lam, scw, w_out, g1, b1, wr, br, wg, wu, wd):
    n_chunks = SEQ // TS
    epc = N_EXPERTS // n_chunks
    assert epc * n_chunks == N_EXPERTS
    in_specs = [
        pl.BlockSpec(memory_space=pl.ANY),
        _const_spec((D_MODEL, D_IN_PROJ)),
        _const_spec((RG_CONV, D_RG)), _const_spec((1, D_RG)),
        _const_spec((D_RG, D_RG)), _const_spec((1, D_RG)),
        _const_spec((D_RG, D_RG)), _const_spec((1, D_RG)),
        _const_spec((1, D_RG)),
        _const_spec((SC_CONV, D_SC)),
        _const_spec((D_MODEL, D_MODEL)),
        _const_spec((1, D_MODEL)), _const_spec((1, D_MODEL)),
        _const_spec((D_MODEL, 2 * ROUTER_LANES)), _const_spec((1, ROUTER_LANES)),
        pl.BlockSpec((epc, D_MODEL, D_EXPERT), lambda c: (c, 0, 0)),
        pl.BlockSpec((epc, D_MODEL, D_EXPERT), lambda c: (c, 0, 0)),
        pl.BlockSpec((epc, D_EXPERT, D_MODEL), lambda c: (c, 0, 0)),
    ]
    out_specs = [
        pl.BlockSpec((M1 * LANE_TILES, LANES), lambda c: (c, 0)),
        pl.BlockSpec((ROUTER_ROWS, M1), lambda c: (0, c)),
        pl.BlockSpec((epc, D_MODEL, 2 * D_EXPERT), lambda c: (c, 0, 0)),
        pl.BlockSpec((epc, D_EXPERT, D_MODEL), lambda c: (c, 0, 0)),
    ]
    return pl.pallas_call(
        _mixer_kernel,
        grid=(n_chunks,),
        in_specs=in_specs,
        out_specs=out_specs,
        out_shape=[
            jax.ShapeDtypeStruct((N_TOKENS * LANE_TILES, LANES), jnp.float32),
            jax.ShapeDtypeStruct((ROUTER_ROWS, N_TOKENS), jnp.float32),
            jax.ShapeDtypeStruct((N_EXPERTS, D_MODEL, 2 * D_EXPERT), jnp.bfloat16),
            jax.ShapeDtypeStruct((N_EXPERTS, D_EXPERT, D_MODEL), jnp.bfloat16),
        ],
        scratch_shapes=[
            pltpu.VMEM((2, TS, BATCH, D_MODEL), jnp.float32),
            pltpu.SemaphoreType.DMA((2,)),
            pltpu.VMEM((M1 + (RG_CONV - 1) * BATCH, D_RG), jnp.float32),
            pltpu.VMEM((M1 + (SC_CONV - 1) * BATCH, D_SC), jnp.float32),
            pltpu.VMEM((M1, D_RG), jnp.float32),
            pltpu.VMEM((M1, D_RG), jnp.float32),
            pltpu.VMEM((BATCH, D_RG), jnp.float32),
        ],
        compiler_params=pltpu.CompilerParams(
            dimension_semantics=("arbitrary",), vmem_limit_bytes=VMEM_LIMIT),
        name="mixer_ln_router",
    )(x, w_in, cw, cb, wa, ba, wx, bx, lam, scw, w_out, g1, b1, wr, br, wg, wu, wd)


def _expert_kernel(blk_ref, grp_ref, ea_ref, eb_ref, rows_ref,
                   xs_hbm, wt_ref, wgu_ref, wd_ref, g2_ref, b2_ref, ys_hbm,
                   xbuf_ref, xsem, obuf_ref, osem):
    s = pl.program_id(0)
    slot = s % 2
    tile_rows = [rows_ref[s * SUB + j] for j in range(SUB)]
    block_rows = SUB * TM * LANE_TILES

    def fetch(step, slot_):
        src = xs_hbm.at[pl.ds(step * block_rows, block_rows), :]
        return pltpu.make_async_copy(src, xbuf_ref.at[slot_], xsem.at[slot_])

    def write_back(step, slot_):
        dst = ys_hbm.at[pl.ds(step * block_rows, block_rows), :]
        return pltpu.make_async_copy(obuf_ref.at[slot_], dst, osem.at[slot_])

    def did_work(step):
        return blk_ref[jnp.clip(step, 0, N_STEPS3 - 1)] == step

    @pl.when(jnp.logical_and(s == 0, did_work(0)))
    def _():
        fetch(0, 0).start()

    @pl.when(did_work(s + 1))
    def _():
        fetch(s + 1, 1 - slot).start()

    @pl.when(did_work(s - 2))
    def _():
        write_back(s - 2, slot).wait()

    xs_ref = xbuf_ref.at[slot]
    out_ref = obuf_ref.at[slot]

    @pl.when(did_work(s))
    def _():
        fetch(s, slot).wait()
        for j in range(SUB):
            tile = s * SUB + j
            first_row = j * TM
            in_use = lax.broadcasted_iota(jnp.int32, (TM, 1), 0) < tile_rows[j]
            x = jnp.where(in_use, _load_row_tiles(xs_ref, first_row, TM), 0.0)
            xb = x.astype(jnp.bfloat16)
            wt = jnp.where(in_use, wt_ref[first_row:first_row + TM, :], 0.0)
            y = jnp.zeros((TM, D_MODEL), jnp.float32)
            for e_ref, col in ((ea_ref, 0), (eb_ref, 1)):
                e = e_ref[tile]
                hgu = jnp.dot(xb, wgu_ref[e], preferred_element_type=jnp.float32)
                hid = jax.nn.silu(hgu[:, :D_EXPERT]) * hgu[:, D_EXPERT:] * wt[:, col:col + 1]
                y = y + jnp.dot(hid.astype(jnp.bfloat16), wd_ref[e],
                                preferred_element_type=jnp.float32)
            _store_row_tiles(out_ref, first_row,
                             _layer_norm(DEEPNORM_ALPHA * x + y, g2_ref[...], b2_ref[...]))
        write_back(s, slot).start()

    @pl.when(s == N_STEPS3 - 1)
    def _():
        @pl.when(did_work(s - 1))
        def _():
            write_back(s - 1, 1 - slot).wait()

        @pl.when(did_work(s))
        def _():
            write_back(s, slot).wait()


def _expert_call(step_block, step_group, tile_ea, tile_eb, tile_rows, xs, wts, wgu, wd, g2, b2):
    rows = SUB * TM
    grid_spec = pltpu.PrefetchScalarGridSpec(
        num_scalar_prefetch=5,
        grid=(N_STEPS3,),
        in_specs=[
            pl.BlockSpec(memory_space=pl.ANY),
            pl.BlockSpec((rows, LANES), lambda s, blk, grp, ea, eb, va: (blk[s], 0)),
            pl.BlockSpec((None, EXPERTS_PER_GROUP, D_MODEL, 2 * D_EXPERT),
                         lambda s, blk, grp, ea, eb, va: (grp[s], 0, 0, 0)),
            pl.BlockSpec((None, EXPERTS_PER_GROUP, D_EXPERT, D_MODEL),
                         lambda s, blk, grp, ea, eb, va: (grp[s], 0, 0, 0)),
            pl.BlockSpec((1, D_MODEL), lambda s, blk, grp, ea, eb, va: (0, 0)),
            pl.BlockSpec((1, D_MODEL), lambda s, blk, grp, ea, eb, va: (0, 0)),
        ],
        out_specs=pl.BlockSpec(memory_space=pl.ANY),
        scratch_shapes=[
            pltpu.VMEM((2, rows * LANE_TILES, LANES), jnp.float32),
            pltpu.SemaphoreType.DMA((2,)),
            pltpu.VMEM((2, rows * LANE_TILES, LANES), jnp.float32),
            pltpu.SemaphoreType.DMA((2,)),
        ],
    )
    return pl.pallas_call(
        _expert_kernel,
        grid_spec=grid_spec,
        out_shape=jax.ShapeDtypeStruct((P_ROWS * LANE_TILES, LANES), jnp.float32),
        compiler_params=pltpu.CompilerParams(
            dimension_semantics=("arbitrary",), vmem_limit_bytes=VMEM_LIMIT),
        name="experts_ln",
    )(step_block, step_group, tile_ea, tile_eb, tile_rows, xs, wts, wgu, wd, g2, b2)


SC_CORES = 2
SC_SUBCORES = 16
SC_WORKERS = SC_CORES * SC_SUBCORES
COMBINE_CHUNK = 32


def _sc_gather_rows(table, idx, chunk, out_row_shape=None):
    n_out, = idx.shape
    row_shape = table.shape[1:]
    out_row_shape = row_shape if out_row_shape is None else out_row_shape
    per_w = n_out // SC_WORKERS
    n_chunks = per_w // chunk
    assert per_w * SC_WORKERS == n_out and n_chunks * chunk == per_w
    assert n_chunks % 2 == 0 and chunk % 8 == 0
    mesh = plsc.VectorSubcoreMesh(core_axis_name="c", subcore_axis_name="s")

    @functools.partial(
        pl.kernel, mesh=mesh,
        out_type=jax.ShapeDtypeStruct((n_out,) + out_row_shape, table.dtype),
        scratch_types=[
            pltpu.VMEM((per_w,), jnp.int32),
            pltpu.VMEM((2, chunk) + row_shape, table.dtype),
            pltpu.SemaphoreType.DMA((2,)),
            pltpu.SemaphoreType.DMA((2,)),
        ],
    )
    def gather_kernel(table_hbm, idx_hbm, out_hbm, idx_v, buf, gsem, wsem):
        wid = lax.axis_index("s") * SC_CORES + lax.axis_index("c")
        base = wid * per_w
        pltpu.sync_copy(idx_hbm.at[pl.ds(base, per_w)], idx_v)

        def gather(j, slot):
            rows = idx_v.at[pl.ds(j * chunk, chunk)]
            return pltpu.make_async_copy(table_hbm.at[rows], buf.at[slot], gsem.at[slot])

        def write(j, slot):
            dst = out_hbm.at[pl.ds(base + j * chunk, chunk)]
            src = buf.at[slot].reshape((chunk,) + out_row_shape)
            return pltpu.make_async_copy(src, dst, wsem.at[slot])

        gather(0, 0).start()

        @pl.loop(0, n_chunks, step=2)
        def _(j0):
            for slot in range(2):
                j = j0 + slot
                gather(j, slot).wait()

                @pl.when(j >= 1)
                def _():
                    write(j - 1, 1 - slot).wait()

                @pl.when(j + 1 < n_chunks)
                def _():
                    gather(j + 1, 1 - slot).start()

                write(j, slot).start()

        write(n_chunks - 1, (n_chunks - 1) % 2).wait()

    return gather_kernel(table, idx)


SC_LANES = 16


def _sc_dispatch(rows, meta, cls, rank, class_start):
    n_rows = rows.shape[0]
    chunk = COMBINE_CHUNK
    per_w = n_rows // SC_WORKERS
    n_chunks = per_w // chunk
    assert per_w * SC_WORKERS == n_rows and n_chunks * chunk == per_w
    assert n_chunks % 2 == 0 and chunk % SC_LANES == 0
    mesh = plsc.VectorSubcoreMesh(core_axis_name="c", subcore_axis_name="s")

    @functools.partial(
        pl.kernel, mesh=mesh,
        out_type=[
            jax.ShapeDtypeStruct((P_ROWS,) + rows.shape[1:], rows.dtype),
            jax.ShapeDtypeStruct((P_ROWS,) + meta.shape[1:], meta.dtype),
            jax.ShapeDtypeStruct((n_rows,), jnp.int32),
        ],
        scratch_types=[
            pltpu.VMEM((per_w,), jnp.int32),
            pltpu.VMEM((per_w,), jnp.int32),
            pltpu.VMEM((LANES,), jnp.int32),
            pltpu.VMEM((per_w,), jnp.int32),
            pltpu.VMEM((n_chunks, chunk), jnp.int32),
            pltpu.VMEM((2, chunk) + rows.shape[1:], rows.dtype),
            pltpu.VMEM((2, chunk) + meta.shape[1:], meta.dtype),
            pltpu.SemaphoreType.DMA((2,)),
            pltpu.SemaphoreType.DMA((2,)),
            pltpu.SemaphoreType.DMA((2,)),
            pltpu.SemaphoreType.DMA((2,)),
        ],
        compiler_params=pltpu.CompilerParams(needs_layout_passes=False),
    )
    def dispatch_kernel(rows_hbm, meta_hbm, cls_hbm, rank_hbm, start_hbm,
                        xs_hbm, ws_hbm, pos_hbm,
                        cls_v, rank_v, start_v, pos_flat, pos_v, rbuf, mbuf,
                        rsem, msem, xsem, wsem):
        wid = lax.axis_index("s") * SC_CORES + lax.axis_index("c")
        base = wid * per_w
        pltpu.sync_copy(cls_hbm.at[pl.ds(base, per_w)], cls_v)
        pltpu.sync_copy(rank_hbm.at[pl.ds(base, per_w)], rank_v)
        pltpu.sync_copy(start_hbm, start_v)

        @pl.loop(0, n_chunks)
        def _(j):
            for k in range(chunk // SC_LANES):
                off = j * chunk + k * SC_LANES
                c = cls_v[pl.ds(off, SC_LANES)]
                pos = plsc.load_gather(start_v, [c]) + rank_v[pl.ds(off, SC_LANES)]
                pos_flat[pl.ds(off, SC_LANES)] = pos
                pos_v[j, pl.ds(k * SC_LANES, SC_LANES)] = pos

        pltpu.sync_copy(pos_flat, pos_hbm.at[pl.ds(base, per_w)])

        def read_rows(j, slot):
            src = rows_hbm.at[pl.ds(base + j * chunk, chunk)]
            return pltpu.make_async_copy(src, rbuf.at[slot], rsem.at[slot])

        def read_meta(j, slot):
            src = meta_hbm.at[pl.ds(base + j * chunk, chunk)]
            return pltpu.make_async_copy(src, mbuf.at[slot], msem.at[slot])

        def put_rows(j, slot):
            return pltpu.make_async_copy(rbuf.at[slot], xs_hbm.at[pos_v.at[j]], xsem.at[slot])

        def put_meta(j, slot):
            return pltpu.make_async_copy(mbuf.at[slot], ws_hbm.at[pos_v.at[j]], wsem.at[slot])

        read_rows(0, 0).start()
        read_meta(0, 0).start()

        @pl.loop(0, n_chunks, step=2)
        def _(j0):
            for slot in range(2):
                j = j0 + slot
                read_rows(j, slot).wait()
                read_meta(j, slot).wait()

                @pl.when(j >= 1)
                def _():
                    put_rows(j - 1, 1 - slot).wait()
                    put_meta(j - 1, 1 - slot).wait()

                @pl.when(j + 1 < n_chunks)
                def _():
                    read_rows(j + 1, 1 - slot).start()
                    read_meta(j + 1, 1 - slot).start()

                put_rows(j, slot).start()
                put_meta(j, slot).start()

        put_rows(n_chunks - 1, (n_chunks - 1) % 2).wait()
        put_meta(n_chunks - 1, (n_chunks - 1) % 2).wait()

    return dispatch_kernel(rows, meta, cls, rank, class_start)


def _block_diag(w):
    h, d, _ = w.shape
    eye = jnp.eye(h, dtype=w.dtype)
    return (eye[:, None, :, None] * w[:, :, None, :]).reshape(h * d, h * d)


def _pair_tables():
    pair_of = np.zeros((EXPERTS_PER_GROUP, EXPERTS_PER_GROUP), np.int32)
    lo = np.zeros((N_PAIRS,), np.int32)
    hi = np.zeros((N_PAIRS,), np.int32)
    p = 0
    for a in range(EXPERTS_PER_GROUP):
        for b in range(a + 1, EXPERTS_PER_GROUP):
            pair_of[a, b] = pair_of[b, a] = p
            lo[p], hi[p] = a, b
            p += 1
    return pair_of, lo, hi


def _route_kernel(lt_ref, info_ref, meta_ref, counts_ref, run_ref):
    step = pl.program_id(0)

    @pl.when(step == 0)
    def _():
        run_ref[...] = jnp.zeros_like(run_ref)

    f32 = jnp.float32
    sub8 = lax.broadcasted_iota(jnp.int32, (8, LANES), 0).astype(f32)
    row_id = lax.broadcasted_iota(jnp.int32, (LANES, LANES), 0)
    col_id = lax.broadcasted_iota(jnp.int32, (LANES, LANES), 1)
    prefix_mat = (row_id <= col_id).astype(jnp.bfloat16)
    ones_mat = jnp.ones((LANES, LANES), jnp.bfloat16)
    neg_inf = f32(-jnp.inf)

    def first_index_of_max(v):
        m = jnp.max(v, axis=0, keepdims=True)
        idx = jnp.min(jnp.where(v == m, sub8, f32(8)), axis=0, keepdims=True)
        return m, idx

    def lane_tile(k, run):
        lanes = pl.ds(pl.multiple_of(k * LANES, LANES), LANES)
        g = jnp.where(sub8 < N_GROUPS, lt_ref[0:GROUP_ROWS, lanes], neg_inf)
        g_max, g_idx = first_index_of_max(g)
        g_top_p = 1.0 / jnp.sum(jnp.exp(g - g_max), axis=0, keepdims=True)

        e_sel = lt_ref[GROUP_ROWS:GROUP_ROWS + EXPERTS_PER_GROUP, lanes]
        for grp in range(1, N_GROUPS):
            lo = GROUP_ROWS + grp * EXPERTS_PER_GROUP
            e_sel = jnp.where(g_idx == grp, lt_ref[lo:lo + EXPERTS_PER_GROUP, lanes], e_sel)
        m1, i1 = first_index_of_max(e_sel)
        rest = jnp.where(sub8 == i1, neg_inf, e_sel)
        m2 = jnp.max(rest, axis=0, keepdims=True)
        i2 = jnp.min(jnp.where((rest == m2) & (sub8 != i1), sub8, f32(8)), axis=0, keepdims=True)

        e = jnp.exp(m2 - m1)
        w1 = g_top_p / (1.0 + e)
        w2 = g_top_p * e / (1.0 + e)
        first_is_lo = i1 < i2
        w_lo = jnp.where(first_is_lo, w1, w2)
        w_hi = jnp.where(first_is_lo, w2, w1)
        lo_e = jnp.minimum(i1, i2)
        hi_e = jnp.maximum(i1, i2)
        pair = lo_e * (2 * EXPERTS_PER_GROUP - 1 - lo_e) * 0.5 + hi_e - lo_e - 1.0
        cls = (g_idx * N_PAIRS + pair).astype(jnp.int32)

        onehot = (row_id == cls).astype(jnp.bfloat16)
        csum = jnp.dot(onehot, prefix_mat, preferred_element_type=f32)
        rank = jnp.sum(onehot.astype(f32) * (csum + run), axis=0, keepdims=True) - 1.0
        run = run + jnp.dot(onehot, ones_mat, preferred_element_type=f32)

        info_ref[:, lanes] = jnp.where(sub8 == 0, cls, jnp.where(sub8 == 1, rank.astype(jnp.int32), 0))
        meta8 = jnp.where(sub8 == 0, w_lo, jnp.where(sub8 == 1, w_hi, 0.0))
        meta_t = jnp.concatenate([meta8, jnp.zeros((LANES - 8, LANES), f32)], axis=0)
        meta_ref[lanes, :] = meta_t.T
        return run

    run = lax.fori_loop(0, ROUTE_BLOCK // LANES, lane_tile, run_ref[...], unroll=ROUTE_UNROLL)
    run_ref[...] = run
    counts_ref[...] = run


def _route_call(lt):
    return pl.pallas_call(
        _route_kernel,
        grid=(N_TOKENS // ROUTE_BLOCK,),
        in_specs=[pl.BlockSpec((ROUTER_ROWS, ROUTE_BLOCK), lambda i: (0, i))],
        out_specs=[
            pl.BlockSpec((8, ROUTE_BLOCK), lambda i: (0, i)),
            pl.BlockSpec((ROUTE_BLOCK, LANES), lambda i: (i, 0)),
            pl.BlockSpec((LANES, LANES), lambda i: (0, 0)),
        ],
        out_shape=[
            jax.ShapeDtypeStruct((8, N_TOKENS), jnp.int32),
            jax.ShapeDtypeStruct((N_TOKENS, LANES), jnp.float32),
            jax.ShapeDtypeStruct((LANES, LANES), jnp.float32),
        ],
        scratch_shapes=[pltpu.VMEM((LANES, LANES), jnp.float32)],
        compiler_params=pltpu.CompilerParams(dimension_semantics=("arbitrary",)),
        name="route_rank",
    )(lt)


def _dispatch_plan(counts):
    _, pair_lo, pair_hi = _pair_tables()
    tiles_c = (counts + TM - 1) // TM
    tiles_g = tiles_c.reshape(N_GROUPS, N_PAIRS).sum(axis=1)
    tiles_g_pad = (tiles_g + SUB - 1) // SUB * SUB
    g_start = jnp.cumsum(tiles_g_pad) - tiles_g_pad
    tc = tiles_c.reshape(N_GROUPS, N_PAIRS)
    c_start = (g_start[:, None] + jnp.cumsum(tc, axis=1) - tc).reshape(N_CLASSES)
    class_start = jnp.zeros((LANES,), jnp.int32).at[:N_CLASSES].set(c_start * TM)

    tile_ids = jnp.arange(N_TILES, dtype=jnp.int32)
    c_end = c_start + tiles_c
    owner = jnp.sum((tile_ids[:, None] >= c_end[None, :]).astype(jnp.int32), axis=1)
    owner = jnp.minimum(owner, N_CLASSES - 1)
    in_class = (tile_ids >= c_start[owner]) & (tile_ids < c_end[owner])
    rows_left = counts[owner] - (tile_ids - c_start[owner]) * TM
    tile_rows = jnp.where(in_class, jnp.minimum(rows_left, TM), 0).astype(jnp.int32)
    pair = owner % N_PAIRS
    tile_ea = jnp.asarray(pair_lo)[pair]
    tile_eb = jnp.asarray(pair_hi)[pair]
    g_end = g_start + tiles_g_pad
    step_first = jnp.arange(N_STEPS3, dtype=jnp.int32) * SUB
    step_group = jnp.sum((step_first[:, None] >= g_end[None, :]).astype(jnp.int32), axis=1)
    step_group = jnp.minimum(step_group, N_GROUPS - 1)
    last_step = jnp.maximum(g_end[-1] // SUB - 1, 0)
    step_block = jnp.minimum(jnp.arange(N_STEPS3, dtype=jnp.int32), last_step)
    return (class_start, tile_ea, tile_eb, tile_rows, step_group.astype(jnp.int32),
            step_block.astype(jnp.int32))


def kernel(x, w_in, rg_conv_w, rg_conv_b, rg_gate_a_w, rg_gate_a_b, rg_gate_x_w, rg_gate_x_b,
           rg_lambda, sc_conv_w, w_out, ln1_g, ln1_b, router_group_w, router_group_b,
           router_expert_w, router_expert_b, exp_w_gate, exp_w_up, exp_w_down, ln2_g, ln2_b):
    bf16 = jnp.bfloat16
    row = lambda v: v.reshape(1, -1)
    pad_g = GROUP_ROWS - N_GROUPS
    pad_e = ROUTER_LANES - ROUTER_ROWS
    wr = jnp.concatenate([router_group_w, jnp.zeros((D_MODEL, pad_g), jnp.float32),
                          router_expert_w, jnp.zeros((D_MODEL, pad_e), jnp.float32)], axis=1)
    br = jnp.concatenate([router_group_b, jnp.zeros((pad_g,), jnp.float32),
                          router_expert_b, jnp.zeros((pad_e,), jnp.float32)]).reshape(1, -1)
    wr_hi = wr.astype(bf16)
    wr_lo = (wr - wr_hi.astype(jnp.float32)).astype(bf16)

    h1, lt, wgu, wdb = _mixer_call(
        x, w_in.astype(bf16), rg_conv_w, row(rg_conv_b),
        _block_diag(rg_gate_a_w).astype(bf16), row(rg_gate_a_b),
        _block_diag(rg_gate_x_w).astype(bf16), row(rg_gate_x_b),
        row(rg_lambda), sc_conv_w, w_out.astype(bf16), row(ln1_g), row(ln1_b),
        jnp.concatenate([wr_hi, wr_lo], axis=1), br, exp_w_gate, exp_w_up, exp_w_down)

    info, meta, counts = _route_call(lt)
    class_start, tile_ea, tile_eb, tile_rows, step_group, step_block = _dispatch_plan(
        counts[:N_CLASSES, 0].astype(jnp.int32))

    row_tiles = (-1, LANE_TILES, LANES)
    xs, wts, pos = _sc_dispatch(h1.reshape(row_tiles), meta, info[0], info[1], class_start)

    grp_shape = (N_GROUPS, EXPERTS_PER_GROUP)
    ys = _expert_call(
        step_block, step_group, tile_ea, tile_eb, tile_rows, xs.reshape(-1, LANES), wts,
        wgu.reshape(grp_shape + (D_MODEL, 2 * D_EXPERT)),
        wdb.reshape(grp_shape + (D_EXPERT, D_MODEL)),
        row(ln2_g), row(ln2_b))

    out_rows = pos.reshape(SEQ, BATCH).T.reshape(-1)
    out = _sc_gather_rows(ys.reshape(row_tiles), out_rows, COMBINE_CHUNK, (D_MODEL,))
    return out.reshape(BATCH, SEQ, D_MODEL)
```

```python
import functools

import jax
import jax.numpy as jnp
import numpy as np
from jax import lax
from jax.experimental import pallas as pl
from jax.experimental.pallas import tpu as pltpu
from jax.experimental.pallas import tpu_sc as plsc

D_MODEL = 1024
BATCH = 16
SEQ = 2048
D_RG = 512
D_SC = 512
RG_HEADS = 8
RG_HEAD_DIM = D_RG // RG_HEADS
RG_CONV = 4
RG_C = 8.0
SC_CONV = 3
D_IN_PROJ = 2 * D_RG + 3 * D_SC
N_GROUPS = 4
EXPERTS_PER_GROUP = 8
N_EXPERTS = N_GROUPS * EXPERTS_PER_GROUP
D_EXPERT = D_MODEL // 4
LN_EPS = 1e-5
DEEPNORM_ALPHA = 2.0 ** 0.25

N_TOKENS = BATCH * SEQ
LANES = 128
LANE_TILES = D_MODEL // LANES
MXU_WIDTH = 256
ROUTER_LANES = LANES
GROUP_ROWS = 8
ROUTER_ROWS = GROUP_ROWS + N_EXPERTS

TS = 64
M1 = TS * BATCH
DOT_ROWS = 256

ROUTE_BLOCK = 4096
ROUTE_UNROLL = 8

N_PAIRS = EXPERTS_PER_GROUP * (EXPERTS_PER_GROUP - 1) // 2
N_CLASSES = N_GROUPS * N_PAIRS
BF16_ROWS = 16
TM = -(-int(N_TOKENS / N_CLASSES * 1.09 / 2) // BF16_ROWS) * BF16_ROWS
SUB = 8
N_TILES = -(-(N_TOKENS // TM + N_CLASSES + N_GROUPS * (SUB - 1)) // SUB) * SUB
N_STEPS3 = N_TILES // SUB
P_ROWS = N_TILES * TM

VMEM_LIMIT = 56 * 1024 * 1024


def _layer_norm(z, g, b):
    mu = jnp.mean(z, axis=-1, keepdims=True)
    zc = z - mu
    var = jnp.mean(zc * zc, axis=-1, keepdims=True)
    return zc * lax.rsqrt(var + LN_EPS) * g + b


def _store_row_tiles(ref, first_row, value):
    n = value.shape[0]
    for j in range(LANE_TILES):
        ref[pl.ds(first_row * LANE_TILES + j, n, stride=LANE_TILES), :] = (
            value[:, j * LANES:(j + 1) * LANES])


def _load_row_tiles(ref, first_row, n):
    return jnp.concatenate(
        [ref[pl.ds(first_row * LANE_TILES + j, n, stride=LANE_TILES), :]
         for j in range(LANE_TILES)], axis=1)


def _mixer_kernel(x_ref, w_in_ref, cw_ref, cb_ref, wa_ref, ba_ref, wx_ref, bx_ref, lam_ref,
                  scw_ref, w_out_ref, g1_ref, b1_ref, wr_ref, br_ref, wg_ref, wu_ref, wd_ref,
                  h1_ref, logit_ref, wgu_ref, wdb_ref,
                  xbuf_ref, xsem, cbuf_ref, sbuf_ref, a_ref, u_ref, hstate_ref):
    c = pl.program_id(0)

    wgu_ref[:, :, :D_EXPERT] = wg_ref[...].astype(jnp.bfloat16)
    wgu_ref[:, :, D_EXPERT:] = wu_ref[...].astype(jnp.bfloat16)
    wdb_ref[...] = wd_ref[...].astype(jnp.bfloat16)
    slot = c % 2
    rg_pad = (RG_CONV - 1) * BATCH
    sc_pad = (SC_CONV - 1) * BATCH

    def x_copies(chunk, slot_):
        return [pltpu.make_async_copy(x_ref.at[b, pl.ds(chunk * TS, TS), :],
                                      xbuf_ref.at[slot_, :, b, :], xsem.at[slot_])
                for b in range(BATCH)]

    @pl.when(c == 0)
    def _():
        for cp in x_copies(0, 0):
            cp.start()
        cbuf_ref[0:rg_pad, :] = jnp.zeros((rg_pad, D_RG), jnp.float32)
        sbuf_ref[0:sc_pad, :] = jnp.zeros((sc_pad, D_SC), jnp.float32)
        hstate_ref[...] = jnp.zeros_like(hstate_ref)

    @pl.when(c + 1 < SEQ // TS)
    def _():
        for cp in x_copies(c + 1, 1 - slot):
            cp.start()

    for cp in x_copies(c, slot):
        cp.wait()

    xt_ref = xbuf_ref.at[slot]

    def gate(xcb, w_ref, b_ref):
        parts = [jnp.dot(xcb[:, lo:lo + MXU_WIDTH], w_ref[lo:lo + MXU_WIDTH, lo:lo + MXU_WIDTH],
                         preferred_element_type=jnp.float32)
                 for lo in range(0, D_RG, MXU_WIDTH)]
        return 0.5 + 0.5 * jnp.tanh(0.5 * (jnp.concatenate(parts, axis=1) + b_ref[...]))

    soft_lam = jax.nn.softplus(-lam_ref[...])
    gelu_chunks, y_sc_chunks = [], []
    for r0 in range(0, M1, DOT_ROWS):
        t0 = r0 // BATCH
        xk = xt_ref[t0:t0 + DOT_ROWS // BATCH].reshape(DOT_ROWS, D_MODEL).astype(jnp.bfloat16)

        def proj(lo, width):
            return jnp.dot(xk, w_in_ref[:, lo:lo + width], preferred_element_type=jnp.float32)

        cbuf_ref[rg_pad + r0:rg_pad + r0 + DOT_ROWS, :] = proj(0, D_RG)
        rg_gate = proj(D_RG, D_RG)
        xc = cb_ref[...] + cw_ref[0:1, :] * cbuf_ref[r0:r0 + DOT_ROWS, :]
        for k in range(1, RG_CONV):
            xc = xc + cw_ref[k:k + 1, :] * cbuf_ref[r0 + k * BATCH:r0 + k * BATCH + DOT_ROWS, :]
        sc_b = proj(2 * D_RG, D_SC)
        xcb = xc.astype(jnp.bfloat16)
        r = gate(xcb, wa_ref, ba_ref)
        i = gate(xcb, wx_ref, bx_ref)
        sc_c = proj(2 * D_RG + D_SC, D_SC)
        log_a = (-RG_C) * r * soft_lam
        a_ref[r0:r0 + DOT_ROWS, :] = jnp.exp(log_a)
        th = jnp.tanh(log_a)
        u_ref[r0:r0 + DOT_ROWS, :] = jnp.sqrt(-2.0 * th / (1.0 - th)) * (i * xc)
        gelu_chunks.append(jax.nn.gelu(rg_gate))

        sbuf_ref[sc_pad + r0:sc_pad + r0 + DOT_ROWS, :] = sc_c * proj(2 * D_RG + 2 * D_SC, D_SC)
        conv = scw_ref[0:1, :] * sbuf_ref[r0:r0 + DOT_ROWS, :]
        for k in range(1, SC_CONV):
            conv = conv + scw_ref[k:k + 1, :] * sbuf_ref[r0 + k * BATCH:r0 + k * BATCH + DOT_ROWS, :]
        y_sc_chunks.append((sc_b * conv).astype(jnp.bfloat16))

    cbuf_ref[0:rg_pad, :] = cbuf_ref[M1:M1 + rg_pad, :]
    sbuf_ref[0:sc_pad, :] = sbuf_ref[M1:M1 + sc_pad, :]

    h = hstate_ref[...]
    for t in range(TS):
        h = a_ref[t * BATCH:(t + 1) * BATCH, :] * h + u_ref[t * BATCH:(t + 1) * BATCH, :]
        u_ref[t * BATCH:(t + 1) * BATCH, :] = h
    hstate_ref[...] = h

    tail_rows = M1 // 2
    per_tail = tail_rows // DOT_ROWS

    def out_proj(k):
        r0 = k * tail_rows
        gelu_k = jnp.concatenate(gelu_chunks[k * per_tail:(k + 1) * per_tail], axis=0)
        y_sc_k = jnp.concatenate(y_sc_chunks[k * per_tail:(k + 1) * per_tail], axis=0)
        y_rg = (u_ref[r0:r0 + tail_rows, :] * gelu_k).astype(jnp.bfloat16)
        mix = jnp.dot(y_rg, w_out_ref[0:D_RG, :], preferred_element_type=jnp.float32)
        return mix + jnp.dot(y_sc_k, w_out_ref[D_RG:, :], preferred_element_type=jnp.float32)

    def finish(k, mix):
        r0 = k * tail_rows
        t0 = r0 // BATCH
        x_rows = xt_ref[t0:t0 + tail_rows // BATCH].reshape(tail_rows, D_MODEL)
        h1 = _layer_norm(DEEPNORM_ALPHA * x_rows + mix, g1_ref[...], b1_ref[...])
        _store_row_tiles(h1_ref, r0, h1)
        h_hi = h1.astype(jnp.bfloat16)
        h_lo = (h1 - h_hi.astype(jnp.float32)).astype(jnp.bfloat16)
        both = jnp.dot(h_hi, wr_ref[...], preferred_element_type=jnp.float32)
        logits = both[:, :ROUTER_LANES] + both[:, ROUTER_LANES:] + br_ref[...]
        logits = logits + jnp.dot(h_lo, wr_ref[:, :ROUTER_LANES],
                                  preferred_element_type=jnp.float32)
        logit_ref[:, r0:r0 + tail_rows] = logits.T[:ROUTER_ROWS, :]

    n_tails = M1 // tail_rows
    mix = out_proj(0)
    for k in range(n_tails):
        next_mix = out_proj(k + 1) if k + 1 < n_tails else None
        finish(k, mix)
        mix = next_mix


def _const_spec(shape):
    return pl.BlockSpec(shape, lambda c: (0,) * len(shape))


def _mixer_call(x, w_in, cw, cb, wa, ba, wx, bx, lam, scw, w_out, g1, b1, wr, br, wg, wu, wd):
    n_chunks = SEQ // TS
    epc = N_EXPERTS // n_chunks
    assert epc * n_chunks == N_EXPERTS
    in_specs = [
        pl.BlockSpec(memory_space=pl.ANY),
        _const_spec((D_MODEL, D_IN_PROJ)),
        _const_spec((RG_CONV, D_RG)), _const_spec((1, D_RG)),
        _const_spec((D_RG, D_RG)), _const_spec((1, D_RG)),
        _const_spec((D_RG, D_RG)), _const_spec((1, D_RG)),
        _const_spec((1, D_RG)),
        _const_spec((SC_CONV, D_SC)),
        _const_spec((D_MODEL, D_MODEL)),
        _const_spec((1, D_MODEL)), _const_spec((1, D_MODEL)),
        _const_spec((D_MODEL, 2 * ROUTER_LANES)), _const_spec((1, ROUTER_LANES)),
        pl.BlockSpec((epc, D_MODEL, D_EXPERT), lambda c: (c, 0, 0)),
        pl.BlockSpec((epc, D_MODEL, D_EXPERT), lambda c: (c, 0, 0)),
        pl.BlockSpec((epc, D_EXPERT, D_MODEL), lambda c: (c, 0, 0)),
    ]
    out_specs = [
        pl.BlockSpec((M1 * LANE_TILES, LANES), lambda c: (c, 0)),
        pl.BlockSpec((ROUTER_ROWS, M1), lambda c: (0, c)),
        pl.BlockSpec((epc, D_MODEL, 2 * D_EXPERT), lambda c: (c, 0, 0)),
        pl.BlockSpec((epc, D_EXPERT, D_MODEL), lambda c: (c, 0, 0)),
    ]
    return pl.pallas_call(
        _mixer_kernel,
        grid=(n_chunks,),
        in_specs=in_specs,
        out_specs=out_specs,
        out_shape=[
            jax.ShapeDtypeStruct((N_TOKENS * LANE_TILES, LANES), jnp.float32),
            jax.ShapeDtypeStruct((ROUTER_ROWS, N_TOKENS), jnp.float32),
            jax.ShapeDtypeStruct((N_EXPERTS, D_MODEL, 2 * D_EXPERT), jnp.bfloat16),
            jax.ShapeDtypeStruct((N_EXPERTS, D_EXPERT, D_MODEL), jnp.bfloat16),
        ],
        scratch_shapes=[
            pltpu.VMEM((2, TS, BATCH, D_MODEL), jnp.float32),
            pltpu.SemaphoreType.DMA((2,)),
            pltpu.VMEM((M1 + (RG_CONV - 1) * BATCH, D_RG), jnp.float32),
            pltpu.VMEM((M1 + (SC_CONV - 1) * BATCH, D_SC), jnp.float32),
            pltpu.VMEM((M1, D_RG), jnp.float32),
            pltpu.VMEM((M1, D_RG), jnp.float32),
            pltpu.VMEM((BATCH, D_RG), jnp.float32),
        ],
        compiler_params=pltpu.CompilerParams(
            dimension_semantics=("arbitrary",), vmem_limit_bytes=VMEM_LIMIT),
        name="mixer_ln_router",
    )(x, w_in, cw, cb, wa, ba, wx, bx, lam, scw, w_out, g1, b1, wr, br, wg, wu, wd)


def _expert_kernel(blk_ref, grp_ref, ea_ref, eb_ref, rows_ref,
                   xs_ref, wt_ref, wgu_ref, wd_ref, g2_ref, b2_ref, ys_hbm, obuf_ref, osem):
    s = pl.program_id(0)
    slot = s % 2
    tile_rows = [rows_ref[s * SUB + j] for j in range(SUB)]
    block_rows = SUB * TM * LANE_TILES

    def write_back(step, slot_):
        dst = ys_hbm.at[pl.ds(step * block_rows, block_rows), :]
        return pltpu.make_async_copy(obuf_ref.at[slot_], dst, osem.at[slot_])

    def did_work(step):
        return blk_ref[jnp.maximum(step, 0)] == step

    @pl.when(jnp.logical_and(s >= 2, did_work(s - 2)))
    def _():
        write_back(s - 2, slot).wait()

    out_ref = obuf_ref.at[slot]

    @pl.when(did_work(s))
    def _():
        for j in range(SUB):
            tile = s * SUB + j
            first_row = j * TM
            in_use = lax.broadcasted_iota(jnp.int32, (TM, 1), 0) < tile_rows[j]
            x = jnp.where(in_use, _load_row_tiles(xs_ref, first_row, TM), 0.0)
            xb = x.astype(jnp.bfloat16)
            wt = jnp.where(in_use, wt_ref[first_row:first_row + TM, :], 0.0)
            y = jnp.zeros((TM, D_MODEL), jnp.float32)
            for e_ref, col in ((ea_ref, 0), (eb_ref, 1)):
                e = e_ref[tile]
                hgu = jnp.dot(xb, wgu_ref[e], preferred_element_type=jnp.float32)
                hid = jax.nn.silu(hgu[:, :D_EXPERT]) * hgu[:, D_EXPERT:] * wt[:, col:col + 1]
                y = y + jnp.dot(hid.astype(jnp.bfloat16), wd_ref[e],
                                preferred_element_type=jnp.float32)
            _store_row_tiles(out_ref, first_row,
                             _layer_norm(DEEPNORM_ALPHA * x + y, g2_ref[...], b2_ref[...]))
        write_back(s, slot).start()

    @pl.when(s == N_STEPS3 - 1)
    def _():
        @pl.when(did_work(s - 1))
        def _():
            write_back(s - 1, 1 - slot).wait()

        @pl.when(did_work(s))
        def _():
            write_back(s, slot).wait()


def _expert_call(step_block, step_group, tile_ea, tile_eb, tile_rows, xs, wts, wgu, wd, g2, b2):
    rows = SUB * TM
    grid_spec = pltpu.PrefetchScalarGridSpec(
        num_scalar_prefetch=5,
        grid=(N_STEPS3,),
        in_specs=[
            pl.BlockSpec((rows * LANE_TILES, LANES), lambda s, blk, grp, ea, eb, va: (blk[s], 0)),
            pl.BlockSpec((rows, LANES), lambda s, blk, grp, ea, eb, va: (blk[s], 0)),
            pl.BlockSpec((None, EXPERTS_PER_GROUP, D_MODEL, 2 * D_EXPERT),
                         lambda s, blk, grp, ea, eb, va: (grp[s], 0, 0, 0)),
            pl.BlockSpec((None, EXPERTS_PER_GROUP, D_EXPERT, D_MODEL),
                         lambda s, blk, grp, ea, eb, va: (grp[s], 0, 0, 0)),
            pl.BlockSpec((1, D_MODEL), lambda s, blk, grp, ea, eb, va: (0, 0)),
            pl.BlockSpec((1, D_MODEL), lambda s, blk, grp, ea, eb, va: (0, 0)),
        ],
        out_specs=pl.BlockSpec(memory_space=pl.ANY),
        scratch_shapes=[
            pltpu.VMEM((2, rows * LANE_TILES, LANES), jnp.float32),
            pltpu.SemaphoreType.DMA((2,)),
        ],
    )
    return pl.pallas_call(
        _expert_kernel,
        grid_spec=grid_spec,
        out_shape=jax.ShapeDtypeStruct((P_ROWS * LANE_TILES, LANES), jnp.float32),
        compiler_params=pltpu.CompilerParams(
            dimension_semantics=("arbitrary",), vmem_limit_bytes=VMEM_LIMIT),
        name="experts_ln",
    )(step_block, step_group, tile_ea, tile_eb, tile_rows, xs, wts, wgu, wd, g2, b2)


SC_CORES = 2
SC_SUBCORES = 16
SC_WORKERS = SC_CORES * SC_SUBCORES
COMBINE_CHUNK = 32


def _sc_gather_rows(table, idx, chunk, out_row_shape=None):
    n_out, = idx.shape
    row_shape = table.shape[1:]
    out_row_shape = row_shape if out_row_shape is None else out_row_shape
    per_w = n_out // SC_WORKERS
    n_chunks = per_w // chunk
    assert per_w * SC_WORKERS == n_out and n_chunks * chunk == per_w
    assert n_chunks % 2 == 0 and chunk % 8 == 0
    mesh = plsc.VectorSubcoreMesh(core_axis_name="c", subcore_axis_name="s")

    @functools.partial(
        pl.kernel, mesh=mesh,
        out_type=jax.ShapeDtypeStruct((n_out,) + out_row_shape, table.dtype),
        scratch_types=[
            pltpu.VMEM((per_w,), jnp.int32),
            pltpu.VMEM((2, chunk) + row_shape, table.dtype),
            pltpu.SemaphoreType.DMA((2,)),
            pltpu.SemaphoreType.DMA((2,)),
        ],
    )
    def gather_kernel(table_hbm, idx_hbm, out_hbm, idx_v, buf, gsem, wsem):
        wid = lax.axis_index("s") * SC_CORES + lax.axis_index("c")
        base = wid * per_w
        pltpu.sync_copy(idx_hbm.at[pl.ds(base, per_w)], idx_v)

        def gather(j, slot):
            rows = idx_v.at[pl.ds(j * chunk, chunk)]
            return pltpu.make_async_copy(table_hbm.at[rows], buf.at[slot], gsem.at[slot])

        def write(j, slot):
            dst = out_hbm.at[pl.ds(base + j * chunk, chunk)]
            src = buf.at[slot].reshape((chunk,) + out_row_shape)
            return pltpu.make_async_copy(src, dst, wsem.at[slot])

        gather(0, 0).start()

        @pl.loop(0, n_chunks, step=2)
        def _(j0):
            for slot in range(2):
                j = j0 + slot
                gather(j, slot).wait()

                @pl.when(j >= 1)
                def _():
                    write(j - 1, 1 - slot).wait()

                @pl.when(j + 1 < n_chunks)
                def _():
                    gather(j + 1, 1 - slot).start()

                write(j, slot).start()

        write(n_chunks - 1, (n_chunks - 1) % 2).wait()

    return gather_kernel(table, idx)


SC_LANES = 16


def _sc_dispatch(rows, meta, cls, rank, class_start):
    n_rows = rows.shape[0]
    chunk = COMBINE_CHUNK
    per_w = n_rows // SC_WORKERS
    n_chunks = per_w // chunk
    assert per_w * SC_WORKERS == n_rows and n_chunks * chunk == per_w
    assert n_chunks % 2 == 0 and chunk % SC_LANES == 0
    mesh = plsc.VectorSubcoreMesh(core_axis_name="c", subcore_axis_name="s")

    @functools.partial(
        pl.kernel, mesh=mesh,
        out_type=[
            jax.ShapeDtypeStruct((P_ROWS,) + rows.shape[1:], rows.dtype),
            jax.ShapeDtypeStruct((P_ROWS,) + meta.shape[1:], meta.dtype),
            jax.ShapeDtypeStruct((n_rows,), jnp.int32),
        ],
        scratch_types=[
            pltpu.VMEM((per_w,), jnp.int32),
            pltpu.VMEM((per_w,), jnp.int32),
            pltpu.VMEM((LANES,), jnp.int32),
            pltpu.VMEM((per_w,), jnp.int32),
            pltpu.VMEM((n_chunks, chunk), jnp.int32),
            pltpu.VMEM((2, chunk) + rows.shape[1:], rows.dtype),
            pltpu.VMEM((2, chunk) + meta.shape[1:], meta.dtype),
            pltpu.SemaphoreType.DMA((2,)),
            pltpu.SemaphoreType.DMA((2,)),
            pltpu.SemaphoreType.DMA((2,)),
            pltpu.SemaphoreType.DMA((2,)),
        ],
        compiler_params=pltpu.CompilerParams(needs_layout_passes=False),
    )
    def dispatch_kernel(rows_hbm, meta_hbm, cls_hbm, rank_hbm, start_hbm,
                        xs_hbm, ws_hbm, pos_hbm,
                        cls_v, rank_v, start_v, pos_flat, pos_v, rbuf, mbuf,
                        rsem, msem, xsem, wsem):
        wid = lax.axis_index("s") * SC_CORES + lax.axis_index("c")
        base = wid * per_w
        pltpu.sync_copy(cls_hbm.at[pl.ds(base, per_w)], cls_v)
        pltpu.sync_copy(rank_hbm.at[pl.ds(base, per_w)], rank_v)
        pltpu.sync_copy(start_hbm, start_v)

        @pl.loop(0, n_chunks)
        def _(j):
            for k in range(chunk // SC_LANES):
                off = j * chunk + k * SC_LANES
                c = cls_v[pl.ds(off, SC_LANES)]
                pos = plsc.load_gather(start_v, [c]) + rank_v[pl.ds(off, SC_LANES)]
                pos_flat[pl.ds(off, SC_LANES)] = pos
                pos_v[j, pl.ds(k * SC_LANES, SC_LANES)] = pos

        pltpu.sync_copy(pos_flat, pos_hbm.at[pl.ds(base, per_w)])

        def read_rows(j, slot):
            src = rows_hbm.at[pl.ds(base + j * chunk, chunk)]
            return pltpu.make_async_copy(src, rbuf.at[slot], rsem.at[slot])

        def read_meta(j, slot):
            src = meta_hbm.at[pl.ds(base + j * chunk, chunk)]
            return pltpu.make_async_copy(src, mbuf.at[slot], msem.at[slot])

        def put_rows(j, slot):
            return pltpu.make_async_copy(rbuf.at[slot], xs_hbm.at[pos_v.at[j]], xsem.at[slot])

        def put_meta(j, slot):
            return pltpu.make_async_copy(mbuf.at[slot], ws_hbm.at[pos_v.at[j]], wsem.at[slot])

        read_rows(0, 0).start()
        read_meta(0, 0).start()

        @pl.loop(0, n_chunks, step=2)
        def _(j0):
            for slot in range(2):
                j = j0 + slot
                read_rows(j, slot).wait()
                read_meta(j, slot).wait()

                @pl.when(j >= 1)
                def _():
                    put_rows(j - 1, 1 - slot).wait()
                    put_meta(j - 1, 1 - slot).wait()

                @pl.when(j + 1 < n_chunks)
                def _():
                    read_rows(j + 1, 1 - slot).start()
                    read_meta(j + 1, 1 - slot).start()

                put_rows(j, slot).start()
                put_meta(j, slot).start()

        put_rows(n_chunks - 1, (n_chunks - 1) % 2).wait()
        put_meta(n_chunks - 1, (n_chunks - 1) % 2).wait()

    return dispatch_kernel(rows, meta, cls, rank, class_start)


def _block_diag(w):
    h, d, _ = w.shape
    eye = jnp.eye(h, dtype=w.dtype)
    return (eye[:, None, :, None] * w[:, :, None, :]).reshape(h * d, h * d)


def _pair_tables():
    pair_of = np.zeros((EXPERTS_PER_GROUP, EXPERTS_PER_GROUP), np.int32)
    lo = np.zeros((N_PAIRS,), np.int32)
    hi = np.zeros((N_PAIRS,), np.int32)
    p = 0
    for a in range(EXPERTS_PER_GROUP):
        for b in range(a + 1, EXPERTS_PER_GROUP):
            pair_of[a, b] = pair_of[b, a] = p
            lo[p], hi[p] = a, b
            p += 1
    return pair_of, lo, hi


def _route_kernel(lt_ref, info_ref, meta_ref, counts_ref, run_ref):
    step = pl.program_id(0)

    @pl.when(step == 0)
    def _():
        run_ref[...] = jnp.zeros_like(run_ref)

    f32 = jnp.float32
    sub8 = lax.broadcasted_iota(jnp.int32, (8, LANES), 0).astype(f32)
    row_id = lax.broadcasted_iota(jnp.int32, (LANES, LANES), 0)
    col_id = lax.broadcasted_iota(jnp.int32, (LANES, LANES), 1)
    prefix_mat = (row_id <= col_id).astype(jnp.bfloat16)
    ones_mat = jnp.ones((LANES, LANES), jnp.bfloat16)
    neg_inf = f32(-jnp.inf)

    def first_index_of_max(v):
        m = jnp.max(v, axis=0, keepdims=True)
        idx = jnp.min(jnp.where(v == m, sub8, f32(8)), axis=0, keepdims=True)
        return m, idx

    def lane_tile(k, run):
        lanes = pl.ds(pl.multiple_of(k * LANES, LANES), LANES)
        g = jnp.where(sub8 < N_GROUPS, lt_ref[0:GROUP_ROWS, lanes], neg_inf)
        g_max, g_idx = first_index_of_max(g)
        g_top_p = 1.0 / jnp.sum(jnp.exp(g - g_max), axis=0, keepdims=True)

        e_sel = lt_ref[GROUP_ROWS:GROUP_ROWS + EXPERTS_PER_GROUP, lanes]
        for grp in range(1, N_GROUPS):
            lo = GROUP_ROWS + grp * EXPERTS_PER_GROUP
            e_sel = jnp.where(g_idx == grp, lt_ref[lo:lo + EXPERTS_PER_GROUP, lanes], e_sel)
        m1, i1 = first_index_of_max(e_sel)
        rest = jnp.where(sub8 == i1, neg_inf, e_sel)
        m2 = jnp.max(rest, axis=0, keepdims=True)
        i2 = jnp.min(jnp.where((rest == m2) & (sub8 != i1), sub8, f32(8)), axis=0, keepdims=True)

        e = jnp.exp(m2 - m1)
        w1 = g_top_p / (1.0 + e)
        w2 = g_top_p * e / (1.0 + e)
        first_is_lo = i1 < i2
        w_lo = jnp.where(first_is_lo, w1, w2)
        w_hi = jnp.where(first_is_lo, w2, w1)
        lo_e = jnp.minimum(i1, i2)
        hi_e = jnp.maximum(i1, i2)
        pair = lo_e * (2 * EXPERTS_PER_GROUP - 1 - lo_e) * 0.5 + hi_e - lo_e - 1.0
        cls = (g_idx * N_PAIRS + pair).astype(jnp.int32)

        onehot = (row_id == cls).astype(jnp.bfloat16)
        csum = jnp.dot(onehot, prefix_mat, preferred_element_type=f32)
        rank = jnp.sum(onehot.astype(f32) * (csum + run), axis=0, keepdims=True) - 1.0
        run = run + jnp.dot(onehot, ones_mat, preferred_element_type=f32)

        info_ref[:, lanes] = jnp.where(sub8 == 0, cls, jnp.where(sub8 == 1, rank.astype(jnp.int32), 0))
        meta8 = jnp.where(sub8 == 0, w_lo, jnp.where(sub8 == 1, w_hi, 0.0))
        meta_t = jnp.concatenate([meta8, jnp.zeros((LANES - 8, LANES), f32)], axis=0)
        meta_ref[lanes, :] = meta_t.T
        return run

    run = lax.fori_loop(0, ROUTE_BLOCK // LANES, lane_tile, run_ref[...], unroll=ROUTE_UNROLL)
    run_ref[...] = run
    counts_ref[...] = run


def _route_call(lt):
    return pl.pallas_call(
        _route_kernel,
        grid=(N_TOKENS // ROUTE_BLOCK,),
        in_specs=[pl.BlockSpec((ROUTER_ROWS, ROUTE_BLOCK), lambda i: (0, i))],
        out_specs=[
            pl.BlockSpec((8, ROUTE_BLOCK), lambda i: (0, i)),
            pl.BlockSpec((ROUTE_BLOCK, LANES), lambda i: (i, 0)),
            pl.BlockSpec((LANES, LANES), lambda i: (0, 0)),
        ],
        out_shape=[
            jax.ShapeDtypeStruct((8, N_TOKENS), jnp.int32),
            jax.ShapeDtypeStruct((N_TOKENS, LANES), jnp.float32),
            jax.ShapeDtypeStruct((LANES, LANES), jnp.float32),
        ],
        scratch_shapes=[pltpu.VMEM((LANES, LANES), jnp.float32)],
        compiler_params=pltpu.CompilerParams(dimension_semantics=("arbitrary",)),
        name="route_rank",
    )(lt)


def _dispatch_plan(counts):
    _, pair_lo, pair_hi = _pair_tables()
    tiles_c = (counts + TM - 1) // TM
    tiles_g = tiles_c.reshape(N_GROUPS, N_PAIRS).sum(axis=1)
    tiles_g_pad = (tiles_g + SUB - 1) // SUB * SUB
    g_start = jnp.cumsum(tiles_g_pad) - tiles_g_pad
    tc = tiles_c.reshape(N_GROUPS, N_PAIRS)
    c_start = (g_start[:, None] + jnp.cumsum(tc, axis=1) - tc).reshape(N_CLASSES)
    class_start = jnp.zeros((LANES,), jnp.int32).at[:N_CLASSES].set(c_start * TM)

    tile_ids = jnp.arange(N_TILES, dtype=jnp.int32)
    c_end = c_start + tiles_c
    owner = jnp.sum((tile_ids[:, None] >= c_end[None, :]).astype(jnp.int32), axis=1)
    owner = jnp.minimum(owner, N_CLASSES - 1)
    in_class = (tile_ids >= c_start[owner]) & (tile_ids < c_end[owner])
    rows_left = counts[owner] - (tile_ids - c_start[owner]) * TM
    tile_rows = jnp.where(in_class, jnp.minimum(rows_left, TM), 0).astype(jnp.int32)
    pair = owner % N_PAIRS
    tile_ea = jnp.asarray(pair_lo)[pair]
    tile_eb = jnp.asarray(pair_hi)[pair]
    g_end = g_start + tiles_g_pad
    step_first = jnp.arange(N_STEPS3, dtype=jnp.int32) * SUB
    step_group = jnp.sum((step_first[:, None] >= g_end[None, :]).astype(jnp.int32), axis=1)
    step_group = jnp.minimum(step_group, N_GROUPS - 1)
    last_step = jnp.maximum(g_end[-1] // SUB - 1, 0)
    step_block = jnp.minimum(jnp.arange(N_STEPS3, dtype=jnp.int32), last_step)
    return (class_start, tile_ea, tile_eb, tile_rows, step_group.astype(jnp.int32),
            step_block.astype(jnp.int32))


def kernel(x, w_in, rg_conv_w, rg_conv_b, rg_gate_a_w, rg_gate_a_b, rg_gate_x_w, rg_gate_x_b,
           rg_lambda, sc_conv_w, w_out, ln1_g, ln1_b, router_group_w, router_group_b,
           router_expert_w, router_expert_b, exp_w_gate, exp_w_up, exp_w_down, ln2_g, ln2_b):
    bf16 = jnp.bfloat16
    row = lambda v: v.reshape(1, -1)
    pad_g = GROUP_ROWS - N_GROUPS
    pad_e = ROUTER_LANES - ROUTER_ROWS
    wr = jnp.concatenate([router_group_w, jnp.zeros((D_MODEL, pad_g), jnp.float32),
                          router_expert_w, jnp.zeros((D_MODEL, pad_e), jnp.float32)], axis=1)
    br = jnp.concatenate([router_group_b, jnp.zeros((pad_g,), jnp.float32),
                          router_expert_b, jnp.zeros((pad_e,), jnp.float32)]).reshape(1, -1)
    wr_hi = wr.astype(bf16)
    wr_lo = (wr - wr_hi.astype(jnp.float32)).astype(bf16)

    h1, lt, wgu, wdb = _mixer_call(
        x, w_in.astype(bf16), rg_conv_w, row(rg_conv_b),
        _block_diag(rg_gate_a_w).astype(bf16), row(rg_gate_a_b),
        _block_diag(rg_gate_x_w).astype(bf16), row(rg_gate_x_b),
        row(rg_lambda), sc_conv_w, w_out.astype(bf16), row(ln1_g), row(ln1_b),
        jnp.concatenate([wr_hi, wr_lo], axis=1), br, exp_w_gate, exp_w_up, exp_w_down)

    info, meta, counts = _route_call(lt)
    class_start, tile_ea, tile_eb, tile_rows, step_group, step_block = _dispatch_plan(
        counts[:N_CLASSES, 0].astype(jnp.int32))

    row_tiles = (-1, LANE_TILES, LANES)
    xs, wts, pos = _sc_dispatch(h1.reshape(row_tiles), meta, info[0], info[1], class_start)

    grp_shape = (N_GROUPS, EXPERTS_PER_GROUP)
    ys = _expert_call(
        step_block, step_group, tile_ea, tile_eb, tile_rows, xs.reshape(-1, LANES), wts,
        wgu.reshape(grp_shape + (D_MODEL, 2 * D_EXPERT)),
        wdb.reshape(grp_shape + (D_EXPERT, D_MODEL)),
        row(ln2_g), row(ln2_b))

    out_rows = pos.reshape(SEQ, BATCH).T.reshape(-1)
    out = _sc_gather_rows(ys.reshape(row_tiles), out_rows, COMBINE_CHUNK, (D_MODEL,))
    return out.reshape(BATCH, SEQ, D_MODEL)
```

```python
import functools

import jax
import jax.numpy as jnp
import numpy as np
from jax import lax
from jax.experimental import pallas as pl
from jax.experimental.pallas import tpu as pltpu
from jax.experimental.pallas import tpu_sc as plsc

D_MODEL = 1024
BATCH = 16
SEQ = 2048
D_RG = 512
D_SC = 512
RG_HEADS = 8
RG_HEAD_DIM = D_RG // RG_HEADS
RG_CONV = 4
RG_C = 8.0
SC_CONV = 3
D_IN_PROJ = 2 * D_RG + 3 * D_SC
N_GROUPS = 4
EXPERTS_PER_GROUP = 8
N_EXPERTS = N_GROUPS * EXPERTS_PER_GROUP
D_EXPERT = D_MODEL // 4
LN_EPS = 1e-5
DEEPNORM_ALPHA = 2.0 ** 0.25

N_TOKENS = BATCH * SEQ
LANES = 128
LANE_TILES = D_MODEL // LANES
MXU_WIDTH = 256
ROUTER_LANES = LANES
GROUP_ROWS = 8
ROUTER_ROWS = GROUP_ROWS + N_EXPERTS

TS = 64
M1 = TS * BATCH
DOT_ROWS = 256

ROUTE_BLOCK = 4096
ROUTE_UNROLL = 8

N_PAIRS = EXPERTS_PER_GROUP * (EXPERTS_PER_GROUP - 1) // 2
N_CLASSES = N_GROUPS * N_PAIRS
BF16_ROWS = 16
TM = -(-int(N_TOKENS / N_CLASSES * 1.09 / 2) // BF16_ROWS) * BF16_ROWS
SUB = 8
N_TILES = -(-(N_TOKENS // TM + N_CLASSES + N_GROUPS * (SUB - 1)) // SUB) * SUB
N_STEPS3 = N_TILES // SUB
P_ROWS = N_TILES * TM

VMEM_LIMIT = 56 * 1024 * 1024


def _layer_norm(z, g, b):
    mu = jnp.mean(z, axis=-1, keepdims=True)
    zc = z - mu
    var = jnp.mean(zc * zc, axis=-1, keepdims=True)
    return zc * lax.rsqrt(var + LN_EPS) * g + b


def _store_row_tiles(ref, first_row, value):
    n = value.shape[0]
    for j in range(LANE_TILES):
        ref[pl.ds(first_row * LANE_TILES + j, n, stride=LANE_TILES), :] = (
            value[:, j * LANES:(j + 1) * LANES])


def _load_row_tiles(ref, first_row, n):
    return jnp.concatenate(
        [ref[pl.ds(first_row * LANE_TILES + j, n, stride=LANE_TILES), :]
         for j in range(LANE_TILES)], axis=1)


def _mixer_kernel(x_ref, w_in_ref, cw_ref, cb_ref, wa_ref, ba_ref, wx_ref, bx_ref, lam_ref,
                  scw_ref, w_out_ref, g1_ref, b1_ref, wr_ref, br_ref, wg_ref, wu_ref, wd_ref,
                  h1_ref, logit_ref, wgu_ref, wdb_ref,
                  xbuf_ref, xsem, cbuf_ref, sbuf_ref, a_ref, u_ref, hstate_ref):
    c = pl.program_id(0)

    wgu_ref[:, :, :D_EXPERT] = wg_ref[...].astype(jnp.bfloat16)
    wgu_ref[:, :, D_EXPERT:] = wu_ref[...].astype(jnp.bfloat16)
    wdb_ref[...] = wd_ref[...].astype(jnp.bfloat16)
    slot = c % 2
    rg_pad = (RG_CONV - 1) * BATCH
    sc_pad = (SC_CONV - 1) * BATCH

    def x_copies(chunk, slot_):
        return [pltpu.make_async_copy(x_ref.at[b, pl.ds(chunk * TS, TS), :],
                                      xbuf_ref.at[slot_, :, b, :], xsem.at[slot_])
                for b in range(BATCH)]

    @pl.when(c == 0)
    def _():
        for cp in x_copies(0, 0):
            cp.start()
        cbuf_ref[0:rg_pad, :] = jnp.zeros((rg_pad, D_RG), jnp.float32)
        sbuf_ref[0:sc_pad, :] = jnp.zeros((sc_pad, D_SC), jnp.float32)
        hstate_ref[...] = jnp.zeros_like(hstate_ref)

    @pl.when(c + 1 < SEQ // TS)
    def _():
        for cp in x_copies(c + 1, 1 - slot):
            cp.start()

    for cp in x_copies(c, slot):
        cp.wait()

    xt_ref = xbuf_ref.at[slot]

    def gate(xcb, w_ref, b_ref):
        parts = [jnp.dot(xcb[:, lo:lo + MXU_WIDTH], w_ref[lo:lo + MXU_WIDTH, lo:lo + MXU_WIDTH],
                         preferred_element_type=jnp.float32)
                 for lo in range(0, D_RG, MXU_WIDTH)]
        return 0.5 + 0.5 * jnp.tanh(0.5 * (jnp.concatenate(parts, axis=1) + b_ref[...]))

    soft_lam = jax.nn.softplus(-lam_ref[...])
    gelu_chunks, y_sc_chunks = [], []
    for r0 in range(0, M1, DOT_ROWS):
        t0 = r0 // BATCH
        xk = xt_ref[t0:t0 + DOT_ROWS // BATCH].reshape(DOT_ROWS, D_MODEL).astype(jnp.bfloat16)

        def proj(lo, width):
            return jnp.dot(xk, w_in_ref[:, lo:lo + width], preferred_element_type=jnp.float32)

        cbuf_ref[rg_pad + r0:rg_pad + r0 + DOT_ROWS, :] = proj(0, D_RG)
        rg_gate = proj(D_RG, D_RG)
        xc = cb_ref[...] + cw_ref[0:1, :] * cbuf_ref[r0:r0 + DOT_ROWS, :]
        for k in range(1, RG_CONV):
            xc = xc + cw_ref[k:k + 1, :] * cbuf_ref[r0 + k * BATCH:r0 + k * BATCH + DOT_ROWS, :]
        sc_b = proj(2 * D_RG, D_SC)
        xcb = xc.astype(jnp.bfloat16)
        r = gate(xcb, wa_ref, ba_ref)
        i = gate(xcb, wx_ref, bx_ref)
        sc_c = proj(2 * D_RG + D_SC, D_SC)
        log_a = (-RG_C) * r * soft_lam
        a_ref[r0:r0 + DOT_ROWS, :] = jnp.exp(log_a)
        th = jnp.tanh(log_a)
        u_ref[r0:r0 + DOT_ROWS, :] = jnp.sqrt(-2.0 * th / (1.0 - th)) * (i * xc)
        gelu_chunks.append(jax.nn.gelu(rg_gate))

        sbuf_ref[sc_pad + r0:sc_pad + r0 + DOT_ROWS, :] = sc_c * proj(2 * D_RG + 2 * D_SC, D_SC)
        conv = scw_ref[0:1, :] * sbuf_ref[r0:r0 + DOT_ROWS, :]
        for k in range(1, SC_CONV):
            conv = conv + scw_ref[k:k + 1, :] * sbuf_ref[r0 + k * BATCH:r0 + k * BATCH + DOT_ROWS, :]
        y_sc_chunks.append((sc_b * conv).astype(jnp.bfloat16))

    cbuf_ref[0:rg_pad, :] = cbuf_ref[M1:M1 + rg_pad, :]
    sbuf_ref[0:sc_pad, :] = sbuf_ref[M1:M1 + sc_pad, :]

    h = hstate_ref[...]
    for t in range(TS):
        h = a_ref[t * BATCH:(t + 1) * BATCH, :] * h + u_ref[t * BATCH:(t + 1) * BATCH, :]
        u_ref[t * BATCH:(t + 1) * BATCH, :] = h
    hstate_ref[...] = h

    tail_rows = M1 // 2
    per_tail = tail_rows // DOT_ROWS

    def out_proj(k):
        r0 = k * tail_rows
        gelu_k = jnp.concatenate(gelu_chunks[k * per_tail:(k + 1) * per_tail], axis=0)
        y_sc_k = jnp.concatenate(y_sc_chunks[k * per_tail:(k + 1) * per_tail], axis=0)
        y_rg = (u_ref[r0:r0 + tail_rows, :] * gelu_k).astype(jnp.bfloat16)
        mix = jnp.dot(y_rg, w_out_ref[0:D_RG, :], preferred_element_type=jnp.float32)
        return mix + jnp.dot(y_sc_k, w_out_ref[D_RG:, :], preferred_element_type=jnp.float32)

    def finish(k, mix):
        r0 = k * tail_rows
        t0 = r0 // BATCH
        x_rows = xt_ref[t0:t0 + tail_rows // BATCH].reshape(tail_rows, D_MODEL)
        h1 = _layer_norm(DEEPNORM_ALPHA * x_rows + mix, g1_ref[...], b1_ref[...])
        _store_row_tiles(h1_ref, r0, h1)
        h_hi = h1.astype(jnp.bfloat16)
        h_lo = (h1 - h_hi.astype(jnp.float32)).astype(jnp.bfloat16)
        both = jnp.dot(h_hi, wr_ref[...], preferred_element_type=jnp.float32)
        logits = both[:, :ROUTER_LANES] + both[:, ROUTER_LANES:] + br_ref[...]
        logits = logits + jnp.dot(h_lo, wr_ref[:, :ROUTER_LANES],
                                  preferred_element_type=jnp.float32)
        logit_ref[:, r0:r0 + tail_rows] = logits.T[:ROUTER_ROWS, :]

    n_tails = M1 // tail_rows
    mix = out_proj(0)
    for k in range(n_tails):
        next_mix = out_proj(k + 1) if k + 1 < n_tails else None
        finish(k, mix)
        mix = next_mix


def _const_spec(shape):
    return pl.BlockSpec(shape, lambda c: (0,) * len(shape))


def _mixer_call(x, w_in, cw, cb, wa, ba, wx, bx, lam, scw, w_out, g1, b1, wr, br, wg, wu, wd):
    n_chunks = SEQ // TS
    epc = N_EXPERTS // n_chunks
    assert epc * n_chunks == N_EXPERTS
    in_specs = [
        pl.BlockSpec(memory_space=pl.ANY),
        _const_spec((D_MODEL, D_IN_PROJ)),
        _const_spec((RG_CONV, D_RG)), _const_spec((1, D_RG)),
        _const_spec((D_RG, D_RG)), _const_spec((1, D_RG)),
        _const_spec((D_RG, D_RG)), _const_spec((1, D_RG)),
        _const_spec((1, D_RG)),
        _const_spec((SC_CONV, D_SC)),
        _const_spec((D_MODEL, D_MODEL)),
        _const_spec((1, D_MODEL)), _const_spec((1, D_MODEL)),
        _const_spec((D_MODEL, 2 * ROUTER_LANES)), _const_spec((1, ROUTER_LANES)),
        pl.BlockSpec((epc, D_MODEL, D_EXPERT), lambda c: (c, 0, 0)),
        pl.BlockSpec((epc, D_MODEL, D_EXPERT), lambda c: (c, 0, 0)),
        pl.BlockSpec((epc, D_EXPERT, D_MODEL), lambda c: (c, 0, 0)),
    ]
    out_specs = [
        pl.BlockSpec((M1 * LANE_TILES, LANES), lambda c: (c, 0)),
        pl.BlockSpec((ROUTER_ROWS, M1), lambda c: (0, c)),
        pl.BlockSpec((epc, D_MODEL, 2 * D_EXPERT), lambda c: (c, 0, 0)),
        pl.BlockSpec((epc, D_EXPERT, D_MODEL), lambda c: (c, 0, 0)),
    ]
    return pl.pallas_call(
        _mixer_kernel,
        grid=(n_chunks,),
        in_specs=in_specs,
        out_specs=out_specs,
        out_shape=[
            jax.ShapeDtypeStruct((N_TOKENS * LANE_TILES, LANES), jnp.float32),
            jax.ShapeDtypeStruct((ROUTER_ROWS, N_TOKENS), jnp.float32),
            jax.ShapeDtypeStruct((N_EXPERTS, D_MODEL, 2 * D_EXPERT), jnp.bfloat16),
            jax.ShapeDtypeStruct((N_EXPERTS, D_EXPERT, D_MODEL), jnp.bfloat16),
        ],
        scratch_shapes=[
            pltpu.VMEM((2, TS, BATCH, D_MODEL), jnp.float32),
            pltpu.SemaphoreType.DMA((2,)),
            pltpu.VMEM((M1 + (RG_CONV - 1) * BATCH, D_RG), jnp.float32),
            pltpu.VMEM((M1 + (SC_CONV - 1) * BATCH, D_SC), jnp.float32),
            pltpu.VMEM((M1, D_RG), jnp.float32),
            pltpu.VMEM((M1, D_RG), jnp.float32),
            pltpu.VMEM((BATCH, D_RG), jnp.float32),
        ],
        compiler_params=pltpu.CompilerParams(
            dimension_semantics=("arbitrary",), vmem_limit_bytes=VMEM_LIMIT),
        name="mixer_ln_router",
    )(x, w_in, cw, cb, wa, ba, wx, bx, lam, scw, w_out, g1, b1, wr, br, wg, wu, wd)


def _expert_kernel(blk_ref, grp_ref, ea_ref, eb_ref, rows_ref,
                   xs_hbm, wt_ref, wgu_ref, wd_ref, g2_ref, b2_ref, ys_hbm,
                   xbuf_ref, xsem, obuf_ref, osem):
    s = pl.program_id(0)
    slot = s % 2
    tile_rows = [rows_ref[s * SUB + j] for j in range(SUB)]
    block_rows = SUB * TM

    def fetches(step, slot_):
        return [pltpu.make_async_copy(xs_hbm.at[pl.ds(step * block_rows, block_rows), j, :],
                                      xbuf_ref.at[slot_, j], xsem.at[slot_])
                for j in range(LANE_TILES)]

    def write_backs(step, slot_):
        return [pltpu.make_async_copy(obuf_ref.at[slot_, j],
                                      ys_hbm.at[pl.ds(step * block_rows, block_rows), j, :],
                                      osem.at[slot_])
                for j in range(LANE_TILES)]

    def did_work(step):
        return blk_ref[jnp.clip(step, 0, N_STEPS3 - 1)] == step

    @pl.when(jnp.logical_and(s == 0, did_work(0)))
    def _():
        for cp in fetches(0, 0):
            cp.start()

    @pl.when(did_work(s + 1))
    def _():
        for cp in fetches(s + 1, 1 - slot):
            cp.start()

    @pl.when(did_work(s - 2))
    def _():
        for cp in write_backs(s - 2, slot):
            cp.wait()

    @pl.when(did_work(s))
    def _():
        for cp in fetches(s, slot):
            cp.wait()
        for j in range(SUB):
            tile = s * SUB + j
            first_row = j * TM
            in_use = lax.broadcasted_iota(jnp.int32, (TM, 1), 0) < tile_rows[j]
            x = jnp.concatenate([xbuf_ref[slot, t, first_row:first_row + TM, :]
                                 for t in range(LANE_TILES)], axis=1)
            x = jnp.where(in_use, x, 0.0)
            xb = x.astype(jnp.bfloat16)
            wt = jnp.where(in_use, wt_ref[first_row:first_row + TM, :], 0.0)
            y = jnp.zeros((TM, D_MODEL), jnp.float32)
            for e_ref, col in ((ea_ref, 0), (eb_ref, 1)):
                e = e_ref[tile]
                hgu = jnp.dot(xb, wgu_ref[e], preferred_element_type=jnp.float32)
                hid = jax.nn.silu(hgu[:, :D_EXPERT]) * hgu[:, D_EXPERT:] * wt[:, col:col + 1]
                y = y + jnp.dot(hid.astype(jnp.bfloat16), wd_ref[e],
                                preferred_element_type=jnp.float32)
            out = _layer_norm(DEEPNORM_ALPHA * x + y, g2_ref[...], b2_ref[...])
            for t in range(LANE_TILES):
                obuf_ref[slot, t, first_row:first_row + TM, :] = out[:, t * LANES:(t + 1) * LANES]
        for cp in write_backs(s, slot):
            cp.start()

    @pl.when(s == N_STEPS3 - 1)
    def _():
        @pl.when(did_work(s - 1))
        def _():
            for cp in write_backs(s - 1, 1 - slot):
                cp.wait()

        @pl.when(did_work(s))
        def _():
            for cp in write_backs(s, slot):
                cp.wait()


def _expert_call(step_block, step_group, tile_ea, tile_eb, tile_rows, xs, wts, wgu, wd, g2, b2):
    rows = SUB * TM
    grid_spec = pltpu.PrefetchScalarGridSpec(
        num_scalar_prefetch=5,
        grid=(N_STEPS3,),
        in_specs=[
            pl.BlockSpec(memory_space=pl.ANY),
            pl.BlockSpec((rows, LANES), lambda s, blk, grp, ea, eb, va: (blk[s], 0)),
            pl.BlockSpec((None, EXPERTS_PER_GROUP, D_MODEL, 2 * D_EXPERT),
                         lambda s, blk, grp, ea, eb, va: (grp[s], 0, 0, 0)),
            pl.BlockSpec((None, EXPERTS_PER_GROUP, D_EXPERT, D_MODEL),
                         lambda s, blk, grp, ea, eb, va: (grp[s], 0, 0, 0)),
            pl.BlockSpec((1, D_MODEL), lambda s, blk, grp, ea, eb, va: (0, 0)),
            pl.BlockSpec((1, D_MODEL), lambda s, blk, grp, ea, eb, va: (0, 0)),
        ],
        out_specs=pl.BlockSpec(memory_space=pl.ANY),
        scratch_shapes=[
            pltpu.VMEM((2, LANE_TILES, rows, LANES), jnp.float32),
            pltpu.SemaphoreType.DMA((2,)),
            pltpu.VMEM((2, LANE_TILES, rows, LANES), jnp.float32),
            pltpu.SemaphoreType.DMA((2,)),
        ],
    )
    return pl.pallas_call(
        _expert_kernel,
        grid_spec=grid_spec,
        out_shape=jax.ShapeDtypeStruct((P_ROWS, LANE_TILES, LANES), jnp.float32),
        compiler_params=pltpu.CompilerParams(
            dimension_semantics=("arbitrary",), vmem_limit_bytes=VMEM_LIMIT),
        name="experts_ln",
    )(step_block, step_group, tile_ea, tile_eb, tile_rows, xs, wts, wgu, wd, g2, b2)


SC_CORES = 2
SC_SUBCORES = 16
SC_WORKERS = SC_CORES * SC_SUBCORES
COMBINE_CHUNK = 32


def _sc_gather_rows(table, idx, chunk, out_row_shape=None):
    n_out, = idx.shape
    row_shape = table.shape[1:]
    out_row_shape = row_shape if out_row_shape is None else out_row_shape
    per_w = n_out // SC_WORKERS
    n_chunks = per_w // chunk
    assert per_w * SC_WORKERS == n_out and n_chunks * chunk == per_w
    assert n_chunks % 2 == 0 and chunk % 8 == 0
    mesh = plsc.VectorSubcoreMesh(core_axis_name="c", subcore_axis_name="s")

    @functools.partial(
        pl.kernel, mesh=mesh,
        out_type=jax.ShapeDtypeStruct((n_out,) + out_row_shape, table.dtype),
        scratch_types=[
            pltpu.VMEM((per_w,), jnp.int32),
            pltpu.VMEM((2, chunk) + row_shape, table.dtype),
            pltpu.SemaphoreType.DMA((2,)),
            pltpu.SemaphoreType.DMA((2,)),
        ],
    )
    def gather_kernel(table_hbm, idx_hbm, out_hbm, idx_v, buf, gsem, wsem):
        wid = lax.axis_index("s") * SC_CORES + lax.axis_index("c")
        base = wid * per_w
        pltpu.sync_copy(idx_hbm.at[pl.ds(base, per_w)], idx_v)

        def gather(j, slot):
            rows = idx_v.at[pl.ds(j * chunk, chunk)]
            return pltpu.make_async_copy(table_hbm.at[rows], buf.at[slot], gsem.at[slot])

        def write(j, slot):
            dst = out_hbm.at[pl.ds(base + j * chunk, chunk)]
            src = buf.at[slot].reshape((chunk,) + out_row_shape)
            return pltpu.make_async_copy(src, dst, wsem.at[slot])

        gather(0, 0).start()

        @pl.loop(0, n_chunks, step=2)
        def _(j0):
            for slot in range(2):
                j = j0 + slot
                gather(j, slot).wait()

                @pl.when(j >= 1)
                def _():
                    write(j - 1, 1 - slot).wait()

                @pl.when(j + 1 < n_chunks)
                def _():
                    gather(j + 1, 1 - slot).start()

                write(j, slot).start()

        write(n_chunks - 1, (n_chunks - 1) % 2).wait()

    return gather_kernel(table, idx)


SC_LANES = 16


def _sc_dispatch(rows, meta, cls, rank, class_start):
    n_rows = rows.shape[0]
    chunk = COMBINE_CHUNK
    per_w = n_rows // SC_WORKERS
    n_chunks = per_w // chunk
    assert per_w * SC_WORKERS == n_rows and n_chunks * chunk == per_w
    assert n_chunks % 2 == 0 and chunk % SC_LANES == 0
    mesh = plsc.VectorSubcoreMesh(core_axis_name="c", subcore_axis_name="s")

    @functools.partial(
        pl.kernel, mesh=mesh,
        out_type=[
            jax.ShapeDtypeStruct((P_ROWS,) + rows.shape[1:], rows.dtype),
            jax.ShapeDtypeStruct((P_ROWS,) + meta.shape[1:], meta.dtype),
            jax.ShapeDtypeStruct((n_rows,), jnp.int32),
        ],
        scratch_types=[
            pltpu.VMEM((per_w,), jnp.int32),
            pltpu.VMEM((per_w,), jnp.int32),
            pltpu.VMEM((LANES,), jnp.int32),
            pltpu.VMEM((per_w,), jnp.int32),
            pltpu.VMEM((n_chunks, chunk), jnp.int32),
            pltpu.VMEM((2, chunk) + rows.shape[1:], rows.dtype),
            pltpu.VMEM((2, chunk) + meta.shape[1:], meta.dtype),
            pltpu.SemaphoreType.DMA((2,)),
            pltpu.SemaphoreType.DMA((2,)),
            pltpu.SemaphoreType.DMA((2,)),
            pltpu.SemaphoreType.DMA((2,)),
        ],
        compiler_params=pltpu.CompilerParams(needs_layout_passes=False),
    )
    def dispatch_kernel(rows_hbm, meta_hbm, cls_hbm, rank_hbm, start_hbm,
                        xs_hbm, ws_hbm, pos_hbm,
                        cls_v, rank_v, start_v, pos_flat, pos_v, rbuf, mbuf,
                        rsem, msem, xsem, wsem):
        wid = lax.axis_index("s") * SC_CORES + lax.axis_index("c")
        base = wid * per_w
        pltpu.sync_copy(cls_hbm.at[pl.ds(base, per_w)], cls_v)
        pltpu.sync_copy(rank_hbm.at[pl.ds(base, per_w)], rank_v)
        pltpu.sync_copy(start_hbm, start_v)

        @pl.loop(0, n_chunks)
        def _(j):
            for k in range(chunk // SC_LANES):
                off = j * chunk + k * SC_LANES
                c = cls_v[pl.ds(off, SC_LANES)]
                pos = plsc.load_gather(start_v, [c]) + rank_v[pl.ds(off, SC_LANES)]
                pos_flat[pl.ds(off, SC_LANES)] = pos
                pos_v[j, pl.ds(k * SC_LANES, SC_LANES)] = pos

        pltpu.sync_copy(pos_flat, pos_hbm.at[pl.ds(base, per_w)])

        def read_rows(j, slot):
            src = rows_hbm.at[pl.ds(base + j * chunk, chunk)]
            return pltpu.make_async_copy(src, rbuf.at[slot], rsem.at[slot])

        def read_meta(j, slot):
            src = meta_hbm.at[pl.ds(base + j * chunk, chunk)]
            return pltpu.make_async_copy(src, mbuf.at[slot], msem.at[slot])

        def put_rows(j, slot):
            return pltpu.make_async_copy(rbuf.at[slot], xs_hbm.at[pos_v.at[j]], xsem.at[slot])

        def put_meta(j, slot):
            return pltpu.make_async_copy(mbuf.at[slot], ws_hbm.at[pos_v.at[j]], wsem.at[slot])

        read_rows(0, 0).start()
        read_meta(0, 0).start()

        @pl.loop(0, n_chunks, step=2)
        def _(j0):
            for slot in range(2):
                j = j0 + slot
                read_rows(j, slot).wait()
                read_meta(j, slot).wait()

                @pl.when(j >= 1)
                def _():
                    put_rows(j - 1, 1 - slot).wait()
                    put_meta(j - 1, 1 - slot).wait()

                @pl.when(j + 1 < n_chunks)
                def _():
                    read_rows(j + 1, 1 - slot).start()
                    read_meta(j + 1, 1 - slot).start()

                put_rows(j, slot).start()
                put_meta(j, slot).start()

        put_rows(n_chunks - 1, (n_chunks - 1) % 2).wait()
        put_meta(n_chunks - 1, (n_chunks - 1) % 2).wait()

    return dispatch_kernel(rows, meta, cls, rank, class_start)


def _block_diag(w):
    h, d, _ = w.shape
    eye = jnp.eye(h, dtype=w.dtype)
    return (eye[:, None, :, None] * w[:, :, None, :]).reshape(h * d, h * d)


def _pair_tables():
    pair_of = np.zeros((EXPERTS_PER_GROUP, EXPERTS_PER_GROUP), np.int32)
    lo = np.zeros((N_PAIRS,), np.int32)
    hi = np.zeros((N_PAIRS,), np.int32)
    p = 0
    for a in range(EXPERTS_PER_GROUP):
        for b in range(a + 1, EXPERTS_PER_GROUP):
            pair_of[a, b] = pair_of[b, a] = p
            lo[p], hi[p] = a, b
            p += 1
    return pair_of, lo, hi


def _route_kernel(lt_ref, info_ref, meta_ref, counts_ref, run_ref):
    step = pl.program_id(0)

    @pl.when(step == 0)
    def _():
        run_ref[...] = jnp.zeros_like(run_ref)

    f32 = jnp.float32
    sub8 = lax.broadcasted_iota(jnp.int32, (8, LANES), 0).astype(f32)
    row_id = lax.broadcasted_iota(jnp.int32, (LANES, LANES), 0)
    col_id = lax.broadcasted_iota(jnp.int32, (LANES, LANES), 1)
    prefix_mat = (row_id <= col_id).astype(jnp.bfloat16)
    ones_mat = jnp.ones((LANES, LANES), jnp.bfloat16)
    neg_inf = f32(-jnp.inf)

    def first_index_of_max(v):
        m = jnp.max(v, axis=0, keepdims=True)
        idx = jnp.min(jnp.where(v == m, sub8, f32(8)), axis=0, keepdims=True)
        return m, idx

    def lane_tile(k, run):
        lanes = pl.ds(pl.multiple_of(k * LANES, LANES), LANES)
        g = jnp.where(sub8 < N_GROUPS, lt_ref[0:GROUP_ROWS, lanes], neg_inf)
        g_max, g_idx = first_index_of_max(g)
        g_top_p = 1.0 / jnp.sum(jnp.exp(g - g_max), axis=0, keepdims=True)

        e_sel = lt_ref[GROUP_ROWS:GROUP_ROWS + EXPERTS_PER_GROUP, lanes]
        for grp in range(1, N_GROUPS):
            lo = GROUP_ROWS + grp * EXPERTS_PER_GROUP
            e_sel = jnp.where(g_idx == grp, lt_ref[lo:lo + EXPERTS_PER_GROUP, lanes], e_sel)
        m1, i1 = first_index_of_max(e_sel)
        rest = jnp.where(sub8 == i1, neg_inf, e_sel)
        m2 = jnp.max(rest, axis=0, keepdims=True)
        i2 = jnp.min(jnp.where((rest == m2) & (sub8 != i1), sub8, f32(8)), axis=0, keepdims=True)

        e = jnp.exp(m2 - m1)
        w1 = g_top_p / (1.0 + e)
        w2 = g_top_p * e / (1.0 + e)
        first_is_lo = i1 < i2
        w_lo = jnp.where(first_is_lo, w1, w2)
        w_hi = jnp.where(first_is_lo, w2, w1)
        lo_e = jnp.minimum(i1, i2)
        hi_e = jnp.maximum(i1, i2)
        pair = lo_e * (2 * EXPERTS_PER_GROUP - 1 - lo_e) * 0.5 + hi_e - lo_e - 1.0
        cls = (g_idx * N_PAIRS + pair).astype(jnp.int32)

        onehot = (row_id == cls).astype(jnp.bfloat16)
        csum = jnp.dot(onehot, prefix_mat, preferred_element_type=f32)
        rank = jnp.sum(onehot.astype(f32) * (csum + run), axis=0, keepdims=True) - 1.0
        run = run + jnp.dot(onehot, ones_mat, preferred_element_type=f32)

        info_ref[:, lanes] = jnp.where(sub8 == 0, cls, jnp.where(sub8 == 1, rank.astype(jnp.int32), 0))
        meta8 = jnp.where(sub8 == 0, w_lo, jnp.where(sub8 == 1, w_hi, 0.0))
        meta_t = jnp.concatenate([meta8, jnp.zeros((LANES - 8, LANES), f32)], axis=0)
        meta_ref[lanes, :] = meta_t.T
        return run

    run = lax.fori_loop(0, ROUTE_BLOCK // LANES, lane_tile, run_ref[...], unroll=ROUTE_UNROLL)
    run_ref[...] = run
    counts_ref[...] = run


def _route_call(lt):
    return pl.pallas_call(
        _route_kernel,
        grid=(N_TOKENS // ROUTE_BLOCK,),
        in_specs=[pl.BlockSpec((ROUTER_ROWS, ROUTE_BLOCK), lambda i: (0, i))],
        out_specs=[
            pl.BlockSpec((8, ROUTE_BLOCK), lambda i: (0, i)),
            pl.BlockSpec((ROUTE_BLOCK, LANES), lambda i: (i, 0)),
            pl.BlockSpec((LANES, LANES), lambda i: (0, 0)),
        ],
        out_shape=[
            jax.ShapeDtypeStruct((8, N_TOKENS), jnp.int32),
            jax.ShapeDtypeStruct((N_TOKENS, LANES), jnp.float32),
            jax.ShapeDtypeStruct((LANES, LANES), jnp.float32),
        ],
        scratch_shapes=[pltpu.VMEM((LANES, LANES), jnp.float32)],
        compiler_params=pltpu.CompilerParams(dimension_semantics=("arbitrary",)),
        name="route_rank",
    )(lt)


def _dispatch_plan(counts):
    _, pair_lo, pair_hi = _pair_tables()
    tiles_c = (counts + TM - 1) // TM
    tiles_g = tiles_c.reshape(N_GROUPS, N_PAIRS).sum(axis=1)
    tiles_g_pad = (tiles_g + SUB - 1) // SUB * SUB
    g_start = jnp.cumsum(tiles_g_pad) - tiles_g_pad
    tc = tiles_c.reshape(N_GROUPS, N_PAIRS)
    c_start = (g_start[:, None] + jnp.cumsum(tc, axis=1) - tc).reshape(N_CLASSES)
    class_start = jnp.zeros((LANES,), jnp.int32).at[:N_CLASSES].set(c_start * TM)

    tile_ids = jnp.arange(N_TILES, dtype=jnp.int32)
    c_end = c_start + tiles_c
    owner = jnp.sum((tile_ids[:, None] >= c_end[None, :]).astype(jnp.int32), axis=1)
    owner = jnp.minimum(owner, N_CLASSES - 1)
    in_class = (tile_ids >= c_start[owner]) & (tile_ids < c_end[owner])
    rows_left = counts[owner] - (tile_ids - c_start[owner]) * TM
    tile_rows = jnp.where(in_class, jnp.minimum(rows_left, TM), 0).astype(jnp.int32)
    pair = owner % N_PAIRS
    tile_ea = jnp.asarray(pair_lo)[pair]
    tile_eb = jnp.asarray(pair_hi)[pair]
    g_end = g_start + tiles_g_pad
    step_first = jnp.arange(N_STEPS3, dtype=jnp.int32) * SUB
    step_group = jnp.sum((step_first[:, None] >= g_end[None, :]).astype(jnp.int32), axis=1)
    step_group = jnp.minimum(step_group, N_GROUPS - 1)
    last_step = jnp.maximum(g_end[-1] // SUB - 1, 0)
    step_block = jnp.minimum(jnp.arange(N_STEPS3, dtype=jnp.int32), last_step)
    return (class_start, tile_ea, tile_eb, tile_rows, step_group.astype(jnp.int32),
            step_block.astype(jnp.int32))


def kernel(x, w_in, rg_conv_w, rg_conv_b, rg_gate_a_w, rg_gate_a_b, rg_gate_x_w, rg_gate_x_b,
           rg_lambda, sc_conv_w, w_out, ln1_g, ln1_b, router_group_w, router_group_b,
           router_expert_w, router_expert_b, exp_w_gate, exp_w_up, exp_w_down, ln2_g, ln2_b):
    bf16 = jnp.bfloat16
    row = lambda v: v.reshape(1, -1)
    pad_g = GROUP_ROWS - N_GROUPS
    pad_e = ROUTER_LANES - ROUTER_ROWS
    wr = jnp.concatenate([router_group_w, jnp.zeros((D_MODEL, pad_g), jnp.float32),
                          router_expert_w, jnp.zeros((D_MODEL, pad_e), jnp.float32)], axis=1)
    br = jnp.concatenate([router_group_b, jnp.zeros((pad_g,), jnp.float32),
                          router_expert_b, jnp.zeros((pad_e,), jnp.float32)]).reshape(1, -1)
    wr_hi = wr.astype(bf16)
    wr_lo = (wr - wr_hi.astype(jnp.float32)).astype(bf16)

    h1, lt, wgu, wdb = _mixer_call(
        x, w_in.astype(bf16), rg_conv_w, row(rg_conv_b),
        _block_diag(rg_gate_a_w).astype(bf16), row(rg_gate_a_b),
        _block_diag(rg_gate_x_w).astype(bf16), row(rg_gate_x_b),
        row(rg_lambda), sc_conv_w, w_out.astype(bf16), row(ln1_g), row(ln1_b),
        jnp.concatenate([wr_hi, wr_lo], axis=1), br, exp_w_gate, exp_w_up, exp_w_down)

    info, meta, counts = _route_call(lt)
    class_start, tile_ea, tile_eb, tile_rows, step_group, step_block = _dispatch_plan(
        counts[:N_CLASSES, 0].astype(jnp.int32))

    row_tiles = (-1, LANE_TILES, LANES)
    xs, wts, pos = _sc_dispatch(h1.reshape(row_tiles), meta, info[0], info[1], class_start)

    grp_shape = (N_GROUPS, EXPERTS_PER_GROUP)
    ys = _expert_call(
        step_block, step_group, tile_ea, tile_eb, tile_rows, xs, wts,
        wgu.reshape(grp_shape + (D_MODEL, 2 * D_EXPERT)),
        wdb.reshape(grp_shape + (D_EXPERT, D_MODEL)),
        row(ln2_g), row(ln2_b))

    out_rows = pos.reshape(SEQ, BATCH).T.reshape(-1)
    out = _sc_gather_rows(ys, out_rows, COMBINE_CHUNK, (D_MODEL,))
    return out.reshape(BATCH, SEQ, D_MODEL)
```

```python
import functools

import jax
import jax.numpy as jnp
import numpy as np
from jax import lax
from jax.experimental import pallas as pl
from jax.experimental.pallas import tpu as pltpu
from jax.experimental.pallas import tpu_sc as plsc

D_MODEL = 1024
BATCH = 16
SEQ = 2048
D_RG = 512
D_SC = 512
RG_HEADS = 8
RG_HEAD_DIM = D_RG // RG_HEADS
RG_CONV = 4
RG_C = 8.0
SC_CONV = 3
D_IN_PROJ = 2 * D_RG + 3 * D_SC
N_GROUPS = 4
EXPERTS_PER_GROUP = 8
N_EXPERTS = N_GROUPS * EXPERTS_PER_GROUP
D_EXPERT = D_MODEL // 4
LN_EPS = 1e-5
DEEPNORM_ALPHA = 2.0 ** 0.25

N_TOKENS = BATCH * SEQ
LANES = 128
LANE_TILES = D_MODEL // LANES
MXU_WIDTH = 256
ROUTER_LANES = LANES
GROUP_ROWS = 8
ROUTER_ROWS = GROUP_ROWS + N_EXPERTS

TS = 64
M1 = TS * BATCH
DOT_ROWS = 256

ROUTE_BLOCK = 4096
ROUTE_UNROLL = 8

N_PAIRS = EXPERTS_PER_GROUP * (EXPERTS_PER_GROUP - 1) // 2
N_CLASSES = N_GROUPS * N_PAIRS
BF16_ROWS = 16
TM = -(-int(N_TOKENS / N_CLASSES * 1.09 / 2) // BF16_ROWS) * BF16_ROWS
SUB = 8
N_TILES = -(-(N_TOKENS // TM + N_CLASSES + N_GROUPS * (SUB - 1)) // SUB) * SUB
N_STEPS3 = N_TILES // SUB
P_ROWS = N_TILES * TM

VMEM_LIMIT = 56 * 1024 * 1024


def _layer_norm(z, g, b):
    mu = jnp.mean(z, axis=-1, keepdims=True)
    zc = z - mu
    var = jnp.mean(zc * zc, axis=-1, keepdims=True)
    return zc * lax.rsqrt(var + LN_EPS) * g + b


def _mixer_kernel(x_ref, w_in_ref, cw_ref, cb_ref, wa_ref, ba_ref, wx_ref, bx_ref, lam_ref,
                  scw_ref, w_out_ref, g1_ref, b1_ref, wr_ref, br_ref, wg_ref, wu_ref, wd_ref,
                  h1_hbm, logit_ref, wgu_ref, wdb_ref,
                  xbuf_ref, xsem, hbuf_ref, hsem, cbuf_ref, sbuf_ref, a_ref, u_ref, hstate_ref):
    c = pl.program_id(0)
    n_steps = SEQ // TS

    def h1_copies(step, slot_):
        return [pltpu.make_async_copy(hbuf_ref.at[slot_, j],
                                      h1_hbm.at[pl.ds(step * M1, M1), j, :], hsem.at[slot_])
                for j in range(LANE_TILES)]

    @pl.when(c >= 2)
    def _():
        for cp in h1_copies(c - 2, c % 2):
            cp.wait()

    wgu_ref[:, :, :D_EXPERT] = wg_ref[...].astype(jnp.bfloat16)
    wgu_ref[:, :, D_EXPERT:] = wu_ref[...].astype(jnp.bfloat16)
    wdb_ref[...] = wd_ref[...].astype(jnp.bfloat16)
    slot = c % 2
    rg_pad = (RG_CONV - 1) * BATCH
    sc_pad = (SC_CONV - 1) * BATCH

    def x_copies(chunk, slot_):
        return [pltpu.make_async_copy(x_ref.at[b, pl.ds(chunk * TS, TS), :],
                                      xbuf_ref.at[slot_, :, b, :], xsem.at[slot_])
                for b in range(BATCH)]

    @pl.when(c == 0)
    def _():
        for cp in x_copies(0, 0):
            cp.start()
        cbuf_ref[0:rg_pad, :] = jnp.zeros((rg_pad, D_RG), jnp.float32)
        sbuf_ref[0:sc_pad, :] = jnp.zeros((sc_pad, D_SC), jnp.float32)
        hstate_ref[...] = jnp.zeros_like(hstate_ref)

    @pl.when(c + 1 < SEQ // TS)
    def _():
        for cp in x_copies(c + 1, 1 - slot):
            cp.start()

    for cp in x_copies(c, slot):
        cp.wait()

    xt_ref = xbuf_ref.at[slot]

    def gate(xcb, w_ref, b_ref):
        parts = [jnp.dot(xcb[:, lo:lo + MXU_WIDTH], w_ref[lo:lo + MXU_WIDTH, lo:lo + MXU_WIDTH],
                         preferred_element_type=jnp.float32)
                 for lo in range(0, D_RG, MXU_WIDTH)]
        return 0.5 + 0.5 * jnp.tanh(0.5 * (jnp.concatenate(parts, axis=1) + b_ref[...]))

    soft_lam = jax.nn.softplus(-lam_ref[...])
    gelu_chunks, y_sc_chunks = [], []
    for r0 in range(0, M1, DOT_ROWS):
        t0 = r0 // BATCH
        xk = xt_ref[t0:t0 + DOT_ROWS // BATCH].reshape(DOT_ROWS, D_MODEL).astype(jnp.bfloat16)

        def proj(lo, width):
            return jnp.dot(xk, w_in_ref[:, lo:lo + width], preferred_element_type=jnp.float32)

        cbuf_ref[rg_pad + r0:rg_pad + r0 + DOT_ROWS, :] = proj(0, D_RG)
        rg_gate = proj(D_RG, D_RG)
        xc = cb_ref[...] + cw_ref[0:1, :] * cbuf_ref[r0:r0 + DOT_ROWS, :]
        for k in range(1, RG_CONV):
            xc = xc + cw_ref[k:k + 1, :] * cbuf_ref[r0 + k * BATCH:r0 + k * BATCH + DOT_ROWS, :]
        sc_b = proj(2 * D_RG, D_SC)
        xcb = xc.astype(jnp.bfloat16)
        r = gate(xcb, wa_ref, ba_ref)
        i = gate(xcb, wx_ref, bx_ref)
        sc_c = proj(2 * D_RG + D_SC, D_SC)
        log_a = (-RG_C) * r * soft_lam
        a_ref[r0:r0 + DOT_ROWS, :] = jnp.exp(log_a)
        th = jnp.tanh(log_a)
        u_ref[r0:r0 + DOT_ROWS, :] = jnp.sqrt(-2.0 * th / (1.0 - th)) * (i * xc)
        gelu_chunks.append(jax.nn.gelu(rg_gate))

        sbuf_ref[sc_pad + r0:sc_pad + r0 + DOT_ROWS, :] = sc_c * proj(2 * D_RG + 2 * D_SC, D_SC)
        conv = scw_ref[0:1, :] * sbuf_ref[r0:r0 + DOT_ROWS, :]
        for k in range(1, SC_CONV):
            conv = conv + scw_ref[k:k + 1, :] * sbuf_ref[r0 + k * BATCH:r0 + k * BATCH + DOT_ROWS, :]
        y_sc_chunks.append((sc_b * conv).astype(jnp.bfloat16))

    cbuf_ref[0:rg_pad, :] = cbuf_ref[M1:M1 + rg_pad, :]
    sbuf_ref[0:sc_pad, :] = sbuf_ref[M1:M1 + sc_pad, :]

    h = hstate_ref[...]
    for t in range(TS):
        h = a_ref[t * BATCH:(t + 1) * BATCH, :] * h + u_ref[t * BATCH:(t + 1) * BATCH, :]
        u_ref[t * BATCH:(t + 1) * BATCH, :] = h
    hstate_ref[...] = h

    tail_rows = M1 // 2
    per_tail = tail_rows // DOT_ROWS

    def out_proj(k):
        r0 = k * tail_rows
        gelu_k = jnp.concatenate(gelu_chunks[k * per_tail:(k + 1) * per_tail], axis=0)
        y_sc_k = jnp.concatenate(y_sc_chunks[k * per_tail:(k + 1) * per_tail], axis=0)
        y_rg = (u_ref[r0:r0 + tail_rows, :] * gelu_k).astype(jnp.bfloat16)
        mix = jnp.dot(y_rg, w_out_ref[0:D_RG, :], preferred_element_type=jnp.float32)
        return mix + jnp.dot(y_sc_k, w_out_ref[D_RG:, :], preferred_element_type=jnp.float32)

    def finish(k, mix):
        r0 = k * tail_rows
        t0 = r0 // BATCH
        x_rows = xt_ref[t0:t0 + tail_rows // BATCH].reshape(tail_rows, D_MODEL)
        h1 = _layer_norm(DEEPNORM_ALPHA * x_rows + mix, g1_ref[...], b1_ref[...])
        for t in range(LANE_TILES):
            hbuf_ref[slot, t, r0:r0 + tail_rows, :] = h1[:, t * LANES:(t + 1) * LANES]
        h_hi = h1.astype(jnp.bfloat16)
        h_lo = (h1 - h_hi.astype(jnp.float32)).astype(jnp.bfloat16)
        both = jnp.dot(h_hi, wr_ref[...], preferred_element_type=jnp.float32)
        logits = both[:, :ROUTER_LANES] + both[:, ROUTER_LANES:] + br_ref[...]
        logits = logits + jnp.dot(h_lo, wr_ref[:, :ROUTER_LANES],
                                  preferred_element_type=jnp.float32)
        logit_ref[:, r0:r0 + tail_rows] = logits.T[:ROUTER_ROWS, :]

    n_tails = M1 // tail_rows
    mix = out_proj(0)
    for k in range(n_tails):
        next_mix = out_proj(k + 1) if k + 1 < n_tails else None
        finish(k, mix)
        mix = next_mix

    for cp in h1_copies(c, slot):
        cp.start()

    @pl.when(c == n_steps - 1)
    def _():
        if n_steps >= 2:
            for cp in h1_copies(c - 1, 1 - slot):
                cp.wait()
        for cp in h1_copies(c, slot):
            cp.wait()


def _const_spec(shape):
    return pl.BlockSpec(shape, lambda c: (0,) * len(shape))


def _mixer_call(x, w_in, cw, cb, wa, ba, wx, bx, lam, scw, w_out, g1, b1, wr, br, wg, wu, wd):
    n_chunks = SEQ // TS
    epc = N_EXPERTS // n_chunks
    assert epc * n_chunks == N_EXPERTS
    in_specs = [
        pl.BlockSpec(memory_space=pl.ANY),
        _const_spec((D_MODEL, D_IN_PROJ)),
        _const_spec((RG_CONV, D_RG)), _const_spec((1, D_RG)),
        _const_spec((D_RG, D_RG)), _const_spec((1, D_RG)),
        _const_spec((D_RG, D_RG)), _const_spec((1, D_RG)),
        _const_spec((1, D_RG)),
        _const_spec((SC_CONV, D_SC)),
        _const_spec((D_MODEL, D_MODEL)),
        _const_spec((1, D_MODEL)), _const_spec((1, D_MODEL)),
        _const_spec((D_MODEL, 2 * ROUTER_LANES)), _const_spec((1, ROUTER_LANES)),
        pl.BlockSpec((epc, D_MODEL, D_EXPERT), lambda c: (c, 0, 0)),
        pl.BlockSpec((epc, D_MODEL, D_EXPERT), lambda c: (c, 0, 0)),
        pl.BlockSpec((epc, D_EXPERT, D_MODEL), lambda c: (c, 0, 0)),
    ]
    out_specs = [
        pl.BlockSpec(memory_space=pl.ANY),
        pl.BlockSpec((ROUTER_ROWS, M1), lambda c: (0, c)),
        pl.BlockSpec((epc, D_MODEL, 2 * D_EXPERT), lambda c: (c, 0, 0)),
        pl.BlockSpec((epc, D_EXPERT, D_MODEL), lambda c: (c, 0, 0)),
    ]
    return pl.pallas_call(
        _mixer_kernel,
        grid=(n_chunks,),
        in_specs=in_specs,
        out_specs=out_specs,
        out_shape=[
            jax.ShapeDtypeStruct((N_TOKENS, LANE_TILES, LANES), jnp.float32),
            jax.ShapeDtypeStruct((ROUTER_ROWS, N_TOKENS), jnp.float32),
            jax.ShapeDtypeStruct((N_EXPERTS, D_MODEL, 2 * D_EXPERT), jnp.bfloat16),
            jax.ShapeDtypeStruct((N_EXPERTS, D_EXPERT, D_MODEL), jnp.bfloat16),
        ],
        scratch_shapes=[
            pltpu.VMEM((2, TS, BATCH, D_MODEL), jnp.float32),
            pltpu.SemaphoreType.DMA((2,)),
            pltpu.VMEM((2, LANE_TILES, M1, LANES), jnp.float32),
            pltpu.SemaphoreType.DMA((2,)),
            pltpu.VMEM((M1 + (RG_CONV - 1) * BATCH, D_RG), jnp.float32),
            pltpu.VMEM((M1 + (SC_CONV - 1) * BATCH, D_SC), jnp.float32),
            pltpu.VMEM((M1, D_RG), jnp.float32),
            pltpu.VMEM((M1, D_RG), jnp.float32),
            pltpu.VMEM((BATCH, D_RG), jnp.float32),
        ],
        compiler_params=pltpu.CompilerParams(
            dimension_semantics=("arbitrary",), vmem_limit_bytes=VMEM_LIMIT),
        name="mixer_ln_router",
    )(x, w_in, cw, cb, wa, ba, wx, bx, lam, scw, w_out, g1, b1, wr, br, wg, wu, wd)


def _expert_kernel(blk_ref, grp_ref, ea_ref, eb_ref, rows_ref,
                   xs_hbm, wt_ref, wgu_ref, wd_ref, g2_ref, b2_ref, ys_hbm,
                   xbuf_ref, xsem, obuf_ref, osem):
    s = pl.program_id(0)
    slot = s % 2
    tile_rows = [rows_ref[s * SUB + j] for j in range(SUB)]
    block_rows = SUB * TM

    def fetches(step, slot_):
        return [pltpu.make_async_copy(xs_hbm.at[pl.ds(step * block_rows, block_rows), j, :],
                                      xbuf_ref.at[slot_, j], xsem.at[slot_])
                for j in range(LANE_TILES)]

    def write_backs(step, slot_):
        return [pltpu.make_async_copy(obuf_ref.at[slot_, j],
                                      ys_hbm.at[pl.ds(step * block_rows, block_rows), j, :],
                                      osem.at[slot_])
                for j in range(LANE_TILES)]

    def did_work(step):
        return blk_ref[jnp.clip(step, 0, N_STEPS3 - 1)] == step

    @pl.when(jnp.logical_and(s == 0, did_work(0)))
    def _():
        for cp in fetches(0, 0):
            cp.start()

    @pl.when(did_work(s + 1))
    def _():
        for cp in fetches(s + 1, 1 - slot):
            cp.start()

    @pl.when(did_work(s - 2))
    def _():
        for cp in write_backs(s - 2, slot):
            cp.wait()

    @pl.when(did_work(s))
    def _():
        for cp in fetches(s, slot):
            cp.wait()
        for j in range(SUB):
            tile = s * SUB + j
            first_row = j * TM
            in_use = lax.broadcasted_iota(jnp.int32, (TM, 1), 0) < tile_rows[j]
            x = jnp.concatenate([xbuf_ref[slot, t, first_row:first_row + TM, :]
                                 for t in range(LANE_TILES)], axis=1)
            x = jnp.where(in_use, x, 0.0)
            xb = x.astype(jnp.bfloat16)
            wt = jnp.where(in_use, wt_ref[first_row:first_row + TM, :], 0.0)
            y = jnp.zeros((TM, D_MODEL), jnp.float32)
            for e_ref, col in ((ea_ref, 0), (eb_ref, 1)):
                e = e_ref[tile]
                hgu = jnp.dot(xb, wgu_ref[e], preferred_element_type=jnp.float32)
                hid = jax.nn.silu(hgu[:, :D_EXPERT]) * hgu[:, D_EXPERT:] * wt[:, col:col + 1]
                y = y + jnp.dot(hid.astype(jnp.bfloat16), wd_ref[e],
                                preferred_element_type=jnp.float32)
            out = _layer_norm(DEEPNORM_ALPHA * x + y, g2_ref[...], b2_ref[...])
            for t in range(LANE_TILES):
                obuf_ref[slot, t, first_row:first_row + TM, :] = out[:, t * LANES:(t + 1) * LANES]
        for cp in write_backs(s, slot):
            cp.start()

    @pl.when(s == N_STEPS3 - 1)
    def _():
        @pl.when(did_work(s - 1))
        def _():
            for cp in write_backs(s - 1, 1 - slot):
                cp.wait()

        @pl.when(did_work(s))
        def _():
            for cp in write_backs(s, slot):
                cp.wait()


def _expert_call(step_block, step_group, tile_ea, tile_eb, tile_rows, xs, wts, wgu, wd, g2, b2):
    rows = SUB * TM
    grid_spec = pltpu.PrefetchScalarGridSpec(
        num_scalar_prefetch=5,
        grid=(N_STEPS3,),
        in_specs=[
            pl.BlockSpec(memory_space=pl.ANY),
            pl.BlockSpec((rows, LANES), lambda s, blk, grp, ea, eb, va: (blk[s], 0)),
            pl.BlockSpec((None, EXPERTS_PER_GROUP, D_MODEL, 2 * D_EXPERT),
                         lambda s, blk, grp, ea, eb, va: (grp[s], 0, 0, 0)),
            pl.BlockSpec((None, EXPERTS_PER_GROUP, D_EXPERT, D_MODEL),
                         lambda s, blk, grp, ea, eb, va: (grp[s], 0, 0, 0)),
            pl.BlockSpec((1, D_MODEL), lambda s, blk, grp, ea, eb, va: (0, 0)),
            pl.BlockSpec((1, D_MODEL), lambda s, blk, grp, ea, eb, va: (0, 0)),
        ],
        out_specs=pl.BlockSpec(memory_space=pl.ANY),
        scratch_shapes=[
            pltpu.VMEM((2, LANE_TILES, rows, LANES), jnp.float32),
            pltpu.SemaphoreType.DMA((2,)),
            pltpu.VMEM((2, LANE_TILES, rows, LANES), jnp.float32),
            pltpu.SemaphoreType.DMA((2,)),
        ],
    )
    return pl.pallas_call(
        _expert_kernel,
        grid_spec=grid_spec,
        out_shape=jax.ShapeDtypeStruct((P_ROWS, LANE_TILES, LANES), jnp.float32),
        compiler_params=pltpu.CompilerParams(
            dimension_semantics=("arbitrary",), vmem_limit_bytes=VMEM_LIMIT),
        name="experts_ln",
    )(step_block, step_group, tile_ea, tile_eb, tile_rows, xs, wts, wgu, wd, g2, b2)


SC_CORES = 2
SC_SUBCORES = 16
SC_WORKERS = SC_CORES * SC_SUBCORES
COMBINE_CHUNK = 32


def _sc_gather_rows(table, idx, chunk, out_row_shape=None):
    n_out, = idx.shape
    row_shape = table.shape[1:]
    out_row_shape = row_shape if out_row_shape is None else out_row_shape
    per_w = n_out // SC_WORKERS
    n_chunks = per_w // chunk
    assert per_w * SC_WORKERS == n_out and n_chunks * chunk == per_w
    assert n_chunks % 2 == 0 and chunk % 8 == 0
    mesh = plsc.VectorSubcoreMesh(core_axis_name="c", subcore_axis_name="s")

    @functools.partial(
        pl.kernel, mesh=mesh,
        out_type=jax.ShapeDtypeStruct((n_out,) + out_row_shape, table.dtype),
        scratch_types=[
            pltpu.VMEM((per_w,), jnp.int32),
            pltpu.VMEM((2, chunk) + row_shape, table.dtype),
            pltpu.SemaphoreType.DMA((2,)),
            pltpu.SemaphoreType.DMA((2,)),
        ],
    )
    def gather_kernel(table_hbm, idx_hbm, out_hbm, idx_v, buf, gsem, wsem):
        wid = lax.axis_index("s") * SC_CORES + lax.axis_index("c")
        base = wid * per_w
        pltpu.sync_copy(idx_hbm.at[pl.ds(base, per_w)], idx_v)

        def gather(j, slot):
            rows = idx_v.at[pl.ds(j * chunk, chunk)]
            return pltpu.make_async_copy(table_hbm.at[rows], buf.at[slot], gsem.at[slot])

        def write(j, slot):
            dst = out_hbm.at[pl.ds(base + j * chunk, chunk)]
            src = buf.at[slot].reshape((chunk,) + out_row_shape)
            return pltpu.make_async_copy(src, dst, wsem.at[slot])

        gather(0, 0).start()

        @pl.loop(0, n_chunks, step=2)
        def _(j0):
            for slot in range(2):
                j = j0 + slot
                gather(j, slot).wait()

                @pl.when(j >= 1)
                def _():
                    write(j - 1, 1 - slot).wait()

                @pl.when(j + 1 < n_chunks)
                def _():
                    gather(j + 1, 1 - slot).start()

                write(j, slot).start()

        write(n_chunks - 1, (n_chunks - 1) % 2).wait()

    return gather_kernel(table, idx)


SC_LANES = 16


def _sc_dispatch(rows, meta, cls, rank, class_start):
    n_rows = rows.shape[0]
    chunk = COMBINE_CHUNK
    per_w = n_rows // SC_WORKERS
    n_chunks = per_w // chunk
    assert per_w * SC_WORKERS == n_rows and n_chunks * chunk == per_w
    assert n_chunks % 2 == 0 and chunk % SC_LANES == 0
    mesh = plsc.VectorSubcoreMesh(core_axis_name="c", subcore_axis_name="s")

    @functools.partial(
        pl.kernel, mesh=mesh,
        out_type=[
            jax.ShapeDtypeStruct((P_ROWS,) + rows.shape[1:], rows.dtype),
            jax.ShapeDtypeStruct((P_ROWS,) + meta.shape[1:], meta.dtype),
            jax.ShapeDtypeStruct((n_rows,), jnp.int32),
        ],
        scratch_types=[
            pltpu.VMEM((per_w,), jnp.int32),
            pltpu.VMEM((per_w,), jnp.int32),
            pltpu.VMEM((LANES,), jnp.int32),
            pltpu.VMEM((per_w,), jnp.int32),
            pltpu.VMEM((n_chunks, chunk), jnp.int32),
            pltpu.VMEM((2, chunk) + rows.shape[1:], rows.dtype),
            pltpu.VMEM((2, chunk) + meta.shape[1:], meta.dtype),
            pltpu.SemaphoreType.DMA((2,)),
            pltpu.SemaphoreType.DMA((2,)),
            pltpu.SemaphoreType.DMA((2,)),
            pltpu.SemaphoreType.DMA((2,)),
        ],
        compiler_params=pltpu.CompilerParams(needs_layout_passes=False),
    )
    def dispatch_kernel(rows_hbm, meta_hbm, cls_hbm, rank_hbm, start_hbm,
                        xs_hbm, ws_hbm, pos_hbm,
                        cls_v, rank_v, start_v, pos_flat, pos_v, rbuf, mbuf,
                        rsem, msem, xsem, wsem):
        wid = lax.axis_index("s") * SC_CORES + lax.axis_index("c")
        base = wid * per_w
        pltpu.sync_copy(cls_hbm.at[pl.ds(base, per_w)], cls_v)
        pltpu.sync_copy(rank_hbm.at[pl.ds(base, per_w)], rank_v)
        pltpu.sync_copy(start_hbm, start_v)

        @pl.loop(0, n_chunks)
        def _(j):
            for k in range(chunk // SC_LANES):
                off = j * chunk + k * SC_LANES
                c = cls_v[pl.ds(off, SC_LANES)]
                pos = plsc.load_gather(start_v, [c]) + rank_v[pl.ds(off, SC_LANES)]
                pos_flat[pl.ds(off, SC_LANES)] = pos
                pos_v[j, pl.ds(k * SC_LANES, SC_LANES)] = pos

        pltpu.sync_copy(pos_flat, pos_hbm.at[pl.ds(base, per_w)])

        def read_rows(j, slot):
            src = rows_hbm.at[pl.ds(base + j * chunk, chunk)]
            return pltpu.make_async_copy(src, rbuf.at[slot], rsem.at[slot])

        def read_meta(j, slot):
            src = meta_hbm.at[pl.ds(base + j * chunk, chunk)]
            return pltpu.make_async_copy(src, mbuf.at[slot], msem.at[slot])

        def put_rows(j, slot):
            return pltpu.make_async_copy(rbuf.at[slot], xs_hbm.at[pos_v.at[j]], xsem.at[slot])

        def put_meta(j, slot):
            return pltpu.make_async_copy(mbuf.at[slot], ws_hbm.at[pos_v.at[j]], wsem.at[slot])

        read_rows(0, 0).start()
        read_meta(0, 0).start()

        @pl.loop(0, n_chunks, step=2)
        def _(j0):
            for slot in range(2):
                j = j0 + slot
                read_rows(j, slot).wait()
                read_meta(j, slot).wait()

                @pl.when(j >= 1)
                def _():
                    put_rows(j - 1, 1 - slot).wait()
                    put_meta(j - 1, 1 - slot).wait()

                @pl.when(j + 1 < n_chunks)
                def _():
                    read_rows(j + 1, 1 - slot).start()
                    read_meta(j + 1, 1 - slot).start()

                put_rows(j, slot).start()
                put_meta(j, slot).start()

        put_rows(n_chunks - 1, (n_chunks - 1) % 2).wait()
        put_meta(n_chunks - 1, (n_chunks - 1) % 2).wait()

    return dispatch_kernel(rows, meta, cls, rank, class_start)


def _block_diag(w):
    h, d, _ = w.shape
    eye = jnp.eye(h, dtype=w.dtype)
    return (eye[:, None, :, None] * w[:, :, None, :]).reshape(h * d, h * d)


def _pair_tables():
    pair_of = np.zeros((EXPERTS_PER_GROUP, EXPERTS_PER_GROUP), np.int32)
    lo = np.zeros((N_PAIRS,), np.int32)
    hi = np.zeros((N_PAIRS,), np.int32)
    p = 0
    for a in range(EXPERTS_PER_GROUP):
        for b in range(a + 1, EXPERTS_PER_GROUP):
            pair_of[a, b] = pair_of[b, a] = p
            lo[p], hi[p] = a, b
            p += 1
    return pair_of, lo, hi


def _route_kernel(lt_ref, info_ref, meta_ref, counts_ref, run_ref):
    step = pl.program_id(0)

    @pl.when(step == 0)
    def _():
        run_ref[...] = jnp.zeros_like(run_ref)

    f32 = jnp.float32
    sub8 = lax.broadcasted_iota(jnp.int32, (8, LANES), 0).astype(f32)
    row_id = lax.broadcasted_iota(jnp.int32, (LANES, LANES), 0)
    col_id = lax.broadcasted_iota(jnp.int32, (LANES, LANES), 1)
    prefix_mat = (row_id <= col_id).astype(jnp.bfloat16)
    ones_mat = jnp.ones((LANES, LANES), jnp.bfloat16)
    neg_inf = f32(-jnp.inf)

    def first_index_of_max(v):
        m = jnp.max(v, axis=0, keepdims=True)
        idx = jnp.min(jnp.where(v == m, sub8, f32(8)), axis=0, keepdims=True)
        return m, idx

    def lane_tile(k, run):
        lanes = pl.ds(pl.multiple_of(k * LANES, LANES), LANES)
        g = jnp.where(sub8 < N_GROUPS, lt_ref[0:GROUP_ROWS, lanes], neg_inf)
        g_max, g_idx = first_index_of_max(g)
        g_top_p = 1.0 / jnp.sum(jnp.exp(g - g_max), axis=0, keepdims=True)

        e_sel = lt_ref[GROUP_ROWS:GROUP_ROWS + EXPERTS_PER_GROUP, lanes]
        for grp in range(1, N_GROUPS):
            lo = GROUP_ROWS + grp * EXPERTS_PER_GROUP
            e_sel = jnp.where(g_idx == grp, lt_ref[lo:lo + EXPERTS_PER_GROUP, lanes], e_sel)
        m1, i1 = first_index_of_max(e_sel)
        rest = jnp.where(sub8 == i1, neg_inf, e_sel)
        m2 = jnp.max(rest, axis=0, keepdims=True)
        i2 = jnp.min(jnp.where((rest == m2) & (sub8 != i1), sub8, f32(8)), axis=0, keepdims=True)

        e = jnp.exp(m2 - m1)
        w1 = g_top_p / (1.0 + e)
        w2 = g_top_p * e / (1.0 + e)
        first_is_lo = i1 < i2
        w_lo = jnp.where(first_is_lo, w1, w2)
        w_hi = jnp.where(first_is_lo, w2, w1)
        lo_e = jnp.minimum(i1, i2)
        hi_e = jnp.maximum(i1, i2)
        pair = lo_e * (2 * EXPERTS_PER_GROUP - 1 - lo_e) * 0.5 + hi_e - lo_e - 1.0
        cls = (g_idx * N_PAIRS + pair).astype(jnp.int32)

        onehot = (row_id == cls).astype(jnp.bfloat16)
        csum = jnp.dot(onehot, prefix_mat, preferred_element_type=f32)
        rank = jnp.sum(onehot.astype(f32) * (csum + run), axis=0, keepdims=True) - 1.0
        run = run + jnp.dot(onehot, ones_mat, preferred_element_type=f32)

        info_ref[:, lanes] = jnp.where(sub8 == 0, cls, jnp.where(sub8 == 1, rank.astype(jnp.int32), 0))
        meta8 = jnp.where(sub8 == 0, w_lo, jnp.where(sub8 == 1, w_hi, 0.0))
        meta_t = jnp.concatenate([meta8, jnp.zeros((LANES - 8, LANES), f32)], axis=0)
        meta_ref[lanes, :] = meta_t.T
        return run

    run = lax.fori_loop(0, ROUTE_BLOCK // LANES, lane_tile, run_ref[...], unroll=ROUTE_UNROLL)
    run_ref[...] = run
    counts_ref[...] = run


def _route_call(lt):
    return pl.pallas_call(
        _route_kernel,
        grid=(N_TOKENS // ROUTE_BLOCK,),
        in_specs=[pl.BlockSpec((ROUTER_ROWS, ROUTE_BLOCK), lambda i: (0, i))],
        out_specs=[
            pl.BlockSpec((8, ROUTE_BLOCK), lambda i: (0, i)),
            pl.BlockSpec((ROUTE_BLOCK, LANES), lambda i: (i, 0)),
            pl.BlockSpec((LANES, LANES), lambda i: (0, 0)),
        ],
        out_shape=[
            jax.ShapeDtypeStruct((8, N_TOKENS), jnp.int32),
            jax.ShapeDtypeStruct((N_TOKENS, LANES), jnp.float32),
            jax.ShapeDtypeStruct((LANES, LANES), jnp.float32),
        ],
        scratch_shapes=[pltpu.VMEM((LANES, LANES), jnp.float32)],
        compiler_params=pltpu.CompilerParams(dimension_semantics=("arbitrary",)),
        name="route_rank",
    )(lt)


def _dispatch_plan(counts):
    _, pair_lo, pair_hi = _pair_tables()
    tiles_c = (counts + TM - 1) // TM
    tiles_g = tiles_c.reshape(N_GROUPS, N_PAIRS).sum(axis=1)
    tiles_g_pad = (tiles_g + SUB - 1) // SUB * SUB
    g_start = jnp.cumsum(tiles_g_pad) - tiles_g_pad
    tc = tiles_c.reshape(N_GROUPS, N_PAIRS)
    c_start = (g_start[:, None] + jnp.cumsum(tc, axis=1) - tc).reshape(N_CLASSES)
    class_start = jnp.zeros((LANES,), jnp.int32).at[:N_CLASSES].set(c_start * TM)

    tile_ids = jnp.arange(N_TILES, dtype=jnp.int32)
    c_end = c_start + tiles_c
    owner = jnp.sum((tile_ids[:, None] >= c_end[None, :]).astype(jnp.int32), axis=1)
    owner = jnp.minimum(owner, N_CLASSES - 1)
    in_class = (tile_ids >= c_start[owner]) & (tile_ids < c_end[owner])
    rows_left = counts[owner] - (tile_ids - c_start[owner]) * TM
    tile_rows = jnp.where(in_class, jnp.minimum(rows_left, TM), 0).astype(jnp.int32)
    pair = owner % N_PAIRS
    tile_ea = jnp.asarray(pair_lo)[pair]
    tile_eb = jnp.asarray(pair_hi)[pair]
    g_end = g_start + tiles_g_pad
    step_first = jnp.arange(N_STEPS3, dtype=jnp.int32) * SUB
    step_group = jnp.sum((step_first[:, None] >= g_end[None, :]).astype(jnp.int32), axis=1)
    step_group = jnp.minimum(step_group, N_GROUPS - 1)
    last_step = jnp.maximum(g_end[-1] // SUB - 1, 0)
    step_block = jnp.minimum(jnp.arange(N_STEPS3, dtype=jnp.int32), last_step)
    return (class_start, tile_ea, tile_eb, tile_rows, step_group.astype(jnp.int32),
            step_block.astype(jnp.int32))


def kernel(x, w_in, rg_conv_w, rg_conv_b, rg_gate_a_w, rg_gate_a_b, rg_gate_x_w, rg_gate_x_b,
           rg_lambda, sc_conv_w, w_out, ln1_g, ln1_b, router_group_w, router_group_b,
           router_expert_w, router_expert_b, exp_w_gate, exp_w_up, exp_w_down, ln2_g, ln2_b):
    bf16 = jnp.bfloat16
    row = lambda v: v.reshape(1, -1)
    pad_g = GROUP_ROWS - N_GROUPS
    pad_e = ROUTER_LANES - ROUTER_ROWS
    wr = jnp.concatenate([router_group_w, jnp.zeros((D_MODEL, pad_g), jnp.float32),
                          router_expert_w, jnp.zeros((D_MODEL, pad_e), jnp.float32)], axis=1)
    br = jnp.concatenate([router_group_b, jnp.zeros((pad_g,), jnp.float32),
                          router_expert_b, jnp.zeros((pad_e,), jnp.float32)]).reshape(1, -1)
    wr_hi = wr.astype(bf16)
    wr_lo = (wr - wr_hi.astype(jnp.float32)).astype(bf16)

    h1, lt, wgu, wdb = _mixer_call(
        x, w_in.astype(bf16), rg_conv_w, row(rg_conv_b),
        _block_diag(rg_gate_a_w).astype(bf16), row(rg_gate_a_b),
        _block_diag(rg_gate_x_w).astype(bf16), row(rg_gate_x_b),
        row(rg_lambda), sc_conv_w, w_out.astype(bf16), row(ln1_g), row(ln1_b),
        jnp.concatenate([wr_hi, wr_lo], axis=1), br, exp_w_gate, exp_w_up, exp_w_down)

    info, meta, counts = _route_call(lt)
    class_start, tile_ea, tile_eb, tile_rows, step_group, step_block = _dispatch_plan(
        counts[:N_CLASSES, 0].astype(jnp.int32))

    xs, wts, pos = _sc_dispatch(h1, meta, info[0], info[1], class_start)

    grp_shape = (N_GROUPS, EXPERTS_PER_GROUP)
    ys = _expert_call(
        step_block, step_group, tile_ea, tile_eb, tile_rows, xs, wts,
        wgu.reshape(grp_shape + (D_MODEL, 2 * D_EXPERT)),
        wdb.reshape(grp_shape + (D_EXPERT, D_MODEL)),
        row(ln2_g), row(ln2_b))

    out_rows = pos.reshape(SEQ, BATCH).T.reshape(-1)
    out = _sc_gather_rows(ys, out_rows, COMBINE_CHUNK, (D_MODEL,))
    return out.reshape(BATCH, SEQ, D_MODEL)
```

```python
import functools

import jax
import jax.numpy as jnp
import numpy as np
from jax import lax
from jax.experimental import pallas as pl
from jax.experimental.pallas import tpu as pltpu
from jax.experimental.pallas import tpu_sc as plsc

D_MODEL = 1024
BATCH = 16
SEQ = 2048
D_RG = 512
D_SC = 512
RG_HEADS = 8
RG_HEAD_DIM = D_RG // RG_HEADS
RG_CONV = 4
RG_C = 8.0
SC_CONV = 3
D_IN_PROJ = 2 * D_RG + 3 * D_SC
N_GROUPS = 4
EXPERTS_PER_GROUP = 8
N_EXPERTS = N_GROUPS * EXPERTS_PER_GROUP
D_EXPERT = D_MODEL // 4
LN_EPS = 1e-5
DEEPNORM_ALPHA = 2.0 ** 0.25

N_TOKENS = BATCH * SEQ
LANES = 128
LANE_TILES = D_MODEL // LANES
MXU_WIDTH = 256
ROUTER_LANES = LANES
GROUP_ROWS = 8
ROUTER_ROWS = GROUP_ROWS + N_EXPERTS

TS = 64
M1 = TS * BATCH
DOT_ROWS = 256

ROUTE_BLOCK = 4096
ROUTE_UNROLL = 8

N_PAIRS = EXPERTS_PER_GROUP * (EXPERTS_PER_GROUP - 1) // 2
N_CLASSES = N_GROUPS * N_PAIRS
BF16_ROWS = 16
TM = -(-int(N_TOKENS / N_CLASSES * 1.09 / 2) // BF16_ROWS) * BF16_ROWS
SUB = 8
N_TILES = -(-(N_TOKENS // TM + N_CLASSES + N_GROUPS * (SUB - 1)) // SUB) * SUB
N_STEPS3 = N_TILES // SUB
P_ROWS = N_TILES * TM

VMEM_LIMIT = 56 * 1024 * 1024


def _layer_norm(z, g, b):
    mu = jnp.mean(z, axis=-1, keepdims=True)
    zc = z - mu
    var = jnp.mean(zc * zc, axis=-1, keepdims=True)
    return zc * lax.rsqrt(var + LN_EPS) * g + b


def _mixer_kernel(x_ref, w_in_ref, cw_ref, cb_ref, wa_heads_ref, ba_ref, wx_heads_ref, bx_ref,
                  lam_ref, scw_ref, w_out_ref, g1_ref, b1_ref, wr_ref, br_ref,
                  wg_ref, wu_ref, wd_ref,
                  h1_hbm, logit_ref, wgu_ref, wdb_ref,
                  xbuf_ref, xsem, hbuf_ref, hsem, cbuf_ref, sbuf_ref, a_ref, u_ref, hstate_ref,
                  wa_ref, wx_ref):
    c = pl.program_id(0)
    n_steps = SEQ // TS

    def h1_copies(step, slot_):
        return [pltpu.make_async_copy(hbuf_ref.at[slot_, j],
                                      h1_hbm.at[pl.ds(step * M1, M1), j, :], hsem.at[slot_])
                for j in range(LANE_TILES)]

    @pl.when(c >= 2)
    def _():
        for cp in h1_copies(c - 2, c % 2):
            cp.wait()

    wgu_ref[:, :, :D_EXPERT] = wg_ref[...].astype(jnp.bfloat16)
    wgu_ref[:, :, D_EXPERT:] = wu_ref[...].astype(jnp.bfloat16)
    wdb_ref[...] = wd_ref[...].astype(jnp.bfloat16)
    slot = c % 2
    rg_pad = (RG_CONV - 1) * BATCH
    sc_pad = (SC_CONV - 1) * BATCH

    def x_copies(chunk, slot_):
        return [pltpu.make_async_copy(x_ref.at[b, pl.ds(chunk * TS, TS), :],
                                      xbuf_ref.at[slot_, :, b, :], xsem.at[slot_])
                for b in range(BATCH)]

    @pl.when(c == 0)
    def _():
        for cp in x_copies(0, 0):
            cp.start()
        cbuf_ref[0:rg_pad, :] = jnp.zeros((rg_pad, D_RG), jnp.float32)
        sbuf_ref[0:sc_pad, :] = jnp.zeros((sc_pad, D_SC), jnp.float32)
        hstate_ref[...] = jnp.zeros_like(hstate_ref)
        for heads_ref, bd_ref in ((wa_heads_ref, wa_ref), (wx_heads_ref, wx_ref)):
            bd_ref[...] = jnp.zeros_like(bd_ref)
            for hd in range(RG_HEADS):
                lo = hd * RG_HEAD_DIM
                bd_ref[lo:lo + RG_HEAD_DIM, lo:lo + RG_HEAD_DIM] = (
                    heads_ref[hd].astype(jnp.bfloat16))

    @pl.when(c + 1 < SEQ // TS)
    def _():
        for cp in x_copies(c + 1, 1 - slot):
            cp.start()

    for cp in x_copies(c, slot):
        cp.wait()

    xt_ref = xbuf_ref.at[slot]

    def gate(xcb, w_ref, b_ref):
        parts = [jnp.dot(xcb[:, lo:lo + MXU_WIDTH], w_ref[lo:lo + MXU_WIDTH, lo:lo + MXU_WIDTH],
                         preferred_element_type=jnp.float32)
                 for lo in range(0, D_RG, MXU_WIDTH)]
        return 0.5 + 0.5 * jnp.tanh(0.5 * (jnp.concatenate(parts, axis=1) + b_ref[...]))

    soft_lam = jax.nn.softplus(-lam_ref[...])
    gelu_chunks, y_sc_chunks = [], []
    for r0 in range(0, M1, DOT_ROWS):
        t0 = r0 // BATCH
        xk = xt_ref[t0:t0 + DOT_ROWS // BATCH].reshape(DOT_ROWS, D_MODEL).astype(jnp.bfloat16)

        def proj(lo, width):
            return jnp.dot(xk, w_in_ref[:, lo:lo + width], preferred_element_type=jnp.float32)

        cbuf_ref[rg_pad + r0:rg_pad + r0 + DOT_ROWS, :] = proj(0, D_RG)
        rg_gate = proj(D_RG, D_RG)
        xc = cb_ref[...] + cw_ref[0:1, :] * cbuf_ref[r0:r0 + DOT_ROWS, :]
        for k in range(1, RG_CONV):
            xc = xc + cw_ref[k:k + 1, :] * cbuf_ref[r0 + k * BATCH:r0 + k * BATCH + DOT_ROWS, :]
        sc_b = proj(2 * D_RG, D_SC)
        xcb = xc.astype(jnp.bfloat16)
        r = gate(xcb, wa_ref, ba_ref)
        i = gate(xcb, wx_ref, bx_ref)
        sc_c = proj(2 * D_RG + D_SC, D_SC)
        log_a = (-RG_C) * r * soft_lam
        a_ref[r0:r0 + DOT_ROWS, :] = jnp.exp(log_a)
        th = jnp.tanh(log_a)
        u_ref[r0:r0 + DOT_ROWS, :] = jnp.sqrt(-2.0 * th / (1.0 - th)) * (i * xc)
        gelu_chunks.append(jax.nn.gelu(rg_gate))

        sbuf_ref[sc_pad + r0:sc_pad + r0 + DOT_ROWS, :] = sc_c * proj(2 * D_RG + 2 * D_SC, D_SC)
        conv = scw_ref[0:1, :] * sbuf_ref[r0:r0 + DOT_ROWS, :]
        for k in range(1, SC_CONV):
            conv = conv + scw_ref[k:k + 1, :] * sbuf_ref[r0 + k * BATCH:r0 + k * BATCH + DOT_ROWS, :]
        y_sc_chunks.append((sc_b * conv).astype(jnp.bfloat16))

    cbuf_ref[0:rg_pad, :] = cbuf_ref[M1:M1 + rg_pad, :]
    sbuf_ref[0:sc_pad, :] = sbuf_ref[M1:M1 + sc_pad, :]

    h = hstate_ref[...]
    for t in range(TS):
        h = a_ref[t * BATCH:(t + 1) * BATCH, :] * h + u_ref[t * BATCH:(t + 1) * BATCH, :]
        u_ref[t * BATCH:(t + 1) * BATCH, :] = h
    hstate_ref[...] = h

    tail_rows = M1 // 2
    per_tail = tail_rows // DOT_ROWS

    def out_proj(k):
        r0 = k * tail_rows
        gelu_k = jnp.concatenate(gelu_chunks[k * per_tail:(k + 1) * per_tail], axis=0)
        y_sc_k = jnp.concatenate(y_sc_chunks[k * per_tail:(k + 1) * per_tail], axis=0)
        y_rg = (u_ref[r0:r0 + tail_rows, :] * gelu_k).astype(jnp.bfloat16)
        mix = jnp.dot(y_rg, w_out_ref[0:D_RG, :], preferred_element_type=jnp.float32)
        return mix + jnp.dot(y_sc_k, w_out_ref[D_RG:, :], preferred_element_type=jnp.float32)

    def finish(k, mix):
        r0 = k * tail_rows
        t0 = r0 // BATCH
        x_rows = xt_ref[t0:t0 + tail_rows // BATCH].reshape(tail_rows, D_MODEL)
        h1 = _layer_norm(DEEPNORM_ALPHA * x_rows + mix, g1_ref[...], b1_ref[...])
        for t in range(LANE_TILES):
            hbuf_ref[slot, t, r0:r0 + tail_rows, :] = h1[:, t * LANES:(t + 1) * LANES]
        h_hi = h1.astype(jnp.bfloat16)
        h_lo = (h1 - h_hi.astype(jnp.float32)).astype(jnp.bfloat16)
        both = jnp.dot(h_hi, wr_ref[...], preferred_element_type=jnp.float32)
        logits = both[:, :ROUTER_LANES] + both[:, ROUTER_LANES:] + br_ref[...]
        logits = logits + jnp.dot(h_lo, wr_ref[:, :ROUTER_LANES],
                                  preferred_element_type=jnp.float32)
        logit_ref[:, r0:r0 + tail_rows] = logits.T[:ROUTER_ROWS, :]

    n_tails = M1 // tail_rows
    mix = out_proj(0)
    for k in range(n_tails):
        next_mix = out_proj(k + 1) if k + 1 < n_tails else None
        finish(k, mix)
        mix = next_mix

    for cp in h1_copies(c, slot):
        cp.start()

    @pl.when(c == n_steps - 1)
    def _():
        if n_steps >= 2:
            for cp in h1_copies(c - 1, 1 - slot):
                cp.wait()
        for cp in h1_copies(c, slot):
            cp.wait()


def _const_spec(shape):
    return pl.BlockSpec(shape, lambda c: (0,) * len(shape))


def _mixer_call(x, w_in, cw, cb, wa, ba, wx, bx, lam, scw, w_out, g1, b1, wr, br, wg, wu, wd):
    n_chunks = SEQ // TS
    epc = N_EXPERTS // n_chunks
    assert epc * n_chunks == N_EXPERTS
    in_specs = [
        pl.BlockSpec(memory_space=pl.ANY),
        _const_spec((D_MODEL, D_IN_PROJ)),
        _const_spec((RG_CONV, D_RG)), _const_spec((1, D_RG)),
        _const_spec((RG_HEADS, RG_HEAD_DIM, RG_HEAD_DIM)), _const_spec((1, D_RG)),
        _const_spec((RG_HEADS, RG_HEAD_DIM, RG_HEAD_DIM)), _const_spec((1, D_RG)),
        _const_spec((1, D_RG)),
        _const_spec((SC_CONV, D_SC)),
        _const_spec((D_MODEL, D_MODEL)),
        _const_spec((1, D_MODEL)), _const_spec((1, D_MODEL)),
        _const_spec((D_MODEL, 2 * ROUTER_LANES)), _const_spec((1, ROUTER_LANES)),
        pl.BlockSpec((epc, D_MODEL, D_EXPERT), lambda c: (c, 0, 0)),
        pl.BlockSpec((epc, D_MODEL, D_EXPERT), lambda c: (c, 0, 0)),
        pl.BlockSpec((epc, D_EXPERT, D_MODEL), lambda c: (c, 0, 0)),
    ]
    out_specs = [
        pl.BlockSpec(memory_space=pl.ANY),
        pl.BlockSpec((ROUTER_ROWS, M1), lambda c: (0, c)),
        pl.BlockSpec((epc, D_MODEL, 2 * D_EXPERT), lambda c: (c, 0, 0)),
        pl.BlockSpec((epc, D_EXPERT, D_MODEL), lambda c: (c, 0, 0)),
    ]
    return pl.pallas_call(
        _mixer_kernel,
        grid=(n_chunks,),
        in_specs=in_specs,
        out_specs=out_specs,
        out_shape=[
            jax.ShapeDtypeStruct((N_TOKENS, LANE_TILES, LANES), jnp.float32),
            jax.ShapeDtypeStruct((ROUTER_ROWS, N_TOKENS), jnp.float32),
            jax.ShapeDtypeStruct((N_EXPERTS, D_MODEL, 2 * D_EXPERT), jnp.bfloat16),
            jax.ShapeDtypeStruct((N_EXPERTS, D_EXPERT, D_MODEL), jnp.bfloat16),
        ],
        scratch_shapes=[
            pltpu.VMEM((2, TS, BATCH, D_MODEL), jnp.float32),
            pltpu.SemaphoreType.DMA((2,)),
            pltpu.VMEM((2, LANE_TILES, M1, LANES), jnp.float32),
            pltpu.SemaphoreType.DMA((2,)),
            pltpu.VMEM((M1 + (RG_CONV - 1) * BATCH, D_RG), jnp.float32),
            pltpu.VMEM((M1 + (SC_CONV - 1) * BATCH, D_SC), jnp.float32),
            pltpu.VMEM((M1, D_RG), jnp.float32),
            pltpu.VMEM((M1, D_RG), jnp.float32),
            pltpu.VMEM((BATCH, D_RG), jnp.float32),
            pltpu.VMEM((D_RG, D_RG), jnp.bfloat16),
            pltpu.VMEM((D_RG, D_RG), jnp.bfloat16),
        ],
        compiler_params=pltpu.CompilerParams(
            dimension_semantics=("arbitrary",), vmem_limit_bytes=VMEM_LIMIT),
        name="mixer_ln_router",
    )(x, w_in, cw, cb, wa, ba, wx, bx, lam, scw, w_out, g1, b1, wr, br, wg, wu, wd)


def _expert_kernel(blk_ref, grp_ref, ea_ref, eb_ref, rows_ref,
                   xs_hbm, wt_ref, wgu_ref, wd_ref, g2_ref, b2_ref, ys_hbm,
                   xbuf_ref, xsem, obuf_ref, osem):
    s = pl.program_id(0)
    slot = s % 2
    tile_rows = [rows_ref[s * SUB + j] for j in range(SUB)]
    block_rows = SUB * TM

    def fetches(step, slot_):
        return [pltpu.make_async_copy(xs_hbm.at[pl.ds(step * block_rows, block_rows), j, :],
                                      xbuf_ref.at[slot_, j], xsem.at[slot_])
                for j in range(LANE_TILES)]

    def write_backs(step, slot_):
        return [pltpu.make_async_copy(obuf_ref.at[slot_, j],
                                      ys_hbm.at[pl.ds(step * block_rows, block_rows), j, :],
                                      osem.at[slot_])
                for j in range(LANE_TILES)]

    def did_work(step):
        return blk_ref[jnp.clip(step, 0, N_STEPS3 - 1)] == step

    @pl.when(jnp.logical_and(s == 0, did_work(0)))
    def _():
        for cp in fetches(0, 0):
            cp.start()

    @pl.when(did_work(s + 1))
    def _():
        for cp in fetches(s + 1, 1 - slot):
            cp.start()

    @pl.when(did_work(s - 2))
    def _():
        for cp in write_backs(s - 2, slot):
            cp.wait()

    @pl.when(did_work(s))
    def _():
        for cp in fetches(s, slot):
            cp.wait()
        for j in range(SUB):
            tile = s * SUB + j
            first_row = j * TM
            in_use = lax.broadcasted_iota(jnp.int32, (TM, 1), 0) < tile_rows[j]
            x = jnp.concatenate([xbuf_ref[slot, t, first_row:first_row + TM, :]
                                 for t in range(LANE_TILES)], axis=1)
            x = jnp.where(in_use, x, 0.0)
            xb = x.astype(jnp.bfloat16)
            wt = jnp.where(in_use, wt_ref[first_row:first_row + TM, :], 0.0)
            y = jnp.zeros((TM, D_MODEL), jnp.float32)
            for e_ref, col in ((ea_ref, 0), (eb_ref, 1)):
                e = e_ref[tile]
                hgu = jnp.dot(xb, wgu_ref[e], preferred_element_type=jnp.float32)
                hid = jax.nn.silu(hgu[:, :D_EXPERT]) * hgu[:, D_EXPERT:] * wt[:, col:col + 1]
                y = y + jnp.dot(hid.astype(jnp.bfloat16), wd_ref[e],
                                preferred_element_type=jnp.float32)
            out = _layer_norm(DEEPNORM_ALPHA * x + y, g2_ref[...], b2_ref[...])
            for t in range(LANE_TILES):
                obuf_ref[slot, t, first_row:first_row + TM, :] = out[:, t * LANES:(t + 1) * LANES]
        for cp in write_backs(s, slot):
            cp.start()

    @pl.when(s == N_STEPS3 - 1)
    def _():
        @pl.when(did_work(s - 1))
        def _():
            for cp in write_backs(s - 1, 1 - slot):
                cp.wait()

        @pl.when(did_work(s))
        def _():
            for cp in write_backs(s, slot):
                cp.wait()


def _expert_call(step_block, step_group, tile_ea, tile_eb, tile_rows, xs, wts, wgu, wd, g2, b2):
    rows = SUB * TM
    grid_spec = pltpu.PrefetchScalarGridSpec(
        num_scalar_prefetch=5,
        grid=(N_STEPS3,),
        in_specs=[
            pl.BlockSpec(memory_space=pl.ANY),
            pl.BlockSpec((rows, LANES), lambda s, blk, grp, ea, eb, va: (blk[s], 0)),
            pl.BlockSpec((None, EXPERTS_PER_GROUP, D_MODEL, 2 * D_EXPERT),
                         lambda s, blk, grp, ea, eb, va: (grp[s], 0, 0, 0)),
            pl.BlockSpec((None, EXPERTS_PER_GROUP, D_EXPERT, D_MODEL),
                         lambda s, blk, grp, ea, eb, va: (grp[s], 0, 0, 0)),
            pl.BlockSpec((1, D_MODEL), lambda s, blk, grp, ea, eb, va: (0, 0)),
            pl.BlockSpec((1, D_MODEL), lambda s, blk, grp, ea, eb, va: (0, 0)),
        ],
        out_specs=pl.BlockSpec(memory_space=pl.ANY),
        scratch_shapes=[
            pltpu.VMEM((2, LANE_TILES, rows, LANES), jnp.float32),
            pltpu.SemaphoreType.DMA((2,)),
            pltpu.VMEM((2, LANE_TILES, rows, LANES), jnp.float32),
            pltpu.SemaphoreType.DMA((2,)),
        ],
    )
    return pl.pallas_call(
        _expert_kernel,
        grid_spec=grid_spec,
        out_shape=jax.ShapeDtypeStruct((P_ROWS, LANE_TILES, LANES), jnp.float32),
        compiler_params=pltpu.CompilerParams(
            dimension_semantics=("arbitrary",), vmem_limit_bytes=VMEM_LIMIT),
        name="experts_ln",
    )(step_block, step_group, tile_ea, tile_eb, tile_rows, xs, wts, wgu, wd, g2, b2)


SC_CORES = 2
SC_SUBCORES = 16
SC_WORKERS = SC_CORES * SC_SUBCORES
COMBINE_CHUNK = 32


def _sc_gather_rows(table, idx, chunk, out_row_shape=None):
    n_out, = idx.shape
    row_shape = table.shape[1:]
    out_row_shape = row_shape if out_row_shape is None else out_row_shape
    per_w = n_out // SC_WORKERS
    n_chunks = per_w // chunk
    assert per_w * SC_WORKERS == n_out and n_chunks * chunk == per_w
    assert n_chunks % 2 == 0 and chunk % 8 == 0
    mesh = plsc.VectorSubcoreMesh(core_axis_name="c", subcore_axis_name="s")

    @functools.partial(
        pl.kernel, mesh=mesh,
        out_type=jax.ShapeDtypeStruct((n_out,) + out_row_shape, table.dtype),
        scratch_types=[
            pltpu.VMEM((per_w,), jnp.int32),
            pltpu.VMEM((2, chunk) + row_shape, table.dtype),
            pltpu.SemaphoreType.DMA((2,)),
            pltpu.SemaphoreType.DMA((2,)),
        ],
    )
    def gather_kernel(table_hbm, idx_hbm, out_hbm, idx_v, buf, gsem, wsem):
        wid = lax.axis_index("s") * SC_CORES + lax.axis_index("c")
        base = wid * per_w
        pltpu.sync_copy(idx_hbm.at[pl.ds(base, per_w)], idx_v)

        def gather(j, slot):
            rows = idx_v.at[pl.ds(j * chunk, chunk)]
            return pltpu.make_async_copy(table_hbm.at[rows], buf.at[slot], gsem.at[slot])

        def write(j, slot):
            dst = out_hbm.at[pl.ds(base + j * chunk, chunk)]
            src = buf.at[slot].reshape((chunk,) + out_row_shape)
            return pltpu.make_async_copy(src, dst, wsem.at[slot])

        gather(0, 0).start()

        @pl.loop(0, n_chunks, step=2)
        def _(j0):
            for slot in range(2):
                j = j0 + slot
                gather(j, slot).wait()

                @pl.when(j >= 1)
                def _():
                    write(j - 1, 1 - slot).wait()

                @pl.when(j + 1 < n_chunks)
                def _():
                    gather(j + 1, 1 - slot).start()

                write(j, slot).start()

        write(n_chunks - 1, (n_chunks - 1) % 2).wait()

    return gather_kernel(table, idx)


SC_LANES = 16


def _sc_dispatch(rows, meta, cls, rank, class_start):
    n_rows = rows.shape[0]
    chunk = COMBINE_CHUNK
    per_w = n_rows // SC_WORKERS
    n_chunks = per_w // chunk
    assert per_w * SC_WORKERS == n_rows and n_chunks * chunk == per_w
    assert n_chunks % 2 == 0 and chunk % SC_LANES == 0
    mesh = plsc.VectorSubcoreMesh(core_axis_name="c", subcore_axis_name="s")

    @functools.partial(
        pl.kernel, mesh=mesh,
        out_type=[
            jax.ShapeDtypeStruct((P_ROWS,) + rows.shape[1:], rows.dtype),
            jax.ShapeDtypeStruct((P_ROWS,) + meta.shape[1:], meta.dtype),
            jax.ShapeDtypeStruct((n_rows,), jnp.int32),
        ],
        scratch_types=[
            pltpu.VMEM((per_w,), jnp.int32),
            pltpu.VMEM((per_w,), jnp.int32),
            pltpu.VMEM((LANES,), jnp.int32),
            pltpu.VMEM((per_w,), jnp.int32),
            pltpu.VMEM((n_chunks, chunk), jnp.int32),
            pltpu.VMEM((2, chunk) + rows.shape[1:], rows.dtype),
            pltpu.VMEM((2, chunk) + meta.shape[1:], meta.dtype),
            pltpu.SemaphoreType.DMA((2,)),
            pltpu.SemaphoreType.DMA((2,)),
            pltpu.SemaphoreType.DMA((2,)),
            pltpu.SemaphoreType.DMA((2,)),
        ],
        compiler_params=pltpu.CompilerParams(needs_layout_passes=False),
    )
    def dispatch_kernel(rows_hbm, meta_hbm, cls_hbm, rank_hbm, start_hbm,
                        xs_hbm, ws_hbm, pos_hbm,
                        cls_v, rank_v, start_v, pos_flat, pos_v, rbuf, mbuf,
                        rsem, msem, xsem, wsem):
        wid = lax.axis_index("s") * SC_CORES + lax.axis_index("c")
        base = wid * per_w
        pltpu.sync_copy(cls_hbm.at[pl.ds(base, per_w)], cls_v)
        pltpu.sync_copy(rank_hbm.at[pl.ds(base, per_w)], rank_v)
        pltpu.sync_copy(start_hbm, start_v)

        @pl.loop(0, n_chunks)
        def _(j):
            for k in range(chunk // SC_LANES):
                off = j * chunk + k * SC_LANES
                c = cls_v[pl.ds(off, SC_LANES)]
                pos = plsc.load_gather(start_v, [c]) + rank_v[pl.ds(off, SC_LANES)]
                pos_flat[pl.ds(off, SC_LANES)] = pos
                pos_v[j, pl.ds(k * SC_LANES, SC_LANES)] = pos

        pltpu.sync_copy(pos_flat, pos_hbm.at[pl.ds(base, per_w)])

        def read_rows(j, slot):
            src = rows_hbm.at[pl.ds(base + j * chunk, chunk)]
            return pltpu.make_async_copy(src, rbuf.at[slot], rsem.at[slot])

        def read_meta(j, slot):
            src = meta_hbm.at[pl.ds(base + j * chunk, chunk)]
            return pltpu.make_async_copy(src, mbuf.at[slot], msem.at[slot])

        def put_rows(j, slot):
            return pltpu.make_async_copy(rbuf.at[slot], xs_hbm.at[pos_v.at[j]], xsem.at[slot])

        def put_meta(j, slot):
            return pltpu.make_async_copy(mbuf.at[slot], ws_hbm.at[pos_v.at[j]], wsem.at[slot])

        read_rows(0, 0).start()
        read_meta(0, 0).start()

        @pl.loop(0, n_chunks, step=2)
        def _(j0):
            for slot in range(2):
                j = j0 + slot
                read_rows(j, slot).wait()
                read_meta(j, slot).wait()

                @pl.when(j >= 1)
                def _():
                    put_rows(j - 1, 1 - slot).wait()
                    put_meta(j - 1, 1 - slot).wait()

                @pl.when(j + 1 < n_chunks)
                def _():
                    read_rows(j + 1, 1 - slot).start()
                    read_meta(j + 1, 1 - slot).start()

                put_rows(j, slot).start()
                put_meta(j, slot).start()

        put_rows(n_chunks - 1, (n_chunks - 1) % 2).wait()
        put_meta(n_chunks - 1, (n_chunks - 1) % 2).wait()

    return dispatch_kernel(rows, meta, cls, rank, class_start)


def _pair_tables():
    pair_of = np.zeros((EXPERTS_PER_GROUP, EXPERTS_PER_GROUP), np.int32)
    lo = np.zeros((N_PAIRS,), np.int32)
    hi = np.zeros((N_PAIRS,), np.int32)
    p = 0
    for a in range(EXPERTS_PER_GROUP):
        for b in range(a + 1, EXPERTS_PER_GROUP):
            pair_of[a, b] = pair_of[b, a] = p
            lo[p], hi[p] = a, b
            p += 1
    return pair_of, lo, hi


def _route_kernel(lt_ref, info_ref, meta_ref, counts_ref, run_ref):
    step = pl.program_id(0)

    @pl.when(step == 0)
    def _():
        run_ref[...] = jnp.zeros_like(run_ref)

    f32 = jnp.float32
    sub8 = lax.broadcasted_iota(jnp.int32, (8, LANES), 0).astype(f32)
    row_id = lax.broadcasted_iota(jnp.int32, (LANES, LANES), 0)
    col_id = lax.broadcasted_iota(jnp.int32, (LANES, LANES), 1)
    prefix_mat = (row_id <= col_id).astype(jnp.bfloat16)
    ones_mat = jnp.ones((LANES, LANES), jnp.bfloat16)
    neg_inf = f32(-jnp.inf)

    def first_index_of_max(v):
        m = jnp.max(v, axis=0, keepdims=True)
        idx = jnp.min(jnp.where(v == m, sub8, f32(8)), axis=0, keepdims=True)
        return m, idx

    def lane_tile(k, run):
        lanes = pl.ds(pl.multiple_of(k * LANES, LANES), LANES)
        g = jnp.where(sub8 < N_GROUPS, lt_ref[0:GROUP_ROWS, lanes], neg_inf)
        g_max, g_idx = first_index_of_max(g)
        g_top_p = 1.0 / jnp.sum(jnp.exp(g - g_max), axis=0, keepdims=True)

        e_sel = lt_ref[GROUP_ROWS:GROUP_ROWS + EXPERTS_PER_GROUP, lanes]
        for grp in range(1, N_GROUPS):
            lo = GROUP_ROWS + grp * EXPERTS_PER_GROUP
            e_sel = jnp.where(g_idx == grp, lt_ref[lo:lo + EXPERTS_PER_GROUP, lanes], e_sel)
        m1, i1 = first_index_of_max(e_sel)
        rest = jnp.where(sub8 == i1, neg_inf, e_sel)
        m2 = jnp.max(rest, axis=0, keepdims=True)
        i2 = jnp.min(jnp.where((rest == m2) & (sub8 != i1), sub8, f32(8)), axis=0, keepdims=True)

        e = jnp.exp(m2 - m1)
        w1 = g_top_p / (1.0 + e)
        w2 = g_top_p * e / (1.0 + e)
        first_is_lo = i1 < i2
        w_lo = jnp.where(first_is_lo, w1, w2)
        w_hi = jnp.where(first_is_lo, w2, w1)
        lo_e = jnp.minimum(i1, i2)
        hi_e = jnp.maximum(i1, i2)
        pair = lo_e * (2 * EXPERTS_PER_GROUP - 1 - lo_e) * 0.5 + hi_e - lo_e - 1.0
        cls = (g_idx * N_PAIRS + pair).astype(jnp.int32)

        onehot = (row_id == cls).astype(jnp.bfloat16)
        csum = jnp.dot(onehot, prefix_mat, preferred_element_type=f32)
        rank = jnp.sum(onehot.astype(f32) * (csum + run), axis=0, keepdims=True) - 1.0
        run = run + jnp.dot(onehot, ones_mat, preferred_element_type=f32)

        info_ref[:, lanes] = jnp.where(sub8 == 0, cls, jnp.where(sub8 == 1, rank.astype(jnp.int32), 0))
        meta8 = jnp.where(sub8 == 0, w_lo, jnp.where(sub8 == 1, w_hi, 0.0))
        meta_t = jnp.concatenate([meta8, jnp.zeros((LANES - 8, LANES), f32)], axis=0)
        meta_ref[lanes, :] = meta_t.T
        return run

    run = lax.fori_loop(0, ROUTE_BLOCK // LANES, lane_tile, run_ref[...], unroll=ROUTE_UNROLL)
    run_ref[...] = run
    counts_ref[...] = run


def _route_call(lt):
    return pl.pallas_call(
        _route_kernel,
        grid=(N_TOKENS // ROUTE_BLOCK,),
        in_specs=[pl.BlockSpec((ROUTER_ROWS, ROUTE_BLOCK), lambda i: (0, i))],
        out_specs=[
            pl.BlockSpec((8, ROUTE_BLOCK), lambda i: (0, i)),
            pl.BlockSpec((ROUTE_BLOCK, LANES), lambda i: (i, 0)),
            pl.BlockSpec((LANES, LANES), lambda i: (0, 0)),
        ],
        out_shape=[
            jax.ShapeDtypeStruct((8, N_TOKENS), jnp.int32),
            jax.ShapeDtypeStruct((N_TOKENS, LANES), jnp.float32),
            jax.ShapeDtypeStruct((LANES, LANES), jnp.float32),
        ],
        scratch_shapes=[pltpu.VMEM((LANES, LANES), jnp.float32)],
        compiler_params=pltpu.CompilerParams(dimension_semantics=("arbitrary",)),
        name="route_rank",
    )(lt)


def _dispatch_plan(counts):
    _, pair_lo, pair_hi = _pair_tables()
    tiles_c = (counts + TM - 1) // TM
    tiles_g = tiles_c.reshape(N_GROUPS, N_PAIRS).sum(axis=1)
    tiles_g_pad = (tiles_g + SUB - 1) // SUB * SUB
    g_start = jnp.cumsum(tiles_g_pad) - tiles_g_pad
    tc = tiles_c.reshape(N_GROUPS, N_PAIRS)
    c_start = (g_start[:, None] + jnp.cumsum(tc, axis=1) - tc).reshape(N_CLASSES)
    class_start = jnp.zeros((LANES,), jnp.int32).at[:N_CLASSES].set(c_start * TM)

    tile_ids = jnp.arange(N_TILES, dtype=jnp.int32)
    c_end = c_start + tiles_c
    owner = jnp.sum((tile_ids[:, None] >= c_end[None, :]).astype(jnp.int32), axis=1)
    owner = jnp.minimum(owner, N_CLASSES - 1)
    in_class = (tile_ids >= c_start[owner]) & (tile_ids < c_end[owner])
    rows_left = counts[owner] - (tile_ids - c_start[owner]) * TM
    tile_rows = jnp.where(in_class, jnp.minimum(rows_left, TM), 0).astype(jnp.int32)
    pair = owner % N_PAIRS
    tile_ea = jnp.asarray(pair_lo)[pair]
    tile_eb = jnp.asarray(pair_hi)[pair]
    g_end = g_start + tiles_g_pad
    step_first = jnp.arange(N_STEPS3, dtype=jnp.int32) * SUB
    step_group = jnp.sum((step_first[:, None] >= g_end[None, :]).astype(jnp.int32), axis=1)
    step_group = jnp.minimum(step_group, N_GROUPS - 1)
    last_step = jnp.maximum(g_end[-1] // SUB - 1, 0)
    step_block = jnp.minimum(jnp.arange(N_STEPS3, dtype=jnp.int32), last_step)
    return (class_start, tile_ea, tile_eb, tile_rows, step_group.astype(jnp.int32),
            step_block.astype(jnp.int32))


def kernel(x, w_in, rg_conv_w, rg_conv_b, rg_gate_a_w, rg_gate_a_b, rg_gate_x_w, rg_gate_x_b,
           rg_lambda, sc_conv_w, w_out, ln1_g, ln1_b, router_group_w, router_group_b,
           router_expert_w, router_expert_b, exp_w_gate, exp_w_up, exp_w_down, ln2_g, ln2_b):
    bf16 = jnp.bfloat16
    row = lambda v: v.reshape(1, -1)
    pad_g = GROUP_ROWS - N_GROUPS
    pad_e = ROUTER_LANES - ROUTER_ROWS
    wr = jnp.concatenate([router_group_w, jnp.zeros((D_MODEL, pad_g), jnp.float32),
                          router_expert_w, jnp.zeros((D_MODEL, pad_e), jnp.float32)], axis=1)
    br = jnp.concatenate([router_group_b, jnp.zeros((pad_g,), jnp.float32),
                          router_expert_b, jnp.zeros((pad_e,), jnp.float32)]).reshape(1, -1)
    wr_hi = wr.astype(bf16)
    wr_lo = (wr - wr_hi.astype(jnp.float32)).astype(bf16)

    h1, lt, wgu, wdb = _mixer_call(
        x, w_in.astype(bf16), rg_conv_w, row(rg_conv_b),
        rg_gate_a_w, row(rg_gate_a_b),
        rg_gate_x_w, row(rg_gate_x_b),
        row(rg_lambda), sc_conv_w, w_out.astype(bf16), row(ln1_g), row(ln1_b),
        jnp.concatenate([wr_hi, wr_lo], axis=1), br, exp_w_gate, exp_w_up, exp_w_down)

    info, meta, counts = _route_call(lt)
    class_start, tile_ea, tile_eb, tile_rows, step_group, step_block = _dispatch_plan(
        counts[:N_CLASSES, 0].astype(jnp.int32))

    xs, wts, pos = _sc_dispatch(h1, meta, info[0], info[1], class_start)

    grp_shape = (N_GROUPS, EXPERTS_PER_GROUP)
    ys = _expert_call(
        step_block, step_group, tile_ea, tile_eb, tile_rows, xs, wts,
        wgu.reshape(grp_shape + (D_MODEL, 2 * D_EXPERT)),
        wdb.reshape(grp_shape + (D_EXPERT, D_MODEL)),
        row(ln2_g), row(ln2_b))

    out_rows = pos.reshape(SEQ, BATCH).T.reshape(-1)
    out = _sc_gather_rows(ys, out_rows, COMBINE_CHUNK, (D_MODEL,))
    return out.reshape(BATCH, SEQ, D_MODEL)
```

```python
import functools

import jax
import jax.numpy as jnp
import numpy as np
from jax import lax
from jax.experimental import pallas as pl
from jax.experimental.pallas import tpu as pltpu
from jax.experimental.pallas import tpu_sc as plsc

D_MODEL = 1024
BATCH = 16
SEQ = 2048
D_RG = 512
D_SC = 512
RG_HEADS = 8
RG_HEAD_DIM = D_RG // RG_HEADS
RG_CONV = 4
RG_C = 8.0
SC_CONV = 3
D_IN_PROJ = 2 * D_RG + 3 * D_SC
N_GROUPS = 4
EXPERTS_PER_GROUP = 8
N_EXPERTS = N_GROUPS * EXPERTS_PER_GROUP
D_EXPERT = D_MODEL // 4
LN_EPS = 1e-5
DEEPNORM_ALPHA = 2.0 ** 0.25

N_TOKENS = BATCH * SEQ
LANES = 128
LANE_TILES = D_MODEL // LANES
MXU_WIDTH = 256
ROUTER_LANES = LANES
GROUP_ROWS = 8
ROUTER_ROWS = GROUP_ROWS + N_EXPERTS

TS = 64
M1 = TS * BATCH
DOT_ROWS = 256

ROUTE_BLOCK = 4096
ROUTE_UNROLL = 8

N_PAIRS = EXPERTS_PER_GROUP * (EXPERTS_PER_GROUP - 1) // 2
N_CLASSES = N_GROUPS * N_PAIRS
BF16_ROWS = 16
TM = -(-int(N_TOKENS / N_CLASSES * 1.09 / 2) // BF16_ROWS) * BF16_ROWS
SUB = 8
N_TILES = -(-(N_TOKENS // TM + N_CLASSES + N_GROUPS * (SUB - 1)) // SUB) * SUB
N_STEPS3 = N_TILES // SUB
P_ROWS = N_TILES * TM

VMEM_LIMIT = 56 * 1024 * 1024


def _layer_norm(z, g, b):
    mu = jnp.mean(z, axis=-1, keepdims=True)
    zc = z - mu
    var = jnp.mean(zc * zc, axis=-1, keepdims=True)
    return zc * lax.rsqrt(var + LN_EPS) * g + b


def _mixer_kernel(x_ref, w_in_f32_ref, cw_ref, cb_ref, wa_heads_ref, ba_ref, wx_heads_ref, bx_ref,
                  lam_ref, scw_ref, w_out_ref, g1_ref, b1_ref, wr_ref, br_ref,
                  wg_ref, wu_ref, wd_ref,
                  h1_hbm, logit_ref, wgu_ref, wdb_ref,
                  xbuf_ref, xsem, hbuf_ref, hsem, cbuf_ref, sbuf_ref, a_ref, u_ref, hstate_ref,
                  wa_ref, wx_ref, w_in_ref):
    c = pl.program_id(0)
    n_steps = SEQ // TS

    def h1_copies(step, slot_):
        return [pltpu.make_async_copy(hbuf_ref.at[slot_, j],
                                      h1_hbm.at[pl.ds(step * M1, M1), j, :], hsem.at[slot_])
                for j in range(LANE_TILES)]

    @pl.when(c >= 2)
    def _():
        for cp in h1_copies(c - 2, c % 2):
            cp.wait()

    wgu_ref[:, :, :D_EXPERT] = wg_ref[...].astype(jnp.bfloat16)
    wgu_ref[:, :, D_EXPERT:] = wu_ref[...].astype(jnp.bfloat16)
    wdb_ref[...] = wd_ref[...].astype(jnp.bfloat16)
    slot = c % 2
    rg_pad = (RG_CONV - 1) * BATCH
    sc_pad = (SC_CONV - 1) * BATCH

    def x_copies(chunk, slot_):
        return [pltpu.make_async_copy(x_ref.at[b, pl.ds(chunk * TS, TS), :],
                                      xbuf_ref.at[slot_, :, b, :], xsem.at[slot_])
                for b in range(BATCH)]

    @pl.when(c == 0)
    def _():
        for cp in x_copies(0, 0):
            cp.start()
        cbuf_ref[0:rg_pad, :] = jnp.zeros((rg_pad, D_RG), jnp.float32)
        sbuf_ref[0:sc_pad, :] = jnp.zeros((sc_pad, D_SC), jnp.float32)
        hstate_ref[...] = jnp.zeros_like(hstate_ref)
        w_in_ref[...] = w_in_f32_ref[...].astype(jnp.bfloat16)
        for heads_ref, bd_ref in ((wa_heads_ref, wa_ref), (wx_heads_ref, wx_ref)):
            bd_ref[...] = jnp.zeros_like(bd_ref)
            for hd in range(RG_HEADS):
                lo = hd * RG_HEAD_DIM
                bd_ref[lo:lo + RG_HEAD_DIM, lo:lo + RG_HEAD_DIM] = (
                    heads_ref[hd].astype(jnp.bfloat16))

    @pl.when(c + 1 < SEQ // TS)
    def _():
        for cp in x_copies(c + 1, 1 - slot):
            cp.start()

    for cp in x_copies(c, slot):
        cp.wait()

    xt_ref = xbuf_ref.at[slot]

    def gate(xcb, w_ref, b_ref):
        parts = [jnp.dot(xcb[:, lo:lo + MXU_WIDTH], w_ref[lo:lo + MXU_WIDTH, lo:lo + MXU_WIDTH],
                         preferred_element_type=jnp.float32)
                 for lo in range(0, D_RG, MXU_WIDTH)]
        return 0.5 + 0.5 * jnp.tanh(0.5 * (jnp.concatenate(parts, axis=1) + b_ref[...]))

    soft_lam = jax.nn.softplus(-lam_ref[...])
    gelu_chunks, y_sc_chunks = [], []
    for r0 in range(0, M1, DOT_ROWS):
        t0 = r0 // BATCH
        xk = xt_ref[t0:t0 + DOT_ROWS // BATCH].reshape(DOT_ROWS, D_MODEL).astype(jnp.bfloat16)

        def proj(lo, width):
            return jnp.dot(xk, w_in_ref[:, lo:lo + width], preferred_element_type=jnp.float32)

        cbuf_ref[rg_pad + r0:rg_pad + r0 + DOT_ROWS, :] = proj(0, D_RG)
        rg_gate = proj(D_RG, D_RG)
        xc = cb_ref[...] + cw_ref[0:1, :] * cbuf_ref[r0:r0 + DOT_ROWS, :]
        for k in range(1, RG_CONV):
            xc = xc + cw_ref[k:k + 1, :] * cbuf_ref[r0 + k * BATCH:r0 + k * BATCH + DOT_ROWS, :]
        sc_b = proj(2 * D_RG, D_SC)
        xcb = xc.astype(jnp.bfloat16)
        r = gate(xcb, wa_ref, ba_ref)
        i = gate(xcb, wx_ref, bx_ref)
        sc_c = proj(2 * D_RG + D_SC, D_SC)
        log_a = (-RG_C) * r * soft_lam
        a_ref[r0:r0 + DOT_ROWS, :] = jnp.exp(log_a)
        th = jnp.tanh(log_a)
        u_ref[r0:r0 + DOT_ROWS, :] = jnp.sqrt(-2.0 * th / (1.0 - th)) * (i * xc)
        gelu_chunks.append(jax.nn.gelu(rg_gate))

        sbuf_ref[sc_pad + r0:sc_pad + r0 + DOT_ROWS, :] = sc_c * proj(2 * D_RG + 2 * D_SC, D_SC)
        conv = scw_ref[0:1, :] * sbuf_ref[r0:r0 + DOT_ROWS, :]
        for k in range(1, SC_CONV):
            conv = conv + scw_ref[k:k + 1, :] * sbuf_ref[r0 + k * BATCH:r0 + k * BATCH + DOT_ROWS, :]
        y_sc_chunks.append((sc_b * conv).astype(jnp.bfloat16))

    cbuf_ref[0:rg_pad, :] = cbuf_ref[M1:M1 + rg_pad, :]
    sbuf_ref[0:sc_pad, :] = sbuf_ref[M1:M1 + sc_pad, :]

    h = hstate_ref[...]
    for t in range(TS):
        h = a_ref[t * BATCH:(t + 1) * BATCH, :] * h + u_ref[t * BATCH:(t + 1) * BATCH, :]
        u_ref[t * BATCH:(t + 1) * BATCH, :] = h
    hstate_ref[...] = h

    tail_rows = M1 // 2
    per_tail = tail_rows // DOT_ROWS

    def out_proj(k):
        r0 = k * tail_rows
        gelu_k = jnp.concatenate(gelu_chunks[k * per_tail:(k + 1) * per_tail], axis=0)
        y_sc_k = jnp.concatenate(y_sc_chunks[k * per_tail:(k + 1) * per_tail], axis=0)
        y_rg = (u_ref[r0:r0 + tail_rows, :] * gelu_k).astype(jnp.bfloat16)
        mix = jnp.dot(y_rg, w_out_ref[0:D_RG, :], preferred_element_type=jnp.float32)
        return mix + jnp.dot(y_sc_k, w_out_ref[D_RG:, :], preferred_element_type=jnp.float32)

    def finish(k, mix):
        r0 = k * tail_rows
        t0 = r0 // BATCH
        x_rows = xt_ref[t0:t0 + tail_rows // BATCH].reshape(tail_rows, D_MODEL)
        h1 = _layer_norm(DEEPNORM_ALPHA * x_rows + mix, g1_ref[...], b1_ref[...])
        for t in range(LANE_TILES):
            hbuf_ref[slot, t, r0:r0 + tail_rows, :] = h1[:, t * LANES:(t + 1) * LANES]
        h_hi = h1.astype(jnp.bfloat16)
        h_lo = (h1 - h_hi.astype(jnp.float32)).astype(jnp.bfloat16)
        both = jnp.dot(h_hi, wr_ref[...], preferred_element_type=jnp.float32)
        logits = both[:, :ROUTER_LANES] + both[:, ROUTER_LANES:] + br_ref[...]
        logits = logits + jnp.dot(h_lo, wr_ref[:, :ROUTER_LANES],
                                  preferred_element_type=jnp.float32)
        logit_ref[:, r0:r0 + tail_rows] = logits.T[:ROUTER_ROWS, :]

    n_tails = M1 // tail_rows
    mix = out_proj(0)
    for k in range(n_tails):
        next_mix = out_proj(k + 1) if k + 1 < n_tails else None
        finish(k, mix)
        mix = next_mix

    for cp in h1_copies(c, slot):
        cp.start()

    @pl.when(c == n_steps - 1)
    def _():
        if n_steps >= 2:
            for cp in h1_copies(c - 1, 1 - slot):
                cp.wait()
        for cp in h1_copies(c, slot):
            cp.wait()


def _const_spec(shape, pipeline_mode=None):
    if pipeline_mode is None:
        return pl.BlockSpec(shape, lambda c: (0,) * len(shape))
    return pl.BlockSpec(shape, lambda c: (0,) * len(shape), pipeline_mode=pipeline_mode)


def _mixer_call(x, w_in, cw, cb, wa, ba, wx, bx, lam, scw, w_out, g1, b1, wr, br, wg, wu, wd):
    n_chunks = SEQ // TS
    epc = N_EXPERTS // n_chunks
    assert epc * n_chunks == N_EXPERTS
    in_specs = [
        pl.BlockSpec(memory_space=pl.ANY),
        _const_spec((D_MODEL, D_IN_PROJ), pl.Buffered(1)),
        _const_spec((RG_CONV, D_RG)), _const_spec((1, D_RG)),
        _const_spec((RG_HEADS, RG_HEAD_DIM, RG_HEAD_DIM)), _const_spec((1, D_RG)),
        _const_spec((RG_HEADS, RG_HEAD_DIM, RG_HEAD_DIM)), _const_spec((1, D_RG)),
        _const_spec((1, D_RG)),
        _const_spec((SC_CONV, D_SC)),
        _const_spec((D_MODEL, D_MODEL), pl.Buffered(1)),
        _const_spec((1, D_MODEL)), _const_spec((1, D_MODEL)),
        _const_spec((D_MODEL, 2 * ROUTER_LANES)), _const_spec((1, ROUTER_LANES)),
        pl.BlockSpec((epc, D_MODEL, D_EXPERT), lambda c: (c, 0, 0)),
        pl.BlockSpec((epc, D_MODEL, D_EXPERT), lambda c: (c, 0, 0)),
        pl.BlockSpec((epc, D_EXPERT, D_MODEL), lambda c: (c, 0, 0)),
    ]
    out_specs = [
        pl.BlockSpec(memory_space=pl.ANY),
        pl.BlockSpec((ROUTER_ROWS, M1), lambda c: (0, c)),
        pl.BlockSpec((epc, D_MODEL, 2 * D_EXPERT), lambda c: (c, 0, 0)),
        pl.BlockSpec((epc, D_EXPERT, D_MODEL), lambda c: (c, 0, 0)),
    ]
    return pl.pallas_call(
        _mixer_kernel,
        grid=(n_chunks,),
        in_specs=in_specs,
        out_specs=out_specs,
        out_shape=[
            jax.ShapeDtypeStruct((N_TOKENS, LANE_TILES, LANES), jnp.float32),
            jax.ShapeDtypeStruct((ROUTER_ROWS, N_TOKENS), jnp.float32),
            jax.ShapeDtypeStruct((N_EXPERTS, D_MODEL, 2 * D_EXPERT), jnp.bfloat16),
            jax.ShapeDtypeStruct((N_EXPERTS, D_EXPERT, D_MODEL), jnp.bfloat16),
        ],
        scratch_shapes=[
            pltpu.VMEM((2, TS, BATCH, D_MODEL), jnp.float32),
            pltpu.SemaphoreType.DMA((2,)),
            pltpu.VMEM((2, LANE_TILES, M1, LANES), jnp.float32),
            pltpu.SemaphoreType.DMA((2,)),
            pltpu.VMEM((M1 + (RG_CONV - 1) * BATCH, D_RG), jnp.float32),
            pltpu.VMEM((M1 + (SC_CONV - 1) * BATCH, D_SC), jnp.float32),
            pltpu.VMEM((M1, D_RG), jnp.float32),
            pltpu.VMEM((M1, D_RG), jnp.float32),
            pltpu.VMEM((BATCH, D_RG), jnp.float32),
            pltpu.VMEM((D_RG, D_RG), jnp.bfloat16),
            pltpu.VMEM((D_RG, D_RG), jnp.bfloat16),
            pltpu.VMEM((D_MODEL, D_IN_PROJ), jnp.bfloat16),
        ],
        compiler_params=pltpu.CompilerParams(
            dimension_semantics=("arbitrary",), vmem_limit_bytes=VMEM_LIMIT),
        name="mixer_ln_router",
    )(x, w_in, cw, cb, wa, ba, wx, bx, lam, scw, w_out, g1, b1, wr, br, wg, wu, wd)


def _expert_kernel(blk_ref, grp_ref, ea_ref, eb_ref, rows_ref,
                   xs_hbm, wt_ref, wgu_ref, wd_ref, g2_ref, b2_ref, ys_hbm,
                   xbuf_ref, xsem, obuf_ref, osem):
    s = pl.program_id(0)
    slot = s % 2
    tile_rows = [rows_ref[s * SUB + j] for j in range(SUB)]
    block_rows = SUB * TM

    def fetches(step, slot_):
        return [pltpu.make_async_copy(xs_hbm.at[pl.ds(step * block_rows, block_rows), j, :],
                                      xbuf_ref.at[slot_, j], xsem.at[slot_])
                for j in range(LANE_TILES)]

    def write_backs(step, slot_):
        return [pltpu.make_async_copy(obuf_ref.at[slot_, j],
                                      ys_hbm.at[pl.ds(step * block_rows, block_rows), j, :],
                                      osem.at[slot_])
                for j in range(LANE_TILES)]

    def did_work(step):
        return blk_ref[jnp.clip(step, 0, N_STEPS3 - 1)] == step

    @pl.when(jnp.logical_and(s == 0, did_work(0)))
    def _():
        for cp in fetches(0, 0):
            cp.start()

    @pl.when(did_work(s + 1))
    def _():
        for cp in fetches(s + 1, 1 - slot):
            cp.start()

    @pl.when(did_work(s - 2))
    def _():
        for cp in write_backs(s - 2, slot):
            cp.wait()

    @pl.when(did_work(s))
    def _():
        for cp in fetches(s, slot):
            cp.wait()
        for j in range(SUB):
            tile = s * SUB + j
            first_row = j * TM
            in_use = lax.broadcasted_iota(jnp.int32, (TM, 1), 0) < tile_rows[j]
            x = jnp.concatenate([xbuf_ref[slot, t, first_row:first_row + TM, :]
                                 for t in range(LANE_TILES)], axis=1)
            x = jnp.where(in_use, x, 0.0)
            xb = x.astype(jnp.bfloat16)
            wt = jnp.where(in_use, wt_ref[first_row:first_row + TM, :], 0.0)
            y = jnp.zeros((TM, D_MODEL), jnp.float32)
            for e_ref, col in ((ea_ref, 0), (eb_ref, 1)):
                e = e_ref[tile]
                hgu = jnp.dot(xb, wgu_ref[e], preferred_element_type=jnp.float32)
                hid = jax.nn.silu(hgu[:, :D_EXPERT]) * hgu[:, D_EXPERT:] * wt[:, col:col + 1]
                y = y + jnp.dot(hid.astype(jnp.bfloat16), wd_ref[e],
                                preferred_element_type=jnp.float32)
            out = _layer_norm(DEEPNORM_ALPHA * x + y, g2_ref[...], b2_ref[...])
            for t in range(LANE_TILES):
                obuf_ref[slot, t, first_row:first_row + TM, :] = out[:, t * LANES:(t + 1) * LANES]
        for cp in write_backs(s, slot):
            cp.start()

    @pl.when(s == N_STEPS3 - 1)
    def _():
        @pl.when(did_work(s - 1))
        def _():
            for cp in write_backs(s - 1, 1 - slot):
                cp.wait()

        @pl.when(did_work(s))
        def _():
            for cp in write_backs(s, slot):
                cp.wait()


def _expert_call(step_block, step_group, tile_ea, tile_eb, tile_rows, xs, wts, wgu, wd, g2, b2):
    rows = SUB * TM
    grid_spec = pltpu.PrefetchScalarGridSpec(
        num_scalar_prefetch=5,
        grid=(N_STEPS3,),
        in_specs=[
            pl.BlockSpec(memory_space=pl.ANY),
            pl.BlockSpec((rows, LANES), lambda s, blk, grp, ea, eb, va: (blk[s], 0)),
            pl.BlockSpec((None, EXPERTS_PER_GROUP, D_MODEL, 2 * D_EXPERT),
                         lambda s, blk, grp, ea, eb, va: (grp[s], 0, 0, 0)),
            pl.BlockSpec((None, EXPERTS_PER_GROUP, D_EXPERT, D_MODEL),
                         lambda s, blk, grp, ea, eb, va: (grp[s], 0, 0, 0)),
            pl.BlockSpec((1, D_MODEL), lambda s, blk, grp, ea, eb, va: (0, 0)),
            pl.BlockSpec((1, D_MODEL), lambda s, blk, grp, ea, eb, va: (0, 0)),
        ],
        out_specs=pl.BlockSpec(memory_space=pl.ANY),
        scratch_shapes=[
            pltpu.VMEM((2, LANE_TILES, rows, LANES), jnp.float32),
            pltpu.SemaphoreType.DMA((2,)),
            pltpu.VMEM((2, LANE_TILES, rows, LANES), jnp.float32),
            pltpu.SemaphoreType.DMA((2,)),
        ],
    )
    return pl.pallas_call(
        _expert_kernel,
        grid_spec=grid_spec,
        out_shape=jax.ShapeDtypeStruct((P_ROWS, LANE_TILES, LANES), jnp.float32),
        compiler_params=pltpu.CompilerParams(
            dimension_semantics=("arbitrary",), vmem_limit_bytes=VMEM_LIMIT),
        name="experts_ln",
    )(step_block, step_group, tile_ea, tile_eb, tile_rows, xs, wts, wgu, wd, g2, b2)


SC_CORES = 2
SC_SUBCORES = 16
SC_WORKERS = SC_CORES * SC_SUBCORES
COMBINE_CHUNK = 32


def _sc_gather_rows(table, idx, chunk, out_row_shape=None):
    n_out, = idx.shape
    row_shape = table.shape[1:]
    out_row_shape = row_shape if out_row_shape is None else out_row_shape
    per_w = n_out // SC_WORKERS
    n_chunks = per_w // chunk
    assert per_w * SC_WORKERS == n_out and n_chunks * chunk == per_w
    assert n_chunks % 2 == 0 and chunk % 8 == 0
    mesh = plsc.VectorSubcoreMesh(core_axis_name="c", subcore_axis_name="s")

    @functools.partial(
        pl.kernel, mesh=mesh,
        out_type=jax.ShapeDtypeStruct((n_out,) + out_row_shape, table.dtype),
        scratch_types=[
            pltpu.VMEM((per_w,), jnp.int32),
            pltpu.VMEM((2, chunk) + row_shape, table.dtype),
            pltpu.SemaphoreType.DMA((2,)),
            pltpu.SemaphoreType.DMA((2,)),
        ],
    )
    def gather_kernel(table_hbm, idx_hbm, out_hbm, idx_v, buf, gsem, wsem):
        wid = lax.axis_index("s") * SC_CORES + lax.axis_index("c")
        base = wid * per_w
        pltpu.sync_copy(idx_hbm.at[pl.ds(base, per_w)], idx_v)

        def gather(j, slot):
            rows = idx_v.at[pl.ds(j * chunk, chunk)]
            return pltpu.make_async_copy(table_hbm.at[rows], buf.at[slot], gsem.at[slot])

        def write(j, slot):
            dst = out_hbm.at[pl.ds(base + j * chunk, chunk)]
            src = buf.at[slot].reshape((chunk,) + out_row_shape)
            return pltpu.make_async_copy(src, dst, wsem.at[slot])

        gather(0, 0).start()

        @pl.loop(0, n_chunks, step=2)
        def _(j0):
            for slot in range(2):
                j = j0 + slot
                gather(j, slot).wait()

                @pl.when(j >= 1)
                def _():
                    write(j - 1, 1 - slot).wait()

                @pl.when(j + 1 < n_chunks)
                def _():
                    gather(j + 1, 1 - slot).start()

                write(j, slot).start()

        write(n_chunks - 1, (n_chunks - 1) % 2).wait()

    return gather_kernel(table, idx)


SC_LANES = 16


def _sc_dispatch(rows, meta, cls, rank, class_start):
    n_rows = rows.shape[0]
    chunk = COMBINE_CHUNK
    per_w = n_rows // SC_WORKERS
    n_chunks = per_w // chunk
    assert per_w * SC_WORKERS == n_rows and n_chunks * chunk == per_w
    assert n_chunks % 2 == 0 and chunk % SC_LANES == 0
    mesh = plsc.VectorSubcoreMesh(core_axis_name="c", subcore_axis_name="s")

    @functools.partial(
        pl.kernel, mesh=mesh,
        out_type=[
            jax.ShapeDtypeStruct((P_ROWS,) + rows.shape[1:], rows.dtype),
            jax.ShapeDtypeStruct((P_ROWS,) + meta.shape[1:], meta.dtype),
            jax.ShapeDtypeStruct((n_rows,), jnp.int32),
        ],
        scratch_types=[
            pltpu.VMEM((per_w,), jnp.int32),
            pltpu.VMEM((per_w,), jnp.int32),
            pltpu.VMEM((LANES,), jnp.int32),
            pltpu.VMEM((per_w,), jnp.int32),
            pltpu.VMEM((n_chunks, chunk), jnp.int32),
            pltpu.VMEM((2, chunk) + rows.shape[1:], rows.dtype),
            pltpu.VMEM((2, chunk) + meta.shape[1:], meta.dtype),
            pltpu.SemaphoreType.DMA((2,)),
            pltpu.SemaphoreType.DMA((2,)),
            pltpu.SemaphoreType.DMA((2,)),
            pltpu.SemaphoreType.DMA((2,)),
        ],
        compiler_params=pltpu.CompilerParams(needs_layout_passes=False),
    )
    def dispatch_kernel(rows_hbm, meta_hbm, cls_hbm, rank_hbm, start_hbm,
                        xs_hbm, ws_hbm, pos_hbm,
                        cls_v, rank_v, start_v, pos_flat, pos_v, rbuf, mbuf,
                        rsem, msem, xsem, wsem):
        wid = lax.axis_index("s") * SC_CORES + lax.axis_index("c")
        base = wid * per_w
        pltpu.sync_copy(cls_hbm.at[pl.ds(base, per_w)], cls_v)
        pltpu.sync_copy(rank_hbm.at[pl.ds(base, per_w)], rank_v)
        pltpu.sync_copy(start_hbm, start_v)

        @pl.loop(0, n_chunks)
        def _(j):
            for k in range(chunk // SC_LANES):
                off = j * chunk + k * SC_LANES
                c = cls_v[pl.ds(off, SC_LANES)]
                pos = plsc.load_gather(start_v, [c]) + rank_v[pl.ds(off, SC_LANES)]
                pos_flat[pl.ds(off, SC_LANES)] = pos
                pos_v[j, pl.ds(k * SC_LANES, SC_LANES)] = pos

        pltpu.sync_copy(pos_flat, pos_hbm.at[pl.ds(base, per_w)])

        def read_rows(j, slot):
            src = rows_hbm.at[pl.ds(base + j * chunk, chunk)]
            return pltpu.make_async_copy(src, rbuf.at[slot], rsem.at[slot])

        def read_meta(j, slot):
            src = meta_hbm.at[pl.ds(base + j * chunk, chunk)]
            return pltpu.make_async_copy(src, mbuf.at[slot], msem.at[slot])

        def put_rows(j, slot):
            return pltpu.make_async_copy(rbuf.at[slot], xs_hbm.at[pos_v.at[j]], xsem.at[slot])

        def put_meta(j, slot):
            return pltpu.make_async_copy(mbuf.at[slot], ws_hbm.at[pos_v.at[j]], wsem.at[slot])

        read_rows(0, 0).start()
        read_meta(0, 0).start()

        @pl.loop(0, n_chunks, step=2)
        def _(j0):
            for slot in range(2):
                j = j0 + slot
                read_rows(j, slot).wait()
                read_meta(j, slot).wait()

                @pl.when(j >= 1)
                def _():
                    put_rows(j - 1, 1 - slot).wait()
                    put_meta(j - 1, 1 - slot).wait()

                @pl.when(j + 1 < n_chunks)
                def _():
                    read_rows(j + 1, 1 - slot).start()
                    read_meta(j + 1, 1 - slot).start()

                put_rows(j, slot).start()
                put_meta(j, slot).start()

        put_rows(n_chunks - 1, (n_chunks - 1) % 2).wait()
        put_meta(n_chunks - 1, (n_chunks - 1) % 2).wait()

    return dispatch_kernel(rows, meta, cls, rank, class_start)


def _pair_tables():
    pair_of = np.zeros((EXPERTS_PER_GROUP, EXPERTS_PER_GROUP), np.int32)
    lo = np.zeros((N_PAIRS,), np.int32)
    hi = np.zeros((N_PAIRS,), np.int32)
    p = 0
    for a in range(EXPERTS_PER_GROUP):
        for b in range(a + 1, EXPERTS_PER_GROUP):
            pair_of[a, b] = pair_of[b, a] = p
            lo[p], hi[p] = a, b
            p += 1
    return pair_of, lo, hi


def _route_kernel(lt_ref, info_ref, meta_ref, counts_ref, run_ref):
    step = pl.program_id(0)

    @pl.when(step == 0)
    def _():
        run_ref[...] = jnp.zeros_like(run_ref)

    f32 = jnp.float32
    sub8 = lax.broadcasted_iota(jnp.int32, (8, LANES), 0).astype(f32)
    row_id = lax.broadcasted_iota(jnp.int32, (LANES, LANES), 0)
    col_id = lax.broadcasted_iota(jnp.int32, (LANES, LANES), 1)
    prefix_mat = (row_id <= col_id).astype(jnp.bfloat16)
    ones_mat = jnp.ones((LANES, LANES), jnp.bfloat16)
    neg_inf = f32(-jnp.inf)

    def first_index_of_max(v):
        m = jnp.max(v, axis=0, keepdims=True)
        idx = jnp.min(jnp.where(v == m, sub8, f32(8)), axis=0, keepdims=True)
        return m, idx

    def lane_tile(k, run):
        lanes = pl.ds(pl.multiple_of(k * LANES, LANES), LANES)
        g = jnp.where(sub8 < N_GROUPS, lt_ref[0:GROUP_ROWS, lanes], neg_inf)
        g_max, g_idx = first_index_of_max(g)
        g_top_p = 1.0 / jnp.sum(jnp.exp(g - g_max), axis=0, keepdims=True)

        e_sel = lt_ref[GROUP_ROWS:GROUP_ROWS + EXPERTS_PER_GROUP, lanes]
        for grp in range(1, N_GROUPS):
            lo = GROUP_ROWS + grp * EXPERTS_PER_GROUP
            e_sel = jnp.where(g_idx == grp, lt_ref[lo:lo + EXPERTS_PER_GROUP, lanes], e_sel)
        m1, i1 = first_index_of_max(e_sel)
        rest = jnp.where(sub8 == i1, neg_inf, e_sel)
        m2 = jnp.max(rest, axis=0, keepdims=True)
        i2 = jnp.min(jnp.where((rest == m2) & (sub8 != i1), sub8, f32(8)), axis=0, keepdims=True)

        e = jnp.exp(m2 - m1)
        w1 = g_top_p / (1.0 + e)
        w2 = g_top_p * e / (1.0 + e)
        first_is_lo = i1 < i2
        w_lo = jnp.where(first_is_lo, w1, w2)
        w_hi = jnp.where(first_is_lo, w2, w1)
        lo_e = jnp.minimum(i1, i2)
        hi_e = jnp.maximum(i1, i2)
        pair = lo_e * (2 * EXPERTS_PER_GROUP - 1 - lo_e) * 0.5 + hi_e - lo_e - 1.0
        cls = (g_idx * N_PAIRS + pair).astype(jnp.int32)

        onehot = (row_id == cls).astype(jnp.bfloat16)
        csum = jnp.dot(onehot, prefix_mat, preferred_element_type=f32)
        rank = jnp.sum(onehot.astype(f32) * (csum + run), axis=0, keepdims=True) - 1.0
        run = run + jnp.dot(onehot, ones_mat, preferred_element_type=f32)

        info_ref[:, lanes] = jnp.where(sub8 == 0, cls, jnp.where(sub8 == 1, rank.astype(jnp.int32), 0))
        meta8 = jnp.where(sub8 == 0, w_lo, jnp.where(sub8 == 1, w_hi, 0.0))
        meta_t = jnp.concatenate([meta8, jnp.zeros((LANES - 8, LANES), f32)], axis=0)
        meta_ref[lanes, :] = meta_t.T
        return run

    run = lax.fori_loop(0, ROUTE_BLOCK // LANES, lane_tile, run_ref[...], unroll=ROUTE_UNROLL)
    run_ref[...] = run
    counts_ref[...] = run


def _route_call(lt):
    return pl.pallas_call(
        _route_kernel,
        grid=(N_TOKENS // ROUTE_BLOCK,),
        in_specs=[pl.BlockSpec((ROUTER_ROWS, ROUTE_BLOCK), lambda i: (0, i))],
        out_specs=[
            pl.BlockSpec((8, ROUTE_BLOCK), lambda i: (0, i)),
            pl.BlockSpec((ROUTE_BLOCK, LANES), lambda i: (i, 0)),
            pl.BlockSpec((LANES, LANES), lambda i: (0, 0)),
        ],
        out_shape=[
            jax.ShapeDtypeStruct((8, N_TOKENS), jnp.int32),
            jax.ShapeDtypeStruct((N_TOKENS, LANES), jnp.float32),
            jax.ShapeDtypeStruct((LANES, LANES), jnp.float32),
        ],
        scratch_shapes=[pltpu.VMEM((LANES, LANES), jnp.float32)],
        compiler_params=pltpu.CompilerParams(dimension_semantics=("arbitrary",)),
        name="route_rank",
    )(lt)


def _dispatch_plan(counts):
    _, pair_lo, pair_hi = _pair_tables()
    tiles_c = (counts + TM - 1) // TM
    tiles_g = tiles_c.reshape(N_GROUPS, N_PAIRS).sum(axis=1)
    tiles_g_pad = (tiles_g + SUB - 1) // SUB * SUB
    g_start = jnp.cumsum(tiles_g_pad) - tiles_g_pad
    tc = tiles_c.reshape(N_GROUPS, N_PAIRS)
    c_start = (g_start[:, None] + jnp.cumsum(tc, axis=1) - tc).reshape(N_CLASSES)
    class_start = jnp.zeros((LANES,), jnp.int32).at[:N_CLASSES].set(c_start * TM)

    tile_ids = jnp.arange(N_TILES, dtype=jnp.int32)
    c_end = c_start + tiles_c
    owner = jnp.sum((tile_ids[:, None] >= c_end[None, :]).astype(jnp.int32), axis=1)
    owner = jnp.minimum(owner, N_CLASSES - 1)
    in_class = (tile_ids >= c_start[owner]) & (tile_ids < c_end[owner])
    rows_left = counts[owner] - (tile_ids - c_start[owner]) * TM
    tile_rows = jnp.where(in_class, jnp.minimum(rows_left, TM), 0).astype(jnp.int32)
    pair = owner % N_PAIRS
    tile_ea = jnp.asarray(pair_lo)[pair]
    tile_eb = jnp.asarray(pair_hi)[pair]
    g_end = g_start + tiles_g_pad
    step_first = jnp.arange(N_STEPS3, dtype=jnp.int32) * SUB
    step_group = jnp.sum((step_first[:, None] >= g_end[None, :]).astype(jnp.int32), axis=1)
    step_group = jnp.minimum(step_group, N_GROUPS - 1)
    last_step = jnp.maximum(g_end[-1] // SUB - 1, 0)
    step_block = jnp.minimum(jnp.arange(N_STEPS3, dtype=jnp.int32), last_step)
    return (class_start, tile_ea, tile_eb, tile_rows, step_group.astype(jnp.int32),
            step_block.astype(jnp.int32))


def kernel(x, w_in, rg_conv_w, rg_conv_b, rg_gate_a_w, rg_gate_a_b, rg_gate_x_w, rg_gate_x_b,
           rg_lambda, sc_conv_w, w_out, ln1_g, ln1_b, router_group_w, router_group_b,
           router_expert_w, router_expert_b, exp_w_gate, exp_w_up, exp_w_down, ln2_g, ln2_b):
    bf16 = jnp.bfloat16
    row = lambda v: v.reshape(1, -1)
    pad_g = GROUP_ROWS - N_GROUPS
    pad_e = ROUTER_LANES - ROUTER_ROWS
    wr = jnp.concatenate([router_group_w, jnp.zeros((D_MODEL, pad_g), jnp.float32),
                          router_expert_w, jnp.zeros((D_MODEL, pad_e), jnp.float32)], axis=1)
    br = jnp.concatenate([router_group_b, jnp.zeros((pad_g,), jnp.float32),
                          router_expert_b, jnp.zeros((pad_e,), jnp.float32)]).reshape(1, -1)
    wr_hi = wr.astype(bf16)
    wr_lo = (wr - wr_hi.astype(jnp.float32)).astype(bf16)

    h1, lt, wgu, wdb = _mixer_call(
        x, w_in, rg_conv_w, row(rg_conv_b),
        rg_gate_a_w, row(rg_gate_a_b),
        rg_gate_x_w, row(rg_gate_x_b),
        row(rg_lambda), sc_conv_w, w_out.astype(bf16), row(ln1_g), row(ln1_b),
        jnp.concatenate([wr_hi, wr_lo], axis=1), br, exp_w_gate, exp_w_up, exp_w_down)

    info, meta, counts = _route_call(lt)
    class_start, tile_ea, tile_eb, tile_rows, step_group, step_block = _dispatch_plan(
        counts[:N_CLASSES, 0].astype(jnp.int32))

    xs, wts, pos = _sc_dispatch(h1, meta, info[0], info[1], class_start)

    grp_shape = (N_GROUPS, EXPERTS_PER_GROUP)
    ys = _expert_call(
        step_block, step_group, tile_ea, tile_eb, tile_rows, xs, wts,
        wgu.reshape(grp_shape + (D_MODEL, 2 * D_EXPERT)),
        wdb.reshape(grp_shape + (D_EXPERT, D_MODEL)),
        row(ln2_g), row(ln2_b))

    out_rows = pos.reshape(SEQ, BATCH).T.reshape(-1)
    out = _sc_gather_rows(ys, out_rows, COMBINE_CHUNK, (D_MODEL,))
    return out.reshape(BATCH, SEQ, D_MODEL)
```

```python
import functools

import jax
import jax.numpy as jnp
import numpy as np
from jax import lax
from jax.experimental import pallas as pl
from jax.experimental.pallas import tpu as pltpu
from jax.experimental.pallas import tpu_sc as plsc

D_MODEL = 1024
BATCH = 16
SEQ = 2048
D_RG = 512
D_SC = 512
RG_HEADS = 8
RG_HEAD_DIM = D_RG // RG_HEADS
RG_CONV = 4
RG_C = 8.0
SC_CONV = 3
D_IN_PROJ = 2 * D_RG + 3 * D_SC
N_GROUPS = 4
EXPERTS_PER_GROUP = 8
N_EXPERTS = N_GROUPS * EXPERTS_PER_GROUP
D_EXPERT = D_MODEL // 4
LN_EPS = 1e-5
DEEPNORM_ALPHA = 2.0 ** 0.25

N_TOKENS = BATCH * SEQ
LANES = 128
LANE_TILES = D_MODEL // LANES
MXU_WIDTH = 256
ROUTER_LANES = LANES
GROUP_ROWS = 8
ROUTER_ROWS = GROUP_ROWS + N_EXPERTS

TS = 64
M1 = TS * BATCH
DOT_ROWS = 256

ROUTE_BLOCK = 4096
ROUTE_UNROLL = 8

N_PAIRS = EXPERTS_PER_GROUP * (EXPERTS_PER_GROUP - 1) // 2
N_CLASSES = N_GROUPS * N_PAIRS
BF16_ROWS = 16
TM = -(-int(N_TOKENS / N_CLASSES * 1.09 / 2) // BF16_ROWS) * BF16_ROWS
SUB = 8
N_TILES = -(-(N_TOKENS // TM + N_CLASSES + N_GROUPS * (SUB - 1)) // SUB) * SUB
N_STEPS3 = N_TILES // SUB
P_ROWS = N_TILES * TM

VMEM_LIMIT = 56 * 1024 * 1024


def _layer_norm(z, g, b):
    mu = jnp.mean(z, axis=-1, keepdims=True)
    zc = z - mu
    var = jnp.mean(zc * zc, axis=-1, keepdims=True)
    return zc * lax.rsqrt(var + LN_EPS) * g + b


def _mixer_kernel(x_ref, w_in_f32_ref, cw_ref, cb_ref, wa_heads_ref, ba_ref, wx_heads_ref, bx_ref,
                  lam_ref, scw_ref, w_out_ref, g1_ref, b1_ref, wr_ref, br_ref,
                  wg_ref, wu_ref, wd_ref,
                  h1_hbm, logit_ref, wgu_ref, wdb_ref,
                  xbuf_ref, xsem, hbuf_ref, hsem, cbuf_ref, sbuf_ref, a_ref, u_ref, hstate_ref,
                  wa_ref, wx_ref, w_in_ref):
    c = pl.program_id(0)
    n_steps = SEQ // TS

    def h1_copies(step, slot_):
        return [pltpu.make_async_copy(hbuf_ref.at[slot_, j],
                                      h1_hbm.at[pl.ds(step * M1, M1), j, :], hsem.at[slot_])
                for j in range(LANE_TILES)]

    @pl.when(c >= 2)
    def _():
        for cp in h1_copies(c - 2, c % 2):
            cp.wait()

    wgu_ref[:, :, :D_EXPERT] = wg_ref[...].astype(jnp.bfloat16)
    wgu_ref[:, :, D_EXPERT:] = wu_ref[...].astype(jnp.bfloat16)
    wdb_ref[...] = wd_ref[...].astype(jnp.bfloat16)
    slot = c % 2
    rg_pad = (RG_CONV - 1) * BATCH
    sc_pad = (SC_CONV - 1) * BATCH

    def x_copies(chunk, slot_):
        return [pltpu.make_async_copy(x_ref.at[b, pl.ds(chunk * TS, TS), :],
                                      xbuf_ref.at[slot_, :, b, :], xsem.at[slot_])
                for b in range(BATCH)]

    @pl.when(c == 0)
    def _():
        for cp in x_copies(0, 0):
            cp.start()
        cbuf_ref[0:rg_pad, :] = jnp.zeros((rg_pad, D_RG), jnp.float32)
        sbuf_ref[0:sc_pad, :] = jnp.zeros((sc_pad, D_SC), jnp.float32)
        hstate_ref[...] = jnp.zeros_like(hstate_ref)
        w_in_ref[...] = w_in_f32_ref[...].astype(jnp.bfloat16)
        for heads_ref, bd_ref in ((wa_heads_ref, wa_ref), (wx_heads_ref, wx_ref)):
            bd_ref[...] = jnp.zeros_like(bd_ref)
            for hd in range(RG_HEADS):
                lo = hd * RG_HEAD_DIM
                bd_ref[lo:lo + RG_HEAD_DIM, lo:lo + RG_HEAD_DIM] = (
                    heads_ref[hd].astype(jnp.bfloat16))

    @pl.when(c + 1 < SEQ // TS)
    def _():
        for cp in x_copies(c + 1, 1 - slot):
            cp.start()

    for cp in x_copies(c, slot):
        cp.wait()

    xt_ref = xbuf_ref.at[slot]

    def gate(xcb, w_ref, b_ref):
        parts = [jnp.dot(xcb[:, lo:lo + MXU_WIDTH], w_ref[lo:lo + MXU_WIDTH, lo:lo + MXU_WIDTH],
                         preferred_element_type=jnp.float32)
                 for lo in range(0, D_RG, MXU_WIDTH)]
        return 0.5 + 0.5 * jnp.tanh(0.5 * (jnp.concatenate(parts, axis=1) + b_ref[...]))

    soft_lam = jax.nn.softplus(-lam_ref[...])
    gelu_chunks, y_sc_chunks = [], []
    for r0 in range(0, M1, DOT_ROWS):
        t0 = r0 // BATCH
        xk = xt_ref[t0:t0 + DOT_ROWS // BATCH].reshape(DOT_ROWS, D_MODEL).astype(jnp.bfloat16)

        def proj(lo, width):
            return jnp.dot(xk, w_in_ref[:, lo:lo + width], preferred_element_type=jnp.float32)

        cbuf_ref[rg_pad + r0:rg_pad + r0 + DOT_ROWS, :] = proj(0, D_RG)
        rg_gate = proj(D_RG, D_RG)
        xc = cb_ref[...] + cw_ref[0:1, :] * cbuf_ref[r0:r0 + DOT_ROWS, :]
        for k in range(1, RG_CONV):
            xc = xc + cw_ref[k:k + 1, :] * cbuf_ref[r0 + k * BATCH:r0 + k * BATCH + DOT_ROWS, :]
        sc_b = proj(2 * D_RG, D_SC)
        xcb = xc.astype(jnp.bfloat16)
        r = gate(xcb, wa_ref, ba_ref)
        i = gate(xcb, wx_ref, bx_ref)
        sc_c = proj(2 * D_RG + D_SC, D_SC)
        log_a = (-RG_C) * r * soft_lam
        a_ref[r0:r0 + DOT_ROWS, :] = jnp.exp(log_a)
        th = jnp.tanh(log_a)
        u_ref[r0:r0 + DOT_ROWS, :] = jnp.sqrt(-2.0 * th / (1.0 - th)) * (i * xc)
        gelu_chunks.append(jax.nn.gelu(rg_gate))

        sbuf_ref[sc_pad + r0:sc_pad + r0 + DOT_ROWS, :] = sc_c * proj(2 * D_RG + 2 * D_SC, D_SC)
        conv = scw_ref[0:1, :] * sbuf_ref[r0:r0 + DOT_ROWS, :]
        for k in range(1, SC_CONV):
            conv = conv + scw_ref[k:k + 1, :] * sbuf_ref[r0 + k * BATCH:r0 + k * BATCH + DOT_ROWS, :]
        y_sc_chunks.append((sc_b * conv).astype(jnp.bfloat16))

    cbuf_ref[0:rg_pad, :] = cbuf_ref[M1:M1 + rg_pad, :]
    sbuf_ref[0:sc_pad, :] = sbuf_ref[M1:M1 + sc_pad, :]

    h = hstate_ref[...]
    for t in range(TS):
        h = a_ref[t * BATCH:(t + 1) * BATCH, :] * h + u_ref[t * BATCH:(t + 1) * BATCH, :]
        u_ref[t * BATCH:(t + 1) * BATCH, :] = h
    hstate_ref[...] = h

    tail_rows = M1 // 2
    per_tail = tail_rows // DOT_ROWS

    def out_proj(k):
        r0 = k * tail_rows
        gelu_k = jnp.concatenate(gelu_chunks[k * per_tail:(k + 1) * per_tail], axis=0)
        y_sc_k = jnp.concatenate(y_sc_chunks[k * per_tail:(k + 1) * per_tail], axis=0)
        y_rg = (u_ref[r0:r0 + tail_rows, :] * gelu_k).astype(jnp.bfloat16)
        mix = jnp.dot(y_rg, w_out_ref[0:D_RG, :], preferred_element_type=jnp.float32)
        return mix + jnp.dot(y_sc_k, w_out_ref[D_RG:, :], preferred_element_type=jnp.float32)

    def finish(k, mix):
        r0 = k * tail_rows
        t0 = r0 // BATCH
        x_rows = xt_ref[t0:t0 + tail_rows // BATCH].reshape(tail_rows, D_MODEL)
        h1 = _layer_norm(DEEPNORM_ALPHA * x_rows + mix, g1_ref[...], b1_ref[...])
        for t in range(LANE_TILES):
            hbuf_ref[slot, t, r0:r0 + tail_rows, :] = h1[:, t * LANES:(t + 1) * LANES]
        h_hi = h1.astype(jnp.bfloat16)
        h_lo = (h1 - h_hi.astype(jnp.float32)).astype(jnp.bfloat16)
        both = jnp.dot(h_hi, wr_ref[...], preferred_element_type=jnp.float32)
        logits = both[:, :ROUTER_LANES] + both[:, ROUTER_LANES:] + br_ref[...]
        logits = logits + jnp.dot(h_lo, wr_ref[:, :ROUTER_LANES],
                                  preferred_element_type=jnp.float32)
        logit_ref[:, r0:r0 + tail_rows] = logits.T[:ROUTER_ROWS, :]

    n_tails = M1 // tail_rows
    mix = out_proj(0)
    for k in range(n_tails):
        next_mix = out_proj(k + 1) if k + 1 < n_tails else None
        finish(k, mix)
        mix = next_mix

    for cp in h1_copies(c, slot):
        cp.start()

    @pl.when(c == n_steps - 1)
    def _():
        if n_steps >= 2:
            for cp in h1_copies(c - 1, 1 - slot):
                cp.wait()
        for cp in h1_copies(c, slot):
            cp.wait()


def _const_spec(shape, pipeline_mode=None):
    if pipeline_mode is None:
        return pl.BlockSpec(shape, lambda c: (0,) * len(shape))
    return pl.BlockSpec(shape, lambda c: (0,) * len(shape), pipeline_mode=pipeline_mode)


def _mixer_call(x, w_in, cw, cb, wa, ba, wx, bx, lam, scw, w_out, g1, b1, wr, br, wg, wu, wd):
    n_chunks = SEQ // TS
    epc = N_EXPERTS // n_chunks
    assert epc * n_chunks == N_EXPERTS
    in_specs = [
        pl.BlockSpec(memory_space=pl.ANY),
        _const_spec((D_MODEL, D_IN_PROJ), pl.Buffered(1)),
        _const_spec((RG_CONV, D_RG)), _const_spec((1, D_RG)),
        _const_spec((RG_HEADS, RG_HEAD_DIM, RG_HEAD_DIM)), _const_spec((1, D_RG)),
        _const_spec((RG_HEADS, RG_HEAD_DIM, RG_HEAD_DIM)), _const_spec((1, D_RG)),
        _const_spec((1, D_RG)),
        _const_spec((SC_CONV, D_SC)),
        _const_spec((D_MODEL, D_MODEL), pl.Buffered(1)),
        _const_spec((1, D_MODEL)), _const_spec((1, D_MODEL)),
        _const_spec((D_MODEL, 2 * ROUTER_LANES)), _const_spec((1, ROUTER_LANES)),
        pl.BlockSpec((epc, D_MODEL, D_EXPERT), lambda c: (c, 0, 0)),
        pl.BlockSpec((epc, D_MODEL, D_EXPERT), lambda c: (c, 0, 0)),
        pl.BlockSpec((epc, D_EXPERT, D_MODEL), lambda c: (c, 0, 0)),
    ]
    out_specs = [
        pl.BlockSpec(memory_space=pl.ANY),
        pl.BlockSpec((ROUTER_ROWS, M1), lambda c: (0, c)),
        pl.BlockSpec((epc, D_MODEL, 2 * D_EXPERT), lambda c: (c, 0, 0)),
        pl.BlockSpec((epc, D_EXPERT, D_MODEL), lambda c: (c, 0, 0)),
    ]
    return pl.pallas_call(
        _mixer_kernel,
        grid=(n_chunks,),
        in_specs=in_specs,
        out_specs=out_specs,
        out_shape=[
            jax.ShapeDtypeStruct((N_TOKENS, LANE_TILES, LANES), jnp.float32),
            jax.ShapeDtypeStruct((ROUTER_ROWS, N_TOKENS), jnp.float32),
            jax.ShapeDtypeStruct((N_EXPERTS, D_MODEL, 2 * D_EXPERT), jnp.bfloat16),
            jax.ShapeDtypeStruct((N_EXPERTS, D_EXPERT, D_MODEL), jnp.bfloat16),
        ],
        scratch_shapes=[
            pltpu.VMEM((2, TS, BATCH, D_MODEL), jnp.float32),
            pltpu.SemaphoreType.DMA((2,)),
            pltpu.VMEM((2, LANE_TILES, M1, LANES), jnp.float32),
            pltpu.SemaphoreType.DMA((2,)),
            pltpu.VMEM((M1 + (RG_CONV - 1) * BATCH, D_RG), jnp.float32),
            pltpu.VMEM((M1 + (SC_CONV - 1) * BATCH, D_SC), jnp.float32),
            pltpu.VMEM((M1, D_RG), jnp.float32),
            pltpu.VMEM((M1, D_RG), jnp.float32),
            pltpu.VMEM((BATCH, D_RG), jnp.float32),
            pltpu.VMEM((D_RG, D_RG), jnp.bfloat16),
            pltpu.VMEM((D_RG, D_RG), jnp.bfloat16),
            pltpu.VMEM((D_MODEL, D_IN_PROJ), jnp.bfloat16),
        ],
        compiler_params=pltpu.CompilerParams(
            dimension_semantics=("arbitrary",), vmem_limit_bytes=VMEM_LIMIT),
        name="mixer_ln_router",
    )(x, w_in, cw, cb, wa, ba, wx, bx, lam, scw, w_out, g1, b1, wr, br, wg, wu, wd)


def _expert_kernel(blk_ref, grp_ref, ea_ref, eb_ref, rows_ref,
                   xs_hbm, wt_ref, wgu_ref, wd_ref, g2_ref, b2_ref, ys_hbm,
                   xbuf_ref, xsem, obuf_ref, osem):
    s = pl.program_id(0)
    slot = s % 2
    tile_rows = [rows_ref[s * SUB + j] for j in range(SUB)]
    block_rows = SUB * TM

    def fetches(step, slot_):
        return [pltpu.make_async_copy(xs_hbm.at[pl.ds(step * block_rows, block_rows), j, :],
                                      xbuf_ref.at[slot_, j], xsem.at[slot_])
                for j in range(LANE_TILES)]

    def write_backs(step, slot_):
        return [pltpu.make_async_copy(obuf_ref.at[slot_, j],
                                      ys_hbm.at[pl.ds(step * block_rows, block_rows), j, :],
                                      osem.at[slot_])
                for j in range(LANE_TILES)]

    def did_work(step):
        return blk_ref[jnp.clip(step, 0, N_STEPS3 - 1)] == step

    @pl.when(jnp.logical_and(s == 0, did_work(0)))
    def _():
        for cp in fetches(0, 0):
            cp.start()

    @pl.when(did_work(s + 1))
    def _():
        for cp in fetches(s + 1, 1 - slot):
            cp.start()

    @pl.when(did_work(s - 2))
    def _():
        for cp in write_backs(s - 2, slot):
            cp.wait()

    @pl.when(did_work(s))
    def _():
        for cp in fetches(s, slot):
            cp.wait()
        for j in range(SUB):
            tile = s * SUB + j
            first_row = j * TM
            in_use = lax.broadcasted_iota(jnp.int32, (TM, 1), 0) < tile_rows[j]
            x = jnp.concatenate([xbuf_ref[slot, t, first_row:first_row + TM, :]
                                 for t in range(LANE_TILES)], axis=1)
            x = jnp.where(in_use, x, 0.0)
            xb = x.astype(jnp.bfloat16)
            wt = jnp.where(in_use, wt_ref[first_row:first_row + TM, :], 0.0)
            y = jnp.zeros((TM, D_MODEL), jnp.float32)
            for e_ref, col in ((ea_ref, 0), (eb_ref, 1)):
                e = e_ref[tile]
                hgu = jnp.dot(xb, wgu_ref[e], preferred_element_type=jnp.float32)
                hg = hgu[:, :D_EXPERT]
                hid = (hg * (0.5 + 0.5 * jnp.tanh(0.5 * hg))) * hgu[:, D_EXPERT:] * wt[:, col:col + 1]
                y = y + jnp.dot(hid.astype(jnp.bfloat16), wd_ref[e],
                                preferred_element_type=jnp.float32)
            out = _layer_norm(DEEPNORM_ALPHA * x + y, g2_ref[...], b2_ref[...])
            for t in range(LANE_TILES):
                obuf_ref[slot, t, first_row:first_row + TM, :] = out[:, t * LANES:(t + 1) * LANES]
        for cp in write_backs(s, slot):
            cp.start()

    @pl.when(s == N_STEPS3 - 1)
    def _():
        @pl.when(did_work(s - 1))
        def _():
            for cp in write_backs(s - 1, 1 - slot):
                cp.wait()

        @pl.when(did_work(s))
        def _():
            for cp in write_backs(s, slot):
                cp.wait()


def _expert_call(step_block, step_group, tile_ea, tile_eb, tile_rows, xs, wts, wgu, wd, g2, b2):
    rows = SUB * TM
    grid_spec = pltpu.PrefetchScalarGridSpec(
        num_scalar_prefetch=5,
        grid=(N_STEPS3,),
        in_specs=[
            pl.BlockSpec(memory_space=pl.ANY),
            pl.BlockSpec((rows, LANES), lambda s, blk, grp, ea, eb, va: (blk[s], 0)),
            pl.BlockSpec((None, EXPERTS_PER_GROUP, D_MODEL, 2 * D_EXPERT),
                         lambda s, blk, grp, ea, eb, va: (grp[s], 0, 0, 0)),
            pl.BlockSpec((None, EXPERTS_PER_GROUP, D_EXPERT, D_MODEL),
                         lambda s, blk, grp, ea, eb, va: (grp[s], 0, 0, 0)),
            pl.BlockSpec((1, D_MODEL), lambda s, blk, grp, ea, eb, va: (0, 0)),
            pl.BlockSpec((1, D_MODEL), lambda s, blk, grp, ea, eb, va: (0, 0)),
        ],
        out_specs=pl.BlockSpec(memory_space=pl.ANY),
        scratch_shapes=[
            pltpu.VMEM((2, LANE_TILES, rows, LANES), jnp.float32),
            pltpu.SemaphoreType.DMA((2,)),
            pltpu.VMEM((2, LANE_TILES, rows, LANES), jnp.float32),
            pltpu.SemaphoreType.DMA((2,)),
        ],
    )
    return pl.pallas_call(
        _expert_kernel,
        grid_spec=grid_spec,
        out_shape=jax.ShapeDtypeStruct((P_ROWS, LANE_TILES, LANES), jnp.float32),
        compiler_params=pltpu.CompilerParams(
            dimension_semantics=("arbitrary",), vmem_limit_bytes=VMEM_LIMIT),
        name="experts_ln",
    )(step_block, step_group, tile_ea, tile_eb, tile_rows, xs, wts, wgu, wd, g2, b2)


SC_CORES = 2
SC_SUBCORES = 16
SC_WORKERS = SC_CORES * SC_SUBCORES
COMBINE_CHUNK = 32


def _sc_gather_rows(table, idx, chunk, out_row_shape=None):
    n_out, = idx.shape
    row_shape = table.shape[1:]
    out_row_shape = row_shape if out_row_shape is None else out_row_shape
    per_w = n_out // SC_WORKERS
    n_chunks = per_w // chunk
    assert per_w * SC_WORKERS == n_out and n_chunks * chunk == per_w
    assert n_chunks % 2 == 0 and chunk % 8 == 0
    mesh = plsc.VectorSubcoreMesh(core_axis_name="c", subcore_axis_name="s")

    @functools.partial(
        pl.kernel, mesh=mesh,
        out_type=jax.ShapeDtypeStruct((n_out,) + out_row_shape, table.dtype),
        scratch_types=[
            pltpu.VMEM((per_w,), jnp.int32),
            pltpu.VMEM((2, chunk) + row_shape, table.dtype),
            pltpu.SemaphoreType.DMA((2,)),
            pltpu.SemaphoreType.DMA((2,)),
        ],
    )
    def gather_kernel(table_hbm, idx_hbm, out_hbm, idx_v, buf, gsem, wsem):
        wid = lax.axis_index("s") * SC_CORES + lax.axis_index("c")
        base = wid * per_w
        pltpu.sync_copy(idx_hbm.at[pl.ds(base, per_w)], idx_v)

        def gather(j, slot):
            rows = idx_v.at[pl.ds(j * chunk, chunk)]
            return pltpu.make_async_copy(table_hbm.at[rows], buf.at[slot], gsem.at[slot])

        def write(j, slot):
            dst = out_hbm.at[pl.ds(base + j * chunk, chunk)]
            src = buf.at[slot].reshape((chunk,) + out_row_shape)
            return pltpu.make_async_copy(src, dst, wsem.at[slot])

        gather(0, 0).start()

        @pl.loop(0, n_chunks, step=2)
        def _(j0):
            for slot in range(2):
                j = j0 + slot
                gather(j, slot).wait()

                @pl.when(j >= 1)
                def _():
                    write(j - 1, 1 - slot).wait()

                @pl.when(j + 1 < n_chunks)
                def _():
                    gather(j + 1, 1 - slot).start()

                write(j, slot).start()

        write(n_chunks - 1, (n_chunks - 1) % 2).wait()

    return gather_kernel(table, idx)


SC_LANES = 16


def _sc_dispatch(rows, meta, cls, rank, class_start):
    n_rows = rows.shape[0]
    chunk = COMBINE_CHUNK
    per_w = n_rows // SC_WORKERS
    n_chunks = per_w // chunk
    assert per_w * SC_WORKERS == n_rows and n_chunks * chunk == per_w
    assert n_chunks % 2 == 0 and chunk % SC_LANES == 0
    mesh = plsc.VectorSubcoreMesh(core_axis_name="c", subcore_axis_name="s")

    @functools.partial(
        pl.kernel, mesh=mesh,
        out_type=[
            jax.ShapeDtypeStruct((P_ROWS,) + rows.shape[1:], rows.dtype),
            jax.ShapeDtypeStruct((P_ROWS,) + meta.shape[1:], meta.dtype),
            jax.ShapeDtypeStruct((n_rows,), jnp.int32),
        ],
        scratch_types=[
            pltpu.VMEM((per_w,), jnp.int32),
            pltpu.VMEM((per_w,), jnp.int32),
            pltpu.VMEM((LANES,), jnp.int32),
            pltpu.VMEM((per_w,), jnp.int32),
            pltpu.VMEM((n_chunks, chunk), jnp.int32),
            pltpu.VMEM((2, chunk) + rows.shape[1:], rows.dtype),
            pltpu.VMEM((2, chunk) + meta.shape[1:], meta.dtype),
            pltpu.SemaphoreType.DMA((2,)),
            pltpu.SemaphoreType.DMA((2,)),
            pltpu.SemaphoreType.DMA((2,)),
            pltpu.SemaphoreType.DMA((2,)),
        ],
        compiler_params=pltpu.CompilerParams(needs_layout_passes=False),
    )
    def dispatch_kernel(rows_hbm, meta_hbm, cls_hbm, rank_hbm, start_hbm,
                        xs_hbm, ws_hbm, pos_hbm,
                        cls_v, rank_v, start_v, pos_flat, pos_v, rbuf, mbuf,
                        rsem, msem, xsem, wsem):
        wid = lax.axis_index("s") * SC_CORES + lax.axis_index("c")
        base = wid * per_w
        pltpu.sync_copy(cls_hbm.at[pl.ds(base, per_w)], cls_v)
        pltpu.sync_copy(rank_hbm.at[pl.ds(base, per_w)], rank_v)
        pltpu.sync_copy(start_hbm, start_v)

        @pl.loop(0, n_chunks)
        def _(j):
            for k in range(chunk // SC_LANES):
                off = j * chunk + k * SC_LANES
                c = cls_v[pl.ds(off, SC_LANES)]
                pos = plsc.load_gather(start_v, [c]) + rank_v[pl.ds(off, SC_LANES)]
                pos_flat[pl.ds(off, SC_LANES)] = pos
                pos_v[j, pl.ds(k * SC_LANES, SC_LANES)] = pos

        pltpu.sync_copy(pos_flat, pos_hbm.at[pl.ds(base, per_w)])

        def read_rows(j, slot):
            src = rows_hbm.at[pl.ds(base + j * chunk, chunk)]
            return pltpu.make_async_copy(src, rbuf.at[slot], rsem.at[slot])

        def read_meta(j, slot):
            src = meta_hbm.at[pl.ds(base + j * chunk, chunk)]
            return pltpu.make_async_copy(src, mbuf.at[slot], msem.at[slot])

        def put_rows(j, slot):
            return pltpu.make_async_copy(rbuf.at[slot], xs_hbm.at[pos_v.at[j]], xsem.at[slot])

        def put_meta(j, slot):
            return pltpu.make_async_copy(mbuf.at[slot], ws_hbm.at[pos_v.at[j]], wsem.at[slot])

        read_rows(0, 0).start()
        read_meta(0, 0).start()

        @pl.loop(0, n_chunks, step=2)
        def _(j0):
            for slot in range(2):
                j = j0 + slot
                read_rows(j, slot).wait()
                read_meta(j, slot).wait()

                @pl.when(j >= 1)
                def _():
                    put_rows(j - 1, 1 - slot).wait()
                    put_meta(j - 1, 1 - slot).wait()

                @pl.when(j + 1 < n_chunks)
                def _():
                    read_rows(j + 1, 1 - slot).start()
                    read_meta(j + 1, 1 - slot).start()

                put_rows(j, slot).start()
                put_meta(j, slot).start()

        put_rows(n_chunks - 1, (n_chunks - 1) % 2).wait()
        put_meta(n_chunks - 1, (n_chunks - 1) % 2).wait()

    return dispatch_kernel(rows, meta, cls, rank, class_start)


def _pair_tables():
    pair_of = np.zeros((EXPERTS_PER_GROUP, EXPERTS_PER_GROUP), np.int32)
    lo = np.zeros((N_PAIRS,), np.int32)
    hi = np.zeros((N_PAIRS,), np.int32)
    p = 0
    for a in range(EXPERTS_PER_GROUP):
        for b in range(a + 1, EXPERTS_PER_GROUP):
            pair_of[a, b] = pair_of[b, a] = p
            lo[p], hi[p] = a, b
            p += 1
    return pair_of, lo, hi


def _route_kernel(lt_ref, info_ref, meta_ref, counts_ref, run_ref):
    step = pl.program_id(0)

    @pl.when(step == 0)
    def _():
        run_ref[...] = jnp.zeros_like(run_ref)

    f32 = jnp.float32
    sub8 = lax.broadcasted_iota(jnp.int32, (8, LANES), 0).astype(f32)
    row_id = lax.broadcasted_iota(jnp.int32, (LANES, LANES), 0)
    col_id = lax.broadcasted_iota(jnp.int32, (LANES, LANES), 1)
    prefix_mat = (row_id <= col_id).astype(jnp.bfloat16)
    ones_mat = jnp.ones((LANES, LANES), jnp.bfloat16)
    neg_inf = f32(-jnp.inf)

    def first_index_of_max(v):
        m = jnp.max(v, axis=0, keepdims=True)
        idx = jnp.min(jnp.where(v == m, sub8, f32(8)), axis=0, keepdims=True)
        return m, idx

    def lane_tile(k, run):
        lanes = pl.ds(pl.multiple_of(k * LANES, LANES), LANES)
        g = jnp.where(sub8 < N_GROUPS, lt_ref[0:GROUP_ROWS, lanes], neg_inf)
        g_max, g_idx = first_index_of_max(g)
        g_top_p = 1.0 / jnp.sum(jnp.exp(g - g_max), axis=0, keepdims=True)

        e_sel = lt_ref[GROUP_ROWS:GROUP_ROWS + EXPERTS_PER_GROUP, lanes]
        for grp in range(1, N_GROUPS):
            lo = GROUP_ROWS + grp * EXPERTS_PER_GROUP
            e_sel = jnp.where(g_idx == grp, lt_ref[lo:lo + EXPERTS_PER_GROUP, lanes], e_sel)
        m1, i1 = first_index_of_max(e_sel)
        rest = jnp.where(sub8 == i1, neg_inf, e_sel)
        m2 = jnp.max(rest, axis=0, keepdims=True)
        i2 = jnp.min(jnp.where((rest == m2) & (sub8 != i1), sub8, f32(8)), axis=0, keepdims=True)

        e = jnp.exp(m2 - m1)
        w1 = g_top_p / (1.0 + e)
        w2 = g_top_p * e / (1.0 + e)
        first_is_lo = i1 < i2
        w_lo = jnp.where(first_is_lo, w1, w2)
        w_hi = jnp.where(first_is_lo, w2, w1)
        lo_e = jnp.minimum(i1, i2)
        hi_e = jnp.maximum(i1, i2)
        pair = lo_e * (2 * EXPERTS_PER_GROUP - 1 - lo_e) * 0.5 + hi_e - lo_e - 1.0
        cls = (g_idx * N_PAIRS + pair).astype(jnp.int32)

        onehot = (row_id == cls).astype(jnp.bfloat16)
        csum = jnp.dot(onehot, prefix_mat, preferred_element_type=f32)
        rank = jnp.sum(onehot.astype(f32) * (csum + run), axis=0, keepdims=True) - 1.0
        run = run + jnp.dot(onehot, ones_mat, preferred_element_type=f32)

        info_ref[:, lanes] = jnp.where(sub8 == 0, cls, jnp.where(sub8 == 1, rank.astype(jnp.int32), 0))
        meta8 = jnp.where(sub8 == 0, w_lo, jnp.where(sub8 == 1, w_hi, 0.0))
        meta_t = jnp.concatenate([meta8, jnp.zeros((LANES - 8, LANES), f32)], axis=0)
        meta_ref[lanes, :] = meta_t.T
        return run

    run = lax.fori_loop(0, ROUTE_BLOCK // LANES, lane_tile, run_ref[...], unroll=ROUTE_UNROLL)
    run_ref[...] = run
    counts_ref[...] = run


def _route_call(lt):
    return pl.pallas_call(
        _route_kernel,
        grid=(N_TOKENS // ROUTE_BLOCK,),
        in_specs=[pl.BlockSpec((ROUTER_ROWS, ROUTE_BLOCK), lambda i: (0, i))],
        out_specs=[
            pl.BlockSpec((8, ROUTE_BLOCK), lambda i: (0, i)),
            pl.BlockSpec((ROUTE_BLOCK, LANES), lambda i: (i, 0)),
            pl.BlockSpec((LANES, LANES), lambda i: (0, 0)),
        ],
        out_shape=[
            jax.ShapeDtypeStruct((8, N_TOKENS), jnp.int32),
            jax.ShapeDtypeStruct((N_TOKENS, LANES), jnp.float32),
            jax.ShapeDtypeStruct((LANES, LANES), jnp.float32),
        ],
        scratch_shapes=[pltpu.VMEM((LANES, LANES), jnp.float32)],
        compiler_params=pltpu.CompilerParams(dimension_semantics=("arbitrary",)),
        name="route_rank",
    )(lt)


def _dispatch_plan(counts):
    _, pair_lo, pair_hi = _pair_tables()
    tiles_c = (counts + TM - 1) // TM
    tiles_g = tiles_c.reshape(N_GROUPS, N_PAIRS).sum(axis=1)
    tiles_g_pad = (tiles_g + SUB - 1) // SUB * SUB
    g_start = jnp.cumsum(tiles_g_pad) - tiles_g_pad
    tc = tiles_c.reshape(N_GROUPS, N_PAIRS)
    c_start = (g_start[:, None] + jnp.cumsum(tc, axis=1) - tc).reshape(N_CLASSES)
    class_start = jnp.zeros((LANES,), jnp.int32).at[:N_CLASSES].set(c_start * TM)

    tile_ids = jnp.arange(N_TILES, dtype=jnp.int32)
    c_end = c_start + tiles_c
    owner = jnp.sum((tile_ids[:, None] >= c_end[None, :]).astype(jnp.int32), axis=1)
    owner = jnp.minimum(owner, N_CLASSES - 1)
    in_class = (tile_ids >= c_start[owner]) & (tile_ids < c_end[owner])
    rows_left = counts[owner] - (tile_ids - c_start[owner]) * TM
    tile_rows = jnp.where(in_class, jnp.minimum(rows_left, TM), 0).astype(jnp.int32)
    pair = owner % N_PAIRS
    tile_ea = jnp.asarray(pair_lo)[pair]
    tile_eb = jnp.asarray(pair_hi)[pair]
    g_end = g_start + tiles_g_pad
    step_first = jnp.arange(N_STEPS3, dtype=jnp.int32) * SUB
    step_group = jnp.sum((step_first[:, None] >= g_end[None, :]).astype(jnp.int32), axis=1)
    step_group = jnp.minimum(step_group, N_GROUPS - 1)
    last_step = jnp.maximum(g_end[-1] // SUB - 1, 0)
    step_block = jnp.minimum(jnp.arange(N_STEPS3, dtype=jnp.int32), last_step)
    return (class_start, tile_ea, tile_eb, tile_rows, step_group.astype(jnp.int32),
            step_block.astype(jnp.int32))


def kernel(x, w_in, rg_conv_w, rg_conv_b, rg_gate_a_w, rg_gate_a_b, rg_gate_x_w, rg_gate_x_b,
           rg_lambda, sc_conv_w, w_out, ln1_g, ln1_b, router_group_w, router_group_b,
           router_expert_w, router_expert_b, exp_w_gate, exp_w_up, exp_w_down, ln2_g, ln2_b):
    bf16 = jnp.bfloat16
    row = lambda v: v.reshape(1, -1)
    pad_g = GROUP_ROWS - N_GROUPS
    pad_e = ROUTER_LANES - ROUTER_ROWS
    wr = jnp.concatenate([router_group_w, jnp.zeros((D_MODEL, pad_g), jnp.float32),
                          router_expert_w, jnp.zeros((D_MODEL, pad_e), jnp.float32)], axis=1)
    br = jnp.concatenate([router_group_b, jnp.zeros((pad_g,), jnp.float32),
                          router_expert_b, jnp.zeros((pad_e,), jnp.float32)]).reshape(1, -1)
    wr_hi = wr.astype(bf16)
    wr_lo = (wr - wr_hi.astype(jnp.float32)).astype(bf16)

    h1, lt, wgu, wdb = _mixer_call(
        x, w_in, rg_conv_w, row(rg_conv_b),
        rg_gate_a_w, row(rg_gate_a_b),
        rg_gate_x_w, row(rg_gate_x_b),
        row(rg_lambda), sc_conv_w, w_out.astype(bf16), row(ln1_g), row(ln1_b),
        jnp.concatenate([wr_hi, wr_lo], axis=1), br, exp_w_gate, exp_w_up, exp_w_down)

    info, meta, counts = _route_call(lt)
    class_start, tile_ea, tile_eb, tile_rows, step_group, step_block = _dispatch_plan(
        counts[:N_CLASSES, 0].astype(jnp.int32))

    xs, wts, pos = _sc_dispatch(h1, meta, info[0], info[1], class_start)

    grp_shape = (N_GROUPS, EXPERTS_PER_GROUP)
    ys = _expert_call(
        step_block, step_group, tile_ea, tile_eb, tile_rows, xs, wts,
        wgu.reshape(grp_shape + (D_MODEL, 2 * D_EXPERT)),
        wdb.reshape(grp_shape + (D_EXPERT, D_MODEL)),
        row(ln2_g), row(ln2_b))

    out_rows = pos.reshape(SEQ, BATCH).T.reshape(-1)
    out = _sc_gather_rows(ys, out_rows, COMBINE_CHUNK, (D_MODEL,))
    return out.reshape(BATCH, SEQ, D_MODEL)
```

```python
import functools

import jax
import jax.numpy as jnp
import numpy as np
from jax import lax
from jax.experimental import pallas as pl
from jax.experimental.pallas import tpu as pltpu
from jax.experimental.pallas import tpu_sc as plsc

D_MODEL = 1024
BATCH = 16
SEQ = 2048
D_RG = 512
D_SC = 512
RG_HEADS = 8
RG_HEAD_DIM = D_RG // RG_HEADS
RG_CONV = 4
RG_C = 8.0
SC_CONV = 3
D_IN_PROJ = 2 * D_RG + 3 * D_SC
N_GROUPS = 4
EXPERTS_PER_GROUP = 8
N_EXPERTS = N_GROUPS * EXPERTS_PER_GROUP
D_EXPERT = D_MODEL // 4
LN_EPS = 1e-5
DEEPNORM_ALPHA = 2.0 ** 0.25

N_TOKENS = BATCH * SEQ
LANES = 128
LANE_TILES = D_MODEL // LANES
MXU_WIDTH = 256
ROUTER_LANES = LANES
GROUP_ROWS = 8
ROUTER_ROWS = GROUP_ROWS + N_EXPERTS

TS = 64
M1 = TS * BATCH
DOT_ROWS = 256

ROUTE_BLOCK = 4096
ROUTE_UNROLL = 8

N_PAIRS = EXPERTS_PER_GROUP * (EXPERTS_PER_GROUP - 1) // 2
N_CLASSES = N_GROUPS * N_PAIRS
BF16_ROWS = 16
TM = -(-int(N_TOKENS / N_CLASSES * 1.09 / 2) // BF16_ROWS) * BF16_ROWS
SUB = 8
N_TILES = -(-(N_TOKENS // TM + N_CLASSES + N_GROUPS * (SUB - 1)) // SUB) * SUB
N_STEPS3 = N_TILES // SUB
P_ROWS = N_TILES * TM

VMEM_LIMIT = 58 * 1024 * 1024


def _layer_norm(z, g, b):
    mu = jnp.mean(z, axis=-1, keepdims=True)
    zc = z - mu
    var = jnp.mean(zc * zc, axis=-1, keepdims=True)
    return zc * lax.rsqrt(var + LN_EPS) * g + b


def _mixer_kernel(x_ref, w_in_f32_ref, cw_ref, cb_ref, wa_heads_ref, ba_ref, wx_heads_ref, bx_ref,
                  lam_ref, scw_ref, w_out_f32_ref, g1_ref, b1_ref, wr_ref, br_ref,
                  wg_ref, wu_ref, wd_ref,
                  h1_hbm, logit_ref, wgu_ref, wdb_ref,
                  xbuf_ref, xsem, hbuf_ref, hsem, cbuf_ref, sbuf_ref, a_ref, u_ref, hstate_ref,
                  wa_ref, wx_ref, w_in_ref, w_out_ref):
    c = pl.program_id(0)
    n_steps = SEQ // TS

    def h1_copies(step, slot_):
        return [pltpu.make_async_copy(hbuf_ref.at[slot_, j],
                                      h1_hbm.at[pl.ds(step * M1, M1), j, :], hsem.at[slot_])
                for j in range(LANE_TILES)]

    @pl.when(c >= 2)
    def _():
        for cp in h1_copies(c - 2, c % 2):
            cp.wait()

    wgu_ref[:, :, :D_EXPERT] = wg_ref[...].astype(jnp.bfloat16)
    wgu_ref[:, :, D_EXPERT:] = wu_ref[...].astype(jnp.bfloat16)
    wdb_ref[...] = wd_ref[...].astype(jnp.bfloat16)
    slot = c % 2
    rg_pad = (RG_CONV - 1) * BATCH
    sc_pad = (SC_CONV - 1) * BATCH

    def x_copies(chunk, slot_):
        return [pltpu.make_async_copy(x_ref.at[b, pl.ds(chunk * TS, TS), :],
                                      xbuf_ref.at[slot_, :, b, :], xsem.at[slot_])
                for b in range(BATCH)]

    @pl.when(c == 0)
    def _():
        for cp in x_copies(0, 0):
            cp.start()
        cbuf_ref[0:rg_pad, :] = jnp.zeros((rg_pad, D_RG), jnp.float32)
        sbuf_ref[0:sc_pad, :] = jnp.zeros((sc_pad, D_SC), jnp.float32)
        hstate_ref[...] = jnp.zeros_like(hstate_ref)
        w_in_ref[...] = w_in_f32_ref[...].astype(jnp.bfloat16)
        w_out_ref[...] = w_out_f32_ref[...].astype(jnp.bfloat16)
        for heads_ref, bd_ref in ((wa_heads_ref, wa_ref), (wx_heads_ref, wx_ref)):
            bd_ref[...] = jnp.zeros_like(bd_ref)
            for hd in range(RG_HEADS):
                lo = hd * RG_HEAD_DIM
                bd_ref[lo:lo + RG_HEAD_DIM, lo:lo + RG_HEAD_DIM] = (
                    heads_ref[hd].astype(jnp.bfloat16))

    @pl.when(c + 1 < SEQ // TS)
    def _():
        for cp in x_copies(c + 1, 1 - slot):
            cp.start()

    for cp in x_copies(c, slot):
        cp.wait()

    xt_ref = xbuf_ref.at[slot]

    def gate(xcb, w_ref, b_ref):
        parts = [jnp.dot(xcb[:, lo:lo + MXU_WIDTH], w_ref[lo:lo + MXU_WIDTH, lo:lo + MXU_WIDTH],
                         preferred_element_type=jnp.float32)
                 for lo in range(0, D_RG, MXU_WIDTH)]
        return 0.5 + 0.5 * jnp.tanh(0.5 * (jnp.concatenate(parts, axis=1) + b_ref[...]))

    soft_lam = jax.nn.softplus(-lam_ref[...])
    gelu_chunks, y_sc_chunks = [], []
    for r0 in range(0, M1, DOT_ROWS):
        t0 = r0 // BATCH
        xk = xt_ref[t0:t0 + DOT_ROWS // BATCH].reshape(DOT_ROWS, D_MODEL).astype(jnp.bfloat16)

        def proj(lo, width):
            return jnp.dot(xk, w_in_ref[:, lo:lo + width], preferred_element_type=jnp.float32)

        cbuf_ref[rg_pad + r0:rg_pad + r0 + DOT_ROWS, :] = proj(0, D_RG)
        rg_gate = proj(D_RG, D_RG)
        xc = cb_ref[...] + cw_ref[0:1, :] * cbuf_ref[r0:r0 + DOT_ROWS, :]
        for k in range(1, RG_CONV):
            xc = xc + cw_ref[k:k + 1, :] * cbuf_ref[r0 + k * BATCH:r0 + k * BATCH + DOT_ROWS, :]
        sc_b = proj(2 * D_RG, D_SC)
        xcb = xc.astype(jnp.bfloat16)
        r = gate(xcb, wa_ref, ba_ref)
        i = gate(xcb, wx_ref, bx_ref)
        sc_c = proj(2 * D_RG + D_SC, D_SC)
        log_a = (-RG_C) * r * soft_lam
        a_ref[r0:r0 + DOT_ROWS, :] = jnp.exp(log_a)
        th = jnp.tanh(log_a)
        u_ref[r0:r0 + DOT_ROWS, :] = jnp.sqrt(-2.0 * th / (1.0 - th)) * (i * xc)
        gelu_chunks.append(jax.nn.gelu(rg_gate))

        sbuf_ref[sc_pad + r0:sc_pad + r0 + DOT_ROWS, :] = sc_c * proj(2 * D_RG + 2 * D_SC, D_SC)
        conv = scw_ref[0:1, :] * sbuf_ref[r0:r0 + DOT_ROWS, :]
        for k in range(1, SC_CONV):
            conv = conv + scw_ref[k:k + 1, :] * sbuf_ref[r0 + k * BATCH:r0 + k * BATCH + DOT_ROWS, :]
        y_sc_chunks.append((sc_b * conv).astype(jnp.bfloat16))

    cbuf_ref[0:rg_pad, :] = cbuf_ref[M1:M1 + rg_pad, :]
    sbuf_ref[0:sc_pad, :] = sbuf_ref[M1:M1 + sc_pad, :]

    h = hstate_ref[...]
    for t in range(TS):
        h = a_ref[t * BATCH:(t + 1) * BATCH, :] * h + u_ref[t * BATCH:(t + 1) * BATCH, :]
        u_ref[t * BATCH:(t + 1) * BATCH, :] = h
    hstate_ref[...] = h

    tail_rows = M1 // 2
    per_tail = tail_rows // DOT_ROWS

    def out_proj(k):
        r0 = k * tail_rows
        gelu_k = jnp.concatenate(gelu_chunks[k * per_tail:(k + 1) * per_tail], axis=0)
        y_sc_k = jnp.concatenate(y_sc_chunks[k * per_tail:(k + 1) * per_tail], axis=0)
        y_rg = (u_ref[r0:r0 + tail_rows, :] * gelu_k).astype(jnp.bfloat16)
        mix = jnp.dot(y_rg, w_out_ref[0:D_RG, :], preferred_element_type=jnp.float32)
        return mix + jnp.dot(y_sc_k, w_out_ref[D_RG:, :], preferred_element_type=jnp.float32)

    def finish(k, mix):
        r0 = k * tail_rows
        t0 = r0 // BATCH
        x_rows = xt_ref[t0:t0 + tail_rows // BATCH].reshape(tail_rows, D_MODEL)
        h1 = _layer_norm(DEEPNORM_ALPHA * x_rows + mix, g1_ref[...], b1_ref[...])
        for t in range(LANE_TILES):
            hbuf_ref[slot, t, r0:r0 + tail_rows, :] = h1[:, t * LANES:(t + 1) * LANES]
        h_hi = h1.astype(jnp.bfloat16)
        h_lo = (h1 - h_hi.astype(jnp.float32)).astype(jnp.bfloat16)
        both = jnp.dot(h_hi, wr_ref[...], preferred_element_type=jnp.float32)
        logits = both[:, :ROUTER_LANES] + both[:, ROUTER_LANES:] + br_ref[...]
        logits = logits + jnp.dot(h_lo, wr_ref[:, :ROUTER_LANES],
                                  preferred_element_type=jnp.float32)
        logit_ref[:, r0:r0 + tail_rows] = logits.T[:ROUTER_ROWS, :]

    n_tails = M1 // tail_rows
    mix = out_proj(0)
    for k in range(n_tails):
        next_mix = out_proj(k + 1) if k + 1 < n_tails else None
        finish(k, mix)
        mix = next_mix

    for cp in h1_copies(c, slot):
        cp.start()

    @pl.when(c == n_steps - 1)
    def _():
        if n_steps >= 2:
            for cp in h1_copies(c - 1, 1 - slot):
                cp.wait()
        for cp in h1_copies(c, slot):
            cp.wait()


def _const_spec(shape, pipeline_mode=None):
    if pipeline_mode is None:
        return pl.BlockSpec(shape, lambda c: (0,) * len(shape))
    return pl.BlockSpec(shape, lambda c: (0,) * len(shape), pipeline_mode=pipeline_mode)


def _mixer_call(x, w_in, cw, cb, wa, ba, wx, bx, lam, scw, w_out, g1, b1, wr, br, wg, wu, wd):
    n_chunks = SEQ // TS
    epc = N_EXPERTS // n_chunks
    assert epc * n_chunks == N_EXPERTS
    in_specs = [
        pl.BlockSpec(memory_space=pl.ANY),
        _const_spec((D_MODEL, D_IN_PROJ), pl.Buffered(1)),
        _const_spec((RG_CONV, D_RG)), _const_spec((1, D_RG)),
        _const_spec((RG_HEADS, RG_HEAD_DIM, RG_HEAD_DIM)), _const_spec((1, D_RG)),
        _const_spec((RG_HEADS, RG_HEAD_DIM, RG_HEAD_DIM)), _const_spec((1, D_RG)),
        _const_spec((1, D_RG)),
        _const_spec((SC_CONV, D_SC)),
        _const_spec((D_MODEL, D_MODEL), pl.Buffered(1)),
        _const_spec((1, D_MODEL)), _const_spec((1, D_MODEL)),
        _const_spec((D_MODEL, 2 * ROUTER_LANES)), _const_spec((1, ROUTER_LANES)),
        pl.BlockSpec((epc, D_MODEL, D_EXPERT), lambda c: (c, 0, 0)),
        pl.BlockSpec((epc, D_MODEL, D_EXPERT), lambda c: (c, 0, 0)),
        pl.BlockSpec((epc, D_EXPERT, D_MODEL), lambda c: (c, 0, 0)),
    ]
    out_specs = [
        pl.BlockSpec(memory_space=pl.ANY),
        pl.BlockSpec((ROUTER_ROWS, M1), lambda c: (0, c)),
        pl.BlockSpec((epc, D_MODEL, 2 * D_EXPERT), lambda c: (c, 0, 0)),
        pl.BlockSpec((epc, D_EXPERT, D_MODEL), lambda c: (c, 0, 0)),
    ]
    return pl.pallas_call(
        _mixer_kernel,
        grid=(n_chunks,),
        in_specs=in_specs,
        out_specs=out_specs,
        out_shape=[
            jax.ShapeDtypeStruct((N_TOKENS, LANE_TILES, LANES), jnp.float32),
            jax.ShapeDtypeStruct((ROUTER_ROWS, N_TOKENS), jnp.float32),
            jax.ShapeDtypeStruct((N_EXPERTS, D_MODEL, 2 * D_EXPERT), jnp.bfloat16),
            jax.ShapeDtypeStruct((N_EXPERTS, D_EXPERT, D_MODEL), jnp.bfloat16),
        ],
        scratch_shapes=[
            pltpu.VMEM((2, TS, BATCH, D_MODEL), jnp.float32),
            pltpu.SemaphoreType.DMA((2,)),
            pltpu.VMEM((2, LANE_TILES, M1, LANES), jnp.float32),
            pltpu.SemaphoreType.DMA((2,)),
            pltpu.VMEM((M1 + (RG_CONV - 1) * BATCH, D_RG), jnp.float32),
            pltpu.VMEM((M1 + (SC_CONV - 1) * BATCH, D_SC), jnp.float32),
            pltpu.VMEM((M1, D_RG), jnp.float32),
            pltpu.VMEM((M1, D_RG), jnp.float32),
            pltpu.VMEM((BATCH, D_RG), jnp.float32),
            pltpu.VMEM((D_RG, D_RG), jnp.bfloat16),
            pltpu.VMEM((D_RG, D_RG), jnp.bfloat16),
            pltpu.VMEM((D_MODEL, D_IN_PROJ), jnp.bfloat16),
            pltpu.VMEM((D_MODEL, D_MODEL), jnp.bfloat16),
        ],
        compiler_params=pltpu.CompilerParams(
            dimension_semantics=("arbitrary",), vmem_limit_bytes=VMEM_LIMIT),
        name="mixer_ln_router",
    )(x, w_in, cw, cb, wa, ba, wx, bx, lam, scw, w_out, g1, b1, wr, br, wg, wu, wd)


def _expert_kernel(blk_ref, grp_ref, ea_ref, eb_ref, rows_ref,
                   xs_hbm, wt_ref, wgu_ref, wd_ref, g2_ref, b2_ref, ys_hbm,
                   xbuf_ref, xsem, obuf_ref, osem):
    s = pl.program_id(0)
    slot = s % 2
    tile_rows = [rows_ref[s * SUB + j] for j in range(SUB)]
    block_rows = SUB * TM

    def fetches(step, slot_):
        return [pltpu.make_async_copy(xs_hbm.at[pl.ds(step * block_rows, block_rows), j, :],
                                      xbuf_ref.at[slot_, j], xsem.at[slot_])
                for j in range(LANE_TILES)]

    def write_backs(step, slot_):
        return [pltpu.make_async_copy(obuf_ref.at[slot_, j],
                                      ys_hbm.at[pl.ds(step * block_rows, block_rows), j, :],
                                      osem.at[slot_])
                for j in range(LANE_TILES)]

    def did_work(step):
        return blk_ref[jnp.clip(step, 0, N_STEPS3 - 1)] == step

    @pl.when(jnp.logical_and(s == 0, did_work(0)))
    def _():
        for cp in fetches(0, 0):
            cp.start()

    @pl.when(did_work(s + 1))
    def _():
        for cp in fetches(s + 1, 1 - slot):
            cp.start()

    @pl.when(did_work(s - 2))
    def _():
        for cp in write_backs(s - 2, slot):
            cp.wait()

    @pl.when(did_work(s))
    def _():
        for cp in fetches(s, slot):
            cp.wait()
        for j in range(SUB):
            tile = s * SUB + j
            first_row = j * TM
            in_use = lax.broadcasted_iota(jnp.int32, (TM, 1), 0) < tile_rows[j]
            x = jnp.concatenate([xbuf_ref[slot, t, first_row:first_row + TM, :]
                                 for t in range(LANE_TILES)], axis=1)
            x = jnp.where(in_use, x, 0.0)
            xb = x.astype(jnp.bfloat16)
            wt = jnp.where(in_use, wt_ref[first_row:first_row + TM, :], 0.0)
            y = jnp.zeros((TM, D_MODEL), jnp.float32)
            for e_ref, col in ((ea_ref, 0), (eb_ref, 1)):
                e = e_ref[tile]
                hgu = jnp.dot(xb, wgu_ref[e], preferred_element_type=jnp.float32)
                hid = jax.nn.silu(hgu[:, :D_EXPERT]) * hgu[:, D_EXPERT:] * wt[:, col:col + 1]
                y = y + jnp.dot(hid.astype(jnp.bfloat16), wd_ref[e],
                                preferred_element_type=jnp.float32)
            out = _layer_norm(DEEPNORM_ALPHA * x + y, g2_ref[...], b2_ref[...])
            for t in range(LANE_TILES):
                obuf_ref[slot, t, first_row:first_row + TM, :] = out[:, t * LANES:(t + 1) * LANES]
        for cp in write_backs(s, slot):
            cp.start()

    @pl.when(s == N_STEPS3 - 1)
    def _():
        @pl.when(did_work(s - 1))
        def _():
            for cp in write_backs(s - 1, 1 - slot):
                cp.wait()

        @pl.when(did_work(s))
        def _():
            for cp in write_backs(s, slot):
                cp.wait()


def _expert_call(step_block, step_group, tile_ea, tile_eb, tile_rows, xs, wts, wgu, wd, g2, b2):
    rows = SUB * TM
    grid_spec = pltpu.PrefetchScalarGridSpec(
        num_scalar_prefetch=5,
        grid=(N_STEPS3,),
        in_specs=[
            pl.BlockSpec(memory_space=pl.ANY),
            pl.BlockSpec((rows, LANES), lambda s, blk, grp, ea, eb, va: (blk[s], 0)),
            pl.BlockSpec((None, EXPERTS_PER_GROUP, D_MODEL, 2 * D_EXPERT),
                         lambda s, blk, grp, ea, eb, va: (grp[s], 0, 0, 0)),
            pl.BlockSpec((None, EXPERTS_PER_GROUP, D_EXPERT, D_MODEL),
                         lambda s, blk, grp, ea, eb, va: (grp[s], 0, 0, 0)),
            pl.BlockSpec((1, D_MODEL), lambda s, blk, grp, ea, eb, va: (0, 0)),
            pl.BlockSpec((1, D_MODEL), lambda s, blk, grp, ea, eb, va: (0, 0)),
        ],
        out_specs=pl.BlockSpec(memory_space=pl.ANY),
        scratch_shapes=[
            pltpu.VMEM((2, LANE_TILES, rows, LANES), jnp.float32),
            pltpu.SemaphoreType.DMA((2,)),
            pltpu.VMEM((2, LANE_TILES, rows, LANES), jnp.float32),
            pltpu.SemaphoreType.DMA((2,)),
        ],
    )
    return pl.pallas_call(
        _expert_kernel,
        grid_spec=grid_spec,
        out_shape=jax.ShapeDtypeStruct((P_ROWS, LANE_TILES, LANES), jnp.float32),
        compiler_params=pltpu.CompilerParams(
            dimension_semantics=("arbitrary",), vmem_limit_bytes=VMEM_LIMIT),
        name="experts_ln",
    )(step_block, step_group, tile_ea, tile_eb, tile_rows, xs, wts, wgu, wd, g2, b2)


SC_CORES = 2
SC_SUBCORES = 16
SC_WORKERS = SC_CORES * SC_SUBCORES
COMBINE_CHUNK = 32


def _sc_gather_rows(table, idx, chunk, out_row_shape=None):
    n_out, = idx.shape
    row_shape = table.shape[1:]
    out_row_shape = row_shape if out_row_shape is None else out_row_shape
    per_w = n_out // SC_WORKERS
    n_chunks = per_w // chunk
    assert per_w * SC_WORKERS == n_out and n_chunks * chunk == per_w
    assert n_chunks % 2 == 0 and chunk % 8 == 0
    mesh = plsc.VectorSubcoreMesh(core_axis_name="c", subcore_axis_name="s")

    @functools.partial(
        pl.kernel, mesh=mesh,
        out_type=jax.ShapeDtypeStruct((n_out,) + out_row_shape, table.dtype),
        scratch_types=[
            pltpu.VMEM((per_w,), jnp.int32),
            pltpu.VMEM((2, chunk) + row_shape, table.dtype),
            pltpu.SemaphoreType.DMA((2,)),
            pltpu.SemaphoreType.DMA((2,)),
        ],
    )
    def gather_kernel(table_hbm, idx_hbm, out_hbm, idx_v, buf, gsem, wsem):
        wid = lax.axis_index("s") * SC_CORES + lax.axis_index("c")
        base = wid * per_w
        pltpu.sync_copy(idx_hbm.at[pl.ds(base, per_w)], idx_v)

        def gather(j, slot):
            rows = idx_v.at[pl.ds(j * chunk, chunk)]
            return pltpu.make_async_copy(table_hbm.at[rows], buf.at[slot], gsem.at[slot])

        def write(j, slot):
            dst = out_hbm.at[pl.ds(base + j * chunk, chunk)]
            src = buf.at[slot].reshape((chunk,) + out_row_shape)
            return pltpu.make_async_copy(src, dst, wsem.at[slot])

        gather(0, 0).start()

        @pl.loop(0, n_chunks, step=2)
        def _(j0):
            for slot in range(2):
                j = j0 + slot
                gather(j, slot).wait()

                @pl.when(j >= 1)
                def _():
                    write(j - 1, 1 - slot).wait()

                @pl.when(j + 1 < n_chunks)
                def _():
                    gather(j + 1, 1 - slot).start()

                write(j, slot).start()

        write(n_chunks - 1, (n_chunks - 1) % 2).wait()

    return gather_kernel(table, idx)


SC_LANES = 16


def _sc_dispatch(rows, meta, cls, rank, class_start):
    n_rows = rows.shape[0]
    chunk = COMBINE_CHUNK
    per_w = n_rows // SC_WORKERS
    n_chunks = per_w // chunk
    assert per_w * SC_WORKERS == n_rows and n_chunks * chunk == per_w
    assert n_chunks % 2 == 0 and chunk % SC_LANES == 0
    mesh = plsc.VectorSubcoreMesh(core_axis_name="c", subcore_axis_name="s")

    @functools.partial(
        pl.kernel, mesh=mesh,
        out_type=[
            jax.ShapeDtypeStruct((P_ROWS,) + rows.shape[1:], rows.dtype),
            jax.ShapeDtypeStruct((P_ROWS,) + meta.shape[1:], meta.dtype),
            jax.ShapeDtypeStruct((n_rows,), jnp.int32),
        ],
        scratch_types=[
            pltpu.VMEM((per_w,), jnp.int32),
            pltpu.VMEM((per_w,), jnp.int32),
            pltpu.VMEM((LANES,), jnp.int32),
            pltpu.VMEM((per_w,), jnp.int32),
            pltpu.VMEM((n_chunks, chunk), jnp.int32),
            pltpu.VMEM((2, chunk) + rows.shape[1:], rows.dtype),
            pltpu.VMEM((2, chunk) + meta.shape[1:], meta.dtype),
            pltpu.SemaphoreType.DMA((2,)),
            pltpu.SemaphoreType.DMA((2,)),
            pltpu.SemaphoreType.DMA((2,)),
            pltpu.SemaphoreType.DMA((2,)),
        ],
        compiler_params=pltpu.CompilerParams(needs_layout_passes=False),
    )
    def dispatch_kernel(rows_hbm, meta_hbm, cls_hbm, rank_hbm, start_hbm,
                        xs_hbm, ws_hbm, pos_hbm,
                        cls_v, rank_v, start_v, pos_flat, pos_v, rbuf, mbuf,
                        rsem, msem, xsem, wsem):
        wid = lax.axis_index("s") * SC_CORES + lax.axis_index("c")
        base = wid * per_w
        pltpu.sync_copy(cls_hbm.at[pl.ds(base, per_w)], cls_v)
        pltpu.sync_copy(rank_hbm.at[pl.ds(base, per_w)], rank_v)
        pltpu.sync_copy(start_hbm, start_v)

        @pl.loop(0, n_chunks)
        def _(j):
            for k in range(chunk // SC_LANES):
                off = j * chunk + k * SC_LANES
                c = cls_v[pl.ds(off, SC_LANES)]
                pos = plsc.load_gather(start_v, [c]) + rank_v[pl.ds(off, SC_LANES)]
                pos_flat[pl.ds(off, SC_LANES)] = pos
                pos_v[j, pl.ds(k * SC_LANES, SC_LANES)] = pos

        pltpu.sync_copy(pos_flat, pos_hbm.at[pl.ds(base, per_w)])

        def read_rows(j, slot):
            src = rows_hbm.at[pl.ds(base + j * chunk, chunk)]
            return pltpu.make_async_copy(src, rbuf.at[slot], rsem.at[slot])

        def read_meta(j, slot):
            src = meta_hbm.at[pl.ds(base + j * chunk, chunk)]
            return pltpu.make_async_copy(src, mbuf.at[slot], msem.at[slot])

        def put_rows(j, slot):
            return pltpu.make_async_copy(rbuf.at[slot], xs_hbm.at[pos_v.at[j]], xsem.at[slot])

        def put_meta(j, slot):
            return pltpu.make_async_copy(mbuf.at[slot], ws_hbm.at[pos_v.at[j]], wsem.at[slot])

        read_rows(0, 0).start()
        read_meta(0, 0).start()

        @pl.loop(0, n_chunks, step=2)
        def _(j0):
            for slot in range(2):
                j = j0 + slot
                read_rows(j, slot).wait()
                read_meta(j, slot).wait()

                @pl.when(j >= 1)
                def _():
                    put_rows(j - 1, 1 - slot).wait()
                    put_meta(j - 1, 1 - slot).wait()

                @pl.when(j + 1 < n_chunks)
                def _():
                    read_rows(j + 1, 1 - slot).start()
                    read_meta(j + 1, 1 - slot).start()

                put_rows(j, slot).start()
                put_meta(j, slot).start()

        put_rows(n_chunks - 1, (n_chunks - 1) % 2).wait()
        put_meta(n_chunks - 1, (n_chunks - 1) % 2).wait()

    return dispatch_kernel(rows, meta, cls, rank, class_start)


def _pair_tables():
    pair_of = np.zeros((EXPERTS_PER_GROUP, EXPERTS_PER_GROUP), np.int32)
    lo = np.zeros((N_PAIRS,), np.int32)
    hi = np.zeros((N_PAIRS,), np.int32)
    p = 0
    for a in range(EXPERTS_PER_GROUP):
        for b in range(a + 1, EXPERTS_PER_GROUP):
            pair_of[a, b] = pair_of[b, a] = p
            lo[p], hi[p] = a, b
            p += 1
    return pair_of, lo, hi


def _route_kernel(lt_ref, info_ref, meta_ref, counts_ref, run_ref):
    step = pl.program_id(0)

    @pl.when(step == 0)
    def _():
        run_ref[...] = jnp.zeros_like(run_ref)

    f32 = jnp.float32
    sub8 = lax.broadcasted_iota(jnp.int32, (8, LANES), 0).astype(f32)
    row_id = lax.broadcasted_iota(jnp.int32, (LANES, LANES), 0)
    col_id = lax.broadcasted_iota(jnp.int32, (LANES, LANES), 1)
    prefix_mat = (row_id <= col_id).astype(jnp.bfloat16)
    ones_mat = jnp.ones((LANES, LANES), jnp.bfloat16)
    neg_inf = f32(-jnp.inf)

    def first_index_of_max(v):
        m = jnp.max(v, axis=0, keepdims=True)
        idx = jnp.min(jnp.where(v == m, sub8, f32(8)), axis=0, keepdims=True)
        return m, idx

    def lane_tile(k, run):
        lanes = pl.ds(pl.multiple_of(k * LANES, LANES), LANES)
        g = jnp.where(sub8 < N_GROUPS, lt_ref[0:GROUP_ROWS, lanes], neg_inf)
        g_max, g_idx = first_index_of_max(g)
        g_top_p = 1.0 / jnp.sum(jnp.exp(g - g_max), axis=0, keepdims=True)

        e_sel = lt_ref[GROUP_ROWS:GROUP_ROWS + EXPERTS_PER_GROUP, lanes]
        for grp in range(1, N_GROUPS):
            lo = GROUP_ROWS + grp * EXPERTS_PER_GROUP
            e_sel = jnp.where(g_idx == grp, lt_ref[lo:lo + EXPERTS_PER_GROUP, lanes], e_sel)
        m1, i1 = first_index_of_max(e_sel)
        rest = jnp.where(sub8 == i1, neg_inf, e_sel)
        m2 = jnp.max(rest, axis=0, keepdims=True)
        i2 = jnp.min(jnp.where((rest == m2) & (sub8 != i1), sub8, f32(8)), axis=0, keepdims=True)

        e = jnp.exp(m2 - m1)
        w1 = g_top_p / (1.0 + e)
        w2 = g_top_p * e / (1.0 + e)
        first_is_lo = i1 < i2
        w_lo = jnp.where(first_is_lo, w1, w2)
        w_hi = jnp.where(first_is_lo, w2, w1)
        lo_e = jnp.minimum(i1, i2)
        hi_e = jnp.maximum(i1, i2)
        pair = lo_e * (2 * EXPERTS_PER_GROUP - 1 - lo_e) * 0.5 + hi_e - lo_e - 1.0
        cls = (g_idx * N_PAIRS + pair).astype(jnp.int32)

        onehot = (row_id == cls).astype(jnp.bfloat16)
        csum = jnp.dot(onehot, prefix_mat, preferred_element_type=f32)
        rank = jnp.sum(onehot.astype(f32) * (csum + run), axis=0, keepdims=True) - 1.0
        run = run + jnp.dot(onehot, ones_mat, preferred_element_type=f32)

        info_ref[:, lanes] = jnp.where(sub8 == 0, cls, jnp.where(sub8 == 1, rank.astype(jnp.int32), 0))
        meta8 = jnp.where(sub8 == 0, w_lo, jnp.where(sub8 == 1, w_hi, 0.0))
        meta_t = jnp.concatenate([meta8, jnp.zeros((LANES - 8, LANES), f32)], axis=0)
        meta_ref[lanes, :] = meta_t.T
        return run

    run = lax.fori_loop(0, ROUTE_BLOCK // LANES, lane_tile, run_ref[...], unroll=ROUTE_UNROLL)
    run_ref[...] = run
    counts_ref[...] = run


def _route_call(lt):
    return pl.pallas_call(
        _route_kernel,
        grid=(N_TOKENS // ROUTE_BLOCK,),
        in_specs=[pl.BlockSpec((ROUTER_ROWS, ROUTE_BLOCK), lambda i: (0, i))],
        out_specs=[
            pl.BlockSpec((8, ROUTE_BLOCK), lambda i: (0, i)),
            pl.BlockSpec((ROUTE_BLOCK, LANES), lambda i: (i, 0)),
            pl.BlockSpec((LANES, LANES), lambda i: (0, 0)),
        ],
        out_shape=[
            jax.ShapeDtypeStruct((8, N_TOKENS), jnp.int32),
            jax.ShapeDtypeStruct((N_TOKENS, LANES), jnp.float32),
            jax.ShapeDtypeStruct((LANES, LANES), jnp.float32),
        ],
        scratch_shapes=[pltpu.VMEM((LANES, LANES), jnp.float32)],
        compiler_params=pltpu.CompilerParams(dimension_semantics=("arbitrary",)),
        name="route_rank",
    )(lt)


def _dispatch_plan(counts):
    _, pair_lo, pair_hi = _pair_tables()
    tiles_c = (counts + TM - 1) // TM
    tiles_g = tiles_c.reshape(N_GROUPS, N_PAIRS).sum(axis=1)
    tiles_g_pad = (tiles_g + SUB - 1) // SUB * SUB
    g_start = jnp.cumsum(tiles_g_pad) - tiles_g_pad
    tc = tiles_c.reshape(N_GROUPS, N_PAIRS)
    c_start = (g_start[:, None] + jnp.cumsum(tc, axis=1) - tc).reshape(N_CLASSES)
    class_start = jnp.zeros((LANES,), jnp.int32).at[:N_CLASSES].set(c_start * TM)

    tile_ids = jnp.arange(N_TILES, dtype=jnp.int32)
    c_end = c_start + tiles_c
    owner = jnp.sum((tile_ids[:, None] >= c_end[None, :]).astype(jnp.int32), axis=1)
    owner = jnp.minimum(owner, N_CLASSES - 1)
    in_class = (tile_ids >= c_start[owner]) & (tile_ids < c_end[owner])
    rows_left = counts[owner] - (tile_ids - c_start[owner]) * TM
    tile_rows = jnp.where(in_class, jnp.minimum(rows_left, TM), 0).astype(jnp.int32)
    pair = owner % N_PAIRS
    tile_ea = jnp.asarray(pair_lo)[pair]
    tile_eb = jnp.asarray(pair_hi)[pair]
    g_end = g_start + tiles_g_pad
    step_first = jnp.arange(N_STEPS3, dtype=jnp.int32) * SUB
    step_group = jnp.sum((step_first[:, None] >= g_end[None, :]).astype(jnp.int32), axis=1)
    step_group = jnp.minimum(step_group, N_GROUPS - 1)
    last_step = jnp.maximum(g_end[-1] // SUB - 1, 0)
    step_block = jnp.minimum(jnp.arange(N_STEPS3, dtype=jnp.int32), last_step)
    return (class_start, tile_ea, tile_eb, tile_rows, step_group.astype(jnp.int32),
            step_block.astype(jnp.int32))


def kernel(x, w_in, rg_conv_w, rg_conv_b, rg_gate_a_w, rg_gate_a_b, rg_gate_x_w, rg_gate_x_b,
           rg_lambda, sc_conv_w, w_out, ln1_g, ln1_b, router_group_w, router_group_b,
           router_expert_w, router_expert_b, exp_w_gate, exp_w_up, exp_w_down, ln2_g, ln2_b):
    bf16 = jnp.bfloat16
    row = lambda v: v.reshape(1, -1)
    pad_g = GROUP_ROWS - N_GROUPS
    pad_e = ROUTER_LANES - ROUTER_ROWS
    wr = jnp.concatenate([router_group_w, jnp.zeros((D_MODEL, pad_g), jnp.float32),
                          router_expert_w, jnp.zeros((D_MODEL, pad_e), jnp.float32)], axis=1)
    br = jnp.concatenate([router_group_b, jnp.zeros((pad_g,), jnp.float32),
                          router_expert_b, jnp.zeros((pad_e,), jnp.float32)]).reshape(1, -1)
    wr_hi = wr.astype(bf16)
    wr_lo = (wr - wr_hi.astype(jnp.float32)).astype(bf16)

    h1, lt, wgu, wdb = _mixer_call(
        x, w_in, rg_conv_w, row(rg_conv_b),
        rg_gate_a_w, row(rg_gate_a_b),
        rg_gate_x_w, row(rg_gate_x_b),
        row(rg_lambda), sc_conv_w, w_out, row(ln1_g), row(ln1_b),
        jnp.concatenate([wr_hi, wr_lo], axis=1), br, exp_w_gate, exp_w_up, exp_w_down)

    info, meta, counts = _route_call(lt)
    class_start, tile_ea, tile_eb, tile_rows, step_group, step_block = _dispatch_plan(
        counts[:N_CLASSES, 0].astype(jnp.int32))

    xs, wts, pos = _sc_dispatch(h1, meta, info[0], info[1], class_start)

    grp_shape = (N_GROUPS, EXPERTS_PER_GROUP)
    ys = _expert_call(
        step_block, step_group, tile_ea, tile_eb, tile_rows, xs, wts,
        wgu.reshape(grp_shape + (D_MODEL, 2 * D_EXPERT)),
        wdb.reshape(grp_shape + (D_EXPERT, D_MODEL)),
        row(ln2_g), row(ln2_b))

    out_rows = pos.reshape(SEQ, BATCH).T.reshape(-1)
    out = _sc_gather_rows(ys, out_rows, COMBINE_CHUNK, (D_MODEL,))
    return out.reshape(BATCH, SEQ, D_MODEL)
```

```python
import functools

import jax
import jax.numpy as jnp
import numpy as np
from jax import lax
from jax.experimental import pallas as pl
from jax.experimental.pallas import tpu as pltpu
from jax.experimental.pallas import tpu_sc as plsc

D_MODEL = 1024
BATCH = 16
SEQ = 2048
D_RG = 512
D_SC = 512
RG_HEADS = 8
RG_HEAD_DIM = D_RG // RG_HEADS
RG_CONV = 4
RG_C = 8.0
SC_CONV = 3
D_IN_PROJ = 2 * D_RG + 3 * D_SC
N_GROUPS = 4
EXPERTS_PER_GROUP = 8
N_EXPERTS = N_GROUPS * EXPERTS_PER_GROUP
D_EXPERT = D_MODEL // 4
LN_EPS = 1e-5
DEEPNORM_ALPHA = 2.0 ** 0.25

N_TOKENS = BATCH * SEQ
LANES = 128
LANE_TILES = D_MODEL // LANES
MXU_WIDTH = 256
ROUTER_LANES = LANES
GROUP_ROWS = 8
ROUTER_ROWS = GROUP_ROWS + N_EXPERTS

TS = 64
M1 = TS * BATCH
DOT_ROWS = 256

ROUTE_BLOCK = 4096
ROUTE_UNROLL = 8

N_PAIRS = EXPERTS_PER_GROUP * (EXPERTS_PER_GROUP - 1) // 2
N_CLASSES = N_GROUPS * N_PAIRS
BF16_ROWS = 16
TM = -(-int(N_TOKENS / N_CLASSES * 1.09 / 2) // BF16_ROWS) * BF16_ROWS
SUB = 8
N_TILES = -(-(N_TOKENS // TM + N_CLASSES + N_GROUPS * (SUB - 1)) // SUB) * SUB
N_STEPS3 = N_TILES // SUB
P_ROWS = N_TILES * TM

VMEM_LIMIT = 58 * 1024 * 1024


def _start_all(copies):
    for i, cp in enumerate(copies):
        cp.start(priority=i % 2)


def _layer_norm(z, g, b):
    mu = jnp.mean(z, axis=-1, keepdims=True)
    zc = z - mu
    var = jnp.mean(zc * zc, axis=-1, keepdims=True)
    return zc * lax.rsqrt(var + LN_EPS) * g + b


def _mixer_kernel(x_ref, w_in_f32_ref, cw_ref, cb_ref, wa_heads_ref, ba_ref, wx_heads_ref, bx_ref,
                  lam_ref, scw_ref, w_out_f32_ref, g1_ref, b1_ref, wr_ref, br_ref,
                  wg_ref, wu_ref, wd_ref,
                  h1_hbm, logit_ref, wgu_ref, wdb_ref,
                  xbuf_ref, xsem, hbuf_ref, hsem, cbuf_ref, sbuf_ref, a_ref, u_ref, hstate_ref,
                  wa_ref, wx_ref, w_in_ref, w_out_ref):
    c = pl.program_id(0)
    n_steps = SEQ // TS

    def h1_copies(step, slot_):
        return [pltpu.make_async_copy(hbuf_ref.at[slot_, j],
                                      h1_hbm.at[pl.ds(step * M1, M1), j, :], hsem.at[slot_])
                for j in range(LANE_TILES)]

    @pl.when(c >= 2)
    def _():
        for cp in h1_copies(c - 2, c % 2):
            cp.wait()

    wgu_ref[:, :, :D_EXPERT] = wg_ref[...].astype(jnp.bfloat16)
    wgu_ref[:, :, D_EXPERT:] = wu_ref[...].astype(jnp.bfloat16)
    wdb_ref[...] = wd_ref[...].astype(jnp.bfloat16)
    slot = c % 2
    rg_pad = (RG_CONV - 1) * BATCH
    sc_pad = (SC_CONV - 1) * BATCH

    def x_copies(chunk, slot_):
        return [pltpu.make_async_copy(x_ref.at[b, pl.ds(chunk * TS, TS), :],
                                      xbuf_ref.at[slot_, :, b, :], xsem.at[slot_])
                for b in range(BATCH)]

    @pl.when(c == 0)
    def _():
        _start_all(x_copies(0, 0))
        cbuf_ref[0:rg_pad, :] = jnp.zeros((rg_pad, D_RG), jnp.float32)
        sbuf_ref[0:sc_pad, :] = jnp.zeros((sc_pad, D_SC), jnp.float32)
        hstate_ref[...] = jnp.zeros_like(hstate_ref)
        w_in_ref[...] = w_in_f32_ref[...].astype(jnp.bfloat16)
        w_out_ref[...] = w_out_f32_ref[...].astype(jnp.bfloat16)
        for heads_ref, bd_ref in ((wa_heads_ref, wa_ref), (wx_heads_ref, wx_ref)):
            bd_ref[...] = jnp.zeros_like(bd_ref)
            for hd in range(RG_HEADS):
                lo = hd * RG_HEAD_DIM
                bd_ref[lo:lo + RG_HEAD_DIM, lo:lo + RG_HEAD_DIM] = (
                    heads_ref[hd].astype(jnp.bfloat16))

    @pl.when(c + 1 < SEQ // TS)
    def _():
        _start_all(x_copies(c + 1, 1 - slot))

    for cp in x_copies(c, slot):
        cp.wait()

    xt_ref = xbuf_ref.at[slot]

    def gate(xcb, w_ref, b_ref):
        parts = [jnp.dot(xcb[:, lo:lo + MXU_WIDTH], w_ref[lo:lo + MXU_WIDTH, lo:lo + MXU_WIDTH],
                         preferred_element_type=jnp.float32)
                 for lo in range(0, D_RG, MXU_WIDTH)]
        return 0.5 + 0.5 * jnp.tanh(0.5 * (jnp.concatenate(parts, axis=1) + b_ref[...]))

    soft_lam = jax.nn.softplus(-lam_ref[...])
    gelu_chunks, y_sc_chunks = [], []
    for r0 in range(0, M1, DOT_ROWS):
        t0 = r0 // BATCH
        xk = xt_ref[t0:t0 + DOT_ROWS // BATCH].reshape(DOT_ROWS, D_MODEL).astype(jnp.bfloat16)

        def proj(lo, width):
            return jnp.dot(xk, w_in_ref[:, lo:lo + width], preferred_element_type=jnp.float32)

        cbuf_ref[rg_pad + r0:rg_pad + r0 + DOT_ROWS, :] = proj(0, D_RG)
        rg_gate = proj(D_RG, D_RG)
        xc = cb_ref[...] + cw_ref[0:1, :] * cbuf_ref[r0:r0 + DOT_ROWS, :]
        for k in range(1, RG_CONV):
            xc = xc + cw_ref[k:k + 1, :] * cbuf_ref[r0 + k * BATCH:r0 + k * BATCH + DOT_ROWS, :]
        sc_b = proj(2 * D_RG, D_SC)
        xcb = xc.astype(jnp.bfloat16)
        r = gate(xcb, wa_ref, ba_ref)
        i = gate(xcb, wx_ref, bx_ref)
        sc_c = proj(2 * D_RG + D_SC, D_SC)
        log_a = (-RG_C) * r * soft_lam
        a_ref[r0:r0 + DOT_ROWS, :] = jnp.exp(log_a)
        th = jnp.tanh(log_a)
        u_ref[r0:r0 + DOT_ROWS, :] = jnp.sqrt(-2.0 * th / (1.0 - th)) * (i * xc)
        gelu_chunks.append(jax.nn.gelu(rg_gate))

        sbuf_ref[sc_pad + r0:sc_pad + r0 + DOT_ROWS, :] = sc_c * proj(2 * D_RG + 2 * D_SC, D_SC)
        conv = scw_ref[0:1, :] * sbuf_ref[r0:r0 + DOT_ROWS, :]
        for k in range(1, SC_CONV):
            conv = conv + scw_ref[k:k + 1, :] * sbuf_ref[r0 + k * BATCH:r0 + k * BATCH + DOT_ROWS, :]
        y_sc_chunks.append((sc_b * conv).astype(jnp.bfloat16))

    cbuf_ref[0:rg_pad, :] = cbuf_ref[M1:M1 + rg_pad, :]
    sbuf_ref[0:sc_pad, :] = sbuf_ref[M1:M1 + sc_pad, :]

    h = hstate_ref[...]
    for t in range(TS):
        h = a_ref[t * BATCH:(t + 1) * BATCH, :] * h + u_ref[t * BATCH:(t + 1) * BATCH, :]
        u_ref[t * BATCH:(t + 1) * BATCH, :] = h
    hstate_ref[...] = h

    tail_rows = M1 // 2
    per_tail = tail_rows // DOT_ROWS

    def out_proj(k):
        r0 = k * tail_rows
        gelu_k = jnp.concatenate(gelu_chunks[k * per_tail:(k + 1) * per_tail], axis=0)
        y_sc_k = jnp.concatenate(y_sc_chunks[k * per_tail:(k + 1) * per_tail], axis=0)
        y_rg = (u_ref[r0:r0 + tail_rows, :] * gelu_k).astype(jnp.bfloat16)
        mix = jnp.dot(y_rg, w_out_ref[0:D_RG, :], preferred_element_type=jnp.float32)
        return mix + jnp.dot(y_sc_k, w_out_ref[D_RG:, :], preferred_element_type=jnp.float32)

    def finish(k, mix):
        r0 = k * tail_rows
        t0 = r0 // BATCH
        x_rows = xt_ref[t0:t0 + tail_rows // BATCH].reshape(tail_rows, D_MODEL)
        h1 = _layer_norm(DEEPNORM_ALPHA * x_rows + mix, g1_ref[...], b1_ref[...])
        for t in range(LANE_TILES):
            hbuf_ref[slot, t, r0:r0 + tail_rows, :] = h1[:, t * LANES:(t + 1) * LANES]
        h_hi = h1.astype(jnp.bfloat16)
        h_lo = (h1 - h_hi.astype(jnp.float32)).astype(jnp.bfloat16)
        both = jnp.dot(h_hi, wr_ref[...], preferred_element_type=jnp.float32)
        logits = both[:, :ROUTER_LANES] + both[:, ROUTER_LANES:] + br_ref[...]
        logits = logits + jnp.dot(h_lo, wr_ref[:, :ROUTER_LANES],
                                  preferred_element_type=jnp.float32)
        logit_ref[:, r0:r0 + tail_rows] = logits.T[:ROUTER_ROWS, :]

    n_tails = M1 // tail_rows
    mix = out_proj(0)
    for k in range(n_tails):
        next_mix = out_proj(k + 1) if k + 1 < n_tails else None
        finish(k, mix)
        mix = next_mix

    _start_all(h1_copies(c, slot))

    @pl.when(c == n_steps - 1)
    def _():
        if n_steps >= 2:
            for cp in h1_copies(c - 1, 1 - slot):
                cp.wait()
        for cp in h1_copies(c, slot):
            cp.wait()


def _const_spec(shape, pipeline_mode=None):
    if pipeline_mode is None:
        return pl.BlockSpec(shape, lambda c: (0,) * len(shape))
    return pl.BlockSpec(shape, lambda c: (0,) * len(shape), pipeline_mode=pipeline_mode)


def _mixer_call(x, w_in, cw, cb, wa, ba, wx, bx, lam, scw, w_out, g1, b1, wr, br, wg, wu, wd):
    n_chunks = SEQ // TS
    epc = N_EXPERTS // n_chunks
    assert epc * n_chunks == N_EXPERTS
    in_specs = [
        pl.BlockSpec(memory_space=pl.ANY),
        _const_spec((D_MODEL, D_IN_PROJ), pl.Buffered(1)),
        _const_spec((RG_CONV, D_RG)), _const_spec((1, D_RG)),
        _const_spec((RG_HEADS, RG_HEAD_DIM, RG_HEAD_DIM)), _const_spec((1, D_RG)),
        _const_spec((RG_HEADS, RG_HEAD_DIM, RG_HEAD_DIM)), _const_spec((1, D_RG)),
        _const_spec((1, D_RG)),
        _const_spec((SC_CONV, D_SC)),
        _const_spec((D_MODEL, D_MODEL), pl.Buffered(1)),
        _const_spec((1, D_MODEL)), _const_spec((1, D_MODEL)),
        _const_spec((D_MODEL, 2 * ROUTER_LANES)), _const_spec((1, ROUTER_LANES)),
        pl.BlockSpec((epc, D_MODEL, D_EXPERT), lambda c: (c, 0, 0)),
        pl.BlockSpec((epc, D_MODEL, D_EXPERT), lambda c: (c, 0, 0)),
        pl.BlockSpec((epc, D_EXPERT, D_MODEL), lambda c: (c, 0, 0)),
    ]
    out_specs = [
        pl.BlockSpec(memory_space=pl.ANY),
        pl.BlockSpec((ROUTER_ROWS, M1), lambda c: (0, c)),
        pl.BlockSpec((epc, D_MODEL, 2 * D_EXPERT), lambda c: (c, 0, 0)),
        pl.BlockSpec((epc, D_EXPERT, D_MODEL), lambda c: (c, 0, 0)),
    ]
    return pl.pallas_call(
        _mixer_kernel,
        grid=(n_chunks,),
        in_specs=in_specs,
        out_specs=out_specs,
        out_shape=[
            jax.ShapeDtypeStruct((N_TOKENS, LANE_TILES, LANES), jnp.float32),
            jax.ShapeDtypeStruct((ROUTER_ROWS, N_TOKENS), jnp.float32),
            jax.ShapeDtypeStruct((N_EXPERTS, D_MODEL, 2 * D_EXPERT), jnp.bfloat16),
            jax.ShapeDtypeStruct((N_EXPERTS, D_EXPERT, D_MODEL), jnp.bfloat16),
        ],
        scratch_shapes=[
            pltpu.VMEM((2, TS, BATCH, D_MODEL), jnp.float32),
            pltpu.SemaphoreType.DMA((2,)),
            pltpu.VMEM((2, LANE_TILES, M1, LANES), jnp.float32),
            pltpu.SemaphoreType.DMA((2,)),
            pltpu.VMEM((M1 + (RG_CONV - 1) * BATCH, D_RG), jnp.float32),
            pltpu.VMEM((M1 + (SC_CONV - 1) * BATCH, D_SC), jnp.float32),
            pltpu.VMEM((M1, D_RG), jnp.float32),
            pltpu.VMEM((M1, D_RG), jnp.float32),
            pltpu.VMEM((BATCH, D_RG), jnp.float32),
            pltpu.VMEM((D_RG, D_RG), jnp.bfloat16),
            pltpu.VMEM((D_RG, D_RG), jnp.bfloat16),
            pltpu.VMEM((D_MODEL, D_IN_PROJ), jnp.bfloat16),
            pltpu.VMEM((D_MODEL, D_MODEL), jnp.bfloat16),
        ],
        compiler_params=pltpu.CompilerParams(
            dimension_semantics=("arbitrary",), vmem_limit_bytes=VMEM_LIMIT),
        name="mixer_ln_router",
    )(x, w_in, cw, cb, wa, ba, wx, bx, lam, scw, w_out, g1, b1, wr, br, wg, wu, wd)


def _expert_kernel(blk_ref, grp_ref, ea_ref, eb_ref, rows_ref,
                   xs_hbm, wt_ref, wgu_ref, wd_ref, g2_ref, b2_ref, ys_hbm,
                   xbuf_ref, xsem, obuf_ref, osem):
    s = pl.program_id(0)
    slot = s % 2
    tile_rows = [rows_ref[s * SUB + j] for j in range(SUB)]
    block_rows = SUB * TM

    def fetches(step, slot_):
        return [pltpu.make_async_copy(xs_hbm.at[pl.ds(step * block_rows, block_rows), j, :],
                                      xbuf_ref.at[slot_, j], xsem.at[slot_])
                for j in range(LANE_TILES)]

    def write_backs(step, slot_):
        return [pltpu.make_async_copy(obuf_ref.at[slot_, j],
                                      ys_hbm.at[pl.ds(step * block_rows, block_rows), j, :],
                                      osem.at[slot_])
                for j in range(LANE_TILES)]

    def did_work(step):
        return blk_ref[jnp.clip(step, 0, N_STEPS3 - 1)] == step

    @pl.when(jnp.logical_and(s == 0, did_work(0)))
    def _():
        _start_all(fetches(0, 0))

    @pl.when(did_work(s + 1))
    def _():
        _start_all(fetches(s + 1, 1 - slot))

    @pl.when(did_work(s - 2))
    def _():
        for cp in write_backs(s - 2, slot):
            cp.wait()

    @pl.when(did_work(s))
    def _():
        for cp in fetches(s, slot):
            cp.wait()
        for j in range(SUB):
            tile = s * SUB + j
            first_row = j * TM
            in_use = lax.broadcasted_iota(jnp.int32, (TM, 1), 0) < tile_rows[j]
            x = jnp.concatenate([xbuf_ref[slot, t, first_row:first_row + TM, :]
                                 for t in range(LANE_TILES)], axis=1)
            x = jnp.where(in_use, x, 0.0)
            xb = x.astype(jnp.bfloat16)
            wt = jnp.where(in_use, wt_ref[first_row:first_row + TM, :], 0.0)
            y = jnp.zeros((TM, D_MODEL), jnp.float32)
            for e_ref, col in ((ea_ref, 0), (eb_ref, 1)):
                e = e_ref[tile]
                hgu = jnp.dot(xb, wgu_ref[e], preferred_element_type=jnp.float32)
                hid = jax.nn.silu(hgu[:, :D_EXPERT]) * hgu[:, D_EXPERT:] * wt[:, col:col + 1]
                y = y + jnp.dot(hid.astype(jnp.bfloat16), wd_ref[e],
                                preferred_element_type=jnp.float32)
            out = _layer_norm(DEEPNORM_ALPHA * x + y, g2_ref[...], b2_ref[...])
            for t in range(LANE_TILES):
                obuf_ref[slot, t, first_row:first_row + TM, :] = out[:, t * LANES:(t + 1) * LANES]
        _start_all(write_backs(s, slot))

    @pl.when(s == N_STEPS3 - 1)
    def _():
        @pl.when(did_work(s - 1))
        def _():
            for cp in write_backs(s - 1, 1 - slot):
                cp.wait()

        @pl.when(did_work(s))
        def _():
            for cp in write_backs(s, slot):
                cp.wait()


def _expert_call(step_block, step_group, tile_ea, tile_eb, tile_rows, xs, wts, wgu, wd, g2, b2):
    rows = SUB * TM
    grid_spec = pltpu.PrefetchScalarGridSpec(
        num_scalar_prefetch=5,
        grid=(N_STEPS3,),
        in_specs=[
            pl.BlockSpec(memory_space=pl.ANY),
            pl.BlockSpec((rows, LANES), lambda s, blk, grp, ea, eb, va: (blk[s], 0)),
            pl.BlockSpec((None, EXPERTS_PER_GROUP, D_MODEL, 2 * D_EXPERT),
                         lambda s, blk, grp, ea, eb, va: (grp[s], 0, 0, 0)),
            pl.BlockSpec((None, EXPERTS_PER_GROUP, D_EXPERT, D_MODEL),
                         lambda s, blk, grp, ea, eb, va: (grp[s], 0, 0, 0)),
            pl.BlockSpec((1, D_MODEL), lambda s, blk, grp, ea, eb, va: (0, 0)),
            pl.BlockSpec((1, D_MODEL), lambda s, blk, grp, ea, eb, va: (0, 0)),
        ],
        out_specs=pl.BlockSpec(memory_space=pl.ANY),
        scratch_shapes=[
            pltpu.VMEM((2, LANE_TILES, rows, LANES), jnp.float32),
            pltpu.SemaphoreType.DMA((2,)),
            pltpu.VMEM((2, LANE_TILES, rows, LANES), jnp.float32),
            pltpu.SemaphoreType.DMA((2,)),
        ],
    )
    return pl.pallas_call(
        _expert_kernel,
        grid_spec=grid_spec,
        out_shape=jax.ShapeDtypeStruct((P_ROWS, LANE_TILES, LANES), jnp.float32),
        compiler_params=pltpu.CompilerParams(
            dimension_semantics=("arbitrary",), vmem_limit_bytes=VMEM_LIMIT),
        name="experts_ln",
    )(step_block, step_group, tile_ea, tile_eb, tile_rows, xs, wts, wgu, wd, g2, b2)


SC_CORES = 2
SC_SUBCORES = 16
SC_WORKERS = SC_CORES * SC_SUBCORES
COMBINE_CHUNK = 32


def _sc_gather_rows(table, idx, chunk, out_row_shape=None):
    n_out, = idx.shape
    row_shape = table.shape[1:]
    out_row_shape = row_shape if out_row_shape is None else out_row_shape
    per_w = n_out // SC_WORKERS
    n_chunks = per_w // chunk
    assert per_w * SC_WORKERS == n_out and n_chunks * chunk == per_w
    assert n_chunks % 2 == 0 and chunk % 8 == 0
    mesh = plsc.VectorSubcoreMesh(core_axis_name="c", subcore_axis_name="s")

    @functools.partial(
        pl.kernel, mesh=mesh,
        out_type=jax.ShapeDtypeStruct((n_out,) + out_row_shape, table.dtype),
        scratch_types=[
            pltpu.VMEM((per_w,), jnp.int32),
            pltpu.VMEM((2, chunk) + row_shape, table.dtype),
            pltpu.SemaphoreType.DMA((2,)),
            pltpu.SemaphoreType.DMA((2,)),
        ],
    )
    def gather_kernel(table_hbm, idx_hbm, out_hbm, idx_v, buf, gsem, wsem):
        wid = lax.axis_index("s") * SC_CORES + lax.axis_index("c")
        base = wid * per_w
        pltpu.sync_copy(idx_hbm.at[pl.ds(base, per_w)], idx_v)

        def gather(j, slot):
            rows = idx_v.at[pl.ds(j * chunk, chunk)]
            return pltpu.make_async_copy(table_hbm.at[rows], buf.at[slot], gsem.at[slot])

        def write(j, slot):
            dst = out_hbm.at[pl.ds(base + j * chunk, chunk)]
            src = buf.at[slot].reshape((chunk,) + out_row_shape)
            return pltpu.make_async_copy(src, dst, wsem.at[slot])

        gather(0, 0).start()

        @pl.loop(0, n_chunks, step=2)
        def _(j0):
            for slot in range(2):
                j = j0 + slot
                gather(j, slot).wait()

                @pl.when(j >= 1)
                def _():
                    write(j - 1, 1 - slot).wait()

                @pl.when(j + 1 < n_chunks)
                def _():
                    gather(j + 1, 1 - slot).start()

                write(j, slot).start()

        write(n_chunks - 1, (n_chunks - 1) % 2).wait()

    return gather_kernel(table, idx)


SC_LANES = 16


def _sc_dispatch(rows, meta, cls, rank, class_start):
    n_rows = rows.shape[0]
    chunk = COMBINE_CHUNK
    per_w = n_rows // SC_WORKERS
    n_chunks = per_w // chunk
    assert per_w * SC_WORKERS == n_rows and n_chunks * chunk == per_w
    assert n_chunks % 2 == 0 and chunk % SC_LANES == 0
    mesh = plsc.VectorSubcoreMesh(core_axis_name="c", subcore_axis_name="s")

    @functools.partial(
        pl.kernel, mesh=mesh,
        out_type=[
            jax.ShapeDtypeStruct((P_ROWS,) + rows.shape[1:], rows.dtype),
            jax.ShapeDtypeStruct((P_ROWS,) + meta.shape[1:], meta.dtype),
            jax.ShapeDtypeStruct((n_rows,), jnp.int32),
        ],
        scratch_types=[
            pltpu.VMEM((per_w,), jnp.int32),
            pltpu.VMEM((per_w,), jnp.int32),
            pltpu.VMEM((LANES,), jnp.int32),
            pltpu.VMEM((per_w,), jnp.int32),
            pltpu.VMEM((n_chunks, chunk), jnp.int32),
            pltpu.VMEM((2, chunk) + rows.shape[1:], rows.dtype),
            pltpu.VMEM((2, chunk) + meta.shape[1:], meta.dtype),
            pltpu.SemaphoreType.DMA((2,)),
            pltpu.SemaphoreType.DMA((2,)),
            pltpu.SemaphoreType.DMA((2,)),
            pltpu.SemaphoreType.DMA((2,)),
        ],
        compiler_params=pltpu.CompilerParams(needs_layout_passes=False),
    )
    def dispatch_kernel(rows_hbm, meta_hbm, cls_hbm, rank_hbm, start_hbm,
                        xs_hbm, ws_hbm, pos_hbm,
                        cls_v, rank_v, start_v, pos_flat, pos_v, rbuf, mbuf,
                        rsem, msem, xsem, wsem):
        wid = lax.axis_index("s") * SC_CORES + lax.axis_index("c")
        base = wid * per_w
        pltpu.sync_copy(cls_hbm.at[pl.ds(base, per_w)], cls_v)
        pltpu.sync_copy(rank_hbm.at[pl.ds(base, per_w)], rank_v)
        pltpu.sync_copy(start_hbm, start_v)

        @pl.loop(0, n_chunks)
        def _(j):
            for k in range(chunk // SC_LANES):
                off = j * chunk + k * SC_LANES
                c = cls_v[pl.ds(off, SC_LANES)]
                pos = plsc.load_gather(start_v, [c]) + rank_v[pl.ds(off, SC_LANES)]
                pos_flat[pl.ds(off, SC_LANES)] = pos
                pos_v[j, pl.ds(k * SC_LANES, SC_LANES)] = pos

        pltpu.sync_copy(pos_flat, pos_hbm.at[pl.ds(base, per_w)])

        def read_rows(j, slot):
            src = rows_hbm.at[pl.ds(base + j * chunk, chunk)]
            return pltpu.make_async_copy(src, rbuf.at[slot], rsem.at[slot])

        def read_meta(j, slot):
            src = meta_hbm.at[pl.ds(base + j * chunk, chunk)]
            return pltpu.make_async_copy(src, mbuf.at[slot], msem.at[slot])

        def put_rows(j, slot):
            return pltpu.make_async_copy(rbuf.at[slot], xs_hbm.at[pos_v.at[j]], xsem.at[slot])

        def put_meta(j, slot):
            return pltpu.make_async_copy(mbuf.at[slot], ws_hbm.at[pos_v.at[j]], wsem.at[slot])

        read_rows(0, 0).start()
        read_meta(0, 0).start()

        @pl.loop(0, n_chunks, step=2)
        def _(j0):
            for slot in range(2):
                j = j0 + slot
                read_rows(j, slot).wait()
                read_meta(j, slot).wait()

                @pl.when(j >= 1)
                def _():
                    put_rows(j - 1, 1 - slot).wait()
                    put_meta(j - 1, 1 - slot).wait()

                @pl.when(j + 1 < n_chunks)
                def _():
                    read_rows(j + 1, 1 - slot).start()
                    read_meta(j + 1, 1 - slot).start()

                put_rows(j, slot).start()
                put_meta(j, slot).start()

        put_rows(n_chunks - 1, (n_chunks - 1) % 2).wait()
        put_meta(n_chunks - 1, (n_chunks - 1) % 2).wait()

    return dispatch_kernel(rows, meta, cls, rank, class_start)


def _pair_tables():
    pair_of = np.zeros((EXPERTS_PER_GROUP, EXPERTS_PER_GROUP), np.int32)
    lo = np.zeros((N_PAIRS,), np.int32)
    hi = np.zeros((N_PAIRS,), np.int32)
    p = 0
    for a in range(EXPERTS_PER_GROUP):
        for b in range(a + 1, EXPERTS_PER_GROUP):
            pair_of[a, b] = pair_of[b, a] = p
            lo[p], hi[p] = a, b
            p += 1
    return pair_of, lo, hi


def _route_kernel(lt_ref, info_ref, meta_ref, counts_ref, run_ref):
    step = pl.program_id(0)

    @pl.when(step == 0)
    def _():
        run_ref[...] = jnp.zeros_like(run_ref)

    f32 = jnp.float32
    sub8 = lax.broadcasted_iota(jnp.int32, (8, LANES), 0).astype(f32)
    row_id = lax.broadcasted_iota(jnp.int32, (LANES, LANES), 0)
    col_id = lax.broadcasted_iota(jnp.int32, (LANES, LANES), 1)
    prefix_mat = (row_id <= col_id).astype(jnp.bfloat16)
    ones_mat = jnp.ones((LANES, LANES), jnp.bfloat16)
    neg_inf = f32(-jnp.inf)

    def first_index_of_max(v):
        m = jnp.max(v, axis=0, keepdims=True)
        idx = jnp.min(jnp.where(v == m, sub8, f32(8)), axis=0, keepdims=True)
        return m, idx

    def lane_tile(k, run):
        lanes = pl.ds(pl.multiple_of(k * LANES, LANES), LANES)
        g = jnp.where(sub8 < N_GROUPS, lt_ref[0:GROUP_ROWS, lanes], neg_inf)
        g_max, g_idx = first_index_of_max(g)
        g_top_p = 1.0 / jnp.sum(jnp.exp(g - g_max), axis=0, keepdims=True)

        e_sel = lt_ref[GROUP_ROWS:GROUP_ROWS + EXPERTS_PER_GROUP, lanes]
        for grp in range(1, N_GROUPS):
            lo = GROUP_ROWS + grp * EXPERTS_PER_GROUP
            e_sel = jnp.where(g_idx == grp, lt_ref[lo:lo + EXPERTS_PER_GROUP, lanes], e_sel)
        m1, i1 = first_index_of_max(e_sel)
        rest = jnp.where(sub8 == i1, neg_inf, e_sel)
        m2 = jnp.max(rest, axis=0, keepdims=True)
        i2 = jnp.min(jnp.where((rest == m2) & (sub8 != i1), sub8, f32(8)), axis=0, keepdims=True)

        e = jnp.exp(m2 - m1)
        w1 = g_top_p / (1.0 + e)
        w2 = g_top_p * e / (1.0 + e)
        first_is_lo = i1 < i2
        w_lo = jnp.where(first_is_lo, w1, w2)
        w_hi = jnp.where(first_is_lo, w2, w1)
        lo_e = jnp.minimum(i1, i2)
        hi_e = jnp.maximum(i1, i2)
        pair = lo_e * (2 * EXPERTS_PER_GROUP - 1 - lo_e) * 0.5 + hi_e - lo_e - 1.0
        cls = (g_idx * N_PAIRS + pair).astype(jnp.int32)

        onehot = (row_id == cls).astype(jnp.bfloat16)
        csum = jnp.dot(onehot, prefix_mat, preferred_element_type=f32)
        rank = jnp.sum(onehot.astype(f32) * (csum + run), axis=0, keepdims=True) - 1.0
        run = run + jnp.dot(onehot, ones_mat, preferred_element_type=f32)

        info_ref[:, lanes] = jnp.where(sub8 == 0, cls, jnp.where(sub8 == 1, rank.astype(jnp.int32), 0))
        meta8 = jnp.where(sub8 == 0, w_lo, jnp.where(sub8 == 1, w_hi, 0.0))
        meta_t = jnp.concatenate([meta8, jnp.zeros((LANES - 8, LANES), f32)], axis=0)
        meta_ref[lanes, :] = meta_t.T
        return run

    run = lax.fori_loop(0, ROUTE_BLOCK // LANES, lane_tile, run_ref[...], unroll=ROUTE_UNROLL)
    run_ref[...] = run
    counts_ref[...] = run


def _route_call(lt):
    return pl.pallas_call(
        _route_kernel,
        grid=(N_TOKENS // ROUTE_BLOCK,),
        in_specs=[pl.BlockSpec((ROUTER_ROWS, ROUTE_BLOCK), lambda i: (0, i))],
        out_specs=[
            pl.BlockSpec((8, ROUTE_BLOCK), lambda i: (0, i)),
            pl.BlockSpec((ROUTE_BLOCK, LANES), lambda i: (i, 0)),
            pl.BlockSpec((LANES, LANES), lambda i: (0, 0)),
        ],
        out_shape=[
            jax.ShapeDtypeStruct((8, N_TOKENS), jnp.int32),
            jax.ShapeDtypeStruct((N_TOKENS, LANES), jnp.float32),
            jax.ShapeDtypeStruct((LANES, LANES), jnp.float32),
        ],
        scratch_shapes=[pltpu.VMEM((LANES, LANES), jnp.float32)],
        compiler_params=pltpu.CompilerParams(dimension_semantics=("arbitrary",)),
        name="route_rank",
    )(lt)


def _dispatch_plan(counts):
    _, pair_lo, pair_hi = _pair_tables()
    tiles_c = (counts + TM - 1) // TM
    tiles_g = tiles_c.reshape(N_GROUPS, N_PAIRS).sum(axis=1)
    tiles_g_pad = (tiles_g + SUB - 1) // SUB * SUB
    g_start = jnp.cumsum(tiles_g_pad) - tiles_g_pad
    tc = tiles_c.reshape(N_GROUPS, N_PAIRS)
    c_start = (g_start[:, None] + jnp.cumsum(tc, axis=1) - tc).reshape(N_CLASSES)
    class_start = jnp.zeros((LANES,), jnp.int32).at[:N_CLASSES].set(c_start * TM)

    tile_ids = jnp.arange(N_TILES, dtype=jnp.int32)
    c_end = c_start + tiles_c
    owner = jnp.sum((tile_ids[:, None] >= c_end[None, :]).astype(jnp.int32), axis=1)
    owner = jnp.minimum(owner, N_CLASSES - 1)
    in_class = (tile_ids >= c_start[owner]) & (tile_ids < c_end[owner])
    rows_left = counts[owner] - (tile_ids - c_start[owner]) * TM
    tile_rows = jnp.where(in_class, jnp.minimum(rows_left, TM), 0).astype(jnp.int32)
    pair = owner % N_PAIRS
    tile_ea = jnp.asarray(pair_lo)[pair]
    tile_eb = jnp.asarray(pair_hi)[pair]
    g_end = g_start + tiles_g_pad
    step_first = jnp.arange(N_STEPS3, dtype=jnp.int32) * SUB
    step_group = jnp.sum((step_first[:, None] >= g_end[None, :]).astype(jnp.int32), axis=1)
    step_group = jnp.minimum(step_group, N_GROUPS - 1)
    last_step = jnp.maximum(g_end[-1] // SUB - 1, 0)
    step_block = jnp.minimum(jnp.arange(N_STEPS3, dtype=jnp.int32), last_step)
    return (class_start, tile_ea, tile_eb, tile_rows, step_group.astype(jnp.int32),
            step_block.astype(jnp.int32))


def kernel(x, w_in, rg_conv_w, rg_conv_b, rg_gate_a_w, rg_gate_a_b, rg_gate_x_w, rg_gate_x_b,
           rg_lambda, sc_conv_w, w_out, ln1_g, ln1_b, router_group_w, router_group_b,
           router_expert_w, router_expert_b, exp_w_gate, exp_w_up, exp_w_down, ln2_g, ln2_b):
    bf16 = jnp.bfloat16
    row = lambda v: v.reshape(1, -1)
    pad_g = GROUP_ROWS - N_GROUPS
    pad_e = ROUTER_LANES - ROUTER_ROWS
    wr = jnp.concatenate([router_group_w, jnp.zeros((D_MODEL, pad_g), jnp.float32),
                          router_expert_w, jnp.zeros((D_MODEL, pad_e), jnp.float32)], axis=1)
    br = jnp.concatenate([router_group_b, jnp.zeros((pad_g,), jnp.float32),
                          router_expert_b, jnp.zeros((pad_e,), jnp.float32)]).reshape(1, -1)
    wr_hi = wr.astype(bf16)
    wr_lo = (wr - wr_hi.astype(jnp.float32)).astype(bf16)

    h1, lt, wgu, wdb = _mixer_call(
        x, w_in, rg_conv_w, row(rg_conv_b),
        rg_gate_a_w, row(rg_gate_a_b),
        rg_gate_x_w, row(rg_gate_x_b),
        row(rg_lambda), sc_conv_w, w_out, row(ln1_g), row(ln1_b),
        jnp.concatenate([wr_hi, wr_lo], axis=1), br, exp_w_gate, exp_w_up, exp_w_down)

    info, meta, counts = _route_call(lt)
    class_start, tile_ea, tile_eb, tile_rows, step_group, step_block = _dispatch_plan(
        counts[:N_CLASSES, 0].astype(jnp.int32))

    xs, wts, pos = _sc_dispatch(h1, meta, info[0], info[1], class_start)

    grp_shape = (N_GROUPS, EXPERTS_PER_GROUP)
    ys = _expert_call(
        step_block, step_group, tile_ea, tile_eb, tile_rows, xs, wts,
        wgu.reshape(grp_shape + (D_MODEL, 2 * D_EXPERT)),
        wdb.reshape(grp_shape + (D_EXPERT, D_MODEL)),
        row(ln2_g), row(ln2_b))

    out_rows = pos.reshape(SEQ, BATCH).T.reshape(-1)
    out = _sc_gather_rows(ys, out_rows, COMBINE_CHUNK, (D_MODEL,))
    return out.reshape(BATCH, SEQ, D_MODEL)
```

```python
import functools

import jax
import jax.numpy as jnp
import numpy as np
from jax import lax
from jax.experimental import pallas as pl
from jax.experimental.pallas import tpu as pltpu
from jax.experimental.pallas import tpu_sc as plsc

D_MODEL = 1024
BATCH = 16
SEQ = 2048
D_RG = 512
D_SC = 512
RG_HEADS = 8
RG_HEAD_DIM = D_RG // RG_HEADS
RG_CONV = 4
RG_C = 8.0
SC_CONV = 3
D_IN_PROJ = 2 * D_RG + 3 * D_SC
N_GROUPS = 4
EXPERTS_PER_GROUP = 8
N_EXPERTS = N_GROUPS * EXPERTS_PER_GROUP
D_EXPERT = D_MODEL // 4
LN_EPS = 1e-5
DEEPNORM_ALPHA = 2.0 ** 0.25

N_TOKENS = BATCH * SEQ
LANES = 128
LANE_TILES = D_MODEL // LANES
MXU_WIDTH = 256
ROUTER_LANES = LANES
GROUP_ROWS = 8
ROUTER_ROWS = GROUP_ROWS + N_EXPERTS

TS = 64
M1 = TS * BATCH
DOT_ROWS = 256

ROUTE_BLOCK = 8192
ROUTE_UNROLL = 8

N_PAIRS = EXPERTS_PER_GROUP * (EXPERTS_PER_GROUP - 1) // 2
N_CLASSES = N_GROUPS * N_PAIRS
BF16_ROWS = 16
TM = -(-int(N_TOKENS / N_CLASSES * 1.09 / 2) // BF16_ROWS) * BF16_ROWS
SUB = 8
N_TILES = -(-(N_TOKENS // TM + N_CLASSES + N_GROUPS * (SUB - 1)) // SUB) * SUB
N_STEPS3 = N_TILES // SUB
P_ROWS = N_TILES * TM

VMEM_LIMIT = 58 * 1024 * 1024


def _layer_norm(z, g, b):
    mu = jnp.mean(z, axis=-1, keepdims=True)
    zc = z - mu
    var = jnp.mean(zc * zc, axis=-1, keepdims=True)
    return zc * lax.rsqrt(var + LN_EPS) * g + b


def _mixer_kernel(x_ref, w_in_f32_ref, cw_ref, cb_ref, wa_heads_ref, ba_ref, wx_heads_ref, bx_ref,
                  lam_ref, scw_ref, w_out_f32_ref, g1_ref, b1_ref, wr_ref, br_ref,
                  wg_ref, wu_ref, wd_ref,
                  h1_hbm, logit_ref, wgu_ref, wdb_ref,
                  xbuf_ref, xsem, hbuf_ref, hsem, cbuf_ref, sbuf_ref, a_ref, u_ref, hstate_ref,
                  wa_ref, wx_ref, w_in_ref, w_out_ref):
    c = pl.program_id(0)
    n_steps = SEQ // TS

    def h1_copies(step, slot_):
        return [pltpu.make_async_copy(hbuf_ref.at[slot_, j],
                                      h1_hbm.at[pl.ds(step * M1, M1), j, :], hsem.at[slot_])
                for j in range(LANE_TILES)]

    @pl.when(c >= 2)
    def _():
        for cp in h1_copies(c - 2, c % 2):
            cp.wait()

    wgu_ref[:, :, :D_EXPERT] = wg_ref[...].astype(jnp.bfloat16)
    wgu_ref[:, :, D_EXPERT:] = wu_ref[...].astype(jnp.bfloat16)
    wdb_ref[...] = wd_ref[...].astype(jnp.bfloat16)
    slot = c % 2
    rg_pad = (RG_CONV - 1) * BATCH
    sc_pad = (SC_CONV - 1) * BATCH

    def x_copies(chunk, slot_):
        return [pltpu.make_async_copy(x_ref.at[b, pl.ds(chunk * TS, TS), :],
                                      xbuf_ref.at[slot_, :, b, :], xsem.at[slot_])
                for b in range(BATCH)]

    @pl.when(c == 0)
    def _():
        for cp in x_copies(0, 0):
            cp.start()
        cbuf_ref[0:rg_pad, :] = jnp.zeros((rg_pad, D_RG), jnp.float32)
        sbuf_ref[0:sc_pad, :] = jnp.zeros((sc_pad, D_SC), jnp.float32)
        hstate_ref[...] = jnp.zeros_like(hstate_ref)
        w_in_ref[...] = w_in_f32_ref[...].astype(jnp.bfloat16)
        w_out_ref[...] = w_out_f32_ref[...].astype(jnp.bfloat16)
        for heads_ref, bd_ref in ((wa_heads_ref, wa_ref), (wx_heads_ref, wx_ref)):
            bd_ref[...] = jnp.zeros_like(bd_ref)
            for hd in range(RG_HEADS):
                lo = hd * RG_HEAD_DIM
                bd_ref[lo:lo + RG_HEAD_DIM, lo:lo + RG_HEAD_DIM] = (
                    heads_ref[hd].astype(jnp.bfloat16))

    @pl.when(c + 1 < SEQ // TS)
    def _():
        for cp in x_copies(c + 1, 1 - slot):
            cp.start()

    for cp in x_copies(c, slot):
        cp.wait()

    xt_ref = xbuf_ref.at[slot]

    def gate(xcb, w_ref, b_ref):
        parts = [jnp.dot(xcb[:, lo:lo + MXU_WIDTH], w_ref[lo:lo + MXU_WIDTH, lo:lo + MXU_WIDTH],
                         preferred_element_type=jnp.float32)
                 for lo in range(0, D_RG, MXU_WIDTH)]
        return 0.5 + 0.5 * jnp.tanh(0.5 * (jnp.concatenate(parts, axis=1) + b_ref[...]))

    soft_lam = jax.nn.softplus(-lam_ref[...])
    gelu_chunks, y_sc_chunks = [], []
    for r0 in range(0, M1, DOT_ROWS):
        t0 = r0 // BATCH
        xk = xt_ref[t0:t0 + DOT_ROWS // BATCH].reshape(DOT_ROWS, D_MODEL).astype(jnp.bfloat16)

        def proj(lo, width):
            return jnp.dot(xk, w_in_ref[:, lo:lo + width], preferred_element_type=jnp.float32)

        cbuf_ref[rg_pad + r0:rg_pad + r0 + DOT_ROWS, :] = proj(0, D_RG)
        rg_gate = proj(D_RG, D_RG)
        xc = cb_ref[...] + cw_ref[0:1, :] * cbuf_ref[r0:r0 + DOT_ROWS, :]
        for k in range(1, RG_CONV):
            xc = xc + cw_ref[k:k + 1, :] * cbuf_ref[r0 + k * BATCH:r0 + k * BATCH + DOT_ROWS, :]
        sc_b = proj(2 * D_RG, D_SC)
        xcb = xc.astype(jnp.bfloat16)
        r = gate(xcb, wa_ref, ba_ref)
        i = gate(xcb, wx_ref, bx_ref)
        sc_c = proj(2 * D_RG + D_SC, D_SC)
        log_a = (-RG_C) * r * soft_lam
        a_ref[r0:r0 + DOT_ROWS, :] = jnp.exp(log_a)
        th = jnp.tanh(log_a)
        u_ref[r0:r0 + DOT_ROWS, :] = jnp.sqrt(-2.0 * th / (1.0 - th)) * (i * xc)
        gelu_chunks.append(jax.nn.gelu(rg_gate))

        sbuf_ref[sc_pad + r0:sc_pad + r0 + DOT_ROWS, :] = sc_c * proj(2 * D_RG + 2 * D_SC, D_SC)
        conv = scw_ref[0:1, :] * sbuf_ref[r0:r0 + DOT_ROWS, :]
        for k in range(1, SC_CONV):
            conv = conv + scw_ref[k:k + 1, :] * sbuf_ref[r0 + k * BATCH:r0 + k * BATCH + DOT_ROWS, :]
        y_sc_chunks.append((sc_b * conv).astype(jnp.bfloat16))

    cbuf_ref[0:rg_pad, :] = cbuf_ref[M1:M1 + rg_pad, :]
    sbuf_ref[0:sc_pad, :] = sbuf_ref[M1:M1 + sc_pad, :]

    h = hstate_ref[...]
    for t in range(TS):
        h = a_ref[t * BATCH:(t + 1) * BATCH, :] * h + u_ref[t * BATCH:(t + 1) * BATCH, :]
        u_ref[t * BATCH:(t + 1) * BATCH, :] = h
    hstate_ref[...] = h

    tail_rows = M1 // 2
    per_tail = tail_rows // DOT_ROWS

    def out_proj(k):
        r0 = k * tail_rows
        gelu_k = jnp.concatenate(gelu_chunks[k * per_tail:(k + 1) * per_tail], axis=0)
        y_sc_k = jnp.concatenate(y_sc_chunks[k * per_tail:(k + 1) * per_tail], axis=0)
        y_rg = (u_ref[r0:r0 + tail_rows, :] * gelu_k).astype(jnp.bfloat16)
        mix = jnp.dot(y_rg, w_out_ref[0:D_RG, :], preferred_element_type=jnp.float32)
        return mix + jnp.dot(y_sc_k, w_out_ref[D_RG:, :], preferred_element_type=jnp.float32)

    def finish(k, mix):
        r0 = k * tail_rows
        t0 = r0 // BATCH
        x_rows = xt_ref[t0:t0 + tail_rows // BATCH].reshape(tail_rows, D_MODEL)
        h1 = _layer_norm(DEEPNORM_ALPHA * x_rows + mix, g1_ref[...], b1_ref[...])
        for t in range(LANE_TILES):
            hbuf_ref[slot, t, r0:r0 + tail_rows, :] = h1[:, t * LANES:(t + 1) * LANES]
        h_hi = h1.astype(jnp.bfloat16)
        h_lo = (h1 - h_hi.astype(jnp.float32)).astype(jnp.bfloat16)
        both = jnp.dot(h_hi, wr_ref[...], preferred_element_type=jnp.float32)
        logits = both[:, :ROUTER_LANES] + both[:, ROUTER_LANES:] + br_ref[...]
        logits = logits + jnp.dot(h_lo, wr_ref[:, :ROUTER_LANES],
                                  preferred_element_type=jnp.float32)
        logit_ref[:, r0:r0 + tail_rows] = logits.T[:ROUTER_ROWS, :]

    n_tails = M1 // tail_rows
    mix = out_proj(0)
    for k in range(n_tails):
        next_mix = out_proj(k + 1) if k + 1 < n_tails else None
        finish(k, mix)
        mix = next_mix

    for cp in h1_copies(c, slot):
        cp.start()

    @pl.when(c == n_steps - 1)
    def _():
        if n_steps >= 2:
            for cp in h1_copies(c - 1, 1 - slot):
                cp.wait()
        for cp in h1_copies(c, slot):
            cp.wait()


def _const_spec(shape, pipeline_mode=None):
    if pipeline_mode is None:
        return pl.BlockSpec(shape, lambda c: (0,) * len(shape))
    return pl.BlockSpec(shape, lambda c: (0,) * len(shape), pipeline_mode=pipeline_mode)


def _mixer_call(x, w_in, cw, cb, wa, ba, wx, bx, lam, scw, w_out, g1, b1, wr, br, wg, wu, wd):
    n_chunks = SEQ // TS
    epc = N_EXPERTS // n_chunks
    assert epc * n_chunks == N_EXPERTS
    in_specs = [
        pl.BlockSpec(memory_space=pl.ANY),
        _const_spec((D_MODEL, D_IN_PROJ), pl.Buffered(1)),
        _const_spec((RG_CONV, D_RG)), _const_spec((1, D_RG)),
        _const_spec((RG_HEADS, RG_HEAD_DIM, RG_HEAD_DIM)), _const_spec((1, D_RG)),
        _const_spec((RG_HEADS, RG_HEAD_DIM, RG_HEAD_DIM)), _const_spec((1, D_RG)),
        _const_spec((1, D_RG)),
        _const_spec((SC_CONV, D_SC)),
        _const_spec((D_MODEL, D_MODEL), pl.Buffered(1)),
        _const_spec((1, D_MODEL)), _const_spec((1, D_MODEL)),
        _const_spec((D_MODEL, 2 * ROUTER_LANES)), _const_spec((1, ROUTER_LANES)),
        pl.BlockSpec((epc, D_MODEL, D_EXPERT), lambda c: (c, 0, 0)),
        pl.BlockSpec((epc, D_MODEL, D_EXPERT), lambda c: (c, 0, 0)),
        pl.BlockSpec((epc, D_EXPERT, D_MODEL), lambda c: (c, 0, 0)),
    ]
    out_specs = [
        pl.BlockSpec(memory_space=pl.ANY),
        pl.BlockSpec((ROUTER_ROWS, M1), lambda c: (0, c)),
        pl.BlockSpec((epc, D_MODEL, 2 * D_EXPERT), lambda c: (c, 0, 0)),
        pl.BlockSpec((epc, D_EXPERT, D_MODEL), lambda c: (c, 0, 0)),
    ]
    return pl.pallas_call(
        _mixer_kernel,
        grid=(n_chunks,),
        in_specs=in_specs,
        out_specs=out_specs,
        out_shape=[
            jax.ShapeDtypeStruct((N_TOKENS, LANE_TILES, LANES), jnp.float32),
            jax.ShapeDtypeStruct((ROUTER_ROWS, N_TOKENS), jnp.float32),
            jax.ShapeDtypeStruct((N_EXPERTS, D_MODEL, 2 * D_EXPERT), jnp.bfloat16),
            jax.ShapeDtypeStruct((N_EXPERTS, D_EXPERT, D_MODEL), jnp.bfloat16),
        ],
        scratch_shapes=[
            pltpu.VMEM((2, TS, BATCH, D_MODEL), jnp.float32),
            pltpu.SemaphoreType.DMA((2,)),
            pltpu.VMEM((2, LANE_TILES, M1, LANES), jnp.float32),
            pltpu.SemaphoreType.DMA((2,)),
            pltpu.VMEM((M1 + (RG_CONV - 1) * BATCH, D_RG), jnp.float32),
            pltpu.VMEM((M1 + (SC_CONV - 1) * BATCH, D_SC), jnp.float32),
            pltpu.VMEM((M1, D_RG), jnp.float32),
            pltpu.VMEM((M1, D_RG), jnp.float32),
            pltpu.VMEM((BATCH, D_RG), jnp.float32),
            pltpu.VMEM((D_RG, D_RG), jnp.bfloat16),
            pltpu.VMEM((D_RG, D_RG), jnp.bfloat16),
            pltpu.VMEM((D_MODEL, D_IN_PROJ), jnp.bfloat16),
            pltpu.VMEM((D_MODEL, D_MODEL), jnp.bfloat16),
        ],
        compiler_params=pltpu.CompilerParams(
            dimension_semantics=("arbitrary",), vmem_limit_bytes=VMEM_LIMIT),
        name="mixer_ln_router",
    )(x, w_in, cw, cb, wa, ba, wx, bx, lam, scw, w_out, g1, b1, wr, br, wg, wu, wd)


def _expert_kernel(blk_ref, grp_ref, ea_ref, eb_ref, rows_ref,
                   xs_hbm, wt_ref, wgu_ref, wd_ref, g2_ref, b2_ref, ys_hbm,
                   xbuf_ref, xsem, obuf_ref, osem):
    s = pl.program_id(0)
    slot = s % 2
    tile_rows = [rows_ref[s * SUB + j] for j in range(SUB)]
    block_rows = SUB * TM

    def fetches(step, slot_):
        return [pltpu.make_async_copy(xs_hbm.at[pl.ds(step * block_rows, block_rows), j, :],
                                      xbuf_ref.at[slot_, j], xsem.at[slot_])
                for j in range(LANE_TILES)]

    def write_backs(step, slot_):
        return [pltpu.make_async_copy(obuf_ref.at[slot_, j],
                                      ys_hbm.at[pl.ds(step * block_rows, block_rows), j, :],
                                      osem.at[slot_])
                for j in range(LANE_TILES)]

    def did_work(step):
        return blk_ref[jnp.clip(step, 0, N_STEPS3 - 1)] == step

    @pl.when(jnp.logical_and(s == 0, did_work(0)))
    def _():
        for cp in fetches(0, 0):
            cp.start()

    @pl.when(did_work(s + 1))
    def _():
        for cp in fetches(s + 1, 1 - slot):
            cp.start()

    @pl.when(did_work(s - 2))
    def _():
        for cp in write_backs(s - 2, slot):
            cp.wait()

    @pl.when(did_work(s))
    def _():
        for cp in fetches(s, slot):
            cp.wait()
        for j in range(SUB):
            tile = s * SUB + j
            first_row = j * TM
            in_use = lax.broadcasted_iota(jnp.int32, (TM, 1), 0) < tile_rows[j]
            x = jnp.concatenate([xbuf_ref[slot, t, first_row:first_row + TM, :]
                                 for t in range(LANE_TILES)], axis=1)
            x = jnp.where(in_use, x, 0.0)
            xb = x.astype(jnp.bfloat16)
            wt = jnp.where(in_use, wt_ref[first_row:first_row + TM, :], 0.0)
            y = jnp.zeros((TM, D_MODEL), jnp.float32)
            for e_ref, col in ((ea_ref, 0), (eb_ref, 1)):
                e = e_ref[tile]
                hgu = jnp.dot(xb, wgu_ref[e], preferred_element_type=jnp.float32)
                hid = jax.nn.silu(hgu[:, :D_EXPERT]) * hgu[:, D_EXPERT:] * wt[:, col:col + 1]
                y = y + jnp.dot(hid.astype(jnp.bfloat16), wd_ref[e],
                                preferred_element_type=jnp.float32)
            out = _layer_norm(DEEPNORM_ALPHA * x + y, g2_ref[...], b2_ref[...])
            for t in range(LANE_TILES):
                obuf_ref[slot, t, first_row:first_row + TM, :] = out[:, t * LANES:(t + 1) * LANES]
        for cp in write_backs(s, slot):
            cp.start()

    @pl.when(s == N_STEPS3 - 1)
    def _():
        @pl.when(did_work(s - 1))
        def _():
            for cp in write_backs(s - 1, 1 - slot):
                cp.wait()

        @pl.when(did_work(s))
        def _():
            for cp in write_backs(s, slot):
                cp.wait()


def _expert_call(step_block, step_group, tile_ea, tile_eb, tile_rows, xs, wts, wgu, wd, g2, b2):
    rows = SUB * TM
    grid_spec = pltpu.PrefetchScalarGridSpec(
        num_scalar_prefetch=5,
        grid=(N_STEPS3,),
        in_specs=[
            pl.BlockSpec(memory_space=pl.ANY),
            pl.BlockSpec((rows, LANES), lambda s, blk, grp, ea, eb, va: (blk[s], 0)),
            pl.BlockSpec((None, EXPERTS_PER_GROUP, D_MODEL, 2 * D_EXPERT),
                         lambda s, blk, grp, ea, eb, va: (grp[s], 0, 0, 0)),
            pl.BlockSpec((None, EXPERTS_PER_GROUP, D_EXPERT, D_MODEL),
                         lambda s, blk, grp, ea, eb, va: (grp[s], 0, 0, 0)),
            pl.BlockSpec((1, D_MODEL), lambda s, blk, grp, ea, eb, va: (0, 0)),
            pl.BlockSpec((1, D_MODEL), lambda s, blk, grp, ea, eb, va: (0, 0)),
        ],
        out_specs=pl.BlockSpec(memory_space=pl.ANY),
        scratch_shapes=[
            pltpu.VMEM((2, LANE_TILES, rows, LANES), jnp.float32),
            pltpu.SemaphoreType.DMA((2,)),
            pltpu.VMEM((2, LANE_TILES, rows, LANES), jnp.float32),
            pltpu.SemaphoreType.DMA((2,)),
        ],
    )
    return pl.pallas_call(
        _expert_kernel,
        grid_spec=grid_spec,
        out_shape=jax.ShapeDtypeStruct((P_ROWS, LANE_TILES, LANES), jnp.float32),
        compiler_params=pltpu.CompilerParams(
            dimension_semantics=("arbitrary",), vmem_limit_bytes=VMEM_LIMIT),
        name="experts_ln",
    )(step_block, step_group, tile_ea, tile_eb, tile_rows, xs, wts, wgu, wd, g2, b2)


SC_CORES = 2
SC_SUBCORES = 16
SC_WORKERS = SC_CORES * SC_SUBCORES
COMBINE_CHUNK = 32


def _sc_gather_rows(table, idx, chunk, out_row_shape=None):
    n_out, = idx.shape
    row_shape = table.shape[1:]
    out_row_shape = row_shape if out_row_shape is None else out_row_shape
    per_w = n_out // SC_WORKERS
    n_chunks = per_w // chunk
    assert per_w * SC_WORKERS == n_out and n_chunks * chunk == per_w
    assert n_chunks % 2 == 0 and chunk % 8 == 0
    mesh = plsc.VectorSubcoreMesh(core_axis_name="c", subcore_axis_name="s")

    @functools.partial(
        pl.kernel, mesh=mesh,
        out_type=jax.ShapeDtypeStruct((n_out,) + out_row_shape, table.dtype),
        scratch_types=[
            pltpu.VMEM((per_w,), jnp.int32),
            pltpu.VMEM((2, chunk) + row_shape, table.dtype),
            pltpu.SemaphoreType.DMA((2,)),
            pltpu.SemaphoreType.DMA((2,)),
        ],
    )
    def gather_kernel(table_hbm, idx_hbm, out_hbm, idx_v, buf, gsem, wsem):
        wid = lax.axis_index("s") * SC_CORES + lax.axis_index("c")
        base = wid * per_w
        pltpu.sync_copy(idx_hbm.at[pl.ds(base, per_w)], idx_v)

        def gather(j, slot):
            rows = idx_v.at[pl.ds(j * chunk, chunk)]
            return pltpu.make_async_copy(table_hbm.at[rows], buf.at[slot], gsem.at[slot])

        def write(j, slot):
            dst = out_hbm.at[pl.ds(base + j * chunk, chunk)]
            src = buf.at[slot].reshape((chunk,) + out_row_shape)
            return pltpu.make_async_copy(src, dst, wsem.at[slot])

        gather(0, 0).start()

        @pl.loop(0, n_chunks, step=2)
        def _(j0):
            for slot in range(2):
                j = j0 + slot
                gather(j, slot).wait()

                @pl.when(j >= 1)
                def _():
                    write(j - 1, 1 - slot).wait()

                @pl.when(j + 1 < n_chunks)
                def _():
                    gather(j + 1, 1 - slot).start()

                write(j, slot).start()

        write(n_chunks - 1, (n_chunks - 1) % 2).wait()

    return gather_kernel(table, idx)


SC_LANES = 16


def _sc_dispatch(rows, meta, cls, rank, class_start):
    n_rows = rows.shape[0]
    chunk = COMBINE_CHUNK
    per_w = n_rows // SC_WORKERS
    n_chunks = per_w // chunk
    assert per_w * SC_WORKERS == n_rows and n_chunks * chunk == per_w
    assert n_chunks % 2 == 0 and chunk % SC_LANES == 0
    mesh = plsc.VectorSubcoreMesh(core_axis_name="c", subcore_axis_name="s")

    @functools.partial(
        pl.kernel, mesh=mesh,
        out_type=[
            jax.ShapeDtypeStruct((P_ROWS,) + rows.shape[1:], rows.dtype),
            jax.ShapeDtypeStruct((P_ROWS,) + meta.shape[1:], meta.dtype),
            jax.ShapeDtypeStruct((n_rows,), jnp.int32),
        ],
        scratch_types=[
            pltpu.VMEM((per_w,), jnp.int32),
            pltpu.VMEM((per_w,), jnp.int32),
            pltpu.VMEM((LANES,), jnp.int32),
            pltpu.VMEM((per_w,), jnp.int32),
            pltpu.VMEM((n_chunks, chunk), jnp.int32),
            pltpu.VMEM((2, chunk) + rows.shape[1:], rows.dtype),
            pltpu.VMEM((2, chunk) + meta.shape[1:], meta.dtype),
            pltpu.SemaphoreType.DMA((2,)),
            pltpu.SemaphoreType.DMA((2,)),
            pltpu.SemaphoreType.DMA((2,)),
            pltpu.SemaphoreType.DMA((2,)),
        ],
        compiler_params=pltpu.CompilerParams(needs_layout_passes=False),
    )
    def dispatch_kernel(rows_hbm, meta_hbm, cls_hbm, rank_hbm, start_hbm,
                        xs_hbm, ws_hbm, pos_hbm,
                        cls_v, rank_v, start_v, pos_flat, pos_v, rbuf, mbuf,
                        rsem, msem, xsem, wsem):
        wid = lax.axis_index("s") * SC_CORES + lax.axis_index("c")
        base = wid * per_w
        pltpu.sync_copy(cls_hbm.at[pl.ds(base, per_w)], cls_v)
        pltpu.sync_copy(rank_hbm.at[pl.ds(base, per_w)], rank_v)
        pltpu.sync_copy(start_hbm, start_v)

        @pl.loop(0, n_chunks)
        def _(j):
            for k in range(chunk // SC_LANES):
                off = j * chunk + k * SC_LANES
                c = cls_v[pl.ds(off, SC_LANES)]
                pos = plsc.load_gather(start_v, [c]) + rank_v[pl.ds(off, SC_LANES)]
                pos_flat[pl.ds(off, SC_LANES)] = pos
                pos_v[j, pl.ds(k * SC_LANES, SC_LANES)] = pos

        pltpu.sync_copy(pos_flat, pos_hbm.at[pl.ds(base, per_w)])

        def read_rows(j, slot):
            src = rows_hbm.at[pl.ds(base + j * chunk, chunk)]
            return pltpu.make_async_copy(src, rbuf.at[slot], rsem.at[slot])

        def read_meta(j, slot):
            src = meta_hbm.at[pl.ds(base + j * chunk, chunk)]
            return pltpu.make_async_copy(src, mbuf.at[slot], msem.at[slot])

        def put_rows(j, slot):
            return pltpu.make_async_copy(rbuf.at[slot], xs_hbm.at[pos_v.at[j]], xsem.at[slot])

        def put_meta(j, slot):
            return pltpu.make_async_copy(mbuf.at[slot], ws_hbm.at[pos_v.at[j]], wsem.at[slot])

        read_rows(0, 0).start()
        read_meta(0, 0).start()

        @pl.loop(0, n_chunks, step=2)
        def _(j0):
            for slot in range(2):
                j = j0 + slot
                read_rows(j, slot).wait()
                read_meta(j, slot).wait()

                @pl.when(j >= 1)
                def _():
                    put_rows(j - 1, 1 - slot).wait()
                    put_meta(j - 1, 1 - slot).wait()

                @pl.when(j + 1 < n_chunks)
                def _():
                    read_rows(j + 1, 1 - slot).start()
                    read_meta(j + 1, 1 - slot).start()

                put_rows(j, slot).start()
                put_meta(j, slot).start()

        put_rows(n_chunks - 1, (n_chunks - 1) % 2).wait()
        put_meta(n_chunks - 1, (n_chunks - 1) % 2).wait()

    return dispatch_kernel(rows, meta, cls, rank, class_start)


def _pair_tables():
    pair_of = np.zeros((EXPERTS_PER_GROUP, EXPERTS_PER_GROUP), np.int32)
    lo = np.zeros((N_PAIRS,), np.int32)
    hi = np.zeros((N_PAIRS,), np.int32)
    p = 0
    for a in range(EXPERTS_PER_GROUP):
        for b in range(a + 1, EXPERTS_PER_GROUP):
            pair_of[a, b] = pair_of[b, a] = p
            lo[p], hi[p] = a, b
            p += 1
    return pair_of, lo, hi


def _route_kernel(lt_ref, info_ref, meta_ref, counts_ref, run_ref):
    step = pl.program_id(0)

    @pl.when(step == 0)
    def _():
        run_ref[...] = jnp.zeros_like(run_ref)

    f32 = jnp.float32
    sub8 = lax.broadcasted_iota(jnp.int32, (8, LANES), 0).astype(f32)
    row_id = lax.broadcasted_iota(jnp.int32, (LANES, LANES), 0)
    col_id = lax.broadcasted_iota(jnp.int32, (LANES, LANES), 1)
    prefix_mat = (row_id <= col_id).astype(jnp.bfloat16)
    ones_mat = jnp.ones((LANES, LANES), jnp.bfloat16)
    neg_inf = f32(-jnp.inf)

    def first_index_of_max(v):
        m = jnp.max(v, axis=0, keepdims=True)
        idx = jnp.min(jnp.where(v == m, sub8, f32(8)), axis=0, keepdims=True)
        return m, idx

    def lane_tile(k, run):
        lanes = pl.ds(pl.multiple_of(k * LANES, LANES), LANES)
        g = jnp.where(sub8 < N_GROUPS, lt_ref[0:GROUP_ROWS, lanes], neg_inf)
        g_max, g_idx = first_index_of_max(g)
        g_top_p = 1.0 / jnp.sum(jnp.exp(g - g_max), axis=0, keepdims=True)

        e_sel = lt_ref[GROUP_ROWS:GROUP_ROWS + EXPERTS_PER_GROUP, lanes]
        for grp in range(1, N_GROUPS):
            lo = GROUP_ROWS + grp * EXPERTS_PER_GROUP
            e_sel = jnp.where(g_idx == grp, lt_ref[lo:lo + EXPERTS_PER_GROUP, lanes], e_sel)
        m1, i1 = first_index_of_max(e_sel)
        rest = jnp.where(sub8 == i1, neg_inf, e_sel)
        m2 = jnp.max(rest, axis=0, keepdims=True)
        i2 = jnp.min(jnp.where((rest == m2) & (sub8 != i1), sub8, f32(8)), axis=0, keepdims=True)

        e = jnp.exp(m2 - m1)
        w1 = g_top_p / (1.0 + e)
        w2 = g_top_p * e / (1.0 + e)
        first_is_lo = i1 < i2
        w_lo = jnp.where(first_is_lo, w1, w2)
        w_hi = jnp.where(first_is_lo, w2, w1)
        lo_e = jnp.minimum(i1, i2)
        hi_e = jnp.maximum(i1, i2)
        pair = lo_e * (2 * EXPERTS_PER_GROUP - 1 - lo_e) * 0.5 + hi_e - lo_e - 1.0
        cls = (g_idx * N_PAIRS + pair).astype(jnp.int32)

        onehot = (row_id == cls).astype(jnp.bfloat16)
        csum = jnp.dot(onehot, prefix_mat, preferred_element_type=f32)
        rank = jnp.sum(onehot.astype(f32) * (csum + run), axis=0, keepdims=True) - 1.0
        run = run + jnp.dot(onehot, ones_mat, preferred_element_type=f32)

        info_ref[:, lanes] = jnp.where(sub8 == 0, cls, jnp.where(sub8 == 1, rank.astype(jnp.int32), 0))
        meta8 = jnp.where(sub8 == 0, w_lo, jnp.where(sub8 == 1, w_hi, 0.0))
        meta_t = jnp.concatenate([meta8, jnp.zeros((LANES - 8, LANES), f32)], axis=0)
        meta_ref[lanes, :] = meta_t.T
        return run

    run = lax.fori_loop(0, ROUTE_BLOCK // LANES, lane_tile, run_ref[...], unroll=ROUTE_UNROLL)
    run_ref[...] = run
    counts_ref[...] = run


def _route_call(lt):
    return pl.pallas_call(
        _route_kernel,
        grid=(N_TOKENS // ROUTE_BLOCK,),
        in_specs=[pl.BlockSpec((ROUTER_ROWS, ROUTE_BLOCK), lambda i: (0, i))],
        out_specs=[
            pl.BlockSpec((8, ROUTE_BLOCK), lambda i: (0, i)),
            pl.BlockSpec((ROUTE_BLOCK, LANES), lambda i: (i, 0)),
            pl.BlockSpec((LANES, LANES), lambda i: (0, 0)),
        ],
        out_shape=[
            jax.ShapeDtypeStruct((8, N_TOKENS), jnp.int32),
            jax.ShapeDtypeStruct((N_TOKENS, LANES), jnp.float32),
            jax.ShapeDtypeStruct((LANES, LANES), jnp.float32),
        ],
        scratch_shapes=[pltpu.VMEM((LANES, LANES), jnp.float32)],
        compiler_params=pltpu.CompilerParams(dimension_semantics=("arbitrary",)),
        name="route_rank",
    )(lt)


def _dispatch_plan(counts):
    _, pair_lo, pair_hi = _pair_tables()
    tiles_c = (counts + TM - 1) // TM
    tiles_g = tiles_c.reshape(N_GROUPS, N_PAIRS).sum(axis=1)
    tiles_g_pad = (tiles_g + SUB - 1) // SUB * SUB
    g_start = jnp.cumsum(tiles_g_pad) - tiles_g_pad
    tc = tiles_c.reshape(N_GROUPS, N_PAIRS)
    c_start = (g_start[:, None] + jnp.cumsum(tc, axis=1) - tc).reshape(N_CLASSES)
    class_start = jnp.zeros((LANES,), jnp.int32).at[:N_CLASSES].set(c_start * TM)

    tile_ids = jnp.arange(N_TILES, dtype=jnp.int32)
    c_end = c_start + tiles_c
    owner = jnp.sum((tile_ids[:, None] >= c_end[None, :]).astype(jnp.int32), axis=1)
    owner = jnp.minimum(owner, N_CLASSES - 1)
    in_class = (tile_ids >= c_start[owner]) & (tile_ids < c_end[owner])
    rows_left = counts[owner] - (tile_ids - c_start[owner]) * TM
    tile_rows = jnp.where(in_class, jnp.minimum(rows_left, TM), 0).astype(jnp.int32)
    pair = owner % N_PAIRS
    tile_ea = jnp.asarray(pair_lo)[pair]
    tile_eb = jnp.asarray(pair_hi)[pair]
    g_end = g_start + tiles_g_pad
    step_first = jnp.arange(N_STEPS3, dtype=jnp.int32) * SUB
    step_group = jnp.sum((step_first[:, None] >= g_end[None, :]).astype(jnp.int32), axis=1)
    step_group = jnp.minimum(step_group, N_GROUPS - 1)
    last_step = jnp.maximum(g_end[-1] // SUB - 1, 0)
    step_block = jnp.minimum(jnp.arange(N_STEPS3, dtype=jnp.int32), last_step)
    return (class_start, tile_ea, tile_eb, tile_rows, step_group.astype(jnp.int32),
            step_block.astype(jnp.int32))


def kernel(x, w_in, rg_conv_w, rg_conv_b, rg_gate_a_w, rg_gate_a_b, rg_gate_x_w, rg_gate_x_b,
           rg_lambda, sc_conv_w, w_out, ln1_g, ln1_b, router_group_w, router_group_b,
           router_expert_w, router_expert_b, exp_w_gate, exp_w_up, exp_w_down, ln2_g, ln2_b):
    bf16 = jnp.bfloat16
    row = lambda v: v.reshape(1, -1)
    pad_g = GROUP_ROWS - N_GROUPS
    pad_e = ROUTER_LANES - ROUTER_ROWS
    wr = jnp.concatenate([router_group_w, jnp.zeros((D_MODEL, pad_g), jnp.float32),
                          router_expert_w, jnp.zeros((D_MODEL, pad_e), jnp.float32)], axis=1)
    br = jnp.concatenate([router_group_b, jnp.zeros((pad_g,), jnp.float32),
                          router_expert_b, jnp.zeros((pad_e,), jnp.float32)]).reshape(1, -1)
    wr_hi = wr.astype(bf16)
    wr_lo = (wr - wr_hi.astype(jnp.float32)).astype(bf16)

    h1, lt, wgu, wdb = _mixer_call(
        x, w_in, rg_conv_w, row(rg_conv_b),
        rg_gate_a_w, row(rg_gate_a_b),
        rg_gate_x_w, row(rg_gate_x_b),
        row(rg_lambda), sc_conv_w, w_out, row(ln1_g), row(ln1_b),
        jnp.concatenate([wr_hi, wr_lo], axis=1), br, exp_w_gate, exp_w_up, exp_w_down)

    info, meta, counts = _route_call(lt)
    class_start, tile_ea, tile_eb, tile_rows, step_group, step_block = _dispatch_plan(
        counts[:N_CLASSES, 0].astype(jnp.int32))

    xs, wts, pos = _sc_dispatch(h1, meta, info[0], info[1], class_start)

    grp_shape = (N_GROUPS, EXPERTS_PER_GROUP)
    ys = _expert_call(
        step_block, step_group, tile_ea, tile_eb, tile_rows, xs, wts,
        wgu.reshape(grp_shape + (D_MODEL, 2 * D_EXPERT)),
        wdb.reshape(grp_shape + (D_EXPERT, D_MODEL)),
        row(ln2_g), row(ln2_b))

    out_rows = pos.reshape(SEQ, BATCH).T.reshape(-1)
    out = _sc_gather_rows(ys, out_rows, COMBINE_CHUNK, (D_MODEL,))
    return out.reshape(BATCH, SEQ, D_MODEL)
```
